```python
import jax
import jax.numpy as jnp
from jax import lax
import numpy as np

D_MODEL = 1024
BATCH = 8
SEQ = 2048
DEPTH = 1

RET_HEADS = 4
RET_DK = 64
RET_DV = 128
RET_CHUNK = 128
NSA_HEADS = 8
NSA_GROUPS = 2
NSA_HPG = NSA_HEADS // NSA_GROUPS
NSA_DH = 64
CMP_LEN = 32
CMP_STRIDE = 16
CMP_HIDDEN = 128
SLC_LEN = 64
SLC_TOPK = 16
WIN = 512
NSA_QB = 64
D_FF = ((8 * D_MODEL + 3 * 256 - 1) // (3 * 256)) * 256
ROPE_THETA = 10000.0
EPS = 1e-6
NEG = -1e30
FORCE = 1e6
IN_SIZES = (RET_HEADS * RET_DK, RET_HEADS * RET_DK, RET_HEADS * RET_DV, RET_HEADS * RET_DV,
            NSA_HEADS * NSA_DH,
            NSA_GROUPS * NSA_DH, NSA_GROUPS * NSA_DH,
            NSA_GROUPS * NSA_DH, NSA_GROUPS * NSA_DH,
            NSA_GROUPS * NSA_DH, NSA_GROUPS * NSA_DH,
            3 * NSA_HEADS)
IN_COLS = sum(IN_SIZES)
IN_SPLITS = tuple(int(v) for v in np.cumsum(IN_SIZES)[:-1])
MIX_WIDTH = RET_HEADS * RET_DV + NSA_HEADS * NSA_DH

kernel_name = 'hybrid_retention_nsa_block'


def rmsnorm(x, g):
    xf = x.astype(jnp.float32)
    y = xf * lax.rsqrt(jnp.mean(xf * xf, axis=-1, keepdims=True) + EPS)
    return (y * g.astype(jnp.float32)).astype(x.dtype)


def rope(x, pos):
    half = x.shape[-1] // 2
    inv = ROPE_THETA ** (-jnp.arange(half, dtype=jnp.float32) / half)
    ang = pos.astype(jnp.float32)[:, None] * inv
    cos, sin = jnp.cos(ang), jnp.sin(ang)
    x1 = x[..., :half].astype(jnp.float32)
    x2 = x[..., half:].astype(jnp.float32)
    return jnp.concatenate([x1 * cos - x2 * sin, x2 * cos + x1 * sin], axis=-1).astype(x.dtype)


def heads(t, n):
    b, s, _ = t.shape
    return t.reshape(b, s, n, -1).transpose(0, 2, 1, 3)


def retention(q, k, v):
    b, h, s, dk = q.shape
    dv = v.shape[-1]
    c = RET_CHUNK
    n = s // c
    log_g = jnp.log(1.0 - 2.0 ** (-5.0 - jnp.arange(h, dtype=jnp.float32)))
    idx = jnp.arange(c, dtype=jnp.float32)
    diff = idx[:, None] - idx[None, :]
    causal = diff >= 0
    decay = jnp.where(causal, jnp.exp(log_g[:, None, None] * jnp.where(causal, diff, 0.0)), 0.0)
    qc = q.reshape(b, h, n, c, dk).astype(jnp.float32)
    kc = k.reshape(b, h, n, c, dk).astype(jnp.float32)
    vc = v.reshape(b, h, n, c, dv).astype(jnp.float32)
    att = jnp.einsum('bhncd,bhnmd->bhncm', qc, kc) * decay[None, :, None]
    y_intra = jnp.einsum('bhncm,bhnme->bhnce', att, vc)
    zeta = jnp.exp(log_g[:, None] * (c - 1.0 - idx))
    kv = jnp.einsum('bhncd,bhnce->bhnde', kc * zeta[None, :, None, :, None], vc)
    g_chunk = jnp.exp(log_g * c)[None, :, None, None]

    def step(state, kv_n):
        return state * g_chunk + kv_n, state

    _, prev = lax.scan(step, jnp.zeros((b, h, dk, dv), jnp.float32), jnp.moveaxis(kv, 2, 0))
    prev = jnp.moveaxis(prev, 0, 2)
    xi = jnp.exp(log_g[:, None] * (idx + 1.0))
    y_cross = jnp.einsum('bhncd,bhnde->bhnce', qc * xi[None, :, None, :, None], prev)
    y = (y_intra + y_cross).reshape(b, h, s, dv)
    y = y * lax.rsqrt(jnp.mean(y * y, axis=-1, keepdims=True) + EPS)
    return y.astype(v.dtype)


def compress(t, pe, w1, w2):
    b, g, s, dh = t.shape
    r = CMP_LEN // CMP_STRIDE
    pieces = t.reshape(b, g, s // CMP_STRIDE, CMP_STRIDE, dh)
    n_c = s // CMP_STRIDE - r + 1
    blocks = jnp.concatenate([pieces[:, :, i:i + n_c] for i in range(r)], axis=3) + pe
    flat = blocks.reshape(b, g, n_c, CMP_LEN * dh)
    return jax.nn.silu(flat @ w1) @ w2


def nsa(q, kc, vc, ks, vs, kw, vw, gate_logits, pe_k, w1_k, w2_k, pe_v, w1_v, w2_v):
    b, s, _ = q.shape
    G, HPG, DH, QB = NSA_GROUPS, NSA_HPG, NSA_DH, NSA_QB
    pos = jnp.arange(s, dtype=jnp.int32)
    scale = DH ** -0.5
    q = rope(heads(q, NSA_HEADS), pos).reshape(b, G, HPG, s, DH)
    kc, vc = heads(kc, G), heads(vc, G)
    ks, vs = rope(heads(ks, G), pos), heads(vs, G)
    kw, vw = rope(heads(kw, G), pos), heads(vw, G)

    k_cmp = compress(kc, pe_k, w1_k, w2_k)
    v_cmp = compress(vc, pe_v, w1_v, w2_v)
    n_c = k_cmp.shape[2]
    cmp_end = jnp.arange(n_c, dtype=jnp.int32) * CMP_STRIDE + CMP_LEN - 1
    k_cmp = rope(k_cmp, cmp_end)
    s_cmp = jnp.einsum('bghsd,bgcd->bghsc', q, k_cmp).astype(jnp.float32) * scale
    cmask = cmp_end[None, :] <= pos[:, None]
    p_cmp = jnp.where(cmask, jax.nn.softmax(jnp.where(cmask, s_cmp, NEG), axis=-1), 0.0)
    o_cmp = jnp.einsum('bghsc,bgce->bghse', p_cmp.astype(v_cmp.dtype), v_cmp)

    nb = s // SLC_LEN
    n_sel = min(SLC_TOPK, nb)
    cs = jnp.arange(n_c) * CMP_STRIDE
    bs = jnp.arange(nb) * SLC_LEN
    overlap = jnp.maximum(jnp.minimum(cs[:, None] + CMP_LEN, bs[None] + SLC_LEN)
                          - jnp.maximum(cs[:, None], bs[None]), 0).astype(jnp.float32) / CMP_LEN
    imp = jnp.einsum('bghsc,cj->bgsj', p_cmp, overlap)
    blk = jnp.arange(nb)
    cur = pos // SLC_LEN
    forced = (blk[None] == 0) | (blk[None] == cur[:, None]) | (blk[None] == cur[:, None] - 1)
    bcausal = blk[None] * SLC_LEN <= pos[:, None]
    imp = jnp.where(bcausal, jnp.where(forced, FORCE, imp), NEG)
    top_val, top_idx = lax.top_k(imp, n_sel)
    top_ok = top_val > 0.5 * NEG

    ks_blk = ks.reshape(b, G, nb, SLC_LEN, DH)
    vs_blk = vs.reshape(b, G, nb, SLC_LEN, DH)
    kw_pad = jnp.pad(kw, ((0, 0), (0, 0), (WIN, 0), (0, 0)))
    vw_pad = jnp.pad(vw, ((0, 0), (0, 0), (WIN, 0), (0, 0)))
    nq = s // QB
    q_b = jnp.moveaxis(q.reshape(b, G, HPG, nq, QB, DH), 3, 0)
    idx_b = jnp.moveaxis(top_idx.reshape(b, G, nq, QB, n_sel), 2, 0)
    ok_b = jnp.moveaxis(top_ok.reshape(b, G, nq, QB, n_sel), 2, 0)
    gather = jax.vmap(jax.vmap(lambda kb, ix: kb[ix]))

    def query_block(args):
        i, qi, ix, ok = args
        t = i * QB + jnp.arange(QB, dtype=jnp.int32)
        k_sel = gather(ks_blk, ix)
        v_sel = gather(vs_blk, ix)
        s_sel = jnp.einsum('bghqd,bgqnld->bghqnl', qi, k_sel).astype(jnp.float32) * scale
        kpos = ix[..., None] * SLC_LEN + jnp.arange(SLC_LEN, dtype=jnp.int32)
        m_sel = (ok[..., None] & (kpos <= t[:, None, None]))[:, :, None]
        s_sel = jnp.where(m_sel, s_sel, NEG).reshape(b, G, HPG, QB, n_sel * SLC_LEN)
        p_sel = jax.nn.softmax(s_sel, axis=-1).reshape(b, G, HPG, QB, n_sel, SLC_LEN)
        o_sel = jnp.einsum('bghqnl,bgqnle->bghqe', p_sel.astype(v_sel.dtype), v_sel)
        start = i * QB
        k_win = lax.dynamic_slice_in_dim(kw_pad, start, WIN + QB, axis=2)
        v_win = lax.dynamic_slice_in_dim(vw_pad, start, WIN + QB, axis=2)
        wpos = start - WIN + jnp.arange(WIN + QB, dtype=jnp.int32)
        m_win = (wpos[None] <= t[:, None]) & (wpos[None] > t[:, None] - WIN) & (wpos[None] >= 0)
        s_win = jnp.einsum('bghqd,bgkd->bghqk', qi, k_win).astype(jnp.float32) * scale
        p_win = jax.nn.softmax(jnp.where(m_win, s_win, NEG), axis=-1)
        o_win = jnp.einsum('bghqk,bgke->bghqe', p_win.astype(v_win.dtype), v_win)
        return o_sel, o_win

    o_sel, o_win = lax.map(query_block, (jnp.arange(nq, dtype=jnp.int32), q_b, idx_b, ok_b))
    o_sel = jnp.moveaxis(o_sel, 0, 3).reshape(b, G, HPG, s, DH)
    o_win = jnp.moveaxis(o_win, 0, 3).reshape(b, G, HPG, s, DH)

    g = jax.nn.sigmoid(gate_logits.astype(jnp.float32)).astype(q.dtype)
    g = g.reshape(b, s, 3, NSA_HEADS).transpose(2, 0, 3, 1).reshape(3, b, G, HPG, s)[..., None]
    o = g[0] * o_cmp + g[1] * o_sel + g[2] * o_win
    return o.reshape(b, NSA_HEADS, s, DH).transpose(0, 2, 1, 3).reshape(b, s, NSA_HEADS * DH)


def setup_inputs(seed: int = 0) -> dict:
    key = jax.random.key(seed)
    k = jax.random.split(key, 20)

    def nrm(kk, shape, s):
        return jax.random.normal(kk, shape, jnp.float32) * s

    L, D, DH = DEPTH, D_MODEL, NSA_DH
    return {
        'x': nrm(k[0], (BATCH, SEQ, D), 1.0),
        'c': nrm(k[1], (BATCH, D), 1.0),
        'ln_mix_g': 1.0 + nrm(k[2], (L, D), 0.02),
        'ln_ffn_g': 1.0 + nrm(k[3], (L, D), 0.02),
        'w_ada': nrm(k[4], (L, D, 6 * D), 0.5 * D ** -0.5),
        'b_ada': nrm(k[5], (L, 6 * D), 0.01),
        'w_in': nrm(k[6], (L, D, IN_COLS), D ** -0.5),
        'cmp_pe_k': nrm(k[7], (L, CMP_LEN, DH), 0.02),
        'cmp_w1_k': nrm(k[8], (L, CMP_LEN * DH, CMP_HIDDEN), (CMP_LEN * DH) ** -0.5),
        'cmp_w2_k': nrm(k[9], (L, CMP_HIDDEN, DH), CMP_HIDDEN ** -0.5),
        'cmp_pe_v': nrm(k[10], (L, CMP_LEN, DH), 0.02),
        'cmp_w1_v': nrm(k[11], (L, CMP_LEN * DH, CMP_HIDDEN), (CMP_LEN * DH) ** -0.5),
        'cmp_w2_v': nrm(k[12], (L, CMP_HIDDEN, DH), CMP_HIDDEN ** -0.5),
        'w_out': nrm(k[13], (L, MIX_WIDTH, D), MIX_WIDTH ** -0.5),
        'w_ff_gate': nrm(k[14], (L, D, D_FF), D ** -0.5),
        'w_ff_up': nrm(k[15], (L, D, D_FF), D ** -0.5),
        'w_ff_down': nrm(k[16], (L, D_FF, D), D_FF ** -0.5),
        'ln_final_g': 1.0 + nrm(k[17], (D,), 0.02),
    }


def reference(x, c, ln_mix_g, ln_ffn_g, w_ada, b_ada, w_in, cmp_pe_k, cmp_w1_k, cmp_w2_k,
              cmp_pe_v, cmp_w1_v, cmp_w2_v, w_out, w_ff_gate, w_ff_up, w_ff_down, ln_final_g):
    b, s, _ = x.shape
    pos = jnp.arange(s, dtype=jnp.int32)
    for l in range(DEPTH):
        mod = jax.nn.silu(c) @ w_ada[l] + b_ada[l]
        sh1, sc1, ga1, sh2, sc2, ga2 = jnp.split(mod[:, None, :], 6, axis=-1)
        h = rmsnorm(x, ln_mix_g[l]) * (1.0 + sc1) + sh1
        (rq, rk, rv, rg, nq_, nkc, nvc, nks, nvs, nkw, nvw, ngate) = jnp.split(h @ w_in[l], IN_SPLITS, axis=-1)
        rq = rope(heads(rq, RET_HEADS), pos)
        rk = rope(heads(rk, RET_HEADS), pos) * (RET_DK ** -0.5)
        y_ret = retention(rq, rk, heads(rv, RET_HEADS))
        y_ret = y_ret.transpose(0, 2, 1, 3).reshape(b, s, RET_HEADS * RET_DV) * jax.nn.silu(rg)
        y_nsa = nsa(nq_, nkc, nvc, nks, nvs, nkw, nvw, ngate,
                    cmp_pe_k[l], cmp_w1_k[l], cmp_w2_k[l], cmp_pe_v[l], cmp_w1_v[l], cmp_w2_v[l])
        mix = jnp.concatenate([y_ret, y_nsa], axis=-1) @ w_out[l]
        x = x + ga1 * mix
        h2 = rmsnorm(x, ln_ffn_g[l]) * (1.0 + sc2) + sh2
        ffn = (jax.nn.silu(h2 @ w_ff_gate[l]) * (h2 @ w_ff_up[l])) @ w_ff_down[l]
        x = x + ga2 * ffn
    return rmsnorm(x, ln_final_g)
```

```python
import functools

import numpy as np
import jax
import jax.numpy as jnp
from jax import lax
from jax.experimental import pallas as pl
from jax.experimental.pallas import tpu as pltpu

F32 = jnp.float32
BF16 = jnp.bfloat16

D_MODEL = 1024
RET_HEADS = 4
RET_DK = 64
RET_DV = 128
RET_CHUNK = 128
NSA_HEADS = 8
NSA_GROUPS = 2
NSA_HPG = NSA_HEADS // NSA_GROUPS
NSA_DH = 64
CMP_LEN = 32
CMP_STRIDE = 16
CMP_HIDDEN = 128
SLC_LEN = 64
SLC_TOPK = 16
WIN = 512
D_FF = ((8 * D_MODEL + 3 * 256 - 1) // (3 * 256)) * 256
ROPE_THETA = 10000.0
EPS = 1e-6
NEG = -1e30
FORCE = 1e6

LANES = 128
HALF = NSA_DH // 2

TM_IN = 512
TQ = 128
KC = 512
TM_FF = 512
TF = 256
VMEM_LIMIT = 48 * 1024 * 1024

_R_RQ, _R_RK, _R_RV, _R_RG, _R_NQ = 0, 256, 512, 1024, 1536
_R_KC, _R_VC, _R_KS, _R_VS, _R_KW, _R_VW, _R_GATE = 2048, 2176, 2304, 2432, 2560, 2688, 2816
_C_RQ, _C_RK, _C_RV, _C_RG, _C_NQ, _C_KVC, _C_KSV, _C_KWV, _C_GATE = (
    0, 256, 512, 1024, 1536, 2048, 2304, 2560, 2816)
IN_COLS_K = 3072


def _sigmoid(x):
    return 1.0 / (1.0 + jnp.exp(-x))


def _nt(a, b):
    return lax.dot_general(a, b, (((1,), (1,)), ((), ())), preferred_element_type=F32)


def _nn(a, b):
    return jnp.dot(a, b, preferred_element_type=F32)


def _rope_tile(a, c, s1, s2):
    return a * c + pltpu.roll(a, HALF, 1) * s1 + pltpu.roll(a, LANES - HALF, 1) * s2


def _rope_tables(pos, rotary_lanes):
    pos = np.asarray(pos, np.float64)
    lane = np.arange(LANES)
    within = lane % NSA_DH
    freq = ROPE_THETA ** (-(within % HALF).astype(np.float64) / HALF)
    ang = pos[:, None] * freq[None, :]
    cos, sin = np.cos(ang), np.sin(ang)
    first = (within < HALF)[None, :]
    rot = np.asarray(rotary_lanes, bool)[None, :]
    c = np.where(rot, cos, 1.0)
    s1 = np.where(rot & ~first, sin, 0.0)
    s2 = np.where(rot & first, -sin, 0.0)
    return (jnp.asarray(c, F32), jnp.asarray(s1, F32), jnp.asarray(s2, F32))


def _retention_tables():
    h = jnp.arange(RET_HEADS, dtype=F32)
    log_g = jnp.log(1.0 - 2.0 ** (-5.0 - h))
    c = RET_CHUNK
    idx = jnp.arange(c, dtype=F32)
    diff = idx[:, None] - idx[None, :]
    causal = diff >= 0
    decay = jnp.where(causal, jnp.exp(log_g[:, None, None] * jnp.where(causal, diff, 0.0)), 0.0)
    zeta = jnp.exp(log_g[:, None] * (c - 1.0 - idx))
    xi = jnp.exp(log_g[:, None] * (idx + 1.0))
    g_chunk = jnp.exp(log_g * c)
    zeta_b = jnp.broadcast_to(zeta[:, :, None], (RET_HEADS, c, LANES))
    xi_b = jnp.broadcast_to(xi[:, :, None], (RET_HEADS, c, LANES))
    g_b = jnp.broadcast_to(g_chunk[:, None, None], (RET_HEADS, 1, LANES))
    return decay, zeta_b, xi_b, g_b


def _overlap_t(seq):
    n_c = seq // CMP_STRIDE - CMP_LEN // CMP_STRIDE + 1
    nb = seq // SLC_LEN
    cs = np.arange(n_c) * CMP_STRIDE
    bs = np.arange(nb) * SLC_LEN
    ov = np.maximum(np.minimum(cs[:, None] + CMP_LEN, bs[None] + SLC_LEN)
                    - np.maximum(cs[:, None], bs[None]), 0).astype(np.float64) / CMP_LEN
    ncp = seq // CMP_STRIDE
    ovp = np.zeros((ncp, nb))
    ovp[:n_c] = ov
    return jnp.asarray(ovp.T, BF16)


def _ada_kernel(c_ref, w_ref, b_ref, o_ref):
    c = c_ref[...]
    o_ref[...] = _nn(c * _sigmoid(c), w_ref[...]) + b_ref[...]


def _ada(c, w, b):
    bsz, d = c.shape
    n = w.shape[1]
    tn = 1024
    return pl.pallas_call(
        _ada_kernel,
        grid=(n // tn,),
        in_specs=[pl.BlockSpec((bsz, d), lambda j: (0, 0)),
                  pl.BlockSpec((d, tn), lambda j: (0, j)),
                  pl.BlockSpec((1, tn), lambda j: (0, j))],
        out_specs=pl.BlockSpec((bsz, tn), lambda j: (0, j)),
        out_shape=jax.ShapeDtypeStruct((bsz, n), F32),
        compiler_params=pltpu.CompilerParams(vmem_limit_bytes=VMEM_LIMIT),
        name="ada",
    )(c, w, b.reshape(1, n))


def _inproj_kernel(x_ref, g_ref, mod_ref, w_ref, cf_ref, s1f_ref, s2f_ref, ch_ref, s1h_ref, s2h_ref,
                   rq_ref, rk_ref, rv_ref, rg_ref, nq_ref, kc_ref, vc_ref, ksv_ref, kwv_ref, gate_ref):
    x = x_ref[0]
    y = x * lax.rsqrt(jnp.mean(x * x, axis=-1, keepdims=True) + EPS) * g_ref[...]
    h = y * (1.0 + mod_ref[0, 1:2, :]) + mod_ref[0, 0:1, :]
    hb = h.astype(BF16)

    def proj(c0, n):
        return _nn(hb, w_ref[:, c0:c0 + n])

    full = (cf_ref[...], s1f_ref[...], s2f_ref[...])
    half = (ch_ref[...], s1h_ref[...], s2h_ref[...])

    def roped(c0, n, tabs, scale, out_ref):
        for t in range(n // LANES):
            a = proj(c0 + t * LANES, LANES)
            r = _rope_tile(a, *tabs)
            if scale != 1.0:
                r = r * scale
            out_ref[0, :, t * LANES:(t + 1) * LANES] = r.astype(out_ref.dtype)

    roped(_C_RQ, 256, full, 1.0, rq_ref)
    roped(_C_RK, 256, full, RET_DK ** -0.5, rk_ref)
    rv_ref[0] = proj(_C_RV, 512).astype(BF16)
    rg = proj(_C_RG, 512)
    rg_ref[0] = (rg * _sigmoid(rg)).astype(BF16)
    roped(_C_NQ, 512, full, NSA_DH ** -0.5, nq_ref)
    kc_ref[0] = proj(_C_KVC, LANES)
    vc_ref[0] = proj(_C_KVC + LANES, LANES)
    roped(_C_KSV, 256, half, 1.0, ksv_ref)
    roped(_C_KWV, 256, half, 1.0, kwv_ref)
    gate_ref[0] = _sigmoid(proj(_C_GATE, 256))


def _permute_w_in(w_in):
    d = w_in.shape[0]
    g64 = NSA_DH
    cols = [w_in[:, _R_RQ:_R_NQ + 512], w_in[:, _R_KC:_R_KS]]
    for k0, v0 in ((_R_KS, _R_VS), (_R_KW, _R_VW)):
        for g in range(NSA_GROUPS):
            cols.append(w_in[:, k0 + g * g64:k0 + (g + 1) * g64])
            cols.append(w_in[:, v0 + g * g64:v0 + (g + 1) * g64])
    for g in range(NSA_GROUPS):
        for r in range(3):
            c0 = _R_GATE + r * NSA_HEADS + g * NSA_HPG
            cols.append(w_in[:, c0:c0 + NSA_HPG])
        cols.append(jnp.zeros((d, LANES - 3 * NSA_HPG), w_in.dtype))
    return jnp.concatenate(cols, axis=1).astype(BF16)


def _in_proj(x, ln_g, mod, w_perm, tabs_full, tabs_half):
    b, s, d = x.shape
    tm = TM_IN
    grid = (b, s // tm)
    tab_spec = pl.BlockSpec((tm, LANES), lambda bi, j: (j, 0))

    def out(n, dtype):
        return (jax.ShapeDtypeStruct((b, s, n), dtype), pl.BlockSpec((1, tm, n), lambda bi, j: (bi, j, 0)))

    outs = [out(256, BF16), out(256, BF16), out(512, BF16), out(512, BF16), out(512, BF16),
            out(LANES, F32), out(LANES, F32), out(256, BF16), out(256, BF16), out(256, F32)]
    return pl.pallas_call(
        _inproj_kernel,
        grid=grid,
        in_specs=[pl.BlockSpec((1, tm, d), lambda bi, j: (bi, j, 0)),
                  pl.BlockSpec((1, d), lambda bi, j: (0, 0)),
                  pl.BlockSpec((1, 6, d), lambda bi, j: (bi, 0, 0)),
                  pl.BlockSpec((d, IN_COLS_K), lambda bi, j: (0, 0)),
                  tab_spec, tab_spec, tab_spec, tab_spec, tab_spec, tab_spec],
        out_specs=[o[1] for o in outs],
        out_shape=[o[0] for o in outs],
        compiler_params=pltpu.CompilerParams(
            dimension_semantics=("parallel", "parallel"), vmem_limit_bytes=VMEM_LIMIT),
        name="in_proj",
    )(x, ln_g.reshape(1, d), mod, w_perm, *tabs_full, *tabs_half)


def _ret_kernel(q_ref, k_ref, v_ref, rg_ref, dec_ref, zeta_ref, xi_ref, gch_ref, o_ref, state_ref):
    c = RET_CHUNK
    n_chunks = q_ref.shape[1] // c
    parity = pl.program_id(1) % 2
    lane = lax.broadcasted_iota(jnp.int32, (c, LANES), 1)
    hmask = (lane // RET_DK) == parity
    state_ref[...] = jnp.zeros_like(state_ref)

    def body(n, carry):
        r0 = pl.multiple_of(n * c, c)
        q = jnp.where(hmask, q_ref[0, pl.ds(r0, c), :].astype(F32), 0.0)
        k = jnp.where(hmask, k_ref[0, pl.ds(r0, c), :].astype(F32), 0.0)
        v = v_ref[0, pl.ds(r0, c), :]
        att = _nt(q.astype(BF16), k.astype(BF16)) * dec_ref[0]
        y = _nn(att.astype(BF16), v)
        st = state_ref[...]
        y = y + _nn((q * xi_ref[0]).astype(BF16), st.astype(BF16))
        kz_t = (k * zeta_ref[0]).T.astype(BF16)
        state_ref[...] = st * gch_ref[0] + _nn(kz_t, v)
        yn = y * lax.rsqrt(jnp.mean(y * y, axis=-1, keepdims=True) + EPS)
        o_ref[0, pl.ds(r0, c), :] = (yn * rg_ref[0, pl.ds(r0, c), :].astype(F32)).astype(o_ref.dtype)
        return carry

    lax.fori_loop(0, n_chunks, body, 0)


def _retention(rq, rk, rv, rg_act, tables):
    b, s, _ = rq.shape
    decay, zeta_b, xi_b, g_b = tables
    c = RET_CHUNK
    pair = pl.BlockSpec((1, s, LANES), lambda bi, h: (bi, 0, h // 2))
    head = pl.BlockSpec((1, s, LANES), lambda bi, h: (bi, 0, h))
    return pl.pallas_call(
        _ret_kernel,
        grid=(b, RET_HEADS),
        in_specs=[pair, pair, head, head,
                  pl.BlockSpec((1, c, c), lambda bi, h: (h, 0, 0)),
                  pl.BlockSpec((1, c, LANES), lambda bi, h: (h, 0, 0)),
                  pl.BlockSpec((1, c, LANES), lambda bi, h: (h, 0, 0)),
                  pl.BlockSpec((1, 1, LANES), lambda bi, h: (h, 0, 0))],
        out_specs=head,
        out_shape=jax.ShapeDtypeStruct((b, s, RET_HEADS * RET_DV), BF16),
        scratch_shapes=[pltpu.VMEM((LANES, RET_DV), F32)],
        compiler_params=pltpu.CompilerParams(
            dimension_semantics=("parallel", "parallel"), vmem_limit_bytes=VMEM_LIMIT),
        name="retention",
    )(rq, rk, rv, rg_act, decay, zeta_b, xi_b, g_b)


def _cmp_kernel(kc_ref, vc_ref, w1_ref, pe_ref, w2_ref, c_ref, s1_ref, s2_ref, o_ref):
    n_piece = kc_ref.shape[1] // CMP_STRIDE
    hid_w = w1_ref.shape[3]
    out = jnp.zeros((n_piece, o_ref.shape[2]), F32)
    for t, src in enumerate((kc_ref, vc_ref)):
        a_lo = jnp.zeros((n_piece, hid_w), F32)
        a_hi = jnp.zeros((n_piece, hid_w), F32)
        for r in range(CMP_STRIDE):
            xr = src[0, pl.ds(r, n_piece, stride=CMP_STRIDE), :]
            lo, hi = r, CMP_STRIDE + r
            a_lo = a_lo + _nn((xr + pe_ref[t, lo:lo + 1, :]).astype(BF16), w1_ref[t, lo])
            a_hi = a_hi + _nn((xr + pe_ref[t, hi:hi + 1, :]).astype(BF16), w1_ref[t, hi])
        hid = a_lo + pltpu.roll(a_hi, n_piece - 1, 0)
        out = out + _nn((hid * _sigmoid(hid)).astype(BF16), w2_ref[t])
    for t in range(out.shape[1] // LANES):
        sl = slice(t * LANES, (t + 1) * LANES)
        o_ref[0, :, sl] = _rope_tile(out[:, sl], c_ref[...], s1_ref[...], s2_ref[...]).astype(o_ref.dtype)


def _compress_weights(w1_k, w2_k, w1_v, w2_v, pe_k, pe_v):
    dh, hid = NSA_DH, CMP_HIDDEN

    def first(w1):
        w = w1.reshape(CMP_LEN, dh, hid)
        z = jnp.zeros_like(w)
        return jnp.concatenate([jnp.concatenate([w, z], axis=2), jnp.concatenate([z, w], axis=2)], axis=1)

    def second(w2, off):
        z = jnp.zeros_like(w2)
        cols = [z, z, z, z]
        rows = []
        for g in range(NSA_GROUPS):
            c = list(cols)
            c[2 * g + off] = w2
            rows.append(jnp.concatenate(c, axis=1))
        return jnp.concatenate(rows, axis=0)

    w1 = jnp.stack([first(w1_k), first(w1_v)]).astype(BF16)
    w2 = jnp.stack([second(w2_k, 0), second(w2_v, 1)]).astype(BF16)
    pe = jnp.stack([jnp.concatenate([pe_k, pe_k], axis=1), jnp.concatenate([pe_v, pe_v], axis=1)])
    return w1, w2, pe


def _compress(kc, vc, w1, w2, pe, tabs_cmp):
    b, s, w = kc.shape
    n_piece = s // CMP_STRIDE
    const2 = lambda bi: (0, 0)
    src = pl.BlockSpec((1, s, w), lambda bi: (bi, 0, 0))
    return pl.pallas_call(
        _cmp_kernel,
        grid=(b,),
        in_specs=[src, src,
                  pl.BlockSpec(w1.shape, lambda bi: (0, 0, 0, 0)),
                  pl.BlockSpec(pe.shape, lambda bi: (0, 0, 0)),
                  pl.BlockSpec(w2.shape, lambda bi: (0, 0, 0)),
                  pl.BlockSpec((n_piece, LANES), const2),
                  pl.BlockSpec((n_piece, LANES), const2),
                  pl.BlockSpec((n_piece, LANES), const2)],
        out_specs=pl.BlockSpec((1, n_piece, 2 * w), lambda bi: (bi, 0, 0)),
        out_shape=jax.ShapeDtypeStruct((b, n_piece, 2 * w), BF16),
        compiler_params=pltpu.CompilerParams(
            dimension_semantics=("parallel",), vmem_limit_bytes=VMEM_LIMIT),
        name="compress",
    )(kc, vc, w1, pe, w2, *tabs_cmp)


def _nsa_kernel(q_ref, kvcmp_ref, ksv_ref, kwv_ref, gate_ref, ovt_ref, o_ref):
    tq = q_ref.shape[1]
    seq = ksv_ref.shape[1]
    nb = seq // SLC_LEN
    hpg = NSA_HPG
    t0 = pl.program_id(2) * tq

    lane = lax.broadcasted_iota(jnp.int32, (tq, LANES), 1)
    low = lane < NSA_DH

    qf = q_ref[0].astype(F32)
    heads = []
    for hh in range(hpg):
        tile = qf[:, (hh // 2) * LANES:(hh // 2 + 1) * LANES]
        if hh % 2 == 1:
            tile = pltpu.roll(tile, NSA_DH, 1)
        heads.append(jnp.where(low, tile, 0.0))
    qs = jnp.concatenate(heads, axis=0).astype(BF16)

    tpos = t0 + lax.broadcasted_iota(jnp.int32, (tq, LANES), 0)

    kvc = kvcmp_ref[0]
    s_c = _nt(qs, kvc)
    cmask = (lane * CMP_STRIDE + (CMP_LEN - 1)) <= tpos
    o_cmp = []
    psum = jnp.zeros((tq, LANES), F32)
    for hh in range(hpg):
        sh = jnp.where(cmask, s_c[hh * tq:(hh + 1) * tq], NEG)
        e = jnp.exp(sh - jnp.max(sh, axis=-1, keepdims=True))
        p = jnp.where(cmask, e / jnp.sum(e, axis=-1, keepdims=True), 0.0)
        psum = psum + p
        o_cmp.append(_nn(p.astype(BF16), kvc))

    p_hi = psum.astype(BF16)
    p_lo = (psum - p_hi.astype(F32)).astype(BF16)
    ovt = ovt_ref[...]
    imp = _nt(ovt, p_hi) + _nt(ovt, p_lo)
    blk = lax.broadcasted_iota(jnp.int32, (nb, tq), 0)
    tcol = t0 + lax.broadcasted_iota(jnp.int32, (nb, tq), 1)
    cur = tcol // SLC_LEN
    forced = (blk == 0) | (blk == cur) | (blk == cur - 1)
    bcausal = blk * SLC_LEN <= tcol
    imp = jnp.where(bcausal, jnp.where(forced, FORCE, imp), NEG)
    rank = jnp.zeros((nb, tq), F32)
    for j in range(nb):
        row = imp[j:j + 1, :]
        rank = rank + jnp.where(blk > j, jnp.where(row >= imp, 1.0, 0.0), jnp.where(row > imp, 1.0, 0.0))
    sel_t = jnp.where((rank < float(min(SLC_TOPK, nb))) & bcausal, 1.0, 0.0).astype(BF16)
    eye = (lax.broadcasted_iota(jnp.int32, (tq, tq), 0)
           == lax.broadcasted_iota(jnp.int32, (tq, tq), 1))
    sel = _nt(jnp.where(eye, 1.0, 0.0).astype(BF16), sel_t).astype(BF16)

    kc = KC
    n_kc = (t0 + tq + kc - 1) // kc
    jb = lax.broadcasted_iota(jnp.int32, (nb, kc), 0)
    kl = lax.broadcasted_iota(jnp.int32, (nb, kc), 1)
    kpos_l = lax.broadcasted_iota(jnp.int32, (tq, kc), 1)
    tq_k = t0 + lax.broadcasted_iota(jnp.int32, (tq, kc), 0)

    def sel_body(ci, carry):
        m, l, acc = carry
        k0 = pl.multiple_of(ci * kc, kc)
        kv = ksv_ref[0, pl.ds(k0, kc), :]
        s = _nt(qs, kv)
        expand = jnp.where(jb == (kl + k0) // SLC_LEN, 1.0, 0.0).astype(BF16)
        chosen = _nn(sel, expand)
        valid = (chosen > 0.5) & ((kpos_l + k0) <= tq_k)
        bias = jnp.where(valid, 0.0, NEG)
        s = s + jnp.concatenate([bias] * hpg, axis=0)
        m_new = jnp.maximum(m, jnp.max(s, axis=-1, keepdims=True))
        alpha = jnp.exp(m - m_new)
        p = jnp.exp(s - m_new)
        l = alpha * l + jnp.sum(p, axis=-1, keepdims=True)
        acc = alpha * acc + _nn(p.astype(BF16), kv)
        return m_new, l, acc

    m0 = jnp.full((hpg * tq, 1), NEG, F32)
    l0 = jnp.zeros((hpg * tq, 1), F32)
    a0 = jnp.zeros((hpg * tq, LANES), F32)
    _, l_s, acc_s = lax.fori_loop(0, n_kc, sel_body, (m0, l0, a0))
    o_sel = acc_s / l_s

    wlen = WIN + tq
    ws = pl.multiple_of(jnp.maximum(t0 - WIN, 0), tq)
    kvw = kwv_ref[0, pl.ds(ws, wlen), :]
    s_w = _nt(qs, kvw)
    wpos = ws + lax.broadcasted_iota(jnp.int32, (tq, wlen), 1)
    tq_w = t0 + lax.broadcasted_iota(jnp.int32, (tq, wlen), 0)
    wvalid = (wpos <= tq_w) & (wpos > tq_w - WIN)
    wbias = jnp.where(wvalid, 0.0, NEG)
    s_w = s_w + jnp.concatenate([wbias] * hpg, axis=0)
    e_w = jnp.exp(s_w - jnp.max(s_w, axis=-1, keepdims=True))
    o_win = _nn(e_w.astype(BF16), kvw) / jnp.sum(e_w, axis=-1, keepdims=True)

    gt = gate_ref[0]
    outs = []
    for hh in range(hpg):
        rows = slice(hh * tq, (hh + 1) * tq)
        g_c = gt[:, 0 * hpg + hh:0 * hpg + hh + 1]
        g_s = gt[:, 1 * hpg + hh:1 * hpg + hh + 1]
        g_w = gt[:, 2 * hpg + hh:2 * hpg + hh + 1]
        outs.append(g_c * o_cmp[hh] + g_s * o_sel[rows] + g_w * o_win[rows])
    for pr in range(hpg // 2):
        even = pltpu.roll(outs[2 * pr], NSA_DH, 1)
        o_ref[0, :, pr * LANES:(pr + 1) * LANES] = jnp.where(low, even, outs[2 * pr + 1]).astype(o_ref.dtype)


def _nsa_attention(nq, kvcmp, ksv, kwv, gate, ovt):
    b, s, _ = nq.shape
    tq = TQ
    n_cmp = kvcmp.shape[1]
    gw = NSA_HPG * NSA_DH
    return pl.pallas_call(
        _nsa_kernel,
        grid=(b, NSA_GROUPS, s // tq),
        in_specs=[pl.BlockSpec((1, tq, gw), lambda bi, g, i: (bi, i, g)),
                  pl.BlockSpec((1, n_cmp, LANES), lambda bi, g, i: (bi, 0, g)),
                  pl.BlockSpec((1, s, LANES), lambda bi, g, i: (bi, 0, g)),
                  pl.BlockSpec((1, s, LANES), lambda bi, g, i: (bi, 0, g)),
                  pl.BlockSpec((1, tq, LANES), lambda bi, g, i: (bi, i, g)),
                  pl.BlockSpec(ovt.shape, lambda bi, g, i: (0, 0))],
        out_specs=pl.BlockSpec((1, tq, gw), lambda bi, g, i: (bi, i, g)),
        out_shape=jax.ShapeDtypeStruct((b, s, NSA_HEADS * NSA_DH), BF16),
        compiler_params=pltpu.CompilerParams(
            dimension_semantics=("parallel", "parallel", "parallel"), vmem_limit_bytes=VMEM_LIMIT),
        name="nsa_attn",
    )(nq, kvcmp, ksv, kwv, gate, ovt)


def _ffn_kernel(x_ref, yr_ref, yn_ref, mod_ref, wo_ref, g2_ref, gf_ref, wg_ref, wu_ref, wd_ref,
                o_ref, x1_ref, h2_ref, acc_ref):
    j = pl.program_id(1)
    half_w = yr_ref.shape[1]

    @pl.when(j == 0)
    def _():
        mix = _nn(yr_ref[...], wo_ref[0:half_w, :]) + _nn(yn_ref[...], wo_ref[half_w:2 * half_w, :])
        x1 = x_ref[...] + mod_ref[0, 2:3, :] * mix
        x1_ref[...] = x1
        y = x1 * lax.rsqrt(jnp.mean(x1 * x1, axis=-1, keepdims=True) + EPS) * g2_ref[...]
        h2_ref[...] = (y * (1.0 + mod_ref[0, 4:5, :]) + mod_ref[0, 3:4, :]).astype(h2_ref.dtype)
        acc_ref[...] = jnp.zeros_like(acc_ref)

    h2 = h2_ref[...]
    gate = _nn(h2, wg_ref[...])
    up = _nn(h2, wu_ref[...])
    act = (gate * _sigmoid(gate) * up).astype(BF16)
    acc_ref[...] += _nn(act, wd_ref[...])

    @pl.when(j == pl.num_programs(1) - 1)
    def _():
        xo = x1_ref[...] + mod_ref[0, 5:6, :] * acc_ref[...]
        o_ref[...] = xo * lax.rsqrt(jnp.mean(xo * xo, axis=-1, keepdims=True) + EPS) * gf_ref[...]


def _out_ffn(x2d, y_ret, y_nsa, mod, w_out, g2, gf, wg, wu, wd, seq):
    n, d = x2d.shape
    tm, tf = TM_FF, TF
    d_ff = wg.shape[1]
    tiles_per_seq = seq // tm
    half_w = y_ret.shape[1]
    row = lambda i, j: (i, 0)
    const = lambda i, j: (0, 0)
    return pl.pallas_call(
        _ffn_kernel,
        grid=(n // tm, d_ff // tf),
        in_specs=[pl.BlockSpec((tm, d), row),
                  pl.BlockSpec((tm, half_w), row),
                  pl.BlockSpec((tm, half_w), row),
                  pl.BlockSpec((1, 6, d), lambda i, j: (i // tiles_per_seq, 0, 0)),
                  pl.BlockSpec(w_out.shape, const),
                  pl.BlockSpec((1, d), const),
                  pl.BlockSpec((1, d), const),
                  pl.BlockSpec((d, tf), lambda i, j: (0, j)),
                  pl.BlockSpec((d, tf), lambda i, j: (0, j)),
                  pl.BlockSpec((tf, d), lambda i, j: (j, 0))],
        out_specs=pl.BlockSpec((tm, d), row),
        out_shape=jax.ShapeDtypeStruct((n, d), F32),
        scratch_shapes=[pltpu.VMEM((tm, d), F32), pltpu.VMEM((tm, d), BF16), pltpu.VMEM((tm, d), F32)],
        compiler_params=pltpu.CompilerParams(
            dimension_semantics=("parallel", "arbitrary"), vmem_limit_bytes=VMEM_LIMIT),
        name="out_ffn",
    )(x2d, y_ret, y_nsa, mod, w_out, g2.reshape(1, d), gf.reshape(1, d), wg, wu, wd)


def _block(x, mod, ln_mix_g, ln_ffn_g, ln_out_g, w_in, pe_k, w1_k, w2_k, pe_v, w1_v, w2_v,
           w_out, w_ff_gate, w_ff_up, w_ff_down, final):
    b, s, d = x.shape
    pos = np.arange(s)
    lane = np.arange(LANES)
    tabs_full = _rope_tables(pos, np.ones(LANES, bool))
    tabs_half = _rope_tables(pos, lane < NSA_DH)
    n_piece = s // CMP_STRIDE
    tabs_cmp = _rope_tables(np.arange(n_piece) * CMP_STRIDE + CMP_LEN - 1, lane < NSA_DH)

    rq, rk, rv, rg_act, nq, kc, vc, ksv, kwv, gate = _in_proj(
        x, ln_mix_g, mod, _permute_w_in(w_in), tabs_full, tabs_half)
    y_ret = _retention(rq, rk, rv, rg_act, _retention_tables())
    w1, w2, pe = _compress_weights(w1_k, w2_k, w1_v, w2_v, pe_k, pe_v)
    kvcmp = _compress(kc, vc, w1, w2, pe, tabs_cmp)
    y_nsa = _nsa_attention(nq, kvcmp, ksv, kwv, gate, _overlap_t(s))
    out = _out_ffn(x.reshape(b * s, d), y_ret.reshape(b * s, -1), y_nsa.reshape(b * s, -1), mod,
                   w_out.astype(BF16), ln_ffn_g, ln_out_g,
                   w_ff_gate.astype(BF16), w_ff_up.astype(BF16), w_ff_down.astype(BF16), s)
    del final
    return out.reshape(b, s, d)


def kernel(x, c, ln_mix_g, ln_ffn_g, w_ada, b_ada, w_in, cmp_pe_k, cmp_w1_k, cmp_w2_k,
           cmp_pe_v, cmp_w1_v, cmp_w2_v, w_out, w_ff_gate, w_ff_up, w_ff_down, ln_final_g):
    depth = w_in.shape[0]
    assert depth == 1, "the fused final RMSNorm assumes a single layer"
    b, s, d = x.shape
    mod = _ada(c, w_ada[0], b_ada[0]).reshape(b, 6, d)
    return _block(x, mod, ln_mix_g[0], ln_ffn_g[0], ln_final_g, w_in[0],
                  cmp_pe_k[0], cmp_w1_k[0], cmp_w2_k[0], cmp_pe_v[0], cmp_w1_v[0], cmp_w2_v[0],
                  w_out[0], w_ff_gate[0], w_ff_up[0], w_ff_down[0], True)
```

```python
import numpy as np
import jax
import jax.numpy as jnp
from jax import lax
from jax.experimental import pallas as pl
from jax.experimental.pallas import tpu as pltpu

F32 = jnp.float32
BF16 = jnp.bfloat16

D_MODEL = 1024
RET_HEADS = 4
RET_DK = 64
RET_DV = 128
RET_CHUNK = 128
NSA_HEADS = 8
NSA_GROUPS = 2
NSA_HPG = NSA_HEADS // NSA_GROUPS
NSA_DH = 64
CMP_LEN = 32
CMP_STRIDE = 16
CMP_HIDDEN = 128
SLC_LEN = 64
SLC_TOPK = 16
WIN = 512
D_FF = ((8 * D_MODEL + 3 * 256 - 1) // (3 * 256)) * 256
ROPE_THETA = 10000.0
EPS = 1e-6
NEG = -1e30
FORCE = 1e6

LANES = 128
HALF = NSA_DH // 2

TM_IN = 512
TQ = 128
KC = 512
TM_FF = 512
TF = 256
VMEM_LIMIT = 56 * 1024 * 1024

_R_RQ, _R_RK, _R_RV, _R_RG, _R_NQ = 0, 256, 512, 1024, 1536
_R_KC, _R_VC, _R_KS, _R_VS, _R_KW, _R_VW, _R_GATE = 2048, 2176, 2304, 2432, 2560, 2688, 2816
IN_COLS_K = _R_GATE + NSA_GROUPS * LANES


def _sigmoid(x):
    return 1.0 / (1.0 + jnp.exp(-x))


def _nt(a, b):
    return lax.dot_general(a, b, (((1,), (1,)), ((), ())), preferred_element_type=F32)


def _nn(a, b):
    return jnp.dot(a, b, preferred_element_type=F32)


def _rope_tile(a, c, s1, s2):
    return a * c + pltpu.roll(a, HALF, 1) * s1 + pltpu.roll(a, LANES - HALF, 1) * s2


def _swap_halves(a):
    return pltpu.roll(a, LANES // 2, 1)


def _rope_tables(pos, rotary_lanes):
    pos = np.asarray(pos, np.float64)
    lane = np.arange(LANES)
    within = lane % NSA_DH
    freq = ROPE_THETA ** (-(within % HALF).astype(np.float64) / HALF)
    ang = pos[:, None] * freq[None, :]
    cos, sin = np.cos(ang), np.sin(ang)
    first = (within < HALF)[None, :]
    rot = np.asarray(rotary_lanes, bool)[None, :]
    c = np.where(rot, cos, 1.0)
    s1 = np.where(rot & ~first, sin, 0.0)
    s2 = np.where(rot & first, -sin, 0.0)
    return (jnp.asarray(c, F32), jnp.asarray(s1, F32), jnp.asarray(s2, F32))


def _block_onehot_table(seq):
    t = np.zeros((seq, LANES), np.float32)
    pos = np.arange(seq)
    t[pos, NSA_DH + pos // SLC_LEN] = 1.0
    return jnp.asarray(t)


def _retention_tables():
    h = jnp.arange(RET_HEADS, dtype=F32)
    log_g = jnp.log(1.0 - 2.0 ** (-5.0 - h))
    c = RET_CHUNK
    idx = jnp.arange(c, dtype=F32)
    diff = idx[:, None] - idx[None, :]
    causal = diff >= 0
    decay = jnp.where(causal, jnp.exp(log_g[:, None, None] * jnp.where(causal, diff, 0.0)), 0.0)
    zeta = jnp.exp(log_g[:, None] * (c - 1.0 - idx))
    xi = jnp.exp(log_g[:, None] * (idx + 1.0))
    g_chunk = jnp.exp(log_g * c)

    def pair_lanes(t):
        t = t.reshape(RET_HEADS // 2, 2, c)
        return jnp.repeat(jnp.transpose(t, (0, 2, 1)), RET_DK, axis=2)

    g_b = jnp.broadcast_to(g_chunk[:, None, None], (RET_HEADS, 1, LANES))
    return decay, pair_lanes(zeta), pair_lanes(xi), g_b


def _overlap_t(seq):
    n_c = seq // CMP_STRIDE - CMP_LEN // CMP_STRIDE + 1
    nb = seq // SLC_LEN
    cs = np.arange(n_c) * CMP_STRIDE
    bs = np.arange(nb) * SLC_LEN
    ov = np.maximum(np.minimum(cs[:, None] + CMP_LEN, bs[None] + SLC_LEN)
                    - np.maximum(cs[:, None], bs[None]), 0).astype(np.float64) / CMP_LEN
    ncp = seq // CMP_STRIDE
    ovp = np.zeros((ncp, nb))
    ovp[:n_c] = ov
    return jnp.asarray(ovp.T, BF16)


def _ada_kernel(c_ref, w_ref, b_ref, o_ref):
    c = c_ref[...]
    o_ref[...] = _nn(c * _sigmoid(c), w_ref[...]) + b_ref[...]


def _ada(c, w, b):
    bsz, d = c.shape
    n = w.shape[1]
    tn = 1024
    return pl.pallas_call(
        _ada_kernel,
        grid=(n // tn,),
        in_specs=[pl.BlockSpec((bsz, d), lambda j: (0, 0)),
                  pl.BlockSpec((d, tn), lambda j: (0, j)),
                  pl.BlockSpec((1, tn), lambda j: (0, j))],
        out_specs=pl.BlockSpec((bsz, tn), lambda j: (0, j)),
        out_shape=jax.ShapeDtypeStruct((bsz, n), F32),
        compiler_params=pltpu.CompilerParams(vmem_limit_bytes=VMEM_LIMIT),
        name="ada",
    )(c, w, b.reshape(1, n))


def _inproj_kernel(x_ref, g_ref, mod_ref, w_ref, c_ref, s1_ref, s2_ref, hot_ref,
                   rq_ref, rk_ref, rv_ref, rg_ref, nq_ref, kc_ref, vc_ref,
                   ksx_ref, vsx_ref, kwx_ref, vwx_ref, gate_ref):
    x = x_ref[0]
    y = x * lax.rsqrt(jnp.mean(x * x, axis=-1, keepdims=True) + EPS) * g_ref[...]
    h = y * (1.0 + mod_ref[0, 1:2, :]) + mod_ref[0, 0:1, :]
    hb = h.astype(BF16)

    def proj(c0, n):
        return _nn(hb, w_ref[:, c0:c0 + n])

    tabs = (c_ref[...], s1_ref[...], s2_ref[...])

    def roped(c0, n, scale, out_ref):
        for t in range(n // LANES):
            r = _rope_tile(proj(c0 + t * LANES, LANES), *tabs)
            if scale != 1.0:
                r = r * scale
            out_ref[0, :, t * LANES:(t + 1) * LANES] = r.astype(out_ref.dtype)

    roped(_R_RQ, 256, 1.0, rq_ref)
    roped(_R_RK, 256, RET_DK ** -0.5, rk_ref)
    rv_ref[0] = proj(_R_RV, 512).astype(BF16)
    rg = proj(_R_RG, 512)
    rg_ref[0] = (rg * _sigmoid(rg)).astype(BF16)
    roped(_R_NQ, 512, NSA_DH ** -0.5, nq_ref)
    kc_ref[0] = proj(_R_KC, LANES)
    vc_ref[0] = proj(_R_VC, LANES)

    low = lax.broadcasted_iota(jnp.int32, (x.shape[0], LANES), 1) < NSA_DH

    def per_group(tile, fill, out_ref):
        out_ref[0, :, 0:LANES] = jnp.where(low, tile, fill).astype(out_ref.dtype)
        out_ref[0, :, LANES:2 * LANES] = jnp.where(low, _swap_halves(tile), fill).astype(out_ref.dtype)

    per_group(_rope_tile(proj(_R_KS, LANES), *tabs), hot_ref[...], ksx_ref)
    per_group(proj(_R_VS, LANES), 1.0, vsx_ref)
    per_group(_rope_tile(proj(_R_KW, LANES), *tabs), 0.0, kwx_ref)
    per_group(proj(_R_VW, LANES), 1.0, vwx_ref)
    gate_ref[0] = _sigmoid(proj(_R_GATE, NSA_GROUPS * LANES))


def _prep_w_in(w_in):
    d = w_in.shape[0]
    cols = [w_in[:, :_R_GATE]]
    for g in range(NSA_GROUPS):
        for r in range(3):
            c0 = _R_GATE + r * NSA_HEADS + g * NSA_HPG
            cols.append(w_in[:, c0:c0 + NSA_HPG])
        cols.append(jnp.zeros((d, LANES - 3 * NSA_HPG), w_in.dtype))
    return jnp.concatenate(cols, axis=1).astype(BF16)


def _in_proj(x, ln_g, mod, w_prep, tabs, hot):
    b, s, d = x.shape
    tm = TM_IN
    grid = (b, s // tm)
    tab_spec = pl.BlockSpec((tm, LANES), lambda bi, j: (j, 0))

    def out(n, dtype):
        return (jax.ShapeDtypeStruct((b, s, n), dtype), pl.BlockSpec((1, tm, n), lambda bi, j: (bi, j, 0)))

    outs = [out(256, BF16), out(256, BF16), out(512, BF16), out(512, BF16), out(512, BF16),
            out(LANES, F32), out(LANES, F32),
            out(256, BF16), out(256, BF16), out(256, BF16), out(256, BF16), out(256, F32)]
    return pl.pallas_call(
        _inproj_kernel,
        grid=grid,
        in_specs=[pl.BlockSpec((1, tm, d), lambda bi, j: (bi, j, 0)),
                  pl.BlockSpec((1, d), lambda bi, j: (0, 0)),
                  pl.BlockSpec((1, 6, d), lambda bi, j: (bi, 0, 0)),
                  pl.BlockSpec((d, IN_COLS_K), lambda bi, j: (0, 0)),
                  tab_spec, tab_spec, tab_spec, tab_spec],
        out_specs=[o[1] for o in outs],
        out_shape=[o[0] for o in outs],
        compiler_params=pltpu.CompilerParams(
            dimension_semantics=("parallel", "parallel"), vmem_limit_bytes=VMEM_LIMIT),
        name="in_proj",
    )(x, ln_g.reshape(1, d), mod, w_prep, *tabs, hot)


def _ret_kernel(q_ref, k_ref, v_ref, rg_ref, dec_ref, zeta_ref, xi_ref, gch_ref, o_ref, state_ref):
    c = RET_CHUNK
    n_chunks = q_ref.shape[1] // c
    low = lax.broadcasted_iota(jnp.int32, (c, LANES), 1) < RET_DK
    state_ref[...] = jnp.zeros_like(state_ref)

    def body(n, carry):
        r0 = pl.multiple_of(n * c, c)
        rows = pl.ds(r0, c)
        for p in range(RET_HEADS // 2):
            pair = slice(p * LANES, (p + 1) * LANES)
            k2 = k_ref[0, rows, pair]
            q2 = q_ref[0, rows, pair].astype(F32)
            qx2 = q2 * xi_ref[p]
            kz_t = (k2.astype(F32) * zeta_ref[p]).T.astype(BF16)
            for e in range(2):
                h = 2 * p + e
                mine = low if e == 0 else jnp.logical_not(low)
                hs = slice(h * RET_DV, (h + 1) * RET_DV)
                v = v_ref[0, rows, hs]
                att = _nt(jnp.where(mine, q2, 0.0).astype(BF16), k2) * dec_ref[h]
                st = state_ref[h]
                y = _nn(att.astype(BF16), v) + _nn(jnp.where(mine, qx2, 0.0).astype(BF16), st.astype(BF16))
                state_ref[h] = st * gch_ref[h] + _nn(kz_t, v)
                yn = y * lax.rsqrt(jnp.mean(y * y, axis=-1, keepdims=True) + EPS)
                o_ref[0, rows, hs] = (yn * rg_ref[0, rows, hs].astype(F32)).astype(o_ref.dtype)
        return carry

    lax.fori_loop(0, n_chunks, body, 0)


def _retention(rq, rk, rv, rg_act, tables):
    b, s, _ = rq.shape
    decay, zeta_p, xi_p, g_b = tables
    whole = lambda a: pl.BlockSpec(a.shape, lambda bi: (0,) * a.ndim)
    row = lambda a: pl.BlockSpec((1,) + a.shape[1:], lambda bi: (bi, 0, 0))
    return pl.pallas_call(
        _ret_kernel,
        grid=(b,),
        in_specs=[row(rq), row(rk), row(rv), row(rg_act),
                  whole(decay), whole(zeta_p), whole(xi_p), whole(g_b)],
        out_specs=row(rv),
        out_shape=jax.ShapeDtypeStruct(rv.shape, BF16),
        scratch_shapes=[pltpu.VMEM((RET_HEADS, LANES, RET_DV), F32)],
        compiler_params=pltpu.CompilerParams(
            dimension_semantics=("parallel",), vmem_limit_bytes=VMEM_LIMIT),
        name="retention",
    )(rq, rk, rv, rg_act, decay, zeta_p, xi_p, g_b)


def _cmp_kernel(kc_ref, vc_ref, w1_ref, pe_ref, w2_ref, c_ref, s1_ref, s2_ref, kv_ref, vk_ref):
    n_piece = kc_ref.shape[1] // CMP_STRIDE
    hid_w = w1_ref.shape[3]
    out = jnp.zeros((n_piece, kv_ref.shape[2]), F32)
    for t, src in enumerate((kc_ref, vc_ref)):
        a_lo = jnp.zeros((n_piece, hid_w), F32)
        a_hi = jnp.zeros((n_piece, hid_w), F32)
        for r in range(CMP_STRIDE):
            xr = src[0, pl.ds(r, n_piece, stride=CMP_STRIDE), :]
            lo, hi = r, CMP_STRIDE + r
            a_lo = a_lo + _nn((xr + pe_ref[t, lo:lo + 1, :]).astype(BF16), w1_ref[t, lo])
            a_hi = a_hi + _nn((xr + pe_ref[t, hi:hi + 1, :]).astype(BF16), w1_ref[t, hi])
        hid = a_lo + pltpu.roll(a_hi, n_piece - 1, 0)
        out = out + _nn((hid * _sigmoid(hid)).astype(BF16), w2_ref[t])
    for t in range(out.shape[1] // LANES):
        sl = slice(t * LANES, (t + 1) * LANES)
        kv = _rope_tile(out[:, sl], c_ref[...], s1_ref[...], s2_ref[...])
        kv_ref[0, :, sl] = kv.astype(kv_ref.dtype)
        vk_ref[0, :, sl] = _swap_halves(kv).astype(vk_ref.dtype)


def _compress_weights(w1_k, w2_k, w1_v, w2_v, pe_k, pe_v):
    dh, hid = NSA_DH, CMP_HIDDEN

    def first(w1):
        w = w1.reshape(CMP_LEN, dh, hid)
        z = jnp.zeros_like(w)
        return jnp.concatenate([jnp.concatenate([w, z], axis=2), jnp.concatenate([z, w], axis=2)], axis=1)

    def second(w2, off):
        z = jnp.zeros_like(w2)
        rows = []
        for g in range(NSA_GROUPS):
            c = [z, z, z, z]
            c[2 * g + off] = w2
            rows.append(jnp.concatenate(c, axis=1))
        return jnp.concatenate(rows, axis=0)

    w1 = jnp.stack([first(w1_k), first(w1_v)]).astype(BF16)
    w2 = jnp.stack([second(w2_k, 0), second(w2_v, 1)]).astype(BF16)
    pe = jnp.stack([jnp.concatenate([pe_k, pe_k], axis=1), jnp.concatenate([pe_v, pe_v], axis=1)])
    return w1, w2, pe


def _compress(kc, vc, w1, w2, pe, tabs_cmp):
    b, s, w = kc.shape
    n_piece = s // CMP_STRIDE
    const2 = lambda bi: (0, 0)
    src = pl.BlockSpec((1, s, w), lambda bi: (bi, 0, 0))
    dst = pl.BlockSpec((1, n_piece, 2 * w), lambda bi: (bi, 0, 0))
    shape = jax.ShapeDtypeStruct((b, n_piece, 2 * w), BF16)
    return pl.pallas_call(
        _cmp_kernel,
        grid=(b,),
        in_specs=[src, src,
                  pl.BlockSpec(w1.shape, lambda bi: (0, 0, 0, 0)),
                  pl.BlockSpec(pe.shape, lambda bi: (0, 0, 0)),
                  pl.BlockSpec(w2.shape, lambda bi: (0, 0, 0)),
                  pl.BlockSpec((n_piece, LANES), const2),
                  pl.BlockSpec((n_piece, LANES), const2),
                  pl.BlockSpec((n_piece, LANES), const2)],
        out_specs=[dst, dst],
        out_shape=[shape, shape],
        compiler_params=pltpu.CompilerParams(
            dimension_semantics=("parallel",), vmem_limit_bytes=VMEM_LIMIT),
        name="compress",
    )(kc, vc, w1, pe, w2, *tabs_cmp)


def _nsa_kernel(q_ref, kcmp_ref, vcmp_ref, ksx_ref, vsx_ref, kwx_ref, vwx_ref, gate_ref, ovt_ref, o_ref):
    tq, kc = TQ, KC
    seq = q_ref.shape[1]
    nb = seq // SLC_LEN
    hpg = NSA_HPG
    wlen = WIN + tq

    lane = lax.broadcasted_iota(jnp.int32, (tq, LANES), 1)
    low = lane < NSA_DH
    row = lax.broadcasted_iota(jnp.int32, (tq, LANES), 0)
    eye = jnp.where(lax.broadcasted_iota(jnp.int32, (tq, tq), 0)
                    == lax.broadcasted_iota(jnp.int32, (tq, tq), 1), 1.0, 0.0).astype(BF16)
    blk = lax.broadcasted_iota(jnp.int32, (nb, tq), 0)
    col = lax.broadcasted_iota(jnp.int32, (nb, tq), 1)
    krow = lax.broadcasted_iota(jnp.int32, (tq, kc), 0)
    kcol = lax.broadcasted_iota(jnp.int32, (tq, kc), 1)
    wrow = lax.broadcasted_iota(jnp.int32, (tq, wlen), 0)
    wcol = lax.broadcasted_iota(jnp.int32, (tq, wlen), 1)
    kcm = kcmp_ref[0]
    vcm = vcmp_ref[0]
    ovt = ovt_ref[...]

    def normalise(acc):
        return acc / _swap_halves(acc)

    def tile(i, carry):
        t0 = pl.multiple_of(i * tq, tq)
        rows = pl.ds(t0, tq)
        qf = q_ref[0, rows, :].astype(F32)
        heads = []
        for hh in range(hpg):
            t = qf[:, (hh // 2) * LANES:(hh // 2 + 1) * LANES]
            if hh % 2 == 1:
                t = _swap_halves(t)
            heads.append(jnp.where(low, t, 0.0))
        qs = jnp.concatenate(heads, axis=0).astype(BF16)

        s_c = _nt(qs, kcm)
        cmask = (lane * CMP_STRIDE + (CMP_LEN - 1)) <= (row + t0)
        o_cmp = []
        psum = jnp.zeros((tq, LANES), F32)
        for hh in range(hpg):
            sh = jnp.where(cmask, s_c[hh * tq:(hh + 1) * tq], NEG)
            e = jnp.exp(sh - jnp.max(sh, axis=-1, keepdims=True))
            p = jnp.where(cmask, e / jnp.sum(e, axis=-1, keepdims=True), 0.0)
            psum = psum + p
            o_cmp.append(_nn(p.astype(BF16), vcm))

        p_hi = psum.astype(BF16)
        p_lo = (psum - p_hi.astype(F32)).astype(BF16)
        imp = _nt(ovt, p_hi) + _nt(ovt, p_lo)
        tcol = col + t0
        cur = tcol // SLC_LEN
        forced = (blk == 0) | (blk == cur) | (blk == cur - 1)
        bcausal = blk * SLC_LEN <= tcol
        imp = jnp.where(bcausal, jnp.where(forced, FORCE, imp), NEG)
        rank = jnp.zeros((nb, tq), F32)
        for j in range(nb):
            r = imp[j:j + 1, :]
            rank = rank + jnp.where(blk > j, jnp.where(r >= imp, 1.0, 0.0), jnp.where(r > imp, 1.0, 0.0))
        chosen = (rank < float(min(SLC_TOPK, nb))) & bcausal
        feat = jnp.concatenate([jnp.zeros((NSA_DH, tq), F32), jnp.where(chosen, 0.0, NEG),
                                jnp.zeros((LANES - NSA_DH - nb, tq), F32)], axis=0).astype(BF16)
        qbias = _nt(eye, feat)
        qsel = jnp.concatenate([hd + qbias for hd in heads], axis=0).astype(BF16)

        def sel_chunk(k0, state, diagonal):
            m, acc = state
            s = _nt(qsel, ksx_ref[0, pl.ds(k0, kc), :])
            if diagonal:
                bias = jnp.where((kcol + k0) <= (krow + t0), 0.0, NEG)
                s = s + jnp.concatenate([bias] * hpg, axis=0)
            m_new = jnp.maximum(m, jnp.max(s, axis=-1, keepdims=True))
            p = jnp.exp(s - m_new)
            acc = jnp.exp(m - m_new) * acc + _nn(p.astype(BF16), vsx_ref[0, pl.ds(k0, kc), :])
            return m_new, acc

        n_past = t0 // kc
        state = (jnp.full((hpg * tq, 1), NEG, F32), jnp.zeros((hpg * tq, LANES), F32))
        state = lax.fori_loop(0, n_past, lambda ci, st: sel_chunk(pl.multiple_of(ci * kc, kc), st, False), state)
        _, acc_s = sel_chunk(pl.multiple_of(n_past * kc, kc), state, True)
        o_sel = normalise(acc_s)

        ws = pl.multiple_of(jnp.maximum(t0 - WIN, 0), tq)
        s_w = _nt(qs, kwx_ref[0, pl.ds(ws, wlen), :])
        wpos = wcol + ws
        tw = wrow + t0
        wbias = jnp.where((wpos <= tw) & (wpos > tw - WIN), 0.0, NEG)
        s_w = s_w + jnp.concatenate([wbias] * hpg, axis=0)
        e_w = jnp.exp(s_w - jnp.max(s_w, axis=-1, keepdims=True))
        o_win = normalise(_nn(e_w.astype(BF16), vwx_ref[0, pl.ds(ws, wlen), :]))

        gt = gate_ref[0, rows, :]
        outs = []
        for hh in range(hpg):
            hr = slice(hh * tq, (hh + 1) * tq)
            g_c = gt[:, 0 * hpg + hh:0 * hpg + hh + 1]
            g_s = gt[:, 1 * hpg + hh:1 * hpg + hh + 1]
            g_w = gt[:, 2 * hpg + hh:2 * hpg + hh + 1]
            outs.append(g_c * o_cmp[hh] + g_s * o_sel[hr] + g_w * o_win[hr])
        for pr in range(hpg // 2):
            both = jnp.where(low, outs[2 * pr], _swap_halves(outs[2 * pr + 1]))
            o_ref[0, rows, pr * LANES:(pr + 1) * LANES] = both.astype(o_ref.dtype)
        return carry

    lax.fori_loop(0, seq // tq, tile, 0)


def _nsa_attention(nq, kvcmp, vkcmp, ksx, vsx, kwx, vwx, gate, ovt):
    b, s, _ = nq.shape
    n_cmp = kvcmp.shape[1]
    gw = NSA_HPG * NSA_DH
    per_group = lambda rows, width: pl.BlockSpec((1, rows, width), lambda bi, g: (bi, 0, g))
    return pl.pallas_call(
        _nsa_kernel,
        grid=(b, NSA_GROUPS),
        in_specs=[per_group(s, gw),
                  per_group(n_cmp, LANES), per_group(n_cmp, LANES),
                  per_group(s, LANES), per_group(s, LANES), per_group(s, LANES), per_group(s, LANES),
                  per_group(s, LANES),
                  pl.BlockSpec(ovt.shape, lambda bi, g: (0, 0))],
        out_specs=per_group(s, gw),
        out_shape=jax.ShapeDtypeStruct((b, s, NSA_HEADS * NSA_DH), BF16),
        compiler_params=pltpu.CompilerParams(
            dimension_semantics=("parallel", "parallel"), vmem_limit_bytes=VMEM_LIMIT),
        name="nsa_attn",
    )(nq, kvcmp, vkcmp, ksx, vsx, kwx, vwx, gate, ovt)


def _ffn_kernel(x_ref, yr_ref, yn_ref, mod_ref, wo_ref, g2_ref, gf_ref, wg_ref, wu_ref, wd_ref,
                o_ref, x1_ref, act_ref):
    half_w = yr_ref.shape[1]
    d_ff = wg_ref.shape[1]
    mix = _nn(yr_ref[...], wo_ref[0:half_w, :]) + _nn(yn_ref[...], wo_ref[half_w:2 * half_w, :])
    x1 = x_ref[...] + mod_ref[0, 2:3, :] * mix
    x1_ref[...] = x1
    y = x1 * lax.rsqrt(jnp.mean(x1 * x1, axis=-1, keepdims=True) + EPS) * g2_ref[...]
    h2 = (y * (1.0 + mod_ref[0, 4:5, :]) + mod_ref[0, 3:4, :]).astype(BF16)
    for j in range(d_ff // TF):
        sl = slice(j * TF, (j + 1) * TF)
        gate = _nn(h2, wg_ref[:, sl])
        up = _nn(h2, wu_ref[:, sl])
        act_ref[:, sl] = (gate * _sigmoid(gate) * up).astype(BF16)
    xo = x1_ref[...] + mod_ref[0, 5:6, :] * _nn(act_ref[...], wd_ref[...])
    o_ref[...] = xo * lax.rsqrt(jnp.mean(xo * xo, axis=-1, keepdims=True) + EPS) * gf_ref[...]


def _out_ffn(x2d, y_ret, y_nsa, mod, w_out, g2, gf, wg, wu, wd, seq):
    n, d = x2d.shape
    tm = TM_FF
    d_ff = wg.shape[1]
    tiles_per_seq = seq // tm
    half_w = y_ret.shape[1]
    row = lambda i: (i, 0)
    resident = lambda a: pl.BlockSpec(a.shape, lambda i: (0, 0), pipeline_mode=pl.Buffered(1))
    return pl.pallas_call(
        _ffn_kernel,
        grid=(n // tm,),
        in_specs=[pl.BlockSpec((tm, d), row),
                  pl.BlockSpec((tm, half_w), row),
                  pl.BlockSpec((tm, half_w), row),
                  pl.BlockSpec((1, 6, d), lambda i: (i // tiles_per_seq, 0, 0)),
                  resident(w_out),
                  pl.BlockSpec((1, d), lambda i: (0, 0)),
                  pl.BlockSpec((1, d), lambda i: (0, 0)),
                  resident(wg), resident(wu), resident(wd)],
        out_specs=pl.BlockSpec((tm, d), row),
        out_shape=jax.ShapeDtypeStruct((n, d), F32),
        scratch_shapes=[pltpu.VMEM((tm, d), F32), pltpu.VMEM((tm, d_ff), BF16)],
        compiler_params=pltpu.CompilerParams(
            dimension_semantics=("parallel",), vmem_limit_bytes=VMEM_LIMIT),
        name="out_ffn",
    )(x2d, y_ret, y_nsa, mod, w_out, g2.reshape(1, d), gf.reshape(1, d), wg, wu, wd)


def kernel(x, c, ln_mix_g, ln_ffn_g, w_ada, b_ada, w_in, cmp_pe_k, cmp_w1_k, cmp_w2_k,
           cmp_pe_v, cmp_w1_v, cmp_w2_v, w_out, w_ff_gate, w_ff_up, w_ff_down, ln_final_g):
    assert w_in.shape[0] == 1, "the final RMSNorm is fused into the (single) layer's FFN kernel"
    b, s, d = x.shape
    lane = np.arange(LANES)
    tabs = _rope_tables(np.arange(s), np.ones(LANES, bool))
    n_piece = s // CMP_STRIDE
    tabs_cmp = _rope_tables(np.arange(n_piece) * CMP_STRIDE + CMP_LEN - 1, lane < NSA_DH)

    mod = _ada(c, w_ada[0], b_ada[0]).reshape(b, 6, d)
    rq, rk, rv, rg_act, nq, kc, vc, ksx, vsx, kwx, vwx, gate = _in_proj(
        x, ln_mix_g[0], mod, _prep_w_in(w_in[0]), tabs, _block_onehot_table(s))
    y_ret = _retention(rq, rk, rv, rg_act, _retention_tables())
    w1, w2, pe = _compress_weights(cmp_w1_k[0], cmp_w2_k[0], cmp_w1_v[0], cmp_w2_v[0],
                                   cmp_pe_k[0], cmp_pe_v[0])
    kvcmp, vkcmp = _compress(kc, vc, w1, w2, pe, tabs_cmp)
    y_nsa = _nsa_attention(nq, kvcmp, vkcmp, ksx, vsx, kwx, vwx, gate, _overlap_t(s))
    out = _out_ffn(x.reshape(b * s, d), y_ret.reshape(b * s, -1), y_nsa.reshape(b * s, -1), mod,
                   w_out[0].astype(BF16), ln_ffn_g[0], ln_final_g,
                   w_ff_gate[0].astype(BF16), w_ff_up[0].astype(BF16), w_ff_down[0].astype(BF16), s)
    return out.reshape(b, s, d)
```

```python
import numpy as np
import jax
import jax.numpy as jnp
from jax import lax
from jax.experimental import pallas as pl
from jax.experimental.pallas import tpu as pltpu

F32 = jnp.float32
BF16 = jnp.bfloat16

D_MODEL = 1024
RET_HEADS = 4
RET_DK = 64
RET_DV = 128
RET_CHUNK = 128
NSA_HEADS = 8
NSA_GROUPS = 2
NSA_HPG = NSA_HEADS // NSA_GROUPS
NSA_DH = 64
CMP_LEN = 32
CMP_STRIDE = 16
CMP_HIDDEN = 128
SLC_LEN = 64
SLC_TOPK = 16
WIN = 512
D_FF = ((8 * D_MODEL + 3 * 256 - 1) // (3 * 256)) * 256
ROPE_THETA = 10000.0
EPS = 1e-6
NEG = -1e30
FORCE = 1e6

LANES = 128
HALF = NSA_DH // 2
ONES_ROWS = 16
GATE_ROWS = 16

TM_IN = 512
TQ = 256
KC = 512
TM_FF = 512
TF = 256
VMEM_LIMIT = 56 * 1024 * 1024

_R_RQ, _R_RK, _R_RV, _R_RG, _R_NQ = 0, 256, 512, 1024, 1536
_R_KC, _R_VC, _R_KS, _R_VS, _R_KW, _R_VW, _R_GATE = 2048, 2176, 2304, 2432, 2560, 2688, 2816
IN_COLS_K = _R_GATE + NSA_GROUPS * LANES


def _sigmoid(x):
    return 1.0 / (1.0 + jnp.exp(-x))


def _nt(a, b):
    return lax.dot_general(a, b, (((1,), (1,)), ((), ())), preferred_element_type=F32)


def _nn(a, b):
    return jnp.dot(a, b, preferred_element_type=F32)


def _rope_tile(a, c, s1, s2):
    return a * c + pltpu.roll(a, HALF, 1) * s1 + pltpu.roll(a, LANES - HALF, 1) * s2


def _swap_halves(a):
    return pltpu.roll(a, LANES // 2, 1)


def _rope_tables(pos, rotary_lanes):
    pos = np.asarray(pos, np.float64)
    lane = np.arange(LANES)
    within = lane % NSA_DH
    freq = ROPE_THETA ** (-(within % HALF).astype(np.float64) / HALF)
    ang = pos[:, None] * freq[None, :]
    cos, sin = np.cos(ang), np.sin(ang)
    first = (within < HALF)[None, :]
    rot = np.asarray(rotary_lanes, bool)[None, :]
    c = np.where(rot, cos, 1.0)
    s1 = np.where(rot & ~first, sin, 0.0)
    s2 = np.where(rot & first, -sin, 0.0)
    return (jnp.asarray(c, F32), jnp.asarray(s1, F32), jnp.asarray(s2, F32))


def _block_onehot_table(seq):
    t = np.zeros((seq, LANES), np.float32)
    pos = np.arange(seq)
    t[pos, NSA_DH + pos // SLC_LEN] = 1.0
    return jnp.asarray(t)


def _retention_tables():
    h = jnp.arange(RET_HEADS, dtype=F32)
    log_g = jnp.log(1.0 - 2.0 ** (-5.0 - h))
    c = RET_CHUNK
    idx = jnp.arange(c, dtype=F32)
    diff = idx[:, None] - idx[None, :]
    causal = diff >= 0
    decay = jnp.where(causal, jnp.exp(log_g[:, None, None] * jnp.where(causal, diff, 0.0)), 0.0)
    zeta = jnp.exp(log_g[:, None] * (c - 1.0 - idx))
    xi = jnp.exp(log_g[:, None] * (idx + 1.0))
    g_chunk = jnp.exp(log_g * c)

    def pair_lanes(t):
        t = t.reshape(RET_HEADS // 2, 2, c)
        return jnp.repeat(jnp.transpose(t, (0, 2, 1)), RET_DK, axis=2)

    g_b = jnp.broadcast_to(g_chunk[:, None, None], (RET_HEADS, 1, LANES))
    return decay, pair_lanes(zeta), pair_lanes(xi), g_b


def _overlap_t(seq):
    n_c = seq // CMP_STRIDE - CMP_LEN // CMP_STRIDE + 1
    nb = seq // SLC_LEN
    cs = np.arange(n_c) * CMP_STRIDE
    bs = np.arange(nb) * SLC_LEN
    ov = np.maximum(np.minimum(cs[:, None] + CMP_LEN, bs[None] + SLC_LEN)
                    - np.maximum(cs[:, None], bs[None]), 0).astype(np.float64) / CMP_LEN
    ncp = seq // CMP_STRIDE
    ovp = np.zeros((ncp, nb))
    ovp[:n_c] = ov
    return jnp.asarray(ovp.T, BF16)


def _ada_kernel(c_ref, w_ref, b_ref, o_ref):
    c = c_ref[...]
    o_ref[...] = _nn(c * _sigmoid(c), w_ref[...]) + b_ref[...]


def _ada(c, w, b):
    bsz, d = c.shape
    n = w.shape[1]
    tn = 1024
    return pl.pallas_call(
        _ada_kernel,
        grid=(n // tn,),
        in_specs=[pl.BlockSpec((bsz, d), lambda j: (0, 0)),
                  pl.BlockSpec((d, tn), lambda j: (0, j)),
                  pl.BlockSpec((1, tn), lambda j: (0, j))],
        out_specs=pl.BlockSpec((bsz, tn), lambda j: (0, j)),
        out_shape=jax.ShapeDtypeStruct((bsz, n), F32),
        compiler_params=pltpu.CompilerParams(vmem_limit_bytes=VMEM_LIMIT),
        name="ada",
    )(c, w, b.reshape(1, n))


def _inproj_kernel(x_ref, g_ref, mod_ref, w_ref, c_ref, s1_ref, s2_ref, hot_ref,
                   rq_ref, rk_ref, rv_ref, rg_ref, nq_ref, kc_ref, vc_ref,
                   ksx_ref, vst_ref, kwx_ref, vwt_ref, gate_ref):
    x = x_ref[0]
    y = x * lax.rsqrt(jnp.mean(x * x, axis=-1, keepdims=True) + EPS) * g_ref[...]
    h = y * (1.0 + mod_ref[0, 1:2, :]) + mod_ref[0, 0:1, :]
    hb = h.astype(BF16)

    def proj(c0, n):
        return _nn(hb, w_ref[:, c0:c0 + n])

    tabs = (c_ref[...], s1_ref[...], s2_ref[...])

    def roped(c0, n, scale, out_ref):
        for t in range(n // LANES):
            r = _rope_tile(proj(c0 + t * LANES, LANES), *tabs)
            if scale != 1.0:
                r = r * scale
            out_ref[0, :, t * LANES:(t + 1) * LANES] = r.astype(out_ref.dtype)

    roped(_R_RQ, 256, 1.0, rq_ref)
    roped(_R_RK, 256, RET_DK ** -0.5, rk_ref)
    rv_ref[0] = proj(_R_RV, 512).astype(BF16)
    rg = proj(_R_RG, 512)
    rg_ref[0] = (rg * _sigmoid(rg)).astype(BF16)
    roped(_R_NQ, 512, NSA_DH ** -0.5, nq_ref)
    kc_ref[0] = proj(_R_KC, LANES)
    vc_ref[0] = proj(_R_VC, LANES)

    low = lax.broadcasted_iota(jnp.int32, (x.shape[0], LANES), 1) < NSA_DH

    def per_group(tile, fill, out_ref):
        out_ref[0, :, 0:LANES] = jnp.where(low, tile, fill).astype(out_ref.dtype)
        out_ref[0, :, LANES:2 * LANES] = jnp.where(low, _swap_halves(tile), fill).astype(out_ref.dtype)

    def per_group_t(tile, out_ref):
        t = tile.T
        ones = jnp.ones((ONES_ROWS, t.shape[1]), out_ref.dtype)
        for g in range(NSA_GROUPS):
            out_ref[0, g, 0:NSA_DH, :] = t[g * NSA_DH:(g + 1) * NSA_DH].astype(out_ref.dtype)
            out_ref[0, g, NSA_DH:NSA_DH + ONES_ROWS, :] = ones

    per_group(_rope_tile(proj(_R_KS, LANES), *tabs), hot_ref[...], ksx_ref)
    per_group_t(proj(_R_VS, LANES), vst_ref)
    per_group(_rope_tile(proj(_R_KW, LANES), *tabs), 0.0, kwx_ref)
    per_group_t(proj(_R_VW, LANES), vwt_ref)
    for g in range(NSA_GROUPS):
        gt = _sigmoid(proj(_R_GATE + g * LANES, LANES)).T
        gate_ref[0, g] = gt[0:GATE_ROWS]


def _prep_w_in(w_in):
    d = w_in.shape[0]
    cols = [w_in[:, :_R_GATE]]
    for g in range(NSA_GROUPS):
        for r in range(3):
            c0 = _R_GATE + r * NSA_HEADS + g * NSA_HPG
            cols.append(w_in[:, c0:c0 + NSA_HPG])
        cols.append(jnp.zeros((d, LANES - 3 * NSA_HPG), w_in.dtype))
    return jnp.concatenate(cols, axis=1).astype(BF16)


def _in_proj(x, ln_g, mod, w_prep, tabs, hot):
    b, s, d = x.shape
    tm = TM_IN
    grid = (b, s // tm)
    tab_spec = pl.BlockSpec((tm, LANES), lambda bi, j: (j, 0))

    def out(n, dtype):
        return (jax.ShapeDtypeStruct((b, s, n), dtype), pl.BlockSpec((1, tm, n), lambda bi, j: (bi, j, 0)))

    def out_t(rows, dtype):
        return (jax.ShapeDtypeStruct((b, NSA_GROUPS, rows, s), dtype),
                pl.BlockSpec((1, NSA_GROUPS, rows, tm), lambda bi, j: (bi, 0, 0, j)))

    vt_rows = NSA_DH + ONES_ROWS
    outs = [out(256, BF16), out(256, BF16), out(512, BF16), out(512, BF16), out(512, BF16),
            out(LANES, F32), out(LANES, F32),
            out(256, BF16), out_t(vt_rows, BF16), out(256, BF16), out_t(vt_rows, BF16),
            out_t(GATE_ROWS, F32)]
    return pl.pallas_call(
        _inproj_kernel,
        grid=grid,
        in_specs=[pl.BlockSpec((1, tm, d), lambda bi, j: (bi, j, 0)),
                  pl.BlockSpec((1, d), lambda bi, j: (0, 0)),
                  pl.BlockSpec((1, 6, d), lambda bi, j: (bi, 0, 0)),
                  pl.BlockSpec((d, IN_COLS_K), lambda bi, j: (0, 0)),
                  tab_spec, tab_spec, tab_spec, tab_spec],
        out_specs=[o[1] for o in outs],
        out_shape=[o[0] for o in outs],
        compiler_params=pltpu.CompilerParams(
            dimension_semantics=("parallel", "parallel"), vmem_limit_bytes=VMEM_LIMIT),
        name="in_proj",
    )(x, ln_g.reshape(1, d), mod, w_prep, *tabs, hot)


def _ret_kernel(q_ref, k_ref, v_ref, rg_ref, dec_ref, zeta_ref, xi_ref, gch_ref, o_ref, state_ref):
    c = RET_CHUNK
    n_chunks = q_ref.shape[1] // c
    low = lax.broadcasted_iota(jnp.int32, (c, LANES), 1) < RET_DK
    state_ref[...] = jnp.zeros_like(state_ref)

    def body(n, carry):
        r0 = pl.multiple_of(n * c, c)
        rows = pl.ds(r0, c)
        for p in range(RET_HEADS // 2):
            pair = slice(p * LANES, (p + 1) * LANES)
            k2 = k_ref[0, rows, pair]
            q2 = q_ref[0, rows, pair].astype(F32)
            qx2 = q2 * xi_ref[p]
            kz_t = (k2.astype(F32) * zeta_ref[p]).T.astype(BF16)
            for e in range(2):
                h = 2 * p + e
                mine = low if e == 0 else jnp.logical_not(low)
                hs = slice(h * RET_DV, (h + 1) * RET_DV)
                v = v_ref[0, rows, hs]
                att = _nt(jnp.where(mine, q2, 0.0).astype(BF16), k2) * dec_ref[h]
                st = state_ref[h]
                y = _nn(att.astype(BF16), v) + _nn(jnp.where(mine, qx2, 0.0).astype(BF16), st.astype(BF16))
                state_ref[h] = st * gch_ref[h] + _nn(kz_t, v)
                yn = y * lax.rsqrt(jnp.mean(y * y, axis=-1, keepdims=True) + EPS)
                o_ref[0, rows, hs] = (yn * rg_ref[0, rows, hs].astype(F32)).astype(o_ref.dtype)
        return carry

    lax.fori_loop(0, n_chunks, body, 0)


def _retention(rq, rk, rv, rg_act, tables):
    b, s, _ = rq.shape
    decay, zeta_p, xi_p, g_b = tables
    whole = lambda a: pl.BlockSpec(a.shape, lambda bi: (0,) * a.ndim)
    row = lambda a: pl.BlockSpec((1,) + a.shape[1:], lambda bi: (bi, 0, 0))
    return pl.pallas_call(
        _ret_kernel,
        grid=(b,),
        in_specs=[row(rq), row(rk), row(rv), row(rg_act),
                  whole(decay), whole(zeta_p), whole(xi_p), whole(g_b)],
        out_specs=row(rv),
        out_shape=jax.ShapeDtypeStruct(rv.shape, BF16),
        scratch_shapes=[pltpu.VMEM((RET_HEADS, LANES, RET_DV), F32)],
        compiler_params=pltpu.CompilerParams(
            dimension_semantics=("parallel",), vmem_limit_bytes=VMEM_LIMIT),
        name="retention",
    )(rq, rk, rv, rg_act, decay, zeta_p, xi_p, g_b)


def _cmp_kernel(kc_ref, vc_ref, w1_ref, pe_ref, w2_ref, c_ref, s1_ref, s2_ref, kv_ref, vt_ref):
    n_piece = kc_ref.shape[1] // CMP_STRIDE
    hid_w = w1_ref.shape[3]
    out = jnp.zeros((n_piece, kv_ref.shape[2]), F32)
    for t, src in enumerate((kc_ref, vc_ref)):
        a_lo = jnp.zeros((n_piece, hid_w), F32)
        a_hi = jnp.zeros((n_piece, hid_w), F32)
        for r in range(CMP_STRIDE):
            xr = src[0, pl.ds(r, n_piece, stride=CMP_STRIDE), :]
            lo, hi = r, CMP_STRIDE + r
            a_lo = a_lo + _nn((xr + pe_ref[t, lo:lo + 1, :]).astype(BF16), w1_ref[t, lo])
            a_hi = a_hi + _nn((xr + pe_ref[t, hi:hi + 1, :]).astype(BF16), w1_ref[t, hi])
        hid = a_lo + pltpu.roll(a_hi, n_piece - 1, 0)
        out = out + _nn((hid * _sigmoid(hid)).astype(BF16), w2_ref[t])
    for t in range(out.shape[1] // LANES):
        sl = slice(t * LANES, (t + 1) * LANES)
        kv = _rope_tile(out[:, sl], c_ref[...], s1_ref[...], s2_ref[...])
        kv_ref[0, :, sl] = kv.astype(kv_ref.dtype)
        vt_ref[0, t] = kv.T[NSA_DH:2 * NSA_DH].astype(vt_ref.dtype)


def _compress_weights(w1_k, w2_k, w1_v, w2_v, pe_k, pe_v):
    dh, hid = NSA_DH, CMP_HIDDEN

    def first(w1):
        w = w1.reshape(CMP_LEN, dh, hid)
        z = jnp.zeros_like(w)
        return jnp.concatenate([jnp.concatenate([w, z], axis=2), jnp.concatenate([z, w], axis=2)], axis=1)

    def second(w2, off):
        z = jnp.zeros_like(w2)
        rows = []
        for g in range(NSA_GROUPS):
            c = [z, z, z, z]
            c[2 * g + off] = w2
            rows.append(jnp.concatenate(c, axis=1))
        return jnp.concatenate(rows, axis=0)

    w1 = jnp.stack([first(w1_k), first(w1_v)]).astype(BF16)
    w2 = jnp.stack([second(w2_k, 0), second(w2_v, 1)]).astype(BF16)
    pe = jnp.stack([jnp.concatenate([pe_k, pe_k], axis=1), jnp.concatenate([pe_v, pe_v], axis=1)])
    return w1, w2, pe


def _compress(kc, vc, w1, w2, pe, tabs_cmp):
    b, s, w = kc.shape
    n_piece = s // CMP_STRIDE
    const2 = lambda bi: (0, 0)
    src = pl.BlockSpec((1, s, w), lambda bi: (bi, 0, 0))
    dst = pl.BlockSpec((1, n_piece, 2 * w), lambda bi: (bi, 0, 0))
    shape = jax.ShapeDtypeStruct((b, n_piece, 2 * w), BF16)
    return pl.pallas_call(
        _cmp_kernel,
        grid=(b,),
        in_specs=[src, src,
                  pl.BlockSpec(w1.shape, lambda bi: (0, 0, 0, 0)),
                  pl.BlockSpec(pe.shape, lambda bi: (0, 0, 0)),
                  pl.BlockSpec(w2.shape, lambda bi: (0, 0, 0)),
                  pl.BlockSpec((n_piece, LANES), const2),
                  pl.BlockSpec((n_piece, LANES), const2),
                  pl.BlockSpec((n_piece, LANES), const2)],
        out_specs=[dst, pl.BlockSpec((1, NSA_GROUPS, NSA_DH, n_piece), lambda bi: (bi, 0, 0, 0))],
        out_shape=[shape, jax.ShapeDtypeStruct((b, NSA_GROUPS, NSA_DH, n_piece), BF16)],
        compiler_params=pltpu.CompilerParams(
            dimension_semantics=("parallel",), vmem_limit_bytes=VMEM_LIMIT),
        name="compress",
    )(kc, vc, w1, pe, w2, *tabs_cmp)


def _nsa_kernel(q_ref, kcmp_ref, vct_ref, ksx_ref, vst_ref, kwx_ref, vwt_ref, gate_ref, ovt_ref, o_ref):
    tq, kc = TQ, KC
    seq = q_ref.shape[1]
    nb = seq // SLC_LEN
    n_cmp = kcmp_ref.shape[1]
    hpg, dh = NSA_HPG, NSA_DH
    wlen = WIN + tq

    low = lax.broadcasted_iota(jnp.int32, (tq, LANES), 1) < dh
    eye = jnp.where(lax.broadcasted_iota(jnp.int32, (tq, tq), 0)
                    == lax.broadcasted_iota(jnp.int32, (tq, tq), 1), 1.0, 0.0).astype(BF16)
    blk = lax.broadcasted_iota(jnp.int32, (nb, tq), 0)
    col = lax.broadcasted_iota(jnp.int32, (nb, tq), 1)
    crow = lax.broadcasted_iota(jnp.int32, (n_cmp, tq), 0)
    ccol = lax.broadcasted_iota(jnp.int32, (n_cmp, tq), 1)
    krow = lax.broadcasted_iota(jnp.int32, (kc, tq), 0)
    kcol = lax.broadcasted_iota(jnp.int32, (kc, tq), 1)
    wrow = lax.broadcasted_iota(jnp.int32, (wlen, tq), 0)
    wcol = lax.broadcasted_iota(jnp.int32, (wlen, tq), 1)
    kcm = kcmp_ref[0]
    vct = vct_ref[0, 0]
    ovt = ovt_ref[...]

    def per_head(x):
        return jnp.concatenate([x] * hpg, axis=1)

    def normalise(acc):
        return acc[0:dh] / acc[dh:dh + 1]

    def tile(i, n_past):
        t0 = pl.multiple_of(i * tq, tq)
        rows = pl.ds(t0, tq)
        qf = q_ref[0, rows, :].astype(F32)
        heads = []
        for hh in range(hpg):
            t = qf[:, (hh // 2) * LANES:(hh // 2 + 1) * LANES]
            if hh % 2 == 1:
                t = _swap_halves(t)
            heads.append(jnp.where(low, t, 0.0))
        qs = jnp.concatenate(heads, axis=0).astype(BF16)

        ws = pl.multiple_of(jnp.maximum(t0 - WIN, 0), tq)
        s_w = _nt(kwx_ref[0, pl.ds(ws, wlen), :], qs)
        wpos = wrow + ws
        tw = wcol + t0
        s_w = s_w + per_head(jnp.where((wpos <= tw) & (wpos > tw - WIN), 0.0, NEG))
        e_w = jnp.exp(s_w - jnp.max(s_w, axis=0, keepdims=True))
        o_win = normalise(_nn(vwt_ref[0, 0, :, pl.ds(ws, wlen)], e_w.astype(BF16)))

        s_c = _nt(kcm, qs)
        cmask = (crow * CMP_STRIDE + (CMP_LEN - 1)) <= (ccol + t0)
        p_all = []
        psum = jnp.zeros((n_cmp, tq), F32)
        for hh in range(hpg):
            sh = jnp.where(cmask, s_c[:, hh * tq:(hh + 1) * tq], NEG)
            e = jnp.exp(sh - jnp.max(sh, axis=0, keepdims=True))
            p = jnp.where(cmask, e / jnp.sum(e, axis=0, keepdims=True), 0.0)
            psum = psum + p
            p_all.append(p.astype(BF16))
        o_cmp = _nn(vct, jnp.concatenate(p_all, axis=1))

        p_hi = psum.astype(BF16)
        p_lo = (psum - p_hi.astype(F32)).astype(BF16)
        imp = _nn(ovt, p_hi) + _nn(ovt, p_lo)
        tcol = col + t0
        cur = tcol // SLC_LEN
        forced = (blk == 0) | (blk == cur) | (blk == cur - 1)
        bcausal = blk * SLC_LEN <= tcol
        imp = jnp.where(bcausal, jnp.where(forced, FORCE, imp), NEG)
        rank = jnp.zeros((nb, tq), F32)
        for j in range(nb):
            r = imp[j:j + 1, :]
            rank = rank + jnp.where(blk > j, jnp.where(r >= imp, 1.0, 0.0), jnp.where(r > imp, 1.0, 0.0))
        chosen = (rank < float(min(SLC_TOPK, nb))) & bcausal
        feat = jnp.concatenate([jnp.zeros((NSA_DH, tq), F32), jnp.where(chosen, 0.0, NEG),
                                jnp.zeros((LANES - NSA_DH - nb, tq), F32)], axis=0).astype(BF16)
        qbias = _nt(eye, feat)
        qsel = jnp.concatenate([hd + qbias for hd in heads], axis=0).astype(BF16)

        past = n_past * kc
        own = slice(past, past + kc)
        dbias = jnp.where((krow + past) <= (kcol + t0), 0.0, NEG)
        s_d = _nt(ksx_ref[0, own, :], qsel) + per_head(dbias)
        m = jnp.max(s_d, axis=0, keepdims=True)
        if n_past:
            s_p = _nt(ksx_ref[0, 0:past, :], qsel)
            m = jnp.maximum(m, jnp.max(s_p, axis=0, keepdims=True))
        acc_s = _nn(vst_ref[0, 0, :, own], jnp.exp(s_d - m).astype(BF16))
        if n_past:
            acc_s = acc_s + _nn(vst_ref[0, 0, :, 0:past], jnp.exp(s_p - m).astype(BF16))
        o_sel = normalise(acc_s)

        gt = gate_ref[0, 0, :, pl.ds(t0, tq)]
        outs = []
        for hh in range(hpg):
            hc = slice(hh * tq, (hh + 1) * tq)
            g_c = gt[0 * hpg + hh:0 * hpg + hh + 1, :]
            g_s = gt[1 * hpg + hh:1 * hpg + hh + 1, :]
            g_w = gt[2 * hpg + hh:2 * hpg + hh + 1, :]
            outs.append(g_c * o_cmp[:, hc] + g_s * o_sel[:, hc] + g_w * o_win[:, hc])
        o_ref[0, rows, :] = jnp.concatenate(outs, axis=0).T.astype(o_ref.dtype)

    tiles_per_chunk = kc // tq
    for n_past in range(seq // kc):
        def body(ii, carry, n_past=n_past):
            tile(n_past * tiles_per_chunk + ii, n_past)
            return carry
        lax.fori_loop(0, tiles_per_chunk, body, 0)


def _nsa_attention(nq, kvcmp, vct, ksx, vst, kwx, vwt, gate_t, ovt):
    b, s, _ = nq.shape
    n_cmp = kvcmp.shape[1]
    gw = NSA_HPG * NSA_DH
    per_group = lambda rows, width: pl.BlockSpec((1, rows, width), lambda bi, g: (bi, 0, g))
    per_group_t = lambda a: pl.BlockSpec((1, 1) + a.shape[2:], lambda bi, g: (bi, g, 0, 0))
    return pl.pallas_call(
        _nsa_kernel,
        grid=(b, NSA_GROUPS),
        in_specs=[per_group(s, gw),
                  per_group(n_cmp, LANES), per_group_t(vct),
                  per_group(s, LANES), per_group_t(vst), per_group(s, LANES), per_group_t(vwt),
                  per_group_t(gate_t),
                  pl.BlockSpec(ovt.shape, lambda bi, g: (0, 0))],
        out_specs=per_group(s, gw),
        out_shape=jax.ShapeDtypeStruct((b, s, NSA_HEADS * NSA_DH), BF16),
        compiler_params=pltpu.CompilerParams(
            dimension_semantics=("parallel", "parallel"), vmem_limit_bytes=VMEM_LIMIT),
        name="nsa_attn",
    )(nq, kvcmp, vct, ksx, vst, kwx, vwt, gate_t, ovt)


def _ffn_kernel(x_ref, yr_ref, yn_ref, mod_ref, wo_ref, g2_ref, gf_ref, wg_ref, wu_ref, wd_ref,
                o_ref, x1_ref, act_ref):
    half_w = yr_ref.shape[1]
    d_ff = wg_ref.shape[1]
    mix = _nn(yr_ref[...], wo_ref[0:half_w, :]) + _nn(yn_ref[...], wo_ref[half_w:2 * half_w, :])
    x1 = x_ref[...] + mod_ref[0, 2:3, :] * mix
    x1_ref[...] = x1
    y = x1 * lax.rsqrt(jnp.mean(x1 * x1, axis=-1, keepdims=True) + EPS) * g2_ref[...]
    h2 = (y * (1.0 + mod_ref[0, 4:5, :]) + mod_ref[0, 3:4, :]).astype(BF16)
    for j in range(d_ff // TF):
        sl = slice(j * TF, (j + 1) * TF)
        gate = _nn(h2, wg_ref[:, sl])
        up = _nn(h2, wu_ref[:, sl])
        act_ref[:, sl] = (gate * _sigmoid(gate) * up).astype(BF16)
    xo = x1_ref[...] + mod_ref[0, 5:6, :] * _nn(act_ref[...], wd_ref[...])
    o_ref[...] = xo * lax.rsqrt(jnp.mean(xo * xo, axis=-1, keepdims=True) + EPS) * gf_ref[...]


def _out_ffn(x2d, y_ret, y_nsa, mod, w_out, g2, gf, wg, wu, wd, seq):
    n, d = x2d.shape
    tm = TM_FF
    d_ff = wg.shape[1]
    tiles_per_seq = seq // tm
    half_w = y_ret.shape[1]
    row = lambda i: (i, 0)
    resident = lambda a: pl.BlockSpec(a.shape, lambda i: (0, 0), pipeline_mode=pl.Buffered(1))
    return pl.pallas_call(
        _ffn_kernel,
        grid=(n // tm,),
        in_specs=[pl.BlockSpec((tm, d), row),
                  pl.BlockSpec((tm, half_w), row),
                  pl.BlockSpec((tm, half_w), row),
                  pl.BlockSpec((1, 6, d), lambda i: (i // tiles_per_seq, 0, 0)),
                  resident(w_out),
                  pl.BlockSpec((1, d), lambda i: (0, 0)),
                  pl.BlockSpec((1, d), lambda i: (0, 0)),
                  resident(wg), resident(wu), resident(wd)],
        out_specs=pl.BlockSpec((tm, d), row),
        out_shape=jax.ShapeDtypeStruct((n, d), F32),
        scratch_shapes=[pltpu.VMEM((tm, d), F32), pltpu.VMEM((tm, d_ff), BF16)],
        compiler_params=pltpu.CompilerParams(
            dimension_semantics=("parallel",), vmem_limit_bytes=VMEM_LIMIT),
        name="out_ffn",
    )(x2d, y_ret, y_nsa, mod, w_out, g2.reshape(1, d), gf.reshape(1, d), wg, wu, wd)


def kernel(x, c, ln_mix_g, ln_ffn_g, w_ada, b_ada, w_in, cmp_pe_k, cmp_w1_k, cmp_w2_k,
           cmp_pe_v, cmp_w1_v, cmp_w2_v, w_out, w_ff_gate, w_ff_up, w_ff_down, ln_final_g):
    assert w_in.shape[0] == 1, "the final RMSNorm is fused into the (single) layer's FFN kernel"
    b, s, d = x.shape
    lane = np.arange(LANES)
    tabs = _rope_tables(np.arange(s), np.ones(LANES, bool))
    n_piece = s // CMP_STRIDE
    tabs_cmp = _rope_tables(np.arange(n_piece) * CMP_STRIDE + CMP_LEN - 1, lane < NSA_DH)

    mod = _ada(c, w_ada[0], b_ada[0]).reshape(b, 6, d)
    rq, rk, rv, rg_act, nq, kc, vc, ksx, vst, kwx, vwt, gate_t = _in_proj(
        x, ln_mix_g[0], mod, _prep_w_in(w_in[0]), tabs, _block_onehot_table(s))
    y_ret = _retention(rq, rk, rv, rg_act, _retention_tables())
    w1, w2, pe = _compress_weights(cmp_w1_k[0], cmp_w2_k[0], cmp_w1_v[0], cmp_w2_v[0],
                                   cmp_pe_k[0], cmp_pe_v[0])
    kvcmp, vct = _compress(kc, vc, w1, w2, pe, tabs_cmp)
    y_nsa = _nsa_attention(nq, kvcmp, vct, ksx, vst, kwx, vwt, gate_t, _overlap_t(s))
    out = _out_ffn(x.reshape(b * s, d), y_ret.reshape(b * s, -1), y_nsa.reshape(b * s, -1), mod,
                   w_out[0].astype(BF16), ln_ffn_g[0], ln_final_g,
                   w_ff_gate[0].astype(BF16), w_ff_up[0].astype(BF16), w_ff_down[0].astype(BF16), s)
    return out.reshape(b, s, d)
```

```python
import numpy as np
import jax
import jax.numpy as jnp
from jax import lax
from jax.experimental import pallas as pl
from jax.experimental.pallas import tpu as pltpu

F32 = jnp.float32
BF16 = jnp.bfloat16

D_MODEL = 1024
RET_HEADS = 4
RET_DK = 64
RET_DV = 128
RET_CHUNK = 128
NSA_HEADS = 8
NSA_GROUPS = 2
NSA_HPG = NSA_HEADS // NSA_GROUPS
NSA_DH = 64
CMP_LEN = 32
CMP_STRIDE = 16
CMP_HIDDEN = 128
SLC_LEN = 64
SLC_TOPK = 16
WIN = 512
D_FF = ((8 * D_MODEL + 3 * 256 - 1) // (3 * 256)) * 256
ROPE_THETA = 10000.0
EPS = 1e-6
NEG = -1e30
FORCE = 1e6

LANES = 128
HALF = NSA_DH // 2
ONES_ROWS = 16
GATE_ROWS = 16

TM_IN = 512
PROJ_W = 512
RET_GROUP = 4
Q_SCALE = NSA_DH ** -0.5 * float(np.log2(np.e))
TQ = 256
KC = 512
TM_FF = 512
TF = 256
VMEM_LIMIT = 56 * 1024 * 1024

_R_RQ, _R_RK, _R_RV, _R_RG, _R_NQ = 0, 256, 512, 1024, 1536
_R_KC, _R_VC, _R_KS, _R_VS, _R_KW, _R_VW, _R_GATE = 2048, 2176, 2304, 2432, 2560, 2688, 2816
IN_COLS_K = _R_GATE + NSA_GROUPS * LANES


def _sigmoid(x):
    return 1.0 / (1.0 + jnp.exp(-x))


def _nt(a, b):
    return lax.dot_general(a, b, (((1,), (1,)), ((), ())), preferred_element_type=F32)


def _nn(a, b):
    return jnp.dot(a, b, preferred_element_type=F32)


def _rope_tile(a, c, s1, s2):
    return a * c + pltpu.roll(a, HALF, 1) * s1 + pltpu.roll(a, LANES - HALF, 1) * s2


def _swap_halves(a):
    return pltpu.roll(a, LANES // 2, 1)


def _rope_tables(pos, rotary_lanes):
    pos = np.asarray(pos, np.float64)
    lane = np.arange(LANES)
    within = lane % NSA_DH
    freq = ROPE_THETA ** (-(within % HALF).astype(np.float64) / HALF)
    ang = pos[:, None] * freq[None, :]
    cos, sin = np.cos(ang), np.sin(ang)
    first = (within < HALF)[None, :]
    rot = np.asarray(rotary_lanes, bool)[None, :]
    c = np.where(rot, cos, 1.0)
    s1 = np.where(rot & ~first, sin, 0.0)
    s2 = np.where(rot & first, -sin, 0.0)
    return (jnp.asarray(c, F32), jnp.asarray(s1, F32), jnp.asarray(s2, F32))


def _block_onehot_table(seq):
    t = np.zeros((seq, LANES), np.float32)
    pos = np.arange(seq)
    t[pos, NSA_DH + pos // SLC_LEN] = 1.0
    return jnp.asarray(t)


def _retention_tables():
    h = jnp.arange(RET_HEADS, dtype=F32)
    log_g = jnp.log(1.0 - 2.0 ** (-5.0 - h))
    c = RET_CHUNK
    idx = jnp.arange(c, dtype=F32)
    diff = idx[:, None] - idx[None, :]
    causal = diff >= 0
    decay = jnp.where(causal, jnp.exp(log_g[:, None, None] * jnp.where(causal, diff, 0.0)), 0.0)
    zeta = jnp.exp(log_g[:, None] * (c - 1.0 - idx))
    xi = jnp.exp(log_g[:, None] * (idx + 1.0))
    g_chunk = jnp.exp(log_g * c)

    def pair_lanes(t):
        t = t.reshape(RET_HEADS // 2, 2, c)
        return jnp.repeat(jnp.transpose(t, (0, 2, 1)), RET_DK, axis=2)

    g_b = jnp.broadcast_to(g_chunk[:, None, None], (RET_HEADS, 1, LANES))
    return decay, pair_lanes(zeta), pair_lanes(xi), g_b


def _overlap_t(seq):
    n_c = seq // CMP_STRIDE - CMP_LEN // CMP_STRIDE + 1
    nb = seq // SLC_LEN
    cs = np.arange(n_c) * CMP_STRIDE
    bs = np.arange(nb) * SLC_LEN
    ov = np.maximum(np.minimum(cs[:, None] + CMP_LEN, bs[None] + SLC_LEN)
                    - np.maximum(cs[:, None], bs[None]), 0).astype(np.float64) / CMP_LEN
    ncp = seq // CMP_STRIDE
    ovp = np.zeros((ncp, nb))
    ovp[:n_c] = ov
    return jnp.asarray(ovp.T, BF16)


def _ada_kernel(c_ref, w_ref, b_ref, o_ref):
    c = c_ref[...]
    o_ref[...] = _nn(c * _sigmoid(c), w_ref[...]) + b_ref[...]


def _ada(c, w, b):
    bsz, d = c.shape
    n = w.shape[1]
    tn = 1024
    return pl.pallas_call(
        _ada_kernel,
        grid=(n // tn,),
        in_specs=[pl.BlockSpec((bsz, d), lambda j: (0, 0)),
                  pl.BlockSpec((d, tn), lambda j: (0, j)),
                  pl.BlockSpec((1, tn), lambda j: (0, j))],
        out_specs=pl.BlockSpec((bsz, tn), lambda j: (0, j)),
        out_shape=jax.ShapeDtypeStruct((bsz, n), F32),
        compiler_params=pltpu.CompilerParams(vmem_limit_bytes=VMEM_LIMIT),
        name="ada",
    )(c, w, b.reshape(1, n))


def _inproj_kernel(x_ref, g_ref, mod_ref, w_ref, c_ref, s1_ref, s2_ref, hot_ref,
                   rq_ref, rk_ref, rv_ref, rg_ref, nq_ref, kc_ref, vc_ref,
                   ksx_ref, vst_ref, kwx_ref, vwt_ref, gate_ref):
    x = x_ref[0]
    y = x * lax.rsqrt(jnp.mean(x * x, axis=-1, keepdims=True) + EPS) * g_ref[...]
    h = y * (1.0 + mod_ref[0, 1:2, :]) + mod_ref[0, 0:1, :]
    hb = h.astype(BF16)

    def proj(c0):
        a = _nn(hb, w_ref[:, c0:c0 + PROJ_W])
        return [a[:, t * LANES:(t + 1) * LANES] for t in range(PROJ_W // LANES)]

    tabs = (c_ref[...], s1_ref[...], s2_ref[...])

    def roped(tiles, scale, out_ref):
        for t, a in enumerate(tiles):
            r = _rope_tile(a, *tabs)
            if scale != 1.0:
                r = r * scale
            out_ref[0, :, t * LANES:(t + 1) * LANES] = r.astype(out_ref.dtype)

    ret_qk = proj(_R_RQ)
    roped(ret_qk[0:2], 1.0, rq_ref)
    roped(ret_qk[2:4], RET_DK ** -0.5, rk_ref)
    for t, a in enumerate(proj(_R_RV)):
        rv_ref[0, :, t * LANES:(t + 1) * LANES] = a.astype(BF16)
    for t, a in enumerate(proj(_R_RG)):
        rg_ref[0, :, t * LANES:(t + 1) * LANES] = (a * _sigmoid(a)).astype(BF16)
    roped(proj(_R_NQ), Q_SCALE, nq_ref)
    kc_t, vc_t, ks_t, vs_t = proj(_R_KC)
    kw_t, vw_t, gate0_t, gate1_t = proj(_R_KW)
    kc_ref[0] = kc_t
    vc_ref[0] = vc_t

    low = lax.broadcasted_iota(jnp.int32, (x.shape[0], LANES), 1) < NSA_DH

    def per_group(tile, fill, out_ref):
        out_ref[0, :, 0:LANES] = jnp.where(low, tile, fill).astype(out_ref.dtype)
        out_ref[0, :, LANES:2 * LANES] = jnp.where(low, _swap_halves(tile), fill).astype(out_ref.dtype)

    def per_group_t(tile, out_ref):
        t = tile.T
        ones = jnp.ones((ONES_ROWS, t.shape[1]), out_ref.dtype)
        for g in range(NSA_GROUPS):
            out_ref[0, g, 0:NSA_DH, :] = t[g * NSA_DH:(g + 1) * NSA_DH].astype(out_ref.dtype)
            out_ref[0, g, NSA_DH:NSA_DH + ONES_ROWS, :] = ones

    per_group(_rope_tile(ks_t, *tabs), hot_ref[...], ksx_ref)
    per_group_t(vs_t, vst_ref)
    per_group(_rope_tile(kw_t, *tabs), 0.0, kwx_ref)
    per_group_t(vw_t, vwt_ref)
    for g, gate_t in enumerate((gate0_t, gate1_t)):
        gate_ref[0, g] = _sigmoid(gate_t.T[0:GATE_ROWS])


def _prep_w_in(w_in):
    d = w_in.shape[0]
    cols = [w_in[:, :_R_GATE]]
    for g in range(NSA_GROUPS):
        for r in range(3):
            c0 = _R_GATE + r * NSA_HEADS + g * NSA_HPG
            cols.append(w_in[:, c0:c0 + NSA_HPG])
        cols.append(jnp.zeros((d, LANES - 3 * NSA_HPG), w_in.dtype))
    return jnp.concatenate(cols, axis=1).astype(BF16)


def _in_proj(x, ln_g, mod, w_prep, tabs, hot):
    b, s, d = x.shape
    tm = TM_IN
    grid = (b, s // tm)
    tab_spec = pl.BlockSpec((tm, LANES), lambda bi, j: (j, 0))

    def out(n, dtype):
        return (jax.ShapeDtypeStruct((b, s, n), dtype), pl.BlockSpec((1, tm, n), lambda bi, j: (bi, j, 0)))

    def out_t(rows, dtype):
        return (jax.ShapeDtypeStruct((b, NSA_GROUPS, rows, s), dtype),
                pl.BlockSpec((1, NSA_GROUPS, rows, tm), lambda bi, j: (bi, 0, 0, j)))

    vt_rows = NSA_DH + ONES_ROWS
    outs = [out(256, BF16), out(256, BF16), out(512, BF16), out(512, BF16), out(512, BF16),
            out(LANES, F32), out(LANES, F32),
            out(256, BF16), out_t(vt_rows, BF16), out(256, BF16), out_t(vt_rows, BF16),
            out_t(GATE_ROWS, F32)]
    return pl.pallas_call(
        _inproj_kernel,
        grid=grid,
        in_specs=[pl.BlockSpec((1, tm, d), lambda bi, j: (bi, j, 0)),
                  pl.BlockSpec((1, d), lambda bi, j: (0, 0)),
                  pl.BlockSpec((1, 6, d), lambda bi, j: (bi, 0, 0)),
                  pl.BlockSpec((d, IN_COLS_K), lambda bi, j: (0, 0)),
                  tab_spec, tab_spec, tab_spec, tab_spec],
        out_specs=[o[1] for o in outs],
        out_shape=[o[0] for o in outs],
        compiler_params=pltpu.CompilerParams(
            dimension_semantics=("parallel", "parallel"), vmem_limit_bytes=VMEM_LIMIT),
        name="in_proj",
    )(x, ln_g.reshape(1, d), mod, w_prep, *tabs, hot)


def _ret_kernel(q_ref, k_ref, v_ref, rg_ref, dec_ref, zeta_ref, xi_ref, gch_ref, o_ref, kv_ref, prev_ref):
    c = RET_CHUNK
    n_chunks = q_ref.shape[1] // c
    low = lax.broadcasted_iota(jnp.int32, (c, LANES), 1) < RET_DK

    def chunk_rows(n):
        return pl.ds(pl.multiple_of(n * c, c), c)

    def head_cols(h):
        return slice(h * RET_DV, (h + 1) * RET_DV)

    def kv_body(it, carry):
        kz_t = {}
        for j in range(RET_GROUP):
            rows = chunk_rows(it * RET_GROUP + j)
            for p in range(RET_HEADS // 2):
                pair = slice(p * LANES, (p + 1) * LANES)
                kz_t[j, p] = (k_ref[0, rows, pair].astype(F32) * zeta_ref[p]).T.astype(BF16)
        for j in range(RET_GROUP):
            n = it * RET_GROUP + j
            for h in range(RET_HEADS):
                kv_ref[h, n] = _nn(kz_t[j, h // 2], v_ref[0, chunk_rows(n), head_cols(h)])
        return carry

    lax.fori_loop(0, n_chunks // RET_GROUP, kv_body, 0)

    for h in range(RET_HEADS):
        def scan_body(n, st, h=h):
            prev_ref[h, n] = st.astype(prev_ref.dtype)
            return st * gch_ref[h] + kv_ref[h, n]
        lax.fori_loop(0, n_chunks, scan_body, jnp.zeros((LANES, RET_DV), F32))

    def out_body(it, carry):
        chains = [(j, h) for j in range(RET_GROUP) for h in range(RET_HEADS)]
        chunk = lambda j: it * RET_GROUP + j
        q_own, qx_own, att, ys = {}, {}, {}, {}
        for j, h in chains:
            p, e = divmod(h, 2)
            pair = slice(p * LANES, (p + 1) * LANES)
            mine = low if e == 0 else jnp.logical_not(low)
            q2 = q_ref[0, chunk_rows(chunk(j)), pair].astype(F32)
            q_own[j, h] = jnp.where(mine, q2, 0.0).astype(BF16)
            qx_own[j, h] = jnp.where(mine, q2 * xi_ref[p], 0.0).astype(BF16)
        for j, h in chains:
            pair = slice((h // 2) * LANES, (h // 2 + 1) * LANES)
            att[j, h] = _nt(q_own[j, h], k_ref[0, chunk_rows(chunk(j)), pair])
        for j, h in chains:
            lhs = jnp.concatenate([(att[j, h] * dec_ref[h]).astype(BF16), qx_own[j, h]], axis=1)
            rhs = jnp.concatenate([v_ref[0, chunk_rows(chunk(j)), head_cols(h)], prev_ref[h, chunk(j)]], axis=0)
            ys[j, h] = _nn(lhs, rhs)
        for j, h in chains:
            y = ys[j, h]
            yn = y * lax.rsqrt(jnp.mean(y * y, axis=-1, keepdims=True) + EPS)
            gate = rg_ref[0, chunk_rows(chunk(j)), head_cols(h)].astype(F32)
            o_ref[0, chunk_rows(chunk(j)), head_cols(h)] = (yn * gate).astype(o_ref.dtype)
        return carry

    lax.fori_loop(0, n_chunks // RET_GROUP, out_body, 0)


def _retention(rq, rk, rv, rg_act, tables):
    b, s, _ = rq.shape
    decay, zeta_p, xi_p, g_b = tables
    whole = lambda a: pl.BlockSpec(a.shape, lambda bi: (0,) * a.ndim)
    row = lambda a: pl.BlockSpec((1,) + a.shape[1:], lambda bi: (bi, 0, 0))
    return pl.pallas_call(
        _ret_kernel,
        grid=(b,),
        in_specs=[row(rq), row(rk), row(rv), row(rg_act),
                  whole(decay), whole(zeta_p), whole(xi_p), whole(g_b)],
        out_specs=row(rv),
        out_shape=jax.ShapeDtypeStruct(rv.shape, BF16),
        scratch_shapes=[pltpu.VMEM((RET_HEADS, s // RET_CHUNK, LANES, RET_DV), F32),
                        pltpu.VMEM((RET_HEADS, s // RET_CHUNK, LANES, RET_DV), BF16)],
        compiler_params=pltpu.CompilerParams(
            dimension_semantics=("parallel",), vmem_limit_bytes=VMEM_LIMIT),
        name="retention",
    )(rq, rk, rv, rg_act, decay, zeta_p, xi_p, g_b)


def _cmp_kernel(kc_ref, vc_ref, w1_ref, pe_ref, w2_ref, c_ref, s1_ref, s2_ref, kv_ref, vt_ref):
    n_piece = kc_ref.shape[1] // CMP_STRIDE
    hid_w = w1_ref.shape[3]
    out = jnp.zeros((n_piece, kv_ref.shape[2]), F32)
    for t, src in enumerate((kc_ref, vc_ref)):
        a_lo = jnp.zeros((n_piece, hid_w), F32)
        a_hi = jnp.zeros((n_piece, hid_w), F32)
        for r in range(CMP_STRIDE):
            xr = src[0, pl.ds(r, n_piece, stride=CMP_STRIDE), :]
            lo, hi = r, CMP_STRIDE + r
            a_lo = a_lo + _nn((xr + pe_ref[t, lo:lo + 1, :]).astype(BF16), w1_ref[t, lo])
            a_hi = a_hi + _nn((xr + pe_ref[t, hi:hi + 1, :]).astype(BF16), w1_ref[t, hi])
        hid = a_lo + pltpu.roll(a_hi, n_piece - 1, 0)
        out = out + _nn((hid * _sigmoid(hid)).astype(BF16), w2_ref[t])
    for t in range(out.shape[1] // LANES):
        sl = slice(t * LANES, (t + 1) * LANES)
        kv = _rope_tile(out[:, sl], c_ref[...], s1_ref[...], s2_ref[...])
        kv_ref[0, :, sl] = kv.astype(kv_ref.dtype)
        vt_ref[0, t] = kv.T[NSA_DH:2 * NSA_DH].astype(vt_ref.dtype)


def _compress_weights(w1_k, w2_k, w1_v, w2_v, pe_k, pe_v):
    dh, hid = NSA_DH, CMP_HIDDEN

    def first(w1):
        w = w1.reshape(CMP_LEN, dh, hid)
        z = jnp.zeros_like(w)
        return jnp.concatenate([jnp.concatenate([w, z], axis=2), jnp.concatenate([z, w], axis=2)], axis=1)

    def second(w2, off):
        z = jnp.zeros_like(w2)
        rows = []
        for g in range(NSA_GROUPS):
            c = [z, z, z, z]
            c[2 * g + off] = w2
            rows.append(jnp.concatenate(c, axis=1))
        return jnp.concatenate(rows, axis=0)

    w1 = jnp.stack([first(w1_k), first(w1_v)]).astype(BF16)
    w2 = jnp.stack([second(w2_k, 0), second(w2_v, 1)]).astype(BF16)
    pe = jnp.stack([jnp.concatenate([pe_k, pe_k], axis=1), jnp.concatenate([pe_v, pe_v], axis=1)])
    return w1, w2, pe


def _compress(kc, vc, w1, w2, pe, tabs_cmp):
    b, s, w = kc.shape
    n_piece = s // CMP_STRIDE
    const2 = lambda bi: (0, 0)
    src = pl.BlockSpec((1, s, w), lambda bi: (bi, 0, 0))
    dst = pl.BlockSpec((1, n_piece, 2 * w), lambda bi: (bi, 0, 0))
    shape = jax.ShapeDtypeStruct((b, n_piece, 2 * w), BF16)
    return pl.pallas_call(
        _cmp_kernel,
        grid=(b,),
        in_specs=[src, src,
                  pl.BlockSpec(w1.shape, lambda bi: (0, 0, 0, 0)),
                  pl.BlockSpec(pe.shape, lambda bi: (0, 0, 0)),
                  pl.BlockSpec(w2.shape, lambda bi: (0, 0, 0)),
                  pl.BlockSpec((n_piece, LANES), const2),
                  pl.BlockSpec((n_piece, LANES), const2),
                  pl.BlockSpec((n_piece, LANES), const2)],
        out_specs=[dst, pl.BlockSpec((1, NSA_GROUPS, NSA_DH, n_piece), lambda bi: (bi, 0, 0, 0))],
        out_shape=[shape, jax.ShapeDtypeStruct((b, NSA_GROUPS, NSA_DH, n_piece), BF16)],
        compiler_params=pltpu.CompilerParams(
            dimension_semantics=("parallel",), vmem_limit_bytes=VMEM_LIMIT),
        name="compress",
    )(kc, vc, w1, pe, w2, *tabs_cmp)


def _nsa_kernel(q_ref, kcmp_ref, vct_ref, ksx_ref, vst_ref, kwx_ref, vwt_ref, gate_ref, ovt_ref, o_ref):
    tq, kc = TQ, KC
    seq = q_ref.shape[1]
    nb = seq // SLC_LEN
    n_cmp = kcmp_ref.shape[1]
    hpg, dh = NSA_HPG, NSA_DH
    wlen = WIN + tq

    low = lax.broadcasted_iota(jnp.int32, (tq, LANES), 1) < dh
    eye = jnp.where(lax.broadcasted_iota(jnp.int32, (tq, tq), 0)
                    == lax.broadcasted_iota(jnp.int32, (tq, tq), 1), 1.0, 0.0).astype(BF16)
    blk = lax.broadcasted_iota(jnp.int32, (nb, tq), 0)
    col = lax.broadcasted_iota(jnp.int32, (nb, tq), 1)
    crow = lax.broadcasted_iota(jnp.int32, (n_cmp, tq), 0)
    ccol = lax.broadcasted_iota(jnp.int32, (n_cmp, tq), 1)
    krow = lax.broadcasted_iota(jnp.int32, (kc, tq), 0)
    kcol = lax.broadcasted_iota(jnp.int32, (kc, tq), 1)
    wrow = lax.broadcasted_iota(jnp.int32, (wlen, tq), 0)
    wcol = lax.broadcasted_iota(jnp.int32, (wlen, tq), 1)
    kcm = kcmp_ref[0]
    vct = vct_ref[0, 0]
    ovt = ovt_ref[...]

    def per_head(x):
        return jnp.concatenate([x] * hpg, axis=1)

    def normalise(acc):
        return acc[0:dh] / acc[dh:dh + 1]

    def tile(i, n_past):
        t0 = pl.multiple_of(i * tq, tq)
        rows = pl.ds(t0, tq)
        qf = q_ref[0, rows, :].astype(F32)
        heads = []
        for hh in range(hpg):
            t = qf[:, (hh // 2) * LANES:(hh // 2 + 1) * LANES]
            if hh % 2 == 1:
                t = _swap_halves(t)
            heads.append(jnp.where(low, t, 0.0))
        qs = jnp.concatenate(heads, axis=0).astype(BF16)

        ws = pl.multiple_of(jnp.maximum(t0 - WIN, 0), tq)
        s_w = _nt(kwx_ref[0, pl.ds(ws, wlen), :], qs)
        wpos = wrow + ws
        tw = wcol + t0
        s_w = s_w + per_head(jnp.where((wpos <= tw) & (wpos > tw - WIN), 0.0, NEG))
        e_w = jnp.exp2(s_w - jnp.max(s_w, axis=0, keepdims=True))
        o_win = normalise(_nn(vwt_ref[0, 0, :, pl.ds(ws, wlen)], e_w.astype(BF16)))

        s_c = _nt(kcm, qs)
        cmask = (crow * CMP_STRIDE + (CMP_LEN - 1)) <= (ccol + t0)
        p_all = []
        psum = jnp.zeros((n_cmp, tq), F32)
        for hh in range(hpg):
            sh = jnp.where(cmask, s_c[:, hh * tq:(hh + 1) * tq], NEG)
            e = jnp.exp2(sh - jnp.max(sh, axis=0, keepdims=True))
            p = jnp.where(cmask, e / jnp.sum(e, axis=0, keepdims=True), 0.0)
            psum = psum + p
            p_all.append(p.astype(BF16))
        o_cmp = _nn(vct, jnp.concatenate(p_all, axis=1))

        p_hi = psum.astype(BF16)
        p_lo = (psum - p_hi.astype(F32)).astype(BF16)
        imp = _nn(ovt, p_hi) + _nn(ovt, p_lo)
        tcol = col + t0
        cur = tcol // SLC_LEN
        forced = (blk == 0) | (blk == cur) | (blk == cur - 1)
        bcausal = blk * SLC_LEN <= tcol
        imp = jnp.where(bcausal, jnp.where(forced, FORCE, imp), NEG)
        rank = jnp.zeros((nb, tq), F32)
        for j in range(nb):
            r = imp[j:j + 1, :]
            rank = rank + jnp.where(blk > j, jnp.where(r >= imp, 1.0, 0.0), jnp.where(r > imp, 1.0, 0.0))
        chosen = (rank < float(min(SLC_TOPK, nb))) & bcausal
        feat = jnp.concatenate([jnp.zeros((NSA_DH, tq), F32), jnp.where(chosen, 0.0, NEG),
                                jnp.zeros((LANES - NSA_DH - nb, tq), F32)], axis=0).astype(BF16)
        qbias = _nt(eye, feat)
        qsel = jnp.concatenate([hd + qbias for hd in heads], axis=0).astype(BF16)

        past = n_past * kc
        own = slice(past, past + kc)
        dbias = jnp.where((krow + past) <= (kcol + t0), 0.0, NEG)
        s_d = _nt(ksx_ref[0, own, :], qsel) + per_head(dbias)
        m = jnp.max(s_d, axis=0, keepdims=True)
        if n_past:
            s_p = _nt(ksx_ref[0, 0:past, :], qsel)
            m = jnp.maximum(m, jnp.max(s_p, axis=0, keepdims=True))
        acc_s = _nn(vst_ref[0, 0, :, own], jnp.exp2(s_d - m).astype(BF16))
        if n_past:
            acc_s = acc_s + _nn(vst_ref[0, 0, :, 0:past], jnp.exp2(s_p - m).astype(BF16))
        o_sel = normalise(acc_s)

        gt = gate_ref[0, 0, :, pl.ds(t0, tq)]
        outs = []
        for hh in range(hpg):
            hc = slice(hh * tq, (hh + 1) * tq)
            g_c = gt[0 * hpg + hh:0 * hpg + hh + 1, :]
            g_s = gt[1 * hpg + hh:1 * hpg + hh + 1, :]
            g_w = gt[2 * hpg + hh:2 * hpg + hh + 1, :]
            outs.append(g_c * o_cmp[:, hc] + g_s * o_sel[:, hc] + g_w * o_win[:, hc])
        o_ref[0, rows, :] = jnp.concatenate(outs, axis=0).T.astype(o_ref.dtype)

    tiles_per_chunk = kc // tq
    for n_past in range(seq // kc):
        def body(ii, carry, n_past=n_past):
            tile(n_past * tiles_per_chunk + ii, n_past)
            return carry
        lax.fori_loop(0, tiles_per_chunk, body, 0)


def _nsa_attention(nq, kvcmp, vct, ksx, vst, kwx, vwt, gate_t, ovt):
    b, s, _ = nq.shape
    n_cmp = kvcmp.shape[1]
    gw = NSA_HPG * NSA_DH
    per_group = lambda rows, width: pl.BlockSpec((1, rows, width), lambda bi, g: (bi, 0, g))
    per_group_t = lambda a: pl.BlockSpec((1, 1) + a.shape[2:], lambda bi, g: (bi, g, 0, 0))
    return pl.pallas_call(
        _nsa_kernel,
        grid=(b, NSA_GROUPS),
        in_specs=[per_group(s, gw),
                  per_group(n_cmp, LANES), per_group_t(vct),
                  per_group(s, LANES), per_group_t(vst), per_group(s, LANES), per_group_t(vwt),
                  per_group_t(gate_t),
                  pl.BlockSpec(ovt.shape, lambda bi, g: (0, 0))],
        out_specs=per_group(s, gw),
        out_shape=jax.ShapeDtypeStruct((b, s, NSA_HEADS * NSA_DH), BF16),
        compiler_params=pltpu.CompilerParams(
            dimension_semantics=("parallel", "parallel"), vmem_limit_bytes=VMEM_LIMIT),
        name="nsa_attn",
    )(nq, kvcmp, vct, ksx, vst, kwx, vwt, gate_t, ovt)


def _ffn_kernel(x_ref, yr_ref, yn_ref, mod_ref, wo_ref, g2_ref, gf_ref, wg_ref, wu_ref, wd_ref,
                o_ref, x1_ref, act_ref):
    half_w = yr_ref.shape[1]
    d_ff = wg_ref.shape[1]
    mix = _nn(yr_ref[...], wo_ref[0:half_w, :]) + _nn(yn_ref[...], wo_ref[half_w:2 * half_w, :])
    x1 = x_ref[...] + mod_ref[0, 2:3, :] * mix
    x1_ref[...] = x1
    y = x1 * lax.rsqrt(jnp.mean(x1 * x1, axis=-1, keepdims=True) + EPS) * g2_ref[...]
    h2 = (y * (1.0 + mod_ref[0, 4:5, :]) + mod_ref[0, 3:4, :]).astype(BF16)
    for j in range(d_ff // TF):
        sl = slice(j * TF, (j + 1) * TF)
        gate = _nn(h2, wg_ref[:, sl])
        up = _nn(h2, wu_ref[:, sl])
        act_ref[:, sl] = (gate * _sigmoid(gate) * up).astype(BF16)
    xo = x1_ref[...] + mod_ref[0, 5:6, :] * _nn(act_ref[...], wd_ref[...])
    o_ref[...] = xo * lax.rsqrt(jnp.mean(xo * xo, axis=-1, keepdims=True) + EPS) * gf_ref[...]


def _out_ffn(x2d, y_ret, y_nsa, mod, w_out, g2, gf, wg, wu, wd, seq):
    n, d = x2d.shape
    tm = TM_FF
    d_ff = wg.shape[1]
    tiles_per_seq = seq // tm
    half_w = y_ret.shape[1]
    row = lambda i: (i, 0)
    resident = lambda a: pl.BlockSpec(a.shape, lambda i: (0, 0), pipeline_mode=pl.Buffered(1))
    return pl.pallas_call(
        _ffn_kernel,
        grid=(n // tm,),
        in_specs=[pl.BlockSpec((tm, d), row),
                  pl.BlockSpec((tm, half_w), row),
                  pl.BlockSpec((tm, half_w), row),
                  pl.BlockSpec((1, 6, d), lambda i: (i // tiles_per_seq, 0, 0)),
                  resident(w_out),
                  pl.BlockSpec((1, d), lambda i: (0, 0)),
                  pl.BlockSpec((1, d), lambda i: (0, 0)),
                  resident(wg), resident(wu), resident(wd)],
        out_specs=pl.BlockSpec((tm, d), row),
        out_shape=jax.ShapeDtypeStruct((n, d), F32),
        scratch_shapes=[pltpu.VMEM((tm, d), F32), pltpu.VMEM((tm, d_ff), BF16)],
        compiler_params=pltpu.CompilerParams(
            dimension_semantics=("parallel",), vmem_limit_bytes=VMEM_LIMIT),
        name="out_ffn",
    )(x2d, y_ret, y_nsa, mod, w_out, g2.reshape(1, d), gf.reshape(1, d), wg, wu, wd)


def kernel(x, c, ln_mix_g, ln_ffn_g, w_ada, b_ada, w_in, cmp_pe_k, cmp_w1_k, cmp_w2_k,
           cmp_pe_v, cmp_w1_v, cmp_w2_v, w_out, w_ff_gate, w_ff_up, w_ff_down, ln_final_g):
    assert w_in.shape[0] == 1, "the final RMSNorm is fused into the (single) layer's FFN kernel"
    b, s, d = x.shape
    lane = np.arange(LANES)
    tabs = _rope_tables(np.arange(s), np.ones(LANES, bool))
    n_piece = s // CMP_STRIDE
    tabs_cmp = _rope_tables(np.arange(n_piece) * CMP_STRIDE + CMP_LEN - 1, lane < NSA_DH)

    mod = _ada(c, w_ada[0], b_ada[0]).reshape(b, 6, d)
    rq, rk, rv, rg_act, nq, kc, vc, ksx, vst, kwx, vwt, gate_t = _in_proj(
        x, ln_mix_g[0], mod, _prep_w_in(w_in[0]), tabs, _block_onehot_table(s))
    y_ret = _retention(rq, rk, rv, rg_act, _retention_tables())
    w1, w2, pe = _compress_weights(cmp_w1_k[0], cmp_w2_k[0], cmp_w1_v[0], cmp_w2_v[0],
                                   cmp_pe_k[0], cmp_pe_v[0])
    kvcmp, vct = _compress(kc, vc, w1, w2, pe, tabs_cmp)
    y_nsa = _nsa_attention(nq, kvcmp, vct, ksx, vst, kwx, vwt, gate_t, _overlap_t(s))
    out = _out_ffn(x.reshape(b * s, d), y_ret.reshape(b * s, -1), y_nsa.reshape(b * s, -1), mod,
                   w_out[0].astype(BF16), ln_ffn_g[0], ln_final_g,
                   w_ff_gate[0].astype(BF16), w_ff_up[0].astype(BF16), w_ff_down[0].astype(BF16), s)
    return out.reshape(b, s, d)
```

```python
import numpy as np
import jax
import jax.numpy as jnp
from jax import lax
from jax.experimental import pallas as pl
from jax.experimental.pallas import tpu as pltpu

F32 = jnp.float32
BF16 = jnp.bfloat16

D_MODEL = 1024
RET_HEADS = 4
RET_DK = 64
RET_DV = 128
RET_CHUNK = 128
NSA_HEADS = 8
NSA_GROUPS = 2
NSA_HPG = NSA_HEADS // NSA_GROUPS
NSA_DH = 64
CMP_LEN = 32
CMP_STRIDE = 16
CMP_HIDDEN = 128
SLC_LEN = 64
SLC_TOPK = 16
WIN = 512
D_FF = ((8 * D_MODEL + 3 * 256 - 1) // (3 * 256)) * 256
ROPE_THETA = 10000.0
EPS = 1e-6
NEG = -1e30
FORCE = 1e6

LANES = 128
SUBLANES = 8
HALF = NSA_DH // 2
ONES_ROWS = 16
GATE_ROWS = 16

TM_IN = 512
PROJ_W = 512
RET_GROUP = 4
Q_SCALE = NSA_DH ** -0.5 * float(np.log2(np.e))
TQ = 256
TILE_GROUP = 2
TM_FF = 512
TF = 256
VMEM_LIMIT = 56 * 1024 * 1024

_R_RQ, _R_RK, _R_RV, _R_RG, _R_NQ = 0, 256, 512, 1024, 1536
_R_KC, _R_VC, _R_KS, _R_VS, _R_KW, _R_VW, _R_GATE = 2048, 2176, 2304, 2432, 2560, 2688, 2816
IN_COLS_K = _R_GATE + NSA_GROUPS * LANES


def _sigmoid(x):
    return 1.0 / (1.0 + jnp.exp(-x))


def _nt(a, b):
    return lax.dot_general(a, b, (((1,), (1,)), ((), ())), preferred_element_type=F32)


def _nn(a, b):
    return jnp.dot(a, b, preferred_element_type=F32)


def _rope_tile(a, c, s1, s2):
    return a * c + pltpu.roll(a, HALF, 1) * s1 + pltpu.roll(a, LANES - HALF, 1) * s2


def _swap_halves(a):
    return pltpu.roll(a, LANES // 2, 1)


def _rope_tables(pos, rotary_lanes):
    pos = np.asarray(pos, np.float64)
    lane = np.arange(LANES)
    within = lane % NSA_DH
    freq = ROPE_THETA ** (-(within % HALF).astype(np.float64) / HALF)
    ang = pos[:, None] * freq[None, :]
    cos, sin = np.cos(ang), np.sin(ang)
    first = (within < HALF)[None, :]
    rot = np.asarray(rotary_lanes, bool)[None, :]
    c = np.where(rot, cos, 1.0)
    s1 = np.where(rot & ~first, sin, 0.0)
    s2 = np.where(rot & first, -sin, 0.0)
    return (jnp.asarray(c, F32), jnp.asarray(s1, F32), jnp.asarray(s2, F32))


def _block_onehot_table(seq):
    t = np.zeros((seq, LANES), np.float32)
    pos = np.arange(seq)
    t[pos, NSA_DH + pos // SLC_LEN] = 1.0
    return jnp.asarray(t)


def _retention_tables():
    h = jnp.arange(RET_HEADS, dtype=F32)
    log_g = jnp.log(1.0 - 2.0 ** (-5.0 - h))
    c = RET_CHUNK
    idx = jnp.arange(c, dtype=F32)
    diff = idx[:, None] - idx[None, :]
    causal = diff >= 0
    decay = jnp.where(causal, jnp.exp(log_g[:, None, None] * jnp.where(causal, diff, 0.0)), 0.0)
    zeta = jnp.exp(log_g[:, None] * (c - 1.0 - idx))
    xi = jnp.exp(log_g[:, None] * (idx + 1.0))
    g_chunk = jnp.exp(log_g * c)

    def pair_lanes(t):
        t = t.reshape(RET_HEADS // 2, 2, c)
        return jnp.repeat(jnp.transpose(t, (0, 2, 1)), RET_DK, axis=2)

    g_b = jnp.broadcast_to(g_chunk[:, None, None], (RET_HEADS, 1, LANES))
    return decay, pair_lanes(zeta), pair_lanes(xi), g_b


def _overlap_t(seq):
    n_c = seq // CMP_STRIDE - CMP_LEN // CMP_STRIDE + 1
    nb = seq // SLC_LEN
    cs = np.arange(n_c) * CMP_STRIDE
    bs = np.arange(nb) * SLC_LEN
    ov = np.maximum(np.minimum(cs[:, None] + CMP_LEN, bs[None] + SLC_LEN)
                    - np.maximum(cs[:, None], bs[None]), 0).astype(np.float64) / CMP_LEN
    ncp = seq // CMP_STRIDE
    ovp = np.zeros((ncp, nb))
    ovp[:n_c] = ov
    return jnp.asarray(ovp.T, BF16)


def _ada_kernel(c_ref, w_ref, b_ref, o_ref):
    c = c_ref[...]
    o_ref[...] = _nn(c * _sigmoid(c), w_ref[...]) + b_ref[...]


def _ada(c, w, b):
    bsz, d = c.shape
    n = w.shape[1]
    tn = 1024
    return pl.pallas_call(
        _ada_kernel,
        grid=(n // tn,),
        in_specs=[pl.BlockSpec((bsz, d), lambda j: (0, 0)),
                  pl.BlockSpec((d, tn), lambda j: (0, j)),
                  pl.BlockSpec((1, tn), lambda j: (0, j))],
        out_specs=pl.BlockSpec((bsz, tn), lambda j: (0, j)),
        out_shape=jax.ShapeDtypeStruct((bsz, n), F32),
        compiler_params=pltpu.CompilerParams(vmem_limit_bytes=VMEM_LIMIT),
        name="ada",
    )(c, w, b.reshape(1, n))


def _inproj_kernel(x_ref, g_ref, mod_ref, w_ref, c_ref, s1_ref, s2_ref, hot_ref,
                   rq_ref, rk_ref, rv_ref, rg_ref, nq_ref, kc_ref, vc_ref,
                   ksx_ref, vst_ref, kwx_ref, vwt_ref, gate_ref):
    x = x_ref[0]
    y = x * lax.rsqrt(jnp.mean(x * x, axis=-1, keepdims=True) + EPS) * g_ref[...]
    h = y * (1.0 + mod_ref[0, 1:2, :]) + mod_ref[0, 0:1, :]
    hb = h.astype(BF16)

    def proj(c0):
        a = _nn(hb, w_ref[:, c0:c0 + PROJ_W])
        return [a[:, t * LANES:(t + 1) * LANES] for t in range(PROJ_W // LANES)]

    tabs = (c_ref[...], s1_ref[...], s2_ref[...])

    def roped(tiles, scale, out_ref):
        for t, a in enumerate(tiles):
            r = _rope_tile(a, *tabs)
            if scale != 1.0:
                r = r * scale
            out_ref[0, :, t * LANES:(t + 1) * LANES] = r.astype(out_ref.dtype)

    ret_qk = proj(_R_RQ)
    roped(ret_qk[0:2], 1.0, rq_ref)
    roped(ret_qk[2:4], RET_DK ** -0.5, rk_ref)
    for t, a in enumerate(proj(_R_RV)):
        rv_ref[0, :, t * LANES:(t + 1) * LANES] = a.astype(BF16)
    for t, a in enumerate(proj(_R_RG)):
        rg_ref[0, :, t * LANES:(t + 1) * LANES] = (a * _sigmoid(a)).astype(BF16)
    roped(proj(_R_NQ), Q_SCALE, nq_ref)
    kc_t, vc_t, ks_t, vs_t = proj(_R_KC)
    kw_t, vw_t, gate0_t, gate1_t = proj(_R_KW)
    kc_ref[0] = kc_t
    vc_ref[0] = vc_t

    low = lax.broadcasted_iota(jnp.int32, (x.shape[0], LANES), 1) < NSA_DH

    def per_group(tile, fill, out_ref):
        out_ref[0, :, 0:LANES] = jnp.where(low, tile, fill).astype(out_ref.dtype)
        out_ref[0, :, LANES:2 * LANES] = jnp.where(low, _swap_halves(tile), fill).astype(out_ref.dtype)

    def per_group_t(tile, out_ref):
        t = tile.T
        ones = jnp.ones((ONES_ROWS, t.shape[1]), out_ref.dtype)
        for g in range(NSA_GROUPS):
            out_ref[0, g, 0:NSA_DH, :] = t[g * NSA_DH:(g + 1) * NSA_DH].astype(out_ref.dtype)
            out_ref[0, g, NSA_DH:NSA_DH + ONES_ROWS, :] = ones

    per_group(_rope_tile(ks_t, *tabs), hot_ref[...], ksx_ref)
    per_group_t(vs_t, vst_ref)
    per_group(_rope_tile(kw_t, *tabs), 0.0, kwx_ref)
    per_group_t(vw_t, vwt_ref)
    for g, gate_t in enumerate((gate0_t, gate1_t)):
        gate_ref[0, g] = _sigmoid(gate_t.T[0:GATE_ROWS])


def _prep_w_in(w_in):
    d = w_in.shape[0]
    cols = [w_in[:, :_R_GATE]]
    for g in range(NSA_GROUPS):
        for r in range(3):
            c0 = _R_GATE + r * NSA_HEADS + g * NSA_HPG
            cols.append(w_in[:, c0:c0 + NSA_HPG])
        cols.append(jnp.zeros((d, LANES - 3 * NSA_HPG), w_in.dtype))
    return jnp.concatenate(cols, axis=1).astype(BF16)


def _in_proj(x, ln_g, mod, w_prep, tabs, hot):
    b, s, d = x.shape
    tm = TM_IN
    grid = (b, s // tm)
    tab_spec = pl.BlockSpec((tm, LANES), lambda bi, j: (j, 0))

    def out(n, dtype):
        return (jax.ShapeDtypeStruct((b, s, n), dtype), pl.BlockSpec((1, tm, n), lambda bi, j: (bi, j, 0)))

    def out_t(rows, dtype):
        return (jax.ShapeDtypeStruct((b, NSA_GROUPS, rows, s), dtype),
                pl.BlockSpec((1, NSA_GROUPS, rows, tm), lambda bi, j: (bi, 0, 0, j)))

    vt_rows = NSA_DH + ONES_ROWS
    outs = [out(256, BF16), out(256, BF16), out(512, BF16), out(512, BF16), out(512, BF16),
            out(LANES, F32), out(LANES, F32),
            out(256, BF16), out_t(vt_rows, BF16), out(256, BF16), out_t(vt_rows, BF16),
            out_t(GATE_ROWS, F32)]
    return pl.pallas_call(
        _inproj_kernel,
        grid=grid,
        in_specs=[pl.BlockSpec((1, tm, d), lambda bi, j: (bi, j, 0)),
                  pl.BlockSpec((1, d), lambda bi, j: (0, 0)),
                  pl.BlockSpec((1, 6, d), lambda bi, j: (bi, 0, 0)),
                  pl.BlockSpec((d, IN_COLS_K), lambda bi, j: (0, 0)),
                  tab_spec, tab_spec, tab_spec, tab_spec],
        out_specs=[o[1] for o in outs],
        out_shape=[o[0] for o in outs],
        compiler_params=pltpu.CompilerParams(
            dimension_semantics=("parallel", "parallel"), vmem_limit_bytes=VMEM_LIMIT),
        name="in_proj",
    )(x, ln_g.reshape(1, d), mod, w_prep, *tabs, hot)


def _ret_kernel(q_ref, k_ref, v_ref, rg_ref, dec_ref, zeta_ref, xi_ref, gch_ref, o_ref, kv_ref, prev_ref):
    c = RET_CHUNK
    n_chunks = q_ref.shape[1] // c
    low = lax.broadcasted_iota(jnp.int32, (c, LANES), 1) < RET_DK

    def chunk_rows(n):
        return pl.ds(pl.multiple_of(n * c, c), c)

    def head_cols(h):
        return slice(h * RET_DV, (h + 1) * RET_DV)

    def kv_body(it, carry):
        kz_t = {}
        for j in range(RET_GROUP):
            rows = chunk_rows(it * RET_GROUP + j)
            for p in range(RET_HEADS // 2):
                pair = slice(p * LANES, (p + 1) * LANES)
                kz_t[j, p] = (k_ref[0, rows, pair].astype(F32) * zeta_ref[p]).T.astype(BF16)
        for j in range(RET_GROUP):
            n = it * RET_GROUP + j
            for h in range(RET_HEADS):
                kv_ref[h, n] = _nn(kz_t[j, h // 2], v_ref[0, chunk_rows(n), head_cols(h)])
        return carry

    lax.fori_loop(0, n_chunks // RET_GROUP, kv_body, 0)

    for h in range(RET_HEADS):
        def scan_body(n, st, h=h):
            prev_ref[h, n] = st.astype(prev_ref.dtype)
            return st * gch_ref[h] + kv_ref[h, n]
        lax.fori_loop(0, n_chunks, scan_body, jnp.zeros((LANES, RET_DV), F32))

    def out_body(it, carry):
        chains = [(j, h) for j in range(RET_GROUP) for h in range(RET_HEADS)]
        chunk = lambda j: it * RET_GROUP + j
        q_own, qx_own, att, ys = {}, {}, {}, {}
        for j, h in chains:
            p, e = divmod(h, 2)
            pair = slice(p * LANES, (p + 1) * LANES)
            mine = low if e == 0 else jnp.logical_not(low)
            q2 = q_ref[0, chunk_rows(chunk(j)), pair].astype(F32)
            q_own[j, h] = jnp.where(mine, q2, 0.0).astype(BF16)
            qx_own[j, h] = jnp.where(mine, q2 * xi_ref[p], 0.0).astype(BF16)
        for j, h in chains:
            pair = slice((h // 2) * LANES, (h // 2 + 1) * LANES)
            att[j, h] = _nt(q_own[j, h], k_ref[0, chunk_rows(chunk(j)), pair])
        for j, h in chains:
            lhs = jnp.concatenate([(att[j, h] * dec_ref[h]).astype(BF16), qx_own[j, h]], axis=1)
            rhs = jnp.concatenate([v_ref[0, chunk_rows(chunk(j)), head_cols(h)], prev_ref[h, chunk(j)]], axis=0)
            ys[j, h] = _nn(lhs, rhs)
        for j, h in chains:
            y = ys[j, h]
            yn = y * lax.rsqrt(jnp.mean(y * y, axis=-1, keepdims=True) + EPS)
            gate = rg_ref[0, chunk_rows(chunk(j)), head_cols(h)].astype(F32)
            o_ref[0, chunk_rows(chunk(j)), head_cols(h)] = (yn * gate).astype(o_ref.dtype)
        return carry

    lax.fori_loop(0, n_chunks // RET_GROUP, out_body, 0)


def _retention(rq, rk, rv, rg_act, tables):
    b, s, _ = rq.shape
    decay, zeta_p, xi_p, g_b = tables
    whole = lambda a: pl.BlockSpec(a.shape, lambda bi: (0,) * a.ndim)
    row = lambda a: pl.BlockSpec((1,) + a.shape[1:], lambda bi: (bi, 0, 0))
    return pl.pallas_call(
        _ret_kernel,
        grid=(b,),
        in_specs=[row(rq), row(rk), row(rv), row(rg_act),
                  whole(decay), whole(zeta_p), whole(xi_p), whole(g_b)],
        out_specs=row(rv),
        out_shape=jax.ShapeDtypeStruct(rv.shape, BF16),
        scratch_shapes=[pltpu.VMEM((RET_HEADS, s // RET_CHUNK, LANES, RET_DV), F32),
                        pltpu.VMEM((RET_HEADS, s // RET_CHUNK, LANES, RET_DV), BF16)],
        compiler_params=pltpu.CompilerParams(
            dimension_semantics=("parallel",), vmem_limit_bytes=VMEM_LIMIT),
        name="retention",
    )(rq, rk, rv, rg_act, decay, zeta_p, xi_p, g_b)


def _cmp_kernel(kc_ref, vc_ref, w1_ref, pe_ref, w2_ref, c_ref, s1_ref, s2_ref, kv_ref, vt_ref):
    n_piece = kc_ref.shape[1] // CMP_STRIDE
    hid_w = w1_ref.shape[3]
    out = jnp.zeros((n_piece, kv_ref.shape[2]), F32)
    for t, src in enumerate((kc_ref, vc_ref)):
        a_lo = jnp.zeros((n_piece, hid_w), F32)
        a_hi = jnp.zeros((n_piece, hid_w), F32)
        for r in range(CMP_STRIDE):
            xr = src[0, pl.ds(r, n_piece, stride=CMP_STRIDE), :]
            lo, hi = r, CMP_STRIDE + r
            a_lo = a_lo + _nn((xr + pe_ref[t, lo:lo + 1, :]).astype(BF16), w1_ref[t, lo])
            a_hi = a_hi + _nn((xr + pe_ref[t, hi:hi + 1, :]).astype(BF16), w1_ref[t, hi])
        hid = a_lo + pltpu.roll(a_hi, n_piece - 1, 0)
        out = out + _nn((hid * _sigmoid(hid)).astype(BF16), w2_ref[t])
    for t in range(out.shape[1] // LANES):
        sl = slice(t * LANES, (t + 1) * LANES)
        kv = _rope_tile(out[:, sl], c_ref[...], s1_ref[...], s2_ref[...])
        kv_ref[0, :, sl] = kv.astype(kv_ref.dtype)
        vt_ref[0, t] = kv.T[NSA_DH:2 * NSA_DH].astype(vt_ref.dtype)


def _compress_weights(w1_k, w2_k, w1_v, w2_v, pe_k, pe_v):
    dh, hid = NSA_DH, CMP_HIDDEN

    def first(w1):
        w = w1.reshape(CMP_LEN, dh, hid)
        z = jnp.zeros_like(w)
        return jnp.concatenate([jnp.concatenate([w, z], axis=2), jnp.concatenate([z, w], axis=2)], axis=1)

    def second(w2, off):
        z = jnp.zeros_like(w2)
        rows = []
        for g in range(NSA_GROUPS):
            c = [z, z, z, z]
            c[2 * g + off] = w2
            rows.append(jnp.concatenate(c, axis=1))
        return jnp.concatenate(rows, axis=0)

    w1 = jnp.stack([first(w1_k), first(w1_v)]).astype(BF16)
    w2 = jnp.stack([second(w2_k, 0), second(w2_v, 1)]).astype(BF16)
    pe = jnp.stack([jnp.concatenate([pe_k, pe_k], axis=1), jnp.concatenate([pe_v, pe_v], axis=1)])
    return w1, w2, pe


def _compress(kc, vc, w1, w2, pe, tabs_cmp):
    b, s, w = kc.shape
    n_piece = s // CMP_STRIDE
    const2 = lambda bi: (0, 0)
    src = pl.BlockSpec((1, s, w), lambda bi: (bi, 0, 0))
    dst = pl.BlockSpec((1, n_piece, 2 * w), lambda bi: (bi, 0, 0))
    shape = jax.ShapeDtypeStruct((b, n_piece, 2 * w), BF16)
    return pl.pallas_call(
        _cmp_kernel,
        grid=(b,),
        in_specs=[src, src,
                  pl.BlockSpec(w1.shape, lambda bi: (0, 0, 0, 0)),
                  pl.BlockSpec(pe.shape, lambda bi: (0, 0, 0)),
                  pl.BlockSpec(w2.shape, lambda bi: (0, 0, 0)),
                  pl.BlockSpec((n_piece, LANES), const2),
                  pl.BlockSpec((n_piece, LANES), const2),
                  pl.BlockSpec((n_piece, LANES), const2)],
        out_specs=[dst, pl.BlockSpec((1, NSA_GROUPS, NSA_DH, n_piece), lambda bi: (bi, 0, 0, 0))],
        out_shape=[shape, jax.ShapeDtypeStruct((b, NSA_GROUPS, NSA_DH, n_piece), BF16)],
        compiler_params=pltpu.CompilerParams(
            dimension_semantics=("parallel",), vmem_limit_bytes=VMEM_LIMIT),
        name="compress",
    )(kc, vc, w1, pe, w2, *tabs_cmp)


def _nsa_kernel(q_ref, kcmp_ref, vct_ref, ksx_ref, vst_ref, kwx_ref, vwt_ref, gate_ref, ovt_ref, o_ref):
    tq = TQ
    seq = q_ref.shape[1]
    nb = seq // SLC_LEN
    n_cmp = kcmp_ref.shape[1]
    hpg, dh = NSA_HPG, NSA_DH
    assert WIN % tq == 0 and seq % tq == 0

    low = lax.broadcasted_iota(jnp.int32, (tq, LANES), 1) < dh
    eye = jnp.where(lax.broadcasted_iota(jnp.int32, (tq, tq), 0)
                    == lax.broadcasted_iota(jnp.int32, (tq, tq), 1), 1.0, 0.0).astype(BF16)
    blk = lax.broadcasted_iota(jnp.int32, (nb, tq), 0)
    col = lax.broadcasted_iota(jnp.int32, (nb, tq), 1)
    crow = lax.broadcasted_iota(jnp.int32, (n_cmp, tq), 0)
    ccol = lax.broadcasted_iota(jnp.int32, (n_cmp, tq), 1)
    kcm = kcmp_ref[0]
    vct = vct_ref[0, 0]
    ovt = ovt_ref[...]

    def per_head(x):
        return jnp.concatenate([x] * hpg, axis=1)

    key_off = lax.broadcasted_iota(jnp.int32, (tq, tq), 0)
    qry_off = lax.broadcasted_iota(jnp.int32, (tq, tq), 1)
    not_after = per_head(jnp.where(key_off <= qry_off, 0.0, NEG))
    inside_win = per_head(jnp.where(key_off > qry_off, 0.0, NEG))

    def masked(s, first_key, t0, windowed):
        blocks = []
        for r in range(0, s.shape[0], tq):
            blk_s = s[r:r + tq]
            if first_key + r == t0:
                blk_s = blk_s + not_after
            elif windowed and first_key + r == t0 - WIN:
                blk_s = blk_s + inside_win
            blocks.append(blk_s)
        return jnp.concatenate(blocks, axis=0)

    def normalise(acc):
        return acc[0:dh] / acc[dh:dh + 1]

    def select_blocks(psum, t0):
        tcol = col + t0
        bcausal = blk * SLC_LEN <= tcol
        n_live = (t0 + tq - 1) // SLC_LEN + 1
        top_n = min(SLC_TOPK, nb)
        if n_live <= top_n:
            bias = jnp.where(bcausal, 0.0, NEG)
        else:
            p_hi = psum.astype(BF16)
            p_lo = (psum - p_hi.astype(F32)).astype(BF16)
            imp = _nn(ovt, p_hi) + _nn(ovt, p_lo)
            cur = tcol // SLC_LEN
            forced = (blk == 0) | (blk == cur) | (blk == cur - 1)
            imp = jnp.where(bcausal, jnp.where(forced, FORCE, imp), NEG)
            rank = jnp.zeros((nb, tq), F32)
            for j in range(n_live):
                r = imp[j:j + 1, :]
                rank = rank + jnp.where(blk > j, jnp.where(r >= imp, 1.0, 0.0), jnp.where(r > imp, 1.0, 0.0))
            bias = jnp.where((rank < float(top_n)) & bcausal, 0.0, NEG)
        feat = jnp.concatenate([jnp.zeros((dh, tq), F32), bias,
                                jnp.zeros((LANES - dh - nb, tq), F32)], axis=0).astype(BF16)
        return _nt(eye, feat)

    def tile_group(tiles):
        wstart = {t0: max(t0 - WIN, 0) for t0 in tiles}
        wkeys = {t0: slice(wstart[t0], t0 + tq) for t0 in tiles}
        skeys = {t0: slice(0, t0 + tq) for t0 in tiles}

        heads, qs = {}, {}
        for t0 in tiles:
            qf = q_ref[0, t0:t0 + tq, :].astype(F32)
            hl = []
            for hh in range(hpg):
                t = qf[:, (hh // 2) * LANES:(hh // 2 + 1) * LANES]
                if hh % 2 == 1:
                    t = _swap_halves(t)
                hl.append(jnp.where(low, t, 0.0))
            heads[t0] = hl
            qs[t0] = jnp.concatenate(hl, axis=0).astype(BF16)

        s_w = {t0: _nt(kwx_ref[0, wkeys[t0], :], qs[t0]) for t0 in tiles}
        s_c = {t0: _nt(kcm, qs[t0]) for t0 in tiles}

        p_cmp, qsel = {}, {}
        for t0 in tiles:
            cmask = (crow * CMP_STRIDE + (CMP_LEN - 1)) <= (ccol + t0)
            p_all = []
            psum = jnp.zeros((n_cmp, tq), F32)
            for hh in range(hpg):
                sh = jnp.where(cmask, s_c[t0][:, hh * tq:(hh + 1) * tq], NEG)
                e = jnp.exp2(sh - jnp.max(sh, axis=0, keepdims=True))
                p = jnp.where(cmask, e / jnp.sum(e, axis=0, keepdims=True), 0.0)
                psum = psum + p
                p_all.append(p.astype(BF16))
            p_cmp[t0] = jnp.concatenate(p_all, axis=1)
            qbias = select_blocks(psum, t0)
            qsel[t0] = jnp.concatenate([hd + qbias for hd in heads[t0]], axis=0).astype(BF16)

        s_s = {t0: _nt(ksx_ref[0, skeys[t0], :], qsel[t0]) for t0 in tiles}

        e_w = {}
        for t0 in tiles:
            sw = masked(s_w[t0], wstart[t0], t0, True)
            e_w[t0] = jnp.exp2(sw - jnp.max(sw, axis=0, keepdims=True)).astype(BF16)
        o_win = {t0: normalise(_nn(vwt_ref[0, 0, :, wkeys[t0]], e_w[t0])) for t0 in tiles}
        o_cmp = {t0: _nn(vct, p_cmp[t0]) for t0 in tiles}

        e_s = {}
        for t0 in tiles:
            ss = masked(s_s[t0], 0, t0, False)
            e_s[t0] = jnp.exp2(ss - jnp.max(ss, axis=0, keepdims=True)).astype(BF16)
        o_sel = {t0: normalise(_nn(vst_ref[0, 0, :, skeys[t0]], e_s[t0])) for t0 in tiles}

        for t0 in tiles:
            gt = gate_ref[0, 0, :, t0:t0 + tq]
            outs = []
            for hh in range(hpg):
                hc = slice(hh * tq, (hh + 1) * tq)
                g_c = gt[0 * hpg + hh:0 * hpg + hh + 1, :]
                g_s = gt[1 * hpg + hh:1 * hpg + hh + 1, :]
                g_w = gt[2 * hpg + hh:2 * hpg + hh + 1, :]
                outs.append(g_c * o_cmp[t0][:, hc] + g_s * o_sel[t0][:, hc] + g_w * o_win[t0][:, hc])
            o_ref[0, t0:t0 + tq, :] = jnp.concatenate(outs, axis=0).T.astype(o_ref.dtype)

    starts = list(range(0, seq, tq))
    for i in range(0, len(starts), TILE_GROUP):
        tile_group(starts[i:i + TILE_GROUP])


def _nsa_attention(nq, kvcmp, vct, ksx, vst, kwx, vwt, gate_t, ovt):
    b, s, _ = nq.shape
    n_cmp = kvcmp.shape[1]
    gw = NSA_HPG * NSA_DH
    per_group = lambda rows, width: pl.BlockSpec((1, rows, width), lambda bi, g: (bi, 0, g))
    per_group_t = lambda a: pl.BlockSpec((1, 1) + a.shape[2:], lambda bi, g: (bi, g, 0, 0))
    return pl.pallas_call(
        _nsa_kernel,
        grid=(b, NSA_GROUPS),
        in_specs=[per_group(s, gw),
                  per_group(n_cmp, LANES), per_group_t(vct),
                  per_group(s, LANES), per_group_t(vst), per_group(s, LANES), per_group_t(vwt),
                  per_group_t(gate_t),
                  pl.BlockSpec(ovt.shape, lambda bi, g: (0, 0))],
        out_specs=per_group(s, gw),
        out_shape=jax.ShapeDtypeStruct((b, s, NSA_HEADS * NSA_DH), BF16),
        compiler_params=pltpu.CompilerParams(
            dimension_semantics=("parallel", "parallel"), vmem_limit_bytes=VMEM_LIMIT),
        name="nsa_attn",
    )(nq, kvcmp, vct, ksx, vst, kwx, vwt, gate_t, ovt)


def _ffn_kernel(x_ref, yr_ref, yn_ref, mod_ref, wo_ref, g2_ref, gf_ref, wg_ref, wu_ref, wd_ref,
                o_ref, x1_ref, act_ref):
    half_w = yr_ref.shape[1]
    d_ff = wg_ref.shape[1]
    mix = _nn(yr_ref[...], wo_ref[0:half_w, :]) + _nn(yn_ref[...], wo_ref[half_w:2 * half_w, :])
    x1 = x_ref[...] + mod_ref[0, 2:3, :] * mix
    x1_ref[...] = x1
    y = x1 * lax.rsqrt(jnp.mean(x1 * x1, axis=-1, keepdims=True) + EPS) * g2_ref[...]
    h2 = (y * (1.0 + mod_ref[0, 4:5, :]) + mod_ref[0, 3:4, :]).astype(BF16)
    for j in range(d_ff // TF):
        sl = slice(j * TF, (j + 1) * TF)
        gate = _nn(h2, wg_ref[:, sl])
        up = _nn(h2, wu_ref[:, sl])
        act_ref[:, sl] = (gate * _sigmoid(gate) * up).astype(BF16)
    xo = x1_ref[...] + mod_ref[0, 5:6, :] * _nn(act_ref[...], wd_ref[...])
    o_ref[...] = xo * lax.rsqrt(jnp.mean(xo * xo, axis=-1, keepdims=True) + EPS) * gf_ref[...]


def _out_ffn(x2d, y_ret, y_nsa, mod, w_out, g2, gf, wg, wu, wd, seq):
    n, d = x2d.shape
    tm = TM_FF
    d_ff = wg.shape[1]
    tiles_per_seq = seq // tm
    half_w = y_ret.shape[1]
    row = lambda i: (i, 0)
    resident = lambda a: pl.BlockSpec(a.shape, lambda i: (0, 0), pipeline_mode=pl.Buffered(1))
    return pl.pallas_call(
        _ffn_kernel,
        grid=(n // tm,),
        in_specs=[pl.BlockSpec((tm, d), row),
                  pl.BlockSpec((tm, half_w), row),
                  pl.BlockSpec((tm, half_w), row),
                  pl.BlockSpec((1, 6, d), lambda i: (i // tiles_per_seq, 0, 0)),
                  resident(w_out),
                  pl.BlockSpec((1, d), lambda i: (0, 0)),
                  pl.BlockSpec((1, d), lambda i: (0, 0)),
                  resident(wg), resident(wu), resident(wd)],
        out_specs=pl.BlockSpec((tm, d), row),
        out_shape=jax.ShapeDtypeStruct((n, d), F32),
        scratch_shapes=[pltpu.VMEM((tm, d), F32), pltpu.VMEM((tm, d_ff), BF16)],
        compiler_params=pltpu.CompilerParams(
            dimension_semantics=("parallel",), vmem_limit_bytes=VMEM_LIMIT),
        name="out_ffn",
    )(x2d, y_ret, y_nsa, mod, w_out, g2.reshape(1, d), gf.reshape(1, d), wg, wu, wd)


def kernel(x, c, ln_mix_g, ln_ffn_g, w_ada, b_ada, w_in, cmp_pe_k, cmp_w1_k, cmp_w2_k,
           cmp_pe_v, cmp_w1_v, cmp_w2_v, w_out, w_ff_gate, w_ff_up, w_ff_down, ln_final_g):
    assert w_in.shape[0] == 1, "the final RMSNorm is fused into the (single) layer's FFN kernel"
    b, s, d = x.shape
    lane = np.arange(LANES)
    tabs = _rope_tables(np.arange(s), np.ones(LANES, bool))
    n_piece = s // CMP_STRIDE
    tabs_cmp = _rope_tables(np.arange(n_piece) * CMP_STRIDE + CMP_LEN - 1, lane < NSA_DH)

    mod = _ada(c, w_ada[0], b_ada[0]).reshape(b, 6, d)
    rq, rk, rv, rg_act, nq, kc, vc, ksx, vst, kwx, vwt, gate_t = _in_proj(
        x, ln_mix_g[0], mod, _prep_w_in(w_in[0]), tabs, _block_onehot_table(s))
    y_ret = _retention(rq, rk, rv, rg_act, _retention_tables())
    w1, w2, pe = _compress_weights(cmp_w1_k[0], cmp_w2_k[0], cmp_w1_v[0], cmp_w2_v[0],
                                   cmp_pe_k[0], cmp_pe_v[0])
    kvcmp, vct = _compress(kc, vc, w1, w2, pe, tabs_cmp)
    y_nsa = _nsa_attention(nq, kvcmp, vct, ksx, vst, kwx, vwt, gate_t, _overlap_t(s))
    out = _out_ffn(x.reshape(b * s, d), y_ret.reshape(b * s, -1), y_nsa.reshape(b * s, -1), mod,
                   w_out[0].astype(BF16), ln_ffn_g[0], ln_final_g,
                   w_ff_gate[0].astype(BF16), w_ff_up[0].astype(BF16), w_ff_down[0].astype(BF16), s)
    return out.reshape(b, s, d)
```

```python
import numpy as np
import jax
import jax.numpy as jnp
from jax import lax
from jax.experimental import pallas as pl
from jax.experimental.pallas import tpu as pltpu

F32 = jnp.float32
BF16 = jnp.bfloat16

D_MODEL = 1024
RET_HEADS = 4
RET_DK = 64
RET_DV = 128
RET_CHUNK = 128
NSA_HEADS = 8
NSA_GROUPS = 2
NSA_HPG = NSA_HEADS // NSA_GROUPS
NSA_DH = 64
CMP_LEN = 32
CMP_STRIDE = 16
CMP_HIDDEN = 128
SLC_LEN = 64
SLC_TOPK = 16
WIN = 512
D_FF = ((8 * D_MODEL + 3 * 256 - 1) // (3 * 256)) * 256
ROPE_THETA = 10000.0
EPS = 1e-6
NEG = -1e30
FORCE = 1e6

LANES = 128
SUBLANES = 8
HALF = NSA_DH // 2
ONES_ROWS = 16
GATE_ROWS = 3 * NSA_HEADS

TM_IN = 512
PROJ_W = 512
RET_GROUP = 4
Q_SCALE = NSA_DH ** -0.5 * float(np.log2(np.e))
TQ = 256
TILE_GROUP = 2
TM_FF = 512
TF = 256
VMEM_LIMIT = 56 * 1024 * 1024

_R_RQ, _R_RK, _R_RV, _R_RG, _R_NQ = 0, 256, 512, 1024, 1536
_R_KC, _R_VC, _R_KS, _R_VS, _R_KW, _R_VW, _R_GATE = 2048, 2176, 2304, 2432, 2560, 2688, 2816
IN_COLS_K = -(-(_R_GATE + 3 * NSA_HEADS) // PROJ_W) * PROJ_W


def _sigmoid(x):
    return 1.0 / (1.0 + jnp.exp(-x))


def _nt(a, b):
    return lax.dot_general(a, b, (((1,), (1,)), ((), ())), preferred_element_type=F32)


def _nn(a, b):
    return jnp.dot(a, b, preferred_element_type=F32)


def _rope_tile(a, c, s1, s2):
    return a * c + pltpu.roll(a, HALF, 1) * s1 + pltpu.roll(a, LANES - HALF, 1) * s2


def _swap_halves(a):
    return pltpu.roll(a, LANES // 2, 1)


def _rope_tables(pos, rotary_lanes):
    pos = np.asarray(pos, np.float64)
    lane = np.arange(LANES)
    within = lane % NSA_DH
    freq = ROPE_THETA ** (-(within % HALF).astype(np.float64) / HALF)
    ang = pos[:, None] * freq[None, :]
    cos, sin = np.cos(ang), np.sin(ang)
    first = (within < HALF)[None, :]
    rot = np.asarray(rotary_lanes, bool)[None, :]
    c = np.where(rot, cos, 1.0)
    s1 = np.where(rot & ~first, sin, 0.0)
    s2 = np.where(rot & first, -sin, 0.0)
    return (jnp.asarray(c, F32), jnp.asarray(s1, F32), jnp.asarray(s2, F32))


def _block_onehot_table(seq):
    t = np.zeros((seq, LANES), np.float32)
    pos = np.arange(seq)
    t[pos, NSA_DH + pos // SLC_LEN] = 1.0
    return jnp.asarray(t)


def _retention_tables():
    h = jnp.arange(RET_HEADS, dtype=F32)
    log_g = jnp.log(1.0 - 2.0 ** (-5.0 - h))
    c = RET_CHUNK
    idx = jnp.arange(c, dtype=F32)
    diff = idx[:, None] - idx[None, :]
    causal = diff >= 0
    decay = jnp.where(causal, jnp.exp(log_g[:, None, None] * jnp.where(causal, diff, 0.0)), 0.0)
    zeta = jnp.exp(log_g[:, None] * (c - 1.0 - idx))
    xi = jnp.exp(log_g[:, None] * (idx + 1.0))
    g_chunk = jnp.exp(log_g * c)

    def pair_lanes(t):
        t = t.reshape(RET_HEADS // 2, 2, c)
        return jnp.repeat(jnp.transpose(t, (0, 2, 1)), RET_DK, axis=2)

    g_b = jnp.broadcast_to(g_chunk[:, None, None], (RET_HEADS, 1, LANES))
    return decay, pair_lanes(zeta), pair_lanes(xi), g_b


def _overlap_t(seq):
    n_c = seq // CMP_STRIDE - CMP_LEN // CMP_STRIDE + 1
    nb = seq // SLC_LEN
    cs = np.arange(n_c) * CMP_STRIDE
    bs = np.arange(nb) * SLC_LEN
    ov = np.maximum(np.minimum(cs[:, None] + CMP_LEN, bs[None] + SLC_LEN)
                    - np.maximum(cs[:, None], bs[None]), 0).astype(np.float64) / CMP_LEN
    ncp = seq // CMP_STRIDE
    ovp = np.zeros((ncp, nb))
    ovp[:n_c] = ov
    return jnp.asarray(ovp.T, BF16)


def _ada_kernel(c_ref, w_ref, b_ref, o_ref):
    c = c_ref[...]
    o_ref[...] = _nn(c * _sigmoid(c), w_ref[...]) + b_ref[...]


def _ada(c, w, b):
    bsz, d = c.shape
    n = w.shape[1]
    tn = 1024
    return pl.pallas_call(
        _ada_kernel,
        grid=(n // tn,),
        in_specs=[pl.BlockSpec((bsz, d), lambda j: (0, 0)),
                  pl.BlockSpec((d, tn), lambda j: (0, j)),
                  pl.BlockSpec((1, tn), lambda j: (0, j))],
        out_specs=pl.BlockSpec((bsz, tn), lambda j: (0, j)),
        out_shape=jax.ShapeDtypeStruct((bsz, n), F32),
        compiler_params=pltpu.CompilerParams(vmem_limit_bytes=VMEM_LIMIT),
        name="ada",
    )(c, w, b.reshape(1, n))


def _inproj_kernel(x_ref, g_ref, mod_ref, w_ref, c_ref, s1_ref, s2_ref, hot_ref,
                   rq_ref, rk_ref, rv_ref, rg_ref, nq_ref, kc_ref, vc_ref,
                   ksx_ref, vst_ref, kwx_ref, vwt_ref, gate_ref):
    x = x_ref[0]
    y = x * lax.rsqrt(jnp.mean(x * x, axis=-1, keepdims=True) + EPS) * g_ref[...]
    h = y * (1.0 + mod_ref[0, 1:2, :]) + mod_ref[0, 0:1, :]
    hb = h.astype(BF16)

    def proj(c0):
        a = _nn(hb, w_ref[:, c0:c0 + PROJ_W])
        return [a[:, t * LANES:(t + 1) * LANES] for t in range(PROJ_W // LANES)]

    tabs = (c_ref[...], s1_ref[...], s2_ref[...])

    def roped(tiles, scale, out_ref):
        for t, a in enumerate(tiles):
            r = _rope_tile(a, *tabs)
            if scale != 1.0:
                r = r * scale
            out_ref[0, :, t * LANES:(t + 1) * LANES] = r.astype(out_ref.dtype)

    ret_qk = proj(_R_RQ)
    roped(ret_qk[0:2], 1.0, rq_ref)
    roped(ret_qk[2:4], RET_DK ** -0.5, rk_ref)
    for t, a in enumerate(proj(_R_RV)):
        rv_ref[0, :, t * LANES:(t + 1) * LANES] = a.astype(BF16)
    for t, a in enumerate(proj(_R_RG)):
        rg_ref[0, :, t * LANES:(t + 1) * LANES] = (a * _sigmoid(a)).astype(BF16)
    roped(proj(_R_NQ), Q_SCALE, nq_ref)
    kc_t, vc_t, ks_t, vs_t = proj(_R_KC)
    kw_t, vw_t, gates_t, unused_t = proj(_R_KW)
    kc_ref[0] = kc_t
    vc_ref[0] = vc_t

    low = lax.broadcasted_iota(jnp.int32, (x.shape[0], LANES), 1) < NSA_DH

    def per_group(tile, fill, out_ref):
        out_ref[0, :, 0:LANES] = jnp.where(low, tile, fill).astype(out_ref.dtype)
        out_ref[0, :, LANES:2 * LANES] = jnp.where(low, _swap_halves(tile), fill).astype(out_ref.dtype)

    def per_group_t(tile, out_ref):
        t = tile.T
        ones = jnp.ones((ONES_ROWS, t.shape[1]), out_ref.dtype)
        for g in range(NSA_GROUPS):
            out_ref[0, g, 0:NSA_DH, :] = t[g * NSA_DH:(g + 1) * NSA_DH].astype(out_ref.dtype)
            out_ref[0, g, NSA_DH:NSA_DH + ONES_ROWS, :] = ones

    per_group(_rope_tile(ks_t, *tabs), hot_ref[...], ksx_ref)
    per_group_t(vs_t, vst_ref)
    per_group(_rope_tile(kw_t, *tabs), 0.0, kwx_ref)
    per_group_t(vw_t, vwt_ref)
    del unused_t
    gate_ref[0] = _sigmoid(gates_t.T[0:GATE_ROWS])


def _prep_w_in(w_in):
    return jnp.pad(w_in, ((0, 0), (0, IN_COLS_K - w_in.shape[1]))).astype(BF16)


def _in_proj(x, ln_g, mod, w_prep, tabs, hot):
    b, s, d = x.shape
    tm = TM_IN
    grid = (b, s // tm)
    tab_spec = pl.BlockSpec((tm, LANES), lambda bi, j: (j, 0))

    def out(n, dtype):
        return (jax.ShapeDtypeStruct((b, s, n), dtype), pl.BlockSpec((1, tm, n), lambda bi, j: (bi, j, 0)))

    def out_t(rows, dtype):
        return (jax.ShapeDtypeStruct((b, NSA_GROUPS, rows, s), dtype),
                pl.BlockSpec((1, NSA_GROUPS, rows, tm), lambda bi, j: (bi, 0, 0, j)))

    vt_rows = NSA_DH + ONES_ROWS
    outs = [out(256, BF16), out(256, BF16), out(512, BF16), out(512, BF16), out(512, BF16),
            out(LANES, F32), out(LANES, F32),
            out(256, BF16), out_t(vt_rows, BF16), out(256, BF16), out_t(vt_rows, BF16),
            (jax.ShapeDtypeStruct((b, GATE_ROWS, s), F32),
             pl.BlockSpec((1, GATE_ROWS, tm), lambda bi, j: (bi, 0, j)))]
    return pl.pallas_call(
        _inproj_kernel,
        grid=grid,
        in_specs=[pl.BlockSpec((1, tm, d), lambda bi, j: (bi, j, 0)),
                  pl.BlockSpec((1, d), lambda bi, j: (0, 0)),
                  pl.BlockSpec((1, 6, d), lambda bi, j: (bi, 0, 0)),
                  pl.BlockSpec((d, IN_COLS_K), lambda bi, j: (0, 0)),
                  tab_spec, tab_spec, tab_spec, tab_spec],
        out_specs=[o[1] for o in outs],
        out_shape=[o[0] for o in outs],
        compiler_params=pltpu.CompilerParams(
            dimension_semantics=("parallel", "parallel"), vmem_limit_bytes=VMEM_LIMIT),
        name="in_proj",
    )(x, ln_g.reshape(1, d), mod, w_prep, *tabs, hot)


def _ret_kernel(q_ref, k_ref, v_ref, rg_ref, dec_ref, zeta_ref, xi_ref, gch_ref, o_ref, kv_ref, prev_ref):
    c = RET_CHUNK
    n_chunks = q_ref.shape[1] // c
    low = lax.broadcasted_iota(jnp.int32, (c, LANES), 1) < RET_DK

    def chunk_rows(n):
        return pl.ds(pl.multiple_of(n * c, c), c)

    def head_cols(h):
        return slice(h * RET_DV, (h + 1) * RET_DV)

    def kv_body(it, carry):
        kz_t = {}
        for j in range(RET_GROUP):
            rows = chunk_rows(it * RET_GROUP + j)
            for p in range(RET_HEADS // 2):
                pair = slice(p * LANES, (p + 1) * LANES)
                kz_t[j, p] = (k_ref[0, rows, pair].astype(F32) * zeta_ref[p]).T.astype(BF16)
        for j in range(RET_GROUP):
            n = it * RET_GROUP + j
            for h in range(RET_HEADS):
                kv_ref[h, n] = _nn(kz_t[j, h // 2], v_ref[0, chunk_rows(n), head_cols(h)])
        return carry

    lax.fori_loop(0, n_chunks // RET_GROUP, kv_body, 0)

    for h in range(RET_HEADS):
        def scan_body(n, st, h=h):
            prev_ref[h, n] = st.astype(prev_ref.dtype)
            return st * gch_ref[h] + kv_ref[h, n]
        lax.fori_loop(0, n_chunks, scan_body, jnp.zeros((LANES, RET_DV), F32))

    def out_body(it, carry):
        chains = [(j, h) for j in range(RET_GROUP) for h in range(RET_HEADS)]
        chunk = lambda j: it * RET_GROUP + j
        q_own, qx_own, att, ys = {}, {}, {}, {}
        for j, h in chains:
            p, e = divmod(h, 2)
            pair = slice(p * LANES, (p + 1) * LANES)
            mine = low if e == 0 else jnp.logical_not(low)
            q2 = q_ref[0, chunk_rows(chunk(j)), pair].astype(F32)
            q_own[j, h] = jnp.where(mine, q2, 0.0).astype(BF16)
            qx_own[j, h] = jnp.where(mine, q2 * xi_ref[p], 0.0).astype(BF16)
        for j, h in chains:
            pair = slice((h // 2) * LANES, (h // 2 + 1) * LANES)
            att[j, h] = _nt(q_own[j, h], k_ref[0, chunk_rows(chunk(j)), pair])
        for j, h in chains:
            lhs = jnp.concatenate([(att[j, h] * dec_ref[h]).astype(BF16), qx_own[j, h]], axis=1)
            rhs = jnp.concatenate([v_ref[0, chunk_rows(chunk(j)), head_cols(h)], prev_ref[h, chunk(j)]], axis=0)
            ys[j, h] = _nn(lhs, rhs)
        for j, h in chains:
            y = ys[j, h]
            yn = y * lax.rsqrt(jnp.mean(y * y, axis=-1, keepdims=True) + EPS)
            gate = rg_ref[0, chunk_rows(chunk(j)), head_cols(h)].astype(F32)
            o_ref[0, chunk_rows(chunk(j)), head_cols(h)] = (yn * gate).astype(o_ref.dtype)
        return carry

    lax.fori_loop(0, n_chunks // RET_GROUP, out_body, 0)


def _retention(rq, rk, rv, rg_act, tables):
    b, s, _ = rq.shape
    decay, zeta_p, xi_p, g_b = tables
    whole = lambda a: pl.BlockSpec(a.shape, lambda bi: (0,) * a.ndim)
    row = lambda a: pl.BlockSpec((1,) + a.shape[1:], lambda bi: (bi, 0, 0))
    return pl.pallas_call(
        _ret_kernel,
        grid=(b,),
        in_specs=[row(rq), row(rk), row(rv), row(rg_act),
                  whole(decay), whole(zeta_p), whole(xi_p), whole(g_b)],
        out_specs=row(rv),
        out_shape=jax.ShapeDtypeStruct(rv.shape, BF16),
        scratch_shapes=[pltpu.VMEM((RET_HEADS, s // RET_CHUNK, LANES, RET_DV), F32),
                        pltpu.VMEM((RET_HEADS, s // RET_CHUNK, LANES, RET_DV), BF16)],
        compiler_params=pltpu.CompilerParams(
            dimension_semantics=("parallel",), vmem_limit_bytes=VMEM_LIMIT),
        name="retention",
    )(rq, rk, rv, rg_act, decay, zeta_p, xi_p, g_b)


def _cmp_kernel(kc_ref, vc_ref, w1_ref, pe_ref, w2_ref, c_ref, s1_ref, s2_ref, kv_ref, vt_ref):
    n_piece = kc_ref.shape[1] // CMP_STRIDE
    hid_w = w1_ref.shape[3]
    out = jnp.zeros((n_piece, kv_ref.shape[2]), F32)
    for t, src in enumerate((kc_ref, vc_ref)):
        a_lo = jnp.zeros((n_piece, hid_w), F32)
        a_hi = jnp.zeros((n_piece, hid_w), F32)
        for r in range(CMP_STRIDE):
            xr = src[0, pl.ds(r, n_piece, stride=CMP_STRIDE), :]
            lo, hi = r, CMP_STRIDE + r
            a_lo = a_lo + _nn((xr + pe_ref[t, lo:lo + 1, :]).astype(BF16), w1_ref[t, lo])
            a_hi = a_hi + _nn((xr + pe_ref[t, hi:hi + 1, :]).astype(BF16), w1_ref[t, hi])
        hid = a_lo + pltpu.roll(a_hi, n_piece - 1, 0)
        out = out + _nn((hid * _sigmoid(hid)).astype(BF16), w2_ref[t])
    for t in range(out.shape[1] // LANES):
        sl = slice(t * LANES, (t + 1) * LANES)
        kv = _rope_tile(out[:, sl], c_ref[...], s1_ref[...], s2_ref[...])
        kv_ref[0, :, sl] = kv.astype(kv_ref.dtype)
        vt_ref[0, t] = kv.T[NSA_DH:2 * NSA_DH].astype(vt_ref.dtype)


def _compress_weights(w1_k, w2_k, w1_v, w2_v, pe_k, pe_v):
    dh, hid = NSA_DH, CMP_HIDDEN

    def first(w1):
        w = w1.reshape(CMP_LEN, dh, hid)
        z = jnp.zeros_like(w)
        return jnp.concatenate([jnp.concatenate([w, z], axis=2), jnp.concatenate([z, w], axis=2)], axis=1)

    def second(w2, off):
        z = jnp.zeros_like(w2)
        rows = []
        for g in range(NSA_GROUPS):
            c = [z, z, z, z]
            c[2 * g + off] = w2
            rows.append(jnp.concatenate(c, axis=1))
        return jnp.concatenate(rows, axis=0)

    w1 = jnp.stack([first(w1_k), first(w1_v)]).astype(BF16)
    w2 = jnp.stack([second(w2_k, 0), second(w2_v, 1)]).astype(BF16)
    pe = jnp.stack([jnp.concatenate([pe_k, pe_k], axis=1), jnp.concatenate([pe_v, pe_v], axis=1)])
    return w1, w2, pe


def _compress(kc, vc, w1, w2, pe, tabs_cmp):
    b, s, w = kc.shape
    n_piece = s // CMP_STRIDE
    const2 = lambda bi: (0, 0)
    src = pl.BlockSpec((1, s, w), lambda bi: (bi, 0, 0))
    dst = pl.BlockSpec((1, n_piece, 2 * w), lambda bi: (bi, 0, 0))
    shape = jax.ShapeDtypeStruct((b, n_piece, 2 * w), BF16)
    return pl.pallas_call(
        _cmp_kernel,
        grid=(b,),
        in_specs=[src, src,
                  pl.BlockSpec(w1.shape, lambda bi: (0, 0, 0, 0)),
                  pl.BlockSpec(pe.shape, lambda bi: (0, 0, 0)),
                  pl.BlockSpec(w2.shape, lambda bi: (0, 0, 0)),
                  pl.BlockSpec((n_piece, LANES), const2),
                  pl.BlockSpec((n_piece, LANES), const2),
                  pl.BlockSpec((n_piece, LANES), const2)],
        out_specs=[dst, pl.BlockSpec((1, NSA_GROUPS, NSA_DH, n_piece), lambda bi: (bi, 0, 0, 0))],
        out_shape=[shape, jax.ShapeDtypeStruct((b, NSA_GROUPS, NSA_DH, n_piece), BF16)],
        compiler_params=pltpu.CompilerParams(
            dimension_semantics=("parallel",), vmem_limit_bytes=VMEM_LIMIT),
        name="compress",
    )(kc, vc, w1, pe, w2, *tabs_cmp)


def _nsa_kernel(q_ref, kcmp_ref, vct_ref, ksx_ref, vst_ref, kwx_ref, vwt_ref, gate_ref, ovt_ref, o_ref):
    tq = TQ
    seq = q_ref.shape[1]
    nb = seq // SLC_LEN
    n_cmp = kcmp_ref.shape[1]
    hpg, dh = NSA_HPG, NSA_DH
    assert WIN % tq == 0 and seq % tq == 0
    group = pl.program_id(1)

    low = lax.broadcasted_iota(jnp.int32, (tq, LANES), 1) < dh
    eye = jnp.where(lax.broadcasted_iota(jnp.int32, (tq, tq), 0)
                    == lax.broadcasted_iota(jnp.int32, (tq, tq), 1), 1.0, 0.0).astype(BF16)
    blk = lax.broadcasted_iota(jnp.int32, (nb, tq), 0)
    col = lax.broadcasted_iota(jnp.int32, (nb, tq), 1)
    crow = lax.broadcasted_iota(jnp.int32, (n_cmp, tq), 0)
    ccol = lax.broadcasted_iota(jnp.int32, (n_cmp, tq), 1)
    kcm = kcmp_ref[0]
    vct = vct_ref[0, 0]
    ovt = ovt_ref[...]

    def per_head(x):
        return jnp.concatenate([x] * hpg, axis=1)

    key_off = lax.broadcasted_iota(jnp.int32, (tq, tq), 0)
    qry_off = lax.broadcasted_iota(jnp.int32, (tq, tq), 1)
    not_after = per_head(jnp.where(key_off <= qry_off, 0.0, NEG))
    inside_win = per_head(jnp.where(key_off > qry_off, 0.0, NEG))

    def masked(s, first_key, t0, windowed):
        blocks = []
        for r in range(0, s.shape[0], tq):
            blk_s = s[r:r + tq]
            if first_key + r == t0:
                blk_s = blk_s + not_after
            elif windowed and first_key + r == t0 - WIN:
                blk_s = blk_s + inside_win
            blocks.append(blk_s)
        return jnp.concatenate(blocks, axis=0)

    def normalise(acc):
        return acc[0:dh] / acc[dh:dh + 1]

    def select_blocks(psum, t0):
        tcol = col + t0
        bcausal = blk * SLC_LEN <= tcol
        n_live = (t0 + tq - 1) // SLC_LEN + 1
        top_n = min(SLC_TOPK, nb)
        if n_live <= top_n:
            bias = jnp.where(bcausal, 0.0, NEG)
        else:
            p_hi = psum.astype(BF16)
            p_lo = (psum - p_hi.astype(F32)).astype(BF16)
            imp = _nn(ovt, p_hi) + _nn(ovt, p_lo)
            cur = tcol // SLC_LEN
            forced = (blk == 0) | (blk == cur) | (blk == cur - 1)
            imp = jnp.where(bcausal, jnp.where(forced, FORCE, imp), NEG)
            rank = jnp.zeros((nb, tq), F32)
            for j in range(n_live):
                r = imp[j:j + 1, :]
                rank = rank + jnp.where(blk > j, jnp.where(r >= imp, 1.0, 0.0), jnp.where(r > imp, 1.0, 0.0))
            bias = jnp.where((rank < float(top_n)) & bcausal, 0.0, NEG)
        feat = jnp.concatenate([jnp.zeros((dh, tq), F32), bias,
                                jnp.zeros((LANES - dh - nb, tq), F32)], axis=0).astype(BF16)
        return _nt(eye, feat)

    def tile_group(tiles):
        wstart = {t0: max(t0 - WIN, 0) for t0 in tiles}
        wkeys = {t0: slice(wstart[t0], t0 + tq) for t0 in tiles}
        skeys = {t0: slice(0, t0 + tq) for t0 in tiles}

        heads, qs = {}, {}
        for t0 in tiles:
            qf = q_ref[0, t0:t0 + tq, :].astype(F32)
            hl = []
            for hh in range(hpg):
                t = qf[:, (hh // 2) * LANES:(hh // 2 + 1) * LANES]
                if hh % 2 == 1:
                    t = _swap_halves(t)
                hl.append(jnp.where(low, t, 0.0))
            heads[t0] = hl
            qs[t0] = jnp.concatenate(hl, axis=0).astype(BF16)

        s_w = {t0: _nt(kwx_ref[0, wkeys[t0], :], qs[t0]) for t0 in tiles}
        s_c = {t0: _nt(kcm, qs[t0]) for t0 in tiles}

        p_cmp, qsel = {}, {}
        for t0 in tiles:
            cmask = (crow * CMP_STRIDE + (CMP_LEN - 1)) <= (ccol + t0)
            p_all = []
            psum = jnp.zeros((n_cmp, tq), F32)
            for hh in range(hpg):
                sh = jnp.where(cmask, s_c[t0][:, hh * tq:(hh + 1) * tq], NEG)
                e = jnp.exp2(sh - jnp.max(sh, axis=0, keepdims=True))
                p = jnp.where(cmask, e / jnp.sum(e, axis=0, keepdims=True), 0.0)
                psum = psum + p
                p_all.append(p.astype(BF16))
            p_cmp[t0] = jnp.concatenate(p_all, axis=1)
            qbias = select_blocks(psum, t0)
            qsel[t0] = jnp.concatenate([hd + qbias for hd in heads[t0]], axis=0).astype(BF16)

        s_s = {t0: _nt(ksx_ref[0, skeys[t0], :], qsel[t0]) for t0 in tiles}

        e_w = {}
        for t0 in tiles:
            sw = masked(s_w[t0], wstart[t0], t0, True)
            e_w[t0] = jnp.exp2(sw - jnp.max(sw, axis=0, keepdims=True)).astype(BF16)
        o_win = {t0: normalise(_nn(vwt_ref[0, 0, :, wkeys[t0]], e_w[t0])) for t0 in tiles}
        o_cmp = {t0: _nn(vct, p_cmp[t0]) for t0 in tiles}

        e_s = {}
        for t0 in tiles:
            ss = masked(s_s[t0], 0, t0, False)
            e_s[t0] = jnp.exp2(ss - jnp.max(ss, axis=0, keepdims=True)).astype(BF16)
        o_sel = {t0: normalise(_nn(vst_ref[0, 0, :, skeys[t0]], e_s[t0])) for t0 in tiles}

        for t0 in tiles:
            gt = gate_ref[0, :, t0:t0 + tq]
            outs = []
            for hh in range(hpg):
                hc = slice(hh * tq, (hh + 1) * tq)

                def gate_row(branch):
                    by_group = [gt[branch * NSA_HEADS + g * hpg + hh:branch * NSA_HEADS + g * hpg + hh + 1, :]
                                for g in range(NSA_GROUPS)]
                    row = by_group[-1]
                    for g in range(NSA_GROUPS - 2, -1, -1):
                        row = jnp.where(group == g, by_group[g], row)
                    return row

                outs.append(gate_row(0) * o_cmp[t0][:, hc] + gate_row(1) * o_sel[t0][:, hc]
                            + gate_row(2) * o_win[t0][:, hc])
            o_ref[0, t0:t0 + tq, :] = jnp.concatenate(outs, axis=0).T.astype(o_ref.dtype)

    starts = list(range(0, seq, tq))
    for i in range(0, len(starts), TILE_GROUP):
        tile_group(starts[i:i + TILE_GROUP])


def _nsa_attention(nq, kvcmp, vct, ksx, vst, kwx, vwt, gate_t, ovt):
    b, s, _ = nq.shape
    n_cmp = kvcmp.shape[1]
    gw = NSA_HPG * NSA_DH
    per_group = lambda rows, width: pl.BlockSpec((1, rows, width), lambda bi, g: (bi, 0, g))
    per_group_t = lambda a: pl.BlockSpec((1, 1) + a.shape[2:], lambda bi, g: (bi, g, 0, 0))
    return pl.pallas_call(
        _nsa_kernel,
        grid=(b, NSA_GROUPS),
        in_specs=[per_group(s, gw),
                  per_group(n_cmp, LANES), per_group_t(vct),
                  per_group(s, LANES), per_group_t(vst), per_group(s, LANES), per_group_t(vwt),
                  pl.BlockSpec((1,) + gate_t.shape[1:], lambda bi, g: (bi, 0, 0)),
                  pl.BlockSpec(ovt.shape, lambda bi, g: (0, 0))],
        out_specs=per_group(s, gw),
        out_shape=jax.ShapeDtypeStruct((b, s, NSA_HEADS * NSA_DH), BF16),
        compiler_params=pltpu.CompilerParams(
            dimension_semantics=("parallel", "parallel"), vmem_limit_bytes=VMEM_LIMIT),
        name="nsa_attn",
    )(nq, kvcmp, vct, ksx, vst, kwx, vwt, gate_t, ovt)


def _ffn_kernel(x_ref, yr_ref, yn_ref, mod_ref, wo_ref, g2_ref, gf_ref, wg_ref, wu_ref, wd_ref,
                o_ref, x1_ref, act_ref):
    half_w = yr_ref.shape[1]
    d_ff = wg_ref.shape[1]
    n_chunks = d_ff // TF
    rows_a = slice(0, x_ref.shape[0] // 2)
    rows_b = slice(x_ref.shape[0] // 2, x_ref.shape[0])

    def mix(rows):
        return _nn(yr_ref[rows, :], wo_ref[0:half_w, :]) + _nn(yn_ref[rows, :], wo_ref[half_w:2 * half_w, :])

    def mid_norm(rows, mixed):
        x1 = x_ref[rows, :] + mod_ref[0, 2:3, :] * mixed
        x1_ref[rows, :] = x1
        y = x1 * lax.rsqrt(jnp.mean(x1 * x1, axis=-1, keepdims=True) + EPS) * g2_ref[...]
        return (y * (1.0 + mod_ref[0, 4:5, :]) + mod_ref[0, 3:4, :]).astype(BF16)

    def ff_chunk(rows, h2, j):
        sl = slice(j * TF, (j + 1) * TF)
        gate = _nn(h2, wg_ref[:, sl])
        up = _nn(h2, wu_ref[:, sl])
        act_ref[rows, sl] = (gate * _sigmoid(gate) * up).astype(BF16)

    def down(rows):
        return x1_ref[rows, :] + mod_ref[0, 5:6, :] * _nn(act_ref[rows, :], wd_ref[...])

    def final_norm(rows, xo):
        o_ref[rows, :] = xo * lax.rsqrt(jnp.mean(xo * xo, axis=-1, keepdims=True) + EPS) * gf_ref[...]

    mix_a = mix(rows_a)
    mix_b = mix(rows_b)
    h2_a = mid_norm(rows_a, mix_a)
    ff_chunk(rows_a, h2_a, 0)
    h2_b = mid_norm(rows_b, mix_b)
    for j in range(1, n_chunks):
        ff_chunk(rows_a, h2_a, j)
    xo_a = down(rows_a)
    ff_chunk(rows_b, h2_b, 0)
    final_norm(rows_a, xo_a)
    for j in range(1, n_chunks):
        ff_chunk(rows_b, h2_b, j)
    final_norm(rows_b, down(rows_b))


def _out_ffn(x2d, y_ret, y_nsa, mod, w_out, g2, gf, wg, wu, wd, seq):
    n, d = x2d.shape
    tm = TM_FF
    d_ff = wg.shape[1]
    tiles_per_seq = seq // tm
    half_w = y_ret.shape[1]
    row = lambda i: (i, 0)
    resident = lambda a: pl.BlockSpec(a.shape, lambda i: (0, 0), pipeline_mode=pl.Buffered(1))
    return pl.pallas_call(
        _ffn_kernel,
        grid=(n // tm,),
        in_specs=[pl.BlockSpec((tm, d), row),
                  pl.BlockSpec((tm, half_w), row),
                  pl.BlockSpec((tm, half_w), row),
                  pl.BlockSpec((1, 6, d), lambda i: (i // tiles_per_seq, 0, 0)),
                  resident(w_out),
                  pl.BlockSpec((1, d), lambda i: (0, 0)),
                  pl.BlockSpec((1, d), lambda i: (0, 0)),
                  resident(wg), resident(wu), resident(wd)],
        out_specs=pl.BlockSpec((tm, d), row),
        out_shape=jax.ShapeDtypeStruct((n, d), F32),
        scratch_shapes=[pltpu.VMEM((tm, d), F32), pltpu.VMEM((tm, d_ff), BF16)],
        compiler_params=pltpu.CompilerParams(
            dimension_semantics=("parallel",), vmem_limit_bytes=VMEM_LIMIT),
        name="out_ffn",
    )(x2d, y_ret, y_nsa, mod, w_out, g2.reshape(1, d), gf.reshape(1, d), wg, wu, wd)


def kernel(x, c, ln_mix_g, ln_ffn_g, w_ada, b_ada, w_in, cmp_pe_k, cmp_w1_k, cmp_w2_k,
           cmp_pe_v, cmp_w1_v, cmp_w2_v, w_out, w_ff_gate, w_ff_up, w_ff_down, ln_final_g):
    assert w_in.shape[0] == 1, "the final RMSNorm is fused into the (single) layer's FFN kernel"
    b, s, d = x.shape
    lane = np.arange(LANES)
    tabs = _rope_tables(np.arange(s), np.ones(LANES, bool))
    n_piece = s // CMP_STRIDE
    tabs_cmp = _rope_tables(np.arange(n_piece) * CMP_STRIDE + CMP_LEN - 1, lane < NSA_DH)

    mod = _ada(c, w_ada[0], b_ada[0]).reshape(b, 6, d)
    rq, rk, rv, rg_act, nq, kc, vc, ksx, vst, kwx, vwt, gate_t = _in_proj(
        x, ln_mix_g[0], mod, _prep_w_in(w_in[0]), tabs, _block_onehot_table(s))
    y_ret = _retention(rq, rk, rv, rg_act, _retention_tables())
    w1, w2, pe = _compress_weights(cmp_w1_k[0], cmp_w2_k[0], cmp_w1_v[0], cmp_w2_v[0],
                                   cmp_pe_k[0], cmp_pe_v[0])
    kvcmp, vct = _compress(kc, vc, w1, w2, pe, tabs_cmp)
    y_nsa = _nsa_attention(nq, kvcmp, vct, ksx, vst, kwx, vwt, gate_t, _overlap_t(s))
    out = _out_ffn(x.reshape(b * s, d), y_ret.reshape(b * s, -1), y_nsa.reshape(b * s, -1), mod,
                   w_out[0].astype(BF16), ln_ffn_g[0], ln_final_g,
                   w_ff_gate[0].astype(BF16), w_ff_up[0].astype(BF16), w_ff_down[0].astype(BF16), s)
    return out.reshape(b, s, d)
```

```python
import numpy as np
import jax
import jax.numpy as jnp
from jax import lax
from jax.experimental import pallas as pl
from jax.experimental.pallas import tpu as pltpu

F32 = jnp.float32
BF16 = jnp.bfloat16

D_MODEL = 1024
RET_HEADS = 4
RET_DK = 64
RET_DV = 128
RET_CHUNK = 128
NSA_HEADS = 8
NSA_GROUPS = 2
NSA_HPG = NSA_HEADS // NSA_GROUPS
NSA_DH = 64
CMP_LEN = 32
CMP_STRIDE = 16
CMP_HIDDEN = 128
SLC_LEN = 64
SLC_TOPK = 16
WIN = 512
D_FF = ((8 * D_MODEL + 3 * 256 - 1) // (3 * 256)) * 256
ROPE_THETA = 10000.0
EPS = 1e-6
NEG = -1e30
FORCE = 1e6

LANES = 128
SUBLANES = 8
HALF = NSA_DH // 2
ONES_ROWS = 16
GATE_ROWS = 3 * NSA_HEADS

TM_IN = 512
PROJ_W = 512
RET_GROUP = 4
Q_SCALE = NSA_DH ** -0.5 * float(np.log2(np.e))
TQ = 256
TILE_GROUP = 2
TM_FF = 512
TF = 256
VMEM_LIMIT = 56 * 1024 * 1024

_R_RQ, _R_RK, _R_RV, _R_RG, _R_NQ = 0, 256, 512, 1024, 1536
_R_KC, _R_VC, _R_KS, _R_VS, _R_KW, _R_VW, _R_GATE = 2048, 2176, 2304, 2432, 2560, 2688, 2816
IN_COLS_K = -(-(_R_GATE + 3 * NSA_HEADS) // PROJ_W) * PROJ_W


def _sigmoid(x):
    return 1.0 / (1.0 + jnp.exp(-x))


def _nt(a, b):
    return lax.dot_general(a, b, (((1,), (1,)), ((), ())), preferred_element_type=F32)


def _nn(a, b):
    return jnp.dot(a, b, preferred_element_type=F32)


def _rope_tile(a, c, s1, s2):
    return a * c + pltpu.roll(a, HALF, 1) * s1 + pltpu.roll(a, LANES - HALF, 1) * s2


def _swap_halves(a):
    return pltpu.roll(a, LANES // 2, 1)


def _rope_tables(pos, rotary_lanes):
    pos = np.asarray(pos, np.float64)
    lane = np.arange(LANES)
    within = lane % NSA_DH
    freq = ROPE_THETA ** (-(within % HALF).astype(np.float64) / HALF)
    ang = pos[:, None] * freq[None, :]
    cos, sin = np.cos(ang), np.sin(ang)
    first = (within < HALF)[None, :]
    rot = np.asarray(rotary_lanes, bool)[None, :]
    c = np.where(rot, cos, 1.0)
    s1 = np.where(rot & ~first, sin, 0.0)
    s2 = np.where(rot & first, -sin, 0.0)
    return (jnp.asarray(c, F32), jnp.asarray(s1, F32), jnp.asarray(s2, F32))


def _block_onehot_table(seq):
    t = np.zeros((seq, LANES), np.float32)
    pos = np.arange(seq)
    t[pos, NSA_DH + pos // SLC_LEN] = 1.0
    return jnp.asarray(t)


def _retention_tables():
    h = jnp.arange(RET_HEADS, dtype=F32)
    log_g = jnp.log(1.0 - 2.0 ** (-5.0 - h))
    c = RET_CHUNK
    idx = jnp.arange(c, dtype=F32)
    diff = idx[:, None] - idx[None, :]
    causal = diff >= 0
    decay = jnp.where(causal, jnp.exp(log_g[:, None, None] * jnp.where(causal, diff, 0.0)), 0.0)
    zeta = jnp.exp(log_g[:, None] * (c - 1.0 - idx))
    xi = jnp.exp(log_g[:, None] * (idx + 1.0))
    g_chunk = jnp.exp(log_g * c)

    def pair_lanes(t):
        t = t.reshape(RET_HEADS // 2, 2, c)
        return jnp.repeat(jnp.transpose(t, (0, 2, 1)), RET_DK, axis=2)

    g_b = jnp.broadcast_to(g_chunk[:, None, None], (RET_HEADS, 1, LANES))
    return decay, pair_lanes(zeta), pair_lanes(xi), g_b


def _overlap_t(seq):
    n_c = seq // CMP_STRIDE - CMP_LEN // CMP_STRIDE + 1
    nb = seq // SLC_LEN
    cs = np.arange(n_c) * CMP_STRIDE
    bs = np.arange(nb) * SLC_LEN
    ov = np.maximum(np.minimum(cs[:, None] + CMP_LEN, bs[None] + SLC_LEN)
                    - np.maximum(cs[:, None], bs[None]), 0).astype(np.float64) / CMP_LEN
    ncp = seq // CMP_STRIDE
    ovp = np.zeros((ncp, nb))
    ovp[:n_c] = ov
    return jnp.asarray(ovp.T, BF16)


def _ada_kernel(c_ref, w_ref, b_ref, o_ref):
    c = c_ref[...]
    o_ref[...] = _nn(c * _sigmoid(c), w_ref[...]) + b_ref[...]


def _ada(c, w, b):
    bsz, d = c.shape
    n = w.shape[1]
    tn = 1024
    return pl.pallas_call(
        _ada_kernel,
        grid=(n // tn,),
        in_specs=[pl.BlockSpec((bsz, d), lambda j: (0, 0)),
                  pl.BlockSpec((d, tn), lambda j: (0, j)),
                  pl.BlockSpec((1, tn), lambda j: (0, j))],
        out_specs=pl.BlockSpec((bsz, tn), lambda j: (0, j)),
        out_shape=jax.ShapeDtypeStruct((bsz, n), F32),
        compiler_params=pltpu.CompilerParams(vmem_limit_bytes=VMEM_LIMIT),
        name="ada",
    )(c, w, b.reshape(1, n))


def _inproj_kernel(x_ref, g_ref, mod_ref, w_ref, c_ref, s1_ref, s2_ref, hot_ref,
                   rq_ref, rk_ref, rv_ref, rg_ref, nq_ref, kc_ref, vc_ref,
                   ksx_ref, vst_ref, kwx_ref, vwt_ref, gate_ref, wb_ref):
    @pl.when((pl.program_id(0) == 0) & (pl.program_id(1) == 0))
    def _():
        n_in = w_ref.shape[2]
        whole = (n_in // PROJ_W) * PROJ_W
        for c0 in range(0, whole, PROJ_W):
            wb_ref[:, c0:c0 + PROJ_W] = w_ref[0, :, c0:c0 + PROJ_W].astype(BF16)
        wb_ref[:, whole:] = jnp.zeros((wb_ref.shape[0], wb_ref.shape[1] - whole), BF16)
        wb_ref[:, whole:n_in] = w_ref[0, :, whole:n_in].astype(BF16)

    def modulated_norm(rows):
        x = x_ref[0, rows, :]
        y = x * lax.rsqrt(jnp.mean(x * x, axis=-1, keepdims=True) + EPS) * g_ref[...]
        return (y * (1.0 + mod_ref[0, 1:2, :]) + mod_ref[0, 0:1, :]).astype(BF16)

    def product_steps(rows, hb):
        tabs = (c_ref[rows, :], s1_ref[rows, :], s2_ref[rows, :])
        low = lax.broadcasted_iota(jnp.int32, (rows.stop - rows.start, LANES), 1) < NSA_DH

        def proj(c0):
            a = _nn(hb, wb_ref[:, c0:c0 + PROJ_W])
            return [a[:, t * LANES:(t + 1) * LANES] for t in range(PROJ_W // LANES)]

        def put(out_ref, t, value):
            out_ref[0, rows, t * LANES:(t + 1) * LANES] = value.astype(out_ref.dtype)

        def roped(tiles, scale, out_ref):
            for t, a in enumerate(tiles):
                r = _rope_tile(a, *tabs)
                put(out_ref, t, r if scale == 1.0 else r * scale)

        def per_group(tile, fill, out_ref):
            put(out_ref, 0, jnp.where(low, tile, fill))
            put(out_ref, 1, jnp.where(low, _swap_halves(tile), fill))

        def per_group_t(tile, out_ref):
            t = tile.T
            ones = jnp.ones((ONES_ROWS, t.shape[1]), out_ref.dtype)
            for g in range(NSA_GROUPS):
                out_ref[0, g, 0:NSA_DH, rows] = t[g * NSA_DH:(g + 1) * NSA_DH].astype(out_ref.dtype)
                out_ref[0, g, NSA_DH:NSA_DH + ONES_ROWS, rows] = ones

        def retention_qk():
            tiles = proj(_R_RQ)
            roped(tiles[0:2], 1.0, rq_ref)
            roped(tiles[2:4], RET_DK ** -0.5, rk_ref)

        def retention_v():
            for t, a in enumerate(proj(_R_RV)):
                put(rv_ref, t, a)

        def retention_gate():
            for t, a in enumerate(proj(_R_RG)):
                put(rg_ref, t, a * _sigmoid(a))

        def nsa_q():
            roped(proj(_R_NQ), Q_SCALE, nq_ref)

        def nsa_compress_selected():
            kc_t, vc_t, ks_t, vs_t = proj(_R_KC)
            kc_ref[0, rows, :] = kc_t
            vc_ref[0, rows, :] = vc_t
            per_group(_rope_tile(ks_t, *tabs), hot_ref[rows, :], ksx_ref)
            per_group_t(vs_t, vst_ref)

        def nsa_window_gates():
            kw_t, vw_t, gates_t, _ = proj(_R_KW)
            per_group(_rope_tile(kw_t, *tabs), 0.0, kwx_ref)
            per_group_t(vw_t, vwt_ref)
            gate_ref[0, :, rows] = _sigmoid(gates_t.T[0:GATE_ROWS])

        return [retention_qk, retention_v, retention_gate, nsa_q, nsa_compress_selected, nsa_window_gates]

    tm = x_ref.shape[1]
    rows_a, rows_b = slice(0, tm // 2), slice(tm // 2, tm)
    steps_a = product_steps(rows_a, modulated_norm(rows_a))
    steps_a[0]()
    steps_b = product_steps(rows_b, modulated_norm(rows_b))
    for step in steps_a[1:] + steps_b:
        step()


def _in_proj(x, ln_g, mod, w_in, tabs, hot):
    b, s, d = x.shape
    tm = TM_IN
    grid = (b, s // tm)
    tab_spec = pl.BlockSpec((tm, LANES), lambda bi, j: (j, 0))

    def out(n, dtype):
        return (jax.ShapeDtypeStruct((b, s, n), dtype), pl.BlockSpec((1, tm, n), lambda bi, j: (bi, j, 0)))

    def out_t(rows, dtype):
        return (jax.ShapeDtypeStruct((b, NSA_GROUPS, rows, s), dtype),
                pl.BlockSpec((1, NSA_GROUPS, rows, tm), lambda bi, j: (bi, 0, 0, j)))

    vt_rows = NSA_DH + ONES_ROWS
    outs = [out(256, BF16), out(256, BF16), out(512, BF16), out(512, BF16), out(512, BF16),
            out(LANES, F32), out(LANES, F32),
            out(256, BF16), out_t(vt_rows, BF16), out(256, BF16), out_t(vt_rows, BF16),
            (jax.ShapeDtypeStruct((b, GATE_ROWS, s), F32),
             pl.BlockSpec((1, GATE_ROWS, tm), lambda bi, j: (bi, 0, j)))]
    return pl.pallas_call(
        _inproj_kernel,
        grid=grid,
        in_specs=[pl.BlockSpec((1, tm, d), lambda bi, j: (bi, j, 0)),
                  pl.BlockSpec((1, d), lambda bi, j: (0, 0)),
                  pl.BlockSpec((1, 6, d), lambda bi, j: (bi, 0, 0)),
                  pl.BlockSpec((1,) + w_in.shape[1:], lambda bi, j: (0, 0, 0), pipeline_mode=pl.Buffered(1)),
                  tab_spec, tab_spec, tab_spec, tab_spec],
        out_specs=[o[1] for o in outs],
        out_shape=[o[0] for o in outs],
        scratch_shapes=[pltpu.VMEM((d, IN_COLS_K), BF16)],
        compiler_params=pltpu.CompilerParams(
            dimension_semantics=("arbitrary", "arbitrary"), vmem_limit_bytes=VMEM_LIMIT),
        name="in_proj",
    )(x, ln_g.reshape(1, d), mod, w_in, *tabs, hot)


def _ret_kernel(q_ref, k_ref, v_ref, rg_ref, dec_ref, zeta_ref, xi_ref, gch_ref, o_ref, kv_ref, prev_ref):
    c = RET_CHUNK
    n_chunks = q_ref.shape[1] // c
    low = lax.broadcasted_iota(jnp.int32, (c, LANES), 1) < RET_DK

    def chunk_rows(n):
        return pl.ds(pl.multiple_of(n * c, c), c)

    def head_cols(h):
        return slice(h * RET_DV, (h + 1) * RET_DV)

    def kv_body(it, carry):
        kz_t = {}
        for j in range(RET_GROUP):
            rows = chunk_rows(it * RET_GROUP + j)
            for p in range(RET_HEADS // 2):
                pair = slice(p * LANES, (p + 1) * LANES)
                kz_t[j, p] = (k_ref[0, rows, pair].astype(F32) * zeta_ref[p]).T.astype(BF16)
        for j in range(RET_GROUP):
            n = it * RET_GROUP + j
            for h in range(RET_HEADS):
                kv_ref[h, n] = _nn(kz_t[j, h // 2], v_ref[0, chunk_rows(n), head_cols(h)])
        return carry

    lax.fori_loop(0, n_chunks // RET_GROUP, kv_body, 0)

    for h in range(RET_HEADS):
        def scan_body(n, st, h=h):
            prev_ref[h, n] = st.astype(prev_ref.dtype)
            return st * gch_ref[h] + kv_ref[h, n]
        lax.fori_loop(0, n_chunks, scan_body, jnp.zeros((LANES, RET_DV), F32))

    def out_body(it, carry):
        chains = [(j, h) for j in range(RET_GROUP) for h in range(RET_HEADS)]
        chunk = lambda j: it * RET_GROUP + j
        q_own, qx_own, att, ys = {}, {}, {}, {}
        for j, h in chains:
            p, e = divmod(h, 2)
            pair = slice(p * LANES, (p + 1) * LANES)
            mine = low if e == 0 else jnp.logical_not(low)
            q2 = q_ref[0, chunk_rows(chunk(j)), pair].astype(F32)
            q_own[j, h] = jnp.where(mine, q2, 0.0).astype(BF16)
            qx_own[j, h] = jnp.where(mine, q2 * xi_ref[p], 0.0).astype(BF16)
        for j, h in chains:
            pair = slice((h // 2) * LANES, (h // 2 + 1) * LANES)
            att[j, h] = _nt(q_own[j, h], k_ref[0, chunk_rows(chunk(j)), pair])
        for j, h in chains:
            lhs = jnp.concatenate([(att[j, h] * dec_ref[h]).astype(BF16), qx_own[j, h]], axis=1)
            rhs = jnp.concatenate([v_ref[0, chunk_rows(chunk(j)), head_cols(h)], prev_ref[h, chunk(j)]], axis=0)
            ys[j, h] = _nn(lhs, rhs)
        for j, h in chains:
            y = ys[j, h]
            yn = y * lax.rsqrt(jnp.mean(y * y, axis=-1, keepdims=True) + EPS)
            gate = rg_ref[0, chunk_rows(chunk(j)), head_cols(h)].astype(F32)
            o_ref[0, chunk_rows(chunk(j)), head_cols(h)] = (yn * gate).astype(o_ref.dtype)
        return carry

    lax.fori_loop(0, n_chunks // RET_GROUP, out_body, 0)


def _retention(rq, rk, rv, rg_act, tables):
    b, s, _ = rq.shape
    decay, zeta_p, xi_p, g_b = tables
    whole = lambda a: pl.BlockSpec(a.shape, lambda bi: (0,) * a.ndim)
    row = lambda a: pl.BlockSpec((1,) + a.shape[1:], lambda bi: (bi, 0, 0))
    return pl.pallas_call(
        _ret_kernel,
        grid=(b,),
        in_specs=[row(rq), row(rk), row(rv), row(rg_act),
                  whole(decay), whole(zeta_p), whole(xi_p), whole(g_b)],
        out_specs=row(rv),
        out_shape=jax.ShapeDtypeStruct(rv.shape, BF16),
        scratch_shapes=[pltpu.VMEM((RET_HEADS, s // RET_CHUNK, LANES, RET_DV), F32),
                        pltpu.VMEM((RET_HEADS, s // RET_CHUNK, LANES, RET_DV), BF16)],
        compiler_params=pltpu.CompilerParams(
            dimension_semantics=("parallel",), vmem_limit_bytes=VMEM_LIMIT),
        name="retention",
    )(rq, rk, rv, rg_act, decay, zeta_p, xi_p, g_b)


def _cmp_kernel(kc_ref, vc_ref, w1_ref, pe_ref, w2_ref, c_ref, s1_ref, s2_ref, kv_ref, vt_ref):
    n_piece = kc_ref.shape[1] // CMP_STRIDE
    hid_w = w1_ref.shape[3]
    out = jnp.zeros((n_piece, kv_ref.shape[2]), F32)
    for t, src in enumerate((kc_ref, vc_ref)):
        a_lo = jnp.zeros((n_piece, hid_w), F32)
        a_hi = jnp.zeros((n_piece, hid_w), F32)
        for r in range(CMP_STRIDE):
            xr = src[0, pl.ds(r, n_piece, stride=CMP_STRIDE), :]
            lo, hi = r, CMP_STRIDE + r
            a_lo = a_lo + _nn((xr + pe_ref[t, lo:lo + 1, :]).astype(BF16), w1_ref[t, lo])
            a_hi = a_hi + _nn((xr + pe_ref[t, hi:hi + 1, :]).astype(BF16), w1_ref[t, hi])
        hid = a_lo + pltpu.roll(a_hi, n_piece - 1, 0)
        out = out + _nn((hid * _sigmoid(hid)).astype(BF16), w2_ref[t])
    for t in range(out.shape[1] // LANES):
        sl = slice(t * LANES, (t + 1) * LANES)
        kv = _rope_tile(out[:, sl], c_ref[...], s1_ref[...], s2_ref[...])
        kv_ref[0, :, sl] = kv.astype(kv_ref.dtype)
        vt_ref[0, t] = kv.T[NSA_DH:2 * NSA_DH].astype(vt_ref.dtype)


def _compress_weights(w1_k, w2_k, w1_v, w2_v, pe_k, pe_v):
    dh, hid = NSA_DH, CMP_HIDDEN

    def first(w1):
        w = w1.reshape(CMP_LEN, dh, hid)
        z = jnp.zeros_like(w)
        return jnp.concatenate([jnp.concatenate([w, z], axis=2), jnp.concatenate([z, w], axis=2)], axis=1)

    def second(w2, off):
        z = jnp.zeros_like(w2)
        rows = []
        for g in range(NSA_GROUPS):
            c = [z, z, z, z]
            c[2 * g + off] = w2
            rows.append(jnp.concatenate(c, axis=1))
        return jnp.concatenate(rows, axis=0)

    w1 = jnp.stack([first(w1_k), first(w1_v)]).astype(BF16)
    w2 = jnp.stack([second(w2_k, 0), second(w2_v, 1)]).astype(BF16)
    pe = jnp.stack([jnp.concatenate([pe_k, pe_k], axis=1), jnp.concatenate([pe_v, pe_v], axis=1)])
    return w1, w2, pe


def _compress(kc, vc, w1, w2, pe, tabs_cmp):
    b, s, w = kc.shape
    n_piece = s // CMP_STRIDE
    const2 = lambda bi: (0, 0)
    src = pl.BlockSpec((1, s, w), lambda bi: (bi, 0, 0))
    dst = pl.BlockSpec((1, n_piece, 2 * w), lambda bi: (bi, 0, 0))
    shape = jax.ShapeDtypeStruct((b, n_piece, 2 * w), BF16)
    return pl.pallas_call(
        _cmp_kernel,
        grid=(b,),
        in_specs=[src, src,
                  pl.BlockSpec(w1.shape, lambda bi: (0, 0, 0, 0)),
                  pl.BlockSpec(pe.shape, lambda bi: (0, 0, 0)),
                  pl.BlockSpec(w2.shape, lambda bi: (0, 0, 0)),
                  pl.BlockSpec((n_piece, LANES), const2),
                  pl.BlockSpec((n_piece, LANES), const2),
                  pl.BlockSpec((n_piece, LANES), const2)],
        out_specs=[dst, pl.BlockSpec((1, NSA_GROUPS, NSA_DH, n_piece), lambda bi: (bi, 0, 0, 0))],
        out_shape=[shape, jax.ShapeDtypeStruct((b, NSA_GROUPS, NSA_DH, n_piece), BF16)],
        compiler_params=pltpu.CompilerParams(
            dimension_semantics=("parallel",), vmem_limit_bytes=VMEM_LIMIT),
        name="compress",
    )(kc, vc, w1, pe, w2, *tabs_cmp)


def _nsa_kernel(q_ref, kcmp_ref, vct_ref, ksx_ref, vst_ref, kwx_ref, vwt_ref, gate_ref, ovt_ref, o_ref):
    tq = TQ
    seq = q_ref.shape[1]
    nb = seq // SLC_LEN
    n_cmp = kcmp_ref.shape[1]
    hpg, dh = NSA_HPG, NSA_DH
    assert WIN % tq == 0 and seq % tq == 0
    group = pl.program_id(1)

    low = lax.broadcasted_iota(jnp.int32, (tq, LANES), 1) < dh
    eye = jnp.where(lax.broadcasted_iota(jnp.int32, (tq, tq), 0)
                    == lax.broadcasted_iota(jnp.int32, (tq, tq), 1), 1.0, 0.0).astype(BF16)
    blk = lax.broadcasted_iota(jnp.int32, (nb, tq), 0)
    col = lax.broadcasted_iota(jnp.int32, (nb, tq), 1)
    crow = lax.broadcasted_iota(jnp.int32, (n_cmp, tq), 0)
    ccol = lax.broadcasted_iota(jnp.int32, (n_cmp, tq), 1)
    kcm = kcmp_ref[0]
    vct = vct_ref[0, 0]
    ovt = ovt_ref[...]

    def per_head(x):
        return jnp.concatenate([x] * hpg, axis=1)

    key_off = lax.broadcasted_iota(jnp.int32, (tq, tq), 0)
    qry_off = lax.broadcasted_iota(jnp.int32, (tq, tq), 1)
    not_after = per_head(jnp.where(key_off <= qry_off, 0.0, NEG))
    inside_win = per_head(jnp.where(key_off > qry_off, 0.0, NEG))

    def masked(s, first_key, t0, windowed):
        blocks = []
        for r in range(0, s.shape[0], tq):
            blk_s = s[r:r + tq]
            if first_key + r == t0:
                blk_s = blk_s + not_after
            elif windowed and first_key + r == t0 - WIN:
                blk_s = blk_s + inside_win
            blocks.append(blk_s)
        return jnp.concatenate(blocks, axis=0)

    def normalise(acc):
        return acc[0:dh] / acc[dh:dh + 1]

    def select_blocks(psum, t0):
        tcol = col + t0
        bcausal = blk * SLC_LEN <= tcol
        n_live = (t0 + tq - 1) // SLC_LEN + 1
        top_n = min(SLC_TOPK, nb)
        if n_live <= top_n:
            bias = jnp.where(bcausal, 0.0, NEG)
        else:
            p_hi = psum.astype(BF16)
            p_lo = (psum - p_hi.astype(F32)).astype(BF16)
            imp = _nn(ovt, p_hi) + _nn(ovt, p_lo)
            cur = tcol // SLC_LEN
            forced = (blk == 0) | (blk == cur) | (blk == cur - 1)
            imp = jnp.where(bcausal, jnp.where(forced, FORCE, imp), NEG)
            rank = jnp.zeros((nb, tq), F32)
            for j in range(n_live):
                r = imp[j:j + 1, :]
                rank = rank + jnp.where(blk > j, jnp.where(r >= imp, 1.0, 0.0), jnp.where(r > imp, 1.0, 0.0))
            bias = jnp.where((rank < float(top_n)) & bcausal, 0.0, NEG)
        feat = jnp.concatenate([jnp.zeros((dh, tq), F32), bias,
                                jnp.zeros((LANES - dh - nb, tq), F32)], axis=0).astype(BF16)
        return _nt(eye, feat)

    def tile_group(tiles):
        wstart = {t0: max(t0 - WIN, 0) for t0 in tiles}
        wkeys = {t0: slice(wstart[t0], t0 + tq) for t0 in tiles}
        skeys = {t0: slice(0, t0 + tq) for t0 in tiles}

        heads, qs = {}, {}
        for t0 in tiles:
            qf = q_ref[0, t0:t0 + tq, :].astype(F32)
            hl = []
            for hh in range(hpg):
                t = qf[:, (hh // 2) * LANES:(hh // 2 + 1) * LANES]
                if hh % 2 == 1:
                    t = _swap_halves(t)
                hl.append(jnp.where(low, t, 0.0))
            heads[t0] = hl
            qs[t0] = jnp.concatenate(hl, axis=0).astype(BF16)

        s_w = {t0: _nt(kwx_ref[0, wkeys[t0], :], qs[t0]) for t0 in tiles}
        s_c = {t0: _nt(kcm, qs[t0]) for t0 in tiles}

        p_cmp, qsel = {}, {}
        for t0 in tiles:
            cmask = (crow * CMP_STRIDE + (CMP_LEN - 1)) <= (ccol + t0)
            p_all = []
            psum = jnp.zeros((n_cmp, tq), F32)
            for hh in range(hpg):
                sh = jnp.where(cmask, s_c[t0][:, hh * tq:(hh + 1) * tq], NEG)
                e = jnp.exp2(sh - jnp.max(sh, axis=0, keepdims=True))
                p = jnp.where(cmask, e / jnp.sum(e, axis=0, keepdims=True), 0.0)
                psum = psum + p
                p_all.append(p.astype(BF16))
            p_cmp[t0] = jnp.concatenate(p_all, axis=1)
            qbias = select_blocks(psum, t0)
            qsel[t0] = jnp.concatenate([hd + qbias for hd in heads[t0]], axis=0).astype(BF16)

        s_s = {t0: _nt(ksx_ref[0, skeys[t0], :], qsel[t0]) for t0 in tiles}

        e_w = {}
        for t0 in tiles:
            sw = masked(s_w[t0], wstart[t0], t0, True)
            e_w[t0] = jnp.exp2(sw - jnp.max(sw, axis=0, keepdims=True)).astype(BF16)
        o_win = {t0: normalise(_nn(vwt_ref[0, 0, :, wkeys[t0]], e_w[t0])) for t0 in tiles}
        o_cmp = {t0: _nn(vct, p_cmp[t0]) for t0 in tiles}

        e_s = {}
        for t0 in tiles:
            ss = masked(s_s[t0], 0, t0, False)
            e_s[t0] = jnp.exp2(ss - jnp.max(ss, axis=0, keepdims=True)).astype(BF16)
        o_sel = {t0: normalise(_nn(vst_ref[0, 0, :, skeys[t0]], e_s[t0])) for t0 in tiles}

        for t0 in tiles:
            gt = gate_ref[0, :, t0:t0 + tq]
            outs = []
            for hh in range(hpg):
                hc = slice(hh * tq, (hh + 1) * tq)

                def gate_row(branch):
                    by_group = [gt[branch * NSA_HEADS + g * hpg + hh:branch * NSA_HEADS + g * hpg + hh + 1, :]
                                for g in range(NSA_GROUPS)]
                    row = by_group[-1]
                    for g in range(NSA_GROUPS - 2, -1, -1):
                        row = jnp.where(group == g, by_group[g], row)
                    return row

                outs.append(gate_row(0) * o_cmp[t0][:, hc] + gate_row(1) * o_sel[t0][:, hc]
                            + gate_row(2) * o_win[t0][:, hc])
            o_ref[0, t0:t0 + tq, :] = jnp.concatenate(outs, axis=0).T.astype(o_ref.dtype)

    starts = list(range(0, seq, tq))
    for i in range(0, len(starts), TILE_GROUP):
        tile_group(starts[i:i + TILE_GROUP])


def _nsa_attention(nq, kvcmp, vct, ksx, vst, kwx, vwt, gate_t, ovt):
    b, s, _ = nq.shape
    n_cmp = kvcmp.shape[1]
    gw = NSA_HPG * NSA_DH
    per_group = lambda rows, width: pl.BlockSpec((1, rows, width), lambda bi, g: (bi, 0, g))
    per_group_t = lambda a: pl.BlockSpec((1, 1) + a.shape[2:], lambda bi, g: (bi, g, 0, 0))
    return pl.pallas_call(
        _nsa_kernel,
        grid=(b, NSA_GROUPS),
        in_specs=[per_group(s, gw),
                  per_group(n_cmp, LANES), per_group_t(vct),
                  per_group(s, LANES), per_group_t(vst), per_group(s, LANES), per_group_t(vwt),
                  pl.BlockSpec((1,) + gate_t.shape[1:], lambda bi, g: (bi, 0, 0)),
                  pl.BlockSpec(ovt.shape, lambda bi, g: (0, 0))],
        out_specs=per_group(s, gw),
        out_shape=jax.ShapeDtypeStruct((b, s, NSA_HEADS * NSA_DH), BF16),
        compiler_params=pltpu.CompilerParams(
            dimension_semantics=("parallel", "parallel"), vmem_limit_bytes=VMEM_LIMIT),
        name="nsa_attn",
    )(nq, kvcmp, vct, ksx, vst, kwx, vwt, gate_t, ovt)


def _ffn_kernel(x_ref, yr_ref, yn_ref, mod_ref, wo_ref, g2_ref, gf_ref, wg_ref, wu_ref, wd_ref,
                o_ref, x1_ref, act_ref):
    half_w = yr_ref.shape[1]
    d_ff = wg_ref.shape[1]
    n_chunks = d_ff // TF
    rows_a = slice(0, x_ref.shape[0] // 2)
    rows_b = slice(x_ref.shape[0] // 2, x_ref.shape[0])

    def mix(rows):
        return _nn(yr_ref[rows, :], wo_ref[0:half_w, :]) + _nn(yn_ref[rows, :], wo_ref[half_w:2 * half_w, :])

    def mid_norm(rows, mixed):
        x1 = x_ref[rows, :] + mod_ref[0, 2:3, :] * mixed
        x1_ref[rows, :] = x1
        y = x1 * lax.rsqrt(jnp.mean(x1 * x1, axis=-1, keepdims=True) + EPS) * g2_ref[...]
        return (y * (1.0 + mod_ref[0, 4:5, :]) + mod_ref[0, 3:4, :]).astype(BF16)

    def ff_chunk(rows, h2, j):
        sl = slice(j * TF, (j + 1) * TF)
        gate = _nn(h2, wg_ref[:, sl])
        up = _nn(h2, wu_ref[:, sl])
        act_ref[rows, sl] = (gate * _sigmoid(gate) * up).astype(BF16)

    def down(rows):
        return x1_ref[rows, :] + mod_ref[0, 5:6, :] * _nn(act_ref[rows, :], wd_ref[...])

    def final_norm(rows, xo):
        o_ref[rows, :] = xo * lax.rsqrt(jnp.mean(xo * xo, axis=-1, keepdims=True) + EPS) * gf_ref[...]

    mix_a = mix(rows_a)
    mix_b = mix(rows_b)
    h2_a = mid_norm(rows_a, mix_a)
    ff_chunk(rows_a, h2_a, 0)
    h2_b = mid_norm(rows_b, mix_b)
    for j in range(1, n_chunks):
        ff_chunk(rows_a, h2_a, j)
    xo_a = down(rows_a)
    ff_chunk(rows_b, h2_b, 0)
    final_norm(rows_a, xo_a)
    for j in range(1, n_chunks):
        ff_chunk(rows_b, h2_b, j)
    final_norm(rows_b, down(rows_b))


def _out_ffn(x2d, y_ret, y_nsa, mod, w_out, g2, gf, wg, wu, wd, seq):
    n, d = x2d.shape
    tm = TM_FF
    d_ff = wg.shape[1]
    tiles_per_seq = seq // tm
    half_w = y_ret.shape[1]
    row = lambda i: (i, 0)
    resident = lambda a: pl.BlockSpec(a.shape, lambda i: (0, 0), pipeline_mode=pl.Buffered(1))
    return pl.pallas_call(
        _ffn_kernel,
        grid=(n // tm,),
        in_specs=[pl.BlockSpec((tm, d), row),
                  pl.BlockSpec((tm, half_w), row),
                  pl.BlockSpec((tm, half_w), row),
                  pl.BlockSpec((1, 6, d), lambda i: (i // tiles_per_seq, 0, 0)),
                  resident(w_out),
                  pl.BlockSpec((1, d), lambda i: (0, 0)),
                  pl.BlockSpec((1, d), lambda i: (0, 0)),
                  resident(wg), resident(wu), resident(wd)],
        out_specs=pl.BlockSpec((tm, d), row),
        out_shape=jax.ShapeDtypeStruct((n, d), F32),
        scratch_shapes=[pltpu.VMEM((tm, d), F32), pltpu.VMEM((tm, d_ff), BF16)],
        compiler_params=pltpu.CompilerParams(
            dimension_semantics=("parallel",), vmem_limit_bytes=VMEM_LIMIT),
        name="out_ffn",
    )(x2d, y_ret, y_nsa, mod, w_out, g2.reshape(1, d), gf.reshape(1, d), wg, wu, wd)


def kernel(x, c, ln_mix_g, ln_ffn_g, w_ada, b_ada, w_in, cmp_pe_k, cmp_w1_k, cmp_w2_k,
           cmp_pe_v, cmp_w1_v, cmp_w2_v, w_out, w_ff_gate, w_ff_up, w_ff_down, ln_final_g):
    assert w_in.shape[0] == 1, "the final RMSNorm is fused into the (single) layer's FFN kernel"
    b, s, d = x.shape
    lane = np.arange(LANES)
    tabs = _rope_tables(np.arange(s), np.ones(LANES, bool))
    n_piece = s // CMP_STRIDE
    tabs_cmp = _rope_tables(np.arange(n_piece) * CMP_STRIDE + CMP_LEN - 1, lane < NSA_DH)

    mod = _ada(c, w_ada[0], b_ada[0]).reshape(b, 6, d)
    rq, rk, rv, rg_act, nq, kc, vc, ksx, vst, kwx, vwt, gate_t = _in_proj(
        x, ln_mix_g[0], mod, w_in, tabs, _block_onehot_table(s))
    y_ret = _retention(rq, rk, rv, rg_act, _retention_tables())
    w1, w2, pe = _compress_weights(cmp_w1_k[0], cmp_w2_k[0], cmp_w1_v[0], cmp_w2_v[0],
                                   cmp_pe_k[0], cmp_pe_v[0])
    kvcmp, vct = _compress(kc, vc, w1, w2, pe, tabs_cmp)
    y_nsa = _nsa_attention(nq, kvcmp, vct, ksx, vst, kwx, vwt, gate_t, _overlap_t(s))
    out = _out_ffn(x.reshape(b * s, d), y_ret.reshape(b * s, -1), y_nsa.reshape(b * s, -1), mod,
                   w_out[0].astype(BF16), ln_ffn_g[0], ln_final_g,
                   w_ff_gate[0].astype(BF16), w_ff_up[0].astype(BF16), w_ff_down[0].astype(BF16), s)
    return out.reshape(b, s, d)
```

```python
import numpy as np
import jax
import jax.numpy as jnp
from jax import lax
from jax.experimental import pallas as pl
from jax.experimental.pallas import tpu as pltpu

F32 = jnp.float32
BF16 = jnp.bfloat16

D_MODEL = 1024
RET_HEADS = 4
RET_DK = 64
RET_DV = 128
RET_CHUNK = 128
NSA_HEADS = 8
NSA_GROUPS = 2
NSA_HPG = NSA_HEADS // NSA_GROUPS
NSA_DH = 64
CMP_LEN = 32
CMP_STRIDE = 16
CMP_HIDDEN = 128
SLC_LEN = 64
SLC_TOPK = 16
WIN = 512
D_FF = ((8 * D_MODEL + 3 * 256 - 1) // (3 * 256)) * 256
ROPE_THETA = 10000.0
EPS = 1e-6
NEG = -1e30
FORCE = 1e6

LANES = 128
SUBLANES = 8
HALF = NSA_DH // 2
ONES_ROWS = 16
GATE_ROWS = 3 * NSA_HEADS

TM_IN = 1024
PROJ_W = 512
RET_GROUP = 4
Q_SCALE = NSA_DH ** -0.5 * float(np.log2(np.e))
TQ = 256
TILE_GROUP = 2
TM_FF = 1024
TF = 256
VMEM_LIMIT = 56 * 1024 * 1024

_R_RQ, _R_RK, _R_RV, _R_RG, _R_NQ = 0, 256, 512, 1024, 1536
_R_KC, _R_VC, _R_KS, _R_VS, _R_KW, _R_VW, _R_GATE = 2048, 2176, 2304, 2432, 2560, 2688, 2816
IN_COLS_K = -(-(_R_GATE + 3 * NSA_HEADS) // PROJ_W) * PROJ_W


def _sigmoid(x):
    return 1.0 / (1.0 + jnp.exp(-x))


def _nt(a, b):
    return lax.dot_general(a, b, (((1,), (1,)), ((), ())), preferred_element_type=F32)


def _nn(a, b):
    return jnp.dot(a, b, preferred_element_type=F32)


def _rope_tile(a, c, s1, s2):
    return a * c + pltpu.roll(a, HALF, 1) * s1 + pltpu.roll(a, LANES - HALF, 1) * s2


def _swap_halves(a):
    return pltpu.roll(a, LANES // 2, 1)


def _rope_tables(pos, rotary_lanes):
    pos = np.asarray(pos, np.float64)
    lane = np.arange(LANES)
    within = lane % NSA_DH
    freq = ROPE_THETA ** (-(within % HALF).astype(np.float64) / HALF)
    ang = pos[:, None] * freq[None, :]
    cos, sin = np.cos(ang), np.sin(ang)
    first = (within < HALF)[None, :]
    rot = np.asarray(rotary_lanes, bool)[None, :]
    c = np.where(rot, cos, 1.0)
    s1 = np.where(rot & ~first, sin, 0.0)
    s2 = np.where(rot & first, -sin, 0.0)
    return (jnp.asarray(c, F32), jnp.asarray(s1, F32), jnp.asarray(s2, F32))


def _block_onehot_table(seq):
    t = np.zeros((seq, LANES), np.float32)
    pos = np.arange(seq)
    t[pos, NSA_DH + pos // SLC_LEN] = 1.0
    return jnp.asarray(t)


def _retention_tables():
    h = jnp.arange(RET_HEADS, dtype=F32)
    log_g = jnp.log(1.0 - 2.0 ** (-5.0 - h))
    c = RET_CHUNK
    idx = jnp.arange(c, dtype=F32)
    diff = idx[:, None] - idx[None, :]
    causal = diff >= 0
    decay = jnp.where(causal, jnp.exp(log_g[:, None, None] * jnp.where(causal, diff, 0.0)), 0.0)
    zeta = jnp.exp(log_g[:, None] * (c - 1.0 - idx))
    xi = jnp.exp(log_g[:, None] * (idx + 1.0))
    g_chunk = jnp.exp(log_g * c)

    def pair_lanes(t):
        t = t.reshape(RET_HEADS // 2, 2, c)
        return jnp.repeat(jnp.transpose(t, (0, 2, 1)), RET_DK, axis=2)

    g_b = jnp.broadcast_to(g_chunk[:, None, None], (RET_HEADS, 1, LANES))
    return decay, pair_lanes(zeta), pair_lanes(xi), g_b


def _overlap_t(seq):
    n_c = seq // CMP_STRIDE - CMP_LEN // CMP_STRIDE + 1
    nb = seq // SLC_LEN
    cs = np.arange(n_c) * CMP_STRIDE
    bs = np.arange(nb) * SLC_LEN
    ov = np.maximum(np.minimum(cs[:, None] + CMP_LEN, bs[None] + SLC_LEN)
                    - np.maximum(cs[:, None], bs[None]), 0).astype(np.float64) / CMP_LEN
    ncp = seq // CMP_STRIDE
    ovp = np.zeros((ncp, nb))
    ovp[:n_c] = ov
    return jnp.asarray(ovp.T, BF16)


def _ada_kernel(c_ref, w_ref, b_ref, o_ref):
    c = c_ref[...]
    o_ref[...] = _nn(c * _sigmoid(c), w_ref[...]) + b_ref[...]


def _ada(c, w, b):
    bsz, d = c.shape
    n = w.shape[1]
    tn = 1024
    return pl.pallas_call(
        _ada_kernel,
        grid=(n // tn,),
        in_specs=[pl.BlockSpec((bsz, d), lambda j: (0, 0)),
                  pl.BlockSpec((d, tn), lambda j: (0, j)),
                  pl.BlockSpec((1, tn), lambda j: (0, j))],
        out_specs=pl.BlockSpec((bsz, tn), lambda j: (0, j)),
        out_shape=jax.ShapeDtypeStruct((bsz, n), F32),
        compiler_params=pltpu.CompilerParams(vmem_limit_bytes=VMEM_LIMIT),
        name="ada",
    )(c, w, b.reshape(1, n))


def _inproj_kernel(x_ref, g_ref, mod_ref, w_ref, c_ref, s1_ref, s2_ref, hot_ref,
                   rq_ref, rk_ref, rv_ref, rg_ref, nq_ref, kc_ref, vc_ref,
                   ksx_ref, vst_ref, kwx_ref, vwt_ref, gate_ref, wb_ref):
    @pl.when((pl.program_id(0) == 0) & (pl.program_id(1) == 0))
    def _():
        n_in = w_ref.shape[2]
        whole = (n_in // PROJ_W) * PROJ_W
        for c0 in range(0, whole, PROJ_W):
            wb_ref[:, c0:c0 + PROJ_W] = w_ref[0, :, c0:c0 + PROJ_W].astype(BF16)
        wb_ref[:, whole:] = jnp.zeros((wb_ref.shape[0], wb_ref.shape[1] - whole), BF16)
        wb_ref[:, whole:n_in] = w_ref[0, :, whole:n_in].astype(BF16)

    def modulated_norm(rows):
        x = x_ref[0, rows, :]
        y = x * lax.rsqrt(jnp.mean(x * x, axis=-1, keepdims=True) + EPS) * g_ref[...]
        return (y * (1.0 + mod_ref[0, 1:2, :]) + mod_ref[0, 0:1, :]).astype(BF16)

    def product_steps(rows, hb):
        tabs = (c_ref[rows, :], s1_ref[rows, :], s2_ref[rows, :])
        low = lax.broadcasted_iota(jnp.int32, (rows.stop - rows.start, LANES), 1) < NSA_DH

        def proj(c0):
            a = _nn(hb, wb_ref[:, c0:c0 + PROJ_W])
            return [a[:, t * LANES:(t + 1) * LANES] for t in range(PROJ_W // LANES)]

        def put(out_ref, t, value):
            out_ref[0, rows, t * LANES:(t + 1) * LANES] = value.astype(out_ref.dtype)

        def roped(tiles, scale, out_ref):
            for t, a in enumerate(tiles):
                r = _rope_tile(a, *tabs)
                put(out_ref, t, r if scale == 1.0 else r * scale)

        def per_group(tile, fill, out_ref):
            put(out_ref, 0, jnp.where(low, tile, fill))
            put(out_ref, 1, jnp.where(low, _swap_halves(tile), fill))

        def per_group_t(tile, out_ref):
            t = tile.T
            ones = jnp.ones((ONES_ROWS, t.shape[1]), out_ref.dtype)
            for g in range(NSA_GROUPS):
                out_ref[0, g, 0:NSA_DH, rows] = t[g * NSA_DH:(g + 1) * NSA_DH].astype(out_ref.dtype)
                out_ref[0, g, NSA_DH:NSA_DH + ONES_ROWS, rows] = ones

        def retention_qk():
            tiles = proj(_R_RQ)
            roped(tiles[0:2], 1.0, rq_ref)
            roped(tiles[2:4], RET_DK ** -0.5, rk_ref)

        def retention_v():
            for t, a in enumerate(proj(_R_RV)):
                put(rv_ref, t, a)

        def retention_gate():
            for t, a in enumerate(proj(_R_RG)):
                put(rg_ref, t, a * _sigmoid(a))

        def nsa_q():
            roped(proj(_R_NQ), Q_SCALE, nq_ref)

        def nsa_compress_selected():
            kc_t, vc_t, ks_t, vs_t = proj(_R_KC)
            kc_ref[0, rows, :] = kc_t
            vc_ref[0, rows, :] = vc_t
            per_group(_rope_tile(ks_t, *tabs), hot_ref[rows, :], ksx_ref)
            per_group_t(vs_t, vst_ref)

        def nsa_window_gates():
            kw_t, vw_t, gates_t, _ = proj(_R_KW)
            per_group(_rope_tile(kw_t, *tabs), 0.0, kwx_ref)
            per_group_t(vw_t, vwt_ref)
            gate_ref[0, :, rows] = _sigmoid(gates_t.T[0:GATE_ROWS])

        return [retention_qk, retention_v, retention_gate, nsa_q, nsa_compress_selected, nsa_window_gates]

    tm = x_ref.shape[1]
    rows_a, rows_b = slice(0, tm // 2), slice(tm // 2, tm)
    steps_a = product_steps(rows_a, modulated_norm(rows_a))
    steps_a[0]()
    steps_b = product_steps(rows_b, modulated_norm(rows_b))
    for step in steps_a[1:] + steps_b:
        step()


def _in_proj(x, ln_g, mod, w_in, tabs, hot):
    b, s, d = x.shape
    tm = TM_IN
    grid = (b, s // tm)
    tab_spec = pl.BlockSpec((tm, LANES), lambda bi, j: (j, 0))

    def out(n, dtype):
        return (jax.ShapeDtypeStruct((b, s, n), dtype), pl.BlockSpec((1, tm, n), lambda bi, j: (bi, j, 0)))

    def out_t(rows, dtype):
        return (jax.ShapeDtypeStruct((b, NSA_GROUPS, rows, s), dtype),
                pl.BlockSpec((1, NSA_GROUPS, rows, tm), lambda bi, j: (bi, 0, 0, j)))

    vt_rows = NSA_DH + ONES_ROWS
    outs = [out(256, BF16), out(256, BF16), out(512, BF16), out(512, BF16), out(512, BF16),
            out(LANES, F32), out(LANES, F32),
            out(256, BF16), out_t(vt_rows, BF16), out(256, BF16), out_t(vt_rows, BF16),
            (jax.ShapeDtypeStruct((b, GATE_ROWS, s), F32),
             pl.BlockSpec((1, GATE_ROWS, tm), lambda bi, j: (bi, 0, j)))]
    return pl.pallas_call(
        _inproj_kernel,
        grid=grid,
        in_specs=[pl.BlockSpec((1, tm, d), lambda bi, j: (bi, j, 0)),
                  pl.BlockSpec((1, d), lambda bi, j: (0, 0)),
                  pl.BlockSpec((1, 6, d), lambda bi, j: (bi, 0, 0)),
                  pl.BlockSpec((1,) + w_in.shape[1:], lambda bi, j: (0, 0, 0), pipeline_mode=pl.Buffered(1)),
                  tab_spec, tab_spec, tab_spec, tab_spec],
        out_specs=[o[1] for o in outs],
        out_shape=[o[0] for o in outs],
        scratch_shapes=[pltpu.VMEM((d, IN_COLS_K), BF16)],
        compiler_params=pltpu.CompilerParams(
            dimension_semantics=("arbitrary", "arbitrary"), vmem_limit_bytes=VMEM_LIMIT),
        name="in_proj",
    )(x, ln_g.reshape(1, d), mod, w_in, *tabs, hot)


def _ret_kernel(q_ref, k_ref, v_ref, rg_ref, dec_ref, zeta_ref, xi_ref, gch_ref, o_ref, kv_ref, prev_ref):
    c = RET_CHUNK
    n_chunks = q_ref.shape[1] // c
    low = lax.broadcasted_iota(jnp.int32, (c, LANES), 1) < RET_DK

    def chunk_rows(n):
        return pl.ds(pl.multiple_of(n * c, c), c)

    def head_cols(h):
        return slice(h * RET_DV, (h + 1) * RET_DV)

    def kv_body(it, carry):
        kz_t = {}
        for j in range(RET_GROUP):
            rows = chunk_rows(it * RET_GROUP + j)
            for p in range(RET_HEADS // 2):
                pair = slice(p * LANES, (p + 1) * LANES)
                kz_t[j, p] = (k_ref[0, rows, pair].astype(F32) * zeta_ref[p]).T.astype(BF16)
        for j in range(RET_GROUP):
            n = it * RET_GROUP + j
            for h in range(RET_HEADS):
                kv_ref[h, n] = _nn(kz_t[j, h // 2], v_ref[0, chunk_rows(n), head_cols(h)])
        return carry

    lax.fori_loop(0, n_chunks // RET_GROUP, kv_body, 0)

    for h in range(RET_HEADS):
        def scan_body(n, st, h=h):
            prev_ref[h, n] = st.astype(prev_ref.dtype)
            return st * gch_ref[h] + kv_ref[h, n]
        lax.fori_loop(0, n_chunks, scan_body, jnp.zeros((LANES, RET_DV), F32))

    def out_body(it, carry):
        chains = [(j, h) for j in range(RET_GROUP) for h in range(RET_HEADS)]
        chunk = lambda j: it * RET_GROUP + j
        q_own, qx_own, att, ys = {}, {}, {}, {}
        for j, h in chains:
            p, e = divmod(h, 2)
            pair = slice(p * LANES, (p + 1) * LANES)
            mine = low if e == 0 else jnp.logical_not(low)
            q2 = q_ref[0, chunk_rows(chunk(j)), pair].astype(F32)
            q_own[j, h] = jnp.where(mine, q2, 0.0).astype(BF16)
            qx_own[j, h] = jnp.where(mine, q2 * xi_ref[p], 0.0).astype(BF16)
        for j, h in chains:
            pair = slice((h // 2) * LANES, (h // 2 + 1) * LANES)
            att[j, h] = _nt(q_own[j, h], k_ref[0, chunk_rows(chunk(j)), pair])
        for j, h in chains:
            lhs = jnp.concatenate([(att[j, h] * dec_ref[h]).astype(BF16), qx_own[j, h]], axis=1)
            rhs = jnp.concatenate([v_ref[0, chunk_rows(chunk(j)), head_cols(h)], prev_ref[h, chunk(j)]], axis=0)
            ys[j, h] = _nn(lhs, rhs)
        for j, h in chains:
            y = ys[j, h]
            yn = y * lax.rsqrt(jnp.mean(y * y, axis=-1, keepdims=True) + EPS)
            gate = rg_ref[0, chunk_rows(chunk(j)), head_cols(h)].astype(F32)
            o_ref[0, chunk_rows(chunk(j)), head_cols(h)] = (yn * gate).astype(o_ref.dtype)
        return carry

    lax.fori_loop(0, n_chunks // RET_GROUP, out_body, 0)


def _retention(rq, rk, rv, rg_act, tables):
    b, s, _ = rq.shape
    decay, zeta_p, xi_p, g_b = tables
    whole = lambda a: pl.BlockSpec(a.shape, lambda bi: (0,) * a.ndim)
    row = lambda a: pl.BlockSpec((1,) + a.shape[1:], lambda bi: (bi, 0, 0))
    return pl.pallas_call(
        _ret_kernel,
        grid=(b,),
        in_specs=[row(rq), row(rk), row(rv), row(rg_act),
                  whole(decay), whole(zeta_p), whole(xi_p), whole(g_b)],
        out_specs=row(rv),
        out_shape=jax.ShapeDtypeStruct(rv.shape, BF16),
        scratch_shapes=[pltpu.VMEM((RET_HEADS, s // RET_CHUNK, LANES, RET_DV), F32),
                        pltpu.VMEM((RET_HEADS, s // RET_CHUNK, LANES, RET_DV), BF16)],
        compiler_params=pltpu.CompilerParams(
            dimension_semantics=("parallel",), vmem_limit_bytes=VMEM_LIMIT),
        name="retention",
    )(rq, rk, rv, rg_act, decay, zeta_p, xi_p, g_b)


def _cmp_kernel(kc_ref, vc_ref, w1_ref, pe_ref, w2_ref, c_ref, s1_ref, s2_ref, kv_ref, vt_ref):
    n_piece = kc_ref.shape[1] // CMP_STRIDE
    hid_w = w1_ref.shape[3]
    out = jnp.zeros((n_piece, kv_ref.shape[2]), F32)
    for t, src in enumerate((kc_ref, vc_ref)):
        a_lo = jnp.zeros((n_piece, hid_w), F32)
        a_hi = jnp.zeros((n_piece, hid_w), F32)
        for r in range(CMP_STRIDE):
            xr = src[0, pl.ds(r, n_piece, stride=CMP_STRIDE), :]
            lo, hi = r, CMP_STRIDE + r
            a_lo = a_lo + _nn((xr + pe_ref[t, lo:lo + 1, :]).astype(BF16), w1_ref[t, lo])
            a_hi = a_hi + _nn((xr + pe_ref[t, hi:hi + 1, :]).astype(BF16), w1_ref[t, hi])
        hid = a_lo + pltpu.roll(a_hi, n_piece - 1, 0)
        out = out + _nn((hid * _sigmoid(hid)).astype(BF16), w2_ref[t])
    for t in range(out.shape[1] // LANES):
        sl = slice(t * LANES, (t + 1) * LANES)
        kv = _rope_tile(out[:, sl], c_ref[...], s1_ref[...], s2_ref[...])
        kv_ref[0, :, sl] = kv.astype(kv_ref.dtype)
        vt_ref[0, t] = kv.T[NSA_DH:2 * NSA_DH].astype(vt_ref.dtype)


def _compress_weights(w1_k, w2_k, w1_v, w2_v, pe_k, pe_v):
    dh, hid = NSA_DH, CMP_HIDDEN

    def first(w1):
        w = w1.reshape(CMP_LEN, dh, hid)
        z = jnp.zeros_like(w)
        return jnp.concatenate([jnp.concatenate([w, z], axis=2), jnp.concatenate([z, w], axis=2)], axis=1)

    def second(w2, off):
        z = jnp.zeros_like(w2)
        rows = []
        for g in range(NSA_GROUPS):
            c = [z, z, z, z]
            c[2 * g + off] = w2
            rows.append(jnp.concatenate(c, axis=1))
        return jnp.concatenate(rows, axis=0)

    w1 = jnp.stack([first(w1_k), first(w1_v)]).astype(BF16)
    w2 = jnp.stack([second(w2_k, 0), second(w2_v, 1)]).astype(BF16)
    pe = jnp.stack([jnp.concatenate([pe_k, pe_k], axis=1), jnp.concatenate([pe_v, pe_v], axis=1)])
    return w1, w2, pe


def _compress(kc, vc, w1, w2, pe, tabs_cmp):
    b, s, w = kc.shape
    n_piece = s // CMP_STRIDE
    const2 = lambda bi: (0, 0)
    src = pl.BlockSpec((1, s, w), lambda bi: (bi, 0, 0))
    dst = pl.BlockSpec((1, n_piece, 2 * w), lambda bi: (bi, 0, 0))
    shape = jax.ShapeDtypeStruct((b, n_piece, 2 * w), BF16)
    return pl.pallas_call(
        _cmp_kernel,
        grid=(b,),
        in_specs=[src, src,
                  pl.BlockSpec(w1.shape, lambda bi: (0, 0, 0, 0)),
                  pl.BlockSpec(pe.shape, lambda bi: (0, 0, 0)),
                  pl.BlockSpec(w2.shape, lambda bi: (0, 0, 0)),
                  pl.BlockSpec((n_piece, LANES), const2),
                  pl.BlockSpec((n_piece, LANES), const2),
                  pl.BlockSpec((n_piece, LANES), const2)],
        out_specs=[dst, pl.BlockSpec((1, NSA_GROUPS, NSA_DH, n_piece), lambda bi: (bi, 0, 0, 0))],
        out_shape=[shape, jax.ShapeDtypeStruct((b, NSA_GROUPS, NSA_DH, n_piece), BF16)],
        compiler_params=pltpu.CompilerParams(
            dimension_semantics=("parallel",), vmem_limit_bytes=VMEM_LIMIT),
        name="compress",
    )(kc, vc, w1, pe, w2, *tabs_cmp)


def _nsa_kernel(q_ref, kcmp_ref, vct_ref, ksx_ref, vst_ref, kwx_ref, vwt_ref, gate_ref, ovt_ref, o_ref):
    tq = TQ
    seq = q_ref.shape[1]
    nb = seq // SLC_LEN
    n_cmp = kcmp_ref.shape[1]
    hpg, dh = NSA_HPG, NSA_DH
    assert WIN % tq == 0 and seq % tq == 0
    group = pl.program_id(1)

    low = lax.broadcasted_iota(jnp.int32, (tq, LANES), 1) < dh
    eye = jnp.where(lax.broadcasted_iota(jnp.int32, (tq, tq), 0)
                    == lax.broadcasted_iota(jnp.int32, (tq, tq), 1), 1.0, 0.0).astype(BF16)
    blk = lax.broadcasted_iota(jnp.int32, (nb, tq), 0)
    col = lax.broadcasted_iota(jnp.int32, (nb, tq), 1)
    crow = lax.broadcasted_iota(jnp.int32, (n_cmp, tq), 0)
    ccol = lax.broadcasted_iota(jnp.int32, (n_cmp, tq), 1)
    kcm = kcmp_ref[0]
    vct = vct_ref[0, 0]
    ovt = ovt_ref[...]

    def per_head(x):
        return jnp.concatenate([x] * hpg, axis=1)

    key_off = lax.broadcasted_iota(jnp.int32, (tq, tq), 0)
    qry_off = lax.broadcasted_iota(jnp.int32, (tq, tq), 1)
    not_after = per_head(jnp.where(key_off <= qry_off, 0.0, NEG))
    inside_win = per_head(jnp.where(key_off > qry_off, 0.0, NEG))

    def masked(s, first_key, t0, windowed):
        blocks = []
        for r in range(0, s.shape[0], tq):
            blk_s = s[r:r + tq]
            if first_key + r == t0:
                blk_s = blk_s + not_after
            elif windowed and first_key + r == t0 - WIN:
                blk_s = blk_s + inside_win
            blocks.append(blk_s)
        return jnp.concatenate(blocks, axis=0)

    def normalise(acc):
        return acc[0:dh] / acc[dh:dh + 1]

    def select_blocks(psum, t0):
        tcol = col + t0
        bcausal = blk * SLC_LEN <= tcol
        n_live = (t0 + tq - 1) // SLC_LEN + 1
        top_n = min(SLC_TOPK, nb)
        if n_live <= top_n:
            bias = jnp.where(bcausal, 0.0, NEG)
        else:
            p_hi = psum.astype(BF16)
            p_lo = (psum - p_hi.astype(F32)).astype(BF16)
            imp = _nn(ovt, p_hi) + _nn(ovt, p_lo)
            cur = tcol // SLC_LEN
            forced = (blk == 0) | (blk == cur) | (blk == cur - 1)
            imp = jnp.where(bcausal, jnp.where(forced, FORCE, imp), NEG)
            rank = jnp.zeros((nb, tq), F32)
            for j in range(n_live):
                r = imp[j:j + 1, :]
                rank = rank + jnp.where(blk > j, jnp.where(r >= imp, 1.0, 0.0), jnp.where(r > imp, 1.0, 0.0))
            bias = jnp.where((rank < float(top_n)) & bcausal, 0.0, NEG)
        feat = jnp.concatenate([jnp.zeros((dh, tq), F32), bias,
                                jnp.zeros((LANES - dh - nb, tq), F32)], axis=0).astype(BF16)
        return _nt(eye, feat)

    class TileGroup:
        def __init__(self, tiles):
            self.tiles = tiles
            self.wstart = {t0: max(t0 - WIN, 0) for t0 in tiles}
            self.wkeys = {t0: slice(self.wstart[t0], t0 + tq) for t0 in tiles}
            self.skeys = {t0: slice(0, t0 + tq) for t0 in tiles}

        def scores_window_compressed(self):
            self.heads, self.qs = {}, {}
            for t0 in self.tiles:
                qf = q_ref[0, t0:t0 + tq, :].astype(F32)
                hl = []
                for hh in range(hpg):
                    t = qf[:, (hh // 2) * LANES:(hh // 2 + 1) * LANES]
                    if hh % 2 == 1:
                        t = _swap_halves(t)
                    hl.append(jnp.where(low, t, 0.0))
                self.heads[t0] = hl
                self.qs[t0] = jnp.concatenate(hl, axis=0).astype(BF16)
            self.s_w = {t0: _nt(kwx_ref[0, self.wkeys[t0], :], self.qs[t0]) for t0 in self.tiles}
            self.s_c = {t0: _nt(kcm, self.qs[t0]) for t0 in self.tiles}

        def select_and_scores_selected(self):
            self.p_cmp, qsel = {}, {}
            for t0 in self.tiles:
                cmask = (crow * CMP_STRIDE + (CMP_LEN - 1)) <= (ccol + t0)
                p_all = []
                psum = jnp.zeros((n_cmp, tq), F32)
                for hh in range(hpg):
                    sh = jnp.where(cmask, self.s_c[t0][:, hh * tq:(hh + 1) * tq], NEG)
                    e = jnp.exp2(sh - jnp.max(sh, axis=0, keepdims=True))
                    p = jnp.where(cmask, e / jnp.sum(e, axis=0, keepdims=True), 0.0)
                    psum = psum + p
                    p_all.append(p.astype(BF16))
                self.p_cmp[t0] = jnp.concatenate(p_all, axis=1)
                qbias = select_blocks(psum, t0)
                qsel[t0] = jnp.concatenate([hd + qbias for hd in self.heads[t0]], axis=0).astype(BF16)
            self.s_s = {t0: _nt(ksx_ref[0, self.skeys[t0], :], qsel[t0]) for t0 in self.tiles}

        def outputs_window_compressed(self):
            e_w = {}
            for t0 in self.tiles:
                sw = masked(self.s_w[t0], self.wstart[t0], t0, True)
                e_w[t0] = jnp.exp2(sw - jnp.max(sw, axis=0, keepdims=True)).astype(BF16)
            self.o_win = {t0: normalise(_nn(vwt_ref[0, 0, :, self.wkeys[t0]], e_w[t0])) for t0 in self.tiles}
            self.o_cmp = {t0: _nn(vct, self.p_cmp[t0]) for t0 in self.tiles}

        def outputs_selected_and_store(self):
            e_s = {}
            for t0 in self.tiles:
                ss = masked(self.s_s[t0], 0, t0, False)
                e_s[t0] = jnp.exp2(ss - jnp.max(ss, axis=0, keepdims=True)).astype(BF16)
            o_sel = {t0: normalise(_nn(vst_ref[0, 0, :, self.skeys[t0]], e_s[t0])) for t0 in self.tiles}
            for t0 in self.tiles:
                gt = gate_ref[0, :, t0:t0 + tq]
                outs = []
                for hh in range(hpg):
                    hc = slice(hh * tq, (hh + 1) * tq)

                    def gate_row(branch):
                        by_group = [gt[branch * NSA_HEADS + g * hpg + hh:branch * NSA_HEADS + g * hpg + hh + 1, :]
                                    for g in range(NSA_GROUPS)]
                        row = by_group[-1]
                        for g in range(NSA_GROUPS - 2, -1, -1):
                            row = jnp.where(group == g, by_group[g], row)
                        return row

                    outs.append(gate_row(0) * self.o_cmp[t0][:, hc] + gate_row(1) * o_sel[t0][:, hc]
                                + gate_row(2) * self.o_win[t0][:, hc])
                o_ref[0, t0:t0 + tq, :] = jnp.concatenate(outs, axis=0).T.astype(o_ref.dtype)

    starts = list(range(0, seq, tq))
    groups = [TileGroup(starts[i:i + TILE_GROUP]) for i in range(0, len(starts), TILE_GROUP)]
    n = len(groups)
    groups[0].scores_window_compressed()
    for i, g in enumerate(groups):
        if i + 1 < n:
            groups[i + 1].scores_window_compressed()
        g.select_and_scores_selected()
        g.outputs_window_compressed()
        if i > 0:
            groups[i - 1].outputs_selected_and_store()
    groups[n - 1].outputs_selected_and_store()


def _nsa_attention(nq, kvcmp, vct, ksx, vst, kwx, vwt, gate_t, ovt):
    b, s, _ = nq.shape
    n_cmp = kvcmp.shape[1]
    gw = NSA_HPG * NSA_DH
    per_group = lambda rows, width: pl.BlockSpec((1, rows, width), lambda bi, g: (bi, 0, g))
    per_group_t = lambda a: pl.BlockSpec((1, 1) + a.shape[2:], lambda bi, g: (bi, g, 0, 0))
    return pl.pallas_call(
        _nsa_kernel,
        grid=(b, NSA_GROUPS),
        in_specs=[per_group(s, gw),
                  per_group(n_cmp, LANES), per_group_t(vct),
                  per_group(s, LANES), per_group_t(vst), per_group(s, LANES), per_group_t(vwt),
                  pl.BlockSpec((1,) + gate_t.shape[1:], lambda bi, g: (bi, 0, 0)),
                  pl.BlockSpec(ovt.shape, lambda bi, g: (0, 0))],
        out_specs=per_group(s, gw),
        out_shape=jax.ShapeDtypeStruct((b, s, NSA_HEADS * NSA_DH), BF16),
        compiler_params=pltpu.CompilerParams(
            dimension_semantics=("parallel", "parallel"), vmem_limit_bytes=VMEM_LIMIT),
        name="nsa_attn",
    )(nq, kvcmp, vct, ksx, vst, kwx, vwt, gate_t, ovt)


def _ffn_kernel(x_ref, yr_ref, yn_ref, mod_ref, wo_ref, g2_ref, gf_ref, wg_ref, wu_ref, wd_ref,
                o_ref, x1_ref, act_ref):
    half_w = yr_ref.shape[1]
    d_ff = wg_ref.shape[1]
    n_chunks = d_ff // TF
    rows_a = slice(0, x_ref.shape[0] // 2)
    rows_b = slice(x_ref.shape[0] // 2, x_ref.shape[0])

    def mix(rows):
        return _nn(yr_ref[rows, :], wo_ref[0:half_w, :]) + _nn(yn_ref[rows, :], wo_ref[half_w:2 * half_w, :])

    def mid_norm(rows, mixed):
        x1 = x_ref[rows, :] + mod_ref[0, 2:3, :] * mixed
        x1_ref[rows, :] = x1
        y = x1 * lax.rsqrt(jnp.mean(x1 * x1, axis=-1, keepdims=True) + EPS) * g2_ref[...]
        return (y * (1.0 + mod_ref[0, 4:5, :]) + mod_ref[0, 3:4, :]).astype(BF16)

    def ff_chunk(rows, h2, j):
        sl = slice(j * TF, (j + 1) * TF)
        gate = _nn(h2, wg_ref[:, sl])
        up = _nn(h2, wu_ref[:, sl])
        act_ref[rows, sl] = (gate * _sigmoid(gate) * up).astype(BF16)

    def down(rows):
        return x1_ref[rows, :] + mod_ref[0, 5:6, :] * _nn(act_ref[rows, :], wd_ref[...])

    def final_norm(rows, xo):
        o_ref[rows, :] = xo * lax.rsqrt(jnp.mean(xo * xo, axis=-1, keepdims=True) + EPS) * gf_ref[...]

    mix_a = mix(rows_a)
    mix_b = mix(rows_b)
    h2_a = mid_norm(rows_a, mix_a)
    ff_chunk(rows_a, h2_a, 0)
    h2_b = mid_norm(rows_b, mix_b)
    for j in range(1, n_chunks):
        ff_chunk(rows_a, h2_a, j)
    xo_a = down(rows_a)
    ff_chunk(rows_b, h2_b, 0)
    final_norm(rows_a, xo_a)
    for j in range(1, n_chunks):
        ff_chunk(rows_b, h2_b, j)
    final_norm(rows_b, down(rows_b))


def _out_ffn(x2d, y_ret, y_nsa, mod, w_out, g2, gf, wg, wu, wd, seq):
    n, d = x2d.shape
    tm = TM_FF
    d_ff = wg.shape[1]
    tiles_per_seq = seq // tm
    half_w = y_ret.shape[1]
    row = lambda i: (i, 0)
    resident = lambda a: pl.BlockSpec(a.shape, lambda i: (0, 0), pipeline_mode=pl.Buffered(1))
    return pl.pallas_call(
        _ffn_kernel,
        grid=(n // tm,),
        in_specs=[pl.BlockSpec((tm, d), row),
                  pl.BlockSpec((tm, half_w), row),
                  pl.BlockSpec((tm, half_w), row),
                  pl.BlockSpec((1, 6, d), lambda i: (i // tiles_per_seq, 0, 0)),
                  resident(w_out),
                  pl.BlockSpec((1, d), lambda i: (0, 0)),
                  pl.BlockSpec((1, d), lambda i: (0, 0)),
                  resident(wg), resident(wu), resident(wd)],
        out_specs=pl.BlockSpec((tm, d), row),
        out_shape=jax.ShapeDtypeStruct((n, d), F32),
        scratch_shapes=[pltpu.VMEM((tm, d), F32), pltpu.VMEM((tm, d_ff), BF16)],
        compiler_params=pltpu.CompilerParams(
            dimension_semantics=("parallel",), vmem_limit_bytes=VMEM_LIMIT),
        name="out_ffn",
    )(x2d, y_ret, y_nsa, mod, w_out, g2.reshape(1, d), gf.reshape(1, d), wg, wu, wd)


def kernel(x, c, ln_mix_g, ln_ffn_g, w_ada, b_ada, w_in, cmp_pe_k, cmp_w1_k, cmp_w2_k,
           cmp_pe_v, cmp_w1_v, cmp_w2_v, w_out, w_ff_gate, w_ff_up, w_ff_down, ln_final_g):
    assert w_in.shape[0] == 1, "the final RMSNorm is fused into the (single) layer's FFN kernel"
    b, s, d = x.shape
    lane = np.arange(LANES)
    tabs = _rope_tables(np.arange(s), np.ones(LANES, bool))
    n_piece = s // CMP_STRIDE
    tabs_cmp = _rope_tables(np.arange(n_piece) * CMP_STRIDE + CMP_LEN - 1, lane < NSA_DH)

    mod = _ada(c, w_ada[0], b_ada[0]).reshape(b, 6, d)
    rq, rk, rv, rg_act, nq, kc, vc, ksx, vst, kwx, vwt, gate_t = _in_proj(
        x, ln_mix_g[0], mod, w_in, tabs, _block_onehot_table(s))
    y_ret = _retention(rq, rk, rv, rg_act, _retention_tables())
    w1, w2, pe = _compress_weights(cmp_w1_k[0], cmp_w2_k[0], cmp_w1_v[0], cmp_w2_v[0],
                                   cmp_pe_k[0], cmp_pe_v[0])
    kvcmp, vct = _compress(kc, vc, w1, w2, pe, tabs_cmp)
    y_nsa = _nsa_attention(nq, kvcmp, vct, ksx, vst, kwx, vwt, gate_t, _overlap_t(s))
    out = _out_ffn(x.reshape(b * s, d), y_ret.reshape(b * s, -1), y_nsa.reshape(b * s, -1), mod,
                   w_out[0].astype(BF16), ln_ffn_g[0], ln_final_g,
                   w_ff_gate[0].astype(BF16), w_ff_up[0].astype(BF16), w_ff_down[0].astype(BF16), s)
    return out.reshape(b, s, d)
```

```python
import numpy as np
import jax
import jax.numpy as jnp
from jax import lax
from jax.experimental import pallas as pl
from jax.experimental.pallas import tpu as pltpu

F32 = jnp.float32
BF16 = jnp.bfloat16

D_MODEL = 1024
RET_HEADS = 4
RET_DK = 64
RET_DV = 128
RET_CHUNK = 128
NSA_HEADS = 8
NSA_GROUPS = 2
NSA_HPG = NSA_HEADS // NSA_GROUPS
NSA_DH = 64
CMP_LEN = 32
CMP_STRIDE = 16
CMP_HIDDEN = 128
SLC_LEN = 64
SLC_TOPK = 16
WIN = 512
D_FF = ((8 * D_MODEL + 3 * 256 - 1) // (3 * 256)) * 256
ROPE_THETA = 10000.0
EPS = 1e-6
NEG = -1e30
FORCE = 1e6

LANES = 128
SUBLANES = 8
HALF = NSA_DH // 2
ONES_ROWS = 16
GATE_ROWS = 3 * NSA_HEADS

TM_IN = 1024
PROJ_W = 512
RET_GROUP = 8
Q_SCALE = NSA_DH ** -0.5 * float(np.log2(np.e))
TQ = 256
TILE_GROUP = 2
TM_FF = 1024
TF = 256
VMEM_LIMIT = 56 * 1024 * 1024

_R_RQ, _R_RK, _R_RV, _R_RG, _R_NQ = 0, 256, 512, 1024, 1536
_R_KC, _R_VC, _R_KS, _R_VS, _R_KW, _R_VW, _R_GATE = 2048, 2176, 2304, 2432, 2560, 2688, 2816
IN_COLS_K = -(-(_R_GATE + 3 * NSA_HEADS) // PROJ_W) * PROJ_W


def _sigmoid(x):
    return 1.0 / (1.0 + jnp.exp(-x))


def _nt(a, b):
    return lax.dot_general(a, b, (((1,), (1,)), ((), ())), preferred_element_type=F32)


def _nn(a, b):
    return jnp.dot(a, b, preferred_element_type=F32)


def _rope_tile(a, c, s1, s2):
    return a * c + pltpu.roll(a, HALF, 1) * s1 + pltpu.roll(a, LANES - HALF, 1) * s2


def _swap_halves(a):
    return pltpu.roll(a, LANES // 2, 1)


def _rope_tables(pos, rotary_lanes):
    pos = np.asarray(pos, np.float64)
    lane = np.arange(LANES)
    within = lane % NSA_DH
    freq = ROPE_THETA ** (-(within % HALF).astype(np.float64) / HALF)
    ang = pos[:, None] * freq[None, :]
    cos, sin = np.cos(ang), np.sin(ang)
    first = (within < HALF)[None, :]
    rot = np.asarray(rotary_lanes, bool)[None, :]
    c = np.where(rot, cos, 1.0)
    s1 = np.where(rot & ~first, sin, 0.0)
    s2 = np.where(rot & first, -sin, 0.0)
    return (jnp.asarray(c, F32), jnp.asarray(s1, F32), jnp.asarray(s2, F32))


def _block_onehot_table(seq):
    t = np.zeros((seq, LANES), np.float32)
    pos = np.arange(seq)
    t[pos, NSA_DH + pos // SLC_LEN] = 1.0
    return jnp.asarray(t)


def _retention_tables():
    h = jnp.arange(RET_HEADS, dtype=F32)
    log_g = jnp.log(1.0 - 2.0 ** (-5.0 - h))
    c = RET_CHUNK
    idx = jnp.arange(c, dtype=F32)
    diff = idx[:, None] - idx[None, :]
    causal = diff >= 0
    decay = jnp.where(causal, jnp.exp(log_g[:, None, None] * jnp.where(causal, diff, 0.0)), 0.0)
    zeta = jnp.exp(log_g[:, None] * (c - 1.0 - idx))
    xi = jnp.exp(log_g[:, None] * (idx + 1.0))
    g_chunk = jnp.exp(log_g * c)

    def pair_lanes(t):
        t = t.reshape(RET_HEADS // 2, 2, c)
        return jnp.repeat(jnp.transpose(t, (0, 2, 1)), RET_DK, axis=2)

    g_b = jnp.broadcast_to(g_chunk[:, None, None], (RET_HEADS, 1, LANES))
    return decay, pair_lanes(zeta), pair_lanes(xi), g_b


def _overlap_t(seq):
    n_c = seq // CMP_STRIDE - CMP_LEN // CMP_STRIDE + 1
    nb = seq // SLC_LEN
    cs = np.arange(n_c) * CMP_STRIDE
    bs = np.arange(nb) * SLC_LEN
    ov = np.maximum(np.minimum(cs[:, None] + CMP_LEN, bs[None] + SLC_LEN)
                    - np.maximum(cs[:, None], bs[None]), 0).astype(np.float64) / CMP_LEN
    ncp = seq // CMP_STRIDE
    ovp = np.zeros((ncp, nb))
    ovp[:n_c] = ov
    return jnp.asarray(ovp.T, BF16)


def _ada_kernel(c_ref, w_ref, b_ref, o_ref):
    c = c_ref[...]
    o_ref[...] = _nn(c * _sigmoid(c), w_ref[...]) + b_ref[...]


def _ada(c, w, b):
    bsz, d = c.shape
    n = w.shape[1]
    tn = 1024
    return pl.pallas_call(
        _ada_kernel,
        grid=(n // tn,),
        in_specs=[pl.BlockSpec((bsz, d), lambda j: (0, 0)),
                  pl.BlockSpec((d, tn), lambda j: (0, j)),
                  pl.BlockSpec((1, tn), lambda j: (0, j))],
        out_specs=pl.BlockSpec((bsz, tn), lambda j: (0, j)),
        out_shape=jax.ShapeDtypeStruct((bsz, n), F32),
        compiler_params=pltpu.CompilerParams(vmem_limit_bytes=VMEM_LIMIT),
        name="ada",
    )(c, w, b.reshape(1, n))


def _inproj_kernel(x_ref, g_ref, mod_ref, w_ref, c_ref, s1_ref, s2_ref, hot_ref,
                   rq_ref, rk_ref, rv_ref, rg_ref, nq_ref, kc_ref, vc_ref,
                   ksx_ref, vst_ref, kwx_ref, vwt_ref, gate_ref, wb_ref):
    @pl.when((pl.program_id(0) == 0) & (pl.program_id(1) == 0))
    def _():
        n_in = w_ref.shape[2]
        whole = (n_in // PROJ_W) * PROJ_W
        for c0 in range(0, whole, PROJ_W):
            wb_ref[:, c0:c0 + PROJ_W] = w_ref[0, :, c0:c0 + PROJ_W].astype(BF16)
        wb_ref[:, whole:] = jnp.zeros((wb_ref.shape[0], wb_ref.shape[1] - whole), BF16)
        wb_ref[:, whole:n_in] = w_ref[0, :, whole:n_in].astype(BF16)

    def modulated_norm(rows):
        x = x_ref[0, rows, :]
        y = x * lax.rsqrt(jnp.mean(x * x, axis=-1, keepdims=True) + EPS) * g_ref[...]
        return (y * (1.0 + mod_ref[0, 1:2, :]) + mod_ref[0, 0:1, :]).astype(BF16)

    def product_steps(rows, hb):
        tabs = (c_ref[rows, :], s1_ref[rows, :], s2_ref[rows, :])
        low = lax.broadcasted_iota(jnp.int32, (rows.stop - rows.start, LANES), 1) < NSA_DH

        def proj(c0):
            a = _nn(hb, wb_ref[:, c0:c0 + PROJ_W])
            return [a[:, t * LANES:(t + 1) * LANES] for t in range(PROJ_W // LANES)]

        def put(out_ref, t, value):
            out_ref[0, rows, t * LANES:(t + 1) * LANES] = value.astype(out_ref.dtype)

        def roped(tiles, scale, out_ref):
            for t, a in enumerate(tiles):
                r = _rope_tile(a, *tabs)
                put(out_ref, t, r if scale == 1.0 else r * scale)

        def per_group(tile, fill, out_ref):
            put(out_ref, 0, jnp.where(low, tile, fill))
            put(out_ref, 1, jnp.where(low, _swap_halves(tile), fill))

        def per_group_t(tile, out_ref):
            t = tile.T
            ones = jnp.ones((ONES_ROWS, t.shape[1]), out_ref.dtype)
            for g in range(NSA_GROUPS):
                out_ref[0, g, 0:NSA_DH, rows] = t[g * NSA_DH:(g + 1) * NSA_DH].astype(out_ref.dtype)
                out_ref[0, g, NSA_DH:NSA_DH + ONES_ROWS, rows] = ones

        def retention_qk():
            tiles = proj(_R_RQ)
            roped(tiles[0:2], 1.0, rq_ref)
            roped(tiles[2:4], RET_DK ** -0.5, rk_ref)

        def retention_v():
            for t, a in enumerate(proj(_R_RV)):
                put(rv_ref, t, a)

        def retention_gate():
            for t, a in enumerate(proj(_R_RG)):
                put(rg_ref, t, a * _sigmoid(a))

        def nsa_q():
            roped(proj(_R_NQ), Q_SCALE, nq_ref)

        def nsa_compress_selected():
            kc_t, vc_t, ks_t, vs_t = proj(_R_KC)
            kc_ref[0, rows, :] = kc_t
            vc_ref[0, rows, :] = vc_t
            per_group(_rope_tile(ks_t, *tabs), hot_ref[rows, :], ksx_ref)
            per_group_t(vs_t, vst_ref)

        def nsa_window_gates():
            kw_t, vw_t, gates_t, _ = proj(_R_KW)
            per_group(_rope_tile(kw_t, *tabs), 0.0, kwx_ref)
            per_group_t(vw_t, vwt_ref)
            gate_ref[0, :, rows] = _sigmoid(gates_t.T[0:GATE_ROWS])

        return [retention_qk, retention_v, retention_gate, nsa_q, nsa_compress_selected, nsa_window_gates]

    tm = x_ref.shape[1]
    rows_a, rows_b = slice(0, tm // 2), slice(tm // 2, tm)
    steps_a = product_steps(rows_a, modulated_norm(rows_a))
    steps_a[0]()
    steps_b = product_steps(rows_b, modulated_norm(rows_b))
    for step in steps_a[1:] + steps_b:
        step()


def _in_proj(x, ln_g, mod, w_in, tabs, hot):
    b, s, d = x.shape
    tm = TM_IN
    grid = (b, s // tm)
    tab_spec = pl.BlockSpec((tm, LANES), lambda bi, j: (j, 0))

    def out(n, dtype):
        return (jax.ShapeDtypeStruct((b, s, n), dtype), pl.BlockSpec((1, tm, n), lambda bi, j: (bi, j, 0)))

    def out_t(rows, dtype):
        return (jax.ShapeDtypeStruct((b, NSA_GROUPS, rows, s), dtype),
                pl.BlockSpec((1, NSA_GROUPS, rows, tm), lambda bi, j: (bi, 0, 0, j)))

    vt_rows = NSA_DH + ONES_ROWS
    outs = [out(256, BF16), out(256, BF16), out(512, BF16), out(512, BF16), out(512, BF16),
            out(LANES, F32), out(LANES, F32),
            out(256, BF16), out_t(vt_rows, BF16), out(256, BF16), out_t(vt_rows, BF16),
            (jax.ShapeDtypeStruct((b, GATE_ROWS, s), F32),
             pl.BlockSpec((1, GATE_ROWS, tm), lambda bi, j: (bi, 0, j)))]
    return pl.pallas_call(
        _inproj_kernel,
        grid=grid,
        in_specs=[pl.BlockSpec((1, tm, d), lambda bi, j: (bi, j, 0)),
                  pl.BlockSpec((1, d), lambda bi, j: (0, 0)),
                  pl.BlockSpec((1, 6, d), lambda bi, j: (bi, 0, 0)),
                  pl.BlockSpec((1,) + w_in.shape[1:], lambda bi, j: (0, 0, 0), pipeline_mode=pl.Buffered(1)),
                  tab_spec, tab_spec, tab_spec, tab_spec],
        out_specs=[o[1] for o in outs],
        out_shape=[o[0] for o in outs],
        scratch_shapes=[pltpu.VMEM((d, IN_COLS_K), BF16)],
        compiler_params=pltpu.CompilerParams(
            dimension_semantics=("arbitrary", "arbitrary"), vmem_limit_bytes=VMEM_LIMIT),
        name="in_proj",
    )(x, ln_g.reshape(1, d), mod, w_in, *tabs, hot)


def _ret_kernel(q_ref, k_ref, v_ref, rg_ref, dec_ref, zeta_ref, xi_ref, gch_ref, o_ref, kv_ref, prev_ref):
    c = RET_CHUNK
    n_chunks = q_ref.shape[1] // c
    low = lax.broadcasted_iota(jnp.int32, (c, LANES), 1) < RET_DK

    def chunk_rows(n):
        return pl.ds(pl.multiple_of(n * c, c), c)

    def head_cols(h):
        return slice(h * RET_DV, (h + 1) * RET_DV)

    def kv_body(it, carry):
        kz_t = {}
        for j in range(RET_GROUP):
            rows = chunk_rows(it * RET_GROUP + j)
            for p in range(RET_HEADS // 2):
                pair = slice(p * LANES, (p + 1) * LANES)
                kz_t[j, p] = (k_ref[0, rows, pair].astype(F32) * zeta_ref[p]).T.astype(BF16)
        for j in range(RET_GROUP):
            n = it * RET_GROUP + j
            for h in range(RET_HEADS):
                kv_ref[h, n] = _nn(kz_t[j, h // 2], v_ref[0, chunk_rows(n), head_cols(h)])
        return carry

    lax.fori_loop(0, n_chunks // RET_GROUP, kv_body, 0)

    for h in range(RET_HEADS):
        def scan_body(n, st, h=h):
            prev_ref[h, n] = st.astype(prev_ref.dtype)
            return st * gch_ref[h] + kv_ref[h, n]
        lax.fori_loop(0, n_chunks, scan_body, jnp.zeros((LANES, RET_DV), F32))

    def out_body(it, carry):
        chains = [(j, h) for j in range(RET_GROUP) for h in range(RET_HEADS)]
        chunk = lambda j: it * RET_GROUP + j
        q_own, qx_own, att, ys = {}, {}, {}, {}
        for j, h in chains:
            p, e = divmod(h, 2)
            pair = slice(p * LANES, (p + 1) * LANES)
            mine = low if e == 0 else jnp.logical_not(low)
            q2 = q_ref[0, chunk_rows(chunk(j)), pair].astype(F32)
            q_own[j, h] = jnp.where(mine, q2, 0.0).astype(BF16)
            qx_own[j, h] = jnp.where(mine, q2 * xi_ref[p], 0.0).astype(BF16)
        for j, h in chains:
            pair = slice((h // 2) * LANES, (h // 2 + 1) * LANES)
            att[j, h] = _nt(q_own[j, h], k_ref[0, chunk_rows(chunk(j)), pair])
        for j, h in chains:
            lhs = jnp.concatenate([(att[j, h] * dec_ref[h]).astype(BF16), qx_own[j, h]], axis=1)
            rhs = jnp.concatenate([v_ref[0, chunk_rows(chunk(j)), head_cols(h)], prev_ref[h, chunk(j)]], axis=0)
            ys[j, h] = _nn(lhs, rhs)
        for j, h in chains:
            y = ys[j, h]
            yn = y * lax.rsqrt(jnp.mean(y * y, axis=-1, keepdims=True) + EPS)
            gate = rg_ref[0, chunk_rows(chunk(j)), head_cols(h)].astype(F32)
            o_ref[0, chunk_rows(chunk(j)), head_cols(h)] = (yn * gate).astype(o_ref.dtype)
        return carry

    lax.fori_loop(0, n_chunks // RET_GROUP, out_body, 0)


def _retention(rq, rk, rv, rg_act, tables):
    b, s, _ = rq.shape
    decay, zeta_p, xi_p, g_b = tables
    whole = lambda a: pl.BlockSpec(a.shape, lambda bi: (0,) * a.ndim)
    row = lambda a: pl.BlockSpec((1,) + a.shape[1:], lambda bi: (bi, 0, 0))
    return pl.pallas_call(
        _ret_kernel,
        grid=(b,),
        in_specs=[row(rq), row(rk), row(rv), row(rg_act),
                  whole(decay), whole(zeta_p), whole(xi_p), whole(g_b)],
        out_specs=row(rv),
        out_shape=jax.ShapeDtypeStruct(rv.shape, BF16),
        scratch_shapes=[pltpu.VMEM((RET_HEADS, s // RET_CHUNK, LANES, RET_DV), F32),
                        pltpu.VMEM((RET_HEADS, s // RET_CHUNK, LANES, RET_DV), BF16)],
        compiler_params=pltpu.CompilerParams(
            dimension_semantics=("parallel",), vmem_limit_bytes=VMEM_LIMIT),
        name="retention",
    )(rq, rk, rv, rg_act, decay, zeta_p, xi_p, g_b)


def _cmp_kernel(kc_ref, vc_ref, w1_ref, pe_ref, w2_ref, c_ref, s1_ref, s2_ref, kv_ref, vt_ref):
    n_piece = kc_ref.shape[1] // CMP_STRIDE
    halves = CMP_LEN // CMP_STRIDE
    lhs = {}
    for t, src in enumerate((kc_ref, vc_ref)):
        rows = [src[0, pl.ds(r, n_piece, stride=CMP_STRIDE), :] for r in range(CMP_STRIDE)]
        for half in range(halves):
            pe0 = half * CMP_STRIDE
            lhs[t, half] = jnp.concatenate(
                [(rows[r] + pe_ref[t, pe0 + r:pe0 + r + 1, :]).astype(BF16) for r in range(CMP_STRIDE)], axis=1)
    part = {key: _nn(lhs[key], w1_ref[key[0], key[1]]) for key in lhs}
    out = jnp.zeros((n_piece, kv_ref.shape[2]), F32)
    for t in range(2):
        hid = part[t, 0]
        for half in range(1, halves):
            hid = hid + pltpu.roll(part[t, half], n_piece - half, 0)
        out = out + _nn((hid * _sigmoid(hid)).astype(BF16), w2_ref[t])
    for t in range(out.shape[1] // LANES):
        sl = slice(t * LANES, (t + 1) * LANES)
        kv = _rope_tile(out[:, sl], c_ref[...], s1_ref[...], s2_ref[...])
        kv_ref[0, :, sl] = kv.astype(kv_ref.dtype)
        vt_ref[0, t] = kv.T[NSA_DH:2 * NSA_DH].astype(vt_ref.dtype)


def _compress_weights(w1_k, w2_k, w1_v, w2_v, pe_k, pe_v):
    dh, hid = NSA_DH, CMP_HIDDEN

    def first(w1):
        w = w1.reshape(CMP_LEN, dh, hid)
        z = jnp.zeros_like(w)
        return jnp.concatenate([jnp.concatenate([w, z], axis=2), jnp.concatenate([z, w], axis=2)], axis=1)

    def second(w2, off):
        z = jnp.zeros_like(w2)
        rows = []
        for g in range(NSA_GROUPS):
            c = [z, z, z, z]
            c[2 * g + off] = w2
            rows.append(jnp.concatenate(c, axis=1))
        return jnp.concatenate(rows, axis=0)

    w1 = jnp.stack([first(w1_k), first(w1_v)]).astype(BF16).reshape(
        2, CMP_LEN // CMP_STRIDE, CMP_STRIDE * NSA_GROUPS * dh, NSA_GROUPS * hid)
    w2 = jnp.stack([second(w2_k, 0), second(w2_v, 1)]).astype(BF16)
    pe = jnp.stack([jnp.concatenate([pe_k, pe_k], axis=1), jnp.concatenate([pe_v, pe_v], axis=1)])
    return w1, w2, pe


def _compress(kc, vc, w1, w2, pe, tabs_cmp):
    b, s, w = kc.shape
    n_piece = s // CMP_STRIDE
    const2 = lambda bi: (0, 0)
    src = pl.BlockSpec((1, s, w), lambda bi: (bi, 0, 0))
    dst = pl.BlockSpec((1, n_piece, 2 * w), lambda bi: (bi, 0, 0))
    shape = jax.ShapeDtypeStruct((b, n_piece, 2 * w), BF16)
    return pl.pallas_call(
        _cmp_kernel,
        grid=(b,),
        in_specs=[src, src,
                  pl.BlockSpec(w1.shape, lambda bi: (0, 0, 0, 0)),
                  pl.BlockSpec(pe.shape, lambda bi: (0, 0, 0)),
                  pl.BlockSpec(w2.shape, lambda bi: (0, 0, 0)),
                  pl.BlockSpec((n_piece, LANES), const2),
                  pl.BlockSpec((n_piece, LANES), const2),
                  pl.BlockSpec((n_piece, LANES), const2)],
        out_specs=[dst, pl.BlockSpec((1, NSA_GROUPS, NSA_DH, n_piece), lambda bi: (bi, 0, 0, 0))],
        out_shape=[shape, jax.ShapeDtypeStruct((b, NSA_GROUPS, NSA_DH, n_piece), BF16)],
        compiler_params=pltpu.CompilerParams(
            dimension_semantics=("parallel",), vmem_limit_bytes=VMEM_LIMIT),
        name="compress",
    )(kc, vc, w1, pe, w2, *tabs_cmp)


def _nsa_kernel(q_ref, kcmp_ref, vct_ref, ksx_ref, vst_ref, kwx_ref, vwt_ref, gate_ref, ovt_ref, o_ref):
    tq = TQ
    seq = q_ref.shape[1]
    nb = seq // SLC_LEN
    n_cmp = kcmp_ref.shape[1]
    hpg, dh = NSA_HPG, NSA_DH
    assert WIN % tq == 0 and seq % tq == 0
    group = pl.program_id(1)

    low = lax.broadcasted_iota(jnp.int32, (tq, LANES), 1) < dh
    eye = jnp.where(lax.broadcasted_iota(jnp.int32, (tq, tq), 0)
                    == lax.broadcasted_iota(jnp.int32, (tq, tq), 1), 1.0, 0.0).astype(BF16)
    blk = lax.broadcasted_iota(jnp.int32, (nb, tq), 0)
    col = lax.broadcasted_iota(jnp.int32, (nb, tq), 1)
    crow = lax.broadcasted_iota(jnp.int32, (n_cmp, tq), 0)
    ccol = lax.broadcasted_iota(jnp.int32, (n_cmp, tq), 1)
    kcm = kcmp_ref[0]
    vct = vct_ref[0, 0]
    ovt = ovt_ref[...]

    def per_head(x):
        return jnp.concatenate([x] * hpg, axis=1)

    key_off = lax.broadcasted_iota(jnp.int32, (tq, tq), 0)
    qry_off = lax.broadcasted_iota(jnp.int32, (tq, tq), 1)
    not_after = per_head(jnp.where(key_off <= qry_off, 0.0, NEG))
    inside_win = per_head(jnp.where(key_off > qry_off, 0.0, NEG))

    def masked(s, first_key, t0, windowed):
        blocks = []
        for r in range(0, s.shape[0], tq):
            blk_s = s[r:r + tq]
            if first_key + r == t0:
                blk_s = blk_s + not_after
            elif windowed and first_key + r == t0 - WIN:
                blk_s = blk_s + inside_win
            blocks.append(blk_s)
        return jnp.concatenate(blocks, axis=0)

    def normalise(acc):
        return acc[0:dh] / acc[dh:dh + 1]

    def select_blocks(psum, t0):
        tcol = col + t0
        bcausal = blk * SLC_LEN <= tcol
        n_live = (t0 + tq - 1) // SLC_LEN + 1
        top_n = min(SLC_TOPK, nb)
        if n_live <= top_n:
            bias = jnp.where(bcausal, 0.0, NEG)
        else:
            p_hi = psum.astype(BF16)
            p_lo = (psum - p_hi.astype(F32)).astype(BF16)
            imp = _nn(ovt, p_hi) + _nn(ovt, p_lo)
            cur = tcol // SLC_LEN
            forced = (blk == 0) | (blk == cur) | (blk == cur - 1)
            imp = jnp.where(bcausal, jnp.where(forced, FORCE, imp), NEG)
            rank = jnp.zeros((nb, tq), F32)
            for j in range(n_live):
                r = imp[j:j + 1, :]
                rank = rank + jnp.where(blk > j, jnp.where(r >= imp, 1.0, 0.0), jnp.where(r > imp, 1.0, 0.0))
            bias = jnp.where((rank < float(top_n)) & bcausal, 0.0, NEG)
        feat = jnp.concatenate([jnp.zeros((dh, tq), F32), bias,
                                jnp.zeros((LANES - dh - nb, tq), F32)], axis=0).astype(BF16)
        return _nt(eye, feat)

    class TileGroup:
        def __init__(self, tiles):
            self.tiles = tiles
            self.wstart = {t0: max(t0 - WIN, 0) for t0 in tiles}
            self.wkeys = {t0: slice(self.wstart[t0], t0 + tq) for t0 in tiles}
            self.skeys = {t0: slice(0, t0 + tq) for t0 in tiles}

        def scores_window_compressed(self):
            self.heads, self.qs = {}, {}
            for t0 in self.tiles:
                qf = q_ref[0, t0:t0 + tq, :].astype(F32)
                hl = []
                for hh in range(hpg):
                    t = qf[:, (hh // 2) * LANES:(hh // 2 + 1) * LANES]
                    if hh % 2 == 1:
                        t = _swap_halves(t)
                    hl.append(jnp.where(low, t, 0.0))
                self.heads[t0] = hl
                self.qs[t0] = jnp.concatenate(hl, axis=0).astype(BF16)
            self.s_w = {t0: _nt(kwx_ref[0, self.wkeys[t0], :], self.qs[t0]) for t0 in self.tiles}
            self.s_c = {t0: _nt(kcm, self.qs[t0]) for t0 in self.tiles}

        def select_and_scores_selected(self):
            self.p_cmp, qsel = {}, {}
            for t0 in self.tiles:
                cmask = (crow * CMP_STRIDE + (CMP_LEN - 1)) <= (ccol + t0)
                p_all = []
                psum = jnp.zeros((n_cmp, tq), F32)
                for hh in range(hpg):
                    sh = jnp.where(cmask, self.s_c[t0][:, hh * tq:(hh + 1) * tq], NEG)
                    e = jnp.exp2(sh - jnp.max(sh, axis=0, keepdims=True))
                    p = jnp.where(cmask, e / jnp.sum(e, axis=0, keepdims=True), 0.0)
                    psum = psum + p
                    p_all.append(p.astype(BF16))
                self.p_cmp[t0] = jnp.concatenate(p_all, axis=1)
                qbias = select_blocks(psum, t0)
                qsel[t0] = jnp.concatenate([hd + qbias for hd in self.heads[t0]], axis=0).astype(BF16)
            self.s_s = {t0: _nt(ksx_ref[0, self.skeys[t0], :], qsel[t0]) for t0 in self.tiles}

        def outputs_window_compressed(self):
            e_w = {}
            for t0 in self.tiles:
                sw = masked(self.s_w[t0], self.wstart[t0], t0, True)
                e_w[t0] = jnp.exp2(sw - jnp.max(sw, axis=0, keepdims=True)).astype(BF16)
            self.o_win = {t0: normalise(_nn(vwt_ref[0, 0, :, self.wkeys[t0]], e_w[t0])) for t0 in self.tiles}
            self.o_cmp = {t0: _nn(vct, self.p_cmp[t0]) for t0 in self.tiles}

        def outputs_selected_and_store(self):
            e_s = {}
            for t0 in self.tiles:
                ss = masked(self.s_s[t0], 0, t0, False)
                e_s[t0] = jnp.exp2(ss - jnp.max(ss, axis=0, keepdims=True)).astype(BF16)
            o_sel = {t0: normalise(_nn(vst_ref[0, 0, :, self.skeys[t0]], e_s[t0])) for t0 in self.tiles}
            for t0 in self.tiles:
                gt = gate_ref[0, :, t0:t0 + tq]
                outs = []
                for hh in range(hpg):
                    hc = slice(hh * tq, (hh + 1) * tq)

                    def gate_row(branch):
                        by_group = [gt[branch * NSA_HEADS + g * hpg + hh:branch * NSA_HEADS + g * hpg + hh + 1, :]
                                    for g in range(NSA_GROUPS)]
                        row = by_group[-1]
                        for g in range(NSA_GROUPS - 2, -1, -1):
                            row = jnp.where(group == g, by_group[g], row)
                        return row

                    outs.append(gate_row(0) * self.o_cmp[t0][:, hc] + gate_row(1) * o_sel[t0][:, hc]
                                + gate_row(2) * self.o_win[t0][:, hc])
                o_ref[0, t0:t0 + tq, :] = jnp.concatenate(outs, axis=0).T.astype(o_ref.dtype)

    starts = list(range(0, seq, tq))
    groups = [TileGroup(starts[i:i + TILE_GROUP]) for i in range(0, len(starts), TILE_GROUP)]
    n = len(groups)
    groups[0].scores_window_compressed()
    for i, g in enumerate(groups):
        if i + 1 < n:
            groups[i + 1].scores_window_compressed()
        g.select_and_scores_selected()
        g.outputs_window_compressed()
        if i > 0:
            groups[i - 1].outputs_selected_and_store()
    groups[n - 1].outputs_selected_and_store()


def _nsa_attention(nq, kvcmp, vct, ksx, vst, kwx, vwt, gate_t, ovt):
    b, s, _ = nq.shape
    n_cmp = kvcmp.shape[1]
    gw = NSA_HPG * NSA_DH
    per_group = lambda rows, width: pl.BlockSpec((1, rows, width), lambda bi, g: (bi, 0, g))
    per_group_t = lambda a: pl.BlockSpec((1, 1) + a.shape[2:], lambda bi, g: (bi, g, 0, 0))
    return pl.pallas_call(
        _nsa_kernel,
        grid=(b, NSA_GROUPS),
        in_specs=[per_group(s, gw),
                  per_group(n_cmp, LANES), per_group_t(vct),
                  per_group(s, LANES), per_group_t(vst), per_group(s, LANES), per_group_t(vwt),
                  pl.BlockSpec((1,) + gate_t.shape[1:], lambda bi, g: (bi, 0, 0)),
                  pl.BlockSpec(ovt.shape, lambda bi, g: (0, 0))],
        out_specs=per_group(s, gw),
        out_shape=jax.ShapeDtypeStruct((b, s, NSA_HEADS * NSA_DH), BF16),
        compiler_params=pltpu.CompilerParams(
            dimension_semantics=("parallel", "parallel"), vmem_limit_bytes=VMEM_LIMIT),
        name="nsa_attn",
    )(nq, kvcmp, vct, ksx, vst, kwx, vwt, gate_t, ovt)


def _ffn_kernel(x_ref, yr_ref, yn_ref, mod_ref, wo_ref, g2_ref, gf_ref, wg_ref, wu_ref, wd_ref,
                o_ref, x1_ref, act_ref):
    half_w = yr_ref.shape[1]
    d_ff = wg_ref.shape[1]
    n_chunks = d_ff // TF
    rows_a = slice(0, x_ref.shape[0] // 2)
    rows_b = slice(x_ref.shape[0] // 2, x_ref.shape[0])

    def mix(rows):
        return _nn(yr_ref[rows, :], wo_ref[0:half_w, :]) + _nn(yn_ref[rows, :], wo_ref[half_w:2 * half_w, :])

    def mid_norm(rows, mixed):
        x1 = x_ref[rows, :] + mod_ref[0, 2:3, :] * mixed
        x1_ref[rows, :] = x1
        y = x1 * lax.rsqrt(jnp.mean(x1 * x1, axis=-1, keepdims=True) + EPS) * g2_ref[...]
        return (y * (1.0 + mod_ref[0, 4:5, :]) + mod_ref[0, 3:4, :]).astype(BF16)

    def ff_chunk(rows, h2, j):
        sl = slice(j * TF, (j + 1) * TF)
        gate = _nn(h2, wg_ref[:, sl])
        up = _nn(h2, wu_ref[:, sl])
        act_ref[rows, sl] = (gate * _sigmoid(gate) * up).astype(BF16)

    def down(rows):
        return x1_ref[rows, :] + mod_ref[0, 5:6, :] * _nn(act_ref[rows, :], wd_ref[...])

    def final_norm(rows, xo):
        o_ref[rows, :] = xo * lax.rsqrt(jnp.mean(xo * xo, axis=-1, keepdims=True) + EPS) * gf_ref[...]

    mix_a = mix(rows_a)
    mix_b = mix(rows_b)
    h2_a = mid_norm(rows_a, mix_a)
    ff_chunk(rows_a, h2_a, 0)
    h2_b = mid_norm(rows_b, mix_b)
    for j in range(1, n_chunks):
        ff_chunk(rows_a, h2_a, j)
    xo_a = down(rows_a)
    ff_chunk(rows_b, h2_b, 0)
    final_norm(rows_a, xo_a)
    for j in range(1, n_chunks):
        ff_chunk(rows_b, h2_b, j)
    final_norm(rows_b, down(rows_b))


def _out_ffn(x2d, y_ret, y_nsa, mod, w_out, g2, gf, wg, wu, wd, seq):
    n, d = x2d.shape
    tm = TM_FF
    d_ff = wg.shape[1]
    tiles_per_seq = seq // tm
    half_w = y_ret.shape[1]
    row = lambda i: (i, 0)
    resident = lambda a: pl.BlockSpec(a.shape, lambda i: (0, 0), pipeline_mode=pl.Buffered(1))
    return pl.pallas_call(
        _ffn_kernel,
        grid=(n // tm,),
        in_specs=[pl.BlockSpec((tm, d), row),
                  pl.BlockSpec((tm, half_w), row),
                  pl.BlockSpec((tm, half_w), row),
                  pl.BlockSpec((1, 6, d), lambda i: (i // tiles_per_seq, 0, 0)),
                  resident(w_out),
                  pl.BlockSpec((1, d), lambda i: (0, 0)),
                  pl.BlockSpec((1, d), lambda i: (0, 0)),
                  resident(wg), resident(wu), resident(wd)],
        out_specs=pl.BlockSpec((tm, d), row),
        out_shape=jax.ShapeDtypeStruct((n, d), F32),
        scratch_shapes=[pltpu.VMEM((tm, d), F32), pltpu.VMEM((tm, d_ff), BF16)],
        compiler_params=pltpu.CompilerParams(
            dimension_semantics=("parallel",), vmem_limit_bytes=VMEM_LIMIT),
        name="out_ffn",
    )(x2d, y_ret, y_nsa, mod, w_out, g2.reshape(1, d), gf.reshape(1, d), wg, wu, wd)


def kernel(x, c, ln_mix_g, ln_ffn_g, w_ada, b_ada, w_in, cmp_pe_k, cmp_w1_k, cmp_w2_k,
           cmp_pe_v, cmp_w1_v, cmp_w2_v, w_out, w_ff_gate, w_ff_up, w_ff_down, ln_final_g):
    assert w_in.shape[0] == 1, "the final RMSNorm is fused into the (single) layer's FFN kernel"
    b, s, d = x.shape
    lane = np.arange(LANES)
    tabs = _rope_tables(np.arange(s), np.ones(LANES, bool))
    n_piece = s // CMP_STRIDE
    tabs_cmp = _rope_tables(np.arange(n_piece) * CMP_STRIDE + CMP_LEN - 1, lane < NSA_DH)

    mod = _ada(c, w_ada[0], b_ada[0]).reshape(b, 6, d)
    rq, rk, rv, rg_act, nq, kc, vc, ksx, vst, kwx, vwt, gate_t = _in_proj(
        x, ln_mix_g[0], mod, w_in, tabs, _block_onehot_table(s))
    y_ret = _retention(rq, rk, rv, rg_act, _retention_tables())
    w1, w2, pe = _compress_weights(cmp_w1_k[0], cmp_w2_k[0], cmp_w1_v[0], cmp_w2_v[0],
                                   cmp_pe_k[0], cmp_pe_v[0])
    kvcmp, vct = _compress(kc, vc, w1, w2, pe, tabs_cmp)
    y_nsa = _nsa_attention(nq, kvcmp, vct, ksx, vst, kwx, vwt, gate_t, _overlap_t(s))
    out = _out_ffn(x.reshape(b * s, d), y_ret.reshape(b * s, -1), y_nsa.reshape(b * s, -1), mod,
                   w_out[0].astype(BF16), ln_ffn_g[0], ln_final_g,
                   w_ff_gate[0].astype(BF16), w_ff_up[0].astype(BF16), w_ff_down[0].astype(BF16), s)
    return out.reshape(b, s, d)
```

```python
import numpy as np
import jax
import jax.numpy as jnp
from jax import lax
from jax.experimental import pallas as pl
from jax.experimental.pallas import tpu as pltpu

F32 = jnp.float32
BF16 = jnp.bfloat16

D_MODEL = 1024
RET_HEADS = 4
RET_DK = 64
RET_DV = 128
RET_CHUNK = 128
NSA_HEADS = 8
NSA_GROUPS = 2
NSA_HPG = NSA_HEADS // NSA_GROUPS
NSA_DH = 64
CMP_LEN = 32
CMP_STRIDE = 16
CMP_HIDDEN = 128
SLC_LEN = 64
SLC_TOPK = 16
WIN = 512
D_FF = ((8 * D_MODEL + 3 * 256 - 1) // (3 * 256)) * 256
ROPE_THETA = 10000.0
EPS = 1e-6
NEG = -1e30
FORCE = 1e6

LANES = 128
SUBLANES = 8
HALF = NSA_DH // 2
ONES_ROWS = 16
GATE_ROWS = 3 * NSA_HEADS

TM_IN = 1024
PROJ_W = 512
RET_GROUP = 8
Q_SCALE = NSA_DH ** -0.5 * float(np.log2(np.e))
TQ = 256
TILE_GROUP = 2
TM_FF = 512
TF = 256
CAST_ROWS = 128
VMEM_LIMIT = 56 * 1024 * 1024

_R_RQ, _R_RK, _R_RV, _R_RG, _R_NQ = 0, 256, 512, 1024, 1536
_R_KC, _R_VC, _R_KS, _R_VS, _R_KW, _R_VW, _R_GATE = 2048, 2176, 2304, 2432, 2560, 2688, 2816
IN_COLS_K = -(-(_R_GATE + 3 * NSA_HEADS) // PROJ_W) * PROJ_W


def _sigmoid(x):
    return 1.0 / (1.0 + jnp.exp(-x))


def _nt(a, b):
    return lax.dot_general(a, b, (((1,), (1,)), ((), ())), preferred_element_type=F32)


def _nn(a, b):
    return jnp.dot(a, b, preferred_element_type=F32)


def _rope_tile(a, c, s1, s2):
    return a * c + pltpu.roll(a, HALF, 1) * s1 + pltpu.roll(a, LANES - HALF, 1) * s2


def _swap_halves(a):
    return pltpu.roll(a, LANES // 2, 1)


def _rope_tables(pos, rotary_lanes):
    pos = np.asarray(pos, np.float64)
    lane = np.arange(LANES)
    within = lane % NSA_DH
    freq = ROPE_THETA ** (-(within % HALF).astype(np.float64) / HALF)
    ang = pos[:, None] * freq[None, :]
    cos, sin = np.cos(ang), np.sin(ang)
    first = (within < HALF)[None, :]
    rot = np.asarray(rotary_lanes, bool)[None, :]
    c = np.where(rot, cos, 1.0)
    s1 = np.where(rot & ~first, sin, 0.0)
    s2 = np.where(rot & first, -sin, 0.0)
    return (jnp.asarray(c, F32), jnp.asarray(s1, F32), jnp.asarray(s2, F32))


def _block_onehot_table(seq):
    t = np.zeros((seq, LANES), np.float32)
    pos = np.arange(seq)
    t[pos, NSA_DH + pos // SLC_LEN] = 1.0
    return jnp.asarray(t)


def _retention_tables():
    h = jnp.arange(RET_HEADS, dtype=F32)
    log_g = jnp.log(1.0 - 2.0 ** (-5.0 - h))
    c = RET_CHUNK
    idx = jnp.arange(c, dtype=F32)
    diff = idx[:, None] - idx[None, :]
    causal = diff >= 0
    decay = jnp.where(causal, jnp.exp(log_g[:, None, None] * jnp.where(causal, diff, 0.0)), 0.0)
    zeta = jnp.exp(log_g[:, None] * (c - 1.0 - idx))
    xi = jnp.exp(log_g[:, None] * (idx + 1.0))
    g_chunk = jnp.exp(log_g * c)

    def pair_lanes(t):
        t = t.reshape(RET_HEADS // 2, 2, c)
        return jnp.repeat(jnp.transpose(t, (0, 2, 1)), RET_DK, axis=2)

    g_b = jnp.broadcast_to(g_chunk[:, None, None], (RET_HEADS, 1, LANES))
    return decay, pair_lanes(zeta), pair_lanes(xi), g_b


def _overlap_t(seq):
    n_c = seq // CMP_STRIDE - CMP_LEN // CMP_STRIDE + 1
    nb = seq // SLC_LEN
    cs = np.arange(n_c) * CMP_STRIDE
    bs = np.arange(nb) * SLC_LEN
    ov = np.maximum(np.minimum(cs[:, None] + CMP_LEN, bs[None] + SLC_LEN)
                    - np.maximum(cs[:, None], bs[None]), 0).astype(np.float64) / CMP_LEN
    ncp = seq // CMP_STRIDE
    ovp = np.zeros((ncp, nb))
    ovp[:n_c] = ov
    return jnp.asarray(ovp.T, BF16)


def _ada_kernel(c_ref, w_ref, b_ref, o_ref):
    c = c_ref[...]
    o_ref[...] = _nn(c * _sigmoid(c), w_ref[...]) + b_ref[...]


def _ada(c, w, b):
    bsz, d = c.shape
    n = w.shape[1]
    tn = 1024
    return pl.pallas_call(
        _ada_kernel,
        grid=(n // tn,),
        in_specs=[pl.BlockSpec((bsz, d), lambda j: (0, 0)),
                  pl.BlockSpec((d, tn), lambda j: (0, j)),
                  pl.BlockSpec((1, tn), lambda j: (0, j))],
        out_specs=pl.BlockSpec((bsz, tn), lambda j: (0, j)),
        out_shape=jax.ShapeDtypeStruct((bsz, n), F32),
        compiler_params=pltpu.CompilerParams(vmem_limit_bytes=VMEM_LIMIT),
        name="ada",
    )(c, w, b.reshape(1, n))


def _inproj_kernel(x_ref, g_ref, mod_ref, w_ref, c_ref, s1_ref, s2_ref, hot_ref,
                   rq_ref, rk_ref, rv_ref, rg_ref, nq_ref, kc_ref, vc_ref,
                   ksx_ref, vst_ref, kwx_ref, vwt_ref, gate_ref, wb_ref):
    @pl.when((pl.program_id(0) == 0) & (pl.program_id(1) == 0))
    def _():
        n_in = w_ref.shape[2]
        whole = (n_in // PROJ_W) * PROJ_W
        for c0 in range(0, whole, PROJ_W):
            wb_ref[:, c0:c0 + PROJ_W] = w_ref[0, :, c0:c0 + PROJ_W].astype(BF16)
        wb_ref[:, whole:] = jnp.zeros((wb_ref.shape[0], wb_ref.shape[1] - whole), BF16)
        wb_ref[:, whole:n_in] = w_ref[0, :, whole:n_in].astype(BF16)

    def modulated_norm(rows):
        x = x_ref[0, rows, :]
        y = x * lax.rsqrt(jnp.mean(x * x, axis=-1, keepdims=True) + EPS) * g_ref[...]
        return (y * (1.0 + mod_ref[0, 1:2, :]) + mod_ref[0, 0:1, :]).astype(BF16)

    def product_steps(rows, hb):
        tabs = (c_ref[rows, :], s1_ref[rows, :], s2_ref[rows, :])
        low = lax.broadcasted_iota(jnp.int32, (rows.stop - rows.start, LANES), 1) < NSA_DH

        def proj(c0):
            a = _nn(hb, wb_ref[:, c0:c0 + PROJ_W])
            return [a[:, t * LANES:(t + 1) * LANES] for t in range(PROJ_W // LANES)]

        def put(out_ref, t, value):
            out_ref[0, rows, t * LANES:(t + 1) * LANES] = value.astype(out_ref.dtype)

        def roped(tiles, scale, out_ref):
            for t, a in enumerate(tiles):
                r = _rope_tile(a, *tabs)
                put(out_ref, t, r if scale == 1.0 else r * scale)

        def per_group(tile, fill, out_ref):
            put(out_ref, 0, jnp.where(low, tile, fill))
            put(out_ref, 1, jnp.where(low, _swap_halves(tile), fill))

        def per_group_t(tile, out_ref):
            t = tile.T
            ones = jnp.ones((ONES_ROWS, t.shape[1]), out_ref.dtype)
            for g in range(NSA_GROUPS):
                out_ref[0, g, 0:NSA_DH, rows] = t[g * NSA_DH:(g + 1) * NSA_DH].astype(out_ref.dtype)
                out_ref[0, g, NSA_DH:NSA_DH + ONES_ROWS, rows] = ones

        def retention_qk():
            tiles = proj(_R_RQ)
            roped(tiles[0:2], 1.0, rq_ref)
            roped(tiles[2:4], RET_DK ** -0.5, rk_ref)

        def retention_v():
            for t, a in enumerate(proj(_R_RV)):
                put(rv_ref, t, a)

        def retention_gate():
            for t, a in enumerate(proj(_R_RG)):
                put(rg_ref, t, a * _sigmoid(a))

        def nsa_q():
            roped(proj(_R_NQ), Q_SCALE, nq_ref)

        def nsa_compress_selected():
            kc_t, vc_t, ks_t, vs_t = proj(_R_KC)
            kc_ref[0, rows, :] = kc_t
            vc_ref[0, rows, :] = vc_t
            per_group(_rope_tile(ks_t, *tabs), hot_ref[rows, :], ksx_ref)
            per_group_t(vs_t, vst_ref)

        def nsa_window_gates():
            kw_t, vw_t, gates_t, _ = proj(_R_KW)
            per_group(_rope_tile(kw_t, *tabs), 0.0, kwx_ref)
            per_group_t(vw_t, vwt_ref)
            gate_ref[0, :, rows] = _sigmoid(gates_t.T[0:GATE_ROWS])

        return [retention_qk, retention_v, retention_gate, nsa_q, nsa_compress_selected, nsa_window_gates]

    tm = x_ref.shape[1]
    rows_a, rows_b = slice(0, tm // 2), slice(tm // 2, tm)
    steps_a = product_steps(rows_a, modulated_norm(rows_a))
    steps_a[0]()
    steps_b = product_steps(rows_b, modulated_norm(rows_b))
    for step in steps_a[1:] + steps_b:
        step()


def _in_proj(x, ln_g, mod, w_in, tabs, hot):
    b, s, d = x.shape
    tm = TM_IN
    grid = (b, s // tm)
    tab_spec = pl.BlockSpec((tm, LANES), lambda bi, j: (j, 0))

    def out(n, dtype):
        return (jax.ShapeDtypeStruct((b, s, n), dtype), pl.BlockSpec((1, tm, n), lambda bi, j: (bi, j, 0)))

    def out_t(rows, dtype):
        return (jax.ShapeDtypeStruct((b, NSA_GROUPS, rows, s), dtype),
                pl.BlockSpec((1, NSA_GROUPS, rows, tm), lambda bi, j: (bi, 0, 0, j)))

    vt_rows = NSA_DH + ONES_ROWS
    outs = [out(256, BF16), out(256, BF16), out(512, BF16), out(512, BF16), out(512, BF16),
            out(LANES, F32), out(LANES, F32),
            out(256, BF16), out_t(vt_rows, BF16), out(256, BF16), out_t(vt_rows, BF16),
            (jax.ShapeDtypeStruct((b, GATE_ROWS, s), F32),
             pl.BlockSpec((1, GATE_ROWS, tm), lambda bi, j: (bi, 0, j)))]
    return pl.pallas_call(
        _inproj_kernel,
        grid=grid,
        in_specs=[pl.BlockSpec((1, tm, d), lambda bi, j: (bi, j, 0)),
                  pl.BlockSpec((1, d), lambda bi, j: (0, 0)),
                  pl.BlockSpec((1, 6, d), lambda bi, j: (bi, 0, 0)),
                  pl.BlockSpec((1,) + w_in.shape[1:], lambda bi, j: (0, 0, 0), pipeline_mode=pl.Buffered(1)),
                  tab_spec, tab_spec, tab_spec, tab_spec],
        out_specs=[o[1] for o in outs],
        out_shape=[o[0] for o in outs],
        scratch_shapes=[pltpu.VMEM((d, IN_COLS_K), BF16)],
        compiler_params=pltpu.CompilerParams(
            dimension_semantics=("arbitrary", "arbitrary"), vmem_limit_bytes=VMEM_LIMIT),
        name="in_proj",
    )(x, ln_g.reshape(1, d), mod, w_in, *tabs, hot)


def _ret_kernel(q_ref, k_ref, v_ref, rg_ref, dec_ref, zeta_ref, xi_ref, gch_ref, o_ref, kv_ref, prev_ref):
    c = RET_CHUNK
    n_chunks = q_ref.shape[1] // c
    low = lax.broadcasted_iota(jnp.int32, (c, LANES), 1) < RET_DK

    def chunk_rows(n):
        return pl.ds(pl.multiple_of(n * c, c), c)

    def head_cols(h):
        return slice(h * RET_DV, (h + 1) * RET_DV)

    def kv_body(it, carry):
        kz_t = {}
        for j in range(RET_GROUP):
            rows = chunk_rows(it * RET_GROUP + j)
            for p in range(RET_HEADS // 2):
                pair = slice(p * LANES, (p + 1) * LANES)
                kz_t[j, p] = (k_ref[0, rows, pair].astype(F32) * zeta_ref[p]).T.astype(BF16)
        for j in range(RET_GROUP):
            n = it * RET_GROUP + j
            for h in range(RET_HEADS):
                kv_ref[h, n] = _nn(kz_t[j, h // 2], v_ref[0, chunk_rows(n), head_cols(h)])
        return carry

    lax.fori_loop(0, n_chunks // RET_GROUP, kv_body, 0)

    for h in range(RET_HEADS):
        def scan_body(n, st, h=h):
            prev_ref[h, n] = st.astype(prev_ref.dtype)
            return st * gch_ref[h] + kv_ref[h, n]
        lax.fori_loop(0, n_chunks, scan_body, jnp.zeros((LANES, RET_DV), F32))

    def out_body(it, carry):
        chains = [(j, h) for j in range(RET_GROUP) for h in range(RET_HEADS)]
        chunk = lambda j: it * RET_GROUP + j
        q_own, qx_own, att, ys = {}, {}, {}, {}
        for j, h in chains:
            p, e = divmod(h, 2)
            pair = slice(p * LANES, (p + 1) * LANES)
            mine = low if e == 0 else jnp.logical_not(low)
            q2 = q_ref[0, chunk_rows(chunk(j)), pair].astype(F32)
            q_own[j, h] = jnp.where(mine, q2, 0.0).astype(BF16)
            qx_own[j, h] = jnp.where(mine, q2 * xi_ref[p], 0.0).astype(BF16)
        for j, h in chains:
            pair = slice((h // 2) * LANES, (h // 2 + 1) * LANES)
            att[j, h] = _nt(q_own[j, h], k_ref[0, chunk_rows(chunk(j)), pair])
        for j, h in chains:
            lhs = jnp.concatenate([(att[j, h] * dec_ref[h]).astype(BF16), qx_own[j, h]], axis=1)
            rhs = jnp.concatenate([v_ref[0, chunk_rows(chunk(j)), head_cols(h)], prev_ref[h, chunk(j)]], axis=0)
            ys[j, h] = _nn(lhs, rhs)
        for j, h in chains:
            y = ys[j, h]
            yn = y * lax.rsqrt(jnp.mean(y * y, axis=-1, keepdims=True) + EPS)
            gate = rg_ref[0, chunk_rows(chunk(j)), head_cols(h)].astype(F32)
            o_ref[0, chunk_rows(chunk(j)), head_cols(h)] = (yn * gate).astype(o_ref.dtype)
        return carry

    lax.fori_loop(0, n_chunks // RET_GROUP, out_body, 0)


def _retention(rq, rk, rv, rg_act, tables):
    b, s, _ = rq.shape
    decay, zeta_p, xi_p, g_b = tables
    whole = lambda a: pl.BlockSpec(a.shape, lambda bi: (0,) * a.ndim)
    row = lambda a: pl.BlockSpec((1,) + a.shape[1:], lambda bi: (bi, 0, 0))
    return pl.pallas_call(
        _ret_kernel,
        grid=(b,),
        in_specs=[row(rq), row(rk), row(rv), row(rg_act),
                  whole(decay), whole(zeta_p), whole(xi_p), whole(g_b)],
        out_specs=row(rv),
        out_shape=jax.ShapeDtypeStruct(rv.shape, BF16),
        scratch_shapes=[pltpu.VMEM((RET_HEADS, s // RET_CHUNK, LANES, RET_DV), F32),
                        pltpu.VMEM((RET_HEADS, s // RET_CHUNK, LANES, RET_DV), BF16)],
        compiler_params=pltpu.CompilerParams(
            dimension_semantics=("parallel",), vmem_limit_bytes=VMEM_LIMIT),
        name="retention",
    )(rq, rk, rv, rg_act, decay, zeta_p, xi_p, g_b)


def _cmp_kernel(kc_ref, vc_ref, w1_ref, pe_ref, w2_ref, c_ref, s1_ref, s2_ref, kv_ref, vt_ref):
    n_piece = kc_ref.shape[1] // CMP_STRIDE
    halves = CMP_LEN // CMP_STRIDE
    lhs = {}
    for t, src in enumerate((kc_ref, vc_ref)):
        rows = [src[0, pl.ds(r, n_piece, stride=CMP_STRIDE), :] for r in range(CMP_STRIDE)]
        for half in range(halves):
            pe0 = half * CMP_STRIDE
            lhs[t, half] = jnp.concatenate(
                [(rows[r] + pe_ref[t, pe0 + r:pe0 + r + 1, :]).astype(BF16) for r in range(CMP_STRIDE)], axis=1)
    part = {key: _nn(lhs[key], w1_ref[key[0], key[1]]) for key in lhs}
    out = jnp.zeros((n_piece, kv_ref.shape[2]), F32)
    for t in range(2):
        hid = part[t, 0]
        for half in range(1, halves):
            hid = hid + pltpu.roll(part[t, half], n_piece - half, 0)
        out = out + _nn((hid * _sigmoid(hid)).astype(BF16), w2_ref[t])
    for t in range(out.shape[1] // LANES):
        sl = slice(t * LANES, (t + 1) * LANES)
        kv = _rope_tile(out[:, sl], c_ref[...], s1_ref[...], s2_ref[...])
        kv_ref[0, :, sl] = kv.astype(kv_ref.dtype)
        vt_ref[0, t] = kv.T[NSA_DH:2 * NSA_DH].astype(vt_ref.dtype)


def _compress_weights(w1_k, w2_k, w1_v, w2_v, pe_k, pe_v):
    dh, hid = NSA_DH, CMP_HIDDEN

    def first(w1):
        w = w1.reshape(CMP_LEN, dh, hid)
        z = jnp.zeros_like(w)
        return jnp.concatenate([jnp.concatenate([w, z], axis=2), jnp.concatenate([z, w], axis=2)], axis=1)

    def second(w2, off):
        z = jnp.zeros_like(w2)
        rows = []
        for g in range(NSA_GROUPS):
            c = [z, z, z, z]
            c[2 * g + off] = w2
            rows.append(jnp.concatenate(c, axis=1))
        return jnp.concatenate(rows, axis=0)

    w1 = jnp.stack([first(w1_k), first(w1_v)]).astype(BF16).reshape(
        2, CMP_LEN // CMP_STRIDE, CMP_STRIDE * NSA_GROUPS * dh, NSA_GROUPS * hid)
    w2 = jnp.stack([second(w2_k, 0), second(w2_v, 1)]).astype(BF16)
    pe = jnp.stack([jnp.concatenate([pe_k, pe_k], axis=1), jnp.concatenate([pe_v, pe_v], axis=1)])
    return w1, w2, pe


def _compress(kc, vc, w1, w2, pe, tabs_cmp):
    b, s, w = kc.shape
    n_piece = s // CMP_STRIDE
    const2 = lambda bi: (0, 0)
    src = pl.BlockSpec((1, s, w), lambda bi: (bi, 0, 0))
    dst = pl.BlockSpec((1, n_piece, 2 * w), lambda bi: (bi, 0, 0))
    shape = jax.ShapeDtypeStruct((b, n_piece, 2 * w), BF16)
    return pl.pallas_call(
        _cmp_kernel,
        grid=(b,),
        in_specs=[src, src,
                  pl.BlockSpec(w1.shape, lambda bi: (0, 0, 0, 0)),
                  pl.BlockSpec(pe.shape, lambda bi: (0, 0, 0)),
                  pl.BlockSpec(w2.shape, lambda bi: (0, 0, 0)),
                  pl.BlockSpec((n_piece, LANES), const2),
                  pl.BlockSpec((n_piece, LANES), const2),
                  pl.BlockSpec((n_piece, LANES), const2)],
        out_specs=[dst, pl.BlockSpec((1, NSA_GROUPS, NSA_DH, n_piece), lambda bi: (bi, 0, 0, 0))],
        out_shape=[shape, jax.ShapeDtypeStruct((b, NSA_GROUPS, NSA_DH, n_piece), BF16)],
        compiler_params=pltpu.CompilerParams(
            dimension_semantics=("parallel",), vmem_limit_bytes=VMEM_LIMIT),
        name="compress",
    )(kc, vc, w1, pe, w2, *tabs_cmp)


def _nsa_kernel(q_ref, kcmp_ref, vct_ref, ksx_ref, vst_ref, kwx_ref, vwt_ref, gate_ref, ovt_ref, o_ref):
    tq = TQ
    seq = q_ref.shape[1]
    nb = seq // SLC_LEN
    n_cmp = kcmp_ref.shape[1]
    hpg, dh = NSA_HPG, NSA_DH
    assert WIN % tq == 0 and seq % tq == 0
    group = pl.program_id(1)

    low = lax.broadcasted_iota(jnp.int32, (tq, LANES), 1) < dh
    eye = jnp.where(lax.broadcasted_iota(jnp.int32, (tq, tq), 0)
                    == lax.broadcasted_iota(jnp.int32, (tq, tq), 1), 1.0, 0.0).astype(BF16)
    blk = lax.broadcasted_iota(jnp.int32, (nb, tq), 0)
    col = lax.broadcasted_iota(jnp.int32, (nb, tq), 1)
    crow = lax.broadcasted_iota(jnp.int32, (n_cmp, tq), 0)
    ccol = lax.broadcasted_iota(jnp.int32, (n_cmp, tq), 1)
    kcm = kcmp_ref[0]
    vct = vct_ref[0, 0]
    ovt = ovt_ref[...]

    def per_head(x):
        return jnp.concatenate([x] * hpg, axis=1)

    key_off = lax.broadcasted_iota(jnp.int32, (tq, tq), 0)
    qry_off = lax.broadcasted_iota(jnp.int32, (tq, tq), 1)
    not_after = per_head(jnp.where(key_off <= qry_off, 0.0, NEG))
    inside_win = per_head(jnp.where(key_off > qry_off, 0.0, NEG))

    def masked(s, first_key, t0, windowed):
        blocks = []
        for r in range(0, s.shape[0], tq):
            blk_s = s[r:r + tq]
            if first_key + r == t0:
                blk_s = blk_s + not_after
            elif windowed and first_key + r == t0 - WIN:
                blk_s = blk_s + inside_win
            blocks.append(blk_s)
        return jnp.concatenate(blocks, axis=0)

    def normalise(acc):
        return acc[0:dh] / acc[dh:dh + 1]

    def select_blocks(psum, t0):
        tcol = col + t0
        bcausal = blk * SLC_LEN <= tcol
        n_live = (t0 + tq - 1) // SLC_LEN + 1
        top_n = min(SLC_TOPK, nb)
        if n_live <= top_n:
            bias = jnp.where(bcausal, 0.0, NEG)
        else:
            p_hi = psum.astype(BF16)
            p_lo = (psum - p_hi.astype(F32)).astype(BF16)
            imp = _nn(ovt, p_hi) + _nn(ovt, p_lo)
            cur = tcol // SLC_LEN
            forced = (blk == 0) | (blk == cur) | (blk == cur - 1)
            imp = jnp.where(bcausal, jnp.where(forced, FORCE, imp), NEG)
            rank = jnp.zeros((nb, tq), F32)
            for j in range(n_live):
                r = imp[j:j + 1, :]
                rank = rank + jnp.where(blk > j, jnp.where(r >= imp, 1.0, 0.0), jnp.where(r > imp, 1.0, 0.0))
            bias = jnp.where((rank < float(top_n)) & bcausal, 0.0, NEG)
        feat = jnp.concatenate([jnp.zeros((dh, tq), F32), bias,
                                jnp.zeros((LANES - dh - nb, tq), F32)], axis=0).astype(BF16)
        return _nt(eye, feat)

    class TileGroup:
        def __init__(self, tiles):
            self.tiles = tiles
            self.wstart = {t0: max(t0 - WIN, 0) for t0 in tiles}
            self.wkeys = {t0: slice(self.wstart[t0], t0 + tq) for t0 in tiles}
            self.skeys = {t0: slice(0, t0 + tq) for t0 in tiles}

        def scores_window_compressed(self):
            self.heads, self.qs = {}, {}
            for t0 in self.tiles:
                qf = q_ref[0, t0:t0 + tq, :].astype(F32)
                hl = []
                for hh in range(hpg):
                    t = qf[:, (hh // 2) * LANES:(hh // 2 + 1) * LANES]
                    if hh % 2 == 1:
                        t = _swap_halves(t)
                    hl.append(jnp.where(low, t, 0.0))
                self.heads[t0] = hl
                self.qs[t0] = jnp.concatenate(hl, axis=0).astype(BF16)
            self.s_w = {t0: _nt(kwx_ref[0, self.wkeys[t0], :], self.qs[t0]) for t0 in self.tiles}
            self.s_c = {t0: _nt(kcm, self.qs[t0]) for t0 in self.tiles}

        def select_and_scores_selected(self):
            self.p_cmp, qsel = {}, {}
            for t0 in self.tiles:
                cmask = (crow * CMP_STRIDE + (CMP_LEN - 1)) <= (ccol + t0)
                p_all = []
                psum = jnp.zeros((n_cmp, tq), F32)
                for hh in range(hpg):
                    sh = jnp.where(cmask, self.s_c[t0][:, hh * tq:(hh + 1) * tq], NEG)
                    e = jnp.exp2(sh - jnp.max(sh, axis=0, keepdims=True))
                    p = jnp.where(cmask, e / jnp.sum(e, axis=0, keepdims=True), 0.0)
                    psum = psum + p
                    p_all.append(p.astype(BF16))
                self.p_cmp[t0] = jnp.concatenate(p_all, axis=1)
                qbias = select_blocks(psum, t0)
                qsel[t0] = jnp.concatenate([hd + qbias for hd in self.heads[t0]], axis=0).astype(BF16)
            self.s_s = {t0: _nt(ksx_ref[0, self.skeys[t0], :], qsel[t0]) for t0 in self.tiles}

        def outputs_window_compressed(self):
            e_w = {}
            for t0 in self.tiles:
                sw = masked(self.s_w[t0], self.wstart[t0], t0, True)
                e_w[t0] = jnp.exp2(sw - jnp.max(sw, axis=0, keepdims=True)).astype(BF16)
            self.o_win = {t0: normalise(_nn(vwt_ref[0, 0, :, self.wkeys[t0]], e_w[t0])) for t0 in self.tiles}
            self.o_cmp = {t0: _nn(vct, self.p_cmp[t0]) for t0 in self.tiles}

        def outputs_selected_and_store(self):
            e_s = {}
            for t0 in self.tiles:
                ss = masked(self.s_s[t0], 0, t0, False)
                e_s[t0] = jnp.exp2(ss - jnp.max(ss, axis=0, keepdims=True)).astype(BF16)
            o_sel = {t0: normalise(_nn(vst_ref[0, 0, :, self.skeys[t0]], e_s[t0])) for t0 in self.tiles}
            for t0 in self.tiles:
                gt = gate_ref[0, :, t0:t0 + tq]
                outs = []
                for hh in range(hpg):
                    hc = slice(hh * tq, (hh + 1) * tq)

                    def gate_row(branch):
                        by_group = [gt[branch * NSA_HEADS + g * hpg + hh:branch * NSA_HEADS + g * hpg + hh + 1, :]
                                    for g in range(NSA_GROUPS)]
                        row = by_group[-1]
                        for g in range(NSA_GROUPS - 2, -1, -1):
                            row = jnp.where(group == g, by_group[g], row)
                        return row

                    outs.append(gate_row(0) * self.o_cmp[t0][:, hc] + gate_row(1) * o_sel[t0][:, hc]
                                + gate_row(2) * self.o_win[t0][:, hc])
                o_ref[0, t0:t0 + tq, :] = jnp.concatenate(outs, axis=0).T.astype(o_ref.dtype)

    starts = list(range(0, seq, tq))
    groups = [TileGroup(starts[i:i + TILE_GROUP]) for i in range(0, len(starts), TILE_GROUP)]
    n = len(groups)
    groups[0].scores_window_compressed()
    for i, g in enumerate(groups):
        if i + 1 < n:
            groups[i + 1].scores_window_compressed()
        g.select_and_scores_selected()
        g.outputs_window_compressed()
        if i > 0:
            groups[i - 1].outputs_selected_and_store()
    groups[n - 1].outputs_selected_and_store()


def _nsa_attention(nq, kvcmp, vct, ksx, vst, kwx, vwt, gate_t, ovt):
    b, s, _ = nq.shape
    n_cmp = kvcmp.shape[1]
    gw = NSA_HPG * NSA_DH
    per_group = lambda rows, width: pl.BlockSpec((1, rows, width), lambda bi, g: (bi, 0, g))
    per_group_t = lambda a: pl.BlockSpec((1, 1) + a.shape[2:], lambda bi, g: (bi, g, 0, 0))
    return pl.pallas_call(
        _nsa_kernel,
        grid=(b, NSA_GROUPS),
        in_specs=[per_group(s, gw),
                  per_group(n_cmp, LANES), per_group_t(vct),
                  per_group(s, LANES), per_group_t(vst), per_group(s, LANES), per_group_t(vwt),
                  pl.BlockSpec((1,) + gate_t.shape[1:], lambda bi, g: (bi, 0, 0)),
                  pl.BlockSpec(ovt.shape, lambda bi, g: (0, 0))],
        out_specs=per_group(s, gw),
        out_shape=jax.ShapeDtypeStruct((b, s, NSA_HEADS * NSA_DH), BF16),
        compiler_params=pltpu.CompilerParams(
            dimension_semantics=("parallel", "parallel"), vmem_limit_bytes=VMEM_LIMIT),
        name="nsa_attn",
    )(nq, kvcmp, vct, ksx, vst, kwx, vwt, gate_t, ovt)


def _ffn_kernel(x_ref, yr_ref, yn_ref, mod_ref, g2_ref, gf_ref, wo_hbm, wg_hbm, wu_hbm, wd_hbm,
                o_ref, x1_ref, act_ref, wo_ref, wg_ref, wu_ref, wd_ref, stage_ref, stage_sem):
    @pl.when(pl.program_id(0) == 0)
    def _():
        jobs = [(src, dst, r0, min(CAST_ROWS, dst.shape[0] - r0))
                for src, dst in ((wo_hbm, wo_ref), (wg_hbm, wg_ref), (wu_hbm, wu_ref), (wd_hbm, wd_ref))
                for r0 in range(0, dst.shape[0], CAST_ROWS)]

        def staged(k):
            src, dst, r0, rows = jobs[k]
            slot = k % 2
            return pltpu.make_async_copy(src.at[0, pl.ds(r0, rows), :],
                                         stage_ref.at[slot, pl.ds(0, rows), pl.ds(0, dst.shape[1])],
                                         stage_sem.at[slot])

        staged(0).start()
        for k, (src, dst, r0, rows) in enumerate(jobs):
            if k + 1 < len(jobs):
                staged(k + 1).start()
            staged(k).wait()
            dst[r0:r0 + rows, :] = stage_ref[k % 2, 0:rows, 0:dst.shape[1]].astype(BF16)

    half_w = yr_ref.shape[1]
    d_ff = wg_ref.shape[1]
    n_chunks = d_ff // TF
    rows_a = slice(0, x_ref.shape[0] // 2)
    rows_b = slice(x_ref.shape[0] // 2, x_ref.shape[0])

    def mix(rows):
        return _nn(yr_ref[rows, :], wo_ref[0:half_w, :]) + _nn(yn_ref[rows, :], wo_ref[half_w:2 * half_w, :])

    def mid_norm(rows, mixed):
        x1 = x_ref[rows, :] + mod_ref[0, 2:3, :] * mixed
        x1_ref[rows, :] = x1
        y = x1 * lax.rsqrt(jnp.mean(x1 * x1, axis=-1, keepdims=True) + EPS) * g2_ref[...]
        return (y * (1.0 + mod_ref[0, 4:5, :]) + mod_ref[0, 3:4, :]).astype(BF16)

    def ff_chunk(rows, h2, j):
        sl = slice(j * TF, (j + 1) * TF)
        gate = _nn(h2, wg_ref[:, sl])
        up = _nn(h2, wu_ref[:, sl])
        act_ref[rows, sl] = (gate * _sigmoid(gate) * up).astype(BF16)

    def down(rows):
        return x1_ref[rows, :] + mod_ref[0, 5:6, :] * _nn(act_ref[rows, :], wd_ref[...])

    def final_norm(rows, xo):
        o_ref[rows, :] = xo * lax.rsqrt(jnp.mean(xo * xo, axis=-1, keepdims=True) + EPS) * gf_ref[...]

    mix_a = mix(rows_a)
    mix_b = mix(rows_b)
    h2_a = mid_norm(rows_a, mix_a)
    ff_chunk(rows_a, h2_a, 0)
    h2_b = mid_norm(rows_b, mix_b)
    for j in range(1, n_chunks):
        ff_chunk(rows_a, h2_a, j)
    xo_a = down(rows_a)
    ff_chunk(rows_b, h2_b, 0)
    final_norm(rows_a, xo_a)
    for j in range(1, n_chunks):
        ff_chunk(rows_b, h2_b, j)
    final_norm(rows_b, down(rows_b))


def _out_ffn(x2d, y_ret, y_nsa, mod, w_out, g2, gf, wg, wu, wd, seq):
    n, d = x2d.shape
    tm = TM_FF
    d_ff = wg.shape[2]
    tiles_per_seq = seq // tm
    half_w = y_ret.shape[1]
    row = lambda i: (i, 0)
    in_hbm = pl.BlockSpec(memory_space=pl.ANY)
    weights = (w_out, wg, wu, wd)
    stage_cols = max(w.shape[2] for w in weights)
    return pl.pallas_call(
        _ffn_kernel,
        grid=(n // tm,),
        in_specs=[pl.BlockSpec((tm, d), row),
                  pl.BlockSpec((tm, half_w), row),
                  pl.BlockSpec((tm, half_w), row),
                  pl.BlockSpec((1, 6, d), lambda i: (i // tiles_per_seq, 0, 0)),
                  pl.BlockSpec((1, d), lambda i: (0, 0)),
                  pl.BlockSpec((1, d), lambda i: (0, 0)),
                  in_hbm, in_hbm, in_hbm, in_hbm],
        out_specs=pl.BlockSpec((tm, d), row),
        out_shape=jax.ShapeDtypeStruct((n, d), F32),
        scratch_shapes=[pltpu.VMEM((tm, d), F32), pltpu.VMEM((tm, d_ff), BF16)]
                       + [pltpu.VMEM(w.shape[1:], BF16) for w in weights]
                       + [pltpu.VMEM((2, CAST_ROWS, stage_cols), F32), pltpu.SemaphoreType.DMA((2,))],
        compiler_params=pltpu.CompilerParams(
            dimension_semantics=("arbitrary",), vmem_limit_bytes=VMEM_LIMIT),
        name="out_ffn",
    )(x2d, y_ret, y_nsa, mod, g2.reshape(1, d), gf.reshape(1, d), *weights)


def kernel(x, c, ln_mix_g, ln_ffn_g, w_ada, b_ada, w_in, cmp_pe_k, cmp_w1_k, cmp_w2_k,
           cmp_pe_v, cmp_w1_v, cmp_w2_v, w_out, w_ff_gate, w_ff_up, w_ff_down, ln_final_g):
    assert w_in.shape[0] == 1, "the final RMSNorm is fused into the (single) layer's FFN kernel"
    b, s, d = x.shape
    lane = np.arange(LANES)
    tabs = _rope_tables(np.arange(s), np.ones(LANES, bool))
    n_piece = s // CMP_STRIDE
    tabs_cmp = _rope_tables(np.arange(n_piece) * CMP_STRIDE + CMP_LEN - 1, lane < NSA_DH)

    mod = _ada(c, w_ada[0], b_ada[0]).reshape(b, 6, d)
    rq, rk, rv, rg_act, nq, kc, vc, ksx, vst, kwx, vwt, gate_t = _in_proj(
        x, ln_mix_g[0], mod, w_in, tabs, _block_onehot_table(s))
    y_ret = _retention(rq, rk, rv, rg_act, _retention_tables())
    w1, w2, pe = _compress_weights(cmp_w1_k[0], cmp_w2_k[0], cmp_w1_v[0], cmp_w2_v[0],
                                   cmp_pe_k[0], cmp_pe_v[0])
    kvcmp, vct = _compress(kc, vc, w1, w2, pe, tabs_cmp)
    y_nsa = _nsa_attention(nq, kvcmp, vct, ksx, vst, kwx, vwt, gate_t, _overlap_t(s))
    out = _out_ffn(x.reshape(b * s, d), y_ret.reshape(b * s, -1), y_nsa.reshape(b * s, -1), mod,
                   w_out, ln_ffn_g[0], ln_final_g, w_ff_gate, w_ff_up, w_ff_down, s)
    return out.reshape(b, s, d)
```

```python
import numpy as np
import jax
import jax.numpy as jnp
from jax import lax
from jax.experimental import pallas as pl
from jax.experimental.pallas import tpu as pltpu

F32 = jnp.float32
BF16 = jnp.bfloat16

D_MODEL = 1024
RET_HEADS = 4
RET_DK = 64
RET_DV = 128
RET_CHUNK = 128
NSA_HEADS = 8
NSA_GROUPS = 2
NSA_HPG = NSA_HEADS // NSA_GROUPS
NSA_DH = 64
CMP_LEN = 32
CMP_STRIDE = 16
CMP_HIDDEN = 128
SLC_LEN = 64
SLC_TOPK = 16
WIN = 512
D_FF = ((8 * D_MODEL + 3 * 256 - 1) // (3 * 256)) * 256
ROPE_THETA = 10000.0
EPS = 1e-6
NEG = -1e30
FORCE = 1e6

LANES = 128
SUBLANES = 8
HALF = NSA_DH // 2
ONES_ROWS = 16
GATE_ROWS = 3 * NSA_HEADS

TM_IN = 1024
PROJ_W = 512
RET_GROUP = 8
Q_SCALE = NSA_DH ** -0.5 * float(np.log2(np.e))
TQ = 256
TILE_GROUP = 2
TM_FF = 512
TF = 256
CAST_ROWS = 512
VMEM_LIMIT = 56 * 1024 * 1024

_R_RQ, _R_RK, _R_RV, _R_RG, _R_NQ = 0, 256, 512, 1024, 1536
_R_KC, _R_VC, _R_KS, _R_VS, _R_KW, _R_VW, _R_GATE = 2048, 2176, 2304, 2432, 2560, 2688, 2816
IN_COLS_K = -(-(_R_GATE + 3 * NSA_HEADS) // PROJ_W) * PROJ_W


def _sigmoid(x):
    return 1.0 / (1.0 + jnp.exp(-x))


def _nt(a, b):
    return lax.dot_general(a, b, (((1,), (1,)), ((), ())), preferred_element_type=F32)


def _nn(a, b):
    return jnp.dot(a, b, preferred_element_type=F32)


def _rope_tile(a, c, s1, s2):
    return a * c + pltpu.roll(a, HALF, 1) * s1 + pltpu.roll(a, LANES - HALF, 1) * s2


def _swap_halves(a):
    return pltpu.roll(a, LANES // 2, 1)


def _rope_tables(pos, rotary_lanes):
    pos = np.asarray(pos, np.float64)
    lane = np.arange(LANES)
    within = lane % NSA_DH
    freq = ROPE_THETA ** (-(within % HALF).astype(np.float64) / HALF)
    ang = pos[:, None] * freq[None, :]
    cos, sin = np.cos(ang), np.sin(ang)
    first = (within < HALF)[None, :]
    rot = np.asarray(rotary_lanes, bool)[None, :]
    c = np.where(rot, cos, 1.0)
    s1 = np.where(rot & ~first, sin, 0.0)
    s2 = np.where(rot & first, -sin, 0.0)
    return (jnp.asarray(c, F32), jnp.asarray(s1, F32), jnp.asarray(s2, F32))


def _block_onehot_table(seq):
    t = np.zeros((seq, LANES), np.float32)
    pos = np.arange(seq)
    t[pos, NSA_DH + pos // SLC_LEN] = 1.0
    return jnp.asarray(t)


def _retention_tables():
    h = jnp.arange(RET_HEADS, dtype=F32)
    log_g = jnp.log(1.0 - 2.0 ** (-5.0 - h))
    c = RET_CHUNK
    idx = jnp.arange(c, dtype=F32)
    diff = idx[:, None] - idx[None, :]
    causal = diff >= 0
    decay = jnp.where(causal, jnp.exp(log_g[:, None, None] * jnp.where(causal, diff, 0.0)), 0.0)
    zeta = jnp.exp(log_g[:, None] * (c - 1.0 - idx))
    xi = jnp.exp(log_g[:, None] * (idx + 1.0))
    g_chunk = jnp.exp(log_g * c)

    def pair_lanes(t):
        t = t.reshape(RET_HEADS // 2, 2, c)
        return jnp.repeat(jnp.transpose(t, (0, 2, 1)), RET_DK, axis=2)

    g_b = jnp.broadcast_to(g_chunk[:, None, None], (RET_HEADS, 1, LANES))
    return decay, pair_lanes(zeta), pair_lanes(xi), g_b


def _overlap_t(seq):
    n_c = seq // CMP_STRIDE - CMP_LEN // CMP_STRIDE + 1
    nb = seq // SLC_LEN
    cs = np.arange(n_c) * CMP_STRIDE
    bs = np.arange(nb) * SLC_LEN
    ov = np.maximum(np.minimum(cs[:, None] + CMP_LEN, bs[None] + SLC_LEN)
                    - np.maximum(cs[:, None], bs[None]), 0).astype(np.float64) / CMP_LEN
    ncp = seq // CMP_STRIDE
    ovp = np.zeros((ncp, nb))
    ovp[:n_c] = ov
    return jnp.asarray(ovp.T, BF16)


def _ada_kernel(c_ref, w_ref, b_ref, o_ref):
    c = c_ref[...]
    o_ref[...] = _nn(c * _sigmoid(c), w_ref[...]) + b_ref[...]


def _ada(c, w, b):
    bsz, d = c.shape
    n = w.shape[1]
    tn = 1024
    return pl.pallas_call(
        _ada_kernel,
        grid=(n // tn,),
        in_specs=[pl.BlockSpec((bsz, d), lambda j: (0, 0)),
                  pl.BlockSpec((d, tn), lambda j: (0, j)),
                  pl.BlockSpec((1, tn), lambda j: (0, j))],
        out_specs=pl.BlockSpec((bsz, tn), lambda j: (0, j)),
        out_shape=jax.ShapeDtypeStruct((bsz, n), F32),
        compiler_params=pltpu.CompilerParams(vmem_limit_bytes=VMEM_LIMIT),
        name="ada",
    )(c, w, b.reshape(1, n))


def _inproj_kernel(x_ref, g_ref, mod_ref, w_ref, c_ref, s1_ref, s2_ref, hot_ref,
                   rq_ref, rk_ref, rv_ref, rg_ref, nq_ref, kc_ref, vc_ref,
                   ksx_ref, vst_ref, kwx_ref, vwt_ref, gate_ref, wb_ref):
    @pl.when((pl.program_id(0) == 0) & (pl.program_id(1) == 0))
    def _():
        n_in = w_ref.shape[2]
        whole = (n_in // PROJ_W) * PROJ_W
        for c0 in range(0, whole, PROJ_W):
            wb_ref[:, c0:c0 + PROJ_W] = w_ref[0, :, c0:c0 + PROJ_W].astype(BF16)
        wb_ref[:, whole:] = jnp.zeros((wb_ref.shape[0], wb_ref.shape[1] - whole), BF16)
        wb_ref[:, whole:n_in] = w_ref[0, :, whole:n_in].astype(BF16)

    def modulated_norm(rows):
        x = x_ref[0, rows, :]
        y = x * lax.rsqrt(jnp.mean(x * x, axis=-1, keepdims=True) + EPS) * g_ref[...]
        return (y * (1.0 + mod_ref[0, 1:2, :]) + mod_ref[0, 0:1, :]).astype(BF16)

    def product_steps(rows, hb):
        tabs = (c_ref[rows, :], s1_ref[rows, :], s2_ref[rows, :])
        low = lax.broadcasted_iota(jnp.int32, (rows.stop - rows.start, LANES), 1) < NSA_DH

        def proj(c0):
            a = _nn(hb, wb_ref[:, c0:c0 + PROJ_W])
            return [a[:, t * LANES:(t + 1) * LANES] for t in range(PROJ_W // LANES)]

        def put(out_ref, t, value):
            out_ref[0, rows, t * LANES:(t + 1) * LANES] = value.astype(out_ref.dtype)

        def roped(tiles, scale, out_ref):
            for t, a in enumerate(tiles):
                r = _rope_tile(a, *tabs)
                put(out_ref, t, r if scale == 1.0 else r * scale)

        def per_group(tile, fill, out_ref):
            put(out_ref, 0, jnp.where(low, tile, fill))
            put(out_ref, 1, jnp.where(low, _swap_halves(tile), fill))

        def per_group_t(tile, out_ref):
            t = tile.T
            ones = jnp.ones((ONES_ROWS, t.shape[1]), out_ref.dtype)
            for g in range(NSA_GROUPS):
                out_ref[0, g, 0:NSA_DH, rows] = t[g * NSA_DH:(g + 1) * NSA_DH].astype(out_ref.dtype)
                out_ref[0, g, NSA_DH:NSA_DH + ONES_ROWS, rows] = ones

        def retention_qk():
            tiles = proj(_R_RQ)
            roped(tiles[0:2], 1.0, rq_ref)
            roped(tiles[2:4], RET_DK ** -0.5, rk_ref)

        def retention_v():
            for t, a in enumerate(proj(_R_RV)):
                put(rv_ref, t, a)

        def retention_gate():
            for t, a in enumerate(proj(_R_RG)):
                put(rg_ref, t, a * _sigmoid(a))

        def nsa_q():
            roped(proj(_R_NQ), Q_SCALE, nq_ref)

        def nsa_compress_selected():
            kc_t, vc_t, ks_t, vs_t = proj(_R_KC)
            kc_ref[0, rows, :] = kc_t
            vc_ref[0, rows, :] = vc_t
            per_group(_rope_tile(ks_t, *tabs), hot_ref[rows, :], ksx_ref)
            per_group_t(vs_t, vst_ref)

        def nsa_window_gates():
            kw_t, vw_t, gates_t, _ = proj(_R_KW)
            per_group(_rope_tile(kw_t, *tabs), 0.0, kwx_ref)
            per_group_t(vw_t, vwt_ref)
            gate_ref[0, :, rows] = _sigmoid(gates_t.T[0:GATE_ROWS])

        return [retention_qk, retention_v, retention_gate, nsa_q, nsa_compress_selected, nsa_window_gates]

    tm = x_ref.shape[1]
    rows_a, rows_b = slice(0, tm // 2), slice(tm // 2, tm)
    steps_a = product_steps(rows_a, modulated_norm(rows_a))
    steps_a[0]()
    steps_b = product_steps(rows_b, modulated_norm(rows_b))
    for step in steps_a[1:] + steps_b:
        step()


def _in_proj(x, ln_g, mod, w_in, tabs, hot):
    b, s, d = x.shape
    tm = TM_IN
    grid = (b, s // tm)
    tab_spec = pl.BlockSpec((tm, LANES), lambda bi, j: (j, 0))

    def out(n, dtype):
        return (jax.ShapeDtypeStruct((b, s, n), dtype), pl.BlockSpec((1, tm, n), lambda bi, j: (bi, j, 0)))

    def out_t(rows, dtype):
        return (jax.ShapeDtypeStruct((b, NSA_GROUPS, rows, s), dtype),
                pl.BlockSpec((1, NSA_GROUPS, rows, tm), lambda bi, j: (bi, 0, 0, j)))

    vt_rows = NSA_DH + ONES_ROWS
    outs = [out(256, BF16), out(256, BF16), out(512, BF16), out(512, BF16), out(512, BF16),
            out(LANES, F32), out(LANES, F32),
            out(256, BF16), out_t(vt_rows, BF16), out(256, BF16), out_t(vt_rows, BF16),
            (jax.ShapeDtypeStruct((b, GATE_ROWS, s), F32),
             pl.BlockSpec((1, GATE_ROWS, tm), lambda bi, j: (bi, 0, j)))]
    return pl.pallas_call(
        _inproj_kernel,
        grid=grid,
        in_specs=[pl.BlockSpec((1, tm, d), lambda bi, j: (bi, j, 0)),
                  pl.BlockSpec((1, d), lambda bi, j: (0, 0)),
                  pl.BlockSpec((1, 6, d), lambda bi, j: (bi, 0, 0)),
                  pl.BlockSpec((1,) + w_in.shape[1:], lambda bi, j: (0, 0, 0), pipeline_mode=pl.Buffered(1)),
                  tab_spec, tab_spec, tab_spec, tab_spec],
        out_specs=[o[1] for o in outs],
        out_shape=[o[0] for o in outs],
        scratch_shapes=[pltpu.VMEM((d, IN_COLS_K), BF16)],
        compiler_params=pltpu.CompilerParams(
            dimension_semantics=("arbitrary", "arbitrary"), vmem_limit_bytes=VMEM_LIMIT),
        name="in_proj",
    )(x, ln_g.reshape(1, d), mod, w_in, *tabs, hot)


def _ret_kernel(q_ref, k_ref, v_ref, rg_ref, dec_ref, zeta_ref, xi_ref, gch_ref, o_ref, kv_ref, prev_ref):
    c = RET_CHUNK
    n_chunks = q_ref.shape[1] // c
    low = lax.broadcasted_iota(jnp.int32, (c, LANES), 1) < RET_DK

    def chunk_rows(n):
        return pl.ds(pl.multiple_of(n * c, c), c)

    def head_cols(h):
        return slice(h * RET_DV, (h + 1) * RET_DV)

    def kv_body(it, carry):
        kz_t = {}
        for j in range(RET_GROUP):
            rows = chunk_rows(it * RET_GROUP + j)
            for p in range(RET_HEADS // 2):
                pair = slice(p * LANES, (p + 1) * LANES)
                kz_t[j, p] = (k_ref[0, rows, pair].astype(F32) * zeta_ref[p]).T.astype(BF16)
        for j in range(RET_GROUP):
            n = it * RET_GROUP + j
            for h in range(RET_HEADS):
                kv_ref[h, n] = _nn(kz_t[j, h // 2], v_ref[0, chunk_rows(n), head_cols(h)])
        return carry

    lax.fori_loop(0, n_chunks // RET_GROUP, kv_body, 0)

    for h in range(RET_HEADS):
        def scan_body(n, st, h=h):
            prev_ref[h, n] = st.astype(prev_ref.dtype)
            return st * gch_ref[h] + kv_ref[h, n]
        lax.fori_loop(0, n_chunks, scan_body, jnp.zeros((LANES, RET_DV), F32))

    def out_body(it, carry):
        chains = [(j, h) for j in range(RET_GROUP) for h in range(RET_HEADS)]
        chunk = lambda j: it * RET_GROUP + j
        q_own, qx_own, att, ys = {}, {}, {}, {}
        for j, h in chains:
            p, e = divmod(h, 2)
            pair = slice(p * LANES, (p + 1) * LANES)
            mine = low if e == 0 else jnp.logical_not(low)
            q2 = q_ref[0, chunk_rows(chunk(j)), pair].astype(F32)
            q_own[j, h] = jnp.where(mine, q2, 0.0).astype(BF16)
            qx_own[j, h] = jnp.where(mine, q2 * xi_ref[p], 0.0).astype(BF16)
        for j, h in chains:
            pair = slice((h // 2) * LANES, (h // 2 + 1) * LANES)
            att[j, h] = _nt(q_own[j, h], k_ref[0, chunk_rows(chunk(j)), pair])
        for j, h in chains:
            lhs = jnp.concatenate([(att[j, h] * dec_ref[h]).astype(BF16), qx_own[j, h]], axis=1)
            rhs = jnp.concatenate([v_ref[0, chunk_rows(chunk(j)), head_cols(h)], prev_ref[h, chunk(j)]], axis=0)
            ys[j, h] = _nn(lhs, rhs)
        for j, h in chains:
            y = ys[j, h]
            yn = y * lax.rsqrt(jnp.mean(y * y, axis=-1, keepdims=True) + EPS)
            gate = rg_ref[0, chunk_rows(chunk(j)), head_cols(h)].astype(F32)
            o_ref[0, chunk_rows(chunk(j)), head_cols(h)] = (yn * gate).astype(o_ref.dtype)
        return carry

    lax.fori_loop(0, n_chunks // RET_GROUP, out_body, 0)


def _retention(rq, rk, rv, rg_act, tables):
    b, s, _ = rq.shape
    decay, zeta_p, xi_p, g_b = tables
    whole = lambda a: pl.BlockSpec(a.shape, lambda bi: (0,) * a.ndim)
    row = lambda a: pl.BlockSpec((1,) + a.shape[1:], lambda bi: (bi, 0, 0))
    return pl.pallas_call(
        _ret_kernel,
        grid=(b,),
        in_specs=[row(rq), row(rk), row(rv), row(rg_act),
                  whole(decay), whole(zeta_p), whole(xi_p), whole(g_b)],
        out_specs=row(rv),
        out_shape=jax.ShapeDtypeStruct(rv.shape, BF16),
        scratch_shapes=[pltpu.VMEM((RET_HEADS, s // RET_CHUNK, LANES, RET_DV), F32),
                        pltpu.VMEM((RET_HEADS, s // RET_CHUNK, LANES, RET_DV), BF16)],
        compiler_params=pltpu.CompilerParams(
            dimension_semantics=("parallel",), vmem_limit_bytes=VMEM_LIMIT),
        name="retention",
    )(rq, rk, rv, rg_act, decay, zeta_p, xi_p, g_b)


def _cmp_kernel(kc_ref, vc_ref, w1_ref, pe_ref, w2_ref, c_ref, s1_ref, s2_ref, kv_ref, vt_ref):
    n_piece = kc_ref.shape[1] // CMP_STRIDE
    halves = CMP_LEN // CMP_STRIDE
    lhs = {}
    for t, src in enumerate((kc_ref, vc_ref)):
        rows = [src[0, pl.ds(r, n_piece, stride=CMP_STRIDE), :] for r in range(CMP_STRIDE)]
        for half in range(halves):
            pe0 = half * CMP_STRIDE
            lhs[t, half] = jnp.concatenate(
                [(rows[r] + pe_ref[t, pe0 + r:pe0 + r + 1, :]).astype(BF16) for r in range(CMP_STRIDE)], axis=1)
    part = {key: _nn(lhs[key], w1_ref[key[0], key[1]]) for key in lhs}
    out = jnp.zeros((n_piece, kv_ref.shape[2]), F32)
    for t in range(2):
        hid = part[t, 0]
        for half in range(1, halves):
            hid = hid + pltpu.roll(part[t, half], n_piece - half, 0)
        out = out + _nn((hid * _sigmoid(hid)).astype(BF16), w2_ref[t])
    for t in range(out.shape[1] // LANES):
        sl = slice(t * LANES, (t + 1) * LANES)
        kv = _rope_tile(out[:, sl], c_ref[...], s1_ref[...], s2_ref[...])
        kv_ref[0, :, sl] = kv.astype(kv_ref.dtype)
        vt_ref[0, t] = kv.T[NSA_DH:2 * NSA_DH].astype(vt_ref.dtype)


def _compress_weights(w1_k, w2_k, w1_v, w2_v, pe_k, pe_v):
    dh, hid = NSA_DH, CMP_HIDDEN

    def first(w1):
        w = w1.reshape(CMP_LEN, dh, hid)
        z = jnp.zeros_like(w)
        return jnp.concatenate([jnp.concatenate([w, z], axis=2), jnp.concatenate([z, w], axis=2)], axis=1)

    def second(w2, off):
        z = jnp.zeros_like(w2)
        rows = []
        for g in range(NSA_GROUPS):
            c = [z, z, z, z]
            c[2 * g + off] = w2
            rows.append(jnp.concatenate(c, axis=1))
        return jnp.concatenate(rows, axis=0)

    w1 = jnp.stack([first(w1_k), first(w1_v)]).astype(BF16).reshape(
        2, CMP_LEN // CMP_STRIDE, CMP_STRIDE * NSA_GROUPS * dh, NSA_GROUPS * hid)
    w2 = jnp.stack([second(w2_k, 0), second(w2_v, 1)]).astype(BF16)
    pe = jnp.stack([jnp.concatenate([pe_k, pe_k], axis=1), jnp.concatenate([pe_v, pe_v], axis=1)])
    return w1, w2, pe


def _compress(kc, vc, w1, w2, pe, tabs_cmp):
    b, s, w = kc.shape
    n_piece = s // CMP_STRIDE
    const2 = lambda bi: (0, 0)
    src = pl.BlockSpec((1, s, w), lambda bi: (bi, 0, 0))
    dst = pl.BlockSpec((1, n_piece, 2 * w), lambda bi: (bi, 0, 0))
    shape = jax.ShapeDtypeStruct((b, n_piece, 2 * w), BF16)
    return pl.pallas_call(
        _cmp_kernel,
        grid=(b,),
        in_specs=[src, src,
                  pl.BlockSpec(w1.shape, lambda bi: (0, 0, 0, 0)),
                  pl.BlockSpec(pe.shape, lambda bi: (0, 0, 0)),
                  pl.BlockSpec(w2.shape, lambda bi: (0, 0, 0)),
                  pl.BlockSpec((n_piece, LANES), const2),
                  pl.BlockSpec((n_piece, LANES), const2),
                  pl.BlockSpec((n_piece, LANES), const2)],
        out_specs=[dst, pl.BlockSpec((1, NSA_GROUPS, NSA_DH, n_piece), lambda bi: (bi, 0, 0, 0))],
        out_shape=[shape, jax.ShapeDtypeStruct((b, NSA_GROUPS, NSA_DH, n_piece), BF16)],
        compiler_params=pltpu.CompilerParams(
            dimension_semantics=("parallel",), vmem_limit_bytes=VMEM_LIMIT),
        name="compress",
    )(kc, vc, w1, pe, w2, *tabs_cmp)


def _nsa_kernel(q_ref, kcmp_ref, vct_ref, ksx_ref, vst_ref, kwx_ref, vwt_ref, gate_ref, ovt_ref, o_ref):
    tq = TQ
    seq = q_ref.shape[1]
    nb = seq // SLC_LEN
    n_cmp = kcmp_ref.shape[1]
    hpg, dh = NSA_HPG, NSA_DH
    assert WIN % tq == 0 and seq % tq == 0
    group = pl.program_id(1)

    low = lax.broadcasted_iota(jnp.int32, (tq, LANES), 1) < dh
    eye = jnp.where(lax.broadcasted_iota(jnp.int32, (tq, tq), 0)
                    == lax.broadcasted_iota(jnp.int32, (tq, tq), 1), 1.0, 0.0).astype(BF16)
    blk = lax.broadcasted_iota(jnp.int32, (nb, tq), 0)
    col = lax.broadcasted_iota(jnp.int32, (nb, tq), 1)
    crow = lax.broadcasted_iota(jnp.int32, (n_cmp, tq), 0)
    ccol = lax.broadcasted_iota(jnp.int32, (n_cmp, tq), 1)
    kcm = kcmp_ref[0]
    vct = vct_ref[0, 0]
    ovt = ovt_ref[...]

    def per_head(x):
        return jnp.concatenate([x] * hpg, axis=1)

    key_off = lax.broadcasted_iota(jnp.int32, (tq, tq), 0)
    qry_off = lax.broadcasted_iota(jnp.int32, (tq, tq), 1)
    not_after = per_head(jnp.where(key_off <= qry_off, 0.0, NEG))
    inside_win = per_head(jnp.where(key_off > qry_off, 0.0, NEG))

    def masked(s, first_key, t0, windowed):
        blocks = []
        for r in range(0, s.shape[0], tq):
            blk_s = s[r:r + tq]
            if first_key + r == t0:
                blk_s = blk_s + not_after
            elif windowed and first_key + r == t0 - WIN:
                blk_s = blk_s + inside_win
            blocks.append(blk_s)
        return jnp.concatenate(blocks, axis=0)

    def normalise(acc):
        return acc[0:dh] / acc[dh:dh + 1]

    def select_blocks(psum, t0):
        tcol = col + t0
        bcausal = blk * SLC_LEN <= tcol
        n_live = (t0 + tq - 1) // SLC_LEN + 1
        top_n = min(SLC_TOPK, nb)
        if n_live <= top_n:
            bias = jnp.where(bcausal, 0.0, NEG)
        else:
            p_hi = psum.astype(BF16)
            p_lo = (psum - p_hi.astype(F32)).astype(BF16)
            imp = _nn(ovt, p_hi) + _nn(ovt, p_lo)
            cur = tcol // SLC_LEN
            forced = (blk == 0) | (blk == cur) | (blk == cur - 1)
            imp = jnp.where(bcausal, jnp.where(forced, FORCE, imp), NEG)
            rank = jnp.zeros((nb, tq), F32)
            for j in range(n_live):
                r = imp[j:j + 1, :]
                rank = rank + jnp.where(blk > j, jnp.where(r >= imp, 1.0, 0.0), jnp.where(r > imp, 1.0, 0.0))
            bias = jnp.where((rank < float(top_n)) & bcausal, 0.0, NEG)
        feat = jnp.concatenate([jnp.zeros((dh, tq), F32), bias,
                                jnp.zeros((LANES - dh - nb, tq), F32)], axis=0).astype(BF16)
        return _nt(eye, feat)

    class TileGroup:
        def __init__(self, tiles):
            self.tiles = tiles
            self.wstart = {t0: max(t0 - WIN, 0) for t0 in tiles}
            self.wkeys = {t0: slice(self.wstart[t0], t0 + tq) for t0 in tiles}
            self.skeys = {t0: slice(0, t0 + tq) for t0 in tiles}

        def scores_window_compressed(self):
            self.heads, self.qs = {}, {}
            for t0 in self.tiles:
                qf = q_ref[0, t0:t0 + tq, :].astype(F32)
                hl = []
                for hh in range(hpg):
                    t = qf[:, (hh // 2) * LANES:(hh // 2 + 1) * LANES]
                    if hh % 2 == 1:
                        t = _swap_halves(t)
                    hl.append(jnp.where(low, t, 0.0))
                self.heads[t0] = hl
                self.qs[t0] = jnp.concatenate(hl, axis=0).astype(BF16)
            self.s_w = {t0: _nt(kwx_ref[0, self.wkeys[t0], :], self.qs[t0]) for t0 in self.tiles}
            self.s_c = {t0: _nt(kcm, self.qs[t0]) for t0 in self.tiles}

        def select_and_scores_selected(self):
            self.p_cmp, qsel = {}, {}
            for t0 in self.tiles:
                cmask = (crow * CMP_STRIDE + (CMP_LEN - 1)) <= (ccol + t0)
                p_all = []
                psum = jnp.zeros((n_cmp, tq), F32)
                for hh in range(hpg):
                    sh = jnp.where(cmask, self.s_c[t0][:, hh * tq:(hh + 1) * tq], NEG)
                    e = jnp.exp2(sh - jnp.max(sh, axis=0, keepdims=True))
                    p = jnp.where(cmask, e / jnp.sum(e, axis=0, keepdims=True), 0.0)
                    psum = psum + p
                    p_all.append(p.astype(BF16))
                self.p_cmp[t0] = jnp.concatenate(p_all, axis=1)
                qbias = select_blocks(psum, t0)
                qsel[t0] = jnp.concatenate([hd + qbias for hd in self.heads[t0]], axis=0).astype(BF16)
            self.s_s = {t0: _nt(ksx_ref[0, self.skeys[t0], :], qsel[t0]) for t0 in self.tiles}

        def outputs_window_compressed(self):
            e_w = {}
            for t0 in self.tiles:
                sw = masked(self.s_w[t0], self.wstart[t0], t0, True)
                e_w[t0] = jnp.exp2(sw - jnp.max(sw, axis=0, keepdims=True)).astype(BF16)
            self.o_win = {t0: normalise(_nn(vwt_ref[0, 0, :, self.wkeys[t0]], e_w[t0])) for t0 in self.tiles}
            self.o_cmp = {t0: _nn(vct, self.p_cmp[t0]) for t0 in self.tiles}

        def outputs_selected_and_store(self):
            e_s = {}
            for t0 in self.tiles:
                ss = masked(self.s_s[t0], 0, t0, False)
                e_s[t0] = jnp.exp2(ss - jnp.max(ss, axis=0, keepdims=True)).astype(BF16)
            o_sel = {t0: normalise(_nn(vst_ref[0, 0, :, self.skeys[t0]], e_s[t0])) for t0 in self.tiles}
            for t0 in self.tiles:
                gt = gate_ref[0, :, t0:t0 + tq]
                outs = []
                for hh in range(hpg):
                    hc = slice(hh * tq, (hh + 1) * tq)

                    def gate_row(branch):
                        by_group = [gt[branch * NSA_HEADS + g * hpg + hh:branch * NSA_HEADS + g * hpg + hh + 1, :]
                                    for g in range(NSA_GROUPS)]
                        row = by_group[-1]
                        for g in range(NSA_GROUPS - 2, -1, -1):
                            row = jnp.where(group == g, by_group[g], row)
                        return row

                    outs.append(gate_row(0) * self.o_cmp[t0][:, hc] + gate_row(1) * o_sel[t0][:, hc]
                                + gate_row(2) * self.o_win[t0][:, hc])
                o_ref[0, t0:t0 + tq, :] = jnp.concatenate(outs, axis=0).T.astype(o_ref.dtype)

    starts = list(range(0, seq, tq))
    groups = [TileGroup(starts[i:i + TILE_GROUP]) for i in range(0, len(starts), TILE_GROUP)]
    n = len(groups)
    groups[0].scores_window_compressed()
    for i, g in enumerate(groups):
        if i + 1 < n:
            groups[i + 1].scores_window_compressed()
        g.select_and_scores_selected()
        g.outputs_window_compressed()
        if i > 0:
            groups[i - 1].outputs_selected_and_store()
    groups[n - 1].outputs_selected_and_store()


def _nsa_attention(nq, kvcmp, vct, ksx, vst, kwx, vwt, gate_t, ovt):
    b, s, _ = nq.shape
    n_cmp = kvcmp.shape[1]
    gw = NSA_HPG * NSA_DH
    per_group = lambda rows, width: pl.BlockSpec((1, rows, width), lambda bi, g: (bi, 0, g))
    per_group_t = lambda a: pl.BlockSpec((1, 1) + a.shape[2:], lambda bi, g: (bi, g, 0, 0))
    return pl.pallas_call(
        _nsa_kernel,
        grid=(b, NSA_GROUPS),
        in_specs=[per_group(s, gw),
                  per_group(n_cmp, LANES), per_group_t(vct),
                  per_group(s, LANES), per_group_t(vst), per_group(s, LANES), per_group_t(vwt),
                  pl.BlockSpec((1,) + gate_t.shape[1:], lambda bi, g: (bi, 0, 0)),
                  pl.BlockSpec(ovt.shape, lambda bi, g: (0, 0))],
        out_specs=per_group(s, gw),
        out_shape=jax.ShapeDtypeStruct((b, s, NSA_HEADS * NSA_DH), BF16),
        compiler_params=pltpu.CompilerParams(
            dimension_semantics=("parallel", "parallel"), vmem_limit_bytes=VMEM_LIMIT),
        name="nsa_attn",
    )(nq, kvcmp, vct, ksx, vst, kwx, vwt, gate_t, ovt)


def _ffn_kernel(x_ref, yr_ref, yn_ref, mod_ref, g2_ref, gf_ref, wo_hbm, wg_hbm, wu_hbm, wd_hbm,
                o_ref, x1_ref, act_ref, wo_ref, wg_ref, wu_ref, wd_ref, stage_ref, stage_sem):
    @pl.when(pl.program_id(0) == 0)
    def _():
        jobs = [(src, dst, r0, min(CAST_ROWS, dst.shape[0] - r0))
                for src, dst in ((wo_hbm, wo_ref), (wg_hbm, wg_ref), (wu_hbm, wu_ref), (wd_hbm, wd_ref))
                for r0 in range(0, dst.shape[0], CAST_ROWS)]

        def staged(k):
            src, dst, r0, rows = jobs[k]
            slot = k % 2
            return pltpu.make_async_copy(src.at[0, pl.ds(r0, rows), :],
                                         stage_ref.at[slot, pl.ds(0, rows), pl.ds(0, dst.shape[1])],
                                         stage_sem.at[slot])

        staged(0).start()
        for k, (src, dst, r0, rows) in enumerate(jobs):
            if k + 1 < len(jobs):
                staged(k + 1).start()
            staged(k).wait()
            dst[r0:r0 + rows, :] = stage_ref[k % 2, 0:rows, 0:dst.shape[1]].astype(BF16)

    half_w = yr_ref.shape[1]
    d_ff = wg_ref.shape[1]
    n_chunks = d_ff // TF
    rows_a = slice(0, x_ref.shape[0] // 2)
    rows_b = slice(x_ref.shape[0] // 2, x_ref.shape[0])

    def mix(rows):
        return _nn(yr_ref[rows, :], wo_ref[0:half_w, :]) + _nn(yn_ref[rows, :], wo_ref[half_w:2 * half_w, :])

    def mid_norm(rows, mixed):
        x1 = x_ref[rows, :] + mod_ref[0, 2:3, :] * mixed
        x1_ref[rows, :] = x1
        y = x1 * lax.rsqrt(jnp.mean(x1 * x1, axis=-1, keepdims=True) + EPS) * g2_ref[...]
        return (y * (1.0 + mod_ref[0, 4:5, :]) + mod_ref[0, 3:4, :]).astype(BF16)

    def ff_chunk(rows, h2, j):
        sl = slice(j * TF, (j + 1) * TF)
        gate = _nn(h2, wg_ref[:, sl])
        up = _nn(h2, wu_ref[:, sl])
        act_ref[rows, sl] = (gate * _sigmoid(gate) * up).astype(BF16)

    def down(rows):
        return x1_ref[rows, :] + mod_ref[0, 5:6, :] * _nn(act_ref[rows, :], wd_ref[...])

    def final_norm(rows, xo):
        o_ref[rows, :] = xo * lax.rsqrt(jnp.mean(xo * xo, axis=-1, keepdims=True) + EPS) * gf_ref[...]

    mix_a = mix(rows_a)
    mix_b = mix(rows_b)
    h2_a = mid_norm(rows_a, mix_a)
    ff_chunk(rows_a, h2_a, 0)
    h2_b = mid_norm(rows_b, mix_b)
    for j in range(1, n_chunks):
        ff_chunk(rows_a, h2_a, j)
    xo_a = down(rows_a)
    ff_chunk(rows_b, h2_b, 0)
    final_norm(rows_a, xo_a)
    for j in range(1, n_chunks):
        ff_chunk(rows_b, h2_b, j)
    final_norm(rows_b, down(rows_b))


def _out_ffn(x2d, y_ret, y_nsa, mod, w_out, g2, gf, wg, wu, wd, seq):
    n, d = x2d.shape
    tm = TM_FF
    d_ff = wg.shape[2]
    tiles_per_seq = seq // tm
    half_w = y_ret.shape[1]
    row = lambda i: (i, 0)
    in_hbm = pl.BlockSpec(memory_space=pl.ANY)
    weights = (w_out, wg, wu, wd)
    stage_cols = max(w.shape[2] for w in weights)
    return pl.pallas_call(
        _ffn_kernel,
        grid=(n // tm,),
        in_specs=[pl.BlockSpec((tm, d), row),
                  pl.BlockSpec((tm, half_w), row),
                  pl.BlockSpec((tm, half_w), row),
                  pl.BlockSpec((1, 6, d), lambda i: (i // tiles_per_seq, 0, 0)),
                  pl.BlockSpec((1, d), lambda i: (0, 0)),
                  pl.BlockSpec((1, d), lambda i: (0, 0)),
                  in_hbm, in_hbm, in_hbm, in_hbm],
        out_specs=pl.BlockSpec((tm, d), row),
        out_shape=jax.ShapeDtypeStruct((n, d), F32),
        scratch_shapes=[pltpu.VMEM((tm, d), F32), pltpu.VMEM((tm, d_ff), BF16)]
                       + [pltpu.VMEM(w.shape[1:], BF16) for w in weights]
                       + [pltpu.VMEM((2, CAST_ROWS, stage_cols), F32), pltpu.SemaphoreType.DMA((2,))],
        compiler_params=pltpu.CompilerParams(
            dimension_semantics=("arbitrary",), vmem_limit_bytes=VMEM_LIMIT),
        name="out_ffn",
    )(x2d, y_ret, y_nsa, mod, g2.reshape(1, d), gf.reshape(1, d), *weights)


def kernel(x, c, ln_mix_g, ln_ffn_g, w_ada, b_ada, w_in, cmp_pe_k, cmp_w1_k, cmp_w2_k,
           cmp_pe_v, cmp_w1_v, cmp_w2_v, w_out, w_ff_gate, w_ff_up, w_ff_down, ln_final_g):
    assert w_in.shape[0] == 1, "the final RMSNorm is fused into the (single) layer's FFN kernel"
    b, s, d = x.shape
    lane = np.arange(LANES)
    tabs = _rope_tables(np.arange(s), np.ones(LANES, bool))
    n_piece = s // CMP_STRIDE
    tabs_cmp = _rope_tables(np.arange(n_piece) * CMP_STRIDE + CMP_LEN - 1, lane < NSA_DH)

    mod = _ada(c, w_ada[0], b_ada[0]).reshape(b, 6, d)
    rq, rk, rv, rg_act, nq, kc, vc, ksx, vst, kwx, vwt, gate_t = _in_proj(
        x, ln_mix_g[0], mod, w_in, tabs, _block_onehot_table(s))
    y_ret = _retention(rq, rk, rv, rg_act, _retention_tables())
    w1, w2, pe = _compress_weights(cmp_w1_k[0], cmp_w2_k[0], cmp_w1_v[0], cmp_w2_v[0],
                                   cmp_pe_k[0], cmp_pe_v[0])
    kvcmp, vct = _compress(kc, vc, w1, w2, pe, tabs_cmp)
    y_nsa = _nsa_attention(nq, kvcmp, vct, ksx, vst, kwx, vwt, gate_t, _overlap_t(s))
    out = _out_ffn(x.reshape(b * s, d), y_ret.reshape(b * s, -1), y_nsa.reshape(b * s, -1), mod,
                   w_out, ln_ffn_g[0], ln_final_g, w_ff_gate, w_ff_up, w_ff_down, s)
    return out.reshape(b, s, d)
```

```python
import numpy as np
import jax
import jax.numpy as jnp
from jax import lax
from jax.experimental import pallas as pl
from jax.experimental.pallas import tpu as pltpu

F32 = jnp.float32
BF16 = jnp.bfloat16

D_MODEL = 1024
RET_HEADS = 4
RET_DK = 64
RET_DV = 128
RET_CHUNK = 128
NSA_HEADS = 8
NSA_GROUPS = 2
NSA_HPG = NSA_HEADS // NSA_GROUPS
NSA_DH = 64
CMP_LEN = 32
CMP_STRIDE = 16
CMP_HIDDEN = 128
SLC_LEN = 64
SLC_TOPK = 16
WIN = 512
D_FF = ((8 * D_MODEL + 3 * 256 - 1) // (3 * 256)) * 256
ROPE_THETA = 10000.0
EPS = 1e-6
NEG = -1e30
FORCE = 1e6

LANES = 128
SUBLANES = 8
HALF = NSA_DH // 2
ONES_ROWS = 16
GATE_ROWS = 3 * NSA_HEADS

TM_IN = 1024
PROJ_W = 512
RET_GROUP = 8
Q_SCALE = NSA_DH ** -0.5 * float(np.log2(np.e))
TQ = 128
TILE_GROUP = 2
TM_FF = 512
TF = 256
CAST_ROWS = 512
VMEM_LIMIT = 56 * 1024 * 1024

_R_RQ, _R_RK, _R_RV, _R_RG, _R_NQ = 0, 256, 512, 1024, 1536
_R_KC, _R_VC, _R_KS, _R_VS, _R_KW, _R_VW, _R_GATE = 2048, 2176, 2304, 2432, 2560, 2688, 2816
IN_COLS_K = -(-(_R_GATE + 3 * NSA_HEADS) // PROJ_W) * PROJ_W


def _sigmoid(x):
    return 1.0 / (1.0 + jnp.exp(-x))


def _nt(a, b):
    return lax.dot_general(a, b, (((1,), (1,)), ((), ())), preferred_element_type=F32)


def _nn(a, b):
    return jnp.dot(a, b, preferred_element_type=F32)


def _rope_tile(a, c, s1, s2):
    return a * c + pltpu.roll(a, HALF, 1) * s1 + pltpu.roll(a, LANES - HALF, 1) * s2


def _swap_halves(a):
    return pltpu.roll(a, LANES // 2, 1)


def _rope_tables(pos, rotary_lanes):
    pos = np.asarray(pos, np.float64)
    lane = np.arange(LANES)
    within = lane % NSA_DH
    freq = ROPE_THETA ** (-(within % HALF).astype(np.float64) / HALF)
    ang = pos[:, None] * freq[None, :]
    cos, sin = np.cos(ang), np.sin(ang)
    first = (within < HALF)[None, :]
    rot = np.asarray(rotary_lanes, bool)[None, :]
    c = np.where(rot, cos, 1.0)
    s1 = np.where(rot & ~first, sin, 0.0)
    s2 = np.where(rot & first, -sin, 0.0)
    return (jnp.asarray(c, F32), jnp.asarray(s1, F32), jnp.asarray(s2, F32))


def _block_onehot_table(seq):
    t = np.zeros((seq, LANES), np.float32)
    pos = np.arange(seq)
    t[pos, NSA_DH + pos // SLC_LEN] = 1.0
    return jnp.asarray(t)


def _retention_tables():
    h = jnp.arange(RET_HEADS, dtype=F32)
    log_g = jnp.log(1.0 - 2.0 ** (-5.0 - h))
    c = RET_CHUNK
    idx = jnp.arange(c, dtype=F32)
    diff = idx[:, None] - idx[None, :]
    causal = diff >= 0
    decay = jnp.where(causal, jnp.exp(log_g[:, None, None] * jnp.where(causal, diff, 0.0)), 0.0)
    zeta = jnp.exp(log_g[:, None] * (c - 1.0 - idx))
    xi = jnp.exp(log_g[:, None] * (idx + 1.0))
    g_chunk = jnp.exp(log_g * c)

    def pair_lanes(t):
        t = t.reshape(RET_HEADS // 2, 2, c)
        return jnp.repeat(jnp.transpose(t, (0, 2, 1)), RET_DK, axis=2)

    g_b = jnp.broadcast_to(g_chunk[:, None, None], (RET_HEADS, 1, LANES))
    return decay, pair_lanes(zeta), pair_lanes(xi), g_b


def _overlap_t(seq):
    n_c = seq // CMP_STRIDE - CMP_LEN // CMP_STRIDE + 1
    nb = seq // SLC_LEN
    cs = np.arange(n_c) * CMP_STRIDE
    bs = np.arange(nb) * SLC_LEN
    ov = np.maximum(np.minimum(cs[:, None] + CMP_LEN, bs[None] + SLC_LEN)
                    - np.maximum(cs[:, None], bs[None]), 0).astype(np.float64) / CMP_LEN
    ncp = seq // CMP_STRIDE
    ovp = np.zeros((ncp, nb))
    ovp[:n_c] = ov
    return jnp.asarray(ovp.T, BF16)


def _ada_kernel(c_ref, w_ref, b_ref, o_ref):
    c = c_ref[...]
    o_ref[...] = _nn(c * _sigmoid(c), w_ref[...]) + b_ref[...]


def _ada(c, w, b):
    bsz, d = c.shape
    n = w.shape[1]
    tn = 1024
    return pl.pallas_call(
        _ada_kernel,
        grid=(n // tn,),
        in_specs=[pl.BlockSpec((bsz, d), lambda j: (0, 0)),
                  pl.BlockSpec((d, tn), lambda j: (0, j)),
                  pl.BlockSpec((1, tn), lambda j: (0, j))],
        out_specs=pl.BlockSpec((bsz, tn), lambda j: (0, j)),
        out_shape=jax.ShapeDtypeStruct((bsz, n), F32),
        compiler_params=pltpu.CompilerParams(vmem_limit_bytes=VMEM_LIMIT),
        name="ada",
    )(c, w, b.reshape(1, n))


def _stream_staged(n_jobs, copy_of, consume):
    copy_of(0).start()
    for k in range(n_jobs):
        if k + 1 < n_jobs:
            copy_of(k + 1).start()
        copy_of(k).wait()
        consume(k)


def _inproj_kernel(x_ref, g_ref, mod_ref, w_ref, c_ref, s1_ref, s2_ref, hot_ref,
                   rq_ref, rk_ref, rv_ref, rg_ref, nq_ref, kc_ref, vc_ref,
                   ksx_ref, vst_ref, kwx_ref, vwt_ref, gate_ref, wb_ref):
    @pl.when((pl.program_id(0) == 0) & (pl.program_id(1) == 0))
    def _():
        n_in = w_ref.shape[2]
        whole = (n_in // PROJ_W) * PROJ_W
        for c0 in range(0, whole, PROJ_W):
            wb_ref[:, c0:c0 + PROJ_W] = w_ref[0, :, c0:c0 + PROJ_W].astype(BF16)
        wb_ref[:, whole:] = jnp.zeros((wb_ref.shape[0], wb_ref.shape[1] - whole), BF16)
        wb_ref[:, whole:n_in] = w_ref[0, :, whole:n_in].astype(BF16)

    def modulated_norm(rows):
        x = x_ref[0, rows, :]
        y = x * lax.rsqrt(jnp.mean(x * x, axis=-1, keepdims=True) + EPS) * g_ref[...]
        return (y * (1.0 + mod_ref[0, 1:2, :]) + mod_ref[0, 0:1, :]).astype(BF16)

    def product_steps(rows, hb):
        tabs = (c_ref[rows, :], s1_ref[rows, :], s2_ref[rows, :])
        low = lax.broadcasted_iota(jnp.int32, (rows.stop - rows.start, LANES), 1) < NSA_DH

        def proj(c0):
            a = _nn(hb, wb_ref[:, c0:c0 + PROJ_W])
            return [a[:, t * LANES:(t + 1) * LANES] for t in range(PROJ_W // LANES)]

        def put(out_ref, t, value):
            out_ref[0, rows, t * LANES:(t + 1) * LANES] = value.astype(out_ref.dtype)

        def roped(tiles, scale, out_ref):
            for t, a in enumerate(tiles):
                r = _rope_tile(a, *tabs)
                put(out_ref, t, r if scale == 1.0 else r * scale)

        def per_group(tile, fill, out_ref):
            put(out_ref, 0, jnp.where(low, tile, fill))
            put(out_ref, 1, jnp.where(low, _swap_halves(tile), fill))

        def per_group_t(tile, out_ref):
            t = tile.T
            ones = jnp.ones((ONES_ROWS, t.shape[1]), out_ref.dtype)
            for g in range(NSA_GROUPS):
                out_ref[0, g, 0:NSA_DH, rows] = t[g * NSA_DH:(g + 1) * NSA_DH].astype(out_ref.dtype)
                out_ref[0, g, NSA_DH:NSA_DH + ONES_ROWS, rows] = ones

        def retention_qk():
            tiles = proj(_R_RQ)
            roped(tiles[0:2], 1.0, rq_ref)
            roped(tiles[2:4], RET_DK ** -0.5, rk_ref)

        def retention_v():
            for t, a in enumerate(proj(_R_RV)):
                put(rv_ref, t, a)

        def retention_gate():
            for t, a in enumerate(proj(_R_RG)):
                put(rg_ref, t, a * _sigmoid(a))

        def nsa_q():
            roped(proj(_R_NQ), Q_SCALE, nq_ref)

        def nsa_compress_selected():
            kc_t, vc_t, ks_t, vs_t = proj(_R_KC)
            kc_ref[0, rows, :] = kc_t
            vc_ref[0, rows, :] = vc_t
            per_group(_rope_tile(ks_t, *tabs), hot_ref[rows, :], ksx_ref)
            per_group_t(vs_t, vst_ref)

        def nsa_window_gates():
            kw_t, vw_t, gates_t, _ = proj(_R_KW)
            per_group(_rope_tile(kw_t, *tabs), 0.0, kwx_ref)
            per_group_t(vw_t, vwt_ref)
            gate_ref[0, :, rows] = _sigmoid(gates_t.T[0:GATE_ROWS])

        return [retention_qk, retention_v, retention_gate, nsa_q, nsa_compress_selected, nsa_window_gates]

    tm = x_ref.shape[1]
    rows_a, rows_b = slice(0, tm // 2), slice(tm // 2, tm)
    steps_a = product_steps(rows_a, modulated_norm(rows_a))
    steps_a[0]()
    steps_b = product_steps(rows_b, modulated_norm(rows_b))
    for step in steps_a[1:] + steps_b:
        step()


def _in_proj(x, ln_g, mod, w_in, tabs, hot):
    b, s, d = x.shape
    tm = TM_IN
    grid = (b, s // tm)
    tab_spec = pl.BlockSpec((tm, LANES), lambda bi, j: (j, 0))

    def out(n, dtype):
        return (jax.ShapeDtypeStruct((b, s, n), dtype), pl.BlockSpec((1, tm, n), lambda bi, j: (bi, j, 0)))

    def out_t(rows, dtype):
        return (jax.ShapeDtypeStruct((b, NSA_GROUPS, rows, s), dtype),
                pl.BlockSpec((1, NSA_GROUPS, rows, tm), lambda bi, j: (bi, 0, 0, j)))

    vt_rows = NSA_DH + ONES_ROWS
    outs = [out(256, BF16), out(256, BF16), out(512, BF16), out(512, BF16), out(512, BF16),
            out(LANES, F32), out(LANES, F32),
            out(256, BF16), out_t(vt_rows, BF16), out(256, BF16), out_t(vt_rows, BF16),
            (jax.ShapeDtypeStruct((b, GATE_ROWS, s), F32),
             pl.BlockSpec((1, GATE_ROWS, tm), lambda bi, j: (bi, 0, j)))]
    return pl.pallas_call(
        _inproj_kernel,
        grid=grid,
        in_specs=[pl.BlockSpec((1, tm, d), lambda bi, j: (bi, j, 0)),
                  pl.BlockSpec((1, d), lambda bi, j: (0, 0)),
                  pl.BlockSpec((1, 6, d), lambda bi, j: (bi, 0, 0)),
                  pl.BlockSpec((1,) + w_in.shape[1:], lambda bi, j: (0, 0, 0), pipeline_mode=pl.Buffered(1)),
                  tab_spec, tab_spec, tab_spec, tab_spec],
        out_specs=[o[1] for o in outs],
        out_shape=[o[0] for o in outs],
        scratch_shapes=[pltpu.VMEM((d, IN_COLS_K), BF16)],
        compiler_params=pltpu.CompilerParams(
            dimension_semantics=("arbitrary", "arbitrary"), vmem_limit_bytes=VMEM_LIMIT),
        name="in_proj",
    )(x, ln_g.reshape(1, d), mod, w_in, *tabs, hot)


def _ret_kernel(q_ref, k_ref, v_ref, rg_ref, dec_ref, zeta_ref, xi_ref, gch_ref, o_ref, kv_ref, prev_ref):
    c = RET_CHUNK
    n_chunks = q_ref.shape[1] // c
    low = lax.broadcasted_iota(jnp.int32, (c, LANES), 1) < RET_DK

    def chunk_rows(n):
        return pl.ds(pl.multiple_of(n * c, c), c)

    def head_cols(h):
        return slice(h * RET_DV, (h + 1) * RET_DV)

    def kv_body(it, carry):
        kz_t = {}
        for j in range(RET_GROUP):
            rows = chunk_rows(it * RET_GROUP + j)
            for p in range(RET_HEADS // 2):
                pair = slice(p * LANES, (p + 1) * LANES)
                kz_t[j, p] = (k_ref[0, rows, pair].astype(F32) * zeta_ref[p]).T.astype(BF16)
        for j in range(RET_GROUP):
            n = it * RET_GROUP + j
            for h in range(RET_HEADS):
                kv_ref[h, n] = _nn(kz_t[j, h // 2], v_ref[0, chunk_rows(n), head_cols(h)])
        return carry

    lax.fori_loop(0, n_chunks // RET_GROUP, kv_body, 0)

    for h in range(RET_HEADS):
        def scan_body(n, st, h=h):
            prev_ref[h, n] = st.astype(prev_ref.dtype)
            return st * gch_ref[h] + kv_ref[h, n]
        lax.fori_loop(0, n_chunks, scan_body, jnp.zeros((LANES, RET_DV), F32))

    def out_body(it, carry):
        chains = [(j, h) for j in range(RET_GROUP) for h in range(RET_HEADS)]
        chunk = lambda j: it * RET_GROUP + j
        q_own, qx_own, att, ys = {}, {}, {}, {}
        for j, h in chains:
            p, e = divmod(h, 2)
            pair = slice(p * LANES, (p + 1) * LANES)
            mine = low if e == 0 else jnp.logical_not(low)
            q2 = q_ref[0, chunk_rows(chunk(j)), pair].astype(F32)
            q_own[j, h] = jnp.where(mine, q2, 0.0).astype(BF16)
            qx_own[j, h] = jnp.where(mine, q2 * xi_ref[p], 0.0).astype(BF16)
        for j, h in chains:
            pair = slice((h // 2) * LANES, (h // 2 + 1) * LANES)
            att[j, h] = _nt(q_own[j, h], k_ref[0, chunk_rows(chunk(j)), pair])
        for j, h in chains:
            lhs = jnp.concatenate([(att[j, h] * dec_ref[h]).astype(BF16), qx_own[j, h]], axis=1)
            rhs = jnp.concatenate([v_ref[0, chunk_rows(chunk(j)), head_cols(h)], prev_ref[h, chunk(j)]], axis=0)
            ys[j, h] = _nn(lhs, rhs)
        for j, h in chains:
            y = ys[j, h]
            yn = y * lax.rsqrt(jnp.mean(y * y, axis=-1, keepdims=True) + EPS)
            gate = rg_ref[0, chunk_rows(chunk(j)), head_cols(h)].astype(F32)
            o_ref[0, chunk_rows(chunk(j)), head_cols(h)] = (yn * gate).astype(o_ref.dtype)
        return carry

    lax.fori_loop(0, n_chunks // RET_GROUP, out_body, 0)


def _retention(rq, rk, rv, rg_act, tables):
    b, s, _ = rq.shape
    decay, zeta_p, xi_p, g_b = tables
    whole = lambda a: pl.BlockSpec(a.shape, lambda bi: (0,) * a.ndim)
    row = lambda a: pl.BlockSpec((1,) + a.shape[1:], lambda bi: (bi, 0, 0))
    return pl.pallas_call(
        _ret_kernel,
        grid=(b,),
        in_specs=[row(rq), row(rk), row(rv), row(rg_act),
                  whole(decay), whole(zeta_p), whole(xi_p), whole(g_b)],
        out_specs=row(rv),
        out_shape=jax.ShapeDtypeStruct(rv.shape, BF16),
        scratch_shapes=[pltpu.VMEM((RET_HEADS, s // RET_CHUNK, LANES, RET_DV), F32),
                        pltpu.VMEM((RET_HEADS, s // RET_CHUNK, LANES, RET_DV), BF16)],
        compiler_params=pltpu.CompilerParams(
            dimension_semantics=("parallel",), vmem_limit_bytes=VMEM_LIMIT),
        name="retention",
    )(rq, rk, rv, rg_act, decay, zeta_p, xi_p, g_b)


def _cmp_kernel(kc_ref, vc_ref, w1_ref, pe_ref, w2_ref, c_ref, s1_ref, s2_ref, kv_ref, vt_ref):
    n_piece = kc_ref.shape[1] // CMP_STRIDE
    halves = CMP_LEN // CMP_STRIDE
    lhs = {}
    for t, src in enumerate((kc_ref, vc_ref)):
        rows = [src[0, pl.ds(r, n_piece, stride=CMP_STRIDE), :] for r in range(CMP_STRIDE)]
        for half in range(halves):
            pe0 = half * CMP_STRIDE
            lhs[t, half] = jnp.concatenate(
                [(rows[r] + pe_ref[t, pe0 + r:pe0 + r + 1, :]).astype(BF16) for r in range(CMP_STRIDE)], axis=1)
    part = {key: _nn(lhs[key], w1_ref[key[0], key[1]]) for key in lhs}
    out = jnp.zeros((n_piece, kv_ref.shape[2]), F32)
    for t in range(2):
        hid = part[t, 0]
        for half in range(1, halves):
            hid = hid + pltpu.roll(part[t, half], n_piece - half, 0)
        out = out + _nn((hid * _sigmoid(hid)).astype(BF16), w2_ref[t])
    for t in range(out.shape[1] // LANES):
        sl = slice(t * LANES, (t + 1) * LANES)
        kv = _rope_tile(out[:, sl], c_ref[...], s1_ref[...], s2_ref[...])
        kv_ref[0, :, sl] = kv.astype(kv_ref.dtype)
        vt_ref[0, t] = kv.T[NSA_DH:2 * NSA_DH].astype(vt_ref.dtype)


def _compress_weights(w1_k, w2_k, w1_v, w2_v, pe_k, pe_v):
    dh, hid = NSA_DH, CMP_HIDDEN

    def first(w1):
        w = w1.reshape(CMP_LEN, dh, hid)
        z = jnp.zeros_like(w)
        return jnp.concatenate([jnp.concatenate([w, z], axis=2), jnp.concatenate([z, w], axis=2)], axis=1)

    def second(w2, off):
        z = jnp.zeros_like(w2)
        rows = []
        for g in range(NSA_GROUPS):
            c = [z, z, z, z]
            c[2 * g + off] = w2
            rows.append(jnp.concatenate(c, axis=1))
        return jnp.concatenate(rows, axis=0)

    w1 = jnp.stack([first(w1_k), first(w1_v)]).astype(BF16).reshape(
        2, CMP_LEN // CMP_STRIDE, CMP_STRIDE * NSA_GROUPS * dh, NSA_GROUPS * hid)
    w2 = jnp.stack([second(w2_k, 0), second(w2_v, 1)]).astype(BF16)
    pe = jnp.stack([jnp.concatenate([pe_k, pe_k], axis=1), jnp.concatenate([pe_v, pe_v], axis=1)])
    return w1, w2, pe


def _compress(kc, vc, w1, w2, pe, tabs_cmp):
    b, s, w = kc.shape
    n_piece = s // CMP_STRIDE
    const2 = lambda bi: (0, 0)
    src = pl.BlockSpec((1, s, w), lambda bi: (bi, 0, 0))
    dst = pl.BlockSpec((1, n_piece, 2 * w), lambda bi: (bi, 0, 0))
    shape = jax.ShapeDtypeStruct((b, n_piece, 2 * w), BF16)
    return pl.pallas_call(
        _cmp_kernel,
        grid=(b,),
        in_specs=[src, src,
                  pl.BlockSpec(w1.shape, lambda bi: (0, 0, 0, 0)),
                  pl.BlockSpec(pe.shape, lambda bi: (0, 0, 0)),
                  pl.BlockSpec(w2.shape, lambda bi: (0, 0, 0)),
                  pl.BlockSpec((n_piece, LANES), const2),
                  pl.BlockSpec((n_piece, LANES), const2),
                  pl.BlockSpec((n_piece, LANES), const2)],
        out_specs=[dst, pl.BlockSpec((1, NSA_GROUPS, NSA_DH, n_piece), lambda bi: (bi, 0, 0, 0))],
        out_shape=[shape, jax.ShapeDtypeStruct((b, NSA_GROUPS, NSA_DH, n_piece), BF16)],
        compiler_params=pltpu.CompilerParams(
            dimension_semantics=("parallel",), vmem_limit_bytes=VMEM_LIMIT),
        name="compress",
    )(kc, vc, w1, pe, w2, *tabs_cmp)


def _nsa_kernel(q_ref, kcmp_ref, vct_ref, ksx_ref, vst_ref, kwx_ref, vwt_ref, gate_ref, ovt_ref, o_ref):
    tq = TQ
    seq = q_ref.shape[1]
    nb = seq // SLC_LEN
    n_cmp = kcmp_ref.shape[1]
    hpg, dh = NSA_HPG, NSA_DH
    assert WIN % tq == 0 and seq % tq == 0
    group = pl.program_id(1)

    low = lax.broadcasted_iota(jnp.int32, (tq, LANES), 1) < dh
    eye = jnp.where(lax.broadcasted_iota(jnp.int32, (tq, tq), 0)
                    == lax.broadcasted_iota(jnp.int32, (tq, tq), 1), 1.0, 0.0).astype(BF16)
    blk = lax.broadcasted_iota(jnp.int32, (nb, tq), 0)
    col = lax.broadcasted_iota(jnp.int32, (nb, tq), 1)
    crow = lax.broadcasted_iota(jnp.int32, (n_cmp, tq), 0)
    ccol = lax.broadcasted_iota(jnp.int32, (n_cmp, tq), 1)
    kcm = kcmp_ref[0]
    vct = vct_ref[0, 0]
    ovt = ovt_ref[...]

    def per_head(x):
        return jnp.concatenate([x] * hpg, axis=1)

    key_off = lax.broadcasted_iota(jnp.int32, (tq, tq), 0)
    qry_off = lax.broadcasted_iota(jnp.int32, (tq, tq), 1)
    not_after = per_head(jnp.where(key_off <= qry_off, 0.0, NEG))
    inside_win = per_head(jnp.where(key_off > qry_off, 0.0, NEG))

    def masked(s, first_key, t0, windowed):
        blocks = []
        for r in range(0, s.shape[0], tq):
            blk_s = s[r:r + tq]
            if first_key + r == t0:
                blk_s = blk_s + not_after
            elif windowed and first_key + r == t0 - WIN:
                blk_s = blk_s + inside_win
            blocks.append(blk_s)
        return jnp.concatenate(blocks, axis=0)

    def normalise(acc):
        return acc[0:dh] / acc[dh:dh + 1]

    def select_blocks(psum, t0):
        tcol = col + t0
        bcausal = blk * SLC_LEN <= tcol
        n_live = (t0 + tq - 1) // SLC_LEN + 1
        top_n = min(SLC_TOPK, nb)
        if n_live <= top_n:
            bias = jnp.where(bcausal, 0.0, NEG)
        else:
            p_hi = psum.astype(BF16)
            p_lo = (psum - p_hi.astype(F32)).astype(BF16)
            imp = _nn(ovt, p_hi) + _nn(ovt, p_lo)
            cur = tcol // SLC_LEN
            forced = (blk == 0) | (blk == cur) | (blk == cur - 1)
            imp = jnp.where(bcausal, jnp.where(forced, FORCE, imp), NEG)
            rank = jnp.zeros((nb, tq), F32)
            for j in range(n_live):
                r = imp[j:j + 1, :]
                rank = rank + jnp.where(blk > j, jnp.where(r >= imp, 1.0, 0.0), jnp.where(r > imp, 1.0, 0.0))
            bias = jnp.where((rank < float(top_n)) & bcausal, 0.0, NEG)
        feat = jnp.concatenate([jnp.zeros((dh, tq), F32), bias,
                                jnp.zeros((LANES - dh - nb, tq), F32)], axis=0).astype(BF16)
        return _nt(eye, feat)

    class TileGroup:
        def __init__(self, tiles):
            self.tiles = tiles
            self.wstart = {t0: max(t0 - WIN, 0) for t0 in tiles}
            self.wkeys = {t0: slice(self.wstart[t0], t0 + tq) for t0 in tiles}
            self.skeys = {t0: slice(0, t0 + tq) for t0 in tiles}

        def scores_window_compressed(self):
            self.heads, self.qs = {}, {}
            for t0 in self.tiles:
                qf = q_ref[0, t0:t0 + tq, :].astype(F32)
                hl = []
                for hh in range(hpg):
                    t = qf[:, (hh // 2) * LANES:(hh // 2 + 1) * LANES]
                    if hh % 2 == 1:
                        t = _swap_halves(t)
                    hl.append(jnp.where(low, t, 0.0))
                self.heads[t0] = hl
                self.qs[t0] = jnp.concatenate(hl, axis=0).astype(BF16)
            self.s_w = {t0: _nt(kwx_ref[0, self.wkeys[t0], :], self.qs[t0]) for t0 in self.tiles}
            self.s_c = {t0: _nt(kcm, self.qs[t0]) for t0 in self.tiles}

        def select_and_scores_selected(self):
            self.p_cmp, qsel = {}, {}
            for t0 in self.tiles:
                cmask = (crow * CMP_STRIDE + (CMP_LEN - 1)) <= (ccol + t0)
                p_all = []
                psum = jnp.zeros((n_cmp, tq), F32)
                for hh in range(hpg):
                    sh = jnp.where(cmask, self.s_c[t0][:, hh * tq:(hh + 1) * tq], NEG)
                    e = jnp.exp2(sh - jnp.max(sh, axis=0, keepdims=True))
                    p = jnp.where(cmask, e / jnp.sum(e, axis=0, keepdims=True), 0.0)
                    psum = psum + p
                    p_all.append(p.astype(BF16))
                self.p_cmp[t0] = jnp.concatenate(p_all, axis=1)
                qbias = select_blocks(psum, t0)
                qsel[t0] = jnp.concatenate([hd + qbias for hd in self.heads[t0]], axis=0).astype(BF16)
            self.s_s = {t0: _nt(ksx_ref[0, self.skeys[t0], :], qsel[t0]) for t0 in self.tiles}

        def outputs_window_compressed(self):
            e_w = {}
            for t0 in self.tiles:
                sw = masked(self.s_w[t0], self.wstart[t0], t0, True)
                e_w[t0] = jnp.exp2(sw - jnp.max(sw, axis=0, keepdims=True)).astype(BF16)
            self.o_win = {t0: normalise(_nn(vwt_ref[0, 0, :, self.wkeys[t0]], e_w[t0])) for t0 in self.tiles}
            self.o_cmp = {t0: _nn(vct, self.p_cmp[t0]) for t0 in self.tiles}

        def outputs_selected_and_store(self):
            e_s = {}
            for t0 in self.tiles:
                ss = masked(self.s_s[t0], 0, t0, False)
                e_s[t0] = jnp.exp2(ss - jnp.max(ss, axis=0, keepdims=True)).astype(BF16)
            o_sel = {t0: normalise(_nn(vst_ref[0, 0, :, self.skeys[t0]], e_s[t0])) for t0 in self.tiles}
            for t0 in self.tiles:
                gt = gate_ref[0, :, t0:t0 + tq]
                outs = []
                for hh in range(hpg):
                    hc = slice(hh * tq, (hh + 1) * tq)

                    def gate_row(branch):
                        by_group = [gt[branch * NSA_HEADS + g * hpg + hh:branch * NSA_HEADS + g * hpg + hh + 1, :]
                                    for g in range(NSA_GROUPS)]
                        row = by_group[-1]
                        for g in range(NSA_GROUPS - 2, -1, -1):
                            row = jnp.where(group == g, by_group[g], row)
                        return row

                    outs.append(gate_row(0) * self.o_cmp[t0][:, hc] + gate_row(1) * o_sel[t0][:, hc]
                                + gate_row(2) * self.o_win[t0][:, hc])
                o_ref[0, t0:t0 + tq, :] = jnp.concatenate(outs, axis=0).T.astype(o_ref.dtype)

    starts = list(range(0, seq, tq))
    groups = [TileGroup(starts[i:i + TILE_GROUP]) for i in range(0, len(starts), TILE_GROUP)]
    n = len(groups)
    groups[0].scores_window_compressed()
    for i, g in enumerate(groups):
        if i + 1 < n:
            groups[i + 1].scores_window_compressed()
        g.select_and_scores_selected()
        g.outputs_window_compressed()
        if i > 0:
            groups[i - 1].outputs_selected_and_store()
    groups[n - 1].outputs_selected_and_store()


def _nsa_attention(nq, kvcmp, vct, ksx, vst, kwx, vwt, gate_t, ovt):
    b, s, _ = nq.shape
    n_cmp = kvcmp.shape[1]
    gw = NSA_HPG * NSA_DH
    per_group = lambda rows, width: pl.BlockSpec((1, rows, width), lambda bi, g: (bi, 0, g))
    per_group_t = lambda a: pl.BlockSpec((1, 1) + a.shape[2:], lambda bi, g: (bi, g, 0, 0))
    return pl.pallas_call(
        _nsa_kernel,
        grid=(b, NSA_GROUPS),
        in_specs=[per_group(s, gw),
                  per_group(n_cmp, LANES), per_group_t(vct),
                  per_group(s, LANES), per_group_t(vst), per_group(s, LANES), per_group_t(vwt),
                  pl.BlockSpec((1,) + gate_t.shape[1:], lambda bi, g: (bi, 0, 0)),
                  pl.BlockSpec(ovt.shape, lambda bi, g: (0, 0))],
        out_specs=per_group(s, gw),
        out_shape=jax.ShapeDtypeStruct((b, s, NSA_HEADS * NSA_DH), BF16),
        compiler_params=pltpu.CompilerParams(
            dimension_semantics=("parallel", "parallel"), vmem_limit_bytes=VMEM_LIMIT),
        name="nsa_attn",
    )(nq, kvcmp, vct, ksx, vst, kwx, vwt, gate_t, ovt)


def _ffn_kernel(x_ref, yr_ref, yn_ref, mod_ref, g2_ref, gf_ref, wo_hbm, wg_hbm, wu_hbm, wd_hbm,
                o_ref, x1_ref, act_ref, wo_ref, wg_ref, wu_ref, wd_ref, stage_ref, stage_sem):
    @pl.when(pl.program_id(0) == 0)
    def _():
        jobs = [(src, dst, r0, min(CAST_ROWS, dst.shape[0] - r0))
                for src, dst in ((wo_hbm, wo_ref), (wg_hbm, wg_ref), (wu_hbm, wu_ref), (wd_hbm, wd_ref))
                for r0 in range(0, dst.shape[0], CAST_ROWS)]

        def copy_of(k):
            src, dst, r0, rows = jobs[k]
            return pltpu.make_async_copy(src.at[0, pl.ds(r0, rows), :],
                                         stage_ref.at[k % 2, pl.ds(0, rows), pl.ds(0, dst.shape[1])],
                                         stage_sem.at[k % 2])

        def consume(k):
            _, dst, r0, rows = jobs[k]
            dst[r0:r0 + rows, :] = stage_ref[k % 2, 0:rows, 0:dst.shape[1]].astype(BF16)

        _stream_staged(len(jobs), copy_of, consume)

    half_w = yr_ref.shape[1]
    d_ff = wg_ref.shape[1]
    n_chunks = d_ff // TF
    rows_a = slice(0, x_ref.shape[0] // 2)
    rows_b = slice(x_ref.shape[0] // 2, x_ref.shape[0])

    def mix(rows):
        return _nn(yr_ref[rows, :], wo_ref[0:half_w, :]) + _nn(yn_ref[rows, :], wo_ref[half_w:2 * half_w, :])

    def mid_norm(rows, mixed):
        x1 = x_ref[rows, :] + mod_ref[0, 2:3, :] * mixed
        x1_ref[rows, :] = x1
        y = x1 * lax.rsqrt(jnp.mean(x1 * x1, axis=-1, keepdims=True) + EPS) * g2_ref[...]
        return (y * (1.0 + mod_ref[0, 4:5, :]) + mod_ref[0, 3:4, :]).astype(BF16)

    def ff_chunk(rows, h2, j):
        sl = slice(j * TF, (j + 1) * TF)
        gate = _nn(h2, wg_ref[:, sl])
        up = _nn(h2, wu_ref[:, sl])
        act_ref[rows, sl] = (gate * _sigmoid(gate) * up).astype(BF16)

    def down(rows):
        return x1_ref[rows, :] + mod_ref[0, 5:6, :] * _nn(act_ref[rows, :], wd_ref[...])

    def final_norm(rows, xo):
        o_ref[rows, :] = xo * lax.rsqrt(jnp.mean(xo * xo, axis=-1, keepdims=True) + EPS) * gf_ref[...]

    mix_a = mix(rows_a)
    mix_b = mix(rows_b)
    h2_a = mid_norm(rows_a, mix_a)
    ff_chunk(rows_a, h2_a, 0)
    h2_b = mid_norm(rows_b, mix_b)
    for j in range(1, n_chunks):
        ff_chunk(rows_a, h2_a, j)
    xo_a = down(rows_a)
    ff_chunk(rows_b, h2_b, 0)
    final_norm(rows_a, xo_a)
    for j in range(1, n_chunks):
        ff_chunk(rows_b, h2_b, j)
    final_norm(rows_b, down(rows_b))


def _out_ffn(x2d, y_ret, y_nsa, mod, w_out, g2, gf, wg, wu, wd, seq):
    n, d = x2d.shape
    tm = TM_FF
    d_ff = wg.shape[2]
    tiles_per_seq = seq // tm
    half_w = y_ret.shape[1]
    row = lambda i: (i, 0)
    in_hbm = pl.BlockSpec(memory_space=pl.ANY)
    weights = (w_out, wg, wu, wd)
    stage_cols = max(w.shape[2] for w in weights)
    return pl.pallas_call(
        _ffn_kernel,
        grid=(n // tm,),
        in_specs=[pl.BlockSpec((tm, d), row),
                  pl.BlockSpec((tm, half_w), row),
                  pl.BlockSpec((tm, half_w), row),
                  pl.BlockSpec((1, 6, d), lambda i: (i // tiles_per_seq, 0, 0)),
                  pl.BlockSpec((1, d), lambda i: (0, 0)),
                  pl.BlockSpec((1, d), lambda i: (0, 0)),
                  in_hbm, in_hbm, in_hbm, in_hbm],
        out_specs=pl.BlockSpec((tm, d), row),
        out_shape=jax.ShapeDtypeStruct((n, d), F32),
        scratch_shapes=[pltpu.VMEM((tm, d), F32), pltpu.VMEM((tm, d_ff), BF16)]
                       + [pltpu.VMEM(w.shape[1:], BF16) for w in weights]
                       + [pltpu.VMEM((2, CAST_ROWS, stage_cols), F32), pltpu.SemaphoreType.DMA((2,))],
        compiler_params=pltpu.CompilerParams(
            dimension_semantics=("arbitrary",), vmem_limit_bytes=VMEM_LIMIT),
        name="out_ffn",
    )(x2d, y_ret, y_nsa, mod, g2.reshape(1, d), gf.reshape(1, d), *weights)


def kernel(x, c, ln_mix_g, ln_ffn_g, w_ada, b_ada, w_in, cmp_pe_k, cmp_w1_k, cmp_w2_k,
           cmp_pe_v, cmp_w1_v, cmp_w2_v, w_out, w_ff_gate, w_ff_up, w_ff_down, ln_final_g):
    assert w_in.shape[0] == 1, "the final RMSNorm is fused into the (single) layer's FFN kernel"
    b, s, d = x.shape
    lane = np.arange(LANES)
    tabs = _rope_tables(np.arange(s), np.ones(LANES, bool))
    n_piece = s // CMP_STRIDE
    tabs_cmp = _rope_tables(np.arange(n_piece) * CMP_STRIDE + CMP_LEN - 1, lane < NSA_DH)

    mod = _ada(c, w_ada[0], b_ada[0]).reshape(b, 6, d)
    rq, rk, rv, rg_act, nq, kc, vc, ksx, vst, kwx, vwt, gate_t = _in_proj(
        x, ln_mix_g[0], mod, w_in, tabs, _block_onehot_table(s))
    y_ret = _retention(rq, rk, rv, rg_act, _retention_tables())
    w1, w2, pe = _compress_weights(cmp_w1_k[0], cmp_w2_k[0], cmp_w1_v[0], cmp_w2_v[0],
                                   cmp_pe_k[0], cmp_pe_v[0])
    kvcmp, vct = _compress(kc, vc, w1, w2, pe, tabs_cmp)
    y_nsa = _nsa_attention(nq, kvcmp, vct, ksx, vst, kwx, vwt, gate_t, _overlap_t(s))
    out = _out_ffn(x.reshape(b * s, d), y_ret.reshape(b * s, -1), y_nsa.reshape(b * s, -1), mod,
                   w_out, ln_ffn_g[0], ln_final_g, w_ff_gate, w_ff_up, w_ff_down, s)
    return out.reshape(b, s, d)
```

```python
import numpy as np
import jax
import jax.numpy as jnp
from jax import lax
from jax.experimental import pallas as pl
from jax.experimental.pallas import tpu as pltpu

F32 = jnp.float32
BF16 = jnp.bfloat16

D_MODEL = 1024
RET_HEADS = 4
RET_DK = 64
RET_DV = 128
RET_CHUNK = 128
NSA_HEADS = 8
NSA_GROUPS = 2
NSA_HPG = NSA_HEADS // NSA_GROUPS
NSA_DH = 64
CMP_LEN = 32
CMP_STRIDE = 16
CMP_HIDDEN = 128
SLC_LEN = 64
SLC_TOPK = 16
WIN = 512
D_FF = ((8 * D_MODEL + 3 * 256 - 1) // (3 * 256)) * 256
ROPE_THETA = 10000.0
EPS = 1e-6
NEG = -1e30
FORCE = 1e6

LANES = 128
SUBLANES = 8
HALF = NSA_DH // 2
ONES_ROWS = 16
GATE_ROWS = 3 * NSA_HEADS

TN_ADA = 2048
TM_IN = 1024
PROJ_W = 512
RET_GROUP = 8
Q_SCALE = NSA_DH ** -0.5 * float(np.log2(np.e))
TQ = 128
TILE_GROUP = 2
TM_FF = 512
TF = 256
CAST_ROWS = 512
VMEM_LIMIT = 56 * 1024 * 1024

_R_RQ, _R_RK, _R_RV, _R_RG, _R_NQ = 0, 256, 512, 1024, 1536
_R_KC, _R_VC, _R_KS, _R_VS, _R_KW, _R_VW, _R_GATE = 2048, 2176, 2304, 2432, 2560, 2688, 2816


def _sigmoid(x):
    return 1.0 / (1.0 + jnp.exp(-x))


def _nt(a, b):
    return lax.dot_general(a, b, (((1,), (1,)), ((), ())), preferred_element_type=F32)


def _nn(a, b):
    return jnp.dot(a, b, preferred_element_type=F32)


def _rope_tile(a, c, s1, s2):
    return a * c + pltpu.roll(a, HALF, 1) * s1 + pltpu.roll(a, LANES - HALF, 1) * s2


def _swap_halves(a):
    return pltpu.roll(a, LANES // 2, 1)


def _rope_tables(pos, rotary_lanes):
    pos = np.asarray(pos, np.float64)
    lane = np.arange(LANES)
    within = lane % NSA_DH
    freq = ROPE_THETA ** (-(within % HALF).astype(np.float64) / HALF)
    ang = pos[:, None] * freq[None, :]
    cos, sin = np.cos(ang), np.sin(ang)
    first = (within < HALF)[None, :]
    rot = np.asarray(rotary_lanes, bool)[None, :]
    c = np.where(rot, cos, 1.0)
    s1 = np.where(rot & ~first, sin, 0.0)
    s2 = np.where(rot & first, -sin, 0.0)
    return (jnp.asarray(c, F32), jnp.asarray(s1, F32), jnp.asarray(s2, F32))


def _block_onehot_table(seq):
    t = np.zeros((seq, LANES), np.float32)
    pos = np.arange(seq)
    t[pos, NSA_DH + pos // SLC_LEN] = 1.0
    return jnp.asarray(t)


def _retention_tables():
    h = jnp.arange(RET_HEADS, dtype=F32)
    log_g = jnp.log(1.0 - 2.0 ** (-5.0 - h))
    c = RET_CHUNK
    idx = jnp.arange(c, dtype=F32)
    diff = idx[:, None] - idx[None, :]
    causal = diff >= 0
    decay = jnp.where(causal, jnp.exp(log_g[:, None, None] * jnp.where(causal, diff, 0.0)), 0.0)
    zeta = jnp.exp(log_g[:, None] * (c - 1.0 - idx))
    xi = jnp.exp(log_g[:, None] * (idx + 1.0))
    g_chunk = jnp.exp(log_g * c)

    def pair_lanes(t):
        t = t.reshape(RET_HEADS // 2, 2, c)
        return jnp.repeat(jnp.transpose(t, (0, 2, 1)), RET_DK, axis=2)

    g_b = jnp.broadcast_to(g_chunk[:, None, None], (RET_HEADS, 1, LANES))
    return decay, pair_lanes(zeta), pair_lanes(xi), g_b


def _overlap_t(seq):
    n_c = seq // CMP_STRIDE - CMP_LEN // CMP_STRIDE + 1
    nb = seq // SLC_LEN
    cs = np.arange(n_c) * CMP_STRIDE
    bs = np.arange(nb) * SLC_LEN
    ov = np.maximum(np.minimum(cs[:, None] + CMP_LEN, bs[None] + SLC_LEN)
                    - np.maximum(cs[:, None], bs[None]), 0).astype(np.float64) / CMP_LEN
    ncp = seq // CMP_STRIDE
    ovp = np.zeros((ncp, nb))
    ovp[:n_c] = ov
    return jnp.asarray(ovp.T, BF16)


def _ada_kernel(c_ref, w_ref, b_ref, o_ref):
    c = c_ref[...]
    o_ref[...] = _nn(c * _sigmoid(c), w_ref[...]) + b_ref[...]


def _ada(c, w, b):
    bsz, d = c.shape
    n = w.shape[1]
    tn = TN_ADA
    return pl.pallas_call(
        _ada_kernel,
        grid=(n // tn,),
        in_specs=[pl.BlockSpec((bsz, d), lambda j: (0, 0)),
                  pl.BlockSpec((d, tn), lambda j: (0, j)),
                  pl.BlockSpec((1, tn), lambda j: (0, j))],
        out_specs=pl.BlockSpec((bsz, tn), lambda j: (0, j)),
        out_shape=jax.ShapeDtypeStruct((bsz, n), F32),
        compiler_params=pltpu.CompilerParams(vmem_limit_bytes=VMEM_LIMIT),
        name="ada",
    )(c, w, b.reshape(1, n))


def _stream_staged(n_jobs, copy_of, consume):
    copy_of(0).start()
    for k in range(n_jobs):
        if k + 1 < n_jobs:
            copy_of(k + 1).start()
        copy_of(k).wait()
        consume(k)


def _inproj_kernel(x_ref, g_ref, mod_ref, w_ref, c_ref, s1_ref, s2_ref, hot_ref,
                   rq_ref, rk_ref, rv_ref, rg_ref, nq_ref, kc_ref, vc_ref,
                   ksx_ref, vst_ref, kwx_ref, vwt_ref, gate_ref, wtail_ref):
    n_in = w_ref.shape[2]
    whole = (n_in // PROJ_W) * PROJ_W

    @pl.when((pl.program_id(0) == 0) & (pl.program_id(1) == 0))
    def _():
        wtail_ref[...] = jnp.zeros_like(wtail_ref)
        wtail_ref[:, 0:n_in - whole] = w_ref[0, :, whole:n_in]

    def modulated_norm(rows):
        x = x_ref[0, rows, :]
        y = x * lax.rsqrt(jnp.mean(x * x, axis=-1, keepdims=True) + EPS) * g_ref[...]
        return (y * (1.0 + mod_ref[0, 1:2, :]) + mod_ref[0, 0:1, :]).astype(BF16)

    def product_steps(rows, hb):
        tabs = (c_ref[rows, :], s1_ref[rows, :], s2_ref[rows, :])
        low = lax.broadcasted_iota(jnp.int32, (rows.stop - rows.start, LANES), 1) < NSA_DH

        def proj(c0):
            w = w_ref[0, :, c0:c0 + PROJ_W] if c0 < whole else wtail_ref[...]
            a = _nn(hb, w)
            return [a[:, t * LANES:(t + 1) * LANES] for t in range(PROJ_W // LANES)]

        def put(out_ref, t, value):
            out_ref[0, rows, t * LANES:(t + 1) * LANES] = value.astype(out_ref.dtype)

        def roped(tiles, scale, out_ref):
            for t, a in enumerate(tiles):
                r = _rope_tile(a, *tabs)
                put(out_ref, t, r if scale == 1.0 else r * scale)

        def per_group(tile, fill, out_ref):
            put(out_ref, 0, jnp.where(low, tile, fill))
            put(out_ref, 1, jnp.where(low, _swap_halves(tile), fill))

        def per_group_t(tile, out_ref):
            t = tile.T
            ones = jnp.ones((ONES_ROWS, t.shape[1]), out_ref.dtype)
            for g in range(NSA_GROUPS):
                out_ref[0, g, 0:NSA_DH, rows] = t[g * NSA_DH:(g + 1) * NSA_DH].astype(out_ref.dtype)
                out_ref[0, g, NSA_DH:NSA_DH + ONES_ROWS, rows] = ones

        def retention_qk():
            tiles = proj(_R_RQ)
            roped(tiles[0:2], 1.0, rq_ref)
            roped(tiles[2:4], RET_DK ** -0.5, rk_ref)

        def retention_v():
            for t, a in enumerate(proj(_R_RV)):
                put(rv_ref, t, a)

        def retention_gate():
            for t, a in enumerate(proj(_R_RG)):
                put(rg_ref, t, a * _sigmoid(a))

        def nsa_q():
            roped(proj(_R_NQ), Q_SCALE, nq_ref)

        def nsa_compress_selected():
            kc_t, vc_t, ks_t, vs_t = proj(_R_KC)
            kc_ref[0, rows, :] = kc_t
            vc_ref[0, rows, :] = vc_t
            per_group(_rope_tile(ks_t, *tabs), hot_ref[rows, :], ksx_ref)
            per_group_t(vs_t, vst_ref)

        def nsa_window_gates():
            kw_t, vw_t, gates_t, _ = proj(_R_KW)
            per_group(_rope_tile(kw_t, *tabs), 0.0, kwx_ref)
            per_group_t(vw_t, vwt_ref)
            gate_ref[0, :, rows] = _sigmoid(gates_t.T[0:GATE_ROWS])

        return [retention_qk, retention_v, retention_gate, nsa_q, nsa_compress_selected, nsa_window_gates]

    tm = x_ref.shape[1]
    rows_a, rows_b = slice(0, tm // 2), slice(tm // 2, tm)
    steps_a = product_steps(rows_a, modulated_norm(rows_a))
    steps_a[0]()
    steps_b = product_steps(rows_b, modulated_norm(rows_b))
    for step in steps_a[1:] + steps_b:
        step()


def _in_proj(x, ln_g, mod, w_in, tabs, hot):
    b, s, d = x.shape
    tm = TM_IN
    grid = (b, s // tm)
    tab_spec = pl.BlockSpec((tm, LANES), lambda bi, j: (j, 0))

    def out(n, dtype):
        return (jax.ShapeDtypeStruct((b, s, n), dtype), pl.BlockSpec((1, tm, n), lambda bi, j: (bi, j, 0)))

    def out_t(rows, dtype):
        return (jax.ShapeDtypeStruct((b, NSA_GROUPS, rows, s), dtype),
                pl.BlockSpec((1, NSA_GROUPS, rows, tm), lambda bi, j: (bi, 0, 0, j)))

    vt_rows = NSA_DH + ONES_ROWS
    outs = [out(256, BF16), out(256, BF16), out(512, BF16), out(512, BF16), out(512, BF16),
            out(LANES, F32), out(LANES, F32),
            out(256, BF16), out_t(vt_rows, BF16), out(256, BF16), out_t(vt_rows, BF16),
            (jax.ShapeDtypeStruct((b, GATE_ROWS, s), F32),
             pl.BlockSpec((1, GATE_ROWS, tm), lambda bi, j: (bi, 0, j)))]
    return pl.pallas_call(
        _inproj_kernel,
        grid=grid,
        in_specs=[pl.BlockSpec((1, tm, d), lambda bi, j: (bi, j, 0)),
                  pl.BlockSpec((1, d), lambda bi, j: (0, 0)),
                  pl.BlockSpec((1, 6, d), lambda bi, j: (bi, 0, 0)),
                  pl.BlockSpec((1,) + w_in.shape[1:], lambda bi, j: (0, 0, 0), pipeline_mode=pl.Buffered(1)),
                  tab_spec, tab_spec, tab_spec, tab_spec],
        out_specs=[o[1] for o in outs],
        out_shape=[o[0] for o in outs],
        scratch_shapes=[pltpu.VMEM((d, PROJ_W), BF16)],
        compiler_params=pltpu.CompilerParams(
            dimension_semantics=("arbitrary", "arbitrary"), vmem_limit_bytes=VMEM_LIMIT),
        name="in_proj",
    )(x, ln_g.reshape(1, d), mod, w_in, *tabs, hot)


def _ret_kernel(q_ref, k_ref, v_ref, rg_ref, dec_ref, zeta_ref, xi_ref, gch_ref, o_ref, kv_ref, prev_ref):
    c = RET_CHUNK
    n_chunks = q_ref.shape[1] // c
    low = lax.broadcasted_iota(jnp.int32, (c, LANES), 1) < RET_DK

    def chunk_rows(n):
        return pl.ds(pl.multiple_of(n * c, c), c)

    def head_cols(h):
        return slice(h * RET_DV, (h + 1) * RET_DV)

    def kv_body(it, carry):
        kz_t = {}
        for j in range(RET_GROUP):
            rows = chunk_rows(it * RET_GROUP + j)
            for p in range(RET_HEADS // 2):
                pair = slice(p * LANES, (p + 1) * LANES)
                kz_t[j, p] = (k_ref[0, rows, pair].astype(F32) * zeta_ref[p]).T.astype(BF16)
        for j in range(RET_GROUP):
            n = it * RET_GROUP + j
            for h in range(RET_HEADS):
                kv_ref[h, n] = _nn(kz_t[j, h // 2], v_ref[0, chunk_rows(n), head_cols(h)])
        return carry

    lax.fori_loop(0, n_chunks // RET_GROUP, kv_body, 0)

    for h in range(RET_HEADS):
        def scan_body(n, st, h=h):
            prev_ref[h, n] = st.astype(prev_ref.dtype)
            return st * gch_ref[h] + kv_ref[h, n]
        lax.fori_loop(0, n_chunks, scan_body, jnp.zeros((LANES, RET_DV), F32))

    def out_body(it, carry):
        chains = [(j, h) for j in range(RET_GROUP) for h in range(RET_HEADS)]
        chunk = lambda j: it * RET_GROUP + j
        q_own, qx_own, att, ys = {}, {}, {}, {}
        for j, h in chains:
            p, e = divmod(h, 2)
            pair = slice(p * LANES, (p + 1) * LANES)
            mine = low if e == 0 else jnp.logical_not(low)
            q2 = q_ref[0, chunk_rows(chunk(j)), pair].astype(F32)
            q_own[j, h] = jnp.where(mine, q2, 0.0).astype(BF16)
            qx_own[j, h] = jnp.where(mine, q2 * xi_ref[p], 0.0).astype(BF16)
        for j, h in chains:
            pair = slice((h // 2) * LANES, (h // 2 + 1) * LANES)
            att[j, h] = _nt(q_own[j, h], k_ref[0, chunk_rows(chunk(j)), pair])
        for j, h in chains:
            lhs = jnp.concatenate([(att[j, h] * dec_ref[h]).astype(BF16), qx_own[j, h]], axis=1)
            rhs = jnp.concatenate([v_ref[0, chunk_rows(chunk(j)), head_cols(h)], prev_ref[h, chunk(j)]], axis=0)
            ys[j, h] = _nn(lhs, rhs)
        for j, h in chains:
            y = ys[j, h]
            yn = y * lax.rsqrt(jnp.mean(y * y, axis=-1, keepdims=True) + EPS)
            gate = rg_ref[0, chunk_rows(chunk(j)), head_cols(h)].astype(F32)
            o_ref[0, chunk_rows(chunk(j)), head_cols(h)] = (yn * gate).astype(o_ref.dtype)
        return carry

    lax.fori_loop(0, n_chunks // RET_GROUP, out_body, 0)


def _retention(rq, rk, rv, rg_act, tables):
    b, s, _ = rq.shape
    decay, zeta_p, xi_p, g_b = tables
    whole = lambda a: pl.BlockSpec(a.shape, lambda bi: (0,) * a.ndim)
    row = lambda a: pl.BlockSpec((1,) + a.shape[1:], lambda bi: (bi, 0, 0))
    return pl.pallas_call(
        _ret_kernel,
        grid=(b,),
        in_specs=[row(rq), row(rk), row(rv), row(rg_act),
                  whole(decay), whole(zeta_p), whole(xi_p), whole(g_b)],
        out_specs=row(rv),
        out_shape=jax.ShapeDtypeStruct(rv.shape, BF16),
        scratch_shapes=[pltpu.VMEM((RET_HEADS, s // RET_CHUNK, LANES, RET_DV), F32),
                        pltpu.VMEM((RET_HEADS, s // RET_CHUNK, LANES, RET_DV), BF16)],
        compiler_params=pltpu.CompilerParams(
            dimension_semantics=("parallel",), vmem_limit_bytes=VMEM_LIMIT),
        name="retention",
    )(rq, rk, rv, rg_act, decay, zeta_p, xi_p, g_b)


def _cmp_kernel(kc_ref, vc_ref, w1_ref, pe_ref, w2_ref, c_ref, s1_ref, s2_ref, kv_ref, vt_ref):
    n_piece = kc_ref.shape[1] // CMP_STRIDE
    halves = CMP_LEN // CMP_STRIDE
    lhs = {}
    for t, src in enumerate((kc_ref, vc_ref)):
        rows = [src[0, pl.ds(r, n_piece, stride=CMP_STRIDE), :] for r in range(CMP_STRIDE)]
        for half in range(halves):
            pe0 = half * CMP_STRIDE
            lhs[t, half] = jnp.concatenate(
                [(rows[r] + pe_ref[t, pe0 + r:pe0 + r + 1, :]).astype(BF16) for r in range(CMP_STRIDE)], axis=1)
    part = {key: _nn(lhs[key], w1_ref[key[0], key[1]]) for key in lhs}
    out = jnp.zeros((n_piece, kv_ref.shape[2]), F32)
    for t in range(2):
        hid = part[t, 0]
        for half in range(1, halves):
            hid = hid + pltpu.roll(part[t, half], n_piece - half, 0)
        out = out + _nn((hid * _sigmoid(hid)).astype(BF16), w2_ref[t])
    for t in range(out.shape[1] // LANES):
        sl = slice(t * LANES, (t + 1) * LANES)
        kv = _rope_tile(out[:, sl], c_ref[...], s1_ref[...], s2_ref[...])
        kv_ref[0, :, sl] = kv.astype(kv_ref.dtype)
        vt_ref[0, t] = kv.T[NSA_DH:2 * NSA_DH].astype(vt_ref.dtype)


def _compress_weights(w1_k, w2_k, w1_v, w2_v, pe_k, pe_v):
    dh, hid = NSA_DH, CMP_HIDDEN

    def first(w1):
        w = w1.reshape(CMP_LEN, dh, hid)
        z = jnp.zeros_like(w)
        return jnp.concatenate([jnp.concatenate([w, z], axis=2), jnp.concatenate([z, w], axis=2)], axis=1)

    def second(w2, off):
        z = jnp.zeros_like(w2)
        rows = []
        for g in range(NSA_GROUPS):
            c = [z, z, z, z]
            c[2 * g + off] = w2
            rows.append(jnp.concatenate(c, axis=1))
        return jnp.concatenate(rows, axis=0)

    w1 = jnp.stack([first(w1_k), first(w1_v)]).astype(BF16).reshape(
        2, CMP_LEN // CMP_STRIDE, CMP_STRIDE * NSA_GROUPS * dh, NSA_GROUPS * hid)
    w2 = jnp.stack([second(w2_k, 0), second(w2_v, 1)]).astype(BF16)
    pe = jnp.stack([jnp.concatenate([pe_k, pe_k], axis=1), jnp.concatenate([pe_v, pe_v], axis=1)])
    return w1, w2, pe


def _compress(kc, vc, w1, w2, pe, tabs_cmp):
    b, s, w = kc.shape
    n_piece = s // CMP_STRIDE
    const2 = lambda bi: (0, 0)
    src = pl.BlockSpec((1, s, w), lambda bi: (bi, 0, 0))
    dst = pl.BlockSpec((1, n_piece, 2 * w), lambda bi: (bi, 0, 0))
    shape = jax.ShapeDtypeStruct((b, n_piece, 2 * w), BF16)
    return pl.pallas_call(
        _cmp_kernel,
        grid=(b,),
        in_specs=[src, src,
                  pl.BlockSpec(w1.shape, lambda bi: (0, 0, 0, 0)),
                  pl.BlockSpec(pe.shape, lambda bi: (0, 0, 0)),
                  pl.BlockSpec(w2.shape, lambda bi: (0, 0, 0)),
                  pl.BlockSpec((n_piece, LANES), const2),
                  pl.BlockSpec((n_piece, LANES), const2),
                  pl.BlockSpec((n_piece, LANES), const2)],
        out_specs=[dst, pl.BlockSpec((1, NSA_GROUPS, NSA_DH, n_piece), lambda bi: (bi, 0, 0, 0))],
        out_shape=[shape, jax.ShapeDtypeStruct((b, NSA_GROUPS, NSA_DH, n_piece), BF16)],
        compiler_params=pltpu.CompilerParams(
            dimension_semantics=("parallel",), vmem_limit_bytes=VMEM_LIMIT),
        name="compress",
    )(kc, vc, w1, pe, w2, *tabs_cmp)


def _nsa_kernel(q_ref, kcmp_ref, vct_ref, ksx_ref, vst_ref, kwx_ref, vwt_ref, gate_ref, ovt_ref, o_ref):
    tq = TQ
    seq = q_ref.shape[1]
    nb = seq // SLC_LEN
    n_cmp = kcmp_ref.shape[1]
    hpg, dh = NSA_HPG, NSA_DH
    assert WIN % tq == 0 and seq % tq == 0
    group = pl.program_id(1)

    low = lax.broadcasted_iota(jnp.int32, (tq, LANES), 1) < dh
    eye = jnp.where(lax.broadcasted_iota(jnp.int32, (tq, tq), 0)
                    == lax.broadcasted_iota(jnp.int32, (tq, tq), 1), 1.0, 0.0).astype(BF16)
    blk = lax.broadcasted_iota(jnp.int32, (nb, tq), 0)
    col = lax.broadcasted_iota(jnp.int32, (nb, tq), 1)
    crow = lax.broadcasted_iota(jnp.int32, (n_cmp, tq), 0)
    ccol = lax.broadcasted_iota(jnp.int32, (n_cmp, tq), 1)
    kcm = kcmp_ref[0]
    vct = vct_ref[0, 0]
    ovt = ovt_ref[...]

    def per_head(x):
        return jnp.concatenate([x] * hpg, axis=1)

    key_off = lax.broadcasted_iota(jnp.int32, (tq, tq), 0)
    qry_off = lax.broadcasted_iota(jnp.int32, (tq, tq), 1)
    not_after = per_head(jnp.where(key_off <= qry_off, 0.0, NEG))
    inside_win = per_head(jnp.where(key_off > qry_off, 0.0, NEG))

    def masked(s, first_key, t0, windowed):
        blocks = []
        for r in range(0, s.shape[0], tq):
            blk_s = s[r:r + tq]
            if first_key + r == t0:
                blk_s = blk_s + not_after
            elif windowed and first_key + r == t0 - WIN:
                blk_s = blk_s + inside_win
            blocks.append(blk_s)
        return jnp.concatenate(blocks, axis=0)

    def normalise(acc):
        return acc[0:dh] / acc[dh:dh + 1]

    def select_blocks(psum, t0):
        tcol = col + t0
        bcausal = blk * SLC_LEN <= tcol
        n_live = (t0 + tq - 1) // SLC_LEN + 1
        top_n = min(SLC_TOPK, nb)
        if n_live <= top_n:
            bias = jnp.where(bcausal, 0.0, NEG)
        else:
            p_hi = psum.astype(BF16)
            p_lo = (psum - p_hi.astype(F32)).astype(BF16)
            imp = _nn(ovt, p_hi) + _nn(ovt, p_lo)
            cur = tcol // SLC_LEN
            forced = (blk == 0) | (blk == cur) | (blk == cur - 1)
            imp = jnp.where(bcausal, jnp.where(forced, FORCE, imp), NEG)
            rank = jnp.zeros((nb, tq), F32)
            for j in range(n_live):
                r = imp[j:j + 1, :]
                rank = rank + jnp.where(blk > j, jnp.where(r >= imp, 1.0, 0.0), jnp.where(r > imp, 1.0, 0.0))
            bias = jnp.where((rank < float(top_n)) & bcausal, 0.0, NEG)
        feat = jnp.concatenate([jnp.zeros((dh, tq), F32), bias,
                                jnp.zeros((LANES - dh - nb, tq), F32)], axis=0).astype(BF16)
        return _nt(eye, feat)

    class TileGroup:
        def __init__(self, tiles):
            self.tiles = tiles
            self.wstart = {t0: max(t0 - WIN, 0) for t0 in tiles}
            self.wkeys = {t0: slice(self.wstart[t0], t0 + tq) for t0 in tiles}
            self.skeys = {t0: slice(0, t0 + tq) for t0 in tiles}

        def scores_window_compressed(self):
            self.heads, self.qs = {}, {}
            for t0 in self.tiles:
                qf = q_ref[0, t0:t0 + tq, :].astype(F32)
                hl = []
                for hh in range(hpg):
                    t = qf[:, (hh // 2) * LANES:(hh // 2 + 1) * LANES]
                    if hh % 2 == 1:
                        t = _swap_halves(t)
                    hl.append(jnp.where(low, t, 0.0))
                self.heads[t0] = hl
                self.qs[t0] = jnp.concatenate(hl, axis=0).astype(BF16)
            self.s_w = {t0: _nt(kwx_ref[0, self.wkeys[t0], :], self.qs[t0]) for t0 in self.tiles}
            self.s_c = {t0: _nt(kcm, self.qs[t0]) for t0 in self.tiles}

        def select_and_scores_selected(self):
            self.p_cmp, qsel = {}, {}
            for t0 in self.tiles:
                cmask = (crow * CMP_STRIDE + (CMP_LEN - 1)) <= (ccol + t0)
                p_all = []
                psum = jnp.zeros((n_cmp, tq), F32)
                for hh in range(hpg):
                    sh = jnp.where(cmask, self.s_c[t0][:, hh * tq:(hh + 1) * tq], NEG)
                    e = jnp.exp2(sh - jnp.max(sh, axis=0, keepdims=True))
                    p = jnp.where(cmask, e / jnp.sum(e, axis=0, keepdims=True), 0.0)
                    psum = psum + p
                    p_all.append(p.astype(BF16))
                self.p_cmp[t0] = jnp.concatenate(p_all, axis=1)
                qbias = select_blocks(psum, t0)
                qsel[t0] = jnp.concatenate([hd + qbias for hd in self.heads[t0]], axis=0).astype(BF16)
            self.s_s = {t0: _nt(ksx_ref[0, self.skeys[t0], :], qsel[t0]) for t0 in self.tiles}

        def outputs_window_compressed(self):
            e_w = {}
            for t0 in self.tiles:
                sw = masked(self.s_w[t0], self.wstart[t0], t0, True)
                e_w[t0] = jnp.exp2(sw - jnp.max(sw, axis=0, keepdims=True)).astype(BF16)
            self.o_win = {t0: normalise(_nn(vwt_ref[0, 0, :, self.wkeys[t0]], e_w[t0])) for t0 in self.tiles}
            self.o_cmp = {t0: _nn(vct, self.p_cmp[t0]) for t0 in self.tiles}

        def outputs_selected_and_store(self):
            e_s = {}
            for t0 in self.tiles:
                ss = masked(self.s_s[t0], 0, t0, False)
                e_s[t0] = jnp.exp2(ss - jnp.max(ss, axis=0, keepdims=True)).astype(BF16)
            o_sel = {t0: normalise(_nn(vst_ref[0, 0, :, self.skeys[t0]], e_s[t0])) for t0 in self.tiles}
            for t0 in self.tiles:
                gt = gate_ref[0, :, t0:t0 + tq]
                outs = []
                for hh in range(hpg):
                    hc = slice(hh * tq, (hh + 1) * tq)

                    def gate_row(branch):
                        by_group = [gt[branch * NSA_HEADS + g * hpg + hh:branch * NSA_HEADS + g * hpg + hh + 1, :]
                                    for g in range(NSA_GROUPS)]
                        row = by_group[-1]
                        for g in range(NSA_GROUPS - 2, -1, -1):
                            row = jnp.where(group == g, by_group[g], row)
                        return row

                    outs.append(gate_row(0) * self.o_cmp[t0][:, hc] + gate_row(1) * o_sel[t0][:, hc]
                                + gate_row(2) * self.o_win[t0][:, hc])
                o_ref[0, t0:t0 + tq, :] = jnp.concatenate(outs, axis=0).T.astype(o_ref.dtype)

    starts = list(range(0, seq, tq))
    groups = [TileGroup(starts[i:i + TILE_GROUP]) for i in range(0, len(starts), TILE_GROUP)]
    n = len(groups)
    groups[0].scores_window_compressed()
    for i, g in enumerate(groups):
        if i + 1 < n:
            groups[i + 1].scores_window_compressed()
        g.select_and_scores_selected()
        g.outputs_window_compressed()
        if i > 0:
            groups[i - 1].outputs_selected_and_store()
    groups[n - 1].outputs_selected_and_store()


def _nsa_attention(nq, kvcmp, vct, ksx, vst, kwx, vwt, gate_t, ovt):
    b, s, _ = nq.shape
    n_cmp = kvcmp.shape[1]
    gw = NSA_HPG * NSA_DH
    per_group = lambda rows, width: pl.BlockSpec((1, rows, width), lambda bi, g: (bi, 0, g))
    per_group_t = lambda a: pl.BlockSpec((1, 1) + a.shape[2:], lambda bi, g: (bi, g, 0, 0))
    return pl.pallas_call(
        _nsa_kernel,
        grid=(b, NSA_GROUPS),
        in_specs=[per_group(s, gw),
                  per_group(n_cmp, LANES), per_group_t(vct),
                  per_group(s, LANES), per_group_t(vst), per_group(s, LANES), per_group_t(vwt),
                  pl.BlockSpec((1,) + gate_t.shape[1:], lambda bi, g: (bi, 0, 0)),
                  pl.BlockSpec(ovt.shape, lambda bi, g: (0, 0))],
        out_specs=per_group(s, gw),
        out_shape=jax.ShapeDtypeStruct((b, s, NSA_HEADS * NSA_DH), BF16),
        compiler_params=pltpu.CompilerParams(
            dimension_semantics=("parallel", "parallel"), vmem_limit_bytes=VMEM_LIMIT),
        name="nsa_attn",
    )(nq, kvcmp, vct, ksx, vst, kwx, vwt, gate_t, ovt)


def _ffn_kernel(x_ref, yr_ref, yn_ref, mod_ref, g2_ref, gf_ref, wo_hbm, wg_hbm, wu_hbm, wd_hbm,
                o_ref, x1_ref, act_ref, wo_ref, wg_ref, wu_ref, wd_ref, stage_ref, stage_sem):
    @pl.when(pl.program_id(0) == 0)
    def _():
        jobs = [(src, dst, r0, min(CAST_ROWS, dst.shape[0] - r0))
                for src, dst in ((wo_hbm, wo_ref), (wg_hbm, wg_ref), (wu_hbm, wu_ref), (wd_hbm, wd_ref))
                for r0 in range(0, dst.shape[0], CAST_ROWS)]

        def copy_of(k):
            src, dst, r0, rows = jobs[k]
            return pltpu.make_async_copy(src.at[0, pl.ds(r0, rows), :],
                                         stage_ref.at[k % 2, pl.ds(0, rows), pl.ds(0, dst.shape[1])],
                                         stage_sem.at[k % 2])

        def consume(k):
            _, dst, r0, rows = jobs[k]
            dst[r0:r0 + rows, :] = stage_ref[k % 2, 0:rows, 0:dst.shape[1]].astype(BF16)

        _stream_staged(len(jobs), copy_of, consume)

    half_w = yr_ref.shape[1]
    d_ff = wg_ref.shape[1]
    n_chunks = d_ff // TF
    rows_a = slice(0, x_ref.shape[0] // 2)
    rows_b = slice(x_ref.shape[0] // 2, x_ref.shape[0])

    def mix(rows):
        return _nn(yr_ref[rows, :], wo_ref[0:half_w, :]) + _nn(yn_ref[rows, :], wo_ref[half_w:2 * half_w, :])

    def mid_norm(rows, mixed):
        x1 = x_ref[rows, :] + mod_ref[0, 2:3, :] * mixed
        x1_ref[rows, :] = x1
        y = x1 * lax.rsqrt(jnp.mean(x1 * x1, axis=-1, keepdims=True) + EPS) * g2_ref[...]
        return (y * (1.0 + mod_ref[0, 4:5, :]) + mod_ref[0, 3:4, :]).astype(BF16)

    def ff_chunk(rows, h2, j):
        sl = slice(j * TF, (j + 1) * TF)
        gate = _nn(h2, wg_ref[:, sl])
        up = _nn(h2, wu_ref[:, sl])
        act_ref[rows, sl] = (gate * _sigmoid(gate) * up).astype(BF16)

    def down(rows):
        return x1_ref[rows, :] + mod_ref[0, 5:6, :] * _nn(act_ref[rows, :], wd_ref[...])

    def final_norm(rows, xo):
        o_ref[rows, :] = xo * lax.rsqrt(jnp.mean(xo * xo, axis=-1, keepdims=True) + EPS) * gf_ref[...]

    mix_a = mix(rows_a)
    mix_b = mix(rows_b)
    h2_a = mid_norm(rows_a, mix_a)
    ff_chunk(rows_a, h2_a, 0)
    h2_b = mid_norm(rows_b, mix_b)
    for j in range(1, n_chunks):
        ff_chunk(rows_a, h2_a, j)
    xo_a = down(rows_a)
    ff_chunk(rows_b, h2_b, 0)
    final_norm(rows_a, xo_a)
    for j in range(1, n_chunks):
        ff_chunk(rows_b, h2_b, j)
    final_norm(rows_b, down(rows_b))


def _out_ffn(x2d, y_ret, y_nsa, mod, w_out, g2, gf, wg, wu, wd, seq):
    n, d = x2d.shape
    tm = TM_FF
    d_ff = wg.shape[2]
    tiles_per_seq = seq // tm
    half_w = y_ret.shape[1]
    row = lambda i: (i, 0)
    in_hbm = pl.BlockSpec(memory_space=pl.ANY)
    weights = (w_out, wg, wu, wd)
    stage_cols = max(w.shape[2] for w in weights)
    return pl.pallas_call(
        _ffn_kernel,
        grid=(n // tm,),
        in_specs=[pl.BlockSpec((tm, d), row),
                  pl.BlockSpec((tm, half_w), row),
                  pl.BlockSpec((tm, half_w), row),
                  pl.BlockSpec((1, 6, d), lambda i: (i // tiles_per_seq, 0, 0)),
                  pl.BlockSpec((1, d), lambda i: (0, 0)),
                  pl.BlockSpec((1, d), lambda i: (0, 0)),
                  in_hbm, in_hbm, in_hbm, in_hbm],
        out_specs=pl.BlockSpec((tm, d), row),
        out_shape=jax.ShapeDtypeStruct((n, d), F32),
        scratch_shapes=[pltpu.VMEM((tm, d), F32), pltpu.VMEM((tm, d_ff), BF16)]
                       + [pltpu.VMEM(w.shape[1:], BF16) for w in weights]
                       + [pltpu.VMEM((2, CAST_ROWS, stage_cols), F32), pltpu.SemaphoreType.DMA((2,))],
        compiler_params=pltpu.CompilerParams(
            dimension_semantics=("arbitrary",), vmem_limit_bytes=VMEM_LIMIT),
        name="out_ffn",
    )(x2d, y_ret, y_nsa, mod, g2.reshape(1, d), gf.reshape(1, d), *weights)


def kernel(x, c, ln_mix_g, ln_ffn_g, w_ada, b_ada, w_in, cmp_pe_k, cmp_w1_k, cmp_w2_k,
           cmp_pe_v, cmp_w1_v, cmp_w2_v, w_out, w_ff_gate, w_ff_up, w_ff_down, ln_final_g):
    assert w_in.shape[0] == 1, "the final RMSNorm is fused into the (single) layer's FFN kernel"
    b, s, d = x.shape
    lane = np.arange(LANES)
    tabs = _rope_tables(np.arange(s), np.ones(LANES, bool))
    n_piece = s // CMP_STRIDE
    tabs_cmp = _rope_tables(np.arange(n_piece) * CMP_STRIDE + CMP_LEN - 1, lane < NSA_DH)

    mod = _ada(c, w_ada[0], b_ada[0]).reshape(b, 6, d)
    rq, rk, rv, rg_act, nq, kc, vc, ksx, vst, kwx, vwt, gate_t = _in_proj(
        x, ln_mix_g[0], mod, w_in.astype(BF16), tabs, _block_onehot_table(s))
    y_ret = _retention(rq, rk, rv, rg_act, _retention_tables())
    w1, w2, pe = _compress_weights(cmp_w1_k[0], cmp_w2_k[0], cmp_w1_v[0], cmp_w2_v[0],
                                   cmp_pe_k[0], cmp_pe_v[0])
    kvcmp, vct = _compress(kc, vc, w1, w2, pe, tabs_cmp)
    y_nsa = _nsa_attention(nq, kvcmp, vct, ksx, vst, kwx, vwt, gate_t, _overlap_t(s))
    out = _out_ffn(x.reshape(b * s, d), y_ret.reshape(b * s, -1), y_nsa.reshape(b * s, -1), mod,
                   w_out, ln_ffn_g[0], ln_final_g, w_ff_gate, w_ff_up, w_ff_down, s)
    return out.reshape(b, s, d)
```

```python
import numpy as np
import jax
import jax.numpy as jnp
from jax import lax
from jax.experimental import pallas as pl
from jax.experimental.pallas import tpu as pltpu

F32 = jnp.float32
BF16 = jnp.bfloat16

D_MODEL = 1024
RET_HEADS = 4
RET_DK = 64
RET_DV = 128
RET_CHUNK = 128
NSA_HEADS = 8
NSA_GROUPS = 2
NSA_HPG = NSA_HEADS // NSA_GROUPS
NSA_DH = 64
CMP_LEN = 32
CMP_STRIDE = 16
CMP_HIDDEN = 128
SLC_LEN = 64
SLC_TOPK = 16
WIN = 512
D_FF = ((8 * D_MODEL + 3 * 256 - 1) // (3 * 256)) * 256
ROPE_THETA = 10000.0
EPS = 1e-6
NEG = -1e30
FORCE = 1e6

LANES = 128
SUBLANES = 8
HALF = NSA_DH // 2
ONES_ROWS = 16
GATE_ROWS = 3 * NSA_HEADS

TM_IN = 1024
PROJ_W = 512
RET_GROUP = 8
Q_SCALE = NSA_DH ** -0.5 * float(np.log2(np.e))
TQ = 128
TILE_GROUP = 2
TM_FF = 512
TF = 256
VMEM_LIMIT = 56 * 1024 * 1024

_R_RQ, _R_RK, _R_RV, _R_RG, _R_NQ = 0, 256, 512, 1024, 1536
_R_KC, _R_VC, _R_KS, _R_VS, _R_KW, _R_VW, _R_GATE = 2048, 2176, 2304, 2432, 2560, 2688, 2816
IN_COLS_K = -(-(_R_GATE + 3 * NSA_HEADS) // PROJ_W) * PROJ_W


def _sigmoid(x):
    return 1.0 / (1.0 + jnp.exp(-x))


def _nt(a, b):
    return lax.dot_general(a, b, (((1,), (1,)), ((), ())), preferred_element_type=F32)


def _nn(a, b):
    return jnp.dot(a, b, preferred_element_type=F32)


def _rope_tile(a, c, s1, s2):
    return a * c + pltpu.roll(a, HALF, 1) * s1 + pltpu.roll(a, LANES - HALF, 1) * s2


def _swap_halves(a):
    return pltpu.roll(a, LANES // 2, 1)


def _rope_tables(pos, rotary_lanes):
    pos = np.asarray(pos, np.float64)
    lane = np.arange(LANES)
    within = lane % NSA_DH
    freq = ROPE_THETA ** (-(within % HALF).astype(np.float64) / HALF)
    ang = pos[:, None] * freq[None, :]
    cos, sin = np.cos(ang), np.sin(ang)
    first = (within < HALF)[None, :]
    rot = np.asarray(rotary_lanes, bool)[None, :]
    c = np.where(rot, cos, 1.0)
    s1 = np.where(rot & ~first, sin, 0.0)
    s2 = np.where(rot & first, -sin, 0.0)
    return (jnp.asarray(c, F32), jnp.asarray(s1, F32), jnp.asarray(s2, F32))


def _block_onehot_table(seq):
    t = np.zeros((seq, LANES), np.float32)
    pos = np.arange(seq)
    t[pos, NSA_DH + pos // SLC_LEN] = 1.0
    return jnp.asarray(t)


def _retention_tables():
    h = jnp.arange(RET_HEADS, dtype=F32)
    log_g = jnp.log(1.0 - 2.0 ** (-5.0 - h))
    c = RET_CHUNK
    idx = jnp.arange(c, dtype=F32)
    diff = idx[:, None] - idx[None, :]
    causal = diff >= 0
    decay = jnp.where(causal, jnp.exp(log_g[:, None, None] * jnp.where(causal, diff, 0.0)), 0.0)
    zeta = jnp.exp(log_g[:, None] * (c - 1.0 - idx))
    xi = jnp.exp(log_g[:, None] * (idx + 1.0))
    g_chunk = jnp.exp(log_g * c)

    def pair_lanes(t):
        t = t.reshape(RET_HEADS // 2, 2, c)
        return jnp.repeat(jnp.transpose(t, (0, 2, 1)), RET_DK, axis=2)

    g_b = jnp.broadcast_to(g_chunk[:, None, None], (RET_HEADS, 1, LANES))
    return decay, pair_lanes(zeta), pair_lanes(xi), g_b


def _overlap_t(seq):
    n_c = seq // CMP_STRIDE - CMP_LEN // CMP_STRIDE + 1
    nb = seq // SLC_LEN
    cs = np.arange(n_c) * CMP_STRIDE
    bs = np.arange(nb) * SLC_LEN
    ov = np.maximum(np.minimum(cs[:, None] + CMP_LEN, bs[None] + SLC_LEN)
                    - np.maximum(cs[:, None], bs[None]), 0).astype(np.float64) / CMP_LEN
    ncp = seq // CMP_STRIDE
    ovp = np.zeros((ncp, nb))
    ovp[:n_c] = ov
    return jnp.asarray(ovp.T, BF16)


def _ada_kernel(c_ref, w_ref, b_ref, o_ref):
    c = c_ref[...]
    o_ref[...] = _nn(c * _sigmoid(c), w_ref[...]) + b_ref[...]


def _ada(c, w, b):
    bsz, d = c.shape
    n = w.shape[1]
    tn = 1024
    return pl.pallas_call(
        _ada_kernel,
        grid=(n // tn,),
        in_specs=[pl.BlockSpec((bsz, d), lambda j: (0, 0)),
                  pl.BlockSpec((d, tn), lambda j: (0, j)),
                  pl.BlockSpec((1, tn), lambda j: (0, j))],
        out_specs=pl.BlockSpec((bsz, tn), lambda j: (0, j)),
        out_shape=jax.ShapeDtypeStruct((bsz, n), F32),
        compiler_params=pltpu.CompilerParams(vmem_limit_bytes=VMEM_LIMIT),
        name="ada",
    )(c, w, b.reshape(1, n))


def _inproj_kernel(x_ref, g_ref, mod_ref, w_ref, c_ref, s1_ref, s2_ref, hot_ref,
                   rq_ref, rk_ref, rv_ref, rg_ref, nq_ref, kc_ref, vc_ref,
                   ksx_ref, vst_ref, kwx_ref, vwt_ref, gate_ref, wb_ref):
    @pl.when((pl.program_id(0) == 0) & (pl.program_id(1) == 0))
    def _():
        n_in = w_ref.shape[2]
        whole = (n_in // PROJ_W) * PROJ_W
        for c0 in range(0, whole, PROJ_W):
            wb_ref[:, c0:c0 + PROJ_W] = w_ref[0, :, c0:c0 + PROJ_W].astype(BF16)
        wb_ref[:, whole:] = jnp.zeros((wb_ref.shape[0], wb_ref.shape[1] - whole), BF16)
        wb_ref[:, whole:n_in] = w_ref[0, :, whole:n_in].astype(BF16)

    def modulated_norm(rows):
        x = x_ref[0, rows, :]
        y = x * lax.rsqrt(jnp.mean(x * x, axis=-1, keepdims=True) + EPS) * g_ref[...]
        return (y * (1.0 + mod_ref[0, 1:2, :]) + mod_ref[0, 0:1, :]).astype(BF16)

    def product_steps(rows, hb):
        tabs = (c_ref[rows, :], s1_ref[rows, :], s2_ref[rows, :])
        low = lax.broadcasted_iota(jnp.int32, (rows.stop - rows.start, LANES), 1) < NSA_DH

        def proj(c0):
            a = _nn(hb, wb_ref[:, c0:c0 + PROJ_W])
            return [a[:, t * LANES:(t + 1) * LANES] for t in range(PROJ_W // LANES)]

        def put(out_ref, t, value):
            out_ref[0, rows, t * LANES:(t + 1) * LANES] = value.astype(out_ref.dtype)

        def roped(tiles, scale, out_ref):
            for t, a in enumerate(tiles):
                r = _rope_tile(a, *tabs)
                put(out_ref, t, r if scale == 1.0 else r * scale)

        def per_group(tile, fill, out_ref):
            put(out_ref, 0, jnp.where(low, tile, fill))
            put(out_ref, 1, jnp.where(low, _swap_halves(tile), fill))

        def per_group_t(tile, out_ref):
            t = tile.T
            ones = jnp.ones((ONES_ROWS, t.shape[1]), out_ref.dtype)
            for g in range(NSA_GROUPS):
                out_ref[0, g, 0:NSA_DH, rows] = t[g * NSA_DH:(g + 1) * NSA_DH].astype(out_ref.dtype)
                out_ref[0, g, NSA_DH:NSA_DH + ONES_ROWS, rows] = ones

        def retention_qk():
            tiles = proj(_R_RQ)
            roped(tiles[0:2], 1.0, rq_ref)
            roped(tiles[2:4], RET_DK ** -0.5, rk_ref)

        def retention_v():
            for t, a in enumerate(proj(_R_RV)):
                put(rv_ref, t, a)

        def retention_gate():
            for t, a in enumerate(proj(_R_RG)):
                put(rg_ref, t, a * _sigmoid(a))

        def nsa_q():
            roped(proj(_R_NQ), Q_SCALE, nq_ref)

        def nsa_compress_selected():
            kc_t, vc_t, ks_t, vs_t = proj(_R_KC)
            kc_ref[0, rows, :] = kc_t
            vc_ref[0, rows, :] = vc_t
            per_group(_rope_tile(ks_t, *tabs), hot_ref[rows, :], ksx_ref)
            per_group_t(vs_t, vst_ref)

        def nsa_window_gates():
            kw_t, vw_t, gates_t, _ = proj(_R_KW)
            per_group(_rope_tile(kw_t, *tabs), 0.0, kwx_ref)
            per_group_t(vw_t, vwt_ref)
            gate_ref[0, :, rows] = _sigmoid(gates_t.T[0:GATE_ROWS])

        return [retention_qk, retention_v, retention_gate, nsa_q, nsa_compress_selected, nsa_window_gates]

    tm = x_ref.shape[1]
    rows_a, rows_b = slice(0, tm // 2), slice(tm // 2, tm)
    steps_a = product_steps(rows_a, modulated_norm(rows_a))
    steps_a[0]()
    steps_b = product_steps(rows_b, modulated_norm(rows_b))
    for step in steps_a[1:] + steps_b:
        step()


def _in_proj(x, ln_g, mod, w_in, tabs, hot):
    b, s, d = x.shape
    tm = TM_IN
    grid = (b, s // tm)
    tab_spec = pl.BlockSpec((tm, LANES), lambda bi, j: (j, 0))

    def out(n, dtype):
        return (jax.ShapeDtypeStruct((b, s, n), dtype), pl.BlockSpec((1, tm, n), lambda bi, j: (bi, j, 0)))

    def out_t(rows, dtype):
        return (jax.ShapeDtypeStruct((b, NSA_GROUPS, rows, s), dtype),
                pl.BlockSpec((1, NSA_GROUPS, rows, tm), lambda bi, j: (bi, 0, 0, j)))

    vt_rows = NSA_DH + ONES_ROWS
    outs = [out(256, BF16), out(256, BF16), out(512, BF16), out(512, BF16), out(512, BF16),
            out(LANES, F32), out(LANES, F32),
            out(256, BF16), out_t(vt_rows, BF16), out(256, BF16), out_t(vt_rows, BF16),
            (jax.ShapeDtypeStruct((b, GATE_ROWS, s), F32),
             pl.BlockSpec((1, GATE_ROWS, tm), lambda bi, j: (bi, 0, j)))]
    return pl.pallas_call(
        _inproj_kernel,
        grid=grid,
        in_specs=[pl.BlockSpec((1, tm, d), lambda bi, j: (bi, j, 0)),
                  pl.BlockSpec((1, d), lambda bi, j: (0, 0)),
                  pl.BlockSpec((1, 6, d), lambda bi, j: (bi, 0, 0)),
                  pl.BlockSpec((1,) + w_in.shape[1:], lambda bi, j: (0, 0, 0), pipeline_mode=pl.Buffered(1)),
                  tab_spec, tab_spec, tab_spec, tab_spec],
        out_specs=[o[1] for o in outs],
        out_shape=[o[0] for o in outs],
        scratch_shapes=[pltpu.VMEM((d, IN_COLS_K), BF16)],
        compiler_params=pltpu.CompilerParams(
            dimension_semantics=("arbitrary", "arbitrary"), vmem_limit_bytes=VMEM_LIMIT),
        name="in_proj",
    )(x, ln_g.reshape(1, d), mod, w_in, *tabs, hot)


def _ret_kernel(q_ref, k_ref, v_ref, rg_ref, dec_ref, zeta_ref, xi_ref, gch_ref, *rest):
    n_w = (len(rest) - 3) // 2
    w_in_refs, o_ref, w_out_refs = rest[:n_w], rest[n_w], rest[n_w + 1:2 * n_w + 1]
    kv_ref, prev_ref = rest[2 * n_w + 1:]
    for w_src, w_dst in zip(w_in_refs, w_out_refs):
        w_dst[...] = w_src[...].astype(w_dst.dtype)
    c = RET_CHUNK
    n_chunks = q_ref.shape[1] // c
    low = lax.broadcasted_iota(jnp.int32, (c, LANES), 1) < RET_DK

    def chunk_rows(n):
        return pl.ds(pl.multiple_of(n * c, c), c)

    def head_cols(h):
        return slice(h * RET_DV, (h + 1) * RET_DV)

    def kv_body(it, carry):
        kz_t = {}
        for j in range(RET_GROUP):
            rows = chunk_rows(it * RET_GROUP + j)
            for p in range(RET_HEADS // 2):
                pair = slice(p * LANES, (p + 1) * LANES)
                kz_t[j, p] = (k_ref[0, rows, pair].astype(F32) * zeta_ref[p]).T.astype(BF16)
        for j in range(RET_GROUP):
            n = it * RET_GROUP + j
            for h in range(RET_HEADS):
                kv_ref[h, n] = _nn(kz_t[j, h // 2], v_ref[0, chunk_rows(n), head_cols(h)])
        return carry

    lax.fori_loop(0, n_chunks // RET_GROUP, kv_body, 0)

    for h in range(RET_HEADS):
        def scan_body(n, st, h=h):
            prev_ref[h, n] = st.astype(prev_ref.dtype)
            return st * gch_ref[h] + kv_ref[h, n]
        lax.fori_loop(0, n_chunks, scan_body, jnp.zeros((LANES, RET_DV), F32))

    def out_body(it, carry):
        chains = [(j, h) for j in range(RET_GROUP) for h in range(RET_HEADS)]
        chunk = lambda j: it * RET_GROUP + j
        q_own, qx_own, att, ys = {}, {}, {}, {}
        for j, h in chains:
            p, e = divmod(h, 2)
            pair = slice(p * LANES, (p + 1) * LANES)
            mine = low if e == 0 else jnp.logical_not(low)
            q2 = q_ref[0, chunk_rows(chunk(j)), pair].astype(F32)
            q_own[j, h] = jnp.where(mine, q2, 0.0).astype(BF16)
            qx_own[j, h] = jnp.where(mine, q2 * xi_ref[p], 0.0).astype(BF16)
        for j, h in chains:
            pair = slice((h // 2) * LANES, (h // 2 + 1) * LANES)
            att[j, h] = _nt(q_own[j, h], k_ref[0, chunk_rows(chunk(j)), pair])
        for j, h in chains:
            lhs = jnp.concatenate([(att[j, h] * dec_ref[h]).astype(BF16), qx_own[j, h]], axis=1)
            rhs = jnp.concatenate([v_ref[0, chunk_rows(chunk(j)), head_cols(h)], prev_ref[h, chunk(j)]], axis=0)
            ys[j, h] = _nn(lhs, rhs)
        for j, h in chains:
            y = ys[j, h]
            yn = y * lax.rsqrt(jnp.mean(y * y, axis=-1, keepdims=True) + EPS)
            gate = rg_ref[0, chunk_rows(chunk(j)), head_cols(h)].astype(F32)
            o_ref[0, chunk_rows(chunk(j)), head_cols(h)] = (yn * gate).astype(o_ref.dtype)
        return carry

    lax.fori_loop(0, n_chunks // RET_GROUP, out_body, 0)


def _retention(rq, rk, rv, rg_act, tables, weights):
    b, s, _ = rq.shape
    decay, zeta_p, xi_p, g_b = tables
    whole = lambda a: pl.BlockSpec(a.shape, lambda bi: (0,) * a.ndim)
    row = lambda a: pl.BlockSpec((1,) + a.shape[1:], lambda bi: (bi, 0, 0))
    assert all(w.shape[1] % (b * 16) == 0 for w in weights)
    slab = lambda w: pl.BlockSpec((1, w.shape[1] // b, w.shape[2]), lambda bi: (0, bi, 0))
    outs = pl.pallas_call(
        _ret_kernel,
        grid=(b,),
        in_specs=[row(rq), row(rk), row(rv), row(rg_act),
                  whole(decay), whole(zeta_p), whole(xi_p), whole(g_b)] + [slab(w) for w in weights],
        out_specs=[row(rv)] + [slab(w) for w in weights],
        out_shape=[jax.ShapeDtypeStruct(rv.shape, BF16)]
                  + [jax.ShapeDtypeStruct(w.shape, BF16) for w in weights],
        scratch_shapes=[pltpu.VMEM((RET_HEADS, s // RET_CHUNK, LANES, RET_DV), F32),
                        pltpu.VMEM((RET_HEADS, s // RET_CHUNK, LANES, RET_DV), BF16)],
        compiler_params=pltpu.CompilerParams(
            dimension_semantics=("parallel",), vmem_limit_bytes=VMEM_LIMIT),
        name="retention",
    )(rq, rk, rv, rg_act, decay, zeta_p, xi_p, g_b, *weights)
    return outs[0], [w[0] for w in outs[1:]]


def _cmp_kernel(kc_ref, vc_ref, w1_ref, pe_ref, w2_ref, c_ref, s1_ref, s2_ref, kv_ref, vt_ref):
    n_piece = kc_ref.shape[1] // CMP_STRIDE
    halves = CMP_LEN // CMP_STRIDE
    lhs = {}
    for t, src in enumerate((kc_ref, vc_ref)):
        rows = [src[0, pl.ds(r, n_piece, stride=CMP_STRIDE), :] for r in range(CMP_STRIDE)]
        for half in range(halves):
            pe0 = half * CMP_STRIDE
            lhs[t, half] = jnp.concatenate(
                [(rows[r] + pe_ref[t, pe0 + r:pe0 + r + 1, :]).astype(BF16) for r in range(CMP_STRIDE)], axis=1)
    part = {key: _nn(lhs[key], w1_ref[key[0], key[1]]) for key in lhs}
    out = jnp.zeros((n_piece, kv_ref.shape[2]), F32)
    for t in range(2):
        hid = part[t, 0]
        for half in range(1, halves):
            hid = hid + pltpu.roll(part[t, half], n_piece - half, 0)
        out = out + _nn((hid * _sigmoid(hid)).astype(BF16), w2_ref[t])
    for t in range(out.shape[1] // LANES):
        sl = slice(t * LANES, (t + 1) * LANES)
        kv = _rope_tile(out[:, sl], c_ref[...], s1_ref[...], s2_ref[...])
        kv_ref[0, :, sl] = kv.astype(kv_ref.dtype)
        vt_ref[0, t] = kv.T[NSA_DH:2 * NSA_DH].astype(vt_ref.dtype)


def _compress_weights(w1_k, w2_k, w1_v, w2_v, pe_k, pe_v):
    dh, hid = NSA_DH, CMP_HIDDEN

    def first(w1):
        w = w1.reshape(CMP_LEN, dh, hid)
        z = jnp.zeros_like(w)
        return jnp.concatenate([jnp.concatenate([w, z], axis=2), jnp.concatenate([z, w], axis=2)], axis=1)

    def second(w2, off):
        z = jnp.zeros_like(w2)
        rows = []
        for g in range(NSA_GROUPS):
            c = [z, z, z, z]
            c[2 * g + off] = w2
            rows.append(jnp.concatenate(c, axis=1))
        return jnp.concatenate(rows, axis=0)

    w1 = jnp.stack([first(w1_k), first(w1_v)]).astype(BF16).reshape(
        2, CMP_LEN // CMP_STRIDE, CMP_STRIDE * NSA_GROUPS * dh, NSA_GROUPS * hid)
    w2 = jnp.stack([second(w2_k, 0), second(w2_v, 1)]).astype(BF16)
    pe = jnp.stack([jnp.concatenate([pe_k, pe_k], axis=1), jnp.concatenate([pe_v, pe_v], axis=1)])
    return w1, w2, pe


def _compress(kc, vc, w1, w2, pe, tabs_cmp):
    b, s, w = kc.shape
    n_piece = s // CMP_STRIDE
    const2 = lambda bi: (0, 0)
    src = pl.BlockSpec((1, s, w), lambda bi: (bi, 0, 0))
    dst = pl.BlockSpec((1, n_piece, 2 * w), lambda bi: (bi, 0, 0))
    shape = jax.ShapeDtypeStruct((b, n_piece, 2 * w), BF16)
    return pl.pallas_call(
        _cmp_kernel,
        grid=(b,),
        in_specs=[src, src,
                  pl.BlockSpec(w1.shape, lambda bi: (0, 0, 0, 0)),
                  pl.BlockSpec(pe.shape, lambda bi: (0, 0, 0)),
                  pl.BlockSpec(w2.shape, lambda bi: (0, 0, 0)),
                  pl.BlockSpec((n_piece, LANES), const2),
                  pl.BlockSpec((n_piece, LANES), const2),
                  pl.BlockSpec((n_piece, LANES), const2)],
        out_specs=[dst, pl.BlockSpec((1, NSA_GROUPS, NSA_DH, n_piece), lambda bi: (bi, 0, 0, 0))],
        out_shape=[shape, jax.ShapeDtypeStruct((b, NSA_GROUPS, NSA_DH, n_piece), BF16)],
        compiler_params=pltpu.CompilerParams(
            dimension_semantics=("parallel",), vmem_limit_bytes=VMEM_LIMIT),
        name="compress",
    )(kc, vc, w1, pe, w2, *tabs_cmp)


def _nsa_kernel(q_ref, kcmp_ref, vct_ref, ksx_ref, vst_ref, kwx_ref, vwt_ref, gate_ref, ovt_ref, o_ref):
    tq = TQ
    seq = q_ref.shape[1]
    nb = seq // SLC_LEN
    n_cmp = kcmp_ref.shape[1]
    hpg, dh = NSA_HPG, NSA_DH
    assert WIN % tq == 0 and seq % tq == 0
    group = pl.program_id(1)

    low = lax.broadcasted_iota(jnp.int32, (tq, LANES), 1) < dh
    eye = jnp.where(lax.broadcasted_iota(jnp.int32, (tq, tq), 0)
                    == lax.broadcasted_iota(jnp.int32, (tq, tq), 1), 1.0, 0.0).astype(BF16)
    blk = lax.broadcasted_iota(jnp.int32, (nb, tq), 0)
    col = lax.broadcasted_iota(jnp.int32, (nb, tq), 1)
    crow = lax.broadcasted_iota(jnp.int32, (n_cmp, tq), 0)
    ccol = lax.broadcasted_iota(jnp.int32, (n_cmp, tq), 1)
    kcm = kcmp_ref[0]
    vct = vct_ref[0, 0]
    ovt = ovt_ref[...]

    def per_head(x):
        return jnp.concatenate([x] * hpg, axis=1)

    key_off = lax.broadcasted_iota(jnp.int32, (tq, tq), 0)
    qry_off = lax.broadcasted_iota(jnp.int32, (tq, tq), 1)
    not_after = per_head(jnp.where(key_off <= qry_off, 0.0, NEG))
    inside_win = per_head(jnp.where(key_off > qry_off, 0.0, NEG))

    def masked(s, first_key, t0, windowed):
        blocks = []
        for r in range(0, s.shape[0], tq):
            blk_s = s[r:r + tq]
            if first_key + r == t0:
                blk_s = blk_s + not_after
            elif windowed and first_key + r == t0 - WIN:
                blk_s = blk_s + inside_win
            blocks.append(blk_s)
        return jnp.concatenate(blocks, axis=0)

    def normalise(acc):
        return acc[0:dh] / acc[dh:dh + 1]

    def select_blocks(psum, t0):
        tcol = col + t0
        bcausal = blk * SLC_LEN <= tcol
        n_live = (t0 + tq - 1) // SLC_LEN + 1
        top_n = min(SLC_TOPK, nb)
        if n_live <= top_n:
            bias = jnp.where(bcausal, 0.0, NEG)
        else:
            p_hi = psum.astype(BF16)
            p_lo = (psum - p_hi.astype(F32)).astype(BF16)
            imp = _nn(ovt, p_hi) + _nn(ovt, p_lo)
            cur = tcol // SLC_LEN
            forced = (blk == 0) | (blk == cur) | (blk == cur - 1)
            imp = jnp.where(bcausal, jnp.where(forced, FORCE, imp), NEG)
            rank = jnp.zeros((nb, tq), F32)
            for j in range(n_live):
                r = imp[j:j + 1, :]
                rank = rank + jnp.where(blk > j, jnp.where(r >= imp, 1.0, 0.0), jnp.where(r > imp, 1.0, 0.0))
            bias = jnp.where((rank < float(top_n)) & bcausal, 0.0, NEG)
        feat = jnp.concatenate([jnp.zeros((dh, tq), F32), bias,
                                jnp.zeros((LANES - dh - nb, tq), F32)], axis=0).astype(BF16)
        return _nt(eye, feat)

    class TileGroup:
        def __init__(self, tiles):
            self.tiles = tiles
            self.wstart = {t0: max(t0 - WIN, 0) for t0 in tiles}
            self.wkeys = {t0: slice(self.wstart[t0], t0 + tq) for t0 in tiles}
            self.skeys = {t0: slice(0, t0 + tq) for t0 in tiles}

        def scores_window_compressed(self):
            self.heads, self.qs = {}, {}
            for t0 in self.tiles:
                qf = q_ref[0, t0:t0 + tq, :].astype(F32)
                hl = []
                for hh in range(hpg):
                    t = qf[:, (hh // 2) * LANES:(hh // 2 + 1) * LANES]
                    if hh % 2 == 1:
                        t = _swap_halves(t)
                    hl.append(jnp.where(low, t, 0.0))
                self.heads[t0] = hl
                self.qs[t0] = jnp.concatenate(hl, axis=0).astype(BF16)
            self.s_w = {t0: _nt(kwx_ref[0, self.wkeys[t0], :], self.qs[t0]) for t0 in self.tiles}
            self.s_c = {t0: _nt(kcm, self.qs[t0]) for t0 in self.tiles}

        def select_and_scores_selected(self):
            self.p_cmp, qsel = {}, {}
            for t0 in self.tiles:
                cmask = (crow * CMP_STRIDE + (CMP_LEN - 1)) <= (ccol + t0)
                p_all = []
                psum = jnp.zeros((n_cmp, tq), F32)
                for hh in range(hpg):
                    sh = jnp.where(cmask, self.s_c[t0][:, hh * tq:(hh + 1) * tq], NEG)
                    e = jnp.exp2(sh - jnp.max(sh, axis=0, keepdims=True))
                    p = jnp.where(cmask, e / jnp.sum(e, axis=0, keepdims=True), 0.0)
                    psum = psum + p
                    p_all.append(p.astype(BF16))
                self.p_cmp[t0] = jnp.concatenate(p_all, axis=1)
                qbias = select_blocks(psum, t0)
                qsel[t0] = jnp.concatenate([hd + qbias for hd in self.heads[t0]], axis=0).astype(BF16)
            self.s_s = {t0: _nt(ksx_ref[0, self.skeys[t0], :], qsel[t0]) for t0 in self.tiles}

        def outputs_window_compressed(self):
            e_w = {}
            for t0 in self.tiles:
                sw = masked(self.s_w[t0], self.wstart[t0], t0, True)
                e_w[t0] = jnp.exp2(sw - jnp.max(sw, axis=0, keepdims=True)).astype(BF16)
            self.o_win = {t0: normalise(_nn(vwt_ref[0, 0, :, self.wkeys[t0]], e_w[t0])) for t0 in self.tiles}
            self.o_cmp = {t0: _nn(vct, self.p_cmp[t0]) for t0 in self.tiles}

        def outputs_selected_and_store(self):
            e_s = {}
            for t0 in self.tiles:
                ss = masked(self.s_s[t0], 0, t0, False)
                e_s[t0] = jnp.exp2(ss - jnp.max(ss, axis=0, keepdims=True)).astype(BF16)
            o_sel = {t0: normalise(_nn(vst_ref[0, 0, :, self.skeys[t0]], e_s[t0])) for t0 in self.tiles}
            for t0 in self.tiles:
                gt = gate_ref[0, :, t0:t0 + tq]
                outs = []
                for hh in range(hpg):
                    hc = slice(hh * tq, (hh + 1) * tq)

                    def gate_row(branch):
                        by_group = [gt[branch * NSA_HEADS + g * hpg + hh:branch * NSA_HEADS + g * hpg + hh + 1, :]
                                    for g in range(NSA_GROUPS)]
                        row = by_group[-1]
                        for g in range(NSA_GROUPS - 2, -1, -1):
                            row = jnp.where(group == g, by_group[g], row)
                        return row

                    outs.append(gate_row(0) * self.o_cmp[t0][:, hc] + gate_row(1) * o_sel[t0][:, hc]
                                + gate_row(2) * self.o_win[t0][:, hc])
                o_ref[0, t0:t0 + tq, :] = jnp.concatenate(outs, axis=0).T.astype(o_ref.dtype)

    starts = list(range(0, seq, tq))
    groups = [TileGroup(starts[i:i + TILE_GROUP]) for i in range(0, len(starts), TILE_GROUP)]
    n = len(groups)
    groups[0].scores_window_compressed()
    for i, g in enumerate(groups):
        if i + 1 < n:
            groups[i + 1].scores_window_compressed()
        g.select_and_scores_selected()
        g.outputs_window_compressed()
        if i > 0:
            groups[i - 1].outputs_selected_and_store()
    groups[n - 1].outputs_selected_and_store()


def _nsa_attention(nq, kvcmp, vct, ksx, vst, kwx, vwt, gate_t, ovt):
    b, s, _ = nq.shape
    n_cmp = kvcmp.shape[1]
    gw = NSA_HPG * NSA_DH
    per_group = lambda rows, width: pl.BlockSpec((1, rows, width), lambda bi, g: (bi, 0, g))
    per_group_t = lambda a: pl.BlockSpec((1, 1) + a.shape[2:], lambda bi, g: (bi, g, 0, 0))
    return pl.pallas_call(
        _nsa_kernel,
        grid=(b, NSA_GROUPS),
        in_specs=[per_group(s, gw),
                  per_group(n_cmp, LANES), per_group_t(vct),
                  per_group(s, LANES), per_group_t(vst), per_group(s, LANES), per_group_t(vwt),
                  pl.BlockSpec((1,) + gate_t.shape[1:], lambda bi, g: (bi, 0, 0)),
                  pl.BlockSpec(ovt.shape, lambda bi, g: (0, 0))],
        out_specs=per_group(s, gw),
        out_shape=jax.ShapeDtypeStruct((b, s, NSA_HEADS * NSA_DH), BF16),
        compiler_params=pltpu.CompilerParams(
            dimension_semantics=("parallel", "parallel"), vmem_limit_bytes=VMEM_LIMIT),
        name="nsa_attn",
    )(nq, kvcmp, vct, ksx, vst, kwx, vwt, gate_t, ovt)


def _ffn_kernel(x_ref, yr_ref, yn_ref, mod_ref, g2_ref, gf_ref, wo_ref, wg_ref, wu_ref, wd_ref,
                o_ref, x1_ref, act_ref):
    half_w = yr_ref.shape[1]
    d_ff = wg_ref.shape[1]
    n_chunks = d_ff // TF
    rows_a = slice(0, x_ref.shape[0] // 2)
    rows_b = slice(x_ref.shape[0] // 2, x_ref.shape[0])

    def mix(rows):
        return _nn(yr_ref[rows, :], wo_ref[0:half_w, :]) + _nn(yn_ref[rows, :], wo_ref[half_w:2 * half_w, :])

    def mid_norm(rows, mixed):
        x1 = x_ref[rows, :] + mod_ref[0, 2:3, :] * mixed
        x1_ref[rows, :] = x1
        y = x1 * lax.rsqrt(jnp.mean(x1 * x1, axis=-1, keepdims=True) + EPS) * g2_ref[...]
        return (y * (1.0 + mod_ref[0, 4:5, :]) + mod_ref[0, 3:4, :]).astype(BF16)

    def ff_chunk(rows, h2, j):
        sl = slice(j * TF, (j + 1) * TF)
        gate = _nn(h2, wg_ref[:, sl])
        up = _nn(h2, wu_ref[:, sl])
        act_ref[rows, sl] = (gate * _sigmoid(gate) * up).astype(BF16)

    def down(rows):
        return x1_ref[rows, :] + mod_ref[0, 5:6, :] * _nn(act_ref[rows, :], wd_ref[...])

    def final_norm(rows, xo):
        o_ref[rows, :] = xo * lax.rsqrt(jnp.mean(xo * xo, axis=-1, keepdims=True) + EPS) * gf_ref[...]

    mix_a = mix(rows_a)
    mix_b = mix(rows_b)
    h2_a = mid_norm(rows_a, mix_a)
    ff_chunk(rows_a, h2_a, 0)
    h2_b = mid_norm(rows_b, mix_b)
    for j in range(1, n_chunks):
        ff_chunk(rows_a, h2_a, j)
    xo_a = down(rows_a)
    ff_chunk(rows_b, h2_b, 0)
    final_norm(rows_a, xo_a)
    for j in range(1, n_chunks):
        ff_chunk(rows_b, h2_b, j)
    final_norm(rows_b, down(rows_b))


def _out_ffn(x2d, y_ret, y_nsa, mod, w_out, g2, gf, wg, wu, wd, seq):
    n, d = x2d.shape
    tm = TM_FF
    d_ff = wg.shape[1]
    tiles_per_seq = seq // tm
    half_w = y_ret.shape[1]
    row = lambda i: (i, 0)
    resident = lambda a: pl.BlockSpec(a.shape, lambda i: (0, 0), pipeline_mode=pl.Buffered(1))
    return pl.pallas_call(
        _ffn_kernel,
        grid=(n // tm,),
        in_specs=[pl.BlockSpec((tm, d), row),
                  pl.BlockSpec((tm, half_w), row),
                  pl.BlockSpec((tm, half_w), row),
                  pl.BlockSpec((1, 6, d), lambda i: (i // tiles_per_seq, 0, 0)),
                  pl.BlockSpec((1, d), lambda i: (0, 0)),
                  pl.BlockSpec((1, d), lambda i: (0, 0)),
                  resident(w_out), resident(wg), resident(wu), resident(wd)],
        out_specs=pl.BlockSpec((tm, d), row),
        out_shape=jax.ShapeDtypeStruct((n, d), F32),
        scratch_shapes=[pltpu.VMEM((tm, d), F32), pltpu.VMEM((tm, d_ff), BF16)],
        compiler_params=pltpu.CompilerParams(
            dimension_semantics=("parallel",), vmem_limit_bytes=VMEM_LIMIT),
        name="out_ffn",
    )(x2d, y_ret, y_nsa, mod, g2.reshape(1, d), gf.reshape(1, d), w_out, wg, wu, wd)


def kernel(x, c, ln_mix_g, ln_ffn_g, w_ada, b_ada, w_in, cmp_pe_k, cmp_w1_k, cmp_w2_k,
           cmp_pe_v, cmp_w1_v, cmp_w2_v, w_out, w_ff_gate, w_ff_up, w_ff_down, ln_final_g):
    assert w_in.shape[0] == 1, "the final RMSNorm is fused into the (single) layer's FFN kernel"
    b, s, d = x.shape
    lane = np.arange(LANES)
    tabs = _rope_tables(np.arange(s), np.ones(LANES, bool))
    n_piece = s // CMP_STRIDE
    tabs_cmp = _rope_tables(np.arange(n_piece) * CMP_STRIDE + CMP_LEN - 1, lane < NSA_DH)

    mod = _ada(c, w_ada[0], b_ada[0]).reshape(b, 6, d)
    rq, rk, rv, rg_act, nq, kc, vc, ksx, vst, kwx, vwt, gate_t = _in_proj(
        x, ln_mix_g[0], mod, w_in, tabs, _block_onehot_table(s))
    y_ret, (wo_b, wg_b, wu_b, wd_b) = _retention(rq, rk, rv, rg_act, _retention_tables(),
                                                 (w_out, w_ff_gate, w_ff_up, w_ff_down))
    w1, w2, pe = _compress_weights(cmp_w1_k[0], cmp_w2_k[0], cmp_w1_v[0], cmp_w2_v[0],
                                   cmp_pe_k[0], cmp_pe_v[0])
    kvcmp, vct = _compress(kc, vc, w1, w2, pe, tabs_cmp)
    y_nsa = _nsa_attention(nq, kvcmp, vct, ksx, vst, kwx, vwt, gate_t, _overlap_t(s))
    out = _out_ffn(x.reshape(b * s, d), y_ret.reshape(b * s, -1), y_nsa.reshape(b * s, -1), mod,
                   wo_b, ln_ffn_g[0], ln_final_g, wg_b, wu_b, wd_b, s)
    return out.reshape(b, s, d)
```

```python
import numpy as np
import jax
import jax.numpy as jnp
from jax import lax
from jax.experimental import pallas as pl
from jax.experimental.pallas import tpu as pltpu

F32 = jnp.float32
BF16 = jnp.bfloat16

D_MODEL = 1024
RET_HEADS = 4
RET_DK = 64
RET_DV = 128
RET_CHUNK = 128
NSA_HEADS = 8
NSA_GROUPS = 2
NSA_HPG = NSA_HEADS // NSA_GROUPS
NSA_DH = 64
CMP_LEN = 32
CMP_STRIDE = 16
CMP_HIDDEN = 128
SLC_LEN = 64
SLC_TOPK = 16
WIN = 512
D_FF = ((8 * D_MODEL + 3 * 256 - 1) // (3 * 256)) * 256
ROPE_THETA = 10000.0
EPS = 1e-6
NEG = -1e30
FORCE = 1e6

LANES = 128
SUBLANES = 8
HALF = NSA_DH // 2
ONES_ROWS = 16
GATE_ROWS = 3 * NSA_HEADS

TM_IN = 1024
PROJ_W = 512
RET_GROUP = 8
Q_SCALE = NSA_DH ** -0.5 * float(np.log2(np.e))
TQ = 128
TILE_GROUP = 2
TM_FF = 512
TF = 256
VMEM_LIMIT = 56 * 1024 * 1024

_R_RQ, _R_RK, _R_RV, _R_RG, _R_NQ = 0, 256, 512, 1024, 1536
_R_KC, _R_VC, _R_KS, _R_VS, _R_KW, _R_VW, _R_GATE = 2048, 2176, 2304, 2432, 2560, 2688, 2816
IN_COLS_K = -(-(_R_GATE + 3 * NSA_HEADS) // PROJ_W) * PROJ_W


def _sigmoid(x):
    return 1.0 / (1.0 + jnp.exp(-x))


def _nt(a, b):
    return lax.dot_general(a, b, (((1,), (1,)), ((), ())), preferred_element_type=F32)


def _nn(a, b):
    return jnp.dot(a, b, preferred_element_type=F32)


def _rope_tile(a, c, s1, s2):
    return a * c + pltpu.roll(a, HALF, 1) * s1 + pltpu.roll(a, LANES - HALF, 1) * s2


def _swap_halves(a):
    return pltpu.roll(a, LANES // 2, 1)


def _rope_tables(pos, rotary_lanes):
    pos = np.asarray(pos, np.float64)
    lane = np.arange(LANES)
    within = lane % NSA_DH
    freq = ROPE_THETA ** (-(within % HALF).astype(np.float64) / HALF)
    ang = pos[:, None] * freq[None, :]
    cos, sin = np.cos(ang), np.sin(ang)
    first = (within < HALF)[None, :]
    rot = np.asarray(rotary_lanes, bool)[None, :]
    c = np.where(rot, cos, 1.0)
    s1 = np.where(rot & ~first, sin, 0.0)
    s2 = np.where(rot & first, -sin, 0.0)
    return (jnp.asarray(c, F32), jnp.asarray(s1, F32), jnp.asarray(s2, F32))


def _block_onehot_table(seq):
    t = np.zeros((seq, LANES), np.float32)
    pos = np.arange(seq)
    t[pos, NSA_DH + pos // SLC_LEN] = 1.0
    return jnp.asarray(t)


def _retention_tables():
    h = jnp.arange(RET_HEADS, dtype=F32)
    log_g = jnp.log(1.0 - 2.0 ** (-5.0 - h))
    c = RET_CHUNK
    idx = jnp.arange(c, dtype=F32)
    diff = idx[:, None] - idx[None, :]
    causal = diff >= 0
    decay = jnp.where(causal, jnp.exp(log_g[:, None, None] * jnp.where(causal, diff, 0.0)), 0.0)
    zeta = jnp.exp(log_g[:, None] * (c - 1.0 - idx))
    xi = jnp.exp(log_g[:, None] * (idx + 1.0))
    g_chunk = jnp.exp(log_g * c)

    def pair_lanes(t):
        t = t.reshape(RET_HEADS // 2, 2, c)
        return jnp.repeat(jnp.transpose(t, (0, 2, 1)), RET_DK, axis=2)

    g_b = jnp.broadcast_to(g_chunk[:, None, None], (RET_HEADS, 1, LANES))
    return decay, pair_lanes(zeta), pair_lanes(xi), g_b


def _overlap_t(seq):
    n_c = seq // CMP_STRIDE - CMP_LEN // CMP_STRIDE + 1
    nb = seq // SLC_LEN
    cs = np.arange(n_c) * CMP_STRIDE
    bs = np.arange(nb) * SLC_LEN
    ov = np.maximum(np.minimum(cs[:, None] + CMP_LEN, bs[None] + SLC_LEN)
                    - np.maximum(cs[:, None], bs[None]), 0).astype(np.float64) / CMP_LEN
    ncp = seq // CMP_STRIDE
    ovp = np.zeros((ncp, nb))
    ovp[:n_c] = ov
    return jnp.asarray(ovp.T, BF16)


def _ada_kernel(c_ref, w_ref, b_ref, o_ref):
    c = c_ref[...]
    o_ref[...] = _nn(c * _sigmoid(c), w_ref[...]) + b_ref[...]


def _ada(c, w, b):
    bsz, d = c.shape
    n = w.shape[1]
    tn = 1024
    return pl.pallas_call(
        _ada_kernel,
        grid=(n // tn,),
        in_specs=[pl.BlockSpec((bsz, d), lambda j: (0, 0)),
                  pl.BlockSpec((d, tn), lambda j: (0, j)),
                  pl.BlockSpec((1, tn), lambda j: (0, j))],
        out_specs=pl.BlockSpec((bsz, tn), lambda j: (0, j)),
        out_shape=jax.ShapeDtypeStruct((bsz, n), F32),
        compiler_params=pltpu.CompilerParams(vmem_limit_bytes=VMEM_LIMIT),
        name="ada",
    )(c, w, b.reshape(1, n))


def _inproj_kernel(x_ref, g_ref, mod_ref, w_ref, c_ref, s1_ref, s2_ref, hot_ref,
                   rq_ref, rk_ref, rv_ref, rg_ref, nq_ref, kc_ref, vc_ref,
                   ksx_ref, vst_ref, kwx_ref, vwt_ref, gate_ref, wb_ref):
    @pl.when((pl.program_id(0) == 0) & (pl.program_id(1) == 0))
    def _():
        n_in = w_ref.shape[2]
        whole = (n_in // PROJ_W) * PROJ_W
        for c0 in range(0, whole, PROJ_W):
            wb_ref[:, c0:c0 + PROJ_W] = w_ref[0, :, c0:c0 + PROJ_W].astype(BF16)
        wb_ref[:, whole:] = jnp.zeros((wb_ref.shape[0], wb_ref.shape[1] - whole), BF16)
        wb_ref[:, whole:n_in] = w_ref[0, :, whole:n_in].astype(BF16)

    def modulated_norm(rows):
        x = x_ref[0, rows, :]
        y = x * lax.rsqrt(jnp.mean(x * x, axis=-1, keepdims=True) + EPS) * g_ref[...]
        return (y * (1.0 + mod_ref[0, 1:2, :]) + mod_ref[0, 0:1, :]).astype(BF16)

    def product_steps(rows, hb):
        tabs = (c_ref[rows, :], s1_ref[rows, :], s2_ref[rows, :])
        low = lax.broadcasted_iota(jnp.int32, (rows.stop - rows.start, LANES), 1) < NSA_DH

        def proj(c0):
            a = _nn(hb, wb_ref[:, c0:c0 + PROJ_W])
            return [a[:, t * LANES:(t + 1) * LANES] for t in range(PROJ_W // LANES)]

        def put(out_ref, t, value):
            out_ref[0, rows, t * LANES:(t + 1) * LANES] = value.astype(out_ref.dtype)

        def roped(tiles, scale, out_ref):
            for t, a in enumerate(tiles):
                r = _rope_tile(a, *tabs)
                put(out_ref, t, r if scale == 1.0 else r * scale)

        def per_group(tile, fill, out_ref):
            put(out_ref, 0, jnp.where(low, tile, fill))
            put(out_ref, 1, jnp.where(low, _swap_halves(tile), fill))

        def per_group_t(tile, out_ref):
            t = tile.T
            ones = jnp.ones((ONES_ROWS, t.shape[1]), out_ref.dtype)
            for g in range(NSA_GROUPS):
                out_ref[0, g, 0:NSA_DH, rows] = t[g * NSA_DH:(g + 1) * NSA_DH].astype(out_ref.dtype)
                out_ref[0, g, NSA_DH:NSA_DH + ONES_ROWS, rows] = ones

        def retention_qk():
            tiles = proj(_R_RQ)
            roped(tiles[0:2], 1.0, rq_ref)
            roped(tiles[2:4], RET_DK ** -0.5, rk_ref)

        def retention_v():
            for t, a in enumerate(proj(_R_RV)):
                put(rv_ref, t, a)

        def retention_gate():
            for t, a in enumerate(proj(_R_RG)):
                put(rg_ref, t, a * _sigmoid(a))

        def nsa_q():
            roped(proj(_R_NQ), Q_SCALE, nq_ref)

        def nsa_compress_selected():
            kc_t, vc_t, ks_t, vs_t = proj(_R_KC)
            kc_ref[0, rows, :] = kc_t
            vc_ref[0, rows, :] = vc_t
            per_group(_rope_tile(ks_t, *tabs), hot_ref[rows, :], ksx_ref)
            per_group_t(vs_t, vst_ref)

        def nsa_window_gates():
            kw_t, vw_t, gates_t, _ = proj(_R_KW)
            per_group(_rope_tile(kw_t, *tabs), 0.0, kwx_ref)
            per_group_t(vw_t, vwt_ref)
            gate_ref[0, :, rows] = _sigmoid(gates_t.T[0:GATE_ROWS])

        return [retention_qk, retention_v, retention_gate, nsa_q, nsa_compress_selected, nsa_window_gates]

    tm = x_ref.shape[1]
    rows_a, rows_b = slice(0, tm // 2), slice(tm // 2, tm)
    steps_a = product_steps(rows_a, modulated_norm(rows_a))
    steps_a[0]()
    steps_b = product_steps(rows_b, modulated_norm(rows_b))
    for step in steps_a[1:] + steps_b:
        step()


def _in_proj(x, ln_g, mod, w_in, tabs, hot):
    b, s, d = x.shape
    tm = TM_IN
    grid = (b, s // tm)
    tab_spec = pl.BlockSpec((tm, LANES), lambda bi, j: (j, 0))

    def out(n, dtype):
        return (jax.ShapeDtypeStruct((b, s, n), dtype), pl.BlockSpec((1, tm, n), lambda bi, j: (bi, j, 0)))

    def out_t(rows, dtype):
        return (jax.ShapeDtypeStruct((b, NSA_GROUPS, rows, s), dtype),
                pl.BlockSpec((1, NSA_GROUPS, rows, tm), lambda bi, j: (bi, 0, 0, j)))

    vt_rows = NSA_DH + ONES_ROWS
    outs = [out(256, BF16), out(256, BF16), out(512, BF16), out(512, BF16), out(512, BF16),
            out(LANES, F32), out(LANES, F32),
            out(256, BF16), out_t(vt_rows, BF16), out(256, BF16), out_t(vt_rows, BF16),
            (jax.ShapeDtypeStruct((b, GATE_ROWS, s), F32),
             pl.BlockSpec((1, GATE_ROWS, tm), lambda bi, j: (bi, 0, j)))]
    return pl.pallas_call(
        _inproj_kernel,
        grid=grid,
        in_specs=[pl.BlockSpec((1, tm, d), lambda bi, j: (bi, j, 0)),
                  pl.BlockSpec((1, d), lambda bi, j: (0, 0)),
                  pl.BlockSpec((1, 6, d), lambda bi, j: (bi, 0, 0)),
                  pl.BlockSpec((1,) + w_in.shape[1:], lambda bi, j: (0, 0, 0), pipeline_mode=pl.Buffered(1)),
                  tab_spec, tab_spec, tab_spec, tab_spec],
        out_specs=[o[1] for o in outs],
        out_shape=[o[0] for o in outs],
        scratch_shapes=[pltpu.VMEM((d, IN_COLS_K), BF16)],
        compiler_params=pltpu.CompilerParams(
            dimension_semantics=("arbitrary", "arbitrary"), vmem_limit_bytes=VMEM_LIMIT),
        name="in_proj",
    )(x, ln_g.reshape(1, d), mod, w_in, *tabs, hot)


def _ret_kernel(q_ref, k_ref, v_ref, rg_ref, dec_ref, zeta_ref, xi_ref, gch_ref, o_ref, kv_ref, prev_ref):
    c = RET_CHUNK
    n_chunks = q_ref.shape[1] // c
    low = lax.broadcasted_iota(jnp.int32, (c, LANES), 1) < RET_DK

    def chunk_rows(n):
        return pl.ds(pl.multiple_of(n * c, c), c)

    def head_cols(h):
        return slice(h * RET_DV, (h + 1) * RET_DV)

    def kv_body(it, carry):
        kz_t = {}
        for j in range(RET_GROUP):
            rows = chunk_rows(it * RET_GROUP + j)
            for p in range(RET_HEADS // 2):
                pair = slice(p * LANES, (p + 1) * LANES)
                kz_t[j, p] = (k_ref[0, rows, pair].astype(F32) * zeta_ref[p]).T.astype(BF16)
        for j in range(RET_GROUP):
            n = it * RET_GROUP + j
            for h in range(RET_HEADS):
                kv_ref[h, n] = _nn(kz_t[j, h // 2], v_ref[0, chunk_rows(n), head_cols(h)])
        return carry

    lax.fori_loop(0, n_chunks // RET_GROUP, kv_body, 0)

    for h in range(RET_HEADS):
        def scan_body(n, st, h=h):
            prev_ref[h, n] = st.astype(prev_ref.dtype)
            return st * gch_ref[h] + kv_ref[h, n]
        lax.fori_loop(0, n_chunks, scan_body, jnp.zeros((LANES, RET_DV), F32))

    def out_body(it, carry):
        chains = [(j, h) for j in range(RET_GROUP) for h in range(RET_HEADS)]
        chunk = lambda j: it * RET_GROUP + j
        q_own, qx_own, att, ys = {}, {}, {}, {}
        for j, h in chains:
            p, e = divmod(h, 2)
            pair = slice(p * LANES, (p + 1) * LANES)
            mine = low if e == 0 else jnp.logical_not(low)
            q2 = q_ref[0, chunk_rows(chunk(j)), pair].astype(F32)
            q_own[j, h] = jnp.where(mine, q2, 0.0).astype(BF16)
            qx_own[j, h] = jnp.where(mine, q2 * xi_ref[p], 0.0).astype(BF16)
        for j, h in chains:
            pair = slice((h // 2) * LANES, (h // 2 + 1) * LANES)
            att[j, h] = _nt(q_own[j, h], k_ref[0, chunk_rows(chunk(j)), pair])
        for j, h in chains:
            lhs = jnp.concatenate([(att[j, h] * dec_ref[h]).astype(BF16), qx_own[j, h]], axis=1)
            rhs = jnp.concatenate([v_ref[0, chunk_rows(chunk(j)), head_cols(h)], prev_ref[h, chunk(j)]], axis=0)
            ys[j, h] = _nn(lhs, rhs)
        for j, h in chains:
            y = ys[j, h]
            yn = y * lax.rsqrt(jnp.mean(y * y, axis=-1, keepdims=True) + EPS)
            gate = rg_ref[0, chunk_rows(chunk(j)), head_cols(h)].astype(F32)
            o_ref[0, chunk_rows(chunk(j)), head_cols(h)] = (yn * gate).astype(o_ref.dtype)
        return carry

    lax.fori_loop(0, n_chunks // RET_GROUP, out_body, 0)


def _retention(rq, rk, rv, rg_act, tables):
    b, s, _ = rq.shape
    decay, zeta_p, xi_p, g_b = tables
    whole = lambda a: pl.BlockSpec(a.shape, lambda bi: (0,) * a.ndim)
    row = lambda a: pl.BlockSpec((1,) + a.shape[1:], lambda bi: (bi, 0, 0))
    return pl.pallas_call(
        _ret_kernel,
        grid=(b,),
        in_specs=[row(rq), row(rk), row(rv), row(rg_act),
                  whole(decay), whole(zeta_p), whole(xi_p), whole(g_b)],
        out_specs=row(rv),
        out_shape=jax.ShapeDtypeStruct(rv.shape, BF16),
        scratch_shapes=[pltpu.VMEM((RET_HEADS, s // RET_CHUNK, LANES, RET_DV), F32),
                        pltpu.VMEM((RET_HEADS, s // RET_CHUNK, LANES, RET_DV), BF16)],
        compiler_params=pltpu.CompilerParams(
            dimension_semantics=("parallel",), vmem_limit_bytes=VMEM_LIMIT),
        name="retention",
    )(rq, rk, rv, rg_act, decay, zeta_p, xi_p, g_b)


def _cmp_kernel(kc_ref, vc_ref, w1_ref, pe_ref, w2_ref, c_ref, s1_ref, s2_ref, kv_ref, vt_ref):
    n_piece = kc_ref.shape[1] // CMP_STRIDE
    halves = CMP_LEN // CMP_STRIDE
    lhs = {}
    for t, src in enumerate((kc_ref, vc_ref)):
        rows = [src[0, pl.ds(r, n_piece, stride=CMP_STRIDE), :] for r in range(CMP_STRIDE)]
        for half in range(halves):
            pe0 = half * CMP_STRIDE
            lhs[t, half] = jnp.concatenate(
                [(rows[r] + pe_ref[t, pe0 + r:pe0 + r + 1, :]).astype(BF16) for r in range(CMP_STRIDE)], axis=1)
    part = {key: _nn(lhs[key], w1_ref[key[0], key[1]]) for key in lhs}
    out = jnp.zeros((n_piece, kv_ref.shape[2]), F32)
    for t in range(2):
        hid = part[t, 0]
        for half in range(1, halves):
            hid = hid + pltpu.roll(part[t, half], n_piece - half, 0)
        out = out + _nn((hid * _sigmoid(hid)).astype(BF16), w2_ref[t])
    for t in range(out.shape[1] // LANES):
        sl = slice(t * LANES, (t + 1) * LANES)
        kv = _rope_tile(out[:, sl], c_ref[...], s1_ref[...], s2_ref[...])
        kv_ref[0, :, sl] = kv.astype(kv_ref.dtype)
        vt_ref[0, t] = kv.T[NSA_DH:2 * NSA_DH].astype(vt_ref.dtype)


def _compress_weights(w1_k, w2_k, w1_v, w2_v, pe_k, pe_v):
    dh, hid = NSA_DH, CMP_HIDDEN

    def first(w1):
        w = w1.reshape(CMP_LEN, dh, hid)
        z = jnp.zeros_like(w)
        return jnp.concatenate([jnp.concatenate([w, z], axis=2), jnp.concatenate([z, w], axis=2)], axis=1)

    def second(w2, off):
        z = jnp.zeros_like(w2)
        rows = []
        for g in range(NSA_GROUPS):
            c = [z, z, z, z]
            c[2 * g + off] = w2
            rows.append(jnp.concatenate(c, axis=1))
        return jnp.concatenate(rows, axis=0)

    w1 = jnp.stack([first(w1_k), first(w1_v)]).astype(BF16).reshape(
        2, CMP_LEN // CMP_STRIDE, CMP_STRIDE * NSA_GROUPS * dh, NSA_GROUPS * hid)
    w2 = jnp.stack([second(w2_k, 0), second(w2_v, 1)]).astype(BF16)
    pe = jnp.stack([jnp.concatenate([pe_k, pe_k], axis=1), jnp.concatenate([pe_v, pe_v], axis=1)])
    return w1, w2, pe


def _compress(kc, vc, w1, w2, pe, tabs_cmp):
    b, s, w = kc.shape
    n_piece = s // CMP_STRIDE
    const2 = lambda bi: (0, 0)
    src = pl.BlockSpec((1, s, w), lambda bi: (bi, 0, 0))
    dst = pl.BlockSpec((1, n_piece, 2 * w), lambda bi: (bi, 0, 0))
    shape = jax.ShapeDtypeStruct((b, n_piece, 2 * w), BF16)
    return pl.pallas_call(
        _cmp_kernel,
        grid=(b,),
        in_specs=[src, src,
                  pl.BlockSpec(w1.shape, lambda bi: (0, 0, 0, 0)),
                  pl.BlockSpec(pe.shape, lambda bi: (0, 0, 0)),
                  pl.BlockSpec(w2.shape, lambda bi: (0, 0, 0)),
                  pl.BlockSpec((n_piece, LANES), const2),
                  pl.BlockSpec((n_piece, LANES), const2),
                  pl.BlockSpec((n_piece, LANES), const2)],
        out_specs=[dst, pl.BlockSpec((1, NSA_GROUPS, NSA_DH, n_piece), lambda bi: (bi, 0, 0, 0))],
        out_shape=[shape, jax.ShapeDtypeStruct((b, NSA_GROUPS, NSA_DH, n_piece), BF16)],
        compiler_params=pltpu.CompilerParams(
            dimension_semantics=("parallel",), vmem_limit_bytes=VMEM_LIMIT),
        name="compress",
    )(kc, vc, w1, pe, w2, *tabs_cmp)


def _nsa_kernel(q_ref, kcmp_ref, vct_ref, ksx_ref, vst_ref, kwx_ref, vwt_ref, gate_ref, ovt_ref, *rest):
    n_w = (len(rest) - 1) // 2
    o_ref = rest[n_w]
    for w_src, w_dst in zip(rest[:n_w], rest[n_w + 1:]):
        w_dst[...] = w_src[...].astype(w_dst.dtype)

    tq = TQ
    seq = q_ref.shape[1]
    nb = seq // SLC_LEN
    n_cmp = kcmp_ref.shape[1]
    hpg, dh = NSA_HPG, NSA_DH
    assert WIN % tq == 0 and seq % tq == 0
    group = pl.program_id(1)

    low = lax.broadcasted_iota(jnp.int32, (tq, LANES), 1) < dh
    eye = jnp.where(lax.broadcasted_iota(jnp.int32, (tq, tq), 0)
                    == lax.broadcasted_iota(jnp.int32, (tq, tq), 1), 1.0, 0.0).astype(BF16)
    blk = lax.broadcasted_iota(jnp.int32, (nb, tq), 0)
    col = lax.broadcasted_iota(jnp.int32, (nb, tq), 1)
    crow = lax.broadcasted_iota(jnp.int32, (n_cmp, tq), 0)
    ccol = lax.broadcasted_iota(jnp.int32, (n_cmp, tq), 1)
    kcm = kcmp_ref[0]
    vct = vct_ref[0, 0]
    ovt = ovt_ref[...]

    def per_head(x):
        return jnp.concatenate([x] * hpg, axis=1)

    key_off = lax.broadcasted_iota(jnp.int32, (tq, tq), 0)
    qry_off = lax.broadcasted_iota(jnp.int32, (tq, tq), 1)
    not_after = per_head(jnp.where(key_off <= qry_off, 0.0, NEG))
    inside_win = per_head(jnp.where(key_off > qry_off, 0.0, NEG))

    def masked(s, first_key, t0, windowed):
        blocks = []
        for r in range(0, s.shape[0], tq):
            blk_s = s[r:r + tq]
            if first_key + r == t0:
                blk_s = blk_s + not_after
            elif windowed and first_key + r == t0 - WIN:
                blk_s = blk_s + inside_win
            blocks.append(blk_s)
        return jnp.concatenate(blocks, axis=0)

    def normalise(acc):
        return acc[0:dh] / acc[dh:dh + 1]

    def select_blocks(psum, t0):
        tcol = col + t0
        bcausal = blk * SLC_LEN <= tcol
        n_live = (t0 + tq - 1) // SLC_LEN + 1
        top_n = min(SLC_TOPK, nb)
        if n_live <= top_n:
            bias = jnp.where(bcausal, 0.0, NEG)
        else:
            p_hi = psum.astype(BF16)
            p_lo = (psum - p_hi.astype(F32)).astype(BF16)
            imp = _nn(ovt, p_hi) + _nn(ovt, p_lo)
            cur = tcol // SLC_LEN
            forced = (blk == 0) | (blk == cur) | (blk == cur - 1)
            imp = jnp.where(bcausal, jnp.where(forced, FORCE, imp), NEG)
            rank = jnp.zeros((nb, tq), F32)
            for j in range(n_live):
                r = imp[j:j + 1, :]
                rank = rank + jnp.where(blk > j, jnp.where(r >= imp, 1.0, 0.0), jnp.where(r > imp, 1.0, 0.0))
            bias = jnp.where((rank < float(top_n)) & bcausal, 0.0, NEG)
        feat = jnp.concatenate([jnp.zeros((dh, tq), F32), bias,
                                jnp.zeros((LANES - dh - nb, tq), F32)], axis=0).astype(BF16)
        return _nt(eye, feat)

    class TileGroup:
        def __init__(self, tiles):
            self.tiles = tiles
            self.wstart = {t0: max(t0 - WIN, 0) for t0 in tiles}
            self.wkeys = {t0: slice(self.wstart[t0], t0 + tq) for t0 in tiles}
            self.skeys = {t0: slice(0, t0 + tq) for t0 in tiles}

        def scores_window_compressed(self):
            self.heads, self.qs = {}, {}
            for t0 in self.tiles:
                qf = q_ref[0, t0:t0 + tq, :].astype(F32)
                hl = []
                for hh in range(hpg):
                    t = qf[:, (hh // 2) * LANES:(hh // 2 + 1) * LANES]
                    if hh % 2 == 1:
                        t = _swap_halves(t)
                    hl.append(jnp.where(low, t, 0.0))
                self.heads[t0] = hl
                self.qs[t0] = jnp.concatenate(hl, axis=0).astype(BF16)
            self.s_w = {t0: _nt(kwx_ref[0, self.wkeys[t0], :], self.qs[t0]) for t0 in self.tiles}
            self.s_c = {t0: _nt(kcm, self.qs[t0]) for t0 in self.tiles}

        def select_and_scores_selected(self):
            self.p_cmp, qsel = {}, {}
            for t0 in self.tiles:
                cmask = (crow * CMP_STRIDE + (CMP_LEN - 1)) <= (ccol + t0)
                p_all = []
                psum = jnp.zeros((n_cmp, tq), F32)
                for hh in range(hpg):
                    sh = jnp.where(cmask, self.s_c[t0][:, hh * tq:(hh + 1) * tq], NEG)
                    e = jnp.exp2(sh - jnp.max(sh, axis=0, keepdims=True))
                    p = jnp.where(cmask, e / jnp.sum(e, axis=0, keepdims=True), 0.0)
                    psum = psum + p
                    p_all.append(p.astype(BF16))
                self.p_cmp[t0] = jnp.concatenate(p_all, axis=1)
                qbias = select_blocks(psum, t0)
                qsel[t0] = jnp.concatenate([hd + qbias for hd in self.heads[t0]], axis=0).astype(BF16)
            self.s_s = {t0: _nt(ksx_ref[0, self.skeys[t0], :], qsel[t0]) for t0 in self.tiles}

        def outputs_window_compressed(self):
            e_w = {}
            for t0 in self.tiles:
                sw = masked(self.s_w[t0], self.wstart[t0], t0, True)
                e_w[t0] = jnp.exp2(sw - jnp.max(sw, axis=0, keepdims=True)).astype(BF16)
            self.o_win = {t0: normalise(_nn(vwt_ref[0, 0, :, self.wkeys[t0]], e_w[t0])) for t0 in self.tiles}
            self.o_cmp = {t0: _nn(vct, self.p_cmp[t0]) for t0 in self.tiles}

        def outputs_selected_and_store(self):
            e_s = {}
            for t0 in self.tiles:
                ss = masked(self.s_s[t0], 0, t0, False)
                e_s[t0] = jnp.exp2(ss - jnp.max(ss, axis=0, keepdims=True)).astype(BF16)
            o_sel = {t0: normalise(_nn(vst_ref[0, 0, :, self.skeys[t0]], e_s[t0])) for t0 in self.tiles}
            for t0 in self.tiles:
                gt = gate_ref[0, :, t0:t0 + tq]
                outs = []
                for hh in range(hpg):
                    hc = slice(hh * tq, (hh + 1) * tq)

                    def gate_row(branch):
                        by_group = [gt[branch * NSA_HEADS + g * hpg + hh:branch * NSA_HEADS + g * hpg + hh + 1, :]
                                    for g in range(NSA_GROUPS)]
                        row = by_group[-1]
                        for g in range(NSA_GROUPS - 2, -1, -1):
                            row = jnp.where(group == g, by_group[g], row)
                        return row

                    outs.append(gate_row(0) * self.o_cmp[t0][:, hc] + gate_row(1) * o_sel[t0][:, hc]
                                + gate_row(2) * self.o_win[t0][:, hc])
                o_ref[0, t0:t0 + tq, :] = jnp.concatenate(outs, axis=0).T.astype(o_ref.dtype)

    starts = list(range(0, seq, tq))
    groups = [TileGroup(starts[i:i + TILE_GROUP]) for i in range(0, len(starts), TILE_GROUP)]
    n = len(groups)
    groups[0].scores_window_compressed()
    for i, g in enumerate(groups):
        if i + 1 < n:
            groups[i + 1].scores_window_compressed()
        g.select_and_scores_selected()
        g.outputs_window_compressed()
        if i > 0:
            groups[i - 1].outputs_selected_and_store()
    groups[n - 1].outputs_selected_and_store()


def _nsa_attention(nq, kvcmp, vct, ksx, vst, kwx, vwt, gate_t, ovt, weights):
    b, s, _ = nq.shape
    n_cmp = kvcmp.shape[1]
    gw = NSA_HPG * NSA_DH
    steps = b * NSA_GROUPS
    per_group = lambda rows, width: pl.BlockSpec((1, rows, width), lambda bi, g: (bi, 0, g))
    per_group_t = lambda a: pl.BlockSpec((1, 1) + a.shape[2:], lambda bi, g: (bi, g, 0, 0))
    assert all(w.shape[1] % (steps * 16) == 0 for w in weights)
    slab = lambda w: pl.BlockSpec((1, w.shape[1] // steps, w.shape[2]),
                                  lambda bi, g: (0, bi * NSA_GROUPS + g, 0))
    outs = pl.pallas_call(
        _nsa_kernel,
        grid=(b, NSA_GROUPS),
        in_specs=[per_group(s, gw),
                  per_group(n_cmp, LANES), per_group_t(vct),
                  per_group(s, LANES), per_group_t(vst), per_group(s, LANES), per_group_t(vwt),
                  pl.BlockSpec((1,) + gate_t.shape[1:], lambda bi, g: (bi, 0, 0)),
                  pl.BlockSpec(ovt.shape, lambda bi, g: (0, 0))] + [slab(w) for w in weights],
        out_specs=[per_group(s, gw)] + [slab(w) for w in weights],
        out_shape=[jax.ShapeDtypeStruct((b, s, NSA_HEADS * NSA_DH), BF16)]
                  + [jax.ShapeDtypeStruct(w.shape, BF16) for w in weights],
        compiler_params=pltpu.CompilerParams(
            dimension_semantics=("parallel", "parallel"), vmem_limit_bytes=VMEM_LIMIT),
        name="nsa_attn",
    )(nq, kvcmp, vct, ksx, vst, kwx, vwt, gate_t, ovt, *weights)
    return outs[0], [w[0] for w in outs[1:]]


def _ffn_kernel(x_ref, yr_ref, yn_ref, mod_ref, g2_ref, gf_ref, wo_ref, wg_ref, wu_ref, wd_ref,
                o_ref, x1_ref, act_ref):
    half_w = yr_ref.shape[1]
    d_ff = wg_ref.shape[1]
    n_chunks = d_ff // TF
    rows_a = slice(0, x_ref.shape[0] // 2)
    rows_b = slice(x_ref.shape[0] // 2, x_ref.shape[0])

    def mix(rows):
        return _nn(yr_ref[rows, :], wo_ref[0:half_w, :]) + _nn(yn_ref[rows, :], wo_ref[half_w:2 * half_w, :])

    def mid_norm(rows, mixed):
        x1 = x_ref[rows, :] + mod_ref[0, 2:3, :] * mixed
        x1_ref[rows, :] = x1
        y = x1 * lax.rsqrt(jnp.mean(x1 * x1, axis=-1, keepdims=True) + EPS) * g2_ref[...]
        return (y * (1.0 + mod_ref[0, 4:5, :]) + mod_ref[0, 3:4, :]).astype(BF16)

    def ff_chunk(rows, h2, j):
        sl = slice(j * TF, (j + 1) * TF)
        gate = _nn(h2, wg_ref[:, sl])
        up = _nn(h2, wu_ref[:, sl])
        act_ref[rows, sl] = (gate * _sigmoid(gate) * up).astype(BF16)

    def down(rows):
        return x1_ref[rows, :] + mod_ref[0, 5:6, :] * _nn(act_ref[rows, :], wd_ref[...])

    def final_norm(rows, xo):
        o_ref[rows, :] = xo * lax.rsqrt(jnp.mean(xo * xo, axis=-1, keepdims=True) + EPS) * gf_ref[...]

    mix_a = mix(rows_a)
    mix_b = mix(rows_b)
    h2_a = mid_norm(rows_a, mix_a)
    ff_chunk(rows_a, h2_a, 0)
    h2_b = mid_norm(rows_b, mix_b)
    for j in range(1, n_chunks):
        ff_chunk(rows_a, h2_a, j)
    xo_a = down(rows_a)
    ff_chunk(rows_b, h2_b, 0)
    final_norm(rows_a, xo_a)
    for j in range(1, n_chunks):
        ff_chunk(rows_b, h2_b, j)
    final_norm(rows_b, down(rows_b))


def _out_ffn(x2d, y_ret, y_nsa, mod, w_out, g2, gf, wg, wu, wd, seq):
    n, d = x2d.shape
    tm = TM_FF
    d_ff = wg.shape[1]
    tiles_per_seq = seq // tm
    half_w = y_ret.shape[1]
    row = lambda i: (i, 0)
    resident = lambda a: pl.BlockSpec(a.shape, lambda i: (0, 0), pipeline_mode=pl.Buffered(1))
    return pl.pallas_call(
        _ffn_kernel,
        grid=(n // tm,),
        in_specs=[pl.BlockSpec((tm, d), row),
                  pl.BlockSpec((tm, half_w), row),
                  pl.BlockSpec((tm, half_w), row),
                  pl.BlockSpec((1, 6, d), lambda i: (i // tiles_per_seq, 0, 0)),
                  pl.BlockSpec((1, d), lambda i: (0, 0)),
                  pl.BlockSpec((1, d), lambda i: (0, 0)),
                  resident(w_out), resident(wg), resident(wu), resident(wd)],
        out_specs=pl.BlockSpec((tm, d), row),
        out_shape=jax.ShapeDtypeStruct((n, d), F32),
        scratch_shapes=[pltpu.VMEM((tm, d), F32), pltpu.VMEM((tm, d_ff), BF16)],
        compiler_params=pltpu.CompilerParams(
            dimension_semantics=("parallel",), vmem_limit_bytes=VMEM_LIMIT),
        name="out_ffn",
    )(x2d, y_ret, y_nsa, mod, g2.reshape(1, d), gf.reshape(1, d), w_out, wg, wu, wd)


def kernel(x, c, ln_mix_g, ln_ffn_g, w_ada, b_ada, w_in, cmp_pe_k, cmp_w1_k, cmp_w2_k,
           cmp_pe_v, cmp_w1_v, cmp_w2_v, w_out, w_ff_gate, w_ff_up, w_ff_down, ln_final_g):
    assert w_in.shape[0] == 1, "the final RMSNorm is fused into the (single) layer's FFN kernel"
    b, s, d = x.shape
    lane = np.arange(LANES)
    tabs = _rope_tables(np.arange(s), np.ones(LANES, bool))
    n_piece = s // CMP_STRIDE
    tabs_cmp = _rope_tables(np.arange(n_piece) * CMP_STRIDE + CMP_LEN - 1, lane < NSA_DH)

    mod = _ada(c, w_ada[0], b_ada[0]).reshape(b, 6, d)
    rq, rk, rv, rg_act, nq, kc, vc, ksx, vst, kwx, vwt, gate_t = _in_proj(
        x, ln_mix_g[0], mod, w_in, tabs, _block_onehot_table(s))
    y_ret = _retention(rq, rk, rv, rg_act, _retention_tables())
    w1, w2, pe = _compress_weights(cmp_w1_k[0], cmp_w2_k[0], cmp_w1_v[0], cmp_w2_v[0],
                                   cmp_pe_k[0], cmp_pe_v[0])
    kvcmp, vct = _compress(kc, vc, w1, w2, pe, tabs_cmp)
    y_nsa, (wo_b, wg_b, wu_b, wd_b) = _nsa_attention(
        nq, kvcmp, vct, ksx, vst, kwx, vwt, gate_t, _overlap_t(s), (w_out, w_ff_gate, w_ff_up, w_ff_down))
    out = _out_ffn(x.reshape(b * s, d), y_ret.reshape(b * s, -1), y_nsa.reshape(b * s, -1), mod,
                   wo_b, ln_ffn_g[0], ln_final_g, wg_b, wu_b, wd_b, s)
    return out.reshape(b, s, d)
```

```python
import numpy as np
import jax
import jax.numpy as jnp
from jax import lax
from jax.experimental import pallas as pl
from jax.experimental.pallas import tpu as pltpu

F32 = jnp.float32
BF16 = jnp.bfloat16

D_MODEL = 1024
RET_HEADS = 4
RET_DK = 64
RET_DV = 128
RET_CHUNK = 128
NSA_HEADS = 8
NSA_GROUPS = 2
NSA_HPG = NSA_HEADS // NSA_GROUPS
NSA_DH = 64
CMP_LEN = 32
CMP_STRIDE = 16
CMP_HIDDEN = 128
SLC_LEN = 64
SLC_TOPK = 16
WIN = 512
D_FF = ((8 * D_MODEL + 3 * 256 - 1) // (3 * 256)) * 256
ROPE_THETA = 10000.0
EPS = 1e-6
NEG = -1e30
FORCE = 1e6

LANES = 128
SUBLANES = 8
HALF = NSA_DH // 2
ONES_ROWS = 16
GATE_ROWS = 3 * NSA_HEADS

TM_IN = 1024
PROJ_W = 512
RET_GROUP = 16
Q_SCALE = NSA_DH ** -0.5 * float(np.log2(np.e))
TQ = 128
TILE_GROUP = 2
TM_FF = 512
TF = 256
VMEM_LIMIT = 56 * 1024 * 1024

_R_RQ, _R_RK, _R_RV, _R_RG, _R_NQ = 0, 256, 512, 1024, 1536
_R_KC, _R_VC, _R_KS, _R_VS, _R_KW, _R_VW, _R_GATE = 2048, 2176, 2304, 2432, 2560, 2688, 2816
IN_COLS_K = -(-(_R_GATE + 3 * NSA_HEADS) // PROJ_W) * PROJ_W


def _sigmoid(x):
    return 1.0 / (1.0 + jnp.exp(-x))


def _nt(a, b):
    return lax.dot_general(a, b, (((1,), (1,)), ((), ())), preferred_element_type=F32)


def _nn(a, b):
    return jnp.dot(a, b, preferred_element_type=F32)


def _rope_tile(a, c, s1, s2):
    return a * c + pltpu.roll(a, HALF, 1) * s1 + pltpu.roll(a, LANES - HALF, 1) * s2


def _swap_halves(a):
    return pltpu.roll(a, LANES // 2, 1)


def _rope_tables(pos, rotary_lanes):
    pos = np.asarray(pos, np.float64)
    lane = np.arange(LANES)
    within = lane % NSA_DH
    freq = ROPE_THETA ** (-(within % HALF).astype(np.float64) / HALF)
    ang = pos[:, None] * freq[None, :]
    cos, sin = np.cos(ang), np.sin(ang)
    first = (within < HALF)[None, :]
    rot = np.asarray(rotary_lanes, bool)[None, :]
    c = np.where(rot, cos, 1.0)
    s1 = np.where(rot & ~first, sin, 0.0)
    s2 = np.where(rot & first, -sin, 0.0)
    return (jnp.asarray(c, F32), jnp.asarray(s1, F32), jnp.asarray(s2, F32))


def _block_onehot_table(seq):
    t = np.zeros((seq, LANES), np.float32)
    pos = np.arange(seq)
    t[pos, NSA_DH + pos // SLC_LEN] = 1.0
    return jnp.asarray(t)


def _retention_tables():
    h = np.arange(RET_HEADS, dtype=np.float64)
    log_g = np.log(1.0 - 2.0 ** (-5.0 - h))
    c = RET_CHUNK
    idx = np.arange(c, dtype=np.float64)
    diff = idx[:, None] - idx[None, :]
    causal = diff >= 0
    decay = np.where(causal, np.exp(log_g[:, None, None] * np.where(causal, diff, 0.0)), 0.0)
    zeta = np.exp(log_g[:, None] * (c - 1.0 - idx))
    xi = np.exp(log_g[:, None] * (idx + 1.0))
    g_chunk = np.exp(log_g * c)

    def pair_lanes(t):
        t = t.reshape(RET_HEADS // 2, 2, c)
        return np.repeat(np.transpose(t, (0, 2, 1)), RET_DK, axis=2)

    g_b = np.broadcast_to(g_chunk[:, None, None], (RET_HEADS, 1, LANES))
    return tuple(jnp.asarray(t, F32) for t in (decay, pair_lanes(zeta), pair_lanes(xi), g_b))


def _overlap_t(seq):
    n_c = seq // CMP_STRIDE - CMP_LEN // CMP_STRIDE + 1
    nb = seq // SLC_LEN
    cs = np.arange(n_c) * CMP_STRIDE
    bs = np.arange(nb) * SLC_LEN
    ov = np.maximum(np.minimum(cs[:, None] + CMP_LEN, bs[None] + SLC_LEN)
                    - np.maximum(cs[:, None], bs[None]), 0).astype(np.float64) / CMP_LEN
    ncp = seq // CMP_STRIDE
    ovp = np.zeros((ncp, nb))
    ovp[:n_c] = ov
    return jnp.asarray(ovp.T, BF16)


def _ada_kernel(c_ref, w_ref, b_ref, o_ref):
    c = c_ref[...]
    o_ref[...] = _nn(c * _sigmoid(c), w_ref[...]) + b_ref[...]


def _ada(c, w, b):
    bsz, d = c.shape
    n = w.shape[1]
    tn = 1024
    return pl.pallas_call(
        _ada_kernel,
        grid=(n // tn,),
        in_specs=[pl.BlockSpec((bsz, d), lambda j: (0, 0)),
                  pl.BlockSpec((d, tn), lambda j: (0, j)),
                  pl.BlockSpec((1, tn), lambda j: (0, j))],
        out_specs=pl.BlockSpec((bsz, tn), lambda j: (0, j)),
        out_shape=jax.ShapeDtypeStruct((bsz, n), F32),
        compiler_params=pltpu.CompilerParams(vmem_limit_bytes=VMEM_LIMIT),
        name="ada",
    )(c, w, b.reshape(1, n))


def _inproj_kernel(x_ref, g_ref, mod_ref, w_ref, c_ref, s1_ref, s2_ref, hot_ref,
                   rq_ref, rk_ref, rv_ref, rg_ref, nq_ref, kc_ref, vc_ref,
                   ksx_ref, vst_ref, kwx_ref, vwt_ref, gate_ref, wb_ref):
    @pl.when((pl.program_id(0) == 0) & (pl.program_id(1) == 0))
    def _():
        n_in = w_ref.shape[2]
        whole = (n_in // PROJ_W) * PROJ_W
        for c0 in range(0, whole, PROJ_W):
            wb_ref[:, c0:c0 + PROJ_W] = w_ref[0, :, c0:c0 + PROJ_W].astype(BF16)
        wb_ref[:, whole:] = jnp.zeros((wb_ref.shape[0], wb_ref.shape[1] - whole), BF16)
        wb_ref[:, whole:n_in] = w_ref[0, :, whole:n_in].astype(BF16)

    def modulated_norm(rows):
        x = x_ref[0, rows, :]
        y = x * lax.rsqrt(jnp.mean(x * x, axis=-1, keepdims=True) + EPS) * g_ref[...]
        return (y * (1.0 + mod_ref[0, 1:2, :]) + mod_ref[0, 0:1, :]).astype(BF16)

    def product_steps(rows, hb):
        tabs = (c_ref[rows, :], s1_ref[rows, :], s2_ref[rows, :])
        low = lax.broadcasted_iota(jnp.int32, (rows.stop - rows.start, LANES), 1) < NSA_DH

        def proj(c0):
            a = _nn(hb, wb_ref[:, c0:c0 + PROJ_W])
            return [a[:, t * LANES:(t + 1) * LANES] for t in range(PROJ_W // LANES)]

        def put(out_ref, t, value):
            out_ref[0, rows, t * LANES:(t + 1) * LANES] = value.astype(out_ref.dtype)

        def roped(tiles, scale, out_ref):
            for t, a in enumerate(tiles):
                r = _rope_tile(a, *tabs)
                put(out_ref, t, r if scale == 1.0 else r * scale)

        def per_group(tile, fill, out_ref):
            put(out_ref, 0, jnp.where(low, tile, fill))
            put(out_ref, 1, jnp.where(low, _swap_halves(tile), fill))

        def per_group_t(tile, out_ref):
            t = tile.T
            ones = jnp.ones((ONES_ROWS, t.shape[1]), out_ref.dtype)
            for g in range(NSA_GROUPS):
                out_ref[0, g, 0:NSA_DH, rows] = t[g * NSA_DH:(g + 1) * NSA_DH].astype(out_ref.dtype)
                out_ref[0, g, NSA_DH:NSA_DH + ONES_ROWS, rows] = ones

        def retention_qk():
            tiles = proj(_R_RQ)
            roped(tiles[0:2], 1.0, rq_ref)
            roped(tiles[2:4], RET_DK ** -0.5, rk_ref)

        def retention_v():
            for t, a in enumerate(proj(_R_RV)):
                put(rv_ref, t, a)

        def retention_gate():
            for t, a in enumerate(proj(_R_RG)):
                put(rg_ref, t, a * _sigmoid(a))

        def nsa_q():
            roped(proj(_R_NQ), Q_SCALE, nq_ref)

        def nsa_compress_selected():
            kc_t, vc_t, ks_t, vs_t = proj(_R_KC)
            kc_ref[0, rows, :] = kc_t
            vc_ref[0, rows, :] = vc_t
            per_group(_rope_tile(ks_t, *tabs), hot_ref[rows, :], ksx_ref)
            per_group_t(vs_t, vst_ref)

        def nsa_window_gates():
            kw_t, vw_t, gates_t, _ = proj(_R_KW)
            per_group(_rope_tile(kw_t, *tabs), 0.0, kwx_ref)
            per_group_t(vw_t, vwt_ref)
            gate_ref[0, :, rows] = _sigmoid(gates_t.T[0:GATE_ROWS])

        return [retention_qk, retention_v, retention_gate, nsa_q, nsa_compress_selected, nsa_window_gates]

    tm = x_ref.shape[1]
    rows_a, rows_b = slice(0, tm // 2), slice(tm // 2, tm)
    steps_a = product_steps(rows_a, modulated_norm(rows_a))
    steps_a[0]()
    steps_b = product_steps(rows_b, modulated_norm(rows_b))
    for step in steps_a[1:] + steps_b:
        step()


def _in_proj(x, ln_g, mod, w_in, tabs, hot):
    b, s, d = x.shape
    tm = TM_IN
    grid = (b, s // tm)
    tab_spec = pl.BlockSpec((tm, LANES), lambda bi, j: (j, 0))

    def out(n, dtype):
        return (jax.ShapeDtypeStruct((b, s, n), dtype), pl.BlockSpec((1, tm, n), lambda bi, j: (bi, j, 0)))

    def out_t(rows, dtype):
        return (jax.ShapeDtypeStruct((b, NSA_GROUPS, rows, s), dtype),
                pl.BlockSpec((1, NSA_GROUPS, rows, tm), lambda bi, j: (bi, 0, 0, j)))

    vt_rows = NSA_DH + ONES_ROWS
    outs = [out(256, BF16), out(256, BF16), out(512, BF16), out(512, BF16), out(512, BF16),
            out(LANES, F32), out(LANES, F32),
            out(256, BF16), out_t(vt_rows, BF16), out(256, BF16), out_t(vt_rows, BF16),
            (jax.ShapeDtypeStruct((b, GATE_ROWS, s), F32),
             pl.BlockSpec((1, GATE_ROWS, tm), lambda bi, j: (bi, 0, j)))]
    return pl.pallas_call(
        _inproj_kernel,
        grid=grid,
        in_specs=[pl.BlockSpec((1, tm, d), lambda bi, j: (bi, j, 0)),
                  pl.BlockSpec((1, d), lambda bi, j: (0, 0)),
                  pl.BlockSpec((1, 6, d), lambda bi, j: (bi, 0, 0)),
                  pl.BlockSpec((1,) + w_in.shape[1:], lambda bi, j: (0, 0, 0), pipeline_mode=pl.Buffered(1)),
                  tab_spec, tab_spec, tab_spec, tab_spec],
        out_specs=[o[1] for o in outs],
        out_shape=[o[0] for o in outs],
        scratch_shapes=[pltpu.VMEM((d, IN_COLS_K), BF16)],
        compiler_params=pltpu.CompilerParams(
            dimension_semantics=("arbitrary", "arbitrary"), vmem_limit_bytes=VMEM_LIMIT),
        name="in_proj",
    )(x, ln_g.reshape(1, d), mod, w_in, *tabs, hot)


def _ret_kernel(q_ref, k_ref, v_ref, rg_ref, dec_ref, zeta_ref, xi_ref, gch_ref, o_ref, kv_ref, prev_ref):
    c = RET_CHUNK
    n_chunks = q_ref.shape[1] // c
    low = lax.broadcasted_iota(jnp.int32, (c, LANES), 1) < RET_DK

    def chunk_rows(n):
        return pl.ds(pl.multiple_of(n * c, c), c)

    def head_cols(h):
        return slice(h * RET_DV, (h + 1) * RET_DV)

    def kv_body(it, carry):
        kz_t = {}
        for j in range(RET_GROUP):
            rows = chunk_rows(it * RET_GROUP + j)
            for p in range(RET_HEADS // 2):
                pair = slice(p * LANES, (p + 1) * LANES)
                kz_t[j, p] = (k_ref[0, rows, pair].astype(F32) * zeta_ref[p]).T.astype(BF16)
        for j in range(RET_GROUP):
            n = it * RET_GROUP + j
            for h in range(RET_HEADS):
                kv_ref[h, n] = _nn(kz_t[j, h // 2], v_ref[0, chunk_rows(n), head_cols(h)])
        return carry

    lax.fori_loop(0, n_chunks // RET_GROUP, kv_body, 0)

    for h in range(RET_HEADS):
        def scan_body(n, st, h=h):
            prev_ref[h, n] = st.astype(prev_ref.dtype)
            return st * gch_ref[h] + kv_ref[h, n]
        lax.fori_loop(0, n_chunks, scan_body, jnp.zeros((LANES, RET_DV), F32))

    def out_body(it, carry):
        chains = [(j, h) for j in range(RET_GROUP) for h in range(RET_HEADS)]
        chunk = lambda j: it * RET_GROUP + j
        q_own, qx_own, att, ys = {}, {}, {}, {}
        for j, h in chains:
            p, e = divmod(h, 2)
            pair = slice(p * LANES, (p + 1) * LANES)
            mine = low if e == 0 else jnp.logical_not(low)
            q2 = q_ref[0, chunk_rows(chunk(j)), pair].astype(F32)
            q_own[j, h] = jnp.where(mine, q2, 0.0).astype(BF16)
            qx_own[j, h] = jnp.where(mine, q2 * xi_ref[p], 0.0).astype(BF16)
        for j, h in chains:
            pair = slice((h // 2) * LANES, (h // 2 + 1) * LANES)
            att[j, h] = _nt(q_own[j, h], k_ref[0, chunk_rows(chunk(j)), pair])
        for j, h in chains:
            lhs = jnp.concatenate([(att[j, h] * dec_ref[h]).astype(BF16), qx_own[j, h]], axis=1)
            rhs = jnp.concatenate([v_ref[0, chunk_rows(chunk(j)), head_cols(h)], prev_ref[h, chunk(j)]], axis=0)
            ys[j, h] = _nn(lhs, rhs)
        for j, h in chains:
            y = ys[j, h]
            yn = y * lax.rsqrt(jnp.mean(y * y, axis=-1, keepdims=True) + EPS)
            gate = rg_ref[0, chunk_rows(chunk(j)), head_cols(h)].astype(F32)
            o_ref[0, chunk_rows(chunk(j)), head_cols(h)] = (yn * gate).astype(o_ref.dtype)
        return carry

    lax.fori_loop(0, n_chunks // RET_GROUP, out_body, 0)


def _retention(rq, rk, rv, rg_act, tables):
    b, s, _ = rq.shape
    decay, zeta_p, xi_p, g_b = tables
    whole = lambda a: pl.BlockSpec(a.shape, lambda bi: (0,) * a.ndim)
    row = lambda a: pl.BlockSpec((1,) + a.shape[1:], lambda bi: (bi, 0, 0))
    return pl.pallas_call(
        _ret_kernel,
        grid=(b,),
        in_specs=[row(rq), row(rk), row(rv), row(rg_act),
                  whole(decay), whole(zeta_p), whole(xi_p), whole(g_b)],
        out_specs=row(rv),
        out_shape=jax.ShapeDtypeStruct(rv.shape, BF16),
        scratch_shapes=[pltpu.VMEM((RET_HEADS, s // RET_CHUNK, LANES, RET_DV), F32),
                        pltpu.VMEM((RET_HEADS, s // RET_CHUNK, LANES, RET_DV), BF16)],
        compiler_params=pltpu.CompilerParams(
            dimension_semantics=("parallel",), vmem_limit_bytes=VMEM_LIMIT),
        name="retention",
    )(rq, rk, rv, rg_act, decay, zeta_p, xi_p, g_b)


def _cmp_kernel(kc_ref, vc_ref, w1_ref, pe_ref, w2_ref, c_ref, s1_ref, s2_ref, kv_ref, vt_ref):
    n_piece = kc_ref.shape[1] // CMP_STRIDE
    halves = CMP_LEN // CMP_STRIDE
    lhs = {}
    for t, src in enumerate((kc_ref, vc_ref)):
        rows = [src[0, pl.ds(r, n_piece, stride=CMP_STRIDE), :] for r in range(CMP_STRIDE)]
        for half in range(halves):
            pe0 = half * CMP_STRIDE
            lhs[t, half] = jnp.concatenate(
                [(rows[r] + pe_ref[t, pe0 + r:pe0 + r + 1, :]).astype(BF16) for r in range(CMP_STRIDE)], axis=1)
    part = {key: _nn(lhs[key], w1_ref[key[0], key[1]]) for key in lhs}
    out = jnp.zeros((n_piece, kv_ref.shape[2]), F32)
    for t in range(2):
        hid = part[t, 0]
        for half in range(1, halves):
            hid = hid + pltpu.roll(part[t, half], n_piece - half, 0)
        out = out + _nn((hid * _sigmoid(hid)).astype(BF16), w2_ref[t])
    for t in range(out.shape[1] // LANES):
        sl = slice(t * LANES, (t + 1) * LANES)
        kv = _rope_tile(out[:, sl], c_ref[...], s1_ref[...], s2_ref[...])
        kv_ref[0, :, sl] = kv.astype(kv_ref.dtype)
        vt_ref[0, t] = kv.T[NSA_DH:2 * NSA_DH].astype(vt_ref.dtype)


def _compress_weights(w1_k, w2_k, w1_v, w2_v, pe_k, pe_v):
    dh, hid = NSA_DH, CMP_HIDDEN

    def first(w1):
        w = w1.reshape(CMP_LEN, dh, hid)
        z = jnp.zeros_like(w)
        return jnp.concatenate([jnp.concatenate([w, z], axis=2), jnp.concatenate([z, w], axis=2)], axis=1)

    def second(w2, off):
        z = jnp.zeros_like(w2)
        rows = []
        for g in range(NSA_GROUPS):
            c = [z, z, z, z]
            c[2 * g + off] = w2
            rows.append(jnp.concatenate(c, axis=1))
        return jnp.concatenate(rows, axis=0)

    w1 = jnp.stack([first(w1_k), first(w1_v)]).astype(BF16).reshape(
        2, CMP_LEN // CMP_STRIDE, CMP_STRIDE * NSA_GROUPS * dh, NSA_GROUPS * hid)
    w2 = jnp.stack([second(w2_k, 0), second(w2_v, 1)]).astype(BF16)
    pe = jnp.stack([jnp.concatenate([pe_k, pe_k], axis=1), jnp.concatenate([pe_v, pe_v], axis=1)])
    return w1, w2, pe


def _compress(kc, vc, w1, w2, pe, tabs_cmp):
    b, s, w = kc.shape
    n_piece = s // CMP_STRIDE
    const2 = lambda bi: (0, 0)
    src = pl.BlockSpec((1, s, w), lambda bi: (bi, 0, 0))
    dst = pl.BlockSpec((1, n_piece, 2 * w), lambda bi: (bi, 0, 0))
    shape = jax.ShapeDtypeStruct((b, n_piece, 2 * w), BF16)
    return pl.pallas_call(
        _cmp_kernel,
        grid=(b,),
        in_specs=[src, src,
                  pl.BlockSpec(w1.shape, lambda bi: (0, 0, 0, 0)),
                  pl.BlockSpec(pe.shape, lambda bi: (0, 0, 0)),
                  pl.BlockSpec(w2.shape, lambda bi: (0, 0, 0)),
                  pl.BlockSpec((n_piece, LANES), const2),
                  pl.BlockSpec((n_piece, LANES), const2),
                  pl.BlockSpec((n_piece, LANES), const2)],
        out_specs=[dst, pl.BlockSpec((1, NSA_GROUPS, NSA_DH, n_piece), lambda bi: (bi, 0, 0, 0))],
        out_shape=[shape, jax.ShapeDtypeStruct((b, NSA_GROUPS, NSA_DH, n_piece), BF16)],
        compiler_params=pltpu.CompilerParams(
            dimension_semantics=("parallel",), vmem_limit_bytes=VMEM_LIMIT),
        name="compress",
    )(kc, vc, w1, pe, w2, *tabs_cmp)


def _nsa_kernel(q_ref, kcmp_ref, vct_ref, ksx_ref, vst_ref, kwx_ref, vwt_ref, gate_ref, ovt_ref, *rest):
    n_w = (len(rest) - 1) // 2
    o_ref = rest[n_w]
    for w_src, w_dst in zip(rest[:n_w], rest[n_w + 1:]):
        w_dst[...] = w_src[...].astype(w_dst.dtype)

    tq = TQ
    seq = q_ref.shape[1]
    nb = seq // SLC_LEN
    n_cmp = kcmp_ref.shape[1]
    hpg, dh = NSA_HPG, NSA_DH
    assert WIN % tq == 0 and seq % tq == 0
    group = pl.program_id(1)

    low = lax.broadcasted_iota(jnp.int32, (tq, LANES), 1) < dh
    eye = jnp.where(lax.broadcasted_iota(jnp.int32, (tq, tq), 0)
                    == lax.broadcasted_iota(jnp.int32, (tq, tq), 1), 1.0, 0.0).astype(BF16)
    blk = lax.broadcasted_iota(jnp.int32, (nb, tq), 0)
    col = lax.broadcasted_iota(jnp.int32, (nb, tq), 1)
    crow = lax.broadcasted_iota(jnp.int32, (n_cmp, tq), 0)
    ccol = lax.broadcasted_iota(jnp.int32, (n_cmp, tq), 1)
    kcm = kcmp_ref[0]
    vct = vct_ref[0, 0]
    ovt = ovt_ref[...]

    def per_head(x):
        return jnp.concatenate([x] * hpg, axis=1)

    key_off = lax.broadcasted_iota(jnp.int32, (tq, tq), 0)
    qry_off = lax.broadcasted_iota(jnp.int32, (tq, tq), 1)
    not_after = per_head(jnp.where(key_off <= qry_off, 0.0, NEG))
    inside_win = per_head(jnp.where(key_off > qry_off, 0.0, NEG))

    def masked(s, first_key, t0, windowed):
        blocks = []
        for r in range(0, s.shape[0], tq):
            blk_s = s[r:r + tq]
            if first_key + r == t0:
                blk_s = blk_s + not_after
            elif windowed and first_key + r == t0 - WIN:
                blk_s = blk_s + inside_win
            blocks.append(blk_s)
        return jnp.concatenate(blocks, axis=0)

    def normalise(acc):
        return acc[0:dh] / acc[dh:dh + 1]

    def select_blocks(psum, t0):
        tcol = col + t0
        bcausal = blk * SLC_LEN <= tcol
        n_live = (t0 + tq - 1) // SLC_LEN + 1
        top_n = min(SLC_TOPK, nb)
        if n_live <= top_n:
            bias = jnp.where(bcausal, 0.0, NEG)
        else:
            p_hi = psum.astype(BF16)
            p_lo = (psum - p_hi.astype(F32)).astype(BF16)
            imp = _nn(ovt, p_hi) + _nn(ovt, p_lo)
            cur = tcol // SLC_LEN
            forced = (blk == 0) | (blk == cur) | (blk == cur - 1)
            imp = jnp.where(bcausal, jnp.where(forced, FORCE, imp), NEG)
            rank = jnp.zeros((nb, tq), F32)
            for j in range(n_live):
                r = imp[j:j + 1, :]
                rank = rank + jnp.where(blk > j, jnp.where(r >= imp, 1.0, 0.0), jnp.where(r > imp, 1.0, 0.0))
            bias = jnp.where((rank < float(top_n)) & bcausal, 0.0, NEG)
        feat = jnp.concatenate([jnp.zeros((dh, tq), F32), bias,
                                jnp.zeros((LANES - dh - nb, tq), F32)], axis=0).astype(BF16)
        return _nt(eye, feat)

    class TileGroup:
        def __init__(self, tiles):
            self.tiles = tiles
            self.wstart = {t0: max(t0 - WIN, 0) for t0 in tiles}
            self.wkeys = {t0: slice(self.wstart[t0], t0 + tq) for t0 in tiles}
            self.skeys = {t0: slice(0, t0 + tq) for t0 in tiles}

        def scores_window_compressed(self):
            self.heads, self.qs = {}, {}
            for t0 in self.tiles:
                qf = q_ref[0, t0:t0 + tq, :].astype(F32)
                hl = []
                for hh in range(hpg):
                    t = qf[:, (hh // 2) * LANES:(hh // 2 + 1) * LANES]
                    if hh % 2 == 1:
                        t = _swap_halves(t)
                    hl.append(jnp.where(low, t, 0.0))
                self.heads[t0] = hl
                self.qs[t0] = jnp.concatenate(hl, axis=0).astype(BF16)
            self.s_w = {t0: _nt(kwx_ref[0, self.wkeys[t0], :], self.qs[t0]) for t0 in self.tiles}
            self.s_c = {t0: _nt(kcm, self.qs[t0]) for t0 in self.tiles}

        def select_and_scores_selected(self):
            self.p_cmp, qsel = {}, {}
            for t0 in self.tiles:
                cmask = (crow * CMP_STRIDE + (CMP_LEN - 1)) <= (ccol + t0)
                p_all = []
                psum = jnp.zeros((n_cmp, tq), F32)
                for hh in range(hpg):
                    sh = jnp.where(cmask, self.s_c[t0][:, hh * tq:(hh + 1) * tq], NEG)
                    e = jnp.exp2(sh - jnp.max(sh, axis=0, keepdims=True))
                    p = jnp.where(cmask, e / jnp.sum(e, axis=0, keepdims=True), 0.0)
                    psum = psum + p
                    p_all.append(p.astype(BF16))
                self.p_cmp[t0] = jnp.concatenate(p_all, axis=1)
                qbias = select_blocks(psum, t0)
                qsel[t0] = jnp.concatenate([hd + qbias for hd in self.heads[t0]], axis=0).astype(BF16)
            self.s_s = {t0: _nt(ksx_ref[0, self.skeys[t0], :], qsel[t0]) for t0 in self.tiles}

        def outputs_window_compressed(self):
            e_w = {}
            for t0 in self.tiles:
                sw = masked(self.s_w[t0], self.wstart[t0], t0, True)
                e_w[t0] = jnp.exp2(sw - jnp.max(sw, axis=0, keepdims=True)).astype(BF16)
            self.o_win = {t0: normalise(_nn(vwt_ref[0, 0, :, self.wkeys[t0]], e_w[t0])) for t0 in self.tiles}
            self.o_cmp = {t0: _nn(vct, self.p_cmp[t0]) for t0 in self.tiles}

        def outputs_selected_and_store(self):
            e_s = {}
            for t0 in self.tiles:
                ss = masked(self.s_s[t0], 0, t0, False)
                e_s[t0] = jnp.exp2(ss - jnp.max(ss, axis=0, keepdims=True)).astype(BF16)
            o_sel = {t0: normalise(_nn(vst_ref[0, 0, :, self.skeys[t0]], e_s[t0])) for t0 in self.tiles}
            for t0 in self.tiles:
                gt = gate_ref[0, :, t0:t0 + tq]
                outs = []
                for hh in range(hpg):
                    hc = slice(hh * tq, (hh + 1) * tq)

                    def gate_row(branch):
                        by_group = [gt[branch * NSA_HEADS + g * hpg + hh:branch * NSA_HEADS + g * hpg + hh + 1, :]
                                    for g in range(NSA_GROUPS)]
                        row = by_group[-1]
                        for g in range(NSA_GROUPS - 2, -1, -1):
                            row = jnp.where(group == g, by_group[g], row)
                        return row

                    outs.append(gate_row(0) * self.o_cmp[t0][:, hc] + gate_row(1) * o_sel[t0][:, hc]
                                + gate_row(2) * self.o_win[t0][:, hc])
                o_ref[0, t0:t0 + tq, :] = jnp.concatenate(outs, axis=0).T.astype(o_ref.dtype)

    starts = list(range(0, seq, tq))
    groups = [TileGroup(starts[i:i + TILE_GROUP]) for i in range(0, len(starts), TILE_GROUP)]
    n = len(groups)
    groups[0].scores_window_compressed()
    for i, g in enumerate(groups):
        if i + 1 < n:
            groups[i + 1].scores_window_compressed()
        g.select_and_scores_selected()
        g.outputs_window_compressed()
        if i > 0:
            groups[i - 1].outputs_selected_and_store()
    groups[n - 1].outputs_selected_and_store()


def _nsa_attention(nq, kvcmp, vct, ksx, vst, kwx, vwt, gate_t, ovt, weights):
    b, s, _ = nq.shape
    n_cmp = kvcmp.shape[1]
    gw = NSA_HPG * NSA_DH
    steps = b * NSA_GROUPS
    per_group = lambda rows, width: pl.BlockSpec((1, rows, width), lambda bi, g: (bi, 0, g))
    per_group_t = lambda a: pl.BlockSpec((1, 1) + a.shape[2:], lambda bi, g: (bi, g, 0, 0))
    assert all(w.shape[1] % (steps * 16) == 0 for w in weights)
    slab = lambda w: pl.BlockSpec((1, w.shape[1] // steps, w.shape[2]),
                                  lambda bi, g: (0, bi * NSA_GROUPS + g, 0))
    outs = pl.pallas_call(
        _nsa_kernel,
        grid=(b, NSA_GROUPS),
        in_specs=[per_group(s, gw),
                  per_group(n_cmp, LANES), per_group_t(vct),
                  per_group(s, LANES), per_group_t(vst), per_group(s, LANES), per_group_t(vwt),
                  pl.BlockSpec((1,) + gate_t.shape[1:], lambda bi, g: (bi, 0, 0)),
                  pl.BlockSpec(ovt.shape, lambda bi, g: (0, 0))] + [slab(w) for w in weights],
        out_specs=[per_group(s, gw)] + [slab(w) for w in weights],
        out_shape=[jax.ShapeDtypeStruct((b, s, NSA_HEADS * NSA_DH), BF16)]
                  + [jax.ShapeDtypeStruct(w.shape, BF16) for w in weights],
        compiler_params=pltpu.CompilerParams(
            dimension_semantics=("parallel", "parallel"), vmem_limit_bytes=VMEM_LIMIT),
        name="nsa_attn",
    )(nq, kvcmp, vct, ksx, vst, kwx, vwt, gate_t, ovt, *weights)
    return outs[0], [w[0] for w in outs[1:]]


def _ffn_kernel(x_ref, yr_ref, yn_ref, mod_ref, g2_ref, gf_ref, wo_ref, wg_ref, wu_ref, wd_ref,
                o_ref, x1_ref, act_ref):
    half_w = yr_ref.shape[1]
    d_ff = wg_ref.shape[1]
    n_chunks = d_ff // TF
    rows_a = slice(0, x_ref.shape[0] // 2)
    rows_b = slice(x_ref.shape[0] // 2, x_ref.shape[0])

    def mix(rows):
        return _nn(yr_ref[rows, :], wo_ref[0:half_w, :]) + _nn(yn_ref[rows, :], wo_ref[half_w:2 * half_w, :])

    def mid_norm(rows, mixed):
        x1 = x_ref[rows, :] + mod_ref[0, 2:3, :] * mixed
        x1_ref[rows, :] = x1
        y = x1 * lax.rsqrt(jnp.mean(x1 * x1, axis=-1, keepdims=True) + EPS) * g2_ref[...]
        return (y * (1.0 + mod_ref[0, 4:5, :]) + mod_ref[0, 3:4, :]).astype(BF16)

    def ff_chunk(rows, h2, j):
        sl = slice(j * TF, (j + 1) * TF)
        gate = _nn(h2, wg_ref[:, sl])
        up = _nn(h2, wu_ref[:, sl])
        act_ref[rows, sl] = (gate * _sigmoid(gate) * up).astype(BF16)

    def down(rows):
        return x1_ref[rows, :] + mod_ref[0, 5:6, :] * _nn(act_ref[rows, :], wd_ref[...])

    def final_norm(rows, xo):
        o_ref[rows, :] = xo * lax.rsqrt(jnp.mean(xo * xo, axis=-1, keepdims=True) + EPS) * gf_ref[...]

    mix_a = mix(rows_a)
    mix_b = mix(rows_b)
    h2_a = mid_norm(rows_a, mix_a)
    ff_chunk(rows_a, h2_a, 0)
    h2_b = mid_norm(rows_b, mix_b)
    for j in range(1, n_chunks):
        ff_chunk(rows_a, h2_a, j)
    xo_a = down(rows_a)
    ff_chunk(rows_b, h2_b, 0)
    final_norm(rows_a, xo_a)
    for j in range(1, n_chunks):
        ff_chunk(rows_b, h2_b, j)
    final_norm(rows_b, down(rows_b))


def _out_ffn(x2d, y_ret, y_nsa, mod, w_out, g2, gf, wg, wu, wd, seq):
    n, d = x2d.shape
    tm = TM_FF
    d_ff = wg.shape[1]
    tiles_per_seq = seq // tm
    half_w = y_ret.shape[1]
    row = lambda i: (i, 0)
    resident = lambda a: pl.BlockSpec(a.shape, lambda i: (0, 0), pipeline_mode=pl.Buffered(1))
    return pl.pallas_call(
        _ffn_kernel,
        grid=(n // tm,),
        in_specs=[pl.BlockSpec((tm, d), row),
                  pl.BlockSpec((tm, half_w), row),
                  pl.BlockSpec((tm, half_w), row),
                  pl.BlockSpec((1, 6, d), lambda i: (i // tiles_per_seq, 0, 0)),
                  pl.BlockSpec((1, d), lambda i: (0, 0)),
                  pl.BlockSpec((1, d), lambda i: (0, 0)),
                  resident(w_out), resident(wg), resident(wu), resident(wd)],
        out_specs=pl.BlockSpec((tm, d), row),
        out_shape=jax.ShapeDtypeStruct((n, d), F32),
        scratch_shapes=[pltpu.VMEM((tm, d), F32), pltpu.VMEM((tm, d_ff), BF16)],
        compiler_params=pltpu.CompilerParams(
            dimension_semantics=("parallel",), vmem_limit_bytes=VMEM_LIMIT),
        name="out_ffn",
    )(x2d, y_ret, y_nsa, mod, g2.reshape(1, d), gf.reshape(1, d), w_out, wg, wu, wd)


def kernel(x, c, ln_mix_g, ln_ffn_g, w_ada, b_ada, w_in, cmp_pe_k, cmp_w1_k, cmp_w2_k,
           cmp_pe_v, cmp_w1_v, cmp_w2_v, w_out, w_ff_gate, w_ff_up, w_ff_down, ln_final_g):
    assert w_in.shape[0] == 1, "the final RMSNorm is fused into the (single) layer's FFN kernel"
    b, s, d = x.shape
    lane = np.arange(LANES)
    tabs = _rope_tables(np.arange(s), np.ones(LANES, bool))
    n_piece = s // CMP_STRIDE
    tabs_cmp = _rope_tables(np.arange(n_piece) * CMP_STRIDE + CMP_LEN - 1, lane < NSA_DH)

    mod = _ada(c, w_ada[0], b_ada[0]).reshape(b, 6, d)
    rq, rk, rv, rg_act, nq, kc, vc, ksx, vst, kwx, vwt, gate_t = _in_proj(
        x, ln_mix_g[0], mod, w_in, tabs, _block_onehot_table(s))
    y_ret = _retention(rq, rk, rv, rg_act, _retention_tables())
    w1, w2, pe = _compress_weights(cmp_w1_k[0], cmp_w2_k[0], cmp_w1_v[0], cmp_w2_v[0],
                                   cmp_pe_k[0], cmp_pe_v[0])
    kvcmp, vct = _compress(kc, vc, w1, w2, pe, tabs_cmp)
    y_nsa, (wo_b, wg_b, wu_b, wd_b) = _nsa_attention(
        nq, kvcmp, vct, ksx, vst, kwx, vwt, gate_t, _overlap_t(s), (w_out, w_ff_gate, w_ff_up, w_ff_down))
    out = _out_ffn(x.reshape(b * s, d), y_ret.reshape(b * s, -1), y_nsa.reshape(b * s, -1), mod,
                   wo_b, ln_ffn_g[0], ln_final_g, wg_b, wu_b, wd_b, s)
    return out.reshape(b, s, d)
```

```python
import numpy as np
import jax
import jax.numpy as jnp
from jax import lax
from jax.experimental import pallas as pl
from jax.experimental.pallas import tpu as pltpu

F32 = jnp.float32
BF16 = jnp.bfloat16

D_MODEL = 1024
RET_HEADS = 4
RET_DK = 64
RET_DV = 128
RET_CHUNK = 128
NSA_HEADS = 8
NSA_GROUPS = 2
NSA_HPG = NSA_HEADS // NSA_GROUPS
NSA_DH = 64
CMP_LEN = 32
CMP_STRIDE = 16
CMP_HIDDEN = 128
SLC_LEN = 64
SLC_TOPK = 16
WIN = 512
D_FF = ((8 * D_MODEL + 3 * 256 - 1) // (3 * 256)) * 256
ROPE_THETA = 10000.0
EPS = 1e-6
NEG = -1e30
FORCE = 1e6

LANES = 128
SUBLANES = 8
HALF = NSA_DH // 2
ONES_ROWS = 16
GATE_ROWS = 3 * NSA_HEADS

TM_IN = 1024
PROJ_W = 512
RET_GROUP = 8
Q_SCALE = NSA_DH ** -0.5 * float(np.log2(np.e))
TQ = 128
TILE_GROUP = 2
TM_FF = 512
TF = 256
VMEM_LIMIT = 56 * 1024 * 1024

_R_RQ, _R_RK, _R_RV, _R_RG, _R_NQ = 0, 256, 512, 1024, 1536
_R_KC, _R_VC, _R_KS, _R_VS, _R_KW, _R_VW, _R_GATE = 2048, 2176, 2304, 2432, 2560, 2688, 2816
IN_COLS_K = -(-(_R_GATE + 3 * NSA_HEADS) // PROJ_W) * PROJ_W


def _sigmoid(x):
    return 1.0 / (1.0 + jnp.exp(-x))


def _nt(a, b):
    return lax.dot_general(a, b, (((1,), (1,)), ((), ())), preferred_element_type=F32)


def _nn(a, b):
    return jnp.dot(a, b, preferred_element_type=F32)


def _rope_tile(a, c, s1, s2):
    return a * c + pltpu.roll(a, HALF, 1) * s1 + pltpu.roll(a, LANES - HALF, 1) * s2


def _swap_halves(a):
    return pltpu.roll(a, LANES // 2, 1)


def _rope_tables(pos, rotary_lanes):
    pos = np.asarray(pos, np.float64)
    lane = np.arange(LANES)
    within = lane % NSA_DH
    freq = ROPE_THETA ** (-(within % HALF).astype(np.float64) / HALF)
    ang = pos[:, None] * freq[None, :]
    cos, sin = np.cos(ang), np.sin(ang)
    first = (within < HALF)[None, :]
    rot = np.asarray(rotary_lanes, bool)[None, :]
    c = np.where(rot, cos, 1.0)
    s1 = np.where(rot & ~first, sin, 0.0)
    s2 = np.where(rot & first, -sin, 0.0)
    return (jnp.asarray(c, F32), jnp.asarray(s1, F32), jnp.asarray(s2, F32))


def _block_onehot_table(seq):
    t = np.zeros((seq, LANES), np.float32)
    pos = np.arange(seq)
    t[pos, NSA_DH + pos // SLC_LEN] = 1.0
    return jnp.asarray(t)


def _retention_tables():
    h = np.arange(RET_HEADS, dtype=np.float64)
    log_g = np.log(1.0 - 2.0 ** (-5.0 - h))
    c = RET_CHUNK
    idx = np.arange(c, dtype=np.float64)
    diff = idx[:, None] - idx[None, :]
    causal = diff >= 0
    decay = np.where(causal, np.exp(log_g[:, None, None] * np.where(causal, diff, 0.0)), 0.0)
    zeta = np.exp(log_g[:, None] * (c - 1.0 - idx))
    xi = np.exp(log_g[:, None] * (idx + 1.0))
    g_chunk = np.exp(log_g * c)

    def pair_lanes(t):
        t = t.reshape(RET_HEADS // 2, 2, c)
        return np.repeat(np.transpose(t, (0, 2, 1)), RET_DK, axis=2)

    g_b = np.broadcast_to(g_chunk[:, None, None], (RET_HEADS, 1, LANES))
    return tuple(jnp.asarray(t, F32) for t in (decay, pair_lanes(zeta), pair_lanes(xi), g_b))


def _overlap_t(seq):
    n_c = seq // CMP_STRIDE - CMP_LEN // CMP_STRIDE + 1
    nb = seq // SLC_LEN
    cs = np.arange(n_c) * CMP_STRIDE
    bs = np.arange(nb) * SLC_LEN
    ov = np.maximum(np.minimum(cs[:, None] + CMP_LEN, bs[None] + SLC_LEN)
                    - np.maximum(cs[:, None], bs[None]), 0).astype(np.float64) / CMP_LEN
    ncp = seq // CMP_STRIDE
    ovp = np.zeros((ncp, nb))
    ovp[:n_c] = ov
    return jnp.asarray(ovp.T, BF16)


def _ada_kernel(c_ref, w_ref, b_ref, o_ref):
    c = c_ref[...]
    o_ref[...] = _nn(c * _sigmoid(c), w_ref[...]) + b_ref[...]


def _ada(c, w, b):
    bsz, d = c.shape
    n = w.shape[1]
    tn = 1024
    return pl.pallas_call(
        _ada_kernel,
        grid=(n // tn,),
        in_specs=[pl.BlockSpec((bsz, d), lambda j: (0, 0)),
                  pl.BlockSpec((d, tn), lambda j: (0, j)),
                  pl.BlockSpec((1, tn), lambda j: (0, j))],
        out_specs=pl.BlockSpec((bsz, tn), lambda j: (0, j)),
        out_shape=jax.ShapeDtypeStruct((bsz, n), F32),
        compiler_params=pltpu.CompilerParams(vmem_limit_bytes=VMEM_LIMIT),
        name="ada",
    )(c, w, b.reshape(1, n))


def _inproj_kernel(x_ref, g_ref, mod_ref, w_ref, c_ref, s1_ref, s2_ref, hot_ref,
                   rq_ref, rk_ref, rv_ref, rg_ref, nq_ref, kc_ref, vc_ref,
                   ksx_ref, vst_ref, kwx_ref, vwt_ref, gate_ref, wb_ref):
    @pl.when((pl.program_id(0) == 0) & (pl.program_id(1) == 0))
    def _():
        n_in = w_ref.shape[2]
        whole = (n_in // PROJ_W) * PROJ_W
        for c0 in range(0, whole, PROJ_W):
            wb_ref[:, c0:c0 + PROJ_W] = w_ref[0, :, c0:c0 + PROJ_W].astype(BF16)
        wb_ref[:, whole:] = jnp.zeros((wb_ref.shape[0], wb_ref.shape[1] - whole), BF16)
        wb_ref[:, whole:n_in] = w_ref[0, :, whole:n_in].astype(BF16)

    def modulated_norm(rows):
        x = x_ref[0, rows, :]
        y = x * lax.rsqrt(jnp.mean(x * x, axis=-1, keepdims=True) + EPS) * g_ref[...]
        return (y * (1.0 + mod_ref[0, 1:2, :]) + mod_ref[0, 0:1, :]).astype(BF16)

    def product_steps(rows, hb):
        tabs = (c_ref[rows, :], s1_ref[rows, :], s2_ref[rows, :])
        low = lax.broadcasted_iota(jnp.int32, (rows.stop - rows.start, LANES), 1) < NSA_DH

        def proj(c0):
            a = _nn(hb, wb_ref[:, c0:c0 + PROJ_W])
            return [a[:, t * LANES:(t + 1) * LANES] for t in range(PROJ_W // LANES)]

        def put(out_ref, t, value):
            out_ref[0, rows, t * LANES:(t + 1) * LANES] = value.astype(out_ref.dtype)

        def roped(tiles, scale, out_ref):
            for t, a in enumerate(tiles):
                r = _rope_tile(a, *tabs)
                put(out_ref, t, r if scale == 1.0 else r * scale)

        def per_group(tile, fill, out_ref):
            put(out_ref, 0, jnp.where(low, tile, fill))
            put(out_ref, 1, jnp.where(low, _swap_halves(tile), fill))

        def per_group_t(tile, out_ref):
            t = tile.T
            ones = jnp.ones((ONES_ROWS, t.shape[1]), out_ref.dtype)
            for g in range(NSA_GROUPS):
                out_ref[0, g, 0:NSA_DH, rows] = t[g * NSA_DH:(g + 1) * NSA_DH].astype(out_ref.dtype)
                out_ref[0, g, NSA_DH:NSA_DH + ONES_ROWS, rows] = ones

        def retention_qk():
            tiles = proj(_R_RQ)
            roped(tiles[0:2], 1.0, rq_ref)
            roped(tiles[2:4], RET_DK ** -0.5, rk_ref)

        def retention_v():
            for t, a in enumerate(proj(_R_RV)):
                put(rv_ref, t, a)

        def retention_gate():
            for t, a in enumerate(proj(_R_RG)):
                put(rg_ref, t, a * _sigmoid(a))

        def nsa_q():
            roped(proj(_R_NQ), Q_SCALE, nq_ref)

        def nsa_compress_selected():
            kc_t, vc_t, ks_t, vs_t = proj(_R_KC)
            kc_ref[0, rows, :] = kc_t
            vc_ref[0, rows, :] = vc_t
            per_group(_rope_tile(ks_t, *tabs), hot_ref[rows, :], ksx_ref)
            per_group_t(vs_t, vst_ref)

        def nsa_window_gates():
            kw_t, vw_t, gates_t, _ = proj(_R_KW)
            per_group(_rope_tile(kw_t, *tabs), 0.0, kwx_ref)
            per_group_t(vw_t, vwt_ref)
            gate_ref[0, :, rows] = _sigmoid(gates_t.T[0:GATE_ROWS])

        return [retention_qk, retention_v, retention_gate, nsa_q, nsa_compress_selected, nsa_window_gates]

    tm = x_ref.shape[1]
    rows_a, rows_b = slice(0, tm // 2), slice(tm // 2, tm)
    steps_a = product_steps(rows_a, modulated_norm(rows_a))
    steps_a[0]()
    steps_b = product_steps(rows_b, modulated_norm(rows_b))
    for step in steps_a[1:] + steps_b:
        step()


def _in_proj(x, ln_g, mod, w_in, tabs, hot):
    b, s, d = x.shape
    tm = TM_IN
    grid = (b, s // tm)
    tab_spec = pl.BlockSpec((tm, LANES), lambda bi, j: (j, 0))

    def out(n, dtype):
        return (jax.ShapeDtypeStruct((b, s, n), dtype), pl.BlockSpec((1, tm, n), lambda bi, j: (bi, j, 0)))

    def out_t(rows, dtype):
        return (jax.ShapeDtypeStruct((b, NSA_GROUPS, rows, s), dtype),
                pl.BlockSpec((1, NSA_GROUPS, rows, tm), lambda bi, j: (bi, 0, 0, j)))

    vt_rows = NSA_DH + ONES_ROWS
    outs = [out(256, BF16), out(256, BF16), out(512, BF16), out(512, BF16), out(512, BF16),
            out(LANES, F32), out(LANES, F32),
            out(256, BF16), out_t(vt_rows, BF16), out(256, BF16), out_t(vt_rows, BF16),
            (jax.ShapeDtypeStruct((b, GATE_ROWS, s), F32),
             pl.BlockSpec((1, GATE_ROWS, tm), lambda bi, j: (bi, 0, j)))]
    return pl.pallas_call(
        _inproj_kernel,
        grid=grid,
        in_specs=[pl.BlockSpec((1, tm, d), lambda bi, j: (bi, j, 0)),
                  pl.BlockSpec((1, d), lambda bi, j: (0, 0)),
                  pl.BlockSpec((1, 6, d), lambda bi, j: (bi, 0, 0)),
                  pl.BlockSpec((1,) + w_in.shape[1:], lambda bi, j: (0, 0, 0), pipeline_mode=pl.Buffered(1)),
                  tab_spec, tab_spec, tab_spec, tab_spec],
        out_specs=[o[1] for o in outs],
        out_shape=[o[0] for o in outs],
        scratch_shapes=[pltpu.VMEM((d, IN_COLS_K), BF16)],
        compiler_params=pltpu.CompilerParams(
            dimension_semantics=("arbitrary", "arbitrary"), vmem_limit_bytes=VMEM_LIMIT),
        name="in_proj",
    )(x, ln_g.reshape(1, d), mod, w_in, *tabs, hot)


def _ret_kernel(q_ref, k_ref, v_ref, rg_ref, dec_ref, zeta_ref, xi_ref, gch_ref, o_ref, kv_ref, prev_ref):
    c = RET_CHUNK
    n_chunks = q_ref.shape[1] // c
    low = lax.broadcasted_iota(jnp.int32, (c, LANES), 1) < RET_DK

    def chunk_rows(n):
        return pl.ds(pl.multiple_of(n * c, c), c)

    def head_cols(h):
        return slice(h * RET_DV, (h + 1) * RET_DV)

    def kv_body(it, carry):
        kz_t = {}
        for j in range(RET_GROUP):
            rows = chunk_rows(it * RET_GROUP + j)
            for p in range(RET_HEADS // 2):
                pair = slice(p * LANES, (p + 1) * LANES)
                kz_t[j, p] = (k_ref[0, rows, pair].astype(F32) * zeta_ref[p]).T.astype(BF16)
        for j in range(RET_GROUP):
            n = it * RET_GROUP + j
            for h in range(RET_HEADS):
                kv_ref[h, n] = _nn(kz_t[j, h // 2], v_ref[0, chunk_rows(n), head_cols(h)])
        return carry

    lax.fori_loop(0, n_chunks // RET_GROUP, kv_body, 0)

    for h in range(RET_HEADS):
        def scan_body(n, st, h=h):
            prev_ref[h, n] = st.astype(prev_ref.dtype)
            return st * gch_ref[h] + kv_ref[h, n]
        lax.fori_loop(0, n_chunks, scan_body, jnp.zeros((LANES, RET_DV), F32))

    def out_body(it, carry):
        chains = [(j, h) for j in range(RET_GROUP) for h in range(RET_HEADS)]
        chunk = lambda j: it * RET_GROUP + j
        q_own, qx_own, att, ys = {}, {}, {}, {}
        for j, h in chains:
            p, e = divmod(h, 2)
            pair = slice(p * LANES, (p + 1) * LANES)
            mine = low if e == 0 else jnp.logical_not(low)
            q2 = q_ref[0, chunk_rows(chunk(j)), pair].astype(F32)
            q_own[j, h] = jnp.where(mine, q2, 0.0).astype(BF16)
            qx_own[j, h] = jnp.where(mine, q2 * xi_ref[p], 0.0).astype(BF16)
        for j, h in chains:
            pair = slice((h // 2) * LANES, (h // 2 + 1) * LANES)
            att[j, h] = _nt(q_own[j, h], k_ref[0, chunk_rows(chunk(j)), pair])
        for j, h in chains:
            lhs = jnp.concatenate([(att[j, h] * dec_ref[h]).astype(BF16), qx_own[j, h]], axis=1)
            rhs = jnp.concatenate([v_ref[0, chunk_rows(chunk(j)), head_cols(h)], prev_ref[h, chunk(j)]], axis=0)
            ys[j, h] = _nn(lhs, rhs)
        for j, h in chains:
            y = ys[j, h]
            yn = y * lax.rsqrt(jnp.mean(y * y, axis=-1, keepdims=True) + EPS)
            gate = rg_ref[0, chunk_rows(chunk(j)), head_cols(h)].astype(F32)
            o_ref[0, chunk_rows(chunk(j)), head_cols(h)] = (yn * gate).astype(o_ref.dtype)
        return carry

    lax.fori_loop(0, n_chunks // RET_GROUP, out_body, 0)


def _retention(rq, rk, rv, rg_act, tables):
    b, s, _ = rq.shape
    decay, zeta_p, xi_p, g_b = tables
    whole = lambda a: pl.BlockSpec(a.shape, lambda bi: (0,) * a.ndim)
    row = lambda a: pl.BlockSpec((1,) + a.shape[1:], lambda bi: (bi, 0, 0))
    return pl.pallas_call(
        _ret_kernel,
        grid=(b,),
        in_specs=[row(rq), row(rk), row(rv), row(rg_act),
                  whole(decay), whole(zeta_p), whole(xi_p), whole(g_b)],
        out_specs=row(rv),
        out_shape=jax.ShapeDtypeStruct(rv.shape, BF16),
        scratch_shapes=[pltpu.VMEM((RET_HEADS, s // RET_CHUNK, LANES, RET_DV), F32),
                        pltpu.VMEM((RET_HEADS, s // RET_CHUNK, LANES, RET_DV), BF16)],
        compiler_params=pltpu.CompilerParams(
            dimension_semantics=("parallel",), vmem_limit_bytes=VMEM_LIMIT),
        name="retention",
    )(rq, rk, rv, rg_act, decay, zeta_p, xi_p, g_b)


def _cmp_kernel(kc_ref, vc_ref, w1_ref, pe_ref, w2_ref, c_ref, s1_ref, s2_ref, kv_ref, vt_ref):
    n_piece = kc_ref.shape[1] // CMP_STRIDE
    halves = CMP_LEN // CMP_STRIDE
    lhs = {}
    for t, src in enumerate((kc_ref, vc_ref)):
        rows = [src[0, pl.ds(r, n_piece, stride=CMP_STRIDE), :] for r in range(CMP_STRIDE)]
        for half in range(halves):
            pe0 = half * CMP_STRIDE
            lhs[t, half] = jnp.concatenate(
                [(rows[r] + pe_ref[t, pe0 + r:pe0 + r + 1, :]).astype(BF16) for r in range(CMP_STRIDE)], axis=1)
    part = {key: _nn(lhs[key], w1_ref[key[0], key[1]]) for key in lhs}
    out = jnp.zeros((n_piece, kv_ref.shape[2]), F32)
    for t in range(2):
        hid = part[t, 0]
        for half in range(1, halves):
            hid = hid + pltpu.roll(part[t, half], n_piece - half, 0)
        out = out + _nn((hid * _sigmoid(hid)).astype(BF16), w2_ref[t])
    for t in range(out.shape[1] // LANES):
        sl = slice(t * LANES, (t + 1) * LANES)
        kv = _rope_tile(out[:, sl], c_ref[...], s1_ref[...], s2_ref[...])
        kv_ref[0, :, sl] = kv.astype(kv_ref.dtype)
        vt_ref[0, t] = kv.T[NSA_DH:2 * NSA_DH].astype(vt_ref.dtype)


def _compress_weights(w1_k, w2_k, w1_v, w2_v, pe_k, pe_v):
    dh, hid = NSA_DH, CMP_HIDDEN

    def first(w1):
        w = w1.reshape(CMP_LEN, dh, hid)
        z = jnp.zeros_like(w)
        return jnp.concatenate([jnp.concatenate([w, z], axis=2), jnp.concatenate([z, w], axis=2)], axis=1)

    def second(w2, off):
        z = jnp.zeros_like(w2)
        rows = []
        for g in range(NSA_GROUPS):
            c = [z, z, z, z]
            c[2 * g + off] = w2
            rows.append(jnp.concatenate(c, axis=1))
        return jnp.concatenate(rows, axis=0)

    w1 = jnp.stack([first(w1_k), first(w1_v)]).astype(BF16).reshape(
        2, CMP_LEN // CMP_STRIDE, CMP_STRIDE * NSA_GROUPS * dh, NSA_GROUPS * hid)
    w2 = jnp.stack([second(w2_k, 0), second(w2_v, 1)]).astype(BF16)
    pe = jnp.stack([jnp.concatenate([pe_k, pe_k], axis=1), jnp.concatenate([pe_v, pe_v], axis=1)])
    return w1, w2, pe


def _compress(kc, vc, w1, w2, pe, tabs_cmp):
    b, s, w = kc.shape
    n_piece = s // CMP_STRIDE
    const2 = lambda bi: (0, 0)
    src = pl.BlockSpec((1, s, w), lambda bi: (bi, 0, 0))
    dst = pl.BlockSpec((1, n_piece, 2 * w), lambda bi: (bi, 0, 0))
    shape = jax.ShapeDtypeStruct((b, n_piece, 2 * w), BF16)
    return pl.pallas_call(
        _cmp_kernel,
        grid=(b,),
        in_specs=[src, src,
                  pl.BlockSpec(w1.shape, lambda bi: (0, 0, 0, 0)),
                  pl.BlockSpec(pe.shape, lambda bi: (0, 0, 0)),
                  pl.BlockSpec(w2.shape, lambda bi: (0, 0, 0)),
                  pl.BlockSpec((n_piece, LANES), const2),
                  pl.BlockSpec((n_piece, LANES), const2),
                  pl.BlockSpec((n_piece, LANES), const2)],
        out_specs=[dst, pl.BlockSpec((1, NSA_GROUPS, NSA_DH, n_piece), lambda bi: (bi, 0, 0, 0))],
        out_shape=[shape, jax.ShapeDtypeStruct((b, NSA_GROUPS, NSA_DH, n_piece), BF16)],
        compiler_params=pltpu.CompilerParams(
            dimension_semantics=("parallel",), vmem_limit_bytes=VMEM_LIMIT),
        name="compress",
    )(kc, vc, w1, pe, w2, *tabs_cmp)


def _nsa_kernel(q_ref, kcmp_ref, vct_ref, ksx_ref, vst_ref, kwx_ref, vwt_ref, gate_ref, ovt_ref, *rest):
    n_w = (len(rest) - 1) // 2
    o_ref = rest[n_w]
    for w_src, w_dst in zip(rest[:n_w], rest[n_w + 1:]):
        w_dst[...] = w_src[...].astype(w_dst.dtype)

    tq = TQ
    seq = q_ref.shape[1]
    nb = seq // SLC_LEN
    n_cmp = kcmp_ref.shape[1]
    hpg, dh = NSA_HPG, NSA_DH
    assert WIN % tq == 0 and seq % tq == 0
    group = pl.program_id(1)

    low = lax.broadcasted_iota(jnp.int32, (tq, LANES), 1) < dh
    eye = jnp.where(lax.broadcasted_iota(jnp.int32, (tq, tq), 0)
                    == lax.broadcasted_iota(jnp.int32, (tq, tq), 1), 1.0, 0.0).astype(BF16)
    blk = lax.broadcasted_iota(jnp.int32, (nb, tq), 0)
    col = lax.broadcasted_iota(jnp.int32, (nb, tq), 1)
    crow = lax.broadcasted_iota(jnp.int32, (n_cmp, tq), 0)
    ccol = lax.broadcasted_iota(jnp.int32, (n_cmp, tq), 1)
    kcm = kcmp_ref[0]
    vct = vct_ref[0, 0]
    ovt = ovt_ref[...]

    def per_head(x):
        return jnp.concatenate([x] * hpg, axis=1)

    key_off = lax.broadcasted_iota(jnp.int32, (tq, tq), 0)
    qry_off = lax.broadcasted_iota(jnp.int32, (tq, tq), 1)
    not_after = per_head(jnp.where(key_off <= qry_off, 0.0, NEG))
    inside_win = per_head(jnp.where(key_off > qry_off, 0.0, NEG))

    def masked(s, first_key, t0, windowed):
        blocks = []
        for r in range(0, s.shape[0], tq):
            blk_s = s[r:r + tq]
            if first_key + r == t0:
                blk_s = blk_s + not_after
            elif windowed and first_key + r == t0 - WIN:
                blk_s = blk_s + inside_win
            blocks.append(blk_s)
        return jnp.concatenate(blocks, axis=0)

    def normalise(acc):
        return acc[0:dh] / acc[dh:dh + 1]

    def select_blocks(psum, t0):
        tcol = col + t0
        bcausal = blk * SLC_LEN <= tcol
        n_live = (t0 + tq - 1) // SLC_LEN + 1
        top_n = min(SLC_TOPK, nb)
        if n_live <= top_n:
            bias = jnp.where(bcausal, 0.0, NEG)
        else:
            p_hi = psum.astype(BF16)
            p_lo = (psum - p_hi.astype(F32)).astype(BF16)
            imp = _nn(ovt, p_hi) + _nn(ovt, p_lo)
            cur = tcol // SLC_LEN
            forced = (blk == 0) | (blk == cur) | (blk == cur - 1)
            imp = jnp.where(bcausal, jnp.where(forced, FORCE, imp), NEG)
            rank = jnp.zeros((nb, tq), F32)
            for j in range(n_live):
                r = imp[j:j + 1, :]
                rank = rank + jnp.where(blk > j, jnp.where(r >= imp, 1.0, 0.0), jnp.where(r > imp, 1.0, 0.0))
            bias = jnp.where((rank < float(top_n)) & bcausal, 0.0, NEG)
        feat = jnp.concatenate([jnp.zeros((dh, tq), F32), bias,
                                jnp.zeros((LANES - dh - nb, tq), F32)], axis=0).astype(BF16)
        return _nt(eye, feat)

    class TileGroup:
        def __init__(self, tiles):
            self.tiles = tiles
            self.wstart = {t0: max(t0 - WIN, 0) for t0 in tiles}
            self.wkeys = {t0: slice(self.wstart[t0], t0 + tq) for t0 in tiles}
            self.skeys = {t0: slice(0, t0 + tq) for t0 in tiles}

        def scores_window_compressed(self):
            self.heads, self.qs = {}, {}
            for t0 in self.tiles:
                qf = q_ref[0, t0:t0 + tq, :].astype(F32)
                hl = []
                for hh in range(hpg):
                    t = qf[:, (hh // 2) * LANES:(hh // 2 + 1) * LANES]
                    if hh % 2 == 1:
                        t = _swap_halves(t)
                    hl.append(jnp.where(low, t, 0.0))
                self.heads[t0] = hl
                self.qs[t0] = jnp.concatenate(hl, axis=0).astype(BF16)
            self.s_w = {t0: _nt(kwx_ref[0, self.wkeys[t0], :], self.qs[t0]) for t0 in self.tiles}
            self.s_c = {t0: _nt(kcm, self.qs[t0]) for t0 in self.tiles}

        def select_and_scores_selected(self):
            self.p_cmp, qsel = {}, {}
            for t0 in self.tiles:
                cmask = (crow * CMP_STRIDE + (CMP_LEN - 1)) <= (ccol + t0)
                p_all = []
                psum = jnp.zeros((n_cmp, tq), F32)
                for hh in range(hpg):
                    sh = jnp.where(cmask, self.s_c[t0][:, hh * tq:(hh + 1) * tq], NEG)
                    e = jnp.exp2(sh - jnp.max(sh, axis=0, keepdims=True))
                    p = jnp.where(cmask, e / jnp.sum(e, axis=0, keepdims=True), 0.0)
                    psum = psum + p
                    p_all.append(p.astype(BF16))
                self.p_cmp[t0] = jnp.concatenate(p_all, axis=1)
                qbias = select_blocks(psum, t0)
                qsel[t0] = jnp.concatenate([hd + qbias for hd in self.heads[t0]], axis=0).astype(BF16)
            self.s_s = {t0: _nt(ksx_ref[0, self.skeys[t0], :], qsel[t0]) for t0 in self.tiles}

        def outputs_window_compressed(self):
            e_w = {}
            for t0 in self.tiles:
                sw = masked(self.s_w[t0], self.wstart[t0], t0, True)
                e_w[t0] = jnp.exp2(sw - jnp.max(sw, axis=0, keepdims=True)).astype(BF16)
            self.o_win = {t0: normalise(_nn(vwt_ref[0, 0, :, self.wkeys[t0]], e_w[t0])) for t0 in self.tiles}
            self.o_cmp = {t0: _nn(vct, self.p_cmp[t0]) for t0 in self.tiles}

        def outputs_selected_and_store(self):
            e_s = {}
            for t0 in self.tiles:
                ss = masked(self.s_s[t0], 0, t0, False)
                e_s[t0] = jnp.exp2(ss - jnp.max(ss, axis=0, keepdims=True)).astype(BF16)
            o_sel = {t0: normalise(_nn(vst_ref[0, 0, :, self.skeys[t0]], e_s[t0])) for t0 in self.tiles}
            for t0 in self.tiles:
                gt = gate_ref[0, :, t0:t0 + tq]
                outs = []
                for hh in range(hpg):
                    hc = slice(hh * tq, (hh + 1) * tq)

                    def gate_row(branch):
                        by_group = [gt[branch * NSA_HEADS + g * hpg + hh:branch * NSA_HEADS + g * hpg + hh + 1, :]
                                    for g in range(NSA_GROUPS)]
                        row = by_group[-1]
                        for g in range(NSA_GROUPS - 2, -1, -1):
                            row = jnp.where(group == g, by_group[g], row)
                        return row

                    outs.append(gate_row(0) * self.o_cmp[t0][:, hc] + gate_row(1) * o_sel[t0][:, hc]
                                + gate_row(2) * self.o_win[t0][:, hc])
                o_ref[0, t0:t0 + tq, :] = jnp.concatenate(outs, axis=0).T.astype(o_ref.dtype)

    starts = list(range(0, seq, tq))
    groups = [TileGroup(starts[i:i + TILE_GROUP]) for i in range(0, len(starts), TILE_GROUP)]
    n = len(groups)
    groups[0].scores_window_compressed()
    for i, g in enumerate(groups):
        if i + 1 < n:
            groups[i + 1].scores_window_compressed()
        g.select_and_scores_selected()
        g.outputs_window_compressed()
        if i > 0:
            groups[i - 1].outputs_selected_and_store()
    groups[n - 1].outputs_selected_and_store()


def _nsa_attention(nq, kvcmp, vct, ksx, vst, kwx, vwt, gate_t, ovt, weights):
    b, s, _ = nq.shape
    n_cmp = kvcmp.shape[1]
    gw = NSA_HPG * NSA_DH
    steps = b * NSA_GROUPS
    per_group = lambda rows, width: pl.BlockSpec((1, rows, width), lambda bi, g: (bi, 0, g))
    per_group_t = lambda a: pl.BlockSpec((1, 1) + a.shape[2:], lambda bi, g: (bi, g, 0, 0))
    assert all(w.shape[1] % (steps * 16) == 0 for w in weights)
    slab = lambda w: pl.BlockSpec((1, w.shape[1] // steps, w.shape[2]),
                                  lambda bi, g: (0, bi * NSA_GROUPS + g, 0))
    outs = pl.pallas_call(
        _nsa_kernel,
        grid=(b, NSA_GROUPS),
        in_specs=[per_group(s, gw),
                  per_group(n_cmp, LANES), per_group_t(vct),
                  per_group(s, LANES), per_group_t(vst), per_group(s, LANES), per_group_t(vwt),
                  pl.BlockSpec((1,) + gate_t.shape[1:], lambda bi, g: (bi, 0, 0)),
                  pl.BlockSpec(ovt.shape, lambda bi, g: (0, 0))] + [slab(w) for w in weights],
        out_specs=[per_group(s, gw)] + [slab(w) for w in weights],
        out_shape=[jax.ShapeDtypeStruct((b, s, NSA_HEADS * NSA_DH), BF16)]
                  + [jax.ShapeDtypeStruct(w.shape, BF16) for w in weights],
        compiler_params=pltpu.CompilerParams(
            dimension_semantics=("parallel", "parallel"), vmem_limit_bytes=VMEM_LIMIT),
        name="nsa_attn",
    )(nq, kvcmp, vct, ksx, vst, kwx, vwt, gate_t, ovt, *weights)
    return outs[0], [w[0] for w in outs[1:]]


def _ffn_kernel(x_ref, yr_ref, yn_ref, mod_ref, g2_ref, gf_ref, wo_ref, wg_ref, wu_ref, wd_ref,
                o_ref, x1_ref, act_ref):
    half_w = yr_ref.shape[1]
    d_ff = wg_ref.shape[1]
    n_chunks = d_ff // TF
    rows_a = slice(0, x_ref.shape[0] // 2)
    rows_b = slice(x_ref.shape[0] // 2, x_ref.shape[0])

    def mix(rows):
        return _nn(yr_ref[rows, :], wo_ref[0:half_w, :]) + _nn(yn_ref[rows, :], wo_ref[half_w:2 * half_w, :])

    def mid_norm(rows, mixed):
        x1 = x_ref[rows, :] + mod_ref[0, 2:3, :] * mixed
        x1_ref[rows, :] = x1
        y = x1 * lax.rsqrt(jnp.mean(x1 * x1, axis=-1, keepdims=True) + EPS) * g2_ref[...]
        return (y * (1.0 + mod_ref[0, 4:5, :]) + mod_ref[0, 3:4, :]).astype(BF16)

    def ff_chunk(rows, h2, j):
        sl = slice(j * TF, (j + 1) * TF)
        gate = _nn(h2, wg_ref[:, sl])
        up = _nn(h2, wu_ref[:, sl])
        act_ref[rows, sl] = (gate * _sigmoid(gate) * up).astype(BF16)

    def down(rows):
        return x1_ref[rows, :] + mod_ref[0, 5:6, :] * _nn(act_ref[rows, :], wd_ref[...])

    def final_norm(rows, xo):
        o_ref[rows, :] = xo * lax.rsqrt(jnp.mean(xo * xo, axis=-1, keepdims=True) + EPS) * gf_ref[...]

    mix_a = mix(rows_a)
    mix_b = mix(rows_b)
    h2_a = mid_norm(rows_a, mix_a)
    ff_chunk(rows_a, h2_a, 0)
    h2_b = mid_norm(rows_b, mix_b)
    for j in range(1, n_chunks):
        ff_chunk(rows_a, h2_a, j)
    xo_a = down(rows_a)
    ff_chunk(rows_b, h2_b, 0)
    final_norm(rows_a, xo_a)
    for j in range(1, n_chunks):
        ff_chunk(rows_b, h2_b, j)
    final_norm(rows_b, down(rows_b))


def _out_ffn(x2d, y_ret, y_nsa, mod, w_out, g2, gf, wg, wu, wd, seq):
    n, d = x2d.shape
    tm = TM_FF
    d_ff = wg.shape[1]
    tiles_per_seq = seq // tm
    half_w = y_ret.shape[1]
    row = lambda i: (i, 0)
    resident = lambda a: pl.BlockSpec(a.shape, lambda i: (0, 0), pipeline_mode=pl.Buffered(1))
    return pl.pallas_call(
        _ffn_kernel,
        grid=(n // tm,),
        in_specs=[pl.BlockSpec((tm, d), row),
                  pl.BlockSpec((tm, half_w), row),
                  pl.BlockSpec((tm, half_w), row),
                  pl.BlockSpec((1, 6, d), lambda i: (i // tiles_per_seq, 0, 0)),
                  pl.BlockSpec((1, d), lambda i: (0, 0)),
                  pl.BlockSpec((1, d), lambda i: (0, 0)),
                  resident(w_out), resident(wg), resident(wu), resident(wd)],
        out_specs=pl.BlockSpec((tm, d), row),
        out_shape=jax.ShapeDtypeStruct((n, d), F32),
        scratch_shapes=[pltpu.VMEM((tm, d), F32), pltpu.VMEM((tm, d_ff), BF16)],
        compiler_params=pltpu.CompilerParams(
            dimension_semantics=("parallel",), vmem_limit_bytes=VMEM_LIMIT),
        name="out_ffn",
    )(x2d, y_ret, y_nsa, mod, g2.reshape(1, d), gf.reshape(1, d), w_out, wg, wu, wd)


def kernel(x, c, ln_mix_g, ln_ffn_g, w_ada, b_ada, w_in, cmp_pe_k, cmp_w1_k, cmp_w2_k,
           cmp_pe_v, cmp_w1_v, cmp_w2_v, w_out, w_ff_gate, w_ff_up, w_ff_down, ln_final_g):
    assert w_in.shape[0] == 1, "the final RMSNorm is fused into the (single) layer's FFN kernel"
    b, s, d = x.shape
    lane = np.arange(LANES)
    tabs = _rope_tables(np.arange(s), np.ones(LANES, bool))
    n_piece = s // CMP_STRIDE
    tabs_cmp = _rope_tables(np.arange(n_piece) * CMP_STRIDE + CMP_LEN - 1, lane < NSA_DH)

    mod = _ada(c, w_ada[0], b_ada[0]).reshape(b, 6, d)
    rq, rk, rv, rg_act, nq, kc, vc, ksx, vst, kwx, vwt, gate_t = _in_proj(
        x, ln_mix_g[0], mod, w_in, tabs, _block_onehot_table(s))
    y_ret = _retention(rq, rk, rv, rg_act, _retention_tables())
    w1, w2, pe = _compress_weights(cmp_w1_k[0], cmp_w2_k[0], cmp_w1_v[0], cmp_w2_v[0],
                                   cmp_pe_k[0], cmp_pe_v[0])
    kvcmp, vct = _compress(kc, vc, w1, w2, pe, tabs_cmp)
    y_nsa, (wo_b, wg_b, wu_b, wd_b) = _nsa_attention(
        nq, kvcmp, vct, ksx, vst, kwx, vwt, gate_t, _overlap_t(s), (w_out, w_ff_gate, w_ff_up, w_ff_down))
    out = _out_ffn(x.reshape(b * s, d), y_ret.reshape(b * s, -1), y_nsa.reshape(b * s, -1), mod,
                   wo_b, ln_ffn_g[0], ln_final_g, wg_b, wu_b, wd_b, s)
    return out.reshape(b, s, d)
```

```python
import numpy as np
import jax
import jax.numpy as jnp
from jax import lax
from jax.experimental import pallas as pl
from jax.experimental.pallas import tpu as pltpu

F32 = jnp.float32
BF16 = jnp.bfloat16

D_MODEL = 1024
RET_HEADS = 4
RET_DK = 64
RET_DV = 128
RET_CHUNK = 128
NSA_HEADS = 8
NSA_GROUPS = 2
NSA_HPG = NSA_HEADS // NSA_GROUPS
NSA_DH = 64
CMP_LEN = 32
CMP_STRIDE = 16
CMP_HIDDEN = 128
SLC_LEN = 64
SLC_TOPK = 16
WIN = 512
D_FF = ((8 * D_MODEL + 3 * 256 - 1) // (3 * 256)) * 256
ROPE_THETA = 10000.0
EPS = 1e-6
NEG = -1e30
FORCE = 1e6

LANES = 128
SUBLANES = 8
HALF = NSA_DH // 2
ONES_ROWS = 16
GATE_ROWS = 3 * NSA_HEADS

TM_IN = 1024
PROJ_W = 512
RET_GROUP = 16
Q_SCALE = NSA_DH ** -0.5 * float(np.log2(np.e))
TQ = 128
TILE_GROUP = 2
TM_FF = 1024
SUB_IN = 2
SUB_FF = 4
TF = 256
VMEM_LIMIT = 56 * 1024 * 1024

_R_RQ, _R_RK, _R_RV, _R_RG, _R_NQ = 0, 256, 512, 1024, 1536
_R_KC, _R_VC, _R_KS, _R_VS, _R_KW, _R_VW, _R_GATE = 2048, 2176, 2304, 2432, 2560, 2688, 2816
IN_COLS_K = -(-(_R_GATE + 3 * NSA_HEADS) // PROJ_W) * PROJ_W


def _sigmoid(x):
    return 1.0 / (1.0 + jnp.exp(-x))


def _nt(a, b):
    return lax.dot_general(a, b, (((1,), (1,)), ((), ())), preferred_element_type=F32)


def _nn(a, b):
    return jnp.dot(a, b, preferred_element_type=F32)


def _rope_tile(a, c, s1, s2):
    return a * c + pltpu.roll(a, HALF, 1) * s1 + pltpu.roll(a, LANES - HALF, 1) * s2


def _swap_halves(a):
    return pltpu.roll(a, LANES // 2, 1)


def _rope_tables(pos, rotary_lanes):
    pos = np.asarray(pos, np.float64)
    lane = np.arange(LANES)
    within = lane % NSA_DH
    freq = ROPE_THETA ** (-(within % HALF).astype(np.float64) / HALF)
    ang = pos[:, None] * freq[None, :]
    cos, sin = np.cos(ang), np.sin(ang)
    first = (within < HALF)[None, :]
    rot = np.asarray(rotary_lanes, bool)[None, :]
    c = np.where(rot, cos, 1.0)
    s1 = np.where(rot & ~first, sin, 0.0)
    s2 = np.where(rot & first, -sin, 0.0)
    return (jnp.asarray(c, F32), jnp.asarray(s1, F32), jnp.asarray(s2, F32))


def _block_onehot_table(seq):
    t = np.zeros((seq, LANES), np.float32)
    pos = np.arange(seq)
    t[pos, NSA_DH + pos // SLC_LEN] = 1.0
    return jnp.asarray(t)


def _retention_tables():
    h = np.arange(RET_HEADS, dtype=np.float64)
    log_g = np.log(1.0 - 2.0 ** (-5.0 - h))
    c = RET_CHUNK
    idx = np.arange(c, dtype=np.float64)
    diff = idx[:, None] - idx[None, :]
    causal = diff >= 0
    decay = np.where(causal, np.exp(log_g[:, None, None] * np.where(causal, diff, 0.0)), 0.0)
    zeta = np.exp(log_g[:, None] * (c - 1.0 - idx))
    xi = np.exp(log_g[:, None] * (idx + 1.0))
    g_chunk = np.exp(log_g * c)

    def pair_lanes(t):
        t = t.reshape(RET_HEADS // 2, 2, c)
        return np.repeat(np.transpose(t, (0, 2, 1)), RET_DK, axis=2)

    g_b = np.broadcast_to(g_chunk[:, None, None], (RET_HEADS, 1, LANES))
    return tuple(jnp.asarray(t, F32) for t in (decay, pair_lanes(zeta), pair_lanes(xi), g_b))


def _overlap_t(seq):
    n_c = seq // CMP_STRIDE - CMP_LEN // CMP_STRIDE + 1
    nb = seq // SLC_LEN
    cs = np.arange(n_c) * CMP_STRIDE
    bs = np.arange(nb) * SLC_LEN
    ov = np.maximum(np.minimum(cs[:, None] + CMP_LEN, bs[None] + SLC_LEN)
                    - np.maximum(cs[:, None], bs[None]), 0).astype(np.float64) / CMP_LEN
    ncp = seq // CMP_STRIDE
    ovp = np.zeros((ncp, nb))
    ovp[:n_c] = ov
    return jnp.asarray(ovp.T, BF16)


def _ada_kernel(c_ref, w_ref, b_ref, o_ref):
    c = c_ref[...]
    o_ref[...] = _nn(c * _sigmoid(c), w_ref[...]) + b_ref[...]


def _ada(c, w, b):
    bsz, d = c.shape
    n = w.shape[1]
    tn = 1024
    return pl.pallas_call(
        _ada_kernel,
        grid=(n // tn,),
        in_specs=[pl.BlockSpec((bsz, d), lambda j: (0, 0)),
                  pl.BlockSpec((d, tn), lambda j: (0, j)),
                  pl.BlockSpec((1, tn), lambda j: (0, j))],
        out_specs=pl.BlockSpec((bsz, tn), lambda j: (0, j)),
        out_shape=jax.ShapeDtypeStruct((bsz, n), F32),
        compiler_params=pltpu.CompilerParams(vmem_limit_bytes=VMEM_LIMIT),
        name="ada",
    )(c, w, b.reshape(1, n))


def _inproj_kernel(x_ref, g_ref, mod_ref, w_ref, c_ref, s1_ref, s2_ref, hot_ref,
                   rq_ref, rk_ref, rv_ref, rg_ref, nq_ref, kc_ref, vc_ref,
                   ksx_ref, vst_ref, kwx_ref, vwt_ref, gate_ref, wb_ref):
    @pl.when((pl.program_id(0) == 0) & (pl.program_id(1) == 0))
    def _():
        n_in = w_ref.shape[2]
        whole = (n_in // PROJ_W) * PROJ_W
        for c0 in range(0, whole, PROJ_W):
            wb_ref[:, c0:c0 + PROJ_W] = w_ref[0, :, c0:c0 + PROJ_W].astype(BF16)
        wb_ref[:, whole:] = jnp.zeros((wb_ref.shape[0], wb_ref.shape[1] - whole), BF16)
        wb_ref[:, whole:n_in] = w_ref[0, :, whole:n_in].astype(BF16)

    def modulated_norm(rows):
        x = x_ref[0, rows, :]
        y = x * lax.rsqrt(jnp.mean(x * x, axis=-1, keepdims=True) + EPS) * g_ref[...]
        return (y * (1.0 + mod_ref[0, 1:2, :]) + mod_ref[0, 0:1, :]).astype(BF16)

    def product_steps(rows, hb):
        tabs = (c_ref[rows, :], s1_ref[rows, :], s2_ref[rows, :])
        low = lax.broadcasted_iota(jnp.int32, (rows.stop - rows.start, LANES), 1) < NSA_DH

        def proj(c0):
            a = _nn(hb, wb_ref[:, c0:c0 + PROJ_W])
            return [a[:, t * LANES:(t + 1) * LANES] for t in range(PROJ_W // LANES)]

        def put(out_ref, t, value):
            out_ref[0, rows, t * LANES:(t + 1) * LANES] = value.astype(out_ref.dtype)

        def roped(tiles, scale, out_ref):
            for t, a in enumerate(tiles):
                r = _rope_tile(a, *tabs)
                put(out_ref, t, r if scale == 1.0 else r * scale)

        def per_group(tile, fill, out_ref):
            put(out_ref, 0, jnp.where(low, tile, fill))
            put(out_ref, 1, jnp.where(low, _swap_halves(tile), fill))

        def per_group_t(tile, out_ref):
            t = tile.T
            ones = jnp.ones((ONES_ROWS, t.shape[1]), out_ref.dtype)
            for g in range(NSA_GROUPS):
                out_ref[0, g, 0:NSA_DH, rows] = t[g * NSA_DH:(g + 1) * NSA_DH].astype(out_ref.dtype)
                out_ref[0, g, NSA_DH:NSA_DH + ONES_ROWS, rows] = ones

        def retention_qk():
            tiles = proj(_R_RQ)
            roped(tiles[0:2], 1.0, rq_ref)
            roped(tiles[2:4], RET_DK ** -0.5, rk_ref)

        def retention_v():
            for t, a in enumerate(proj(_R_RV)):
                put(rv_ref, t, a)

        def retention_gate():
            for t, a in enumerate(proj(_R_RG)):
                put(rg_ref, t, a * _sigmoid(a))

        def nsa_q():
            roped(proj(_R_NQ), Q_SCALE, nq_ref)

        def nsa_compress_selected():
            kc_t, vc_t, ks_t, vs_t = proj(_R_KC)
            kc_ref[0, rows, :] = kc_t
            vc_ref[0, rows, :] = vc_t
            per_group(_rope_tile(ks_t, *tabs), hot_ref[rows, :], ksx_ref)
            per_group_t(vs_t, vst_ref)

        def nsa_window_gates():
            kw_t, vw_t, gates_t, _ = proj(_R_KW)
            per_group(_rope_tile(kw_t, *tabs), 0.0, kwx_ref)
            per_group_t(vw_t, vwt_ref)
            gate_ref[0, :, rows] = _sigmoid(gates_t.T[0:GATE_ROWS])

        return [retention_qk, retention_v, retention_gate, nsa_q, nsa_compress_selected, nsa_window_gates]

    n_rows = x_ref.shape[1] // SUB_IN
    slabs = [slice(p * n_rows, (p + 1) * n_rows) for p in range(SUB_IN)]
    hb = modulated_norm(slabs[0])
    for p, rows in enumerate(slabs):
        steps = product_steps(rows, hb)
        steps[0]()
        if p + 1 < SUB_IN:
            hb = modulated_norm(slabs[p + 1])
        for step in steps[1:]:
            step()


def _in_proj(x, ln_g, mod, w_in, tabs, hot):
    b, s, d = x.shape
    tm = TM_IN
    grid = (b, s // tm)
    tab_spec = pl.BlockSpec((tm, LANES), lambda bi, j: (j, 0))

    def out(n, dtype):
        return (jax.ShapeDtypeStruct((b, s, n), dtype), pl.BlockSpec((1, tm, n), lambda bi, j: (bi, j, 0)))

    def out_t(rows, dtype):
        return (jax.ShapeDtypeStruct((b, NSA_GROUPS, rows, s), dtype),
                pl.BlockSpec((1, NSA_GROUPS, rows, tm), lambda bi, j: (bi, 0, 0, j)))

    vt_rows = NSA_DH + ONES_ROWS
    outs = [out(256, BF16), out(256, BF16), out(512, BF16), out(512, BF16), out(512, BF16),
            out(LANES, F32), out(LANES, F32),
            out(256, BF16), out_t(vt_rows, BF16), out(256, BF16), out_t(vt_rows, BF16),
            (jax.ShapeDtypeStruct((b, GATE_ROWS, s), F32),
             pl.BlockSpec((1, GATE_ROWS, tm), lambda bi, j: (bi, 0, j)))]
    return pl.pallas_call(
        _inproj_kernel,
        grid=grid,
        in_specs=[pl.BlockSpec((1, tm, d), lambda bi, j: (bi, j, 0)),
                  pl.BlockSpec((1, d), lambda bi, j: (0, 0)),
                  pl.BlockSpec((1, 6, d), lambda bi, j: (bi, 0, 0)),
                  pl.BlockSpec((1,) + w_in.shape[1:], lambda bi, j: (0, 0, 0), pipeline_mode=pl.Buffered(1)),
                  tab_spec, tab_spec, tab_spec, tab_spec],
        out_specs=[o[1] for o in outs],
        out_shape=[o[0] for o in outs],
        scratch_shapes=[pltpu.VMEM((d, IN_COLS_K), BF16)],
        compiler_params=pltpu.CompilerParams(
            dimension_semantics=("arbitrary", "arbitrary"), vmem_limit_bytes=VMEM_LIMIT),
        name="in_proj",
    )(x, ln_g.reshape(1, d), mod, w_in, *tabs, hot)


def _ret_kernel(q_ref, k_ref, v_ref, rg_ref, dec_ref, zeta_ref, xi_ref, gch_ref, o_ref, kv_ref, prev_ref):
    c = RET_CHUNK
    n_chunks = q_ref.shape[1] // c
    low = lax.broadcasted_iota(jnp.int32, (c, LANES), 1) < RET_DK

    def chunk_rows(n):
        return pl.ds(pl.multiple_of(n * c, c), c)

    def head_cols(h):
        return slice(h * RET_DV, (h + 1) * RET_DV)

    def kv_body(it, carry):
        kz_t = {}
        for j in range(RET_GROUP):
            rows = chunk_rows(it * RET_GROUP + j)
            for p in range(RET_HEADS // 2):
                pair = slice(p * LANES, (p + 1) * LANES)
                kz_t[j, p] = (k_ref[0, rows, pair].astype(F32) * zeta_ref[p]).T.astype(BF16)
        for j in range(RET_GROUP):
            n = it * RET_GROUP + j
            for h in range(RET_HEADS):
                kv_ref[h, n] = _nn(kz_t[j, h // 2], v_ref[0, chunk_rows(n), head_cols(h)])
        return carry

    lax.fori_loop(0, n_chunks // RET_GROUP, kv_body, 0)

    for h in range(RET_HEADS):
        def scan_body(n, st, h=h):
            prev_ref[h, n] = st.astype(prev_ref.dtype)
            return st * gch_ref[h] + kv_ref[h, n]
        lax.fori_loop(0, n_chunks, scan_body, jnp.zeros((LANES, RET_DV), F32))

    def out_body(it, carry):
        chains = [(j, h) for j in range(RET_GROUP) for h in range(RET_HEADS)]
        chunk = lambda j: it * RET_GROUP + j
        q_own, qx_own, att, ys = {}, {}, {}, {}
        for j, h in chains:
            p, e = divmod(h, 2)
            pair = slice(p * LANES, (p + 1) * LANES)
            mine = low if e == 0 else jnp.logical_not(low)
            q2 = q_ref[0, chunk_rows(chunk(j)), pair].astype(F32)
            q_own[j, h] = jnp.where(mine, q2, 0.0).astype(BF16)
            qx_own[j, h] = jnp.where(mine, q2 * xi_ref[p], 0.0).astype(BF16)
        for j, h in chains:
            pair = slice((h // 2) * LANES, (h // 2 + 1) * LANES)
            att[j, h] = _nt(q_own[j, h], k_ref[0, chunk_rows(chunk(j)), pair])
        for j, h in chains:
            lhs = jnp.concatenate([(att[j, h] * dec_ref[h]).astype(BF16), qx_own[j, h]], axis=1)
            rhs = jnp.concatenate([v_ref[0, chunk_rows(chunk(j)), head_cols(h)], prev_ref[h, chunk(j)]], axis=0)
            ys[j, h] = _nn(lhs, rhs)
        for j, h in chains:
            y = ys[j, h]
            yn = y * lax.rsqrt(jnp.mean(y * y, axis=-1, keepdims=True) + EPS)
            gate = rg_ref[0, chunk_rows(chunk(j)), head_cols(h)].astype(F32)
            o_ref[0, chunk_rows(chunk(j)), head_cols(h)] = (yn * gate).astype(o_ref.dtype)
        return carry

    lax.fori_loop(0, n_chunks // RET_GROUP, out_body, 0)


def _retention(rq, rk, rv, rg_act, tables):
    b, s, _ = rq.shape
    decay, zeta_p, xi_p, g_b = tables
    whole = lambda a: pl.BlockSpec(a.shape, lambda bi: (0,) * a.ndim)
    row = lambda a: pl.BlockSpec((1,) + a.shape[1:], lambda bi: (bi, 0, 0))
    return pl.pallas_call(
        _ret_kernel,
        grid=(b,),
        in_specs=[row(rq), row(rk), row(rv), row(rg_act),
                  whole(decay), whole(zeta_p), whole(xi_p), whole(g_b)],
        out_specs=row(rv),
        out_shape=jax.ShapeDtypeStruct(rv.shape, BF16),
        scratch_shapes=[pltpu.VMEM((RET_HEADS, s // RET_CHUNK, LANES, RET_DV), F32),
                        pltpu.VMEM((RET_HEADS, s // RET_CHUNK, LANES, RET_DV), BF16)],
        compiler_params=pltpu.CompilerParams(
            dimension_semantics=("parallel",), vmem_limit_bytes=VMEM_LIMIT),
        name="retention",
    )(rq, rk, rv, rg_act, decay, zeta_p, xi_p, g_b)


def _cmp_kernel(kc_ref, vc_ref, w1_ref, pe_ref, w2_ref, c_ref, s1_ref, s2_ref, kv_ref, vt_ref):
    n_piece = kc_ref.shape[1] // CMP_STRIDE
    halves = CMP_LEN // CMP_STRIDE
    lhs = {}
    for t, src in enumerate((kc_ref, vc_ref)):
        rows = [src[0, pl.ds(r, n_piece, stride=CMP_STRIDE), :] for r in range(CMP_STRIDE)]
        for half in range(halves):
            pe0 = half * CMP_STRIDE
            lhs[t, half] = jnp.concatenate(
                [(rows[r] + pe_ref[t, pe0 + r:pe0 + r + 1, :]).astype(BF16) for r in range(CMP_STRIDE)], axis=1)
    part = {key: _nn(lhs[key], w1_ref[key[0], key[1]]) for key in lhs}
    out = jnp.zeros((n_piece, kv_ref.shape[2]), F32)
    for t in range(2):
        hid = part[t, 0]
        for half in range(1, halves):
            hid = hid + pltpu.roll(part[t, half], n_piece - half, 0)
        out = out + _nn((hid * _sigmoid(hid)).astype(BF16), w2_ref[t])
    for t in range(out.shape[1] // LANES):
        sl = slice(t * LANES, (t + 1) * LANES)
        kv = _rope_tile(out[:, sl], c_ref[...], s1_ref[...], s2_ref[...])
        kv_ref[0, :, sl] = kv.astype(kv_ref.dtype)
        vt_ref[0, t] = kv.T[NSA_DH:2 * NSA_DH].astype(vt_ref.dtype)


def _compress_weights(w1_k, w2_k, w1_v, w2_v, pe_k, pe_v):
    dh, hid = NSA_DH, CMP_HIDDEN

    def first(w1):
        w = w1.reshape(CMP_LEN, dh, hid)
        z = jnp.zeros_like(w)
        return jnp.concatenate([jnp.concatenate([w, z], axis=2), jnp.concatenate([z, w], axis=2)], axis=1)

    def second(w2, off):
        z = jnp.zeros_like(w2)
        rows = []
        for g in range(NSA_GROUPS):
            c = [z, z, z, z]
            c[2 * g + off] = w2
            rows.append(jnp.concatenate(c, axis=1))
        return jnp.concatenate(rows, axis=0)

    w1 = jnp.stack([first(w1_k), first(w1_v)]).astype(BF16).reshape(
        2, CMP_LEN // CMP_STRIDE, CMP_STRIDE * NSA_GROUPS * dh, NSA_GROUPS * hid)
    w2 = jnp.stack([second(w2_k, 0), second(w2_v, 1)]).astype(BF16)
    pe = jnp.stack([jnp.concatenate([pe_k, pe_k], axis=1), jnp.concatenate([pe_v, pe_v], axis=1)])
    return w1, w2, pe


def _compress(kc, vc, w1, w2, pe, tabs_cmp):
    b, s, w = kc.shape
    n_piece = s // CMP_STRIDE
    const2 = lambda bi: (0, 0)
    src = pl.BlockSpec((1, s, w), lambda bi: (bi, 0, 0))
    dst = pl.BlockSpec((1, n_piece, 2 * w), lambda bi: (bi, 0, 0))
    shape = jax.ShapeDtypeStruct((b, n_piece, 2 * w), BF16)
    return pl.pallas_call(
        _cmp_kernel,
        grid=(b,),
        in_specs=[src, src,
                  pl.BlockSpec(w1.shape, lambda bi: (0, 0, 0, 0)),
                  pl.BlockSpec(pe.shape, lambda bi: (0, 0, 0)),
                  pl.BlockSpec(w2.shape, lambda bi: (0, 0, 0)),
                  pl.BlockSpec((n_piece, LANES), const2),
                  pl.BlockSpec((n_piece, LANES), const2),
                  pl.BlockSpec((n_piece, LANES), const2)],
        out_specs=[dst, pl.BlockSpec((1, NSA_GROUPS, NSA_DH, n_piece), lambda bi: (bi, 0, 0, 0))],
        out_shape=[shape, jax.ShapeDtypeStruct((b, NSA_GROUPS, NSA_DH, n_piece), BF16)],
        compiler_params=pltpu.CompilerParams(
            dimension_semantics=("parallel",), vmem_limit_bytes=VMEM_LIMIT),
        name="compress",
    )(kc, vc, w1, pe, w2, *tabs_cmp)


def _nsa_kernel(q_ref, kcmp_ref, vct_ref, ksx_ref, vst_ref, kwx_ref, vwt_ref, gate_ref, ovt_ref, *rest):
    n_w = (len(rest) - 1) // 2
    o_ref = rest[n_w]
    for w_src, w_dst in zip(rest[:n_w], rest[n_w + 1:]):
        w_dst[...] = w_src[...].astype(w_dst.dtype)

    tq = TQ
    seq = q_ref.shape[1]
    nb = seq // SLC_LEN
    n_cmp = kcmp_ref.shape[1]
    hpg, dh = NSA_HPG, NSA_DH
    assert WIN % tq == 0 and seq % tq == 0
    group = pl.program_id(1)

    low = lax.broadcasted_iota(jnp.int32, (tq, LANES), 1) < dh
    eye = jnp.where(lax.broadcasted_iota(jnp.int32, (tq, tq), 0)
                    == lax.broadcasted_iota(jnp.int32, (tq, tq), 1), 1.0, 0.0).astype(BF16)
    blk = lax.broadcasted_iota(jnp.int32, (nb, tq), 0)
    col = lax.broadcasted_iota(jnp.int32, (nb, tq), 1)
    crow = lax.broadcasted_iota(jnp.int32, (n_cmp, tq), 0)
    ccol = lax.broadcasted_iota(jnp.int32, (n_cmp, tq), 1)
    kcm = kcmp_ref[0]
    vct = vct_ref[0, 0]
    ovt = ovt_ref[...]

    def per_head(x):
        return jnp.concatenate([x] * hpg, axis=1)

    key_off = lax.broadcasted_iota(jnp.int32, (tq, tq), 0)
    qry_off = lax.broadcasted_iota(jnp.int32, (tq, tq), 1)
    not_after = per_head(jnp.where(key_off <= qry_off, 0.0, NEG))
    inside_win = per_head(jnp.where(key_off > qry_off, 0.0, NEG))

    def masked(s, first_key, t0, windowed):
        blocks = []
        for r in range(0, s.shape[0], tq):
            blk_s = s[r:r + tq]
            if first_key + r == t0:
                blk_s = blk_s + not_after
            elif windowed and first_key + r == t0 - WIN:
                blk_s = blk_s + inside_win
            blocks.append(blk_s)
        return jnp.concatenate(blocks, axis=0)

    def normalise(acc):
        return acc[0:dh] / acc[dh:dh + 1]

    def select_blocks(psum, t0):
        tcol = col + t0
        bcausal = blk * SLC_LEN <= tcol
        n_live = (t0 + tq - 1) // SLC_LEN + 1
        top_n = min(SLC_TOPK, nb)
        if n_live <= top_n:
            bias = jnp.where(bcausal, 0.0, NEG)
        else:
            p_hi = psum.astype(BF16)
            p_lo = (psum - p_hi.astype(F32)).astype(BF16)
            imp = _nn(ovt, p_hi) + _nn(ovt, p_lo)
            cur = tcol // SLC_LEN
            forced = (blk == 0) | (blk == cur) | (blk == cur - 1)
            imp = jnp.where(bcausal, jnp.where(forced, FORCE, imp), NEG)
            rank = jnp.zeros((nb, tq), F32)
            for j in range(n_live):
                r = imp[j:j + 1, :]
                rank = rank + jnp.where(blk > j, jnp.where(r >= imp, 1.0, 0.0), jnp.where(r > imp, 1.0, 0.0))
            bias = jnp.where((rank < float(top_n)) & bcausal, 0.0, NEG)
        feat = jnp.concatenate([jnp.zeros((dh, tq), F32), bias,
                                jnp.zeros((LANES - dh - nb, tq), F32)], axis=0).astype(BF16)
        return _nt(eye, feat)

    class TileGroup:
        def __init__(self, tiles):
            self.tiles = tiles
            self.wstart = {t0: max(t0 - WIN, 0) for t0 in tiles}
            self.wkeys = {t0: slice(self.wstart[t0], t0 + tq) for t0 in tiles}
            self.skeys = {t0: slice(0, t0 + tq) for t0 in tiles}

        def scores_window_compressed(self):
            self.heads, self.qs = {}, {}
            for t0 in self.tiles:
                qf = q_ref[0, t0:t0 + tq, :].astype(F32)
                hl = []
                for hh in range(hpg):
                    t = qf[:, (hh // 2) * LANES:(hh // 2 + 1) * LANES]
                    if hh % 2 == 1:
                        t = _swap_halves(t)
                    hl.append(jnp.where(low, t, 0.0))
                self.heads[t0] = hl
                self.qs[t0] = jnp.concatenate(hl, axis=0).astype(BF16)
            self.s_w = {t0: _nt(kwx_ref[0, self.wkeys[t0], :], self.qs[t0]) for t0 in self.tiles}
            self.s_c = {t0: _nt(kcm, self.qs[t0]) for t0 in self.tiles}

        def select_and_scores_selected(self):
            self.p_cmp, qsel = {}, {}
            for t0 in self.tiles:
                cmask = (crow * CMP_STRIDE + (CMP_LEN - 1)) <= (ccol + t0)
                p_all = []
                psum = jnp.zeros((n_cmp, tq), F32)
                for hh in range(hpg):
                    sh = jnp.where(cmask, self.s_c[t0][:, hh * tq:(hh + 1) * tq], NEG)
                    e = jnp.exp2(sh - jnp.max(sh, axis=0, keepdims=True))
                    p = jnp.where(cmask, e / jnp.sum(e, axis=0, keepdims=True), 0.0)
                    psum = psum + p
                    p_all.append(p.astype(BF16))
                self.p_cmp[t0] = jnp.concatenate(p_all, axis=1)
                qbias = select_blocks(psum, t0)
                qsel[t0] = jnp.concatenate([hd + qbias for hd in self.heads[t0]], axis=0).astype(BF16)
            self.s_s = {t0: _nt(ksx_ref[0, self.skeys[t0], :], qsel[t0]) for t0 in self.tiles}

        def outputs_window_compressed(self):
            e_w = {}
            for t0 in self.tiles:
                sw = masked(self.s_w[t0], self.wstart[t0], t0, True)
                e_w[t0] = jnp.exp2(sw - jnp.max(sw, axis=0, keepdims=True)).astype(BF16)
            self.o_win = {t0: normalise(_nn(vwt_ref[0, 0, :, self.wkeys[t0]], e_w[t0])) for t0 in self.tiles}
            self.o_cmp = {t0: _nn(vct, self.p_cmp[t0]) for t0 in self.tiles}

        def outputs_selected_and_store(self):
            e_s = {}
            for t0 in self.tiles:
                ss = masked(self.s_s[t0], 0, t0, False)
                e_s[t0] = jnp.exp2(ss - jnp.max(ss, axis=0, keepdims=True)).astype(BF16)
            o_sel = {t0: normalise(_nn(vst_ref[0, 0, :, self.skeys[t0]], e_s[t0])) for t0 in self.tiles}
            for t0 in self.tiles:
                gt = gate_ref[0, :, t0:t0 + tq]
                outs = []
                for hh in range(hpg):
                    hc = slice(hh * tq, (hh + 1) * tq)

                    def gate_row(branch):
                        by_group = [gt[branch * NSA_HEADS + g * hpg + hh:branch * NSA_HEADS + g * hpg + hh + 1, :]
                                    for g in range(NSA_GROUPS)]
                        row = by_group[-1]
                        for g in range(NSA_GROUPS - 2, -1, -1):
                            row = jnp.where(group == g, by_group[g], row)
                        return row

                    outs.append(gate_row(0) * self.o_cmp[t0][:, hc] + gate_row(1) * o_sel[t0][:, hc]
                                + gate_row(2) * self.o_win[t0][:, hc])
                o_ref[0, t0:t0 + tq, :] = jnp.concatenate(outs, axis=0).T.astype(o_ref.dtype)

    starts = list(range(0, seq, tq))
    groups = [TileGroup(starts[i:i + TILE_GROUP]) for i in range(0, len(starts), TILE_GROUP)]
    n = len(groups)
    groups[0].scores_window_compressed()
    for i, g in enumerate(groups):
        if i + 1 < n:
            groups[i + 1].scores_window_compressed()
        g.select_and_scores_selected()
        g.outputs_window_compressed()
        if i > 0:
            groups[i - 1].outputs_selected_and_store()
    groups[n - 1].outputs_selected_and_store()


def _nsa_attention(nq, kvcmp, vct, ksx, vst, kwx, vwt, gate_t, ovt, weights):
    b, s, _ = nq.shape
    n_cmp = kvcmp.shape[1]
    gw = NSA_HPG * NSA_DH
    steps = b * NSA_GROUPS
    per_group = lambda rows, width: pl.BlockSpec((1, rows, width), lambda bi, g: (bi, 0, g))
    per_group_t = lambda a: pl.BlockSpec((1, 1) + a.shape[2:], lambda bi, g: (bi, g, 0, 0))
    assert all(w.shape[1] % (steps * 16) == 0 for w in weights)
    slab = lambda w: pl.BlockSpec((1, w.shape[1] // steps, w.shape[2]),
                                  lambda bi, g: (0, bi * NSA_GROUPS + g, 0))
    outs = pl.pallas_call(
        _nsa_kernel,
        grid=(b, NSA_GROUPS),
        in_specs=[per_group(s, gw),
                  per_group(n_cmp, LANES), per_group_t(vct),
                  per_group(s, LANES), per_group_t(vst), per_group(s, LANES), per_group_t(vwt),
                  pl.BlockSpec((1,) + gate_t.shape[1:], lambda bi, g: (bi, 0, 0)),
                  pl.BlockSpec(ovt.shape, lambda bi, g: (0, 0))] + [slab(w) for w in weights],
        out_specs=[per_group(s, gw)] + [slab(w) for w in weights],
        out_shape=[jax.ShapeDtypeStruct((b, s, NSA_HEADS * NSA_DH), BF16)]
                  + [jax.ShapeDtypeStruct(w.shape, BF16) for w in weights],
        compiler_params=pltpu.CompilerParams(
            dimension_semantics=("parallel", "parallel"), vmem_limit_bytes=VMEM_LIMIT),
        name="nsa_attn",
    )(nq, kvcmp, vct, ksx, vst, kwx, vwt, gate_t, ovt, *weights)
    return outs[0], [w[0] for w in outs[1:]]


def _ffn_kernel(x_ref, yr_ref, yn_ref, mod_ref, g2_ref, gf_ref, wo_ref, wg_ref, wu_ref, wd_ref,
                o_ref, x1_ref, act_ref):
    half_w = yr_ref.shape[1]
    d_ff = wg_ref.shape[1]
    n_chunks = d_ff // TF
    n_rows = x_ref.shape[0] // SUB_FF
    slabs = [slice(p * n_rows, (p + 1) * n_rows) for p in range(SUB_FF)]

    def mix(rows):
        return _nn(yr_ref[rows, :], wo_ref[0:half_w, :]) + _nn(yn_ref[rows, :], wo_ref[half_w:2 * half_w, :])

    def mid_norm(rows, mixed):
        x1 = x_ref[rows, :] + mod_ref[0, 2:3, :] * mixed
        x1_ref[rows, :] = x1
        y = x1 * lax.rsqrt(jnp.mean(x1 * x1, axis=-1, keepdims=True) + EPS) * g2_ref[...]
        return (y * (1.0 + mod_ref[0, 4:5, :]) + mod_ref[0, 3:4, :]).astype(BF16)

    def ff_chunk(rows, h2, j):
        sl = slice(j * TF, (j + 1) * TF)
        gate = _nn(h2, wg_ref[:, sl])
        up = _nn(h2, wu_ref[:, sl])
        act_ref[rows, sl] = (gate * _sigmoid(gate) * up).astype(BF16)

    def down(rows):
        return x1_ref[rows, :] + mod_ref[0, 5:6, :] * _nn(act_ref[rows, :], wd_ref[...])

    def final_norm(rows, xo):
        o_ref[rows, :] = xo * lax.rsqrt(jnp.mean(xo * xo, axis=-1, keepdims=True) + EPS) * gf_ref[...]

    mixed = [mix(rows) for rows in slabs]
    h2 = mid_norm(slabs[0], mixed[0])
    xo_prev = None
    for p, rows in enumerate(slabs):
        ff_chunk(rows, h2, 0)
        h2_next = mid_norm(slabs[p + 1], mixed[p + 1]) if p + 1 < SUB_FF else None
        if xo_prev is not None:
            final_norm(slabs[p - 1], xo_prev)
        for j in range(1, n_chunks):
            ff_chunk(rows, h2, j)
        xo_prev = down(rows)
        h2 = h2_next
    final_norm(slabs[-1], xo_prev)


def _out_ffn(x2d, y_ret, y_nsa, mod, w_out, g2, gf, wg, wu, wd, seq):
    n, d = x2d.shape
    tm = TM_FF
    d_ff = wg.shape[1]
    tiles_per_seq = seq // tm
    half_w = y_ret.shape[1]
    row = lambda i: (i, 0)
    resident = lambda a: pl.BlockSpec(a.shape, lambda i: (0, 0), pipeline_mode=pl.Buffered(1))
    return pl.pallas_call(
        _ffn_kernel,
        grid=(n // tm,),
        in_specs=[pl.BlockSpec((tm, d), row),
                  pl.BlockSpec((tm, half_w), row),
                  pl.BlockSpec((tm, half_w), row),
                  pl.BlockSpec((1, 6, d), lambda i: (i // tiles_per_seq, 0, 0)),
                  pl.BlockSpec((1, d), lambda i: (0, 0)),
                  pl.BlockSpec((1, d), lambda i: (0, 0)),
                  resident(w_out), resident(wg), resident(wu), resident(wd)],
        out_specs=pl.BlockSpec((tm, d), row),
        out_shape=jax.ShapeDtypeStruct((n, d), F32),
        scratch_shapes=[pltpu.VMEM((tm, d), F32), pltpu.VMEM((tm, d_ff), BF16)],
        compiler_params=pltpu.CompilerParams(
            dimension_semantics=("parallel",), vmem_limit_bytes=VMEM_LIMIT),
        name="out_ffn",
    )(x2d, y_ret, y_nsa, mod, g2.reshape(1, d), gf.reshape(1, d), w_out, wg, wu, wd)


def kernel(x, c, ln_mix_g, ln_ffn_g, w_ada, b_ada, w_in, cmp_pe_k, cmp_w1_k, cmp_w2_k,
           cmp_pe_v, cmp_w1_v, cmp_w2_v, w_out, w_ff_gate, w_ff_up, w_ff_down, ln_final_g):
    assert w_in.shape[0] == 1, "the final RMSNorm is fused into the (single) layer's FFN kernel"
    b, s, d = x.shape
    lane = np.arange(LANES)
    tabs = _rope_tables(np.arange(s), np.ones(LANES, bool))
    n_piece = s // CMP_STRIDE
    tabs_cmp = _rope_tables(np.arange(n_piece) * CMP_STRIDE + CMP_LEN - 1, lane < NSA_DH)

    mod = _ada(c, w_ada[0], b_ada[0]).reshape(b, 6, d)
    rq, rk, rv, rg_act, nq, kc, vc, ksx, vst, kwx, vwt, gate_t = _in_proj(
        x, ln_mix_g[0], mod, w_in, tabs, _block_onehot_table(s))
    y_ret = _retention(rq, rk, rv, rg_act, _retention_tables())
    w1, w2, pe = _compress_weights(cmp_w1_k[0], cmp_w2_k[0], cmp_w1_v[0], cmp_w2_v[0],
                                   cmp_pe_k[0], cmp_pe_v[0])
    kvcmp, vct = _compress(kc, vc, w1, w2, pe, tabs_cmp)
    y_nsa, (wo_b, wg_b, wu_b, wd_b) = _nsa_attention(
        nq, kvcmp, vct, ksx, vst, kwx, vwt, gate_t, _overlap_t(s), (w_out, w_ff_gate, w_ff_up, w_ff_down))
    out = _out_ffn(x.reshape(b * s, d), y_ret.reshape(b * s, -1), y_nsa.reshape(b * s, -1), mod,
                   wo_b, ln_ffn_g[0], ln_final_g, wg_b, wu_b, wd_b, s)
    return out.reshape(b, s, d)
```

```python
import numpy as np
import jax
import jax.numpy as jnp
from jax import lax
from jax.experimental import pallas as pl
from jax.experimental.pallas import tpu as pltpu

F32 = jnp.float32
BF16 = jnp.bfloat16

D_MODEL = 1024
RET_HEADS = 4
RET_DK = 64
RET_DV = 128
RET_CHUNK = 128
NSA_HEADS = 8
NSA_GROUPS = 2
NSA_HPG = NSA_HEADS // NSA_GROUPS
NSA_DH = 64
CMP_LEN = 32
CMP_STRIDE = 16
CMP_HIDDEN = 128
SLC_LEN = 64
SLC_TOPK = 16
WIN = 512
D_FF = ((8 * D_MODEL + 3 * 256 - 1) // (3 * 256)) * 256
ROPE_THETA = 10000.0
EPS = 1e-6
NEG = -1e30
FORCE = 1e6

LANES = 128
SUBLANES = 8
HALF = NSA_DH // 2
ONES_ROWS = 16
GATE_ROWS = 3 * NSA_HEADS

TM_IN = 1024
PROJ_W = 512
RET_GROUP = 16
Q_SCALE = NSA_DH ** -0.5 * float(np.log2(np.e))
TQ = 128
TILE_GROUP = 2
TM_FF = 512
SUB_IN = 2
SUB_FF = 2
TF = 256
VMEM_LIMIT = 56 * 1024 * 1024

_R_RQ, _R_RK, _R_RV, _R_RG, _R_NQ = 0, 256, 512, 1024, 1536
_R_KC, _R_VC, _R_KS, _R_VS, _R_KW, _R_VW, _R_GATE = 2048, 2176, 2304, 2432, 2560, 2688, 2816
IN_COLS_K = -(-(_R_GATE + 3 * NSA_HEADS) // PROJ_W) * PROJ_W


def _sigmoid(x):
    return 1.0 / (1.0 + jnp.exp(-x))


def _nt(a, b):
    return lax.dot_general(a, b, (((1,), (1,)), ((), ())), preferred_element_type=F32)


def _nn(a, b):
    return jnp.dot(a, b, preferred_element_type=F32)


def _rope_tile(a, c, s1, s2):
    return a * c + pltpu.roll(a, HALF, 1) * s1 + pltpu.roll(a, LANES - HALF, 1) * s2


def _swap_halves(a):
    return pltpu.roll(a, LANES // 2, 1)


def _rope_tables(pos, rotary_lanes):
    pos = np.asarray(pos, np.float64)
    lane = np.arange(LANES)
    within = lane % NSA_DH
    freq = ROPE_THETA ** (-(within % HALF).astype(np.float64) / HALF)
    ang = pos[:, None] * freq[None, :]
    cos, sin = np.cos(ang), np.sin(ang)
    first = (within < HALF)[None, :]
    rot = np.asarray(rotary_lanes, bool)[None, :]
    c = np.where(rot, cos, 1.0)
    s1 = np.where(rot & ~first, sin, 0.0)
    s2 = np.where(rot & first, -sin, 0.0)
    return (jnp.asarray(c, F32), jnp.asarray(s1, F32), jnp.asarray(s2, F32))


def _block_onehot_table(seq):
    t = np.zeros((seq, LANES), np.float32)
    pos = np.arange(seq)
    t[pos, NSA_DH + pos // SLC_LEN] = 1.0
    return jnp.asarray(t)


def _retention_tables():
    h = np.arange(RET_HEADS, dtype=np.float64)
    log_g = np.log(1.0 - 2.0 ** (-5.0 - h))
    c = RET_CHUNK
    idx = np.arange(c, dtype=np.float64)
    diff = idx[:, None] - idx[None, :]
    causal = diff >= 0
    decay = np.where(causal, np.exp(log_g[:, None, None] * np.where(causal, diff, 0.0)), 0.0)
    zeta = np.exp(log_g[:, None] * (c - 1.0 - idx))
    xi = np.exp(log_g[:, None] * (idx + 1.0))
    g_chunk = np.exp(log_g * c)

    def pair_lanes(t):
        t = t.reshape(RET_HEADS // 2, 2, c)
        return np.repeat(np.transpose(t, (0, 2, 1)), RET_DK, axis=2)

    g_b = np.broadcast_to(g_chunk[:, None, None], (RET_HEADS, 1, LANES))
    return tuple(jnp.asarray(t, F32) for t in (decay, pair_lanes(zeta), pair_lanes(xi), g_b))


def _overlap_t(seq):
    n_c = seq // CMP_STRIDE - CMP_LEN // CMP_STRIDE + 1
    nb = seq // SLC_LEN
    cs = np.arange(n_c) * CMP_STRIDE
    bs = np.arange(nb) * SLC_LEN
    ov = np.maximum(np.minimum(cs[:, None] + CMP_LEN, bs[None] + SLC_LEN)
                    - np.maximum(cs[:, None], bs[None]), 0).astype(np.float64) / CMP_LEN
    ncp = seq // CMP_STRIDE
    ovp = np.zeros((ncp, nb))
    ovp[:n_c] = ov
    return jnp.asarray(ovp.T, BF16)


def _ada_kernel(c_ref, w_ref, b_ref, o_ref):
    c = c_ref[...]
    o_ref[...] = _nn(c * _sigmoid(c), w_ref[...]) + b_ref[...]


def _ada(c, w, b):
    bsz, d = c.shape
    n = w.shape[1]
    tn = 1024
    return pl.pallas_call(
        _ada_kernel,
        grid=(n // tn,),
        in_specs=[pl.BlockSpec((bsz, d), lambda j: (0, 0)),
                  pl.BlockSpec((d, tn), lambda j: (0, j)),
                  pl.BlockSpec((1, tn), lambda j: (0, j))],
        out_specs=pl.BlockSpec((bsz, tn), lambda j: (0, j)),
        out_shape=jax.ShapeDtypeStruct((bsz, n), F32),
        compiler_params=pltpu.CompilerParams(vmem_limit_bytes=VMEM_LIMIT),
        name="ada",
    )(c, w, b.reshape(1, n))


def _inproj_kernel(x_ref, g_ref, mod_ref, w_ref, c_ref, s1_ref, s2_ref, hot_ref,
                   rq_ref, rk_ref, rv_ref, rg_ref, nq_ref, kc_ref, vc_ref,
                   ksx_ref, vst_ref, kwx_ref, vwt_ref, gate_ref, wb_ref):
    @pl.when((pl.program_id(0) == 0) & (pl.program_id(1) == 0))
    def _():
        n_in = w_ref.shape[2]
        whole = (n_in // PROJ_W) * PROJ_W
        for c0 in range(0, whole, PROJ_W):
            wb_ref[:, c0:c0 + PROJ_W] = w_ref[0, :, c0:c0 + PROJ_W].astype(BF16)
        wb_ref[:, whole:] = jnp.zeros((wb_ref.shape[0], wb_ref.shape[1] - whole), BF16)
        wb_ref[:, whole:n_in] = w_ref[0, :, whole:n_in].astype(BF16)

    def modulated_norm(rows):
        x = x_ref[0, rows, :]
        y = x * lax.rsqrt(jnp.mean(x * x, axis=-1, keepdims=True) + EPS) * g_ref[...]
        return (y * (1.0 + mod_ref[0, 1:2, :]) + mod_ref[0, 0:1, :]).astype(BF16)

    def product_steps(rows, hb):
        tabs = (c_ref[rows, :], s1_ref[rows, :], s2_ref[rows, :])
        low = lax.broadcasted_iota(jnp.int32, (rows.stop - rows.start, LANES), 1) < NSA_DH

        def proj(c0):
            a = _nn(hb, wb_ref[:, c0:c0 + PROJ_W])
            return [a[:, t * LANES:(t + 1) * LANES] for t in range(PROJ_W // LANES)]

        def put(out_ref, t, value):
            out_ref[0, rows, t * LANES:(t + 1) * LANES] = value.astype(out_ref.dtype)

        def roped(tiles, scale, out_ref):
            for t, a in enumerate(tiles):
                r = _rope_tile(a, *tabs)
                put(out_ref, t, r if scale == 1.0 else r * scale)

        def per_group(tile, fill, out_ref):
            put(out_ref, 0, jnp.where(low, tile, fill))
            put(out_ref, 1, jnp.where(low, _swap_halves(tile), fill))

        def per_group_t(tile, out_ref):
            t = tile.T
            ones = jnp.ones((ONES_ROWS, t.shape[1]), out_ref.dtype)
            for g in range(NSA_GROUPS):
                out_ref[0, g, 0:NSA_DH, rows] = t[g * NSA_DH:(g + 1) * NSA_DH].astype(out_ref.dtype)
                out_ref[0, g, NSA_DH:NSA_DH + ONES_ROWS, rows] = ones

        def retention_qk():
            tiles = proj(_R_RQ)
            roped(tiles[0:2], 1.0, rq_ref)
            roped(tiles[2:4], RET_DK ** -0.5, rk_ref)

        def retention_v():
            for t, a in enumerate(proj(_R_RV)):
                put(rv_ref, t, a)

        def retention_gate():
            for t, a in enumerate(proj(_R_RG)):
                put(rg_ref, t, a * _sigmoid(a))

        def nsa_q():
            roped(proj(_R_NQ), Q_SCALE, nq_ref)

        def nsa_compress_selected():
            kc_t, vc_t, ks_t, vs_t = proj(_R_KC)
            kc_ref[0, rows, :] = kc_t
            vc_ref[0, rows, :] = vc_t
            per_group(_rope_tile(ks_t, *tabs), hot_ref[rows, :], ksx_ref)
            per_group_t(vs_t, vst_ref)

        def nsa_window_gates():
            kw_t, vw_t, gates_t, _ = proj(_R_KW)
            per_group(_rope_tile(kw_t, *tabs), 0.0, kwx_ref)
            per_group_t(vw_t, vwt_ref)
            gate_ref[0, :, rows] = _sigmoid(gates_t.T[0:GATE_ROWS])

        return [retention_qk, retention_v, retention_gate, nsa_q, nsa_compress_selected, nsa_window_gates]

    n_rows = x_ref.shape[1] // SUB_IN
    slabs = [slice(p * n_rows, (p + 1) * n_rows) for p in range(SUB_IN)]
    hb = modulated_norm(slabs[0])
    for p, rows in enumerate(slabs):
        steps = product_steps(rows, hb)
        steps[0]()
        if p + 1 < SUB_IN:
            hb = modulated_norm(slabs[p + 1])
        for step in steps[1:]:
            step()


def _in_proj(x, ln_g, mod, w_in, tabs, hot):
    b, s, d = x.shape
    tm = TM_IN
    grid = (b, s // tm)
    tab_spec = pl.BlockSpec((tm, LANES), lambda bi, j: (j, 0))

    def out(n, dtype):
        return (jax.ShapeDtypeStruct((b, s, n), dtype), pl.BlockSpec((1, tm, n), lambda bi, j: (bi, j, 0)))

    def out_t(rows, dtype):
        return (jax.ShapeDtypeStruct((b, NSA_GROUPS, rows, s), dtype),
                pl.BlockSpec((1, NSA_GROUPS, rows, tm), lambda bi, j: (bi, 0, 0, j)))

    vt_rows = NSA_DH + ONES_ROWS
    outs = [out(256, BF16), out(256, BF16), out(512, BF16), out(512, BF16), out(512, BF16),
            out(LANES, F32), out(LANES, F32),
            out(256, BF16), out_t(vt_rows, BF16), out(256, BF16), out_t(vt_rows, BF16),
            (jax.ShapeDtypeStruct((b, GATE_ROWS, s), F32),
             pl.BlockSpec((1, GATE_ROWS, tm), lambda bi, j: (bi, 0, j)))]
    return pl.pallas_call(
        _inproj_kernel,
        grid=grid,
        in_specs=[pl.BlockSpec((1, tm, d), lambda bi, j: (bi, j, 0)),
                  pl.BlockSpec((1, d), lambda bi, j: (0, 0)),
                  pl.BlockSpec((1, 6, d), lambda bi, j: (bi, 0, 0)),
                  pl.BlockSpec((1,) + w_in.shape[1:], lambda bi, j: (0, 0, 0), pipeline_mode=pl.Buffered(1)),
                  tab_spec, tab_spec, tab_spec, tab_spec],
        out_specs=[o[1] for o in outs],
        out_shape=[o[0] for o in outs],
        scratch_shapes=[pltpu.VMEM((d, IN_COLS_K), BF16)],
        compiler_params=pltpu.CompilerParams(
            dimension_semantics=("arbitrary", "arbitrary"), vmem_limit_bytes=VMEM_LIMIT),
        name="in_proj",
    )(x, ln_g.reshape(1, d), mod, w_in, *tabs, hot)


def _ret_kernel(q_ref, k_ref, v_ref, rg_ref, dec_ref, zeta_ref, xi_ref, gch_ref, o_ref, kv_ref, prev_ref):
    c = RET_CHUNK
    n_chunks = q_ref.shape[1] // c
    low = lax.broadcasted_iota(jnp.int32, (c, LANES), 1) < RET_DK

    def chunk_rows(n):
        return pl.ds(pl.multiple_of(n * c, c), c)

    def head_cols(h):
        return slice(h * RET_DV, (h + 1) * RET_DV)

    def kv_body(it, carry):
        kz_t = {}
        for j in range(RET_GROUP):
            rows = chunk_rows(it * RET_GROUP + j)
            for p in range(RET_HEADS // 2):
                pair = slice(p * LANES, (p + 1) * LANES)
                kz_t[j, p] = (k_ref[0, rows, pair].astype(F32) * zeta_ref[p]).T.astype(BF16)
        for j in range(RET_GROUP):
            n = it * RET_GROUP + j
            for h in range(RET_HEADS):
                kv_ref[h, n] = _nn(kz_t[j, h // 2], v_ref[0, chunk_rows(n), head_cols(h)])
        return carry

    lax.fori_loop(0, n_chunks // RET_GROUP, kv_body, 0)

    for h in range(RET_HEADS):
        def scan_body(n, st, h=h):
            prev_ref[h, n] = st.astype(prev_ref.dtype)
            return st * gch_ref[h] + kv_ref[h, n]
        lax.fori_loop(0, n_chunks, scan_body, jnp.zeros((LANES, RET_DV), F32))

    def out_body(it, carry):
        chains = [(j, h) for j in range(RET_GROUP) for h in range(RET_HEADS)]
        chunk = lambda j: it * RET_GROUP + j
        q_own, qx_own, att, ys = {}, {}, {}, {}
        for j, h in chains:
            p, e = divmod(h, 2)
            pair = slice(p * LANES, (p + 1) * LANES)
            mine = low if e == 0 else jnp.logical_not(low)
            q2 = q_ref[0, chunk_rows(chunk(j)), pair].astype(F32)
            q_own[j, h] = jnp.where(mine, q2, 0.0).astype(BF16)
            qx_own[j, h] = jnp.where(mine, q2 * xi_ref[p], 0.0).astype(BF16)
        for j, h in chains:
            pair = slice((h // 2) * LANES, (h // 2 + 1) * LANES)
            att[j, h] = _nt(q_own[j, h], k_ref[0, chunk_rows(chunk(j)), pair])
        for j, h in chains:
            lhs = jnp.concatenate([(att[j, h] * dec_ref[h]).astype(BF16), qx_own[j, h]], axis=1)
            rhs = jnp.concatenate([v_ref[0, chunk_rows(chunk(j)), head_cols(h)], prev_ref[h, chunk(j)]], axis=0)
            ys[j, h] = _nn(lhs, rhs)
        for j, h in chains:
            y = ys[j, h]
            yn = y * lax.rsqrt(jnp.mean(y * y, axis=-1, keepdims=True) + EPS)
            gate = rg_ref[0, chunk_rows(chunk(j)), head_cols(h)].astype(F32)
            o_ref[0, chunk_rows(chunk(j)), head_cols(h)] = (yn * gate).astype(o_ref.dtype)
        return carry

    lax.fori_loop(0, n_chunks // RET_GROUP, out_body, 0)


def _retention(rq, rk, rv, rg_act, tables):
    b, s, _ = rq.shape
    decay, zeta_p, xi_p, g_b = tables
    whole = lambda a: pl.BlockSpec(a.shape, lambda bi: (0,) * a.ndim)
    row = lambda a: pl.BlockSpec((1,) + a.shape[1:], lambda bi: (bi, 0, 0))
    return pl.pallas_call(
        _ret_kernel,
        grid=(b,),
        in_specs=[row(rq), row(rk), row(rv), row(rg_act),
                  whole(decay), whole(zeta_p), whole(xi_p), whole(g_b)],
        out_specs=row(rv),
        out_shape=jax.ShapeDtypeStruct(rv.shape, BF16),
        scratch_shapes=[pltpu.VMEM((RET_HEADS, s // RET_CHUNK, LANES, RET_DV), F32),
                        pltpu.VMEM((RET_HEADS, s // RET_CHUNK, LANES, RET_DV), BF16)],
        compiler_params=pltpu.CompilerParams(
            dimension_semantics=("parallel",), vmem_limit_bytes=VMEM_LIMIT),
        name="retention",
    )(rq, rk, rv, rg_act, decay, zeta_p, xi_p, g_b)


def _cmp_kernel(kc_ref, vc_ref, w1_ref, pe_ref, w2_ref, c_ref, s1_ref, s2_ref, kv_ref, vt_ref):
    n_piece = kc_ref.shape[1] // CMP_STRIDE
    halves = CMP_LEN // CMP_STRIDE
    lhs = {}
    for t, src in enumerate((kc_ref, vc_ref)):
        rows = [src[0, pl.ds(r, n_piece, stride=CMP_STRIDE), :] for r in range(CMP_STRIDE)]
        for half in range(halves):
            pe0 = half * CMP_STRIDE
            lhs[t, half] = jnp.concatenate(
                [(rows[r] + pe_ref[t, pe0 + r:pe0 + r + 1, :]).astype(BF16) for r in range(CMP_STRIDE)], axis=1)
    part = {key: _nn(lhs[key], w1_ref[key[0], key[1]]) for key in lhs}
    out = jnp.zeros((n_piece, kv_ref.shape[2]), F32)
    for t in range(2):
        hid = part[t, 0]
        for half in range(1, halves):
            hid = hid + pltpu.roll(part[t, half], n_piece - half, 0)
        out = out + _nn((hid * _sigmoid(hid)).astype(BF16), w2_ref[t])
    for t in range(out.shape[1] // LANES):
        sl = slice(t * LANES, (t + 1) * LANES)
        kv = _rope_tile(out[:, sl], c_ref[...], s1_ref[...], s2_ref[...])
        kv_ref[0, :, sl] = kv.astype(kv_ref.dtype)
        vt_ref[0, t] = kv.T[NSA_DH:2 * NSA_DH].astype(vt_ref.dtype)


def _compress_weights(w1_k, w2_k, w1_v, w2_v, pe_k, pe_v):
    dh, hid = NSA_DH, CMP_HIDDEN

    def first(w1):
        w = w1.reshape(CMP_LEN, dh, hid)
        z = jnp.zeros_like(w)
        return jnp.concatenate([jnp.concatenate([w, z], axis=2), jnp.concatenate([z, w], axis=2)], axis=1)

    def second(w2, off):
        z = jnp.zeros_like(w2)
        rows = []
        for g in range(NSA_GROUPS):
            c = [z, z, z, z]
            c[2 * g + off] = w2
            rows.append(jnp.concatenate(c, axis=1))
        return jnp.concatenate(rows, axis=0)

    w1 = jnp.stack([first(w1_k), first(w1_v)]).astype(BF16).reshape(
        2, CMP_LEN // CMP_STRIDE, CMP_STRIDE * NSA_GROUPS * dh, NSA_GROUPS * hid)
    w2 = jnp.stack([second(w2_k, 0), second(w2_v, 1)]).astype(BF16)
    pe = jnp.stack([jnp.concatenate([pe_k, pe_k], axis=1), jnp.concatenate([pe_v, pe_v], axis=1)])
    return w1, w2, pe


def _compress(kc, vc, w1, w2, pe, tabs_cmp):
    b, s, w = kc.shape
    n_piece = s // CMP_STRIDE
    const2 = lambda bi: (0, 0)
    src = pl.BlockSpec((1, s, w), lambda bi: (bi, 0, 0))
    dst = pl.BlockSpec((1, n_piece, 2 * w), lambda bi: (bi, 0, 0))
    shape = jax.ShapeDtypeStruct((b, n_piece, 2 * w), BF16)
    return pl.pallas_call(
        _cmp_kernel,
        grid=(b,),
        in_specs=[src, src,
                  pl.BlockSpec(w1.shape, lambda bi: (0, 0, 0, 0)),
                  pl.BlockSpec(pe.shape, lambda bi: (0, 0, 0)),
                  pl.BlockSpec(w2.shape, lambda bi: (0, 0, 0)),
                  pl.BlockSpec((n_piece, LANES), const2),
                  pl.BlockSpec((n_piece, LANES), const2),
                  pl.BlockSpec((n_piece, LANES), const2)],
        out_specs=[dst, pl.BlockSpec((1, NSA_GROUPS, NSA_DH, n_piece), lambda bi: (bi, 0, 0, 0))],
        out_shape=[shape, jax.ShapeDtypeStruct((b, NSA_GROUPS, NSA_DH, n_piece), BF16)],
        compiler_params=pltpu.CompilerParams(
            dimension_semantics=("parallel",), vmem_limit_bytes=VMEM_LIMIT),
        name="compress",
    )(kc, vc, w1, pe, w2, *tabs_cmp)


def _nsa_kernel(q_ref, kcmp_ref, vct_ref, ksx_ref, vst_ref, kwx_ref, vwt_ref, gate_ref, ovt_ref, *rest):
    n_w = (len(rest) - 1) // 2
    o_ref = rest[n_w]
    for w_src, w_dst in zip(rest[:n_w], rest[n_w + 1:]):
        w_dst[...] = w_src[...].astype(w_dst.dtype)

    tq = TQ
    seq = q_ref.shape[1]
    nb = seq // SLC_LEN
    n_cmp = kcmp_ref.shape[1]
    hpg, dh = NSA_HPG, NSA_DH
    assert WIN % tq == 0 and seq % tq == 0
    group = pl.program_id(1)

    low = lax.broadcasted_iota(jnp.int32, (tq, LANES), 1) < dh
    eye = jnp.where(lax.broadcasted_iota(jnp.int32, (tq, tq), 0)
                    == lax.broadcasted_iota(jnp.int32, (tq, tq), 1), 1.0, 0.0).astype(BF16)
    blk = lax.broadcasted_iota(jnp.int32, (nb, tq), 0)
    col = lax.broadcasted_iota(jnp.int32, (nb, tq), 1)
    crow = lax.broadcasted_iota(jnp.int32, (n_cmp, tq), 0)
    ccol = lax.broadcasted_iota(jnp.int32, (n_cmp, tq), 1)
    kcm = kcmp_ref[0]
    vct = vct_ref[0, 0]
    ovt = ovt_ref[...]

    def per_head(x):
        return jnp.concatenate([x] * hpg, axis=1)

    key_off = lax.broadcasted_iota(jnp.int32, (tq, tq), 0)
    qry_off = lax.broadcasted_iota(jnp.int32, (tq, tq), 1)
    not_after = per_head(jnp.where(key_off <= qry_off, 0.0, NEG))
    inside_win = per_head(jnp.where(key_off > qry_off, 0.0, NEG))

    def masked(s, first_key, t0, windowed):
        blocks = []
        for r in range(0, s.shape[0], tq):
            blk_s = s[r:r + tq]
            if first_key + r == t0:
                blk_s = blk_s + not_after
            elif windowed and first_key + r == t0 - WIN:
                blk_s = blk_s + inside_win
            blocks.append(blk_s)
        return jnp.concatenate(blocks, axis=0)

    def normalise(acc):
        return acc[0:dh] / acc[dh:dh + 1]

    def select_blocks(psum, t0):
        tcol = col + t0
        bcausal = blk * SLC_LEN <= tcol
        n_live = (t0 + tq - 1) // SLC_LEN + 1
        top_n = min(SLC_TOPK, nb)
        if n_live <= top_n:
            bias = jnp.where(bcausal, 0.0, NEG)
        else:
            p_hi = psum.astype(BF16)
            p_lo = (psum - p_hi.astype(F32)).astype(BF16)
            imp = _nn(ovt, p_hi) + _nn(ovt, p_lo)
            cur = tcol // SLC_LEN
            forced = (blk == 0) | (blk == cur) | (blk == cur - 1)
            imp = jnp.where(bcausal, jnp.where(forced, FORCE, imp), NEG)
            rank = jnp.zeros((nb, tq), F32)
            for j in range(n_live):
                r = imp[j:j + 1, :]
                rank = rank + jnp.where(blk > j, jnp.where(r >= imp, 1.0, 0.0), jnp.where(r > imp, 1.0, 0.0))
            bias = jnp.where((rank < float(top_n)) & bcausal, 0.0, NEG)
        feat = jnp.concatenate([jnp.zeros((dh, tq), F32), bias,
                                jnp.zeros((LANES - dh - nb, tq), F32)], axis=0).astype(BF16)
        return _nt(eye, feat)

    class TileGroup:
        def __init__(self, tiles):
            self.tiles = tiles
            self.wstart = {t0: max(t0 - WIN, 0) for t0 in tiles}
            self.wkeys = {t0: slice(self.wstart[t0], t0 + tq) for t0 in tiles}
            self.skeys = {t0: slice(0, t0 + tq) for t0 in tiles}

        def scores_window_compressed(self):
            self.heads, self.qs = {}, {}
            for t0 in self.tiles:
                qf = q_ref[0, t0:t0 + tq, :].astype(F32)
                hl = []
                for hh in range(hpg):
                    t = qf[:, (hh // 2) * LANES:(hh // 2 + 1) * LANES]
                    if hh % 2 == 1:
                        t = _swap_halves(t)
                    hl.append(jnp.where(low, t, 0.0))
                self.heads[t0] = hl
                self.qs[t0] = jnp.concatenate(hl, axis=0).astype(BF16)
            self.s_w = {t0: _nt(kwx_ref[0, self.wkeys[t0], :], self.qs[t0]) for t0 in self.tiles}
            self.s_c = {t0: _nt(kcm, self.qs[t0]) for t0 in self.tiles}

        def select_and_scores_selected(self):
            self.p_cmp, qsel = {}, {}
            for t0 in self.tiles:
                cmask = (crow * CMP_STRIDE + (CMP_LEN - 1)) <= (ccol + t0)
                p_all = []
                psum = jnp.zeros((n_cmp, tq), F32)
                for hh in range(hpg):
                    sh = jnp.where(cmask, self.s_c[t0][:, hh * tq:(hh + 1) * tq], NEG)
                    e = jnp.exp2(sh - jnp.max(sh, axis=0, keepdims=True))
                    p = jnp.where(cmask, e / jnp.sum(e, axis=0, keepdims=True), 0.0)
                    psum = psum + p
                    p_all.append(p.astype(BF16))
                self.p_cmp[t0] = jnp.concatenate(p_all, axis=1)
                qbias = select_blocks(psum, t0)
                qsel[t0] = jnp.concatenate([hd + qbias for hd in self.heads[t0]], axis=0).astype(BF16)
            self.s_s = {t0: _nt(ksx_ref[0, self.skeys[t0], :], qsel[t0]) for t0 in self.tiles}

        def outputs_window_compressed(self):
            e_w = {}
            for t0 in self.tiles:
                sw = masked(self.s_w[t0], self.wstart[t0], t0, True)
                e_w[t0] = jnp.exp2(sw - jnp.max(sw, axis=0, keepdims=True)).astype(BF16)
            self.o_win = {t0: normalise(_nn(vwt_ref[0, 0, :, self.wkeys[t0]], e_w[t0])) for t0 in self.tiles}
            self.o_cmp = {t0: _nn(vct, self.p_cmp[t0]) for t0 in self.tiles}

        def outputs_selected_and_store(self):
            e_s = {}
            for t0 in self.tiles:
                ss = masked(self.s_s[t0], 0, t0, False)
                e_s[t0] = jnp.exp2(ss - jnp.max(ss, axis=0, keepdims=True)).astype(BF16)
            o_sel = {t0: normalise(_nn(vst_ref[0, 0, :, self.skeys[t0]], e_s[t0])) for t0 in self.tiles}
            for t0 in self.tiles:
                gt = gate_ref[0, :, t0:t0 + tq]
                outs = []
                for hh in range(hpg):
                    hc = slice(hh * tq, (hh + 1) * tq)

                    def gate_row(branch):
                        by_group = [gt[branch * NSA_HEADS + g * hpg + hh:branch * NSA_HEADS + g * hpg + hh + 1, :]
                                    for g in range(NSA_GROUPS)]
                        row = by_group[-1]
                        for g in range(NSA_GROUPS - 2, -1, -1):
                            row = jnp.where(group == g, by_group[g], row)
                        return row

                    outs.append(gate_row(0) * self.o_cmp[t0][:, hc] + gate_row(1) * o_sel[t0][:, hc]
                                + gate_row(2) * self.o_win[t0][:, hc])
                o_ref[0, t0:t0 + tq, :] = jnp.concatenate(outs, axis=0).T.astype(o_ref.dtype)

    starts = list(range(0, seq, tq))
    groups = [TileGroup(starts[i:i + TILE_GROUP]) for i in range(0, len(starts), TILE_GROUP)]
    n = len(groups)
    groups[0].scores_window_compressed()
    for i, g in enumerate(groups):
        if i + 1 < n:
            groups[i + 1].scores_window_compressed()
        g.select_and_scores_selected()
        g.outputs_window_compressed()
        if i > 0:
            groups[i - 1].outputs_selected_and_store()
    groups[n - 1].outputs_selected_and_store()


def _nsa_attention(nq, kvcmp, vct, ksx, vst, kwx, vwt, gate_t, ovt, weights):
    b, s, _ = nq.shape
    n_cmp = kvcmp.shape[1]
    gw = NSA_HPG * NSA_DH
    steps = b * NSA_GROUPS
    per_group = lambda rows, width: pl.BlockSpec((1, rows, width), lambda bi, g: (bi, 0, g))
    per_group_t = lambda a: pl.BlockSpec((1, 1) + a.shape[2:], lambda bi, g: (bi, g, 0, 0))
    assert all(w.shape[1] % (steps * 16) == 0 for w in weights)
    slab = lambda w: pl.BlockSpec((1, w.shape[1] // steps, w.shape[2]),
                                  lambda bi, g: (0, bi * NSA_GROUPS + g, 0))
    outs = pl.pallas_call(
        _nsa_kernel,
        grid=(b, NSA_GROUPS),
        in_specs=[per_group(s, gw),
                  per_group(n_cmp, LANES), per_group_t(vct),
                  per_group(s, LANES), per_group_t(vst), per_group(s, LANES), per_group_t(vwt),
                  pl.BlockSpec((1,) + gate_t.shape[1:], lambda bi, g: (bi, 0, 0)),
                  pl.BlockSpec(ovt.shape, lambda bi, g: (0, 0))] + [slab(w) for w in weights],
        out_specs=[per_group(s, gw)] + [slab(w) for w in weights],
        out_shape=[jax.ShapeDtypeStruct((b, s, NSA_HEADS * NSA_DH), BF16)]
                  + [jax.ShapeDtypeStruct(w.shape, BF16) for w in weights],
        compiler_params=pltpu.CompilerParams(
            dimension_semantics=("parallel", "parallel"), vmem_limit_bytes=VMEM_LIMIT),
        name="nsa_attn",
    )(nq, kvcmp, vct, ksx, vst, kwx, vwt, gate_t, ovt, *weights)
    return outs[0], [w[0] for w in outs[1:]]


def _ffn_kernel(x_ref, yr_ref, yn_ref, mod_ref, g2_ref, gf_ref, wo_ref, wg_ref, wu_ref, wd_ref,
                o_ref, x1_ref, act_ref):
    half_w = yr_ref.shape[1]
    d_ff = wg_ref.shape[1]
    n_chunks = d_ff // TF
    n_rows = x_ref.shape[0] // SUB_FF
    slabs = [slice(p * n_rows, (p + 1) * n_rows) for p in range(SUB_FF)]

    def mix(rows):
        return _nn(yr_ref[rows, :], wo_ref[0:half_w, :]) + _nn(yn_ref[rows, :], wo_ref[half_w:2 * half_w, :])

    def mid_norm(rows, mixed):
        x1 = x_ref[rows, :] + mod_ref[0, 2:3, :] * mixed
        x1_ref[rows, :] = x1
        y = x1 * lax.rsqrt(jnp.mean(x1 * x1, axis=-1, keepdims=True) + EPS) * g2_ref[...]
        return (y * (1.0 + mod_ref[0, 4:5, :]) + mod_ref[0, 3:4, :]).astype(BF16)

    def ff_chunk(rows, h2, j):
        sl = slice(j * TF, (j + 1) * TF)
        gate = _nn(h2, wg_ref[:, sl])
        up = _nn(h2, wu_ref[:, sl])
        act_ref[rows, sl] = (gate * _sigmoid(gate) * up).astype(BF16)

    def down(rows):
        return x1_ref[rows, :] + mod_ref[0, 5:6, :] * _nn(act_ref[rows, :], wd_ref[...])

    def final_norm(rows, xo):
        o_ref[rows, :] = xo * lax.rsqrt(jnp.mean(xo * xo, axis=-1, keepdims=True) + EPS) * gf_ref[...]

    mixed = [mix(rows) for rows in slabs]
    h2 = mid_norm(slabs[0], mixed[0])
    xo_prev = None
    for p, rows in enumerate(slabs):
        ff_chunk(rows, h2, 0)
        h2_next = mid_norm(slabs[p + 1], mixed[p + 1]) if p + 1 < SUB_FF else None
        if xo_prev is not None:
            final_norm(slabs[p - 1], xo_prev)
        for j in range(1, n_chunks):
            ff_chunk(rows, h2, j)
        xo_prev = down(rows)
        h2 = h2_next
    final_norm(slabs[-1], xo_prev)


def _out_ffn(x2d, y_ret, y_nsa, mod, w_out, g2, gf, wg, wu, wd, seq):
    n, d = x2d.shape
    tm = TM_FF
    d_ff = wg.shape[1]
    tiles_per_seq = seq // tm
    half_w = y_ret.shape[1]
    row = lambda i: (i, 0)
    resident = lambda a: pl.BlockSpec(a.shape, lambda i: (0, 0), pipeline_mode=pl.Buffered(1))
    return pl.pallas_call(
        _ffn_kernel,
        grid=(n // tm,),
        in_specs=[pl.BlockSpec((tm, d), row),
                  pl.BlockSpec((tm, half_w), row),
                  pl.BlockSpec((tm, half_w), row),
                  pl.BlockSpec((1, 6, d), lambda i: (i // tiles_per_seq, 0, 0)),
                  pl.BlockSpec((1, d), lambda i: (0, 0)),
                  pl.BlockSpec((1, d), lambda i: (0, 0)),
                  resident(w_out), resident(wg), resident(wu), resident(wd)],
        out_specs=pl.BlockSpec((tm, d), row),
        out_shape=jax.ShapeDtypeStruct((n, d), F32),
        scratch_shapes=[pltpu.VMEM((tm, d), F32), pltpu.VMEM((tm, d_ff), BF16)],
        compiler_params=pltpu.CompilerParams(
            dimension_semantics=("parallel",), vmem_limit_bytes=VMEM_LIMIT),
        name="out_ffn",
    )(x2d, y_ret, y_nsa, mod, g2.reshape(1, d), gf.reshape(1, d), w_out, wg, wu, wd)


def kernel(x, c, ln_mix_g, ln_ffn_g, w_ada, b_ada, w_in, cmp_pe_k, cmp_w1_k, cmp_w2_k,
           cmp_pe_v, cmp_w1_v, cmp_w2_v, w_out, w_ff_gate, w_ff_up, w_ff_down, ln_final_g):
    assert w_in.shape[0] == 1, "the final RMSNorm is fused into the (single) layer's FFN kernel"
    b, s, d = x.shape
    lane = np.arange(LANES)
    tabs = _rope_tables(np.arange(s), np.ones(LANES, bool))
    n_piece = s // CMP_STRIDE
    tabs_cmp = _rope_tables(np.arange(n_piece) * CMP_STRIDE + CMP_LEN - 1, lane < NSA_DH)

    mod = _ada(c, w_ada[0], b_ada[0]).reshape(b, 6, d)
    rq, rk, rv, rg_act, nq, kc, vc, ksx, vst, kwx, vwt, gate_t = _in_proj(
        x, ln_mix_g[0], mod, w_in, tabs, _block_onehot_table(s))
    y_ret = _retention(rq, rk, rv, rg_act, _retention_tables())
    w1, w2, pe = _compress_weights(cmp_w1_k[0], cmp_w2_k[0], cmp_w1_v[0], cmp_w2_v[0],
                                   cmp_pe_k[0], cmp_pe_v[0])
    kvcmp, vct = _compress(kc, vc, w1, w2, pe, tabs_cmp)
    y_nsa, (wo_b, wg_b, wu_b, wd_b) = _nsa_attention(
        nq, kvcmp, vct, ksx, vst, kwx, vwt, gate_t, _overlap_t(s), (w_out, w_ff_gate, w_ff_up, w_ff_down))
    out = _out_ffn(x.reshape(b * s, d), y_ret.reshape(b * s, -1), y_nsa.reshape(b * s, -1), mod,
                   wo_b, ln_ffn_g[0], ln_final_g, wg_b, wu_b, wd_b, s)
    return out.reshape(b, s, d)
```

```python
import numpy as np
import jax
import jax.numpy as jnp
from jax import lax
from jax.experimental import pallas as pl
from jax.experimental.pallas import tpu as pltpu

F32 = jnp.float32
BF16 = jnp.bfloat16

D_MODEL = 1024
RET_HEADS = 4
RET_DK = 64
RET_DV = 128
RET_CHUNK = 128
NSA_HEADS = 8
NSA_GROUPS = 2
NSA_HPG = NSA_HEADS // NSA_GROUPS
NSA_DH = 64
CMP_LEN = 32
CMP_STRIDE = 16
CMP_HIDDEN = 128
SLC_LEN = 64
SLC_TOPK = 16
WIN = 512
D_FF = ((8 * D_MODEL + 3 * 256 - 1) // (3 * 256)) * 256
ROPE_THETA = 10000.0
EPS = 1e-6
NEG = -1e30
FORCE = 1e6

LANES = 128
SUBLANES = 8
HALF = NSA_DH // 2
ONES_ROWS = 16
GATE_ROWS = 3 * NSA_HEADS

TM_IN = 1024
PROJ_W = 512
RET_GROUP = 8
Q_SCALE = NSA_DH ** -0.5 * float(np.log2(np.e))
TQ = 128
TILE_GROUP = 2
TM_FF = 512
SUB_IN = 2
SUB_FF = 2
TF = 256
VMEM_LIMIT = 56 * 1024 * 1024

_R_RQ, _R_RK, _R_RV, _R_RG, _R_NQ = 0, 256, 512, 1024, 1536
_R_KC, _R_VC, _R_KS, _R_VS, _R_KW, _R_VW, _R_GATE = 2048, 2176, 2304, 2432, 2560, 2688, 2816
IN_COLS_K = -(-(_R_GATE + 3 * NSA_HEADS) // PROJ_W) * PROJ_W


def _sigmoid(x):
    return 1.0 / (1.0 + jnp.exp(-x))


def _nt(a, b):
    return lax.dot_general(a, b, (((1,), (1,)), ((), ())), preferred_element_type=F32)


def _nn(a, b):
    return jnp.dot(a, b, preferred_element_type=F32)


def _rope_tile(a, c, s1, s2):
    return a * c + pltpu.roll(a, HALF, 1) * s1 + pltpu.roll(a, LANES - HALF, 1) * s2


def _swap_halves(a):
    return pltpu.roll(a, LANES // 2, 1)


def _rope_tables(pos, rotary_lanes):
    pos = np.asarray(pos, np.float64)
    lane = np.arange(LANES)
    within = lane % NSA_DH
    freq = ROPE_THETA ** (-(within % HALF).astype(np.float64) / HALF)
    ang = pos[:, None] * freq[None, :]
    cos, sin = np.cos(ang), np.sin(ang)
    first = (within < HALF)[None, :]
    rot = np.asarray(rotary_lanes, bool)[None, :]
    c = np.where(rot, cos, 1.0)
    s1 = np.where(rot & ~first, sin, 0.0)
    s2 = np.where(rot & first, -sin, 0.0)
    return (jnp.asarray(c, F32), jnp.asarray(s1, F32), jnp.asarray(s2, F32))


def _block_onehot_table(seq):
    t = np.zeros((seq, LANES), np.float32)
    pos = np.arange(seq)
    t[pos, NSA_DH + pos // SLC_LEN] = 1.0
    return jnp.asarray(t)


def _retention_tables():
    h = np.arange(RET_HEADS, dtype=np.float64)
    log_g = np.log(1.0 - 2.0 ** (-5.0 - h))
    c = RET_CHUNK
    idx = np.arange(c, dtype=np.float64)
    diff = idx[:, None] - idx[None, :]
    causal = diff >= 0
    decay = np.where(causal, np.exp(log_g[:, None, None] * np.where(causal, diff, 0.0)), 0.0)
    zeta = np.exp(log_g[:, None] * (c - 1.0 - idx))
    xi = np.exp(log_g[:, None] * (idx + 1.0))
    g_chunk = np.exp(log_g * c)

    def pair_lanes(t):
        t = t.reshape(RET_HEADS // 2, 2, c)
        return np.repeat(np.transpose(t, (0, 2, 1)), RET_DK, axis=2)

    g_b = np.broadcast_to(g_chunk[:, None, None], (RET_HEADS, 1, LANES))
    return tuple(jnp.asarray(t, F32) for t in (decay, pair_lanes(zeta), pair_lanes(xi), g_b))


def _overlap_t(seq):
    n_c = seq // CMP_STRIDE - CMP_LEN // CMP_STRIDE + 1
    nb = seq // SLC_LEN
    cs = np.arange(n_c) * CMP_STRIDE
    bs = np.arange(nb) * SLC_LEN
    ov = np.maximum(np.minimum(cs[:, None] + CMP_LEN, bs[None] + SLC_LEN)
                    - np.maximum(cs[:, None], bs[None]), 0).astype(np.float64) / CMP_LEN
    ncp = seq // CMP_STRIDE
    ovp = np.zeros((ncp, nb))
    ovp[:n_c] = ov
    return jnp.asarray(ovp.T, BF16)


def _ada_kernel(c_ref, w_ref, b_ref, o_ref):
    c = c_ref[...]
    o_ref[...] = _nn(c * _sigmoid(c), w_ref[...]) + b_ref[...]


def _ada(c, w, b):
    bsz, d = c.shape
    n = w.shape[1]
    tn = 1024
    return pl.pallas_call(
        _ada_kernel,
        grid=(n // tn,),
        in_specs=[pl.BlockSpec((bsz, d), lambda j: (0, 0)),
                  pl.BlockSpec((d, tn), lambda j: (0, j)),
                  pl.BlockSpec((1, tn), lambda j: (0, j))],
        out_specs=pl.BlockSpec((bsz, tn), lambda j: (0, j)),
        out_shape=jax.ShapeDtypeStruct((bsz, n), F32),
        compiler_params=pltpu.CompilerParams(vmem_limit_bytes=VMEM_LIMIT),
        name="ada",
    )(c, w, b.reshape(1, n))


def _inproj_kernel(x_ref, g_ref, mod_ref, w_ref, c_ref, s1_ref, s2_ref, hot_ref,
                   rq_ref, rk_ref, rv_ref, rg_ref, nq_ref, kc_ref, vc_ref,
                   ksx_ref, vst_ref, kwx_ref, vwt_ref, gate_ref, wb_ref):
    @pl.when((pl.program_id(0) == 0) & (pl.program_id(1) == 0))
    def _():
        n_in = w_ref.shape[2]
        whole = (n_in // PROJ_W) * PROJ_W
        for c0 in range(0, whole, PROJ_W):
            wb_ref[:, c0:c0 + PROJ_W] = w_ref[0, :, c0:c0 + PROJ_W].astype(BF16)
        wb_ref[:, whole:] = jnp.zeros((wb_ref.shape[0], wb_ref.shape[1] - whole), BF16)
        wb_ref[:, whole:n_in] = w_ref[0, :, whole:n_in].astype(BF16)

    def modulated_norm(rows):
        x = x_ref[0, rows, :]
        y = x * lax.rsqrt(jnp.mean(x * x, axis=-1, keepdims=True) + EPS) * g_ref[...]
        return (y * (1.0 + mod_ref[0, 1:2, :]) + mod_ref[0, 0:1, :]).astype(BF16)

    def product_steps(rows, hb):
        tabs = (c_ref[rows, :], s1_ref[rows, :], s2_ref[rows, :])
        low = lax.broadcasted_iota(jnp.int32, (rows.stop - rows.start, LANES), 1) < NSA_DH

        def proj(c0):
            a = _nn(hb, wb_ref[:, c0:c0 + PROJ_W])
            return [a[:, t * LANES:(t + 1) * LANES] for t in range(PROJ_W // LANES)]

        def put(out_ref, t, value):
            out_ref[0, rows, t * LANES:(t + 1) * LANES] = value.astype(out_ref.dtype)

        def roped(tiles, scale, out_ref):
            for t, a in enumerate(tiles):
                r = _rope_tile(a, *tabs)
                put(out_ref, t, r if scale == 1.0 else r * scale)

        def per_group(tile, fill, out_ref):
            put(out_ref, 0, jnp.where(low, tile, fill))
            put(out_ref, 1, jnp.where(low, _swap_halves(tile), fill))

        def per_group_t(tile, out_ref):
            t = tile.T
            ones = jnp.ones((ONES_ROWS, t.shape[1]), out_ref.dtype)
            for g in range(NSA_GROUPS):
                out_ref[0, g, 0:NSA_DH, rows] = t[g * NSA_DH:(g + 1) * NSA_DH].astype(out_ref.dtype)
                out_ref[0, g, NSA_DH:NSA_DH + ONES_ROWS, rows] = ones

        def retention_qk():
            tiles = proj(_R_RQ)
            roped(tiles[0:2], 1.0, rq_ref)
            roped(tiles[2:4], RET_DK ** -0.5, rk_ref)

        def retention_v():
            for t, a in enumerate(proj(_R_RV)):
                put(rv_ref, t, a)

        def retention_gate():
            for t, a in enumerate(proj(_R_RG)):
                put(rg_ref, t, a * _sigmoid(a))

        def nsa_q():
            roped(proj(_R_NQ), Q_SCALE, nq_ref)

        def nsa_compress_selected():
            kc_t, vc_t, ks_t, vs_t = proj(_R_KC)
            kc_ref[0, rows, :] = kc_t
            vc_ref[0, rows, :] = vc_t
            per_group(_rope_tile(ks_t, *tabs), hot_ref[rows, :], ksx_ref)
            per_group_t(vs_t, vst_ref)

        def nsa_window_gates():
            kw_t, vw_t, gates_t, _ = proj(_R_KW)
            per_group(_rope_tile(kw_t, *tabs), 0.0, kwx_ref)
            per_group_t(vw_t, vwt_ref)
            gate_ref[0, :, rows] = _sigmoid(gates_t.T[0:GATE_ROWS])

        return [retention_qk, retention_v, retention_gate, nsa_q, nsa_compress_selected, nsa_window_gates]

    n_rows = x_ref.shape[1] // SUB_IN
    slabs = [slice(p * n_rows, (p + 1) * n_rows) for p in range(SUB_IN)]
    hb = modulated_norm(slabs[0])
    for p, rows in enumerate(slabs):
        steps = product_steps(rows, hb)
        steps[0]()
        if p + 1 < SUB_IN:
            hb = modulated_norm(slabs[p + 1])
        for step in steps[1:]:
            step()


def _in_proj(x, ln_g, mod, w_in, tabs, hot):
    b, s, d = x.shape
    tm = TM_IN
    grid = (b, s // tm)
    tab_spec = pl.BlockSpec((tm, LANES), lambda bi, j: (j, 0))

    def out(n, dtype):
        return (jax.ShapeDtypeStruct((b, s, n), dtype), pl.BlockSpec((1, tm, n), lambda bi, j: (bi, j, 0)))

    def out_t(rows, dtype):
        return (jax.ShapeDtypeStruct((b, NSA_GROUPS, rows, s), dtype),
                pl.BlockSpec((1, NSA_GROUPS, rows, tm), lambda bi, j: (bi, 0, 0, j)))

    vt_rows = NSA_DH + ONES_ROWS
    outs = [out(256, BF16), out(256, BF16), out(512, BF16), out(512, BF16), out(512, BF16),
            out(LANES, F32), out(LANES, F32),
            out(256, BF16), out_t(vt_rows, BF16), out(256, BF16), out_t(vt_rows, BF16),
            (jax.ShapeDtypeStruct((b, GATE_ROWS, s), F32),
             pl.BlockSpec((1, GATE_ROWS, tm), lambda bi, j: (bi, 0, j)))]
    return pl.pallas_call(
        _inproj_kernel,
        grid=grid,
        in_specs=[pl.BlockSpec((1, tm, d), lambda bi, j: (bi, j, 0)),
                  pl.BlockSpec((1, d), lambda bi, j: (0, 0)),
                  pl.BlockSpec((1, 6, d), lambda bi, j: (bi, 0, 0)),
                  pl.BlockSpec((1,) + w_in.shape[1:], lambda bi, j: (0, 0, 0), pipeline_mode=pl.Buffered(1)),
                  tab_spec, tab_spec, tab_spec, tab_spec],
        out_specs=[o[1] for o in outs],
        out_shape=[o[0] for o in outs],
        scratch_shapes=[pltpu.VMEM((d, IN_COLS_K), BF16)],
        compiler_params=pltpu.CompilerParams(
            dimension_semantics=("arbitrary", "arbitrary"), vmem_limit_bytes=VMEM_LIMIT),
        name="in_proj",
    )(x, ln_g.reshape(1, d), mod, w_in, *tabs, hot)


def _ret_kernel(q_ref, k_ref, v_ref, rg_ref, dec_ref, zeta_ref, xi_ref, gch_ref, o_ref, kv_ref, prev_ref):
    c = RET_CHUNK
    n_chunks = q_ref.shape[1] // c
    low = lax.broadcasted_iota(jnp.int32, (c, LANES), 1) < RET_DK

    def chunk_rows(n):
        return pl.ds(pl.multiple_of(n * c, c), c)

    def head_cols(h):
        return slice(h * RET_DV, (h + 1) * RET_DV)

    def kv_body(it, carry):
        kz_t = {}
        for j in range(RET_GROUP):
            rows = chunk_rows(it * RET_GROUP + j)
            for p in range(RET_HEADS // 2):
                pair = slice(p * LANES, (p + 1) * LANES)
                kz_t[j, p] = (k_ref[0, rows, pair].astype(F32) * zeta_ref[p]).T.astype(BF16)
        for j in range(RET_GROUP):
            n = it * RET_GROUP + j
            for h in range(RET_HEADS):
                kv_ref[h, n] = _nn(kz_t[j, h // 2], v_ref[0, chunk_rows(n), head_cols(h)])
        return carry

    lax.fori_loop(0, n_chunks // RET_GROUP, kv_body, 0)

    for h in range(RET_HEADS):
        def scan_body(n, st, h=h):
            prev_ref[h, n] = st.astype(prev_ref.dtype)
            return st * gch_ref[h] + kv_ref[h, n]
        lax.fori_loop(0, n_chunks, scan_body, jnp.zeros((LANES, RET_DV), F32))

    def out_body(it, carry):
        chains = [(j, h) for j in range(RET_GROUP) for h in range(RET_HEADS)]
        chunk = lambda j: it * RET_GROUP + j
        q_own, qx_own, att, ys = {}, {}, {}, {}
        for j, h in chains:
            p, e = divmod(h, 2)
            pair = slice(p * LANES, (p + 1) * LANES)
            mine = low if e == 0 else jnp.logical_not(low)
            q2 = q_ref[0, chunk_rows(chunk(j)), pair].astype(F32)
            q_own[j, h] = jnp.where(mine, q2, 0.0).astype(BF16)
            qx_own[j, h] = jnp.where(mine, q2 * xi_ref[p], 0.0).astype(BF16)
        for j, h in chains:
            pair = slice((h // 2) * LANES, (h // 2 + 1) * LANES)
            att[j, h] = _nt(q_own[j, h], k_ref[0, chunk_rows(chunk(j)), pair])
        for j, h in chains:
            lhs = jnp.concatenate([(att[j, h] * dec_ref[h]).astype(BF16), qx_own[j, h]], axis=1)
            rhs = jnp.concatenate([v_ref[0, chunk_rows(chunk(j)), head_cols(h)], prev_ref[h, chunk(j)]], axis=0)
            ys[j, h] = _nn(lhs, rhs)
        for j, h in chains:
            y = ys[j, h]
            yn = y * lax.rsqrt(jnp.mean(y * y, axis=-1, keepdims=True) + EPS)
            gate = rg_ref[0, chunk_rows(chunk(j)), head_cols(h)].astype(F32)
            o_ref[0, chunk_rows(chunk(j)), head_cols(h)] = (yn * gate).astype(o_ref.dtype)
        return carry

    lax.fori_loop(0, n_chunks // RET_GROUP, out_body, 0)


def _retention(rq, rk, rv, rg_act, tables):
    b, s, _ = rq.shape
    decay, zeta_p, xi_p, g_b = tables
    whole = lambda a: pl.BlockSpec(a.shape, lambda bi: (0,) * a.ndim)
    row = lambda a: pl.BlockSpec((1,) + a.shape[1:], lambda bi: (bi, 0, 0))
    return pl.pallas_call(
        _ret_kernel,
        grid=(b,),
        in_specs=[row(rq), row(rk), row(rv), row(rg_act),
                  whole(decay), whole(zeta_p), whole(xi_p), whole(g_b)],
        out_specs=row(rv),
        out_shape=jax.ShapeDtypeStruct(rv.shape, BF16),
        scratch_shapes=[pltpu.VMEM((RET_HEADS, s // RET_CHUNK, LANES, RET_DV), F32),
                        pltpu.VMEM((RET_HEADS, s // RET_CHUNK, LANES, RET_DV), BF16)],
        compiler_params=pltpu.CompilerParams(
            dimension_semantics=("parallel",), vmem_limit_bytes=VMEM_LIMIT),
        name="retention",
    )(rq, rk, rv, rg_act, decay, zeta_p, xi_p, g_b)


def _cmp_kernel(kc_ref, vc_ref, w1_ref, pe_ref, w2_ref, c_ref, s1_ref, s2_ref, kv_ref, vt_ref):
    n_piece = kc_ref.shape[1] // CMP_STRIDE
    halves = CMP_LEN // CMP_STRIDE
    lhs = {}
    for t, src in enumerate((kc_ref, vc_ref)):
        rows = [src[0, pl.ds(r, n_piece, stride=CMP_STRIDE), :] for r in range(CMP_STRIDE)]
        for half in range(halves):
            pe0 = half * CMP_STRIDE
            lhs[t, half] = jnp.concatenate(
                [(rows[r] + pe_ref[t, pe0 + r:pe0 + r + 1, :]).astype(BF16) for r in range(CMP_STRIDE)], axis=1)
    part = {key: _nn(lhs[key], w1_ref[key[0], key[1]]) for key in lhs}
    out = jnp.zeros((n_piece, kv_ref.shape[2]), F32)
    for t in range(2):
        hid = part[t, 0]
        for half in range(1, halves):
            hid = hid + pltpu.roll(part[t, half], n_piece - half, 0)
        out = out + _nn((hid * _sigmoid(hid)).astype(BF16), w2_ref[t])
    for t in range(out.shape[1] // LANES):
        sl = slice(t * LANES, (t + 1) * LANES)
        kv = _rope_tile(out[:, sl], c_ref[...], s1_ref[...], s2_ref[...])
        kv_ref[0, :, sl] = kv.astype(kv_ref.dtype)
        vt_ref[0, t] = kv.T[NSA_DH:2 * NSA_DH].astype(vt_ref.dtype)


def _compress_weights(w1_k, w2_k, w1_v, w2_v, pe_k, pe_v):
    dh, hid = NSA_DH, CMP_HIDDEN

    def first(w1):
        w = w1.reshape(CMP_LEN, dh, hid)
        z = jnp.zeros_like(w)
        return jnp.concatenate([jnp.concatenate([w, z], axis=2), jnp.concatenate([z, w], axis=2)], axis=1)

    def second(w2, off):
        z = jnp.zeros_like(w2)
        rows = []
        for g in range(NSA_GROUPS):
            c = [z, z, z, z]
            c[2 * g + off] = w2
            rows.append(jnp.concatenate(c, axis=1))
        return jnp.concatenate(rows, axis=0)

    w1 = jnp.stack([first(w1_k), first(w1_v)]).astype(BF16).reshape(
        2, CMP_LEN // CMP_STRIDE, CMP_STRIDE * NSA_GROUPS * dh, NSA_GROUPS * hid)
    w2 = jnp.stack([second(w2_k, 0), second(w2_v, 1)]).astype(BF16)
    pe = jnp.stack([jnp.concatenate([pe_k, pe_k], axis=1), jnp.concatenate([pe_v, pe_v], axis=1)])
    return w1, w2, pe


def _compress(kc, vc, w1, w2, pe, tabs_cmp):
    b, s, w = kc.shape
    n_piece = s // CMP_STRIDE
    const2 = lambda bi: (0, 0)
    src = pl.BlockSpec((1, s, w), lambda bi: (bi, 0, 0))
    dst = pl.BlockSpec((1, n_piece, 2 * w), lambda bi: (bi, 0, 0))
    shape = jax.ShapeDtypeStruct((b, n_piece, 2 * w), BF16)
    return pl.pallas_call(
        _cmp_kernel,
        grid=(b,),
        in_specs=[src, src,
                  pl.BlockSpec(w1.shape, lambda bi: (0, 0, 0, 0)),
                  pl.BlockSpec(pe.shape, lambda bi: (0, 0, 0)),
                  pl.BlockSpec(w2.shape, lambda bi: (0, 0, 0)),
                  pl.BlockSpec((n_piece, LANES), const2),
                  pl.BlockSpec((n_piece, LANES), const2),
                  pl.BlockSpec((n_piece, LANES), const2)],
        out_specs=[dst, pl.BlockSpec((1, NSA_GROUPS, NSA_DH, n_piece), lambda bi: (bi, 0, 0, 0))],
        out_shape=[shape, jax.ShapeDtypeStruct((b, NSA_GROUPS, NSA_DH, n_piece), BF16)],
        compiler_params=pltpu.CompilerParams(
            dimension_semantics=("parallel",), vmem_limit_bytes=VMEM_LIMIT),
        name="compress",
    )(kc, vc, w1, pe, w2, *tabs_cmp)


def _nsa_kernel(q_ref, kcmp_ref, vct_ref, ksx_ref, vst_ref, kwx_ref, vwt_ref, gate_ref, ovt_ref, *rest):
    n_w = (len(rest) - 1) // 2
    o_ref = rest[n_w]
    for w_src, w_dst in zip(rest[:n_w], rest[n_w + 1:]):
        w_dst[...] = w_src[...].astype(w_dst.dtype)

    tq = TQ
    seq = q_ref.shape[1]
    nb = seq // SLC_LEN
    n_cmp = kcmp_ref.shape[1]
    hpg, dh = NSA_HPG, NSA_DH
    assert WIN % tq == 0 and seq % tq == 0
    group = pl.program_id(1)

    low = lax.broadcasted_iota(jnp.int32, (tq, LANES), 1) < dh
    eye = jnp.where(lax.broadcasted_iota(jnp.int32, (tq, tq), 0)
                    == lax.broadcasted_iota(jnp.int32, (tq, tq), 1), 1.0, 0.0).astype(BF16)
    blk = lax.broadcasted_iota(jnp.int32, (nb, tq), 0)
    col = lax.broadcasted_iota(jnp.int32, (nb, tq), 1)
    crow = lax.broadcasted_iota(jnp.int32, (n_cmp, tq), 0)
    ccol = lax.broadcasted_iota(jnp.int32, (n_cmp, tq), 1)
    kcm = kcmp_ref[0]
    vct = vct_ref[0, 0]
    ovt = ovt_ref[...]

    def per_head(x):
        return jnp.concatenate([x] * hpg, axis=1)

    key_off = lax.broadcasted_iota(jnp.int32, (tq, tq), 0)
    qry_off = lax.broadcasted_iota(jnp.int32, (tq, tq), 1)
    not_after = per_head(jnp.where(key_off <= qry_off, 0.0, NEG))
    inside_win = per_head(jnp.where(key_off > qry_off, 0.0, NEG))

    def masked(s, first_key, t0, windowed):
        blocks = []
        for r in range(0, s.shape[0], tq):
            blk_s = s[r:r + tq]
            if first_key + r == t0:
                blk_s = blk_s + not_after
            elif windowed and first_key + r == t0 - WIN:
                blk_s = blk_s + inside_win
            blocks.append(blk_s)
        return jnp.concatenate(blocks, axis=0)

    def normalise(acc):
        return acc[0:dh] / acc[dh:dh + 1]

    def select_blocks(psum, t0):
        tcol = col + t0
        bcausal = blk * SLC_LEN <= tcol
        n_live = (t0 + tq - 1) // SLC_LEN + 1
        top_n = min(SLC_TOPK, nb)
        if n_live <= top_n:
            bias = jnp.where(bcausal, 0.0, NEG)
        else:
            p_hi = psum.astype(BF16)
            p_lo = (psum - p_hi.astype(F32)).astype(BF16)
            imp = _nn(ovt, p_hi) + _nn(ovt, p_lo)
            cur = tcol // SLC_LEN
            forced = (blk == 0) | (blk == cur) | (blk == cur - 1)
            imp = jnp.where(bcausal, jnp.where(forced, FORCE, imp), NEG)
            rank = jnp.zeros((nb, tq), F32)
            for j in range(n_live):
                r = imp[j:j + 1, :]
                rank = rank + jnp.where(blk > j, jnp.where(r >= imp, 1.0, 0.0), jnp.where(r > imp, 1.0, 0.0))
            bias = jnp.where((rank < float(top_n)) & bcausal, 0.0, NEG)
        feat = jnp.concatenate([jnp.zeros((dh, tq), F32), bias,
                                jnp.zeros((LANES - dh - nb, tq), F32)], axis=0).astype(BF16)
        return _nt(eye, feat)

    class TileGroup:
        def __init__(self, tiles):
            self.tiles = tiles
            self.wstart = {t0: max(t0 - WIN, 0) for t0 in tiles}
            self.wkeys = {t0: slice(self.wstart[t0], t0 + tq) for t0 in tiles}
            self.skeys = {t0: slice(0, t0 + tq) for t0 in tiles}

        def scores_window_compressed(self):
            self.heads, self.qs = {}, {}
            for t0 in self.tiles:
                qf = q_ref[0, t0:t0 + tq, :].astype(F32)
                hl = []
                for hh in range(hpg):
                    t = qf[:, (hh // 2) * LANES:(hh // 2 + 1) * LANES]
                    if hh % 2 == 1:
                        t = _swap_halves(t)
                    hl.append(jnp.where(low, t, 0.0))
                self.heads[t0] = hl
                self.qs[t0] = jnp.concatenate(hl, axis=0).astype(BF16)
            self.s_w = {t0: _nt(kwx_ref[0, self.wkeys[t0], :], self.qs[t0]) for t0 in self.tiles}
            self.s_c = {t0: _nt(kcm, self.qs[t0]) for t0 in self.tiles}

        def select_and_scores_selected(self):
            self.p_cmp, qsel = {}, {}
            for t0 in self.tiles:
                cmask = (crow * CMP_STRIDE + (CMP_LEN - 1)) <= (ccol + t0)
                p_all = []
                psum = jnp.zeros((n_cmp, tq), F32)
                for hh in range(hpg):
                    sh = jnp.where(cmask, self.s_c[t0][:, hh * tq:(hh + 1) * tq], NEG)
                    e = jnp.exp2(sh - jnp.max(sh, axis=0, keepdims=True))
                    p = jnp.where(cmask, e / jnp.sum(e, axis=0, keepdims=True), 0.0)
                    psum = psum + p
                    p_all.append(p.astype(BF16))
                self.p_cmp[t0] = jnp.concatenate(p_all, axis=1)
                qbias = select_blocks(psum, t0)
                qsel[t0] = jnp.concatenate([hd + qbias for hd in self.heads[t0]], axis=0).astype(BF16)
            self.s_s = {t0: _nt(ksx_ref[0, self.skeys[t0], :], qsel[t0]) for t0 in self.tiles}

        def outputs_window_compressed(self):
            e_w = {}
            for t0 in self.tiles:
                sw = masked(self.s_w[t0], self.wstart[t0], t0, True)
                e_w[t0] = jnp.exp2(sw - jnp.max(sw, axis=0, keepdims=True)).astype(BF16)
            self.o_win = {t0: normalise(_nn(vwt_ref[0, 0, :, self.wkeys[t0]], e_w[t0])) for t0 in self.tiles}
            self.o_cmp = {t0: _nn(vct, self.p_cmp[t0]) for t0 in self.tiles}

        def outputs_selected_and_store(self):
            e_s = {}
            for t0 in self.tiles:
                ss = masked(self.s_s[t0], 0, t0, False)
                e_s[t0] = jnp.exp2(ss - jnp.max(ss, axis=0, keepdims=True)).astype(BF16)
            o_sel = {t0: normalise(_nn(vst_ref[0, 0, :, self.skeys[t0]], e_s[t0])) for t0 in self.tiles}
            for t0 in self.tiles:
                gt = gate_ref[0, :, t0:t0 + tq]
                outs = []
                for hh in range(hpg):
                    hc = slice(hh * tq, (hh + 1) * tq)

                    def gate_row(branch):
                        by_group = [gt[branch * NSA_HEADS + g * hpg + hh:branch * NSA_HEADS + g * hpg + hh + 1, :]
                                    for g in range(NSA_GROUPS)]
                        row = by_group[-1]
                        for g in range(NSA_GROUPS - 2, -1, -1):
                            row = jnp.where(group == g, by_group[g], row)
                        return row

                    outs.append(gate_row(0) * self.o_cmp[t0][:, hc] + gate_row(1) * o_sel[t0][:, hc]
                                + gate_row(2) * self.o_win[t0][:, hc])
                o_ref[0, t0:t0 + tq, :] = jnp.concatenate(outs, axis=0).T.astype(o_ref.dtype)

    starts = list(range(0, seq, tq))
    groups = [TileGroup(starts[i:i + TILE_GROUP]) for i in range(0, len(starts), TILE_GROUP)]
    n = len(groups)
    groups[0].scores_window_compressed()
    for i, g in enumerate(groups):
        if i + 1 < n:
            groups[i + 1].scores_window_compressed()
        g.select_and_scores_selected()
        g.outputs_window_compressed()
        if i > 0:
            groups[i - 1].outputs_selected_and_store()
    groups[n - 1].outputs_selected_and_store()


def _nsa_attention(nq, kvcmp, vct, ksx, vst, kwx, vwt, gate_t, ovt, weights):
    b, s, _ = nq.shape
    n_cmp = kvcmp.shape[1]
    gw = NSA_HPG * NSA_DH
    steps = b * NSA_GROUPS
    per_group = lambda rows, width: pl.BlockSpec((1, rows, width), lambda bi, g: (bi, 0, g))
    per_group_t = lambda a: pl.BlockSpec((1, 1) + a.shape[2:], lambda bi, g: (bi, g, 0, 0))
    assert all(w.shape[1] % (steps * 16) == 0 for w in weights)
    slab = lambda w: pl.BlockSpec((1, w.shape[1] // steps, w.shape[2]),
                                  lambda bi, g: (0, bi * NSA_GROUPS + g, 0))
    outs = pl.pallas_call(
        _nsa_kernel,
        grid=(b, NSA_GROUPS),
        in_specs=[per_group(s, gw),
                  per_group(n_cmp, LANES), per_group_t(vct),
                  per_group(s, LANES), per_group_t(vst), per_group(s, LANES), per_group_t(vwt),
                  pl.BlockSpec((1,) + gate_t.shape[1:], lambda bi, g: (bi, 0, 0)),
                  pl.BlockSpec(ovt.shape, lambda bi, g: (0, 0))] + [slab(w) for w in weights],
        out_specs=[per_group(s, gw)] + [slab(w) for w in weights],
        out_shape=[jax.ShapeDtypeStruct((b, s, NSA_HEADS * NSA_DH), BF16)]
                  + [jax.ShapeDtypeStruct(w.shape, BF16) for w in weights],
        compiler_params=pltpu.CompilerParams(
            dimension_semantics=("parallel", "parallel"), vmem_limit_bytes=VMEM_LIMIT),
        name="nsa_attn",
    )(nq, kvcmp, vct, ksx, vst, kwx, vwt, gate_t, ovt, *weights)
    return outs[0], [w[0] for w in outs[1:]]


def _ffn_kernel(x_ref, yr_ref, yn_ref, mod_ref, g2_ref, gf_ref, wo_ref, wg_ref, wu_ref, wd_ref,
                o_ref, x1_ref, act_ref):
    half_w = yr_ref.shape[1]
    d_ff = wg_ref.shape[1]
    n_chunks = d_ff // TF
    n_rows = x_ref.shape[0] // SUB_FF
    slabs = [slice(p * n_rows, (p + 1) * n_rows) for p in range(SUB_FF)]

    def mix(rows):
        return _nn(yr_ref[rows, :], wo_ref[0:half_w, :]) + _nn(yn_ref[rows, :], wo_ref[half_w:2 * half_w, :])

    def mid_norm(rows, mixed):
        x1 = x_ref[rows, :] + mod_ref[0, 2:3, :] * mixed
        x1_ref[rows, :] = x1
        y = x1 * lax.rsqrt(jnp.mean(x1 * x1, axis=-1, keepdims=True) + EPS) * g2_ref[...]
        return (y * (1.0 + mod_ref[0, 4:5, :]) + mod_ref[0, 3:4, :]).astype(BF16)

    def ff_chunk(rows, h2, j):
        sl = slice(j * TF, (j + 1) * TF)
        gate = _nn(h2, wg_ref[:, sl])
        up = _nn(h2, wu_ref[:, sl])
        act_ref[rows, sl] = (gate * _sigmoid(gate) * up).astype(BF16)

    def down(rows):
        return x1_ref[rows, :] + mod_ref[0, 5:6, :] * _nn(act_ref[rows, :], wd_ref[...])

    def final_norm(rows, xo):
        o_ref[rows, :] = xo * lax.rsqrt(jnp.mean(xo * xo, axis=-1, keepdims=True) + EPS) * gf_ref[...]

    mixed = [mix(rows) for rows in slabs]
    h2 = mid_norm(slabs[0], mixed[0])
    xo_prev = None
    for p, rows in enumerate(slabs):
        ff_chunk(rows, h2, 0)
        h2_next = mid_norm(slabs[p + 1], mixed[p + 1]) if p + 1 < SUB_FF else None
        if xo_prev is not None:
            final_norm(slabs[p - 1], xo_prev)
        for j in range(1, n_chunks):
            ff_chunk(rows, h2, j)
        xo_prev = down(rows)
        h2 = h2_next
    final_norm(slabs[-1], xo_prev)


def _out_ffn(x2d, y_ret, y_nsa, mod, w_out, g2, gf, wg, wu, wd, seq):
    n, d = x2d.shape
    tm = TM_FF
    d_ff = wg.shape[1]
    tiles_per_seq = seq // tm
    half_w = y_ret.shape[1]
    row = lambda i: (i, 0)
    resident = lambda a: pl.BlockSpec(a.shape, lambda i: (0, 0), pipeline_mode=pl.Buffered(1))
    return pl.pallas_call(
        _ffn_kernel,
        grid=(n // tm,),
        in_specs=[pl.BlockSpec((tm, d), row),
                  pl.BlockSpec((tm, half_w), row),
                  pl.BlockSpec((tm, half_w), row),
                  pl.BlockSpec((1, 6, d), lambda i: (i // tiles_per_seq, 0, 0)),
                  pl.BlockSpec((1, d), lambda i: (0, 0)),
                  pl.BlockSpec((1, d), lambda i: (0, 0)),
                  resident(w_out), resident(wg), resident(wu), resident(wd)],
        out_specs=pl.BlockSpec((tm, d), row),
        out_shape=jax.ShapeDtypeStruct((n, d), F32),
        scratch_shapes=[pltpu.VMEM((tm, d), F32), pltpu.VMEM((tm, d_ff), BF16)],
        compiler_params=pltpu.CompilerParams(
            dimension_semantics=("parallel",), vmem_limit_bytes=VMEM_LIMIT),
        name="out_ffn",
    )(x2d, y_ret, y_nsa, mod, g2.reshape(1, d), gf.reshape(1, d), w_out, wg, wu, wd)


def kernel(x, c, ln_mix_g, ln_ffn_g, w_ada, b_ada, w_in, cmp_pe_k, cmp_w1_k, cmp_w2_k,
           cmp_pe_v, cmp_w1_v, cmp_w2_v, w_out, w_ff_gate, w_ff_up, w_ff_down, ln_final_g):
    assert w_in.shape[0] == 1, "the final RMSNorm is fused into the (single) layer's FFN kernel"
    b, s, d = x.shape
    lane = np.arange(LANES)
    tabs = _rope_tables(np.arange(s), np.ones(LANES, bool))
    n_piece = s // CMP_STRIDE
    tabs_cmp = _rope_tables(np.arange(n_piece) * CMP_STRIDE + CMP_LEN - 1, lane < NSA_DH)

    mod = _ada(c, w_ada[0], b_ada[0]).reshape(b, 6, d)
    rq, rk, rv, rg_act, nq, kc, vc, ksx, vst, kwx, vwt, gate_t = _in_proj(
        x, ln_mix_g[0], mod, w_in, tabs, _block_onehot_table(s))
    y_ret = _retention(rq, rk, rv, rg_act, _retention_tables())
    w1, w2, pe = _compress_weights(cmp_w1_k[0], cmp_w2_k[0], cmp_w1_v[0], cmp_w2_v[0],
                                   cmp_pe_k[0], cmp_pe_v[0])
    kvcmp, vct = _compress(kc, vc, w1, w2, pe, tabs_cmp)
    y_nsa, (wo_b, wg_b, wu_b, wd_b) = _nsa_attention(
        nq, kvcmp, vct, ksx, vst, kwx, vwt, gate_t, _overlap_t(s), (w_out, w_ff_gate, w_ff_up, w_ff_down))
    out = _out_ffn(x.reshape(b * s, d), y_ret.reshape(b * s, -1), y_nsa.reshape(b * s, -1), mod,
                   wo_b, ln_ffn_g[0], ln_final_g, wg_b, wu_b, wd_b, s)
    return out.reshape(b, s, d)
```

```python
import numpy as np
import jax
import jax.numpy as jnp
from jax import lax
from jax.experimental import pallas as pl
from jax.experimental.pallas import tpu as pltpu

F32 = jnp.float32
BF16 = jnp.bfloat16

D_MODEL = 1024
RET_HEADS = 4
RET_DK = 64
RET_DV = 128
RET_CHUNK = 128
NSA_HEADS = 8
NSA_GROUPS = 2
NSA_HPG = NSA_HEADS // NSA_GROUPS
NSA_DH = 64
CMP_LEN = 32
CMP_STRIDE = 16
CMP_HIDDEN = 128
SLC_LEN = 64
SLC_TOPK = 16
WIN = 512
D_FF = ((8 * D_MODEL + 3 * 256 - 1) // (3 * 256)) * 256
ROPE_THETA = 10000.0
EPS = 1e-6
NEG = -1e30
FORCE = 1e6

LANES = 128
SUBLANES = 8
HALF = NSA_DH // 2
ONES_ROWS = 16
GATE_ROWS = 3 * NSA_HEADS

TM_IN = 1024
PROJ_W = 512
RET_GROUP = 16
Q_SCALE = NSA_DH ** -0.5 * float(np.log2(np.e))
TQ = 128
TILE_GROUP = 2
TM_FF = 512
SUB_IN = 2
SUB_FF = 2
TF = 256
VMEM_LIMIT = 56 * 1024 * 1024

_R_RQ, _R_RK, _R_RV, _R_RG, _R_NQ = 0, 256, 512, 1024, 1536
_R_KC, _R_VC, _R_KS, _R_VS, _R_KW, _R_VW, _R_GATE = 2048, 2176, 2304, 2432, 2560, 2688, 2816
IN_COLS_K = -(-(_R_GATE + 3 * NSA_HEADS) // PROJ_W) * PROJ_W


def _sigmoid(x):
    return 1.0 / (1.0 + jnp.exp(-x))


def _nt(a, b):
    return lax.dot_general(a, b, (((1,), (1,)), ((), ())), preferred_element_type=F32)


def _nn(a, b):
    return jnp.dot(a, b, preferred_element_type=F32)


def _rope_tile(a, c, s1, s2):
    return a * c + pltpu.roll(a, HALF, 1) * s1 + pltpu.roll(a, LANES - HALF, 1) * s2


def _swap_halves(a):
    return pltpu.roll(a, LANES // 2, 1)


def _rope_tables(pos, rotary_lanes):
    pos = np.asarray(pos, np.float64)
    lane = np.arange(LANES)
    within = lane % NSA_DH
    freq = ROPE_THETA ** (-(within % HALF).astype(np.float64) / HALF)
    ang = pos[:, None] * freq[None, :]
    cos, sin = np.cos(ang), np.sin(ang)
    first = (within < HALF)[None, :]
    rot = np.asarray(rotary_lanes, bool)[None, :]
    c = np.where(rot, cos, 1.0)
    s1 = np.where(rot & ~first, sin, 0.0)
    s2 = np.where(rot & first, -sin, 0.0)
    return (jnp.asarray(c, F32), jnp.asarray(s1, F32), jnp.asarray(s2, F32))


def _block_onehot_table(seq):
    t = np.zeros((seq, LANES), np.float32)
    pos = np.arange(seq)
    t[pos, NSA_DH + pos // SLC_LEN] = 1.0
    return jnp.asarray(t)


def _retention_tables():
    h = np.arange(RET_HEADS, dtype=np.float64)
    log_g = np.log(1.0 - 2.0 ** (-5.0 - h))
    c = RET_CHUNK
    idx = np.arange(c, dtype=np.float64)
    diff = idx[:, None] - idx[None, :]
    causal = diff >= 0
    decay = np.where(causal, np.exp(log_g[:, None, None] * np.where(causal, diff, 0.0)), 0.0)
    zeta = np.exp(log_g[:, None] * (c - 1.0 - idx))
    xi = np.exp(log_g[:, None] * (idx + 1.0))
    g_chunk = np.exp(log_g * c)

    def pair_lanes(t):
        t = t.reshape(RET_HEADS // 2, 2, c)
        return np.repeat(np.transpose(t, (0, 2, 1)), RET_DK, axis=2)

    g_b = np.broadcast_to(g_chunk[:, None, None], (RET_HEADS, 1, LANES))
    return tuple(jnp.asarray(t, F32) for t in (decay, pair_lanes(zeta), pair_lanes(xi), g_b))


def _overlap_t(seq):
    n_c = seq // CMP_STRIDE - CMP_LEN // CMP_STRIDE + 1
    nb = seq // SLC_LEN
    cs = np.arange(n_c) * CMP_STRIDE
    bs = np.arange(nb) * SLC_LEN
    ov = np.maximum(np.minimum(cs[:, None] + CMP_LEN, bs[None] + SLC_LEN)
                    - np.maximum(cs[:, None], bs[None]), 0).astype(np.float64) / CMP_LEN
    ncp = seq // CMP_STRIDE
    ovp = np.zeros((ncp, nb))
    ovp[:n_c] = ov
    return jnp.asarray(ovp.T, BF16)


def _ada_kernel(c_ref, w_ref, b_ref, o_ref):
    c = c_ref[...]
    o_ref[...] = _nn(c * _sigmoid(c), w_ref[...]) + b_ref[...]


def _ada(c, w, b):
    bsz, d = c.shape
    n = w.shape[1]
    tn = 1024
    return pl.pallas_call(
        _ada_kernel,
        grid=(n // tn,),
        in_specs=[pl.BlockSpec((bsz, d), lambda j: (0, 0)),
                  pl.BlockSpec((d, tn), lambda j: (0, j)),
                  pl.BlockSpec((1, tn), lambda j: (0, j))],
        out_specs=pl.BlockSpec((bsz, tn), lambda j: (0, j)),
        out_shape=jax.ShapeDtypeStruct((bsz, n), F32),
        compiler_params=pltpu.CompilerParams(vmem_limit_bytes=VMEM_LIMIT),
        name="ada",
    )(c, w, b.reshape(1, n))


def _inproj_kernel(x_ref, g_ref, mod_ref, w_ref, c_ref, s1_ref, s2_ref, hot_ref,
                   rq_ref, rk_ref, rv_ref, rg_ref, nq_ref, kc_ref, vc_ref,
                   ksx_ref, vst_ref, kwx_ref, vwt_ref, gate_ref, wb_ref):
    @pl.when((pl.program_id(0) == 0) & (pl.program_id(1) == 0))
    def _():
        n_in = w_ref.shape[2]
        whole = (n_in // PROJ_W) * PROJ_W
        for c0 in range(0, whole, PROJ_W):
            wb_ref[:, c0:c0 + PROJ_W] = w_ref[0, :, c0:c0 + PROJ_W].astype(BF16)
        wb_ref[:, whole:] = jnp.zeros((wb_ref.shape[0], wb_ref.shape[1] - whole), BF16)
        wb_ref[:, whole:n_in] = w_ref[0, :, whole:n_in].astype(BF16)

    def modulated_norm(rows):
        x = x_ref[0, rows, :]
        y = x * lax.rsqrt(jnp.mean(x * x, axis=-1, keepdims=True) + EPS) * g_ref[...]
        return (y * (1.0 + mod_ref[0, 1:2, :]) + mod_ref[0, 0:1, :]).astype(BF16)

    def product_steps(rows, hb):
        tabs = (c_ref[rows, :], s1_ref[rows, :], s2_ref[rows, :])
        low = lax.broadcasted_iota(jnp.int32, (rows.stop - rows.start, LANES), 1) < NSA_DH

        def proj(c0):
            a = _nn(hb, wb_ref[:, c0:c0 + PROJ_W])
            return [a[:, t * LANES:(t + 1) * LANES] for t in range(PROJ_W // LANES)]

        def put(out_ref, t, value):
            out_ref[0, rows, t * LANES:(t + 1) * LANES] = value.astype(out_ref.dtype)

        def roped(tiles, scale, out_ref):
            for t, a in enumerate(tiles):
                r = _rope_tile(a, *tabs)
                put(out_ref, t, r if scale == 1.0 else r * scale)

        def per_group(tile, fill, out_ref):
            put(out_ref, 0, jnp.where(low, tile, fill))
            put(out_ref, 1, jnp.where(low, _swap_halves(tile), fill))

        def per_group_t(tile, out_ref):
            t = tile.T
            ones = jnp.ones((ONES_ROWS, t.shape[1]), out_ref.dtype)
            for g in range(NSA_GROUPS):
                out_ref[0, g, 0:NSA_DH, rows] = t[g * NSA_DH:(g + 1) * NSA_DH].astype(out_ref.dtype)
                out_ref[0, g, NSA_DH:NSA_DH + ONES_ROWS, rows] = ones

        def retention_qk():
            tiles = proj(_R_RQ)
            roped(tiles[0:2], 1.0, rq_ref)
            roped(tiles[2:4], RET_DK ** -0.5, rk_ref)

        def retention_v():
            for t, a in enumerate(proj(_R_RV)):
                put(rv_ref, t, a)

        def retention_gate():
            for t, a in enumerate(proj(_R_RG)):
                put(rg_ref, t, a * _sigmoid(a))

        def nsa_q():
            roped(proj(_R_NQ), Q_SCALE, nq_ref)

        def nsa_compress_selected():
            kc_t, vc_t, ks_t, vs_t = proj(_R_KC)
            kc_ref[0, rows, :] = kc_t
            vc_ref[0, rows, :] = vc_t
            per_group(_rope_tile(ks_t, *tabs), hot_ref[rows, :], ksx_ref)
            per_group_t(vs_t, vst_ref)

        def nsa_window_gates():
            kw_t, vw_t, gates_t, _ = proj(_R_KW)
            per_group(_rope_tile(kw_t, *tabs), 0.0, kwx_ref)
            per_group_t(vw_t, vwt_ref)
            gate_ref[0, :, rows] = _sigmoid(gates_t.T[0:GATE_ROWS])

        return [retention_qk, retention_v, retention_gate, nsa_q, nsa_compress_selected, nsa_window_gates]

    n_rows = x_ref.shape[1] // SUB_IN
    slabs = [slice(p * n_rows, (p + 1) * n_rows) for p in range(SUB_IN)]
    hb = modulated_norm(slabs[0])
    for p, rows in enumerate(slabs):
        steps = product_steps(rows, hb)
        steps[0]()
        if p + 1 < SUB_IN:
            hb = modulated_norm(slabs[p + 1])
        for step in steps[1:]:
            step()


def _in_proj(x, ln_g, mod, w_in, tabs, hot):
    b, s, d = x.shape
    tm = TM_IN
    grid = (b, s // tm)
    tab_spec = pl.BlockSpec((tm, LANES), lambda bi, j: (j, 0))

    def out(n, dtype):
        return (jax.ShapeDtypeStruct((b, s, n), dtype), pl.BlockSpec((1, tm, n), lambda bi, j: (bi, j, 0)))

    def out_t(rows, dtype):
        return (jax.ShapeDtypeStruct((b, NSA_GROUPS, rows, s), dtype),
                pl.BlockSpec((1, NSA_GROUPS, rows, tm), lambda bi, j: (bi, 0, 0, j)))

    vt_rows = NSA_DH + ONES_ROWS
    outs = [out(256, BF16), out(256, BF16), out(512, BF16), out(512, BF16), out(512, BF16),
            out(LANES, F32), out(LANES, F32),
            out(256, BF16), out_t(vt_rows, BF16), out(256, BF16), out_t(vt_rows, BF16),
            (jax.ShapeDtypeStruct((b, GATE_ROWS, s), F32),
             pl.BlockSpec((1, GATE_ROWS, tm), lambda bi, j: (bi, 0, j)))]
    return pl.pallas_call(
        _inproj_kernel,
        grid=grid,
        in_specs=[pl.BlockSpec((1, tm, d), lambda bi, j: (bi, j, 0)),
                  pl.BlockSpec((1, d), lambda bi, j: (0, 0)),
                  pl.BlockSpec((1, 6, d), lambda bi, j: (bi, 0, 0)),
                  pl.BlockSpec((1,) + w_in.shape[1:], lambda bi, j: (0, 0, 0), pipeline_mode=pl.Buffered(1)),
                  tab_spec, tab_spec, tab_spec, tab_spec],
        out_specs=[o[1] for o in outs],
        out_shape=[o[0] for o in outs],
        scratch_shapes=[pltpu.VMEM((d, IN_COLS_K), BF16)],
        compiler_params=pltpu.CompilerParams(
            dimension_semantics=("arbitrary", "arbitrary"), vmem_limit_bytes=VMEM_LIMIT),
        name="in_proj",
    )(x, ln_g.reshape(1, d), mod, w_in, *tabs, hot)


def _ret_kernel(q_ref, k_ref, v_ref, rg_ref, dec_ref, zeta_ref, xi_ref, gch_ref, o_ref, kv_ref, prev_ref):
    c = RET_CHUNK
    n_chunks = q_ref.shape[1] // c
    low = lax.broadcasted_iota(jnp.int32, (c, LANES), 1) < RET_DK

    def chunk_rows(n):
        return pl.ds(pl.multiple_of(n * c, c), c)

    def head_cols(h):
        return slice(h * RET_DV, (h + 1) * RET_DV)

    def kv_body(it, carry):
        kz_t = {}
        for j in range(RET_GROUP):
            rows = chunk_rows(it * RET_GROUP + j)
            for p in range(RET_HEADS // 2):
                pair = slice(p * LANES, (p + 1) * LANES)
                kz_t[j, p] = (k_ref[0, rows, pair].astype(F32) * zeta_ref[p]).T.astype(BF16)
        for j in range(RET_GROUP):
            n = it * RET_GROUP + j
            for h in range(RET_HEADS):
                kv_ref[h, n] = _nn(kz_t[j, h // 2], v_ref[0, chunk_rows(n), head_cols(h)])
        return carry

    lax.fori_loop(0, n_chunks // RET_GROUP, kv_body, 0)

    for h in range(RET_HEADS):
        def scan_body(n, st, h=h):
            prev_ref[h, n] = st.astype(prev_ref.dtype)
            return st * gch_ref[h] + kv_ref[h, n]
        lax.fori_loop(0, n_chunks, scan_body, jnp.zeros((LANES, RET_DV), F32))

    def out_body(it, carry):
        chains = [(j, h) for j in range(RET_GROUP) for h in range(RET_HEADS)]
        chunk = lambda j: it * RET_GROUP + j
        q_own, qx_own, att, ys = {}, {}, {}, {}
        for j, h in chains:
            p, e = divmod(h, 2)
            pair = slice(p * LANES, (p + 1) * LANES)
            mine = low if e == 0 else jnp.logical_not(low)
            q2 = q_ref[0, chunk_rows(chunk(j)), pair].astype(F32)
            q_own[j, h] = jnp.where(mine, q2, 0.0).astype(BF16)
            qx_own[j, h] = jnp.where(mine, q2 * xi_ref[p], 0.0).astype(BF16)
        for j, h in chains:
            pair = slice((h // 2) * LANES, (h // 2 + 1) * LANES)
            att[j, h] = _nt(q_own[j, h], k_ref[0, chunk_rows(chunk(j)), pair])
        for j, h in chains:
            lhs = jnp.concatenate([(att[j, h] * dec_ref[h]).astype(BF16), qx_own[j, h]], axis=1)
            rhs = jnp.concatenate([v_ref[0, chunk_rows(chunk(j)), head_cols(h)], prev_ref[h, chunk(j)]], axis=0)
            ys[j, h] = _nn(lhs, rhs)
        for j, h in chains:
            y = ys[j, h]
            yn = y * lax.rsqrt(jnp.mean(y * y, axis=-1, keepdims=True) + EPS)
            gate = rg_ref[0, chunk_rows(chunk(j)), head_cols(h)].astype(F32)
            o_ref[0, chunk_rows(chunk(j)), head_cols(h)] = (yn * gate).astype(o_ref.dtype)
        return carry

    lax.fori_loop(0, n_chunks // RET_GROUP, out_body, 0)


def _retention(rq, rk, rv, rg_act, tables):
    b, s, _ = rq.shape
    decay, zeta_p, xi_p, g_b = tables
    whole = lambda a: pl.BlockSpec(a.shape, lambda bi: (0,) * a.ndim)
    row = lambda a: pl.BlockSpec((1,) + a.shape[1:], lambda bi: (bi, 0, 0))
    return pl.pallas_call(
        _ret_kernel,
        grid=(b,),
        in_specs=[row(rq), row(rk), row(rv), row(rg_act),
                  whole(decay), whole(zeta_p), whole(xi_p), whole(g_b)],
        out_specs=row(rv),
        out_shape=jax.ShapeDtypeStruct(rv.shape, BF16),
        scratch_shapes=[pltpu.VMEM((RET_HEADS, s // RET_CHUNK, LANES, RET_DV), F32),
                        pltpu.VMEM((RET_HEADS, s // RET_CHUNK, LANES, RET_DV), BF16)],
        compiler_params=pltpu.CompilerParams(
            dimension_semantics=("parallel",), vmem_limit_bytes=VMEM_LIMIT),
        name="retention",
    )(rq, rk, rv, rg_act, decay, zeta_p, xi_p, g_b)


def _cmp_kernel(kc_ref, vc_ref, w1_ref, pe_ref, w2_ref, c_ref, s1_ref, s2_ref, kv_ref, vt_ref):
    n_piece = kc_ref.shape[1] // CMP_STRIDE
    halves = CMP_LEN // CMP_STRIDE
    lhs = {}
    for t, src in enumerate((kc_ref, vc_ref)):
        rows = [src[0, pl.ds(r, n_piece, stride=CMP_STRIDE), :] for r in range(CMP_STRIDE)]
        for half in range(halves):
            pe0 = half * CMP_STRIDE
            lhs[t, half] = jnp.concatenate(
                [(rows[r] + pe_ref[t, pe0 + r:pe0 + r + 1, :]).astype(BF16) for r in range(CMP_STRIDE)], axis=1)
    part = {key: _nn(lhs[key], w1_ref[key[0], key[1]]) for key in lhs}
    out = jnp.zeros((n_piece, kv_ref.shape[2]), F32)
    for t in range(2):
        hid = part[t, 0]
        for half in range(1, halves):
            hid = hid + pltpu.roll(part[t, half], n_piece - half, 0)
        out = out + _nn((hid * _sigmoid(hid)).astype(BF16), w2_ref[t])
    for t in range(out.shape[1] // LANES):
        sl = slice(t * LANES, (t + 1) * LANES)
        kv = _rope_tile(out[:, sl], c_ref[...], s1_ref[...], s2_ref[...])
        kv_ref[0, :, sl] = kv.astype(kv_ref.dtype)
        vt_ref[0, t] = kv.T[NSA_DH:2 * NSA_DH].astype(vt_ref.dtype)


def _compress_weights(w1_k, w2_k, w1_v, w2_v, pe_k, pe_v):
    dh, hid = NSA_DH, CMP_HIDDEN

    def first(w1):
        w = w1.reshape(CMP_LEN, dh, hid)
        z = jnp.zeros_like(w)
        return jnp.concatenate([jnp.concatenate([w, z], axis=2), jnp.concatenate([z, w], axis=2)], axis=1)

    def second(w2, off):
        z = jnp.zeros_like(w2)
        rows = []
        for g in range(NSA_GROUPS):
            c = [z, z, z, z]
            c[2 * g + off] = w2
            rows.append(jnp.concatenate(c, axis=1))
        return jnp.concatenate(rows, axis=0)

    w1 = jnp.stack([first(w1_k), first(w1_v)]).astype(BF16).reshape(
        2, CMP_LEN // CMP_STRIDE, CMP_STRIDE * NSA_GROUPS * dh, NSA_GROUPS * hid)
    w2 = jnp.stack([second(w2_k, 0), second(w2_v, 1)]).astype(BF16)
    pe = jnp.stack([jnp.concatenate([pe_k, pe_k], axis=1), jnp.concatenate([pe_v, pe_v], axis=1)])
    return w1, w2, pe


def _compress(kc, vc, w1, w2, pe, tabs_cmp):
    b, s, w = kc.shape
    n_piece = s // CMP_STRIDE
    const2 = lambda bi: (0, 0)
    src = pl.BlockSpec((1, s, w), lambda bi: (bi, 0, 0))
    dst = pl.BlockSpec((1, n_piece, 2 * w), lambda bi: (bi, 0, 0))
    shape = jax.ShapeDtypeStruct((b, n_piece, 2 * w), BF16)
    return pl.pallas_call(
        _cmp_kernel,
        grid=(b,),
        in_specs=[src, src,
                  pl.BlockSpec(w1.shape, lambda bi: (0, 0, 0, 0)),
                  pl.BlockSpec(pe.shape, lambda bi: (0, 0, 0)),
                  pl.BlockSpec(w2.shape, lambda bi: (0, 0, 0)),
                  pl.BlockSpec((n_piece, LANES), const2),
                  pl.BlockSpec((n_piece, LANES), const2),
                  pl.BlockSpec((n_piece, LANES), const2)],
        out_specs=[dst, pl.BlockSpec((1, NSA_GROUPS, NSA_DH, n_piece), lambda bi: (bi, 0, 0, 0))],
        out_shape=[shape, jax.ShapeDtypeStruct((b, NSA_GROUPS, NSA_DH, n_piece), BF16)],
        compiler_params=pltpu.CompilerParams(
            dimension_semantics=("parallel",), vmem_limit_bytes=VMEM_LIMIT),
        name="compress",
    )(kc, vc, w1, pe, w2, *tabs_cmp)


def _nsa_kernel(q_ref, kcmp_ref, vct_ref, ksx_ref, vst_ref, kwx_ref, vwt_ref, gate_ref, ovt_ref, *rest):
    n_w = (len(rest) - 1) // 2
    o_ref = rest[n_w]
    for w_src, w_dst in zip(rest[:n_w], rest[n_w + 1:]):
        w_dst[...] = w_src[...].astype(w_dst.dtype)

    tq = TQ
    seq = q_ref.shape[1]
    nb = seq // SLC_LEN
    n_cmp = kcmp_ref.shape[1]
    hpg, dh = NSA_HPG, NSA_DH
    assert WIN % tq == 0 and seq % tq == 0
    group = pl.program_id(1)

    low = lax.broadcasted_iota(jnp.int32, (tq, LANES), 1) < dh
    eye = jnp.where(lax.broadcasted_iota(jnp.int32, (tq, tq), 0)
                    == lax.broadcasted_iota(jnp.int32, (tq, tq), 1), 1.0, 0.0).astype(BF16)
    blk = lax.broadcasted_iota(jnp.int32, (nb, tq), 0)
    col = lax.broadcasted_iota(jnp.int32, (nb, tq), 1)
    crow = lax.broadcasted_iota(jnp.int32, (n_cmp, tq), 0)
    ccol = lax.broadcasted_iota(jnp.int32, (n_cmp, tq), 1)
    kcm = kcmp_ref[0]
    vct = vct_ref[0, 0]
    ovt = ovt_ref[...]

    def per_head(x):
        return jnp.concatenate([x] * hpg, axis=1)

    key_off = lax.broadcasted_iota(jnp.int32, (tq, tq), 0)
    qry_off = lax.broadcasted_iota(jnp.int32, (tq, tq), 1)
    not_after = per_head(jnp.where(key_off <= qry_off, 0.0, NEG))
    inside_win = per_head(jnp.where(key_off > qry_off, 0.0, NEG))

    def masked(s, first_key, t0, windowed):
        blocks = []
        for r in range(0, s.shape[0], tq):
            blk_s = s[r:r + tq]
            if first_key + r == t0:
                blk_s = blk_s + not_after
            elif windowed and first_key + r == t0 - WIN:
                blk_s = blk_s + inside_win
            blocks.append(blk_s)
        return jnp.concatenate(blocks, axis=0)

    def normalise(acc):
        return acc[0:dh] / acc[dh:dh + 1]

    def select_blocks(psum, t0):
        tcol = col + t0
        bcausal = blk * SLC_LEN <= tcol
        n_live = (t0 + tq - 1) // SLC_LEN + 1
        top_n = min(SLC_TOPK, nb)
        if n_live <= top_n:
            bias = jnp.where(bcausal, 0.0, NEG)
        else:
            p_hi = psum.astype(BF16)
            p_lo = (psum - p_hi.astype(F32)).astype(BF16)
            imp = _nn(ovt, p_hi) + _nn(ovt, p_lo)
            cur = tcol // SLC_LEN
            forced = (blk == 0) | (blk == cur) | (blk == cur - 1)
            imp = jnp.where(bcausal, jnp.where(forced, FORCE, imp), NEG)
            rank = jnp.zeros((nb, tq), F32)
            for j in range(n_live):
                r = imp[j:j + 1, :]
                rank = rank + jnp.where(blk > j, jnp.where(r >= imp, 1.0, 0.0), jnp.where(r > imp, 1.0, 0.0))
            bias = jnp.where((rank < float(top_n)) & bcausal, 0.0, NEG)
        feat = jnp.concatenate([jnp.zeros((dh, tq), F32), bias,
                                jnp.zeros((LANES - dh - nb, tq), F32)], axis=0).astype(BF16)
        return _nt(eye, feat)

    class TileGroup:
        def __init__(self, tiles):
            self.tiles = tiles
            self.wstart = {t0: max(t0 - WIN, 0) for t0 in tiles}
            self.wkeys = {t0: slice(self.wstart[t0], t0 + tq) for t0 in tiles}
            self.skeys = {t0: slice(0, t0 + tq) for t0 in tiles}

        def scores_window_compressed(self):
            self.heads, self.qs = {}, {}
            for t0 in self.tiles:
                qf = q_ref[0, t0:t0 + tq, :].astype(F32)
                hl = []
                for hh in range(hpg):
                    t = qf[:, (hh // 2) * LANES:(hh // 2 + 1) * LANES]
                    if hh % 2 == 1:
                        t = _swap_halves(t)
                    hl.append(jnp.where(low, t, 0.0))
                self.heads[t0] = hl
                self.qs[t0] = jnp.concatenate(hl, axis=0).astype(BF16)
            self.s_w = {t0: _nt(kwx_ref[0, self.wkeys[t0], :], self.qs[t0]) for t0 in self.tiles}
            self.s_c = {t0: _nt(kcm, self.qs[t0]) for t0 in self.tiles}

        def select_and_scores_selected(self):
            self.p_cmp, qsel = {}, {}
            for t0 in self.tiles:
                cmask = (crow * CMP_STRIDE + (CMP_LEN - 1)) <= (ccol + t0)
                p_all = []
                psum = jnp.zeros((n_cmp, tq), F32)
                for hh in range(hpg):
                    sh = jnp.where(cmask, self.s_c[t0][:, hh * tq:(hh + 1) * tq], NEG)
                    e = jnp.exp2(sh - jnp.max(sh, axis=0, keepdims=True))
                    p = jnp.where(cmask, e / jnp.sum(e, axis=0, keepdims=True), 0.0)
                    psum = psum + p
                    p_all.append(p.astype(BF16))
                self.p_cmp[t0] = jnp.concatenate(p_all, axis=1)
                qbias = select_blocks(psum, t0)
                qsel[t0] = jnp.concatenate([hd + qbias for hd in self.heads[t0]], axis=0).astype(BF16)
            self.s_s = {t0: _nt(ksx_ref[0, self.skeys[t0], :], qsel[t0]) for t0 in self.tiles}

        def outputs_window_compressed(self):
            e_w = {}
            for t0 in self.tiles:
                sw = masked(self.s_w[t0], self.wstart[t0], t0, True)
                e_w[t0] = jnp.exp2(sw - jnp.max(sw, axis=0, keepdims=True)).astype(BF16)
            self.o_win = {t0: normalise(_nn(vwt_ref[0, 0, :, self.wkeys[t0]], e_w[t0])) for t0 in self.tiles}
            self.o_cmp = {t0: _nn(vct, self.p_cmp[t0]) for t0 in self.tiles}

        def outputs_selected_and_store(self):
            e_s = {}
            for t0 in self.tiles:
                ss = masked(self.s_s[t0], 0, t0, False)
                e_s[t0] = jnp.exp2(ss - jnp.max(ss, axis=0, keepdims=True)).astype(BF16)
            o_sel = {t0: normalise(_nn(vst_ref[0, 0, :, self.skeys[t0]], e_s[t0])) for t0 in self.tiles}
            for t0 in self.tiles:
                gt = gate_ref[0, :, t0:t0 + tq]
                outs = []
                for hh in range(hpg):
                    hc = slice(hh * tq, (hh + 1) * tq)

                    def gate_row(branch):
                        by_group = [gt[branch * NSA_HEADS + g * hpg + hh:branch * NSA_HEADS + g * hpg + hh + 1, :]
                                    for g in range(NSA_GROUPS)]
                        row = by_group[-1]
                        for g in range(NSA_GROUPS - 2, -1, -1):
                            row = jnp.where(group == g, by_group[g], row)
                        return row

                    outs.append(gate_row(0) * self.o_cmp[t0][:, hc] + gate_row(1) * o_sel[t0][:, hc]
                                + gate_row(2) * self.o_win[t0][:, hc])
                o_ref[0, t0:t0 + tq, :] = jnp.concatenate(outs, axis=0).T.astype(o_ref.dtype)

    starts = list(range(0, seq, tq))
    order = [starts[i // 2] if i % 2 == 0 else starts[-1 - i // 2] for i in range(len(starts))]
    groups = [TileGroup(order[i:i + TILE_GROUP]) for i in range(0, len(order), TILE_GROUP)]
    n = len(groups)
    groups[0].scores_window_compressed()
    for i, g in enumerate(groups):
        if i + 1 < n:
            groups[i + 1].scores_window_compressed()
        g.select_and_scores_selected()
        g.outputs_window_compressed()
        if i > 0:
            groups[i - 1].outputs_selected_and_store()
    groups[n - 1].outputs_selected_and_store()


def _nsa_attention(nq, kvcmp, vct, ksx, vst, kwx, vwt, gate_t, ovt, weights):
    b, s, _ = nq.shape
    n_cmp = kvcmp.shape[1]
    gw = NSA_HPG * NSA_DH
    steps = b * NSA_GROUPS
    per_group = lambda rows, width: pl.BlockSpec((1, rows, width), lambda bi, g: (bi, 0, g))
    per_group_t = lambda a: pl.BlockSpec((1, 1) + a.shape[2:], lambda bi, g: (bi, g, 0, 0))
    assert all(w.shape[1] % (steps * 16) == 0 for w in weights)
    slab = lambda w: pl.BlockSpec((1, w.shape[1] // steps, w.shape[2]),
                                  lambda bi, g: (0, bi * NSA_GROUPS + g, 0))
    outs = pl.pallas_call(
        _nsa_kernel,
        grid=(b, NSA_GROUPS),
        in_specs=[per_group(s, gw),
                  per_group(n_cmp, LANES), per_group_t(vct),
                  per_group(s, LANES), per_group_t(vst), per_group(s, LANES), per_group_t(vwt),
                  pl.BlockSpec((1,) + gate_t.shape[1:], lambda bi, g: (bi, 0, 0)),
                  pl.BlockSpec(ovt.shape, lambda bi, g: (0, 0))] + [slab(w) for w in weights],
        out_specs=[per_group(s, gw)] + [slab(w) for w in weights],
        out_shape=[jax.ShapeDtypeStruct((b, s, NSA_HEADS * NSA_DH), BF16)]
                  + [jax.ShapeDtypeStruct(w.shape, BF16) for w in weights],
        compiler_params=pltpu.CompilerParams(
            dimension_semantics=("parallel", "parallel"), vmem_limit_bytes=VMEM_LIMIT),
        name="nsa_attn",
    )(nq, kvcmp, vct, ksx, vst, kwx, vwt, gate_t, ovt, *weights)
    return outs[0], [w[0] for w in outs[1:]]


def _ffn_kernel(x_ref, yr_ref, yn_ref, mod_ref, g2_ref, gf_ref, wo_ref, wg_ref, wu_ref, wd_ref,
                o_ref, x1_ref, act_ref):
    half_w = yr_ref.shape[1]
    d_ff = wg_ref.shape[1]
    n_chunks = d_ff // TF
    n_rows = x_ref.shape[0] // SUB_FF
    slabs = [slice(p * n_rows, (p + 1) * n_rows) for p in range(SUB_FF)]

    def mix(rows):
        return _nn(yr_ref[rows, :], wo_ref[0:half_w, :]) + _nn(yn_ref[rows, :], wo_ref[half_w:2 * half_w, :])

    def mid_norm(rows, mixed):
        x1 = x_ref[rows, :] + mod_ref[0, 2:3, :] * mixed
        x1_ref[rows, :] = x1
        y = x1 * lax.rsqrt(jnp.mean(x1 * x1, axis=-1, keepdims=True) + EPS) * g2_ref[...]
        return (y * (1.0 + mod_ref[0, 4:5, :]) + mod_ref[0, 3:4, :]).astype(BF16)

    def ff_chunk(rows, h2, j):
        sl = slice(j * TF, (j + 1) * TF)
        gate = _nn(h2, wg_ref[:, sl])
        up = _nn(h2, wu_ref[:, sl])
        act_ref[rows, sl] = (gate * _sigmoid(gate) * up).astype(BF16)

    def down(rows):
        return x1_ref[rows, :] + mod_ref[0, 5:6, :] * _nn(act_ref[rows, :], wd_ref[...])

    def final_norm(rows, xo):
        o_ref[rows, :] = xo * lax.rsqrt(jnp.mean(xo * xo, axis=-1, keepdims=True) + EPS) * gf_ref[...]

    mixed = [mix(rows) for rows in slabs]
    h2 = mid_norm(slabs[0], mixed[0])
    xo_prev = None
    for p, rows in enumerate(slabs):
        ff_chunk(rows, h2, 0)
        h2_next = mid_norm(slabs[p + 1], mixed[p + 1]) if p + 1 < SUB_FF else None
        if xo_prev is not None:
            final_norm(slabs[p - 1], xo_prev)
        for j in range(1, n_chunks):
            ff_chunk(rows, h2, j)
        xo_prev = down(rows)
        h2 = h2_next
    final_norm(slabs[-1], xo_prev)


def _out_ffn(x2d, y_ret, y_nsa, mod, w_out, g2, gf, wg, wu, wd, seq):
    n, d = x2d.shape
    tm = TM_FF
    d_ff = wg.shape[1]
    tiles_per_seq = seq // tm
    half_w = y_ret.shape[1]
    row = lambda i: (i, 0)
    resident = lambda a: pl.BlockSpec(a.shape, lambda i: (0, 0), pipeline_mode=pl.Buffered(1))
    return pl.pallas_call(
        _ffn_kernel,
        grid=(n // tm,),
        in_specs=[pl.BlockSpec((tm, d), row),
                  pl.BlockSpec((tm, half_w), row),
                  pl.BlockSpec((tm, half_w), row),
                  pl.BlockSpec((1, 6, d), lambda i: (i // tiles_per_seq, 0, 0)),
                  pl.BlockSpec((1, d), lambda i: (0, 0)),
                  pl.BlockSpec((1, d), lambda i: (0, 0)),
                  resident(w_out), resident(wg), resident(wu), resident(wd)],
        out_specs=pl.BlockSpec((tm, d), row),
        out_shape=jax.ShapeDtypeStruct((n, d), F32),
        scratch_shapes=[pltpu.VMEM((tm, d), F32), pltpu.VMEM((tm, d_ff), BF16)],
        compiler_params=pltpu.CompilerParams(
            dimension_semantics=("parallel",), vmem_limit_bytes=VMEM_LIMIT),
        name="out_ffn",
    )(x2d, y_ret, y_nsa, mod, g2.reshape(1, d), gf.reshape(1, d), w_out, wg, wu, wd)


def kernel(x, c, ln_mix_g, ln_ffn_g, w_ada, b_ada, w_in, cmp_pe_k, cmp_w1_k, cmp_w2_k,
           cmp_pe_v, cmp_w1_v, cmp_w2_v, w_out, w_ff_gate, w_ff_up, w_ff_down, ln_final_g):
    assert w_in.shape[0] == 1, "the final RMSNorm is fused into the (single) layer's FFN kernel"
    b, s, d = x.shape
    lane = np.arange(LANES)
    tabs = _rope_tables(np.arange(s), np.ones(LANES, bool))
    n_piece = s // CMP_STRIDE
    tabs_cmp = _rope_tables(np.arange(n_piece) * CMP_STRIDE + CMP_LEN - 1, lane < NSA_DH)

    mod = _ada(c, w_ada[0], b_ada[0]).reshape(b, 6, d)
    rq, rk, rv, rg_act, nq, kc, vc, ksx, vst, kwx, vwt, gate_t = _in_proj(
        x, ln_mix_g[0], mod, w_in, tabs, _block_onehot_table(s))
    y_ret = _retention(rq, rk, rv, rg_act, _retention_tables())
    w1, w2, pe = _compress_weights(cmp_w1_k[0], cmp_w2_k[0], cmp_w1_v[0], cmp_w2_v[0],
                                   cmp_pe_k[0], cmp_pe_v[0])
    kvcmp, vct = _compress(kc, vc, w1, w2, pe, tabs_cmp)
    y_nsa, (wo_b, wg_b, wu_b, wd_b) = _nsa_attention(
        nq, kvcmp, vct, ksx, vst, kwx, vwt, gate_t, _overlap_t(s), (w_out, w_ff_gate, w_ff_up, w_ff_down))
    out = _out_ffn(x.reshape(b * s, d), y_ret.reshape(b * s, -1), y_nsa.reshape(b * s, -1), mod,
                   wo_b, ln_ffn_g[0], ln_final_g, wg_b, wu_b, wd_b, s)
    return out.reshape(b, s, d)
```

```python
import numpy as np
import jax
import jax.numpy as jnp
from jax import lax
from jax.experimental import pallas as pl
from jax.experimental.pallas import tpu as pltpu

F32 = jnp.float32
BF16 = jnp.bfloat16

D_MODEL = 1024
RET_HEADS = 4
RET_DK = 64
RET_DV = 128
RET_CHUNK = 128
NSA_HEADS = 8
NSA_GROUPS = 2
NSA_HPG = NSA_HEADS // NSA_GROUPS
NSA_DH = 64
CMP_LEN = 32
CMP_STRIDE = 16
CMP_HIDDEN = 128
SLC_LEN = 64
SLC_TOPK = 16
WIN = 512
D_FF = ((8 * D_MODEL + 3 * 256 - 1) // (3 * 256)) * 256
ROPE_THETA = 10000.0
EPS = 1e-6
NEG = -1e30
FORCE = 1e6

LANES = 128
SUBLANES = 8
HALF = NSA_DH // 2
ONES_ROWS = 16
GATE_ROWS = 3 * NSA_HEADS

TK_ADA = 256
TM_IN = 1024
PROJ_W = 512
RET_GROUP = 16
Q_SCALE = NSA_DH ** -0.5 * float(np.log2(np.e))
TQ = 128
TILE_GROUP = 2
TM_FF = 512
SUB_IN = 2
SUB_FF = 2
TF = 256
VMEM_LIMIT = 56 * 1024 * 1024

_R_RQ, _R_RK, _R_RV, _R_RG, _R_NQ = 0, 256, 512, 1024, 1536
_R_KC, _R_VC, _R_KS, _R_VS, _R_KW, _R_VW, _R_GATE = 2048, 2176, 2304, 2432, 2560, 2688, 2816
IN_COLS_K = -(-(_R_GATE + 3 * NSA_HEADS) // PROJ_W) * PROJ_W


def _sigmoid(x):
    return 1.0 / (1.0 + jnp.exp(-x))


def _nt(a, b):
    return lax.dot_general(a, b, (((1,), (1,)), ((), ())), preferred_element_type=F32)


def _nn(a, b):
    return jnp.dot(a, b, preferred_element_type=F32)


def _rope_tile(a, c, s1, s2):
    return a * c + pltpu.roll(a, HALF, 1) * s1 + pltpu.roll(a, LANES - HALF, 1) * s2


def _swap_halves(a):
    return pltpu.roll(a, LANES // 2, 1)


def _rope_tables(pos, rotary_lanes):
    pos = np.asarray(pos, np.float64)
    lane = np.arange(LANES)
    within = lane % NSA_DH
    freq = ROPE_THETA ** (-(within % HALF).astype(np.float64) / HALF)
    ang = pos[:, None] * freq[None, :]
    cos, sin = np.cos(ang), np.sin(ang)
    first = (within < HALF)[None, :]
    rot = np.asarray(rotary_lanes, bool)[None, :]
    c = np.where(rot, cos, 1.0)
    s1 = np.where(rot & ~first, sin, 0.0)
    s2 = np.where(rot & first, -sin, 0.0)
    return (jnp.asarray(c, F32), jnp.asarray(s1, F32), jnp.asarray(s2, F32))


def _block_onehot_table(seq):
    t = np.zeros((seq, LANES), np.float32)
    pos = np.arange(seq)
    t[pos, NSA_DH + pos // SLC_LEN] = 1.0
    return jnp.asarray(t)


def _retention_tables():
    h = np.arange(RET_HEADS, dtype=np.float64)
    log_g = np.log(1.0 - 2.0 ** (-5.0 - h))
    c = RET_CHUNK
    idx = np.arange(c, dtype=np.float64)
    diff = idx[:, None] - idx[None, :]
    causal = diff >= 0
    decay = np.where(causal, np.exp(log_g[:, None, None] * np.where(causal, diff, 0.0)), 0.0)
    zeta = np.exp(log_g[:, None] * (c - 1.0 - idx))
    xi = np.exp(log_g[:, None] * (idx + 1.0))
    g_chunk = np.exp(log_g * c)

    def pair_lanes(t):
        t = t.reshape(RET_HEADS // 2, 2, c)
        return np.repeat(np.transpose(t, (0, 2, 1)), RET_DK, axis=2)

    g_b = np.broadcast_to(g_chunk[:, None, None], (RET_HEADS, 1, LANES))
    return tuple(jnp.asarray(t, F32) for t in (decay, pair_lanes(zeta), pair_lanes(xi), g_b))


def _overlap_t(seq):
    n_c = seq // CMP_STRIDE - CMP_LEN // CMP_STRIDE + 1
    nb = seq // SLC_LEN
    cs = np.arange(n_c) * CMP_STRIDE
    bs = np.arange(nb) * SLC_LEN
    ov = np.maximum(np.minimum(cs[:, None] + CMP_LEN, bs[None] + SLC_LEN)
                    - np.maximum(cs[:, None], bs[None]), 0).astype(np.float64) / CMP_LEN
    ncp = seq // CMP_STRIDE
    ovp = np.zeros((ncp, nb))
    ovp[:n_c] = ov
    return jnp.asarray(ovp.T, BF16)


def _ada_kernel(c_ref, w_ref, b_ref, o_ref):
    c = c_ref[...]
    part = _nn(c * _sigmoid(c), w_ref[...])

    @pl.when(pl.program_id(0) == 0)
    def _():
        o_ref[...] = part + b_ref[...]

    @pl.when(pl.program_id(0) > 0)
    def _():
        o_ref[...] += part


def _ada(c, w, b):
    bsz, d = c.shape
    n = w.shape[1]
    tk = TK_ADA
    return pl.pallas_call(
        _ada_kernel,
        grid=(d // tk,),
        in_specs=[pl.BlockSpec((bsz, tk), lambda k: (0, k)),
                  pl.BlockSpec((tk, n), lambda k: (k, 0)),
                  pl.BlockSpec((1, n), lambda k: (0, 0))],
        out_specs=pl.BlockSpec((bsz, n), lambda k: (0, 0)),
        out_shape=jax.ShapeDtypeStruct((bsz, n), F32),
        compiler_params=pltpu.CompilerParams(
            dimension_semantics=("arbitrary",), vmem_limit_bytes=VMEM_LIMIT),
        name="ada",
    )(c, w, b.reshape(1, n))


def _inproj_kernel(x_ref, g_ref, mod_ref, w_ref, c_ref, s1_ref, s2_ref, hot_ref,
                   rq_ref, rk_ref, rv_ref, rg_ref, nq_ref, kc_ref, vc_ref,
                   ksx_ref, vst_ref, kwx_ref, vwt_ref, gate_ref, wb_ref):
    @pl.when((pl.program_id(0) == 0) & (pl.program_id(1) == 0))
    def _():
        n_in = w_ref.shape[2]
        whole = (n_in // PROJ_W) * PROJ_W
        for c0 in range(0, whole, PROJ_W):
            wb_ref[:, c0:c0 + PROJ_W] = w_ref[0, :, c0:c0 + PROJ_W].astype(BF16)
        wb_ref[:, whole:] = jnp.zeros((wb_ref.shape[0], wb_ref.shape[1] - whole), BF16)
        wb_ref[:, whole:n_in] = w_ref[0, :, whole:n_in].astype(BF16)

    def modulated_norm(rows):
        x = x_ref[0, rows, :]
        y = x * lax.rsqrt(jnp.mean(x * x, axis=-1, keepdims=True) + EPS) * g_ref[...]
        return (y * (1.0 + mod_ref[0, 1:2, :]) + mod_ref[0, 0:1, :]).astype(BF16)

    def product_steps(rows, hb):
        tabs = (c_ref[rows, :], s1_ref[rows, :], s2_ref[rows, :])
        low = lax.broadcasted_iota(jnp.int32, (rows.stop - rows.start, LANES), 1) < NSA_DH

        def proj(c0):
            a = _nn(hb, wb_ref[:, c0:c0 + PROJ_W])
            return [a[:, t * LANES:(t + 1) * LANES] for t in range(PROJ_W // LANES)]

        def put(out_ref, t, value):
            out_ref[0, rows, t * LANES:(t + 1) * LANES] = value.astype(out_ref.dtype)

        def roped(tiles, scale, out_ref):
            for t, a in enumerate(tiles):
                r = _rope_tile(a, *tabs)
                put(out_ref, t, r if scale == 1.0 else r * scale)

        def per_group(tile, fill, out_ref):
            put(out_ref, 0, jnp.where(low, tile, fill))
            put(out_ref, 1, jnp.where(low, _swap_halves(tile), fill))

        def per_group_t(tile, out_ref):
            t = tile.T
            ones = jnp.ones((ONES_ROWS, t.shape[1]), out_ref.dtype)
            for g in range(NSA_GROUPS):
                out_ref[0, g, 0:NSA_DH, rows] = t[g * NSA_DH:(g + 1) * NSA_DH].astype(out_ref.dtype)
                out_ref[0, g, NSA_DH:NSA_DH + ONES_ROWS, rows] = ones

        def retention_qk():
            tiles = proj(_R_RQ)
            roped(tiles[0:2], 1.0, rq_ref)
            roped(tiles[2:4], RET_DK ** -0.5, rk_ref)

        def retention_v():
            for t, a in enumerate(proj(_R_RV)):
                put(rv_ref, t, a)

        def retention_gate():
            for t, a in enumerate(proj(_R_RG)):
                put(rg_ref, t, a * _sigmoid(a))

        def nsa_q():
            roped(proj(_R_NQ), Q_SCALE, nq_ref)

        def nsa_compress_selected():
            kc_t, vc_t, ks_t, vs_t = proj(_R_KC)
            kc_ref[0, rows, :] = kc_t
            vc_ref[0, rows, :] = vc_t
            per_group(_rope_tile(ks_t, *tabs), hot_ref[rows, :], ksx_ref)
            per_group_t(vs_t, vst_ref)

        def nsa_window_gates():
            kw_t, vw_t, gates_t, _ = proj(_R_KW)
            per_group(_rope_tile(kw_t, *tabs), 0.0, kwx_ref)
            per_group_t(vw_t, vwt_ref)
            gate_ref[0, :, rows] = _sigmoid(gates_t.T[0:GATE_ROWS])

        return [retention_qk, retention_v, retention_gate, nsa_q, nsa_compress_selected, nsa_window_gates]

    n_rows = x_ref.shape[1] // SUB_IN
    slabs = [slice(p * n_rows, (p + 1) * n_rows) for p in range(SUB_IN)]
    hb = modulated_norm(slabs[0])
    for p, rows in enumerate(slabs):
        steps = product_steps(rows, hb)
        steps[0]()
        if p + 1 < SUB_IN:
            hb = modulated_norm(slabs[p + 1])
        for step in steps[1:]:
            step()


def _in_proj(x, ln_g, mod, w_in, tabs, hot):
    b, s, d = x.shape
    tm = TM_IN
    grid = (b, s // tm)
    tab_spec = pl.BlockSpec((tm, LANES), lambda bi, j: (j, 0))

    def out(n, dtype):
        return (jax.ShapeDtypeStruct((b, s, n), dtype), pl.BlockSpec((1, tm, n), lambda bi, j: (bi, j, 0)))

    def out_t(rows, dtype):
        return (jax.ShapeDtypeStruct((b, NSA_GROUPS, rows, s), dtype),
                pl.BlockSpec((1, NSA_GROUPS, rows, tm), lambda bi, j: (bi, 0, 0, j)))

    vt_rows = NSA_DH + ONES_ROWS
    outs = [out(256, BF16), out(256, BF16), out(512, BF16), out(512, BF16), out(512, BF16),
            out(LANES, F32), out(LANES, F32),
            out(256, BF16), out_t(vt_rows, BF16), out(256, BF16), out_t(vt_rows, BF16),
            (jax.ShapeDtypeStruct((b, GATE_ROWS, s), F32),
             pl.BlockSpec((1, GATE_ROWS, tm), lambda bi, j: (bi, 0, j)))]
    return pl.pallas_call(
        _inproj_kernel,
        grid=grid,
        in_specs=[pl.BlockSpec((1, tm, d), lambda bi, j: (bi, j, 0)),
                  pl.BlockSpec((1, d), lambda bi, j: (0, 0)),
                  pl.BlockSpec((1, 6, d), lambda bi, j: (bi, 0, 0)),
                  pl.BlockSpec((1,) + w_in.shape[1:], lambda bi, j: (0, 0, 0), pipeline_mode=pl.Buffered(1)),
                  tab_spec, tab_spec, tab_spec, tab_spec],
        out_specs=[o[1] for o in outs],
        out_shape=[o[0] for o in outs],
        scratch_shapes=[pltpu.VMEM((d, IN_COLS_K), BF16)],
        compiler_params=pltpu.CompilerParams(
            dimension_semantics=("arbitrary", "arbitrary"), vmem_limit_bytes=VMEM_LIMIT),
        name="in_proj",
    )(x, ln_g.reshape(1, d), mod, w_in, *tabs, hot)


def _ret_kernel(q_ref, k_ref, v_ref, rg_ref, dec_ref, zeta_ref, xi_ref, gch_ref, o_ref, kv_ref, prev_ref):
    c = RET_CHUNK
    n_chunks = q_ref.shape[1] // c
    low = lax.broadcasted_iota(jnp.int32, (c, LANES), 1) < RET_DK

    def chunk_rows(n):
        return pl.ds(pl.multiple_of(n * c, c), c)

    def head_cols(h):
        return slice(h * RET_DV, (h + 1) * RET_DV)

    def kv_body(it, carry):
        kz_t = {}
        for j in range(RET_GROUP):
            rows = chunk_rows(it * RET_GROUP + j)
            for p in range(RET_HEADS // 2):
                pair = slice(p * LANES, (p + 1) * LANES)
                kz_t[j, p] = (k_ref[0, rows, pair].astype(F32) * zeta_ref[p]).T.astype(BF16)
        for j in range(RET_GROUP):
            n = it * RET_GROUP + j
            for h in range(RET_HEADS):
                kv_ref[h, n] = _nn(kz_t[j, h // 2], v_ref[0, chunk_rows(n), head_cols(h)])
        return carry

    lax.fori_loop(0, n_chunks // RET_GROUP, kv_body, 0)

    for h in range(RET_HEADS):
        def scan_body(n, st, h=h):
            prev_ref[h, n] = st.astype(prev_ref.dtype)
            return st * gch_ref[h] + kv_ref[h, n]
        lax.fori_loop(0, n_chunks, scan_body, jnp.zeros((LANES, RET_DV), F32))

    def out_body(it, carry):
        chains = [(j, h) for j in range(RET_GROUP) for h in range(RET_HEADS)]
        chunk = lambda j: it * RET_GROUP + j
        q_own, qx_own, att, ys = {}, {}, {}, {}
        for j, h in chains:
            p, e = divmod(h, 2)
            pair = slice(p * LANES, (p + 1) * LANES)
            mine = low if e == 0 else jnp.logical_not(low)
            q2 = q_ref[0, chunk_rows(chunk(j)), pair].astype(F32)
            q_own[j, h] = jnp.where(mine, q2, 0.0).astype(BF16)
            qx_own[j, h] = jnp.where(mine, q2 * xi_ref[p], 0.0).astype(BF16)
        for j, h in chains:
            pair = slice((h // 2) * LANES, (h // 2 + 1) * LANES)
            att[j, h] = _nt(q_own[j, h], k_ref[0, chunk_rows(chunk(j)), pair])
        for j, h in chains:
            lhs = jnp.concatenate([(att[j, h] * dec_ref[h]).astype(BF16), qx_own[j, h]], axis=1)
            rhs = jnp.concatenate([v_ref[0, chunk_rows(chunk(j)), head_cols(h)], prev_ref[h, chunk(j)]], axis=0)
            ys[j, h] = _nn(lhs, rhs)
        for j, h in chains:
            y = ys[j, h]
            yn = y * lax.rsqrt(jnp.mean(y * y, axis=-1, keepdims=True) + EPS)
            gate = rg_ref[0, chunk_rows(chunk(j)), head_cols(h)].astype(F32)
            o_ref[0, chunk_rows(chunk(j)), head_cols(h)] = (yn * gate).astype(o_ref.dtype)
        return carry

    lax.fori_loop(0, n_chunks // RET_GROUP, out_body, 0)


def _retention(rq, rk, rv, rg_act, tables):
    b, s, _ = rq.shape
    decay, zeta_p, xi_p, g_b = tables
    whole = lambda a: pl.BlockSpec(a.shape, lambda bi: (0,) * a.ndim)
    row = lambda a: pl.BlockSpec((1,) + a.shape[1:], lambda bi: (bi, 0, 0))
    return pl.pallas_call(
        _ret_kernel,
        grid=(b,),
        in_specs=[row(rq), row(rk), row(rv), row(rg_act),
                  whole(decay), whole(zeta_p), whole(xi_p), whole(g_b)],
        out_specs=row(rv),
        out_shape=jax.ShapeDtypeStruct(rv.shape, BF16),
        scratch_shapes=[pltpu.VMEM((RET_HEADS, s // RET_CHUNK, LANES, RET_DV), F32),
                        pltpu.VMEM((RET_HEADS, s // RET_CHUNK, LANES, RET_DV), BF16)],
        compiler_params=pltpu.CompilerParams(
            dimension_semantics=("parallel",), vmem_limit_bytes=VMEM_LIMIT),
        name="retention",
    )(rq, rk, rv, rg_act, decay, zeta_p, xi_p, g_b)


def _cmp_kernel(kc_ref, vc_ref, w1_ref, pe_ref, w2_ref, c_ref, s1_ref, s2_ref, kv_ref, vt_ref):
    n_piece = kc_ref.shape[1] // CMP_STRIDE
    halves = CMP_LEN // CMP_STRIDE
    lhs = {}
    for t, src in enumerate((kc_ref, vc_ref)):
        rows = [src[0, pl.ds(r, n_piece, stride=CMP_STRIDE), :] for r in range(CMP_STRIDE)]
        for half in range(halves):
            pe0 = half * CMP_STRIDE
            lhs[t, half] = jnp.concatenate(
                [(rows[r] + pe_ref[t, pe0 + r:pe0 + r + 1, :]).astype(BF16) for r in range(CMP_STRIDE)], axis=1)
    part = {key: _nn(lhs[key], w1_ref[key[0], key[1]]) for key in lhs}
    out = jnp.zeros((n_piece, kv_ref.shape[2]), F32)
    for t in range(2):
        hid = part[t, 0]
        for half in range(1, halves):
            hid = hid + pltpu.roll(part[t, half], n_piece - half, 0)
        out = out + _nn((hid * _sigmoid(hid)).astype(BF16), w2_ref[t])
    for t in range(out.shape[1] // LANES):
        sl = slice(t * LANES, (t + 1) * LANES)
        kv = _rope_tile(out[:, sl], c_ref[...], s1_ref[...], s2_ref[...])
        kv_ref[0, :, sl] = kv.astype(kv_ref.dtype)
        vt_ref[0, t] = kv.T[NSA_DH:2 * NSA_DH].astype(vt_ref.dtype)


def _compress_weights(w1_k, w2_k, w1_v, w2_v, pe_k, pe_v):
    dh, hid = NSA_DH, CMP_HIDDEN

    def first(w1):
        w = w1.reshape(CMP_LEN, dh, hid)
        z = jnp.zeros_like(w)
        return jnp.concatenate([jnp.concatenate([w, z], axis=2), jnp.concatenate([z, w], axis=2)], axis=1)

    def second(w2, off):
        z = jnp.zeros_like(w2)
        rows = []
        for g in range(NSA_GROUPS):
            c = [z, z, z, z]
            c[2 * g + off] = w2
            rows.append(jnp.concatenate(c, axis=1))
        return jnp.concatenate(rows, axis=0)

    w1 = jnp.stack([first(w1_k), first(w1_v)]).astype(BF16).reshape(
        2, CMP_LEN // CMP_STRIDE, CMP_STRIDE * NSA_GROUPS * dh, NSA_GROUPS * hid)
    w2 = jnp.stack([second(w2_k, 0), second(w2_v, 1)]).astype(BF16)
    pe = jnp.stack([jnp.concatenate([pe_k, pe_k], axis=1), jnp.concatenate([pe_v, pe_v], axis=1)])
    return w1, w2, pe


def _compress(kc, vc, w1, w2, pe, tabs_cmp):
    b, s, w = kc.shape
    n_piece = s // CMP_STRIDE
    const2 = lambda bi: (0, 0)
    src = pl.BlockSpec((1, s, w), lambda bi: (bi, 0, 0))
    dst = pl.BlockSpec((1, n_piece, 2 * w), lambda bi: (bi, 0, 0))
    shape = jax.ShapeDtypeStruct((b, n_piece, 2 * w), BF16)
    return pl.pallas_call(
        _cmp_kernel,
        grid=(b,),
        in_specs=[src, src,
                  pl.BlockSpec(w1.shape, lambda bi: (0, 0, 0, 0)),
                  pl.BlockSpec(pe.shape, lambda bi: (0, 0, 0)),
                  pl.BlockSpec(w2.shape, lambda bi: (0, 0, 0)),
                  pl.BlockSpec((n_piece, LANES), const2),
                  pl.BlockSpec((n_piece, LANES), const2),
                  pl.BlockSpec((n_piece, LANES), const2)],
        out_specs=[dst, pl.BlockSpec((1, NSA_GROUPS, NSA_DH, n_piece), lambda bi: (bi, 0, 0, 0))],
        out_shape=[shape, jax.ShapeDtypeStruct((b, NSA_GROUPS, NSA_DH, n_piece), BF16)],
        compiler_params=pltpu.CompilerParams(
            dimension_semantics=("parallel",), vmem_limit_bytes=VMEM_LIMIT),
        name="compress",
    )(kc, vc, w1, pe, w2, *tabs_cmp)


def _nsa_kernel(q_ref, kcmp_ref, vct_ref, ksx_ref, vst_ref, kwx_ref, vwt_ref, gate_ref, ovt_ref, *rest):
    n_w = (len(rest) - 1) // 2
    o_ref = rest[n_w]
    for w_src, w_dst in zip(rest[:n_w], rest[n_w + 1:]):
        w_dst[...] = w_src[...].astype(w_dst.dtype)

    tq = TQ
    seq = q_ref.shape[1]
    nb = seq // SLC_LEN
    n_cmp = kcmp_ref.shape[1]
    hpg, dh = NSA_HPG, NSA_DH
    assert WIN % tq == 0 and seq % tq == 0
    group = pl.program_id(1)

    low = lax.broadcasted_iota(jnp.int32, (tq, LANES), 1) < dh
    eye = jnp.where(lax.broadcasted_iota(jnp.int32, (tq, tq), 0)
                    == lax.broadcasted_iota(jnp.int32, (tq, tq), 1), 1.0, 0.0).astype(BF16)
    blk = lax.broadcasted_iota(jnp.int32, (nb, tq), 0)
    col = lax.broadcasted_iota(jnp.int32, (nb, tq), 1)
    crow = lax.broadcasted_iota(jnp.int32, (n_cmp, tq), 0)
    ccol = lax.broadcasted_iota(jnp.int32, (n_cmp, tq), 1)
    kcm = kcmp_ref[0]
    vct = vct_ref[0, 0]
    ovt = ovt_ref[...]

    def per_head(x):
        return jnp.concatenate([x] * hpg, axis=1)

    key_off = lax.broadcasted_iota(jnp.int32, (tq, tq), 0)
    qry_off = lax.broadcasted_iota(jnp.int32, (tq, tq), 1)
    not_after = per_head(jnp.where(key_off <= qry_off, 0.0, NEG))
    inside_win = per_head(jnp.where(key_off > qry_off, 0.0, NEG))

    def masked(s, first_key, t0, windowed):
        blocks = []
        for r in range(0, s.shape[0], tq):
            blk_s = s[r:r + tq]
            if first_key + r == t0:
                blk_s = blk_s + not_after
            elif windowed and first_key + r == t0 - WIN:
                blk_s = blk_s + inside_win
            blocks.append(blk_s)
        return jnp.concatenate(blocks, axis=0)

    def normalise(acc):
        return acc[0:dh] / acc[dh:dh + 1]

    def select_blocks(psum, t0):
        tcol = col + t0
        bcausal = blk * SLC_LEN <= tcol
        n_live = (t0 + tq - 1) // SLC_LEN + 1
        top_n = min(SLC_TOPK, nb)
        if n_live <= top_n:
            bias = jnp.where(bcausal, 0.0, NEG)
        else:
            p_hi = psum.astype(BF16)
            p_lo = (psum - p_hi.astype(F32)).astype(BF16)
            imp = _nn(ovt, p_hi) + _nn(ovt, p_lo)
            cur = tcol // SLC_LEN
            forced = (blk == 0) | (blk == cur) | (blk == cur - 1)
            imp = jnp.where(bcausal, jnp.where(forced, FORCE, imp), NEG)
            rank = jnp.zeros((nb, tq), F32)
            for j in range(n_live):
                r = imp[j:j + 1, :]
                rank = rank + jnp.where(blk > j, jnp.where(r >= imp, 1.0, 0.0), jnp.where(r > imp, 1.0, 0.0))
            bias = jnp.where((rank < float(top_n)) & bcausal, 0.0, NEG)
        feat = jnp.concatenate([jnp.zeros((dh, tq), F32), bias,
                                jnp.zeros((LANES - dh - nb, tq), F32)], axis=0).astype(BF16)
        return _nt(eye, feat)

    class TileGroup:
        def __init__(self, tiles):
            self.tiles = tiles
            self.wstart = {t0: max(t0 - WIN, 0) for t0 in tiles}
            self.wkeys = {t0: slice(self.wstart[t0], t0 + tq) for t0 in tiles}
            self.skeys = {t0: slice(0, t0 + tq) for t0 in tiles}

        def scores_window_compressed(self):
            self.heads, self.qs = {}, {}
            for t0 in self.tiles:
                qf = q_ref[0, t0:t0 + tq, :].astype(F32)
                hl = []
                for hh in range(hpg):
                    t = qf[:, (hh // 2) * LANES:(hh // 2 + 1) * LANES]
                    if hh % 2 == 1:
                        t = _swap_halves(t)
                    hl.append(jnp.where(low, t, 0.0))
                self.heads[t0] = hl
                self.qs[t0] = jnp.concatenate(hl, axis=0).astype(BF16)
            self.s_w = {t0: _nt(kwx_ref[0, self.wkeys[t0], :], self.qs[t0]) for t0 in self.tiles}
            self.s_c = {t0: _nt(kcm, self.qs[t0]) for t0 in self.tiles}

        def select_and_scores_selected(self):
            self.p_cmp, qsel = {}, {}
            for t0 in self.tiles:
                cmask = (crow * CMP_STRIDE + (CMP_LEN - 1)) <= (ccol + t0)
                p_all = []
                psum = jnp.zeros((n_cmp, tq), F32)
                for hh in range(hpg):
                    sh = jnp.where(cmask, self.s_c[t0][:, hh * tq:(hh + 1) * tq], NEG)
                    e = jnp.exp2(sh - jnp.max(sh, axis=0, keepdims=True))
                    p = jnp.where(cmask, e / jnp.sum(e, axis=0, keepdims=True), 0.0)
                    psum = psum + p
                    p_all.append(p.astype(BF16))
                self.p_cmp[t0] = jnp.concatenate(p_all, axis=1)
                qbias = select_blocks(psum, t0)
                qsel[t0] = jnp.concatenate([hd + qbias for hd in self.heads[t0]], axis=0).astype(BF16)
            self.s_s = {t0: _nt(ksx_ref[0, self.skeys[t0], :], qsel[t0]) for t0 in self.tiles}

        def outputs_window_compressed(self):
            e_w = {}
            for t0 in self.tiles:
                sw = masked(self.s_w[t0], self.wstart[t0], t0, True)
                e_w[t0] = jnp.exp2(sw - jnp.max(sw, axis=0, keepdims=True)).astype(BF16)
            self.o_win = {t0: normalise(_nn(vwt_ref[0, 0, :, self.wkeys[t0]], e_w[t0])) for t0 in self.tiles}
            self.o_cmp = {t0: _nn(vct, self.p_cmp[t0]) for t0 in self.tiles}

        def outputs_selected_and_store(self):
            e_s = {}
            for t0 in self.tiles:
                ss = masked(self.s_s[t0], 0, t0, False)
                e_s[t0] = jnp.exp2(ss - jnp.max(ss, axis=0, keepdims=True)).astype(BF16)
            o_sel = {t0: normalise(_nn(vst_ref[0, 0, :, self.skeys[t0]], e_s[t0])) for t0 in self.tiles}
            for t0 in self.tiles:
                gt = gate_ref[0, :, t0:t0 + tq]
                outs = []
                for hh in range(hpg):
                    hc = slice(hh * tq, (hh + 1) * tq)

                    def gate_row(branch):
                        by_group = [gt[branch * NSA_HEADS + g * hpg + hh:branch * NSA_HEADS + g * hpg + hh + 1, :]
                                    for g in range(NSA_GROUPS)]
                        row = by_group[-1]
                        for g in range(NSA_GROUPS - 2, -1, -1):
                            row = jnp.where(group == g, by_group[g], row)
                        return row

                    outs.append(gate_row(0) * self.o_cmp[t0][:, hc] + gate_row(1) * o_sel[t0][:, hc]
                                + gate_row(2) * self.o_win[t0][:, hc])
                o_ref[0, t0:t0 + tq, :] = jnp.concatenate(outs, axis=0).T.astype(o_ref.dtype)

    starts = list(range(0, seq, tq))
    order = [starts[i // 2] if i % 2 == 0 else starts[-1 - i // 2] for i in range(len(starts))]
    groups = [TileGroup(order[i:i + TILE_GROUP]) for i in range(0, len(order), TILE_GROUP)]
    n = len(groups)
    groups[0].scores_window_compressed()
    for i, g in enumerate(groups):
        if i + 1 < n:
            groups[i + 1].scores_window_compressed()
        g.select_and_scores_selected()
        g.outputs_window_compressed()
        if i > 0:
            groups[i - 1].outputs_selected_and_store()
    groups[n - 1].outputs_selected_and_store()


def _nsa_attention(nq, kvcmp, vct, ksx, vst, kwx, vwt, gate_t, ovt, weights):
    b, s, _ = nq.shape
    n_cmp = kvcmp.shape[1]
    gw = NSA_HPG * NSA_DH
    steps = b * NSA_GROUPS
    per_group = lambda rows, width: pl.BlockSpec((1, rows, width), lambda bi, g: (bi, 0, g))
    per_group_t = lambda a: pl.BlockSpec((1, 1) + a.shape[2:], lambda bi, g: (bi, g, 0, 0))
    assert all(w.shape[1] % (steps * 16) == 0 for w in weights)
    slab = lambda w: pl.BlockSpec((1, w.shape[1] // steps, w.shape[2]),
                                  lambda bi, g: (0, bi * NSA_GROUPS + g, 0))
    outs = pl.pallas_call(
        _nsa_kernel,
        grid=(b, NSA_GROUPS),
        in_specs=[per_group(s, gw),
                  per_group(n_cmp, LANES), per_group_t(vct),
                  per_group(s, LANES), per_group_t(vst), per_group(s, LANES), per_group_t(vwt),
                  pl.BlockSpec((1,) + gate_t.shape[1:], lambda bi, g: (bi, 0, 0)),
                  pl.BlockSpec(ovt.shape, lambda bi, g: (0, 0))] + [slab(w) for w in weights],
        out_specs=[per_group(s, gw)] + [slab(w) for w in weights],
        out_shape=[jax.ShapeDtypeStruct((b, s, NSA_HEADS * NSA_DH), BF16)]
                  + [jax.ShapeDtypeStruct(w.shape, BF16) for w in weights],
        compiler_params=pltpu.CompilerParams(
            dimension_semantics=("parallel", "parallel"), vmem_limit_bytes=VMEM_LIMIT),
        name="nsa_attn",
    )(nq, kvcmp, vct, ksx, vst, kwx, vwt, gate_t, ovt, *weights)
    return outs[0], [w[0] for w in outs[1:]]


def _ffn_kernel(x_ref, yr_ref, yn_ref, mod_ref, g2_ref, gf_ref, wo_ref, wg_ref, wu_ref, wd_ref,
                o_ref, x1_ref, act_ref):
    half_w = yr_ref.shape[1]
    d_ff = wg_ref.shape[1]
    n_chunks = d_ff // TF
    n_rows = x_ref.shape[0] // SUB_FF
    slabs = [slice(p * n_rows, (p + 1) * n_rows) for p in range(SUB_FF)]

    def mix(rows):
        return _nn(yr_ref[rows, :], wo_ref[0:half_w, :]) + _nn(yn_ref[rows, :], wo_ref[half_w:2 * half_w, :])

    def mid_norm(rows, mixed):
        x1 = x_ref[rows, :] + mod_ref[0, 2:3, :] * mixed
        x1_ref[rows, :] = x1
        y = x1 * lax.rsqrt(jnp.mean(x1 * x1, axis=-1, keepdims=True) + EPS) * g2_ref[...]
        return (y * (1.0 + mod_ref[0, 4:5, :]) + mod_ref[0, 3:4, :]).astype(BF16)

    def ff_chunk(rows, h2, j):
        sl = slice(j * TF, (j + 1) * TF)
        gate = _nn(h2, wg_ref[:, sl])
        up = _nn(h2, wu_ref[:, sl])
        act_ref[rows, sl] = (gate * _sigmoid(gate) * up).astype(BF16)

    def down(rows):
        return x1_ref[rows, :] + mod_ref[0, 5:6, :] * _nn(act_ref[rows, :], wd_ref[...])

    def final_norm(rows, xo):
        o_ref[rows, :] = xo * lax.rsqrt(jnp.mean(xo * xo, axis=-1, keepdims=True) + EPS) * gf_ref[...]

    mixed = [mix(rows) for rows in slabs]
    h2 = mid_norm(slabs[0], mixed[0])
    xo_prev = None
    for p, rows in enumerate(slabs):
        ff_chunk(rows, h2, 0)
        h2_next = mid_norm(slabs[p + 1], mixed[p + 1]) if p + 1 < SUB_FF else None
        if xo_prev is not None:
            final_norm(slabs[p - 1], xo_prev)
        for j in range(1, n_chunks):
            ff_chunk(rows, h2, j)
        xo_prev = down(rows)
        h2 = h2_next
    final_norm(slabs[-1], xo_prev)


def _out_ffn(x2d, y_ret, y_nsa, mod, w_out, g2, gf, wg, wu, wd, seq):
    n, d = x2d.shape
    tm = TM_FF
    d_ff = wg.shape[1]
    tiles_per_seq = seq // tm
    half_w = y_ret.shape[1]
    row = lambda i: (i, 0)
    resident = lambda a: pl.BlockSpec(a.shape, lambda i: (0, 0), pipeline_mode=pl.Buffered(1))
    return pl.pallas_call(
        _ffn_kernel,
        grid=(n // tm,),
        in_specs=[pl.BlockSpec((tm, d), row),
                  pl.BlockSpec((tm, half_w), row),
                  pl.BlockSpec((tm, half_w), row),
                  pl.BlockSpec((1, 6, d), lambda i: (i // tiles_per_seq, 0, 0)),
                  pl.BlockSpec((1, d), lambda i: (0, 0)),
                  pl.BlockSpec((1, d), lambda i: (0, 0)),
                  resident(w_out), resident(wg), resident(wu), resident(wd)],
        out_specs=pl.BlockSpec((tm, d), row),
        out_shape=jax.ShapeDtypeStruct((n, d), F32),
        scratch_shapes=[pltpu.VMEM((tm, d), F32), pltpu.VMEM((tm, d_ff), BF16)],
        compiler_params=pltpu.CompilerParams(
            dimension_semantics=("parallel",), vmem_limit_bytes=VMEM_LIMIT),
        name="out_ffn",
    )(x2d, y_ret, y_nsa, mod, g2.reshape(1, d), gf.reshape(1, d), w_out, wg, wu, wd)


def kernel(x, c, ln_mix_g, ln_ffn_g, w_ada, b_ada, w_in, cmp_pe_k, cmp_w1_k, cmp_w2_k,
           cmp_pe_v, cmp_w1_v, cmp_w2_v, w_out, w_ff_gate, w_ff_up, w_ff_down, ln_final_g):
    assert w_in.shape[0] == 1, "the final RMSNorm is fused into the (single) layer's FFN kernel"
    b, s, d = x.shape
    lane = np.arange(LANES)
    tabs = _rope_tables(np.arange(s), np.ones(LANES, bool))
    n_piece = s // CMP_STRIDE
    tabs_cmp = _rope_tables(np.arange(n_piece) * CMP_STRIDE + CMP_LEN - 1, lane < NSA_DH)

    mod = _ada(c, w_ada[0], b_ada[0]).reshape(b, 6, d)
    rq, rk, rv, rg_act, nq, kc, vc, ksx, vst, kwx, vwt, gate_t = _in_proj(
        x, ln_mix_g[0], mod, w_in, tabs, _block_onehot_table(s))
    y_ret = _retention(rq, rk, rv, rg_act, _retention_tables())
    w1, w2, pe = _compress_weights(cmp_w1_k[0], cmp_w2_k[0], cmp_w1_v[0], cmp_w2_v[0],
                                   cmp_pe_k[0], cmp_pe_v[0])
    kvcmp, vct = _compress(kc, vc, w1, w2, pe, tabs_cmp)
    y_nsa, (wo_b, wg_b, wu_b, wd_b) = _nsa_attention(
        nq, kvcmp, vct, ksx, vst, kwx, vwt, gate_t, _overlap_t(s), (w_out, w_ff_gate, w_ff_up, w_ff_down))
    out = _out_ffn(x.reshape(b * s, d), y_ret.reshape(b * s, -1), y_nsa.reshape(b * s, -1), mod,
                   wo_b, ln_ffn_g[0], ln_final_g, wg_b, wu_b, wd_b, s)
    return out.reshape(b, s, d)
```

```python
import numpy as np
import jax
import jax.numpy as jnp
from jax import lax
from jax.experimental import pallas as pl
from jax.experimental.pallas import tpu as pltpu

F32 = jnp.float32
BF16 = jnp.bfloat16

D_MODEL = 1024
RET_HEADS = 4
RET_DK = 64
RET_DV = 128
RET_CHUNK = 128
NSA_HEADS = 8
NSA_GROUPS = 2
NSA_HPG = NSA_HEADS // NSA_GROUPS
NSA_DH = 64
CMP_LEN = 32
CMP_STRIDE = 16
CMP_HIDDEN = 128
SLC_LEN = 64
SLC_TOPK = 16
WIN = 512
D_FF = ((8 * D_MODEL + 3 * 256 - 1) // (3 * 256)) * 256
ROPE_THETA = 10000.0
EPS = 1e-6
NEG = -1e30
FORCE = 1e6

LANES = 128
SUBLANES = 8
HALF = NSA_DH // 2
ONES_ROWS = 16
GATE_ROWS = 3 * NSA_HEADS

TK_ADA = 256
TM_IN = 1024
PROJ_W = 512
RET_GROUP = 16
Q_SCALE = NSA_DH ** -0.5 * float(np.log2(np.e))
TQ = 128
TILE_GROUP = 2
TM_FF = 512
SUB_IN = 2
SUB_FF = 2
TF = 256
VMEM_LIMIT = 56 * 1024 * 1024

_R_RQ, _R_RK, _R_RV, _R_RG, _R_NQ = 0, 256, 512, 1024, 1536
_R_KC, _R_VC, _R_KS, _R_VS, _R_KW, _R_VW, _R_GATE = 2048, 2176, 2304, 2432, 2560, 2688, 2816
IN_COLS_K = -(-(_R_GATE + 3 * NSA_HEADS) // PROJ_W) * PROJ_W


def _sigmoid(x):
    return 1.0 / (1.0 + jnp.exp(-x))


def _nt(a, b):
    return lax.dot_general(a, b, (((1,), (1,)), ((), ())), preferred_element_type=F32)


def _nn(a, b):
    return jnp.dot(a, b, preferred_element_type=F32)


def _rope_tile(a, c, s1, s2):
    return a * c + pltpu.roll(a, HALF, 1) * s1 + pltpu.roll(a, LANES - HALF, 1) * s2


def _swap_halves(a):
    return pltpu.roll(a, LANES // 2, 1)


def _rope_tables(pos, rotary_lanes):
    pos = np.asarray(pos, np.float64)
    lane = np.arange(LANES)
    within = lane % NSA_DH
    freq = ROPE_THETA ** (-(within % HALF).astype(np.float64) / HALF)
    ang = pos[:, None] * freq[None, :]
    cos, sin = np.cos(ang), np.sin(ang)
    first = (within < HALF)[None, :]
    rot = np.asarray(rotary_lanes, bool)[None, :]
    c = np.where(rot, cos, 1.0)
    s1 = np.where(rot & ~first, sin, 0.0)
    s2 = np.where(rot & first, -sin, 0.0)
    return (jnp.asarray(c, F32), jnp.asarray(s1, F32), jnp.asarray(s2, F32))


def _block_onehot_table(seq):
    t = np.zeros((seq, LANES), np.float32)
    pos = np.arange(seq)
    t[pos, NSA_DH + pos // SLC_LEN] = 1.0
    return jnp.asarray(t)


def _retention_tables():
    h = np.arange(RET_HEADS, dtype=np.float64)
    log_g = np.log(1.0 - 2.0 ** (-5.0 - h))
    c = RET_CHUNK
    idx = np.arange(c, dtype=np.float64)
    diff = idx[:, None] - idx[None, :]
    causal = diff >= 0
    decay = np.where(causal, np.exp(log_g[:, None, None] * np.where(causal, diff, 0.0)), 0.0)
    zeta = np.exp(log_g[:, None] * (c - 1.0 - idx))
    xi = np.exp(log_g[:, None] * (idx + 1.0))
    g_chunk = np.exp(log_g * c)

    def pair_lanes(t):
        t = t.reshape(RET_HEADS // 2, 2, c)
        return np.repeat(np.transpose(t, (0, 2, 1)), RET_DK, axis=2)

    g_b = np.broadcast_to(g_chunk[:, None, None], (RET_HEADS, 1, LANES))
    return tuple(jnp.asarray(t, F32) for t in (decay, pair_lanes(zeta), pair_lanes(xi), g_b))


def _overlap_t(seq):
    n_c = seq // CMP_STRIDE - CMP_LEN // CMP_STRIDE + 1
    nb = seq // SLC_LEN
    cs = np.arange(n_c) * CMP_STRIDE
    bs = np.arange(nb) * SLC_LEN
    ov = np.maximum(np.minimum(cs[:, None] + CMP_LEN, bs[None] + SLC_LEN)
                    - np.maximum(cs[:, None], bs[None]), 0).astype(np.float64) / CMP_LEN
    ncp = seq // CMP_STRIDE
    ovp = np.zeros((ncp, nb))
    ovp[:n_c] = ov
    return jnp.asarray(ovp.T, BF16)


def _ada_kernel(c_ref, w_ref, b_ref, o_ref):
    c = c_ref[...]
    part = _nn(c * _sigmoid(c), w_ref[...])

    @pl.when(pl.program_id(0) == 0)
    def _():
        o_ref[...] = part + b_ref[...]

    @pl.when(pl.program_id(0) > 0)
    def _():
        o_ref[...] += part


def _ada(c, w, b):
    bsz, d = c.shape
    n = w.shape[1]
    tk = TK_ADA
    return pl.pallas_call(
        _ada_kernel,
        grid=(d // tk,),
        in_specs=[pl.BlockSpec((bsz, tk), lambda k: (0, k)),
                  pl.BlockSpec((tk, n), lambda k: (k, 0)),
                  pl.BlockSpec((1, n), lambda k: (0, 0))],
        out_specs=pl.BlockSpec((bsz, n), lambda k: (0, 0)),
        out_shape=jax.ShapeDtypeStruct((bsz, n), F32),
        compiler_params=pltpu.CompilerParams(
            dimension_semantics=("arbitrary",), vmem_limit_bytes=VMEM_LIMIT),
        name="ada",
    )(c, w, b.reshape(1, n))


def _inproj_kernel(x_ref, g_ref, mod_ref, w_ref, c_ref, s1_ref, s2_ref, hot_ref,
                   rq_ref, rk_ref, rv_ref, rg_ref, nq_ref, kc_ref, vc_ref,
                   ksx_ref, vst_ref, kwx_ref, vwt_ref, gate_ref, wb_ref):
    @pl.when((pl.program_id(0) == 0) & (pl.program_id(1) == 0))
    def _():
        n_in = w_ref.shape[2]
        whole = (n_in // PROJ_W) * PROJ_W
        for c0 in range(0, whole, PROJ_W):
            wb_ref[:, c0:c0 + PROJ_W] = w_ref[0, :, c0:c0 + PROJ_W].astype(BF16)
        wb_ref[:, whole:] = jnp.zeros((wb_ref.shape[0], wb_ref.shape[1] - whole), BF16)
        wb_ref[:, whole:n_in] = w_ref[0, :, whole:n_in].astype(BF16)

    def modulated_norm(rows):
        x = x_ref[0, rows, :]
        y = x * lax.rsqrt(jnp.mean(x * x, axis=-1, keepdims=True) + EPS) * g_ref[...]
        return (y * (1.0 + mod_ref[0, 1:2, :]) + mod_ref[0, 0:1, :]).astype(BF16)

    def product_steps(rows, hb):
        tabs = (c_ref[rows, :], s1_ref[rows, :], s2_ref[rows, :])
        low = lax.broadcasted_iota(jnp.int32, (rows.stop - rows.start, LANES), 1) < NSA_DH

        def proj(c0):
            a = _nn(hb, wb_ref[:, c0:c0 + PROJ_W])
            return [a[:, t * LANES:(t + 1) * LANES] for t in range(PROJ_W // LANES)]

        def put(out_ref, t, value):
            out_ref[0, rows, t * LANES:(t + 1) * LANES] = value.astype(out_ref.dtype)

        def roped(tiles, scale, out_ref):
            for t, a in enumerate(tiles):
                r = _rope_tile(a, *tabs)
                put(out_ref, t, r if scale == 1.0 else r * scale)

        def per_group(tile, fill, out_ref):
            put(out_ref, 0, jnp.where(low, tile, fill))
            put(out_ref, 1, jnp.where(low, _swap_halves(tile), fill))

        def per_group_t(tile, out_ref):
            t = tile.T
            ones = jnp.ones((ONES_ROWS, t.shape[1]), out_ref.dtype)
            for g in range(NSA_GROUPS):
                out_ref[0, g, 0:NSA_DH, rows] = t[g * NSA_DH:(g + 1) * NSA_DH].astype(out_ref.dtype)
                out_ref[0, g, NSA_DH:NSA_DH + ONES_ROWS, rows] = ones

        def retention_qk():
            tiles = proj(_R_RQ)
            roped(tiles[0:2], 1.0, rq_ref)
            roped(tiles[2:4], RET_DK ** -0.5, rk_ref)

        def retention_v():
            for t, a in enumerate(proj(_R_RV)):
                put(rv_ref, t, a)

        def retention_gate():
            for t, a in enumerate(proj(_R_RG)):
                put(rg_ref, t, a * _sigmoid(a))

        def nsa_q():
            roped(proj(_R_NQ), Q_SCALE, nq_ref)

        def nsa_compress_selected():
            kc_t, vc_t, ks_t, vs_t = proj(_R_KC)
            kc_ref[0, rows, :] = kc_t
            vc_ref[0, rows, :] = vc_t
            per_group(_rope_tile(ks_t, *tabs), hot_ref[rows, :], ksx_ref)
            per_group_t(vs_t, vst_ref)

        def nsa_window_gates():
            kw_t, vw_t, gates_t, _ = proj(_R_KW)
            per_group(_rope_tile(kw_t, *tabs), 0.0, kwx_ref)
            per_group_t(vw_t, vwt_ref)
            gate_ref[0, :, rows] = _sigmoid(gates_t.T[0:GATE_ROWS])

        return [retention_qk, retention_v, retention_gate, nsa_q, nsa_compress_selected, nsa_window_gates]

    n_rows = x_ref.shape[1] // SUB_IN
    slabs = [slice(p * n_rows, (p + 1) * n_rows) for p in range(SUB_IN)]
    hb = modulated_norm(slabs[0])
    for p, rows in enumerate(slabs):
        steps = product_steps(rows, hb)
        steps[0]()
        if p + 1 < SUB_IN:
            hb = modulated_norm(slabs[p + 1])
        for step in steps[1:]:
            step()


def _in_proj(x, ln_g, mod, w_in, tabs, hot):
    b, s, d = x.shape
    tm = TM_IN
    grid = (b, s // tm)
    tab_spec = pl.BlockSpec((tm, LANES), lambda bi, j: (j, 0))

    def out(n, dtype):
        return (jax.ShapeDtypeStruct((b, s, n), dtype), pl.BlockSpec((1, tm, n), lambda bi, j: (bi, j, 0)))

    def out_t(rows, dtype):
        return (jax.ShapeDtypeStruct((b, NSA_GROUPS, rows, s), dtype),
                pl.BlockSpec((1, NSA_GROUPS, rows, tm), lambda bi, j: (bi, 0, 0, j)))

    vt_rows = NSA_DH + ONES_ROWS
    outs = [out(256, BF16), out(256, BF16), out(512, BF16), out(512, BF16), out(512, BF16),
            out(LANES, F32), out(LANES, F32),
            out(256, BF16), out_t(vt_rows, BF16), out(256, BF16), out_t(vt_rows, BF16),
            (jax.ShapeDtypeStruct((b, GATE_ROWS, s), F32),
             pl.BlockSpec((1, GATE_ROWS, tm), lambda bi, j: (bi, 0, j)))]
    return pl.pallas_call(
        _inproj_kernel,
        grid=grid,
        in_specs=[pl.BlockSpec((1, tm, d), lambda bi, j: (bi, j, 0)),
                  pl.BlockSpec((1, d), lambda bi, j: (0, 0)),
                  pl.BlockSpec((1, 6, d), lambda bi, j: (bi, 0, 0)),
                  pl.BlockSpec((1,) + w_in.shape[1:], lambda bi, j: (0, 0, 0), pipeline_mode=pl.Buffered(1)),
                  tab_spec, tab_spec, tab_spec, tab_spec],
        out_specs=[o[1] for o in outs],
        out_shape=[o[0] for o in outs],
        scratch_shapes=[pltpu.VMEM((d, IN_COLS_K), BF16)],
        compiler_params=pltpu.CompilerParams(
            dimension_semantics=("arbitrary", "arbitrary"), vmem_limit_bytes=VMEM_LIMIT),
        name="in_proj",
    )(x, ln_g.reshape(1, d), mod, w_in, *tabs, hot)


def _ret_kernel(q_ref, k_ref, v_ref, rg_ref, dec_ref, zeta_ref, xi_ref, gch_ref, o_ref, kv_ref, prev_ref):
    c = RET_CHUNK
    n_chunks = q_ref.shape[1] // c
    low = lax.broadcasted_iota(jnp.int32, (c, LANES), 1) < RET_DK

    def chunk_rows(n):
        return pl.ds(pl.multiple_of(n * c, c), c)

    def head_cols(h):
        return slice(h * RET_DV, (h + 1) * RET_DV)

    def kv_body(it, carry):
        kz_t = {}
        for j in range(RET_GROUP):
            rows = chunk_rows(it * RET_GROUP + j)
            for p in range(RET_HEADS // 2):
                pair = slice(p * LANES, (p + 1) * LANES)
                kz_t[j, p] = (k_ref[0, rows, pair].astype(F32) * zeta_ref[p]).T.astype(BF16)
        for j in range(RET_GROUP):
            n = it * RET_GROUP + j
            for h in range(RET_HEADS):
                kv_ref[h, n] = _nn(kz_t[j, h // 2], v_ref[0, chunk_rows(n), head_cols(h)])
        return carry

    lax.fori_loop(0, n_chunks // RET_GROUP, kv_body, 0)

    for h in range(RET_HEADS):
        def scan_body(n, st, h=h):
            prev_ref[h, n] = st.astype(prev_ref.dtype)
            return st * gch_ref[h] + kv_ref[h, n]
        lax.fori_loop(0, n_chunks, scan_body, jnp.zeros((LANES, RET_DV), F32))

    def out_body(it, carry):
        chains = [(j, h) for j in range(RET_GROUP) for h in range(RET_HEADS)]
        chunk = lambda j: it * RET_GROUP + j
        q_own, qx_own, att, ys = {}, {}, {}, {}
        for j, h in chains:
            p, e = divmod(h, 2)
            pair = slice(p * LANES, (p + 1) * LANES)
            mine = low if e == 0 else jnp.logical_not(low)
            q2 = q_ref[0, chunk_rows(chunk(j)), pair].astype(F32)
            q_own[j, h] = jnp.where(mine, q2, 0.0).astype(BF16)
            qx_own[j, h] = jnp.where(mine, q2 * xi_ref[p], 0.0).astype(BF16)
        for j, h in chains:
            pair = slice((h // 2) * LANES, (h // 2 + 1) * LANES)
            att[j, h] = _nt(q_own[j, h], k_ref[0, chunk_rows(chunk(j)), pair])
        for j, h in chains:
            lhs = jnp.concatenate([(att[j, h] * dec_ref[h]).astype(BF16), qx_own[j, h]], axis=1)
            rhs = jnp.concatenate([v_ref[0, chunk_rows(chunk(j)), head_cols(h)], prev_ref[h, chunk(j)]], axis=0)
            ys[j, h] = _nn(lhs, rhs)
        for j, h in chains:
            y = ys[j, h]
            yn = y * lax.rsqrt(jnp.mean(y * y, axis=-1, keepdims=True) + EPS)
            gate = rg_ref[0, chunk_rows(chunk(j)), head_cols(h)].astype(F32)
            o_ref[0, chunk_rows(chunk(j)), head_cols(h)] = (yn * gate).astype(o_ref.dtype)
        return carry

    lax.fori_loop(0, n_chunks // RET_GROUP, out_body, 0)


def _retention(rq, rk, rv, rg_act, tables):
    b, s, _ = rq.shape
    decay, zeta_p, xi_p, g_b = tables
    whole = lambda a: pl.BlockSpec(a.shape, lambda bi: (0,) * a.ndim)
    row = lambda a: pl.BlockSpec((1,) + a.shape[1:], lambda bi: (bi, 0, 0))
    return pl.pallas_call(
        _ret_kernel,
        grid=(b,),
        in_specs=[row(rq), row(rk), row(rv), row(rg_act),
                  whole(decay), whole(zeta_p), whole(xi_p), whole(g_b)],
        out_specs=row(rv),
        out_shape=jax.ShapeDtypeStruct(rv.shape, BF16),
        scratch_shapes=[pltpu.VMEM((RET_HEADS, s // RET_CHUNK, LANES, RET_DV), F32),
                        pltpu.VMEM((RET_HEADS, s // RET_CHUNK, LANES, RET_DV), BF16)],
        compiler_params=pltpu.CompilerParams(
            dimension_semantics=("parallel",), vmem_limit_bytes=VMEM_LIMIT),
        name="retention",
    )(rq, rk, rv, rg_act, decay, zeta_p, xi_p, g_b)


def _cmp_kernel(kc_ref, vc_ref, w1_ref, pe_ref, w2_ref, c_ref, s1_ref, s2_ref, kv_ref, vt_ref):
    n_piece = kc_ref.shape[1] // CMP_STRIDE
    halves = CMP_LEN // CMP_STRIDE
    lhs = {}
    for t, src in enumerate((kc_ref, vc_ref)):
        rows = [src[0, pl.ds(r, n_piece, stride=CMP_STRIDE), :] for r in range(CMP_STRIDE)]
        for half in range(halves):
            pe0 = half * CMP_STRIDE
            lhs[t, half] = jnp.concatenate(
                [(rows[r] + pe_ref[t, pe0 + r:pe0 + r + 1, :]).astype(BF16) for r in range(CMP_STRIDE)], axis=1)
    part = {key: _nn(lhs[key], w1_ref[key[0], key[1]]) for key in lhs}
    out = jnp.zeros((n_piece, kv_ref.shape[2]), F32)
    for t in range(2):
        hid = part[t, 0]
        for half in range(1, halves):
            hid = hid + pltpu.roll(part[t, half], n_piece - half, 0)
        out = out + _nn((hid * _sigmoid(hid)).astype(BF16), w2_ref[t])
    for t in range(out.shape[1] // LANES):
        sl = slice(t * LANES, (t + 1) * LANES)
        kv = _rope_tile(out[:, sl], c_ref[...], s1_ref[...], s2_ref[...])
        kv_ref[0, :, sl] = kv.astype(kv_ref.dtype)
        vt_ref[0, t] = kv.T[NSA_DH:2 * NSA_DH].astype(vt_ref.dtype)


def _compress_weights(w1_k, w2_k, w1_v, w2_v, pe_k, pe_v):
    dh, hid = NSA_DH, CMP_HIDDEN

    def first(w1):
        w = w1.reshape(CMP_LEN, dh, hid)
        z = jnp.zeros_like(w)
        return jnp.concatenate([jnp.concatenate([w, z], axis=2), jnp.concatenate([z, w], axis=2)], axis=1)

    def second(w2, off):
        z = jnp.zeros_like(w2)
        rows = []
        for g in range(NSA_GROUPS):
            c = [z, z, z, z]
            c[2 * g + off] = w2
            rows.append(jnp.concatenate(c, axis=1))
        return jnp.concatenate(rows, axis=0)

    w1 = jnp.stack([first(w1_k), first(w1_v)]).astype(BF16).reshape(
        2, CMP_LEN // CMP_STRIDE, CMP_STRIDE * NSA_GROUPS * dh, NSA_GROUPS * hid)
    w2 = jnp.stack([second(w2_k, 0), second(w2_v, 1)]).astype(BF16)
    pe = jnp.stack([jnp.concatenate([pe_k, pe_k], axis=1), jnp.concatenate([pe_v, pe_v], axis=1)])
    return w1, w2, pe


def _compress(kc, vc, w1, w2, pe, tabs_cmp):
    b, s, w = kc.shape
    n_piece = s // CMP_STRIDE
    const2 = lambda bi: (0, 0)
    src = pl.BlockSpec((1, s, w), lambda bi: (bi, 0, 0))
    dst = pl.BlockSpec((1, n_piece, 2 * w), lambda bi: (bi, 0, 0))
    shape = jax.ShapeDtypeStruct((b, n_piece, 2 * w), BF16)
    return pl.pallas_call(
        _cmp_kernel,
        grid=(b,),
        in_specs=[src, src,
                  pl.BlockSpec(w1.shape, lambda bi: (0, 0, 0, 0)),
                  pl.BlockSpec(pe.shape, lambda bi: (0, 0, 0)),
                  pl.BlockSpec(w2.shape, lambda bi: (0, 0, 0)),
                  pl.BlockSpec((n_piece, LANES), const2),
                  pl.BlockSpec((n_piece, LANES), const2),
                  pl.BlockSpec((n_piece, LANES), const2)],
        out_specs=[dst, pl.BlockSpec((1, NSA_GROUPS, NSA_DH, n_piece), lambda bi: (bi, 0, 0, 0))],
        out_shape=[shape, jax.ShapeDtypeStruct((b, NSA_GROUPS, NSA_DH, n_piece), BF16)],
        compiler_params=pltpu.CompilerParams(
            dimension_semantics=("parallel",), vmem_limit_bytes=VMEM_LIMIT),
        name="compress",
    )(kc, vc, w1, pe, w2, *tabs_cmp)


def _ret_cmp_kernel(*refs):
    ret_in, cmp_in = refs[0:8], refs[8:16]
    y_ref, cmp_out, scratch = refs[16], refs[17:19], refs[19:21]
    _cmp_kernel(*cmp_in, *cmp_out)
    _ret_kernel(*ret_in, y_ref, *scratch)


def _retention_and_compress(rq, rk, rv, rg_act, tables, kc, vc, w1, w2, pe, tabs_cmp):
    b, s, w = kc.shape
    n_piece = s // CMP_STRIDE
    decay, zeta_p, xi_p, g_b = tables
    whole = lambda a: pl.BlockSpec(a.shape, lambda bi: (0,) * a.ndim)
    row = lambda a: pl.BlockSpec((1,) + a.shape[1:], lambda bi: (bi,) + (0,) * (a.ndim - 1))
    kv_shape = jax.ShapeDtypeStruct((b, n_piece, 2 * w), BF16)
    vt_shape = jax.ShapeDtypeStruct((b, NSA_GROUPS, NSA_DH, n_piece), BF16)
    y_ret, kvcmp, vct = pl.pallas_call(
        _ret_cmp_kernel,
        grid=(b,),
        in_specs=[row(rq), row(rk), row(rv), row(rg_act),
                  whole(decay), whole(zeta_p), whole(xi_p), whole(g_b),
                  row(kc), row(vc), whole(w1), whole(pe), whole(w2)] + [whole(t) for t in tabs_cmp],
        out_specs=[row(rv), row(kv_shape), row(vt_shape)],
        out_shape=[jax.ShapeDtypeStruct(rv.shape, BF16), kv_shape, vt_shape],
        scratch_shapes=[pltpu.VMEM((RET_HEADS, s // RET_CHUNK, LANES, RET_DV), F32),
                        pltpu.VMEM((RET_HEADS, s // RET_CHUNK, LANES, RET_DV), BF16)],
        compiler_params=pltpu.CompilerParams(
            dimension_semantics=("parallel",), vmem_limit_bytes=VMEM_LIMIT),
        name="retention_compress",
    )(rq, rk, rv, rg_act, decay, zeta_p, xi_p, g_b, kc, vc, w1, pe, w2, *tabs_cmp)
    return y_ret, kvcmp, vct


def _nsa_kernel(q_ref, kcmp_ref, vct_ref, ksx_ref, vst_ref, kwx_ref, vwt_ref, gate_ref, ovt_ref, *rest):
    n_w = (len(rest) - 1) // 2
    o_ref = rest[n_w]
    for w_src, w_dst in zip(rest[:n_w], rest[n_w + 1:]):
        w_dst[...] = w_src[...].astype(w_dst.dtype)

    tq = TQ
    seq = q_ref.shape[1]
    nb = seq // SLC_LEN
    n_cmp = kcmp_ref.shape[1]
    hpg, dh = NSA_HPG, NSA_DH
    assert WIN % tq == 0 and seq % tq == 0
    group = pl.program_id(1)

    low = lax.broadcasted_iota(jnp.int32, (tq, LANES), 1) < dh
    eye = jnp.where(lax.broadcasted_iota(jnp.int32, (tq, tq), 0)
                    == lax.broadcasted_iota(jnp.int32, (tq, tq), 1), 1.0, 0.0).astype(BF16)
    blk = lax.broadcasted_iota(jnp.int32, (nb, tq), 0)
    col = lax.broadcasted_iota(jnp.int32, (nb, tq), 1)
    crow = lax.broadcasted_iota(jnp.int32, (n_cmp, tq), 0)
    ccol = lax.broadcasted_iota(jnp.int32, (n_cmp, tq), 1)
    kcm = kcmp_ref[0]
    vct = vct_ref[0, 0]
    ovt = ovt_ref[...]

    def per_head(x):
        return jnp.concatenate([x] * hpg, axis=1)

    key_off = lax.broadcasted_iota(jnp.int32, (tq, tq), 0)
    qry_off = lax.broadcasted_iota(jnp.int32, (tq, tq), 1)
    not_after = per_head(jnp.where(key_off <= qry_off, 0.0, NEG))
    inside_win = per_head(jnp.where(key_off > qry_off, 0.0, NEG))

    def masked(s, first_key, t0, windowed):
        blocks = []
        for r in range(0, s.shape[0], tq):
            blk_s = s[r:r + tq]
            if first_key + r == t0:
                blk_s = blk_s + not_after
            elif windowed and first_key + r == t0 - WIN:
                blk_s = blk_s + inside_win
            blocks.append(blk_s)
        return jnp.concatenate(blocks, axis=0)

    def normalise(acc):
        return acc[0:dh] / acc[dh:dh + 1]

    def select_blocks(psum, t0):
        tcol = col + t0
        bcausal = blk * SLC_LEN <= tcol
        n_live = (t0 + tq - 1) // SLC_LEN + 1
        top_n = min(SLC_TOPK, nb)
        if n_live <= top_n:
            bias = jnp.where(bcausal, 0.0, NEG)
        else:
            p_hi = psum.astype(BF16)
            p_lo = (psum - p_hi.astype(F32)).astype(BF16)
            imp = _nn(ovt, p_hi) + _nn(ovt, p_lo)
            cur = tcol // SLC_LEN
            forced = (blk == 0) | (blk == cur) | (blk == cur - 1)
            imp = jnp.where(bcausal, jnp.where(forced, FORCE, imp), NEG)
            rank = jnp.zeros((nb, tq), F32)
            for j in range(n_live):
                r = imp[j:j + 1, :]
                rank = rank + jnp.where(blk > j, jnp.where(r >= imp, 1.0, 0.0), jnp.where(r > imp, 1.0, 0.0))
            bias = jnp.where((rank < float(top_n)) & bcausal, 0.0, NEG)
        feat = jnp.concatenate([jnp.zeros((dh, tq), F32), bias,
                                jnp.zeros((LANES - dh - nb, tq), F32)], axis=0).astype(BF16)
        return _nt(eye, feat)

    class TileGroup:
        def __init__(self, tiles):
            self.tiles = tiles
            self.wstart = {t0: max(t0 - WIN, 0) for t0 in tiles}
            self.wkeys = {t0: slice(self.wstart[t0], t0 + tq) for t0 in tiles}
            self.skeys = {t0: slice(0, t0 + tq) for t0 in tiles}

        def scores_window_compressed(self):
            self.heads, self.qs = {}, {}
            for t0 in self.tiles:
                qf = q_ref[0, t0:t0 + tq, :].astype(F32)
                hl = []
                for hh in range(hpg):
                    t = qf[:, (hh // 2) * LANES:(hh // 2 + 1) * LANES]
                    if hh % 2 == 1:
                        t = _swap_halves(t)
                    hl.append(jnp.where(low, t, 0.0))
                self.heads[t0] = hl
                self.qs[t0] = jnp.concatenate(hl, axis=0).astype(BF16)
            self.s_w = {t0: _nt(kwx_ref[0, self.wkeys[t0], :], self.qs[t0]) for t0 in self.tiles}
            self.s_c = {t0: _nt(kcm, self.qs[t0]) for t0 in self.tiles}

        def select_and_scores_selected(self):
            self.p_cmp, qsel = {}, {}
            for t0 in self.tiles:
                cmask = (crow * CMP_STRIDE + (CMP_LEN - 1)) <= (ccol + t0)
                p_all = []
                psum = jnp.zeros((n_cmp, tq), F32)
                for hh in range(hpg):
                    sh = jnp.where(cmask, self.s_c[t0][:, hh * tq:(hh + 1) * tq], NEG)
                    e = jnp.exp2(sh - jnp.max(sh, axis=0, keepdims=True))
                    p = jnp.where(cmask, e / jnp.sum(e, axis=0, keepdims=True), 0.0)
                    psum = psum + p
                    p_all.append(p.astype(BF16))
                self.p_cmp[t0] = jnp.concatenate(p_all, axis=1)
                qbias = select_blocks(psum, t0)
                qsel[t0] = jnp.concatenate([hd + qbias for hd in self.heads[t0]], axis=0).astype(BF16)
            self.s_s = {t0: _nt(ksx_ref[0, self.skeys[t0], :], qsel[t0]) for t0 in self.tiles}

        def outputs_window_compressed(self):
            e_w = {}
            for t0 in self.tiles:
                sw = masked(self.s_w[t0], self.wstart[t0], t0, True)
                e_w[t0] = jnp.exp2(sw - jnp.max(sw, axis=0, keepdims=True)).astype(BF16)
            self.o_win = {t0: normalise(_nn(vwt_ref[0, 0, :, self.wkeys[t0]], e_w[t0])) for t0 in self.tiles}
            self.o_cmp = {t0: _nn(vct, self.p_cmp[t0]) for t0 in self.tiles}

        def outputs_selected_and_store(self):
            e_s = {}
            for t0 in self.tiles:
                ss = masked(self.s_s[t0], 0, t0, False)
                e_s[t0] = jnp.exp2(ss - jnp.max(ss, axis=0, keepdims=True)).astype(BF16)
            o_sel = {t0: normalise(_nn(vst_ref[0, 0, :, self.skeys[t0]], e_s[t0])) for t0 in self.tiles}
            for t0 in self.tiles:
                gt = gate_ref[0, :, t0:t0 + tq]
                outs = []
                for hh in range(hpg):
                    hc = slice(hh * tq, (hh + 1) * tq)

                    def gate_row(branch):
                        by_group = [gt[branch * NSA_HEADS + g * hpg + hh:branch * NSA_HEADS + g * hpg + hh + 1, :]
                                    for g in range(NSA_GROUPS)]
                        row = by_group[-1]
                        for g in range(NSA_GROUPS - 2, -1, -1):
                            row = jnp.where(group == g, by_group[g], row)
                        return row

                    outs.append(gate_row(0) * self.o_cmp[t0][:, hc] + gate_row(1) * o_sel[t0][:, hc]
                                + gate_row(2) * self.o_win[t0][:, hc])
                o_ref[0, t0:t0 + tq, :] = jnp.concatenate(outs, axis=0).T.astype(o_ref.dtype)

    starts = list(range(0, seq, tq))
    order = [starts[i // 2] if i % 2 == 0 else starts[-1 - i // 2] for i in range(len(starts))]
    groups = [TileGroup(order[i:i + TILE_GROUP]) for i in range(0, len(order), TILE_GROUP)]
    n = len(groups)
    groups[0].scores_window_compressed()
    for i, g in enumerate(groups):
        if i + 1 < n:
            groups[i + 1].scores_window_compressed()
        g.select_and_scores_selected()
        g.outputs_window_compressed()
        if i > 0:
            groups[i - 1].outputs_selected_and_store()
    groups[n - 1].outputs_selected_and_store()


def _nsa_attention(nq, kvcmp, vct, ksx, vst, kwx, vwt, gate_t, ovt, weights):
    b, s, _ = nq.shape
    n_cmp = kvcmp.shape[1]
    gw = NSA_HPG * NSA_DH
    steps = b * NSA_GROUPS
    per_group = lambda rows, width: pl.BlockSpec((1, rows, width), lambda bi, g: (bi, 0, g))
    per_group_t = lambda a: pl.BlockSpec((1, 1) + a.shape[2:], lambda bi, g: (bi, g, 0, 0))
    assert all(w.shape[1] % (steps * 16) == 0 for w in weights)
    slab = lambda w: pl.BlockSpec((1, w.shape[1] // steps, w.shape[2]),
                                  lambda bi, g: (0, bi * NSA_GROUPS + g, 0))
    outs = pl.pallas_call(
        _nsa_kernel,
        grid=(b, NSA_GROUPS),
        in_specs=[per_group(s, gw),
                  per_group(n_cmp, LANES), per_group_t(vct),
                  per_group(s, LANES), per_group_t(vst), per_group(s, LANES), per_group_t(vwt),
                  pl.BlockSpec((1,) + gate_t.shape[1:], lambda bi, g: (bi, 0, 0)),
                  pl.BlockSpec(ovt.shape, lambda bi, g: (0, 0))] + [slab(w) for w in weights],
        out_specs=[per_group(s, gw)] + [slab(w) for w in weights],
        out_shape=[jax.ShapeDtypeStruct((b, s, NSA_HEADS * NSA_DH), BF16)]
                  + [jax.ShapeDtypeStruct(w.shape, BF16) for w in weights],
        compiler_params=pltpu.CompilerParams(
            dimension_semantics=("parallel", "parallel"), vmem_limit_bytes=VMEM_LIMIT),
        name="nsa_attn",
    )(nq, kvcmp, vct, ksx, vst, kwx, vwt, gate_t, ovt, *weights)
    return outs[0], [w[0] for w in outs[1:]]


def _ffn_kernel(x_ref, yr_ref, yn_ref, mod_ref, g2_ref, gf_ref, wo_ref, wg_ref, wu_ref, wd_ref,
                o_ref, x1_ref, act_ref):
    half_w = yr_ref.shape[1]
    d_ff = wg_ref.shape[1]
    n_chunks = d_ff // TF
    n_rows = x_ref.shape[0] // SUB_FF
    slabs = [slice(p * n_rows, (p + 1) * n_rows) for p in range(SUB_FF)]

    def mix(rows):
        return _nn(yr_ref[rows, :], wo_ref[0:half_w, :]) + _nn(yn_ref[rows, :], wo_ref[half_w:2 * half_w, :])

    def mid_norm(rows, mixed):
        x1 = x_ref[rows, :] + mod_ref[0, 2:3, :] * mixed
        x1_ref[rows, :] = x1
        y = x1 * lax.rsqrt(jnp.mean(x1 * x1, axis=-1, keepdims=True) + EPS) * g2_ref[...]
        return (y * (1.0 + mod_ref[0, 4:5, :]) + mod_ref[0, 3:4, :]).astype(BF16)

    def ff_chunk(rows, h2, j):
        sl = slice(j * TF, (j + 1) * TF)
        gate = _nn(h2, wg_ref[:, sl])
        up = _nn(h2, wu_ref[:, sl])
        act_ref[rows, sl] = (gate * _sigmoid(gate) * up).astype(BF16)

    def down(rows):
        return x1_ref[rows, :] + mod_ref[0, 5:6, :] * _nn(act_ref[rows, :], wd_ref[...])

    def final_norm(rows, xo):
        o_ref[rows, :] = xo * lax.rsqrt(jnp.mean(xo * xo, axis=-1, keepdims=True) + EPS) * gf_ref[...]

    mixed = [mix(rows) for rows in slabs]
    h2 = mid_norm(slabs[0], mixed[0])
    xo_prev = None
    for p, rows in enumerate(slabs):
        ff_chunk(rows, h2, 0)
        h2_next = mid_norm(slabs[p + 1], mixed[p + 1]) if p + 1 < SUB_FF else None
        if xo_prev is not None:
            final_norm(slabs[p - 1], xo_prev)
        for j in range(1, n_chunks):
            ff_chunk(rows, h2, j)
        xo_prev = down(rows)
        h2 = h2_next
    final_norm(slabs[-1], xo_prev)


def _out_ffn(x2d, y_ret, y_nsa, mod, w_out, g2, gf, wg, wu, wd, seq):
    n, d = x2d.shape
    tm = TM_FF
    d_ff = wg.shape[1]
    tiles_per_seq = seq // tm
    half_w = y_ret.shape[1]
    row = lambda i: (i, 0)
    resident = lambda a: pl.BlockSpec(a.shape, lambda i: (0, 0), pipeline_mode=pl.Buffered(1))
    return pl.pallas_call(
        _ffn_kernel,
        grid=(n // tm,),
        in_specs=[pl.BlockSpec((tm, d), row),
                  pl.BlockSpec((tm, half_w), row),
                  pl.BlockSpec((tm, half_w), row),
                  pl.BlockSpec((1, 6, d), lambda i: (i // tiles_per_seq, 0, 0)),
                  pl.BlockSpec((1, d), lambda i: (0, 0)),
                  pl.BlockSpec((1, d), lambda i: (0, 0)),
                  resident(w_out), resident(wg), resident(wu), resident(wd)],
        out_specs=pl.BlockSpec((tm, d), row),
        out_shape=jax.ShapeDtypeStruct((n, d), F32),
        scratch_shapes=[pltpu.VMEM((tm, d), F32), pltpu.VMEM((tm, d_ff), BF16)],
        compiler_params=pltpu.CompilerParams(
            dimension_semantics=("parallel",), vmem_limit_bytes=VMEM_LIMIT),
        name="out_ffn",
    )(x2d, y_ret, y_nsa, mod, g2.reshape(1, d), gf.reshape(1, d), w_out, wg, wu, wd)


def kernel(x, c, ln_mix_g, ln_ffn_g, w_ada, b_ada, w_in, cmp_pe_k, cmp_w1_k, cmp_w2_k,
           cmp_pe_v, cmp_w1_v, cmp_w2_v, w_out, w_ff_gate, w_ff_up, w_ff_down, ln_final_g):
    assert w_in.shape[0] == 1, "the final RMSNorm is fused into the (single) layer's FFN kernel"
    b, s, d = x.shape
    lane = np.arange(LANES)
    tabs = _rope_tables(np.arange(s), np.ones(LANES, bool))
    n_piece = s // CMP_STRIDE
    tabs_cmp = _rope_tables(np.arange(n_piece) * CMP_STRIDE + CMP_LEN - 1, lane < NSA_DH)

    mod = _ada(c, w_ada[0], b_ada[0]).reshape(b, 6, d)
    rq, rk, rv, rg_act, nq, kc, vc, ksx, vst, kwx, vwt, gate_t = _in_proj(
        x, ln_mix_g[0], mod, w_in, tabs, _block_onehot_table(s))
    w1, w2, pe = _compress_weights(cmp_w1_k[0], cmp_w2_k[0], cmp_w1_v[0], cmp_w2_v[0],
                                   cmp_pe_k[0], cmp_pe_v[0])
    y_ret, kvcmp, vct = _retention_and_compress(rq, rk, rv, rg_act, _retention_tables(),
                                                kc, vc, w1, w2, pe, tabs_cmp)
    y_nsa, (wo_b, wg_b, wu_b, wd_b) = _nsa_attention(
        nq, kvcmp, vct, ksx, vst, kwx, vwt, gate_t, _overlap_t(s), (w_out, w_ff_gate, w_ff_up, w_ff_down))
    out = _out_ffn(x.reshape(b * s, d), y_ret.reshape(b * s, -1), y_nsa.reshape(b * s, -1), mod,
                   wo_b, ln_ffn_g[0], ln_final_g, wg_b, wu_b, wd_b, s)
    return out.reshape(b, s, d)
```

```python
import numpy as np
import jax
import jax.numpy as jnp
from jax import lax
from jax.experimental import pallas as pl
from jax.experimental.pallas import tpu as pltpu

F32 = jnp.float32
BF16 = jnp.bfloat16

D_MODEL = 1024
RET_HEADS = 4
RET_DK = 64
RET_DV = 128
RET_CHUNK = 128
NSA_HEADS = 8
NSA_GROUPS = 2
NSA_HPG = NSA_HEADS // NSA_GROUPS
NSA_DH = 64
CMP_LEN = 32
CMP_STRIDE = 16
CMP_HIDDEN = 128
SLC_LEN = 64
SLC_TOPK = 16
WIN = 512
D_FF = ((8 * D_MODEL + 3 * 256 - 1) // (3 * 256)) * 256
ROPE_THETA = 10000.0
EPS = 1e-6
NEG = -1e30
FORCE = 1e6

LANES = 128
SUBLANES = 8
HALF = NSA_DH // 2
ONES_ROWS = 16
GATE_ROWS = 3 * NSA_HEADS

TK_ADA = 256
TM_IN = 1024
PROJ_W = 512
RET_GROUP = 16
Q_SCALE = NSA_DH ** -0.5 * float(np.log2(np.e))
TQ = 128
TILE_GROUP = 2
TM_FF = 512
SUB_IN = 2
SUB_FF = 2
TF = 256
VMEM_LIMIT = 56 * 1024 * 1024

_R_RQ, _R_RK, _R_RV, _R_RG, _R_NQ = 0, 256, 512, 1024, 1536
_R_KC, _R_VC, _R_KS, _R_VS, _R_KW, _R_VW, _R_GATE = 2048, 2176, 2304, 2432, 2560, 2688, 2816
IN_COLS_K = -(-(_R_GATE + 3 * NSA_HEADS) // PROJ_W) * PROJ_W


def _sigmoid(x):
    return 1.0 / (1.0 + jnp.exp(-x))


def _nt(a, b):
    return lax.dot_general(a, b, (((1,), (1,)), ((), ())), preferred_element_type=F32)


def _nn(a, b):
    return jnp.dot(a, b, preferred_element_type=F32)


def _rope_tile(a, c, s1, s2):
    return a * c + pltpu.roll(a, HALF, 1) * s1 + pltpu.roll(a, LANES - HALF, 1) * s2


def _swap_halves(a):
    return pltpu.roll(a, LANES // 2, 1)


def _rope_tables(pos, rotary_lanes):
    pos = np.asarray(pos, np.float64)
    lane = np.arange(LANES)
    within = lane % NSA_DH
    freq = ROPE_THETA ** (-(within % HALF).astype(np.float64) / HALF)
    ang = pos[:, None] * freq[None, :]
    cos, sin = np.cos(ang), np.sin(ang)
    first = (within < HALF)[None, :]
    rot = np.asarray(rotary_lanes, bool)[None, :]
    c = np.where(rot, cos, 1.0)
    s1 = np.where(rot & ~first, sin, 0.0)
    s2 = np.where(rot & first, -sin, 0.0)
    return (jnp.asarray(c, F32), jnp.asarray(s1, F32), jnp.asarray(s2, F32))


def _block_onehot_table(seq):
    t = np.zeros((seq, LANES), np.float32)
    pos = np.arange(seq)
    t[pos, NSA_DH + pos // SLC_LEN] = 1.0
    return jnp.asarray(t)


def _retention_tables():
    h = np.arange(RET_HEADS, dtype=np.float64)
    log_g = np.log(1.0 - 2.0 ** (-5.0 - h))
    c = RET_CHUNK
    idx = np.arange(c, dtype=np.float64)
    diff = idx[:, None] - idx[None, :]
    causal = diff >= 0
    decay = np.where(causal, np.exp(log_g[:, None, None] * np.where(causal, diff, 0.0)), 0.0)
    zeta = np.exp(log_g[:, None] * (c - 1.0 - idx))
    xi = np.exp(log_g[:, None] * (idx + 1.0))
    g_chunk = np.exp(log_g * c)

    def pair_lanes(t):
        t = t.reshape(RET_HEADS // 2, 2, c)
        return np.repeat(np.transpose(t, (0, 2, 1)), RET_DK, axis=2)

    g_b = np.broadcast_to(g_chunk[:, None, None], (RET_HEADS, 1, LANES))
    return tuple(jnp.asarray(t, F32) for t in (decay, pair_lanes(zeta), pair_lanes(xi), g_b))


def _overlap_t(seq):
    n_c = seq // CMP_STRIDE - CMP_LEN // CMP_STRIDE + 1
    nb = seq // SLC_LEN
    cs = np.arange(n_c) * CMP_STRIDE
    bs = np.arange(nb) * SLC_LEN
    ov = np.maximum(np.minimum(cs[:, None] + CMP_LEN, bs[None] + SLC_LEN)
                    - np.maximum(cs[:, None], bs[None]), 0).astype(np.float64) / CMP_LEN
    ncp = seq // CMP_STRIDE
    ovp = np.zeros((ncp, nb))
    ovp[:n_c] = ov
    return jnp.asarray(ovp.T, BF16)


def _ada_kernel(c_ref, w_ref, b_ref, o_ref):
    c = c_ref[...]
    part = _nn(c * _sigmoid(c), w_ref[...])

    @pl.when(pl.program_id(0) == 0)
    def _():
        o_ref[...] = part + b_ref[...]

    @pl.when(pl.program_id(0) > 0)
    def _():
        o_ref[...] += part


def _ada(c, w, b):
    bsz, d = c.shape
    n = w.shape[1]
    tk = TK_ADA
    return pl.pallas_call(
        _ada_kernel,
        grid=(d // tk,),
        in_specs=[pl.BlockSpec((bsz, tk), lambda k: (0, k)),
                  pl.BlockSpec((tk, n), lambda k: (k, 0)),
                  pl.BlockSpec((1, n), lambda k: (0, 0))],
        out_specs=pl.BlockSpec((bsz, n), lambda k: (0, 0)),
        out_shape=jax.ShapeDtypeStruct((bsz, n), F32),
        compiler_params=pltpu.CompilerParams(
            dimension_semantics=("arbitrary",), vmem_limit_bytes=VMEM_LIMIT),
        name="ada",
    )(c, w, b.reshape(1, n))


def _inproj_kernel(x_ref, g_ref, mod_ref, w_ref, c_ref, s1_ref, s2_ref, hot_ref,
                   rq_ref, rk_ref, rv_ref, rg_ref, nq_ref, kc_ref, vc_ref,
                   ksx_ref, vst_ref, kwx_ref, vwt_ref, gate_ref, wb_ref):
    @pl.when((pl.program_id(0) == 0) & (pl.program_id(1) == 0))
    def _():
        n_in = w_ref.shape[2]
        whole = (n_in // PROJ_W) * PROJ_W
        for c0 in range(0, whole, PROJ_W):
            wb_ref[:, c0:c0 + PROJ_W] = w_ref[0, :, c0:c0 + PROJ_W].astype(BF16)
        wb_ref[:, whole:] = jnp.zeros((wb_ref.shape[0], wb_ref.shape[1] - whole), BF16)
        wb_ref[:, whole:n_in] = w_ref[0, :, whole:n_in].astype(BF16)

    def modulated_norm(rows):
        x = x_ref[0, rows, :]
        y = x * lax.rsqrt(jnp.mean(x * x, axis=-1, keepdims=True) + EPS) * g_ref[...]
        return (y * (1.0 + mod_ref[0, 1:2, :]) + mod_ref[0, 0:1, :]).astype(BF16)

    def product_steps(rows, hb):
        tabs = (c_ref[rows, :], s1_ref[rows, :], s2_ref[rows, :])
        low = lax.broadcasted_iota(jnp.int32, (rows.stop - rows.start, LANES), 1) < NSA_DH

        def proj(c0):
            a = _nn(hb, wb_ref[:, c0:c0 + PROJ_W])
            return [a[:, t * LANES:(t + 1) * LANES] for t in range(PROJ_W // LANES)]

        def put(out_ref, t, value):
            out_ref[0, rows, t * LANES:(t + 1) * LANES] = value.astype(out_ref.dtype)

        def roped(tiles, scale, out_ref):
            for t, a in enumerate(tiles):
                r = _rope_tile(a, *tabs)
                put(out_ref, t, r if scale == 1.0 else r * scale)

        def per_group(tile, fill, out_ref):
            put(out_ref, 0, jnp.where(low, tile, fill))
            put(out_ref, 1, jnp.where(low, _swap_halves(tile), fill))

        def per_group_t(tile, out_ref):
            t = tile.T
            ones = jnp.ones((ONES_ROWS, t.shape[1]), out_ref.dtype)
            for g in range(NSA_GROUPS):
                out_ref[0, g, 0:NSA_DH, rows] = t[g * NSA_DH:(g + 1) * NSA_DH].astype(out_ref.dtype)
                out_ref[0, g, NSA_DH:NSA_DH + ONES_ROWS, rows] = ones

        def retention_qk():
            tiles = proj(_R_RQ)
            roped(tiles[0:2], 1.0, rq_ref)
            roped(tiles[2:4], RET_DK ** -0.5, rk_ref)

        def retention_v():
            for t, a in enumerate(proj(_R_RV)):
                put(rv_ref, t, a)

        def retention_gate():
            for t, a in enumerate(proj(_R_RG)):
                put(rg_ref, t, a * _sigmoid(a))

        def nsa_q():
            roped(proj(_R_NQ), Q_SCALE, nq_ref)

        def nsa_compress_selected():
            kc_t, vc_t, ks_t, vs_t = proj(_R_KC)
            kc_ref[0, rows, :] = kc_t
            vc_ref[0, rows, :] = vc_t
            per_group(_rope_tile(ks_t, *tabs), hot_ref[rows, :], ksx_ref)
            per_group_t(vs_t, vst_ref)

        def nsa_window_gates():
            kw_t, vw_t, gates_t, _ = proj(_R_KW)
            per_group(_rope_tile(kw_t, *tabs), 0.0, kwx_ref)
            per_group_t(vw_t, vwt_ref)
            gate_ref[0, :, rows] = _sigmoid(gates_t.T[0:GATE_ROWS])

        return [retention_qk, retention_v, retention_gate, nsa_q, nsa_compress_selected, nsa_window_gates]

    n_rows = x_ref.shape[1] // SUB_IN
    slabs = [slice(p * n_rows, (p + 1) * n_rows) for p in range(SUB_IN)]
    hb = modulated_norm(slabs[0])
    for p, rows in enumerate(slabs):
        steps = product_steps(rows, hb)
        steps[0]()
        if p + 1 < SUB_IN:
            hb = modulated_norm(slabs[p + 1])
        for step in steps[1:]:
            step()


def _in_proj(x, ln_g, mod, w_in, tabs, hot):
    b, s, d = x.shape
    tm = TM_IN
    grid = (b, s // tm)
    tab_spec = pl.BlockSpec((tm, LANES), lambda bi, j: (j, 0))

    def out(n, dtype):
        return (jax.ShapeDtypeStruct((b, s, n), dtype), pl.BlockSpec((1, tm, n), lambda bi, j: (bi, j, 0)))

    def out_t(rows, dtype):
        return (jax.ShapeDtypeStruct((b, NSA_GROUPS, rows, s), dtype),
                pl.BlockSpec((1, NSA_GROUPS, rows, tm), lambda bi, j: (bi, 0, 0, j)))

    vt_rows = NSA_DH + ONES_ROWS
    outs = [out(256, BF16), out(256, BF16), out(512, BF16), out(512, BF16), out(512, BF16),
            out(LANES, F32), out(LANES, F32),
            out(256, BF16), out_t(vt_rows, BF16), out(256, BF16), out_t(vt_rows, BF16),
            (jax.ShapeDtypeStruct((b, GATE_ROWS, s), F32),
             pl.BlockSpec((1, GATE_ROWS, tm), lambda bi, j: (bi, 0, j)))]
    return pl.pallas_call(
        _inproj_kernel,
        grid=grid,
        in_specs=[pl.BlockSpec((1, tm, d), lambda bi, j: (bi, j, 0)),
                  pl.BlockSpec((1, d), lambda bi, j: (0, 0)),
                  pl.BlockSpec((1, 6, d), lambda bi, j: (bi, 0, 0)),
                  pl.BlockSpec((1,) + w_in.shape[1:], lambda bi, j: (0, 0, 0), pipeline_mode=pl.Buffered(1)),
                  tab_spec, tab_spec, tab_spec, tab_spec],
        out_specs=[o[1] for o in outs],
        out_shape=[o[0] for o in outs],
        scratch_shapes=[pltpu.VMEM((d, IN_COLS_K), BF16)],
        compiler_params=pltpu.CompilerParams(
            dimension_semantics=("arbitrary", "arbitrary"), vmem_limit_bytes=VMEM_LIMIT),
        name="in_proj",
    )(x, ln_g.reshape(1, d), mod, w_in, *tabs, hot)


def _ret_kernel(q_ref, k_ref, v_ref, rg_ref, dec_ref, zeta_ref, xi_ref, gch_ref, o_ref, kv_ref, prev_ref):
    c = RET_CHUNK
    n_chunks = q_ref.shape[1] // c
    low = lax.broadcasted_iota(jnp.int32, (c, LANES), 1) < RET_DK

    def chunk_rows(n):
        return pl.ds(pl.multiple_of(n * c, c), c)

    def head_cols(h):
        return slice(h * RET_DV, (h + 1) * RET_DV)

    def kv_body(it, carry):
        kz_t = {}
        for j in range(RET_GROUP):
            rows = chunk_rows(it * RET_GROUP + j)
            for p in range(RET_HEADS // 2):
                pair = slice(p * LANES, (p + 1) * LANES)
                kz_t[j, p] = (k_ref[0, rows, pair].astype(F32) * zeta_ref[p]).T.astype(BF16)
        for j in range(RET_GROUP):
            n = it * RET_GROUP + j
            for h in range(RET_HEADS):
                kv_ref[h, n] = _nn(kz_t[j, h // 2], v_ref[0, chunk_rows(n), head_cols(h)])
        return carry

    lax.fori_loop(0, n_chunks // RET_GROUP, kv_body, 0)

    for h in range(RET_HEADS):
        def scan_body(n, st, h=h):
            prev_ref[h, n] = st.astype(prev_ref.dtype)
            return st * gch_ref[h] + kv_ref[h, n]
        lax.fori_loop(0, n_chunks, scan_body, jnp.zeros((LANES, RET_DV), F32))

    def out_body(it, carry):
        chains = [(j, h) for j in range(RET_GROUP) for h in range(RET_HEADS)]
        chunk = lambda j: it * RET_GROUP + j
        q_own, qx_own, att, ys = {}, {}, {}, {}
        for j, h in chains:
            p, e = divmod(h, 2)
            pair = slice(p * LANES, (p + 1) * LANES)
            mine = low if e == 0 else jnp.logical_not(low)
            q2 = q_ref[0, chunk_rows(chunk(j)), pair].astype(F32)
            q_own[j, h] = jnp.where(mine, q2, 0.0).astype(BF16)
            qx_own[j, h] = jnp.where(mine, q2 * xi_ref[p], 0.0).astype(BF16)
        for j, h in chains:
            pair = slice((h // 2) * LANES, (h // 2 + 1) * LANES)
            att[j, h] = _nt(q_own[j, h], k_ref[0, chunk_rows(chunk(j)), pair])
        for j, h in chains:
            lhs = jnp.concatenate([(att[j, h] * dec_ref[h]).astype(BF16), qx_own[j, h]], axis=1)
            rhs = jnp.concatenate([v_ref[0, chunk_rows(chunk(j)), head_cols(h)], prev_ref[h, chunk(j)]], axis=0)
            ys[j, h] = _nn(lhs, rhs)
        for j, h in chains:
            y = ys[j, h]
            yn = y * lax.rsqrt(jnp.mean(y * y, axis=-1, keepdims=True) + EPS)
            gate = rg_ref[0, chunk_rows(chunk(j)), head_cols(h)].astype(F32)
            o_ref[0, chunk_rows(chunk(j)), head_cols(h)] = (yn * gate).astype(o_ref.dtype)
        return carry

    lax.fori_loop(0, n_chunks // RET_GROUP, out_body, 0)


def _cmp_kernel(kc_ref, vc_ref, w1_ref, pe_ref, w2_ref, c_ref, s1_ref, s2_ref, kv_ref, vt_ref):
    n_piece = kc_ref.shape[1] // CMP_STRIDE
    halves = CMP_LEN // CMP_STRIDE
    lhs = {}
    for t, src in enumerate((kc_ref, vc_ref)):
        rows = [src[0, pl.ds(r, n_piece, stride=CMP_STRIDE), :] for r in range(CMP_STRIDE)]
        for half in range(halves):
            pe0 = half * CMP_STRIDE
            lhs[t, half] = jnp.concatenate(
                [(rows[r] + pe_ref[t, pe0 + r:pe0 + r + 1, :]).astype(BF16) for r in range(CMP_STRIDE)], axis=1)
    part = {key: _nn(lhs[key], w1_ref[key[0], key[1]]) for key in lhs}
    out = jnp.zeros((n_piece, kv_ref.shape[2]), F32)
    for t in range(2):
        hid = part[t, 0]
        for half in range(1, halves):
            hid = hid + pltpu.roll(part[t, half], n_piece - half, 0)
        out = out + _nn((hid * _sigmoid(hid)).astype(BF16), w2_ref[t])
    for t in range(out.shape[1] // LANES):
        sl = slice(t * LANES, (t + 1) * LANES)
        kv = _rope_tile(out[:, sl], c_ref[...], s1_ref[...], s2_ref[...])
        kv_ref[0, :, sl] = kv.astype(kv_ref.dtype)
        vt_ref[0, t] = kv.T[NSA_DH:2 * NSA_DH].astype(vt_ref.dtype)


def _compress_weights(w1_k, w2_k, w1_v, w2_v, pe_k, pe_v):
    dh, hid = NSA_DH, CMP_HIDDEN

    def first(w1):
        w = w1.reshape(CMP_LEN, dh, hid)
        z = jnp.zeros_like(w)
        return jnp.concatenate([jnp.concatenate([w, z], axis=2), jnp.concatenate([z, w], axis=2)], axis=1)

    def second(w2, off):
        z = jnp.zeros_like(w2)
        rows = []
        for g in range(NSA_GROUPS):
            c = [z, z, z, z]
            c[2 * g + off] = w2
            rows.append(jnp.concatenate(c, axis=1))
        return jnp.concatenate(rows, axis=0)

    w1 = jnp.stack([first(w1_k), first(w1_v)]).astype(BF16).reshape(
        2, CMP_LEN // CMP_STRIDE, CMP_STRIDE * NSA_GROUPS * dh, NSA_GROUPS * hid)
    w2 = jnp.stack([second(w2_k, 0), second(w2_v, 1)]).astype(BF16)
    pe = jnp.stack([jnp.concatenate([pe_k, pe_k], axis=1), jnp.concatenate([pe_v, pe_v], axis=1)])
    return w1, w2, pe


def _ret_cmp_kernel(*refs):
    ret_in, cmp_in = refs[0:8], refs[8:16]
    y_ref, cmp_out, scratch = refs[16], refs[17:19], refs[19:21]
    _cmp_kernel(*cmp_in, *cmp_out)
    _ret_kernel(*ret_in, y_ref, *scratch)


def _retention_and_compress(rq, rk, rv, rg_act, tables, kc, vc, w1, w2, pe, tabs_cmp):
    b, s, w = kc.shape
    n_piece = s // CMP_STRIDE
    decay, zeta_p, xi_p, g_b = tables
    whole = lambda a: pl.BlockSpec(a.shape, lambda bi: (0,) * a.ndim)
    row = lambda a: pl.BlockSpec((1,) + a.shape[1:], lambda bi: (bi,) + (0,) * (a.ndim - 1))
    kv_shape = jax.ShapeDtypeStruct((b, n_piece, 2 * w), BF16)
    vt_shape = jax.ShapeDtypeStruct((b, NSA_GROUPS, NSA_DH, n_piece), BF16)
    y_ret, kvcmp, vct = pl.pallas_call(
        _ret_cmp_kernel,
        grid=(b,),
        in_specs=[row(rq), row(rk), row(rv), row(rg_act),
                  whole(decay), whole(zeta_p), whole(xi_p), whole(g_b),
                  row(kc), row(vc), whole(w1), whole(pe), whole(w2)] + [whole(t) for t in tabs_cmp],
        out_specs=[row(rv), row(kv_shape), row(vt_shape)],
        out_shape=[jax.ShapeDtypeStruct(rv.shape, BF16), kv_shape, vt_shape],
        scratch_shapes=[pltpu.VMEM((RET_HEADS, s // RET_CHUNK, LANES, RET_DV), F32),
                        pltpu.VMEM((RET_HEADS, s // RET_CHUNK, LANES, RET_DV), BF16)],
        compiler_params=pltpu.CompilerParams(
            dimension_semantics=("parallel",), vmem_limit_bytes=VMEM_LIMIT),
        name="retention_compress",
    )(rq, rk, rv, rg_act, decay, zeta_p, xi_p, g_b, kc, vc, w1, pe, w2, *tabs_cmp)
    return y_ret, kvcmp, vct


def _nsa_kernel(q_ref, kcmp_ref, vct_ref, ksx_ref, vst_ref, kwx_ref, vwt_ref, gate_ref, ovt_ref, *rest):
    n_w = (len(rest) - 1) // 2
    o_ref = rest[n_w]
    for w_src, w_dst in zip(rest[:n_w], rest[n_w + 1:]):
        w_dst[...] = w_src[...].astype(w_dst.dtype)

    tq = TQ
    seq = q_ref.shape[1]
    nb = seq // SLC_LEN
    n_cmp = kcmp_ref.shape[1]
    hpg, dh = NSA_HPG, NSA_DH
    assert WIN % tq == 0 and seq % tq == 0
    group = pl.program_id(1)

    low = lax.broadcasted_iota(jnp.int32, (tq, LANES), 1) < dh
    eye = jnp.where(lax.broadcasted_iota(jnp.int32, (tq, tq), 0)
                    == lax.broadcasted_iota(jnp.int32, (tq, tq), 1), 1.0, 0.0).astype(BF16)
    blk = lax.broadcasted_iota(jnp.int32, (nb, tq), 0)
    col = lax.broadcasted_iota(jnp.int32, (nb, tq), 1)
    crow = lax.broadcasted_iota(jnp.int32, (n_cmp, tq), 0)
    ccol = lax.broadcasted_iota(jnp.int32, (n_cmp, tq), 1)
    kcm = kcmp_ref[0]
    vct = vct_ref[0, 0]
    ovt = ovt_ref[...]

    def per_head(x):
        return jnp.concatenate([x] * hpg, axis=1)

    key_off = lax.broadcasted_iota(jnp.int32, (tq, tq), 0)
    qry_off = lax.broadcasted_iota(jnp.int32, (tq, tq), 1)
    not_after = per_head(jnp.where(key_off <= qry_off, 0.0, NEG))
    inside_win = per_head(jnp.where(key_off > qry_off, 0.0, NEG))

    def masked(s, first_key, t0, windowed):
        blocks = []
        for r in range(0, s.shape[0], tq):
            blk_s = s[r:r + tq]
            if first_key + r == t0:
                blk_s = blk_s + not_after
            elif windowed and first_key + r == t0 - WIN:
                blk_s = blk_s + inside_win
            blocks.append(blk_s)
        return jnp.concatenate(blocks, axis=0)

    def normalise(acc):
        return acc[0:dh] / acc[dh:dh + 1]

    def select_blocks(psum, t0):
        tcol = col + t0
        bcausal = blk * SLC_LEN <= tcol
        n_live = (t0 + tq - 1) // SLC_LEN + 1
        top_n = min(SLC_TOPK, nb)
        if n_live <= top_n:
            bias = jnp.where(bcausal, 0.0, NEG)
        else:
            p_hi = psum.astype(BF16)
            p_lo = (psum - p_hi.astype(F32)).astype(BF16)
            imp = _nn(ovt, p_hi) + _nn(ovt, p_lo)
            cur = tcol // SLC_LEN
            forced = (blk == 0) | (blk == cur) | (blk == cur - 1)
            imp = jnp.where(bcausal, jnp.where(forced, FORCE, imp), NEG)
            rank = jnp.zeros((nb, tq), F32)
            for j in range(n_live):
                r = imp[j:j + 1, :]
                rank = rank + jnp.where(blk > j, jnp.where(r >= imp, 1.0, 0.0), jnp.where(r > imp, 1.0, 0.0))
            bias = jnp.where((rank < float(top_n)) & bcausal, 0.0, NEG)
        feat = jnp.concatenate([jnp.zeros((dh, tq), F32), bias,
                                jnp.zeros((LANES - dh - nb, tq), F32)], axis=0).astype(BF16)
        return _nt(eye, feat)

    class TileGroup:
        def __init__(self, tiles):
            self.tiles = tiles
            self.wstart = {t0: max(t0 - WIN, 0) for t0 in tiles}
            self.wkeys = {t0: slice(self.wstart[t0], t0 + tq) for t0 in tiles}
            self.skeys = {t0: slice(0, t0 + tq) for t0 in tiles}

        def scores_window_compressed(self):
            self.heads, self.qs = {}, {}
            for t0 in self.tiles:
                qf = q_ref[0, t0:t0 + tq, :].astype(F32)
                hl = []
                for hh in range(hpg):
                    t = qf[:, (hh // 2) * LANES:(hh // 2 + 1) * LANES]
                    if hh % 2 == 1:
                        t = _swap_halves(t)
                    hl.append(jnp.where(low, t, 0.0))
                self.heads[t0] = hl
                self.qs[t0] = jnp.concatenate(hl, axis=0).astype(BF16)
            self.s_w = {t0: _nt(kwx_ref[0, self.wkeys[t0], :], self.qs[t0]) for t0 in self.tiles}
            self.s_c = {t0: _nt(kcm, self.qs[t0]) for t0 in self.tiles}

        def select_and_scores_selected(self):
            self.p_cmp, qsel = {}, {}
            for t0 in self.tiles:
                cmask = (crow * CMP_STRIDE + (CMP_LEN - 1)) <= (ccol + t0)
                p_all = []
                psum = jnp.zeros((n_cmp, tq), F32)
                for hh in range(hpg):
                    sh = jnp.where(cmask, self.s_c[t0][:, hh * tq:(hh + 1) * tq], NEG)
                    e = jnp.exp2(sh - jnp.max(sh, axis=0, keepdims=True))
                    p = jnp.where(cmask, e / jnp.sum(e, axis=0, keepdims=True), 0.0)
                    psum = psum + p
                    p_all.append(p.astype(BF16))
                self.p_cmp[t0] = jnp.concatenate(p_all, axis=1)
                qbias = select_blocks(psum, t0)
                qsel[t0] = jnp.concatenate([hd + qbias for hd in self.heads[t0]], axis=0).astype(BF16)
            self.s_s = {t0: _nt(ksx_ref[0, self.skeys[t0], :], qsel[t0]) for t0 in self.tiles}

        def outputs_window_compressed(self):
            e_w = {}
            for t0 in self.tiles:
                sw = masked(self.s_w[t0], self.wstart[t0], t0, True)
                e_w[t0] = jnp.exp2(sw - jnp.max(sw, axis=0, keepdims=True)).astype(BF16)
            self.o_win = {t0: normalise(_nn(vwt_ref[0, 0, :, self.wkeys[t0]], e_w[t0])) for t0 in self.tiles}
            self.o_cmp = {t0: _nn(vct, self.p_cmp[t0]) for t0 in self.tiles}

        def outputs_selected_and_store(self):
            e_s = {}
            for t0 in self.tiles:
                ss = masked(self.s_s[t0], 0, t0, False)
                e_s[t0] = jnp.exp2(ss - jnp.max(ss, axis=0, keepdims=True)).astype(BF16)
            o_sel = {t0: normalise(_nn(vst_ref[0, 0, :, self.skeys[t0]], e_s[t0])) for t0 in self.tiles}
            for t0 in self.tiles:
                gt = gate_ref[0, :, t0:t0 + tq]
                outs = []
                for hh in range(hpg):
                    hc = slice(hh * tq, (hh + 1) * tq)

                    def gate_row(branch):
                        by_group = [gt[branch * NSA_HEADS + g * hpg + hh:branch * NSA_HEADS + g * hpg + hh + 1, :]
                                    for g in range(NSA_GROUPS)]
                        row = by_group[-1]
                        for g in range(NSA_GROUPS - 2, -1, -1):
                            row = jnp.where(group == g, by_group[g], row)
                        return row

                    outs.append(gate_row(0) * self.o_cmp[t0][:, hc] + gate_row(1) * o_sel[t0][:, hc]
                                + gate_row(2) * self.o_win[t0][:, hc])
                o_ref[0, t0:t0 + tq, :] = jnp.concatenate(outs, axis=0).T.astype(o_ref.dtype)

    starts = list(range(0, seq, tq))
    order = [starts[i // 2] if i % 2 == 0 else starts[-1 - i // 2] for i in range(len(starts))]
    groups = [TileGroup(order[i:i + TILE_GROUP]) for i in range(0, len(order), TILE_GROUP)]
    n = len(groups)
    groups[0].scores_window_compressed()
    for i, g in enumerate(groups):
        if i + 1 < n:
            groups[i + 1].scores_window_compressed()
        g.select_and_scores_selected()
        g.outputs_window_compressed()
        if i > 0:
            groups[i - 1].outputs_selected_and_store()
    groups[n - 1].outputs_selected_and_store()


def _nsa_attention(nq, kvcmp, vct, ksx, vst, kwx, vwt, gate_t, ovt, weights):
    b, s, _ = nq.shape
    n_cmp = kvcmp.shape[1]
    gw = NSA_HPG * NSA_DH
    steps = b * NSA_GROUPS
    per_group = lambda rows, width: pl.BlockSpec((1, rows, width), lambda bi, g: (bi, 0, g))
    per_group_t = lambda a: pl.BlockSpec((1, 1) + a.shape[2:], lambda bi, g: (bi, g, 0, 0))
    assert all(w.shape[1] % (steps * 16) == 0 for w in weights)
    slab = lambda w: pl.BlockSpec((1, w.shape[1] // steps, w.shape[2]),
                                  lambda bi, g: (0, bi * NSA_GROUPS + g, 0))
    outs = pl.pallas_call(
        _nsa_kernel,
        grid=(b, NSA_GROUPS),
        in_specs=[per_group(s, gw),
                  per_group(n_cmp, LANES), per_group_t(vct),
                  per_group(s, LANES), per_group_t(vst), per_group(s, LANES), per_group_t(vwt),
                  pl.BlockSpec((1,) + gate_t.shape[1:], lambda bi, g: (bi, 0, 0)),
                  pl.BlockSpec(ovt.shape, lambda bi, g: (0, 0))] + [slab(w) for w in weights],
        out_specs=[per_group(s, gw)] + [slab(w) for w in weights],
        out_shape=[jax.ShapeDtypeStruct((b, s, NSA_HEADS * NSA_DH), BF16)]
                  + [jax.ShapeDtypeStruct(w.shape, BF16) for w in weights],
        compiler_params=pltpu.CompilerParams(
            dimension_semantics=("parallel", "parallel"), vmem_limit_bytes=VMEM_LIMIT),
        name="nsa_attn",
    )(nq, kvcmp, vct, ksx, vst, kwx, vwt, gate_t, ovt, *weights)
    return outs[0], [w[0] for w in outs[1:]]


def _ffn_kernel(x_ref, yr_ref, yn_ref, mod_ref, g2_ref, gf_ref, wo_ref, wg_ref, wu_ref, wd_ref,
                o_ref, x1_ref, act_ref):
    half_w = yr_ref.shape[1]
    d_ff = wg_ref.shape[1]
    n_chunks = d_ff // TF
    n_rows = x_ref.shape[0] // SUB_FF
    slabs = [slice(p * n_rows, (p + 1) * n_rows) for p in range(SUB_FF)]

    def mix(rows):
        return _nn(yr_ref[rows, :], wo_ref[0:half_w, :]) + _nn(yn_ref[rows, :], wo_ref[half_w:2 * half_w, :])

    def mid_norm(rows, mixed):
        x1 = x_ref[rows, :] + mod_ref[0, 2:3, :] * mixed
        x1_ref[rows, :] = x1
        y = x1 * lax.rsqrt(jnp.mean(x1 * x1, axis=-1, keepdims=True) + EPS) * g2_ref[...]
        return (y * (1.0 + mod_ref[0, 4:5, :]) + mod_ref[0, 3:4, :]).astype(BF16)

    def ff_chunk(rows, h2, j):
        sl = slice(j * TF, (j + 1) * TF)
        gate = _nn(h2, wg_ref[:, sl])
        up = _nn(h2, wu_ref[:, sl])
        act_ref[rows, sl] = (gate * _sigmoid(gate) * up).astype(BF16)

    def down(rows):
        return x1_ref[rows, :] + mod_ref[0, 5:6, :] * _nn(act_ref[rows, :], wd_ref[...])

    def final_norm(rows, xo):
        o_ref[rows, :] = xo * lax.rsqrt(jnp.mean(xo * xo, axis=-1, keepdims=True) + EPS) * gf_ref[...]

    mixed = [mix(rows) for rows in slabs]
    h2 = mid_norm(slabs[0], mixed[0])
    xo_prev = None
    for p, rows in enumerate(slabs):
        ff_chunk(rows, h2, 0)
        h2_next = mid_norm(slabs[p + 1], mixed[p + 1]) if p + 1 < SUB_FF else None
        if xo_prev is not None:
            final_norm(slabs[p - 1], xo_prev)
        for j in range(1, n_chunks):
            ff_chunk(rows, h2, j)
        xo_prev = down(rows)
        h2 = h2_next
    final_norm(slabs[-1], xo_prev)


def _out_ffn(x2d, y_ret, y_nsa, mod, w_out, g2, gf, wg, wu, wd, seq):
    n, d = x2d.shape
    tm = TM_FF
    d_ff = wg.shape[1]
    tiles_per_seq = seq // tm
    half_w = y_ret.shape[1]
    row = lambda i: (i, 0)
    resident = lambda a: pl.BlockSpec(a.shape, lambda i: (0, 0), pipeline_mode=pl.Buffered(1))
    return pl.pallas_call(
        _ffn_kernel,
        grid=(n // tm,),
        in_specs=[pl.BlockSpec((tm, d), row),
                  pl.BlockSpec((tm, half_w), row),
                  pl.BlockSpec((tm, half_w), row),
                  pl.BlockSpec((1, 6, d), lambda i: (i // tiles_per_seq, 0, 0)),
                  pl.BlockSpec((1, d), lambda i: (0, 0)),
                  pl.BlockSpec((1, d), lambda i: (0, 0)),
                  resident(w_out), resident(wg), resident(wu), resident(wd)],
        out_specs=pl.BlockSpec((tm, d), row),
        out_shape=jax.ShapeDtypeStruct((n, d), F32),
        scratch_shapes=[pltpu.VMEM((tm, d), F32), pltpu.VMEM((tm, d_ff), BF16)],
        compiler_params=pltpu.CompilerParams(
            dimension_semantics=("parallel",), vmem_limit_bytes=VMEM_LIMIT),
        name="out_ffn",
    )(x2d, y_ret, y_nsa, mod, g2.reshape(1, d), gf.reshape(1, d), w_out, wg, wu, wd)


def kernel(x, c, ln_mix_g, ln_ffn_g, w_ada, b_ada, w_in, cmp_pe_k, cmp_w1_k, cmp_w2_k,
           cmp_pe_v, cmp_w1_v, cmp_w2_v, w_out, w_ff_gate, w_ff_up, w_ff_down, ln_final_g):
    assert w_in.shape[0] == 1, "the final RMSNorm is fused into the (single) layer's FFN kernel"
    b, s, d = x.shape
    lane = np.arange(LANES)
    tabs = _rope_tables(np.arange(s), np.ones(LANES, bool))
    n_piece = s // CMP_STRIDE
    tabs_cmp = _rope_tables(np.arange(n_piece) * CMP_STRIDE + CMP_LEN - 1, lane < NSA_DH)

    mod = _ada(c, w_ada[0], b_ada[0]).reshape(b, 6, d)
    rq, rk, rv, rg_act, nq, kc, vc, ksx, vst, kwx, vwt, gate_t = _in_proj(
        x, ln_mix_g[0], mod, w_in, tabs, _block_onehot_table(s))
    w1, w2, pe = _compress_weights(cmp_w1_k[0], cmp_w2_k[0], cmp_w1_v[0], cmp_w2_v[0],
                                   cmp_pe_k[0], cmp_pe_v[0])
    y_ret, kvcmp, vct = _retention_and_compress(rq, rk, rv, rg_act, _retention_tables(),
                                                kc, vc, w1, w2, pe, tabs_cmp)
    y_nsa, (wo_b, wg_b, wu_b, wd_b) = _nsa_attention(
        nq, kvcmp, vct, ksx, vst, kwx, vwt, gate_t, _overlap_t(s), (w_out, w_ff_gate, w_ff_up, w_ff_down))
    out = _out_ffn(x.reshape(b * s, d), y_ret.reshape(b * s, -1), y_nsa.reshape(b * s, -1), mod,
                   wo_b, ln_ffn_g[0], ln_final_g, wg_b, wu_b, wd_b, s)
    return out.reshape(b, s, d)
```

```python
import numpy as np
import jax
import jax.numpy as jnp
from jax import lax
from jax.experimental import pallas as pl
from jax.experimental.pallas import tpu as pltpu

F32 = jnp.float32
BF16 = jnp.bfloat16

D_MODEL = 1024
RET_HEADS = 4
RET_DK = 64
RET_DV = 128
RET_CHUNK = 128
NSA_HEADS = 8
NSA_GROUPS = 2
NSA_HPG = NSA_HEADS // NSA_GROUPS
NSA_DH = 64
CMP_LEN = 32
CMP_STRIDE = 16
CMP_HIDDEN = 128
SLC_LEN = 64
SLC_TOPK = 16
WIN = 512
D_FF = ((8 * D_MODEL + 3 * 256 - 1) // (3 * 256)) * 256
ROPE_THETA = 10000.0
EPS = 1e-6
NEG = -1e30
FORCE = 1e6

LANES = 128
SUBLANES = 8
HALF = NSA_DH // 2
ONES_ROWS = 16
GATE_ROWS = 3 * NSA_HEADS

TK_ADA = 256
TM_IN = 1024
PROJ_W = 512
RET_GROUP = 16
Q_SCALE = NSA_DH ** -0.5 * float(np.log2(np.e))
TQ = 128
TILE_GROUP = 2
TM_FF = 512
SUB_IN = 2
SUB_FF = 2
TF = 256
VMEM_LIMIT = 56 * 1024 * 1024

_R_RQ, _R_RK, _R_RV, _R_RG, _R_NQ = 0, 256, 512, 1024, 1536
_R_KC, _R_VC, _R_KS, _R_VS, _R_KW, _R_VW, _R_GATE = 2048, 2176, 2304, 2432, 2560, 2688, 2816
IN_COLS_K = -(-(_R_GATE + 3 * NSA_HEADS) // PROJ_W) * PROJ_W


def _sigmoid(x):
    return 1.0 / (1.0 + jnp.exp(-x))


def _nt(a, b):
    return lax.dot_general(a, b, (((1,), (1,)), ((), ())), preferred_element_type=F32)


def _nn(a, b):
    return jnp.dot(a, b, preferred_element_type=F32)


def _rope_tile(a, c, s1, s2):
    return a * c + pltpu.roll(a, HALF, 1) * s1 + pltpu.roll(a, LANES - HALF, 1) * s2


def _swap_halves(a):
    return pltpu.roll(a, LANES // 2, 1)


def _rope_tables(pos, rotary_lanes):
    pos = np.asarray(pos, np.float64)
    lane = np.arange(LANES)
    within = lane % NSA_DH
    freq = ROPE_THETA ** (-(within % HALF).astype(np.float64) / HALF)
    ang = pos[:, None] * freq[None, :]
    cos, sin = np.cos(ang), np.sin(ang)
    first = (within < HALF)[None, :]
    rot = np.asarray(rotary_lanes, bool)[None, :]
    c = np.where(rot, cos, 1.0)
    s1 = np.where(rot & ~first, sin, 0.0)
    s2 = np.where(rot & first, -sin, 0.0)
    return (jnp.asarray(c, F32), jnp.asarray(s1, F32), jnp.asarray(s2, F32))


def _block_onehot_table(seq):
    t = np.zeros((seq, LANES), np.float32)
    pos = np.arange(seq)
    t[pos, NSA_DH + pos // SLC_LEN] = 1.0
    return jnp.asarray(t)


def _retention_tables():
    h = np.arange(RET_HEADS, dtype=np.float64)
    log_g = np.log(1.0 - 2.0 ** (-5.0 - h))
    c = RET_CHUNK
    idx = np.arange(c, dtype=np.float64)
    diff = idx[:, None] - idx[None, :]
    causal = diff >= 0
    decay = np.where(causal, np.exp(log_g[:, None, None] * np.where(causal, diff, 0.0)), 0.0)
    zeta = np.exp(log_g[:, None] * (c - 1.0 - idx))
    xi = np.exp(log_g[:, None] * (idx + 1.0))
    g_chunk = np.exp(log_g * c)

    def pair_lanes(t):
        t = t.reshape(RET_HEADS // 2, 2, c)
        return np.repeat(np.transpose(t, (0, 2, 1)), RET_DK, axis=2)

    g_b = np.broadcast_to(g_chunk[:, None, None], (RET_HEADS, 1, LANES))
    return tuple(jnp.asarray(t, F32) for t in (decay, pair_lanes(zeta), pair_lanes(xi), g_b))


def _overlap_t(seq):
    n_c = seq // CMP_STRIDE - CMP_LEN // CMP_STRIDE + 1
    nb = seq // SLC_LEN
    cs = np.arange(n_c) * CMP_STRIDE
    bs = np.arange(nb) * SLC_LEN
    ov = np.maximum(np.minimum(cs[:, None] + CMP_LEN, bs[None] + SLC_LEN)
                    - np.maximum(cs[:, None], bs[None]), 0).astype(np.float64) / CMP_LEN
    ncp = seq // CMP_STRIDE
    ovp = np.zeros((ncp, nb))
    ovp[:n_c] = ov
    return jnp.asarray(ovp.T, BF16)


def _ada_kernel(c_ref, w_ref, b_ref, o_ref):
    c = c_ref[...]
    part = _nn(c * _sigmoid(c), w_ref[...])

    @pl.when(pl.program_id(0) == 0)
    def _():
        o_ref[...] = part + b_ref[...]

    @pl.when(pl.program_id(0) > 0)
    def _():
        o_ref[...] += part


def _ada(c, w, b):
    bsz, d = c.shape
    n = w.shape[1]
    tk = TK_ADA
    return pl.pallas_call(
        _ada_kernel,
        grid=(d // tk,),
        in_specs=[pl.BlockSpec((bsz, tk), lambda k: (0, k)),
                  pl.BlockSpec((tk, n), lambda k: (k, 0)),
                  pl.BlockSpec((1, n), lambda k: (0, 0))],
        out_specs=pl.BlockSpec((bsz, n), lambda k: (0, 0)),
        out_shape=jax.ShapeDtypeStruct((bsz, n), F32),
        compiler_params=pltpu.CompilerParams(
            dimension_semantics=("arbitrary",), vmem_limit_bytes=VMEM_LIMIT),
        name="ada",
    )(c, w, b.reshape(1, n))


def _inproj_kernel(x_ref, g_ref, mod_ref, w_ref, c_ref, s1_ref, s2_ref, hot_ref,
                   rq_ref, rk_ref, rv_ref, rg_ref, nq_ref, kc_ref, vc_ref,
                   ksx_ref, vst_ref, kwx_ref, vwt_ref, gate_ref, wb_ref):
    @pl.when((pl.program_id(0) == 0) & (pl.program_id(1) == 0))
    def _():
        n_in = w_ref.shape[2]
        whole = (n_in // PROJ_W) * PROJ_W
        for c0 in range(0, whole, PROJ_W):
            wb_ref[:, c0:c0 + PROJ_W] = w_ref[0, :, c0:c0 + PROJ_W].astype(BF16)
        wb_ref[:, whole:] = jnp.zeros((wb_ref.shape[0], wb_ref.shape[1] - whole), BF16)
        wb_ref[:, whole:n_in] = w_ref[0, :, whole:n_in].astype(BF16)

    def modulated_norm(rows):
        x = x_ref[0, rows, :]
        y = x * lax.rsqrt(jnp.mean(x * x, axis=-1, keepdims=True) + EPS) * g_ref[...]
        return (y * (1.0 + mod_ref[0, 1:2, :]) + mod_ref[0, 0:1, :]).astype(BF16)

    def product_steps(rows, hb):
        tabs = (c_ref[rows, :], s1_ref[rows, :], s2_ref[rows, :])
        low = lax.broadcasted_iota(jnp.int32, (rows.stop - rows.start, LANES), 1) < NSA_DH

        def proj(c0):
            a = _nn(hb, wb_ref[:, c0:c0 + PROJ_W])
            return [a[:, t * LANES:(t + 1) * LANES] for t in range(PROJ_W // LANES)]

        def put(out_ref, t, value):
            out_ref[0, rows, t * LANES:(t + 1) * LANES] = value.astype(out_ref.dtype)

        def roped(tiles, scale, out_ref):
            for t, a in enumerate(tiles):
                r = _rope_tile(a, *tabs)
                put(out_ref, t, r if scale == 1.0 else r * scale)

        def per_group(tile, fill, out_ref):
            put(out_ref, 0, jnp.where(low, tile, fill))
            put(out_ref, 1, jnp.where(low, _swap_halves(tile), fill))

        def per_group_t(tile, out_ref):
            t = tile.T
            ones = jnp.ones((ONES_ROWS, t.shape[1]), out_ref.dtype)
            for g in range(NSA_GROUPS):
                out_ref[0, g, 0:NSA_DH, rows] = t[g * NSA_DH:(g + 1) * NSA_DH].astype(out_ref.dtype)
                out_ref[0, g, NSA_DH:NSA_DH + ONES_ROWS, rows] = ones

        def retention_qk():
            tiles = proj(_R_RQ)
            roped(tiles[0:2], 1.0, rq_ref)
            roped(tiles[2:4], RET_DK ** -0.5, rk_ref)

        def retention_v():
            for t, a in enumerate(proj(_R_RV)):
                put(rv_ref, t, a)

        def retention_gate():
            for t, a in enumerate(proj(_R_RG)):
                put(rg_ref, t, a * _sigmoid(a))

        def nsa_q():
            roped(proj(_R_NQ), Q_SCALE, nq_ref)

        def nsa_compress_selected():
            kc_t, vc_t, ks_t, vs_t = proj(_R_KC)
            kc_ref[0, rows, :] = kc_t
            vc_ref[0, rows, :] = vc_t
            per_group(_rope_tile(ks_t, *tabs), hot_ref[rows, :], ksx_ref)
            per_group_t(vs_t, vst_ref)

        def nsa_window_gates():
            kw_t, vw_t, gates_t, _ = proj(_R_KW)
            per_group(_rope_tile(kw_t, *tabs), 0.0, kwx_ref)
            per_group_t(vw_t, vwt_ref)
            gate_ref[0, :, rows] = _sigmoid(gates_t.T[0:GATE_ROWS])

        return [retention_qk, retention_v, retention_gate, nsa_q, nsa_compress_selected, nsa_window_gates]

    n_rows = x_ref.shape[1] // SUB_IN
    slabs = [slice(p * n_rows, (p + 1) * n_rows) for p in range(SUB_IN)]
    hb = modulated_norm(slabs[0])
    for p, rows in enumerate(slabs):
        steps = product_steps(rows, hb)
        steps[0]()
        if p + 1 < SUB_IN:
            hb = modulated_norm(slabs[p + 1])
        for step in steps[1:]:
            step()


def _in_proj(x, ln_g, mod, w_in, tabs, hot):
    b, s, d = x.shape
    tm = TM_IN
    grid = (b, s // tm)
    tab_spec = pl.BlockSpec((tm, LANES), lambda bi, j: (j, 0))

    def out(n, dtype):
        return (jax.ShapeDtypeStruct((b, s, n), dtype), pl.BlockSpec((1, tm, n), lambda bi, j: (bi, j, 0)))

    def out_t(rows, dtype):
        return (jax.ShapeDtypeStruct((b, NSA_GROUPS, rows, s), dtype),
                pl.BlockSpec((1, NSA_GROUPS, rows, tm), lambda bi, j: (bi, 0, 0, j)))

    vt_rows = NSA_DH + ONES_ROWS
    outs = [out(256, BF16), out(256, BF16), out(512, BF16), out(512, BF16), out(512, BF16),
            out(LANES, F32), out(LANES, F32),
            out(256, BF16), out_t(vt_rows, BF16), out(256, BF16), out_t(vt_rows, BF16),
            (jax.ShapeDtypeStruct((b, GATE_ROWS, s), F32),
             pl.BlockSpec((1, GATE_ROWS, tm), lambda bi, j: (bi, 0, j)))]
    return pl.pallas_call(
        _inproj_kernel,
        grid=grid,
        in_specs=[pl.BlockSpec((1, tm, d), lambda bi, j: (bi, j, 0)),
                  pl.BlockSpec((1, d), lambda bi, j: (0, 0)),
                  pl.BlockSpec((1, 6, d), lambda bi, j: (bi, 0, 0)),
                  pl.BlockSpec((1,) + w_in.shape[1:], lambda bi, j: (0, 0, 0), pipeline_mode=pl.Buffered(1)),
                  tab_spec, tab_spec, tab_spec, tab_spec],
        out_specs=[o[1] for o in outs],
        out_shape=[o[0] for o in outs],
        scratch_shapes=[pltpu.VMEM((d, IN_COLS_K), BF16)],
        compiler_params=pltpu.CompilerParams(
            dimension_semantics=("arbitrary", "arbitrary"), vmem_limit_bytes=VMEM_LIMIT),
        name="in_proj",
    )(x, ln_g.reshape(1, d), mod, w_in, *tabs, hot)


def _ret_kernel(q_ref, k_ref, v_ref, rg_ref, dec_ref, zeta_ref, xi_ref, gch_ref, o_ref, kv_ref, prev_ref):
    c = RET_CHUNK
    n_chunks = q_ref.shape[1] // c
    low = lax.broadcasted_iota(jnp.int32, (c, LANES), 1) < RET_DK

    def chunk_rows(n):
        return pl.ds(pl.multiple_of(n * c, c), c)

    def head_cols(h):
        return slice(h * RET_DV, (h + 1) * RET_DV)

    def kv_body(it, carry):
        kz_t = {}
        for j in range(RET_GROUP):
            rows = chunk_rows(it * RET_GROUP + j)
            for p in range(RET_HEADS // 2):
                pair = slice(p * LANES, (p + 1) * LANES)
                kz_t[j, p] = (k_ref[0, rows, pair].astype(F32) * zeta_ref[p]).T.astype(BF16)
        for j in range(RET_GROUP):
            n = it * RET_GROUP + j
            for h in range(RET_HEADS):
                kv_ref[h, n] = _nn(kz_t[j, h // 2], v_ref[0, chunk_rows(n), head_cols(h)])
        return carry

    lax.fori_loop(0, n_chunks // RET_GROUP, kv_body, 0)

    for h in range(RET_HEADS):
        def scan_body(n, st, h=h):
            prev_ref[h, n] = st.astype(prev_ref.dtype)
            return st * gch_ref[h] + kv_ref[h, n]
        lax.fori_loop(0, n_chunks, scan_body, jnp.zeros((LANES, RET_DV), F32))

    def out_body(it, carry):
        chains = [(j, h) for j in range(RET_GROUP) for h in range(RET_HEADS)]
        chunk = lambda j: it * RET_GROUP + j
        q_own, qx_own, att, ys = {}, {}, {}, {}
        for j, h in chains:
            p, e = divmod(h, 2)
            pair = slice(p * LANES, (p + 1) * LANES)
            mine = low if e == 0 else jnp.logical_not(low)
            q2 = q_ref[0, chunk_rows(chunk(j)), pair].astype(F32)
            q_own[j, h] = jnp.where(mine, q2, 0.0).astype(BF16)
            qx_own[j, h] = jnp.where(mine, q2 * xi_ref[p], 0.0).astype(BF16)
        for j, h in chains:
            pair = slice((h // 2) * LANES, (h // 2 + 1) * LANES)
            att[j, h] = _nt(q_own[j, h], k_ref[0, chunk_rows(chunk(j)), pair])
        for j, h in chains:
            lhs = jnp.concatenate([(att[j, h] * dec_ref[h]).astype(BF16), qx_own[j, h]], axis=1)
            rhs = jnp.concatenate([v_ref[0, chunk_rows(chunk(j)), head_cols(h)], prev_ref[h, chunk(j)]], axis=0)
            ys[j, h] = _nn(lhs, rhs)
        for j, h in chains:
            y = ys[j, h]
            yn = y * lax.rsqrt(jnp.mean(y * y, axis=-1, keepdims=True) + EPS)
            gate = rg_ref[0, chunk_rows(chunk(j)), head_cols(h)].astype(F32)
            o_ref[0, chunk_rows(chunk(j)), head_cols(h)] = (yn * gate).astype(o_ref.dtype)
        return carry

    lax.fori_loop(0, n_chunks // RET_GROUP, out_body, 0)


def _cmp_kernel(kc_ref, vc_ref, w1_ref, pe_ref, w2_ref, c_ref, s1_ref, s2_ref, kv_ref, vt_ref):
    n_piece = kc_ref.shape[1] // CMP_STRIDE
    halves = CMP_LEN // CMP_STRIDE
    lhs = {}
    for t, src in enumerate((kc_ref, vc_ref)):
        rows = [src[0, pl.ds(r, n_piece, stride=CMP_STRIDE), :] for r in range(CMP_STRIDE)]
        for half in range(halves):
            pe0 = half * CMP_STRIDE
            lhs[t, half] = jnp.concatenate(
                [(rows[r] + pe_ref[t, pe0 + r:pe0 + r + 1, :]).astype(BF16) for r in range(CMP_STRIDE)], axis=1)
    part = {key: _nn(lhs[key], w1_ref[key[0], key[1]]) for key in lhs}
    out = jnp.zeros((n_piece, kv_ref.shape[2]), F32)
    for t in range(2):
        hid = part[t, 0]
        for half in range(1, halves):
            hid = hid + pltpu.roll(part[t, half], n_piece - half, 0)
        out = out + _nn((hid * _sigmoid(hid)).astype(BF16), w2_ref[t])
    for t in range(out.shape[1] // LANES):
        sl = slice(t * LANES, (t + 1) * LANES)
        kv = _rope_tile(out[:, sl], c_ref[...], s1_ref[...], s2_ref[...])
        kv_ref[0, :, sl] = kv.astype(kv_ref.dtype)
        vt_ref[0, t] = kv.T[NSA_DH:2 * NSA_DH].astype(vt_ref.dtype)


def _compress_weights(w1_k, w2_k, w1_v, w2_v, pe_k, pe_v):
    dh, hid = NSA_DH, CMP_HIDDEN

    def first(w1):
        w = w1.reshape(CMP_LEN, dh, hid)
        z = jnp.zeros_like(w)
        return jnp.concatenate([jnp.concatenate([w, z], axis=2), jnp.concatenate([z, w], axis=2)], axis=1)

    def second(w2, off):
        z = jnp.zeros_like(w2)
        rows = []
        for g in range(NSA_GROUPS):
            c = [z, z, z, z]
            c[2 * g + off] = w2
            rows.append(jnp.concatenate(c, axis=1))
        return jnp.concatenate(rows, axis=0)

    w1 = jnp.stack([first(w1_k), first(w1_v)]).astype(BF16).reshape(
        2, CMP_LEN // CMP_STRIDE, CMP_STRIDE * NSA_GROUPS * dh, NSA_GROUPS * hid)
    w2 = jnp.stack([second(w2_k, 0), second(w2_v, 1)]).astype(BF16)
    pe = jnp.stack([jnp.concatenate([pe_k, pe_k], axis=1), jnp.concatenate([pe_v, pe_v], axis=1)])
    return w1, w2, pe


def _ret_cmp_kernel(*refs):
    ret_in, cmp_in = refs[0:8], refs[8:16]
    y_ref, cmp_out, scratch = refs[16], refs[17:19], refs[19:21]
    _ret_kernel(*ret_in, y_ref, *scratch)
    _cmp_kernel(*cmp_in, *cmp_out)


def _retention_and_compress(rq, rk, rv, rg_act, tables, kc, vc, w1, w2, pe, tabs_cmp):
    b, s, w = kc.shape
    n_piece = s // CMP_STRIDE
    decay, zeta_p, xi_p, g_b = tables
    whole = lambda a: pl.BlockSpec(a.shape, lambda bi: (0,) * a.ndim)
    row = lambda a: pl.BlockSpec((1,) + a.shape[1:], lambda bi: (bi,) + (0,) * (a.ndim - 1))
    kv_shape = jax.ShapeDtypeStruct((b, n_piece, 2 * w), BF16)
    vt_shape = jax.ShapeDtypeStruct((b, NSA_GROUPS, NSA_DH, n_piece), BF16)
    y_ret, kvcmp, vct = pl.pallas_call(
        _ret_cmp_kernel,
        grid=(b,),
        in_specs=[row(rq), row(rk), row(rv), row(rg_act),
                  whole(decay), whole(zeta_p), whole(xi_p), whole(g_b),
                  row(kc), row(vc), whole(w1), whole(pe), whole(w2)] + [whole(t) for t in tabs_cmp],
        out_specs=[row(rv), row(kv_shape), row(vt_shape)],
        out_shape=[jax.ShapeDtypeStruct(rv.shape, BF16), kv_shape, vt_shape],
        scratch_shapes=[pltpu.VMEM((RET_HEADS, s // RET_CHUNK, LANES, RET_DV), F32),
                        pltpu.VMEM((RET_HEADS, s // RET_CHUNK, LANES, RET_DV), BF16)],
        compiler_params=pltpu.CompilerParams(
            dimension_semantics=("parallel",), vmem_limit_bytes=VMEM_LIMIT),
        name="retention_compress",
    )(rq, rk, rv, rg_act, decay, zeta_p, xi_p, g_b, kc, vc, w1, pe, w2, *tabs_cmp)
    return y_ret, kvcmp, vct


def _nsa_kernel(q_ref, kcmp_ref, vct_ref, ksx_ref, vst_ref, kwx_ref, vwt_ref, gate_ref, ovt_ref, *rest):
    n_w = (len(rest) - 1) // 2
    o_ref = rest[n_w]
    for w_src, w_dst in zip(rest[:n_w], rest[n_w + 1:]):
        w_dst[...] = w_src[...].astype(w_dst.dtype)

    tq = TQ
    seq = q_ref.shape[1]
    nb = seq // SLC_LEN
    n_cmp = kcmp_ref.shape[1]
    hpg, dh = NSA_HPG, NSA_DH
    assert WIN % tq == 0 and seq % tq == 0
    group = pl.program_id(1)

    low = lax.broadcasted_iota(jnp.int32, (tq, LANES), 1) < dh
    eye = jnp.where(lax.broadcasted_iota(jnp.int32, (tq, tq), 0)
                    == lax.broadcasted_iota(jnp.int32, (tq, tq), 1), 1.0, 0.0).astype(BF16)
    blk = lax.broadcasted_iota(jnp.int32, (nb, tq), 0)
    col = lax.broadcasted_iota(jnp.int32, (nb, tq), 1)
    crow = lax.broadcasted_iota(jnp.int32, (n_cmp, tq), 0)
    ccol = lax.broadcasted_iota(jnp.int32, (n_cmp, tq), 1)
    kcm = kcmp_ref[0]
    vct = vct_ref[0, 0]
    ovt = ovt_ref[...]

    def per_head(x):
        return jnp.concatenate([x] * hpg, axis=1)

    key_off = lax.broadcasted_iota(jnp.int32, (tq, tq), 0)
    qry_off = lax.broadcasted_iota(jnp.int32, (tq, tq), 1)
    not_after = per_head(jnp.where(key_off <= qry_off, 0.0, NEG))
    inside_win = per_head(jnp.where(key_off > qry_off, 0.0, NEG))

    def masked(s, first_key, t0, windowed):
        blocks = []
        for r in range(0, s.shape[0], tq):
            blk_s = s[r:r + tq]
            if first_key + r == t0:
                blk_s = blk_s + not_after
            elif windowed and first_key + r == t0 - WIN:
                blk_s = blk_s + inside_win
            blocks.append(blk_s)
        return jnp.concatenate(blocks, axis=0)

    def normalise(acc):
        return acc[0:dh] / acc[dh:dh + 1]

    def select_blocks(psum, t0):
        tcol = col + t0
        bcausal = blk * SLC_LEN <= tcol
        n_live = (t0 + tq - 1) // SLC_LEN + 1
        top_n = min(SLC_TOPK, nb)
        if n_live <= top_n:
            bias = jnp.where(bcausal, 0.0, NEG)
        else:
            p_hi = psum.astype(BF16)
            p_lo = (psum - p_hi.astype(F32)).astype(BF16)
            imp = _nn(ovt, p_hi) + _nn(ovt, p_lo)
            cur = tcol // SLC_LEN
            forced = (blk == 0) | (blk == cur) | (blk == cur - 1)
            imp = jnp.where(bcausal, jnp.where(forced, FORCE, imp), NEG)
            rank = jnp.zeros((nb, tq), F32)
            for j in range(n_live):
                r = imp[j:j + 1, :]
                rank = rank + jnp.where(blk > j, jnp.where(r >= imp, 1.0, 0.0), jnp.where(r > imp, 1.0, 0.0))
            bias = jnp.where((rank < float(top_n)) & bcausal, 0.0, NEG)
        feat = jnp.concatenate([jnp.zeros((dh, tq), F32), bias,
                                jnp.zeros((LANES - dh - nb, tq), F32)], axis=0).astype(BF16)
        return _nt(eye, feat)

    class TileGroup:
        def __init__(self, tiles):
            self.tiles = tiles
            self.wstart = {t0: max(t0 - WIN, 0) for t0 in tiles}
            self.wkeys = {t0: slice(self.wstart[t0], t0 + tq) for t0 in tiles}
            self.skeys = {t0: slice(0, t0 + tq) for t0 in tiles}

        def scores_window_compressed(self):
            self.heads, self.qs = {}, {}
            for t0 in self.tiles:
                qf = q_ref[0, t0:t0 + tq, :].astype(F32)
                hl = []
                for hh in range(hpg):
                    t = qf[:, (hh // 2) * LANES:(hh // 2 + 1) * LANES]
                    if hh % 2 == 1:
                        t = _swap_halves(t)
                    hl.append(jnp.where(low, t, 0.0))
                self.heads[t0] = hl
                self.qs[t0] = jnp.concatenate(hl, axis=0).astype(BF16)
            self.s_w = {t0: _nt(kwx_ref[0, self.wkeys[t0], :], self.qs[t0]) for t0 in self.tiles}
            self.s_c = {t0: _nt(kcm, self.qs[t0]) for t0 in self.tiles}

        def select_and_scores_selected(self):
            self.p_cmp, qsel = {}, {}
            for t0 in self.tiles:
                cmask = (crow * CMP_STRIDE + (CMP_LEN - 1)) <= (ccol + t0)
                p_all = []
                psum = jnp.zeros((n_cmp, tq), F32)
                for hh in range(hpg):
                    sh = jnp.where(cmask, self.s_c[t0][:, hh * tq:(hh + 1) * tq], NEG)
                    e = jnp.exp2(sh - jnp.max(sh, axis=0, keepdims=True))
                    p = jnp.where(cmask, e / jnp.sum(e, axis=0, keepdims=True), 0.0)
                    psum = psum + p
                    p_all.append(p.astype(BF16))
                self.p_cmp[t0] = jnp.concatenate(p_all, axis=1)
                qbias = select_blocks(psum, t0)
                qsel[t0] = jnp.concatenate([hd + qbias for hd in self.heads[t0]], axis=0).astype(BF16)
            self.s_s = {t0: _nt(ksx_ref[0, self.skeys[t0], :], qsel[t0]) for t0 in self.tiles}

        def outputs_window_compressed(self):
            e_w = {}
            for t0 in self.tiles:
                sw = masked(self.s_w[t0], self.wstart[t0], t0, True)
                e_w[t0] = jnp.exp2(sw - jnp.max(sw, axis=0, keepdims=True)).astype(BF16)
            self.o_win = {t0: normalise(_nn(vwt_ref[0, 0, :, self.wkeys[t0]], e_w[t0])) for t0 in self.tiles}
            self.o_cmp = {t0: _nn(vct, self.p_cmp[t0]) for t0 in self.tiles}

        def outputs_selected_and_store(self):
            e_s = {}
            for t0 in self.tiles:
                ss = masked(self.s_s[t0], 0, t0, False)
                e_s[t0] = jnp.exp2(ss - jnp.max(ss, axis=0, keepdims=True)).astype(BF16)
            o_sel = {t0: normalise(_nn(vst_ref[0, 0, :, self.skeys[t0]], e_s[t0])) for t0 in self.tiles}
            for t0 in self.tiles:
                gt = gate_ref[0, :, t0:t0 + tq]
                outs = []
                for hh in range(hpg):
                    hc = slice(hh * tq, (hh + 1) * tq)

                    def gate_row(branch):
                        by_group = [gt[branch * NSA_HEADS + g * hpg + hh:branch * NSA_HEADS + g * hpg + hh + 1, :]
                                    for g in range(NSA_GROUPS)]
                        row = by_group[-1]
                        for g in range(NSA_GROUPS - 2, -1, -1):
                            row = jnp.where(group == g, by_group[g], row)
                        return row

                    outs.append(gate_row(0) * self.o_cmp[t0][:, hc] + gate_row(1) * o_sel[t0][:, hc]
                                + gate_row(2) * self.o_win[t0][:, hc])
                o_ref[0, t0:t0 + tq, :] = jnp.concatenate(outs, axis=0).T.astype(o_ref.dtype)

    starts = list(range(0, seq, tq))
    order = [starts[i // 2] if i % 2 == 0 else starts[-1 - i // 2] for i in range(len(starts))]
    groups = [TileGroup(order[i:i + TILE_GROUP]) for i in range(0, len(order), TILE_GROUP)]
    n = len(groups)
    groups[0].scores_window_compressed()
    for i, g in enumerate(groups):
        if i + 1 < n:
            groups[i + 1].scores_window_compressed()
        g.select_and_scores_selected()
        g.outputs_window_compressed()
        if i > 0:
            groups[i - 1].outputs_selected_and_store()
    groups[n - 1].outputs_selected_and_store()


def _nsa_attention(nq, kvcmp, vct, ksx, vst, kwx, vwt, gate_t, ovt, weights):
    b, s, _ = nq.shape
    n_cmp = kvcmp.shape[1]
    gw = NSA_HPG * NSA_DH
    steps = b * NSA_GROUPS
    per_group = lambda rows, width: pl.BlockSpec((1, rows, width), lambda bi, g: (bi, 0, g))
    per_group_t = lambda a: pl.BlockSpec((1, 1) + a.shape[2:], lambda bi, g: (bi, g, 0, 0))
    assert all(w.shape[1] % (steps * 16) == 0 for w in weights)
    slab = lambda w: pl.BlockSpec((1, w.shape[1] // steps, w.shape[2]),
                                  lambda bi, g: (0, bi * NSA_GROUPS + g, 0))
    outs = pl.pallas_call(
        _nsa_kernel,
        grid=(b, NSA_GROUPS),
        in_specs=[per_group(s, gw),
                  per_group(n_cmp, LANES), per_group_t(vct),
                  per_group(s, LANES), per_group_t(vst), per_group(s, LANES), per_group_t(vwt),
                  pl.BlockSpec((1,) + gate_t.shape[1:], lambda bi, g: (bi, 0, 0)),
                  pl.BlockSpec(ovt.shape, lambda bi, g: (0, 0))] + [slab(w) for w in weights],
        out_specs=[per_group(s, gw)] + [slab(w) for w in weights],
        out_shape=[jax.ShapeDtypeStruct((b, s, NSA_HEADS * NSA_DH), BF16)]
                  + [jax.ShapeDtypeStruct(w.shape, BF16) for w in weights],
        compiler_params=pltpu.CompilerParams(
            dimension_semantics=("parallel", "parallel"), vmem_limit_bytes=VMEM_LIMIT),
        name="nsa_attn",
    )(nq, kvcmp, vct, ksx, vst, kwx, vwt, gate_t, ovt, *weights)
    return outs[0], [w[0] for w in outs[1:]]


def _ffn_kernel(x_ref, yr_ref, yn_ref, mod_ref, g2_ref, gf_ref, wo_ref, wg_ref, wu_ref, wd_ref,
                o_ref, x1_ref, act_ref):
    half_w = yr_ref.shape[1]
    d_ff = wg_ref.shape[1]
    n_chunks = d_ff // TF
    n_rows = x_ref.shape[0] // SUB_FF
    slabs = [slice(p * n_rows, (p + 1) * n_rows) for p in range(SUB_FF)]

    def mix(rows):
        return _nn(yr_ref[rows, :], wo_ref[0:half_w, :]) + _nn(yn_ref[rows, :], wo_ref[half_w:2 * half_w, :])

    def mid_norm(rows, mixed):
        x1 = x_ref[rows, :] + mod_ref[0, 2:3, :] * mixed
        x1_ref[rows, :] = x1
        y = x1 * lax.rsqrt(jnp.mean(x1 * x1, axis=-1, keepdims=True) + EPS) * g2_ref[...]
        return (y * (1.0 + mod_ref[0, 4:5, :]) + mod_ref[0, 3:4, :]).astype(BF16)

    def ff_chunk(rows, h2, j):
        sl = slice(j * TF, (j + 1) * TF)
        gate = _nn(h2, wg_ref[:, sl])
        up = _nn(h2, wu_ref[:, sl])
        act_ref[rows, sl] = (gate * _sigmoid(gate) * up).astype(BF16)

    def down(rows):
        return x1_ref[rows, :] + mod_ref[0, 5:6, :] * _nn(act_ref[rows, :], wd_ref[...])

    def final_norm(rows, xo):
        o_ref[rows, :] = xo * lax.rsqrt(jnp.mean(xo * xo, axis=-1, keepdims=True) + EPS) * gf_ref[...]

    mixed = [mix(rows) for rows in slabs]
    h2 = mid_norm(slabs[0], mixed[0])
    xo_prev = None
    for p, rows in enumerate(slabs):
        ff_chunk(rows, h2, 0)
        h2_next = mid_norm(slabs[p + 1], mixed[p + 1]) if p + 1 < SUB_FF else None
        if xo_prev is not None:
            final_norm(slabs[p - 1], xo_prev)
        for j in range(1, n_chunks):
            ff_chunk(rows, h2, j)
        xo_prev = down(rows)
        h2 = h2_next
    final_norm(slabs[-1], xo_prev)


def _out_ffn(x2d, y_ret, y_nsa, mod, w_out, g2, gf, wg, wu, wd, seq):
    n, d = x2d.shape
    tm = TM_FF
    d_ff = wg.shape[1]
    tiles_per_seq = seq // tm
    half_w = y_ret.shape[1]
    row = lambda i: (i, 0)
    resident = lambda a: pl.BlockSpec(a.shape, lambda i: (0, 0), pipeline_mode=pl.Buffered(1))
    return pl.pallas_call(
        _ffn_kernel,
        grid=(n // tm,),
        in_specs=[pl.BlockSpec((tm, d), row),
                  pl.BlockSpec((tm, half_w), row),
                  pl.BlockSpec((tm, half_w), row),
                  pl.BlockSpec((1, 6, d), lambda i: (i // tiles_per_seq, 0, 0)),
                  pl.BlockSpec((1, d), lambda i: (0, 0)),
                  pl.BlockSpec((1, d), lambda i: (0, 0)),
                  resident(w_out), resident(wg), resident(wu), resident(wd)],
        out_specs=pl.BlockSpec((tm, d), row),
        out_shape=jax.ShapeDtypeStruct((n, d), F32),
        scratch_shapes=[pltpu.VMEM((tm, d), F32), pltpu.VMEM((tm, d_ff), BF16)],
        compiler_params=pltpu.CompilerParams(
            dimension_semantics=("parallel",), vmem_limit_bytes=VMEM_LIMIT),
        name="out_ffn",
    )(x2d, y_ret, y_nsa, mod, g2.reshape(1, d), gf.reshape(1, d), w_out, wg, wu, wd)


def kernel(x, c, ln_mix_g, ln_ffn_g, w_ada, b_ada, w_in, cmp_pe_k, cmp_w1_k, cmp_w2_k,
           cmp_pe_v, cmp_w1_v, cmp_w2_v, w_out, w_ff_gate, w_ff_up, w_ff_down, ln_final_g):
    assert w_in.shape[0] == 1, "the final RMSNorm is fused into the (single) layer's FFN kernel"
    b, s, d = x.shape
    lane = np.arange(LANES)
    tabs = _rope_tables(np.arange(s), np.ones(LANES, bool))
    n_piece = s // CMP_STRIDE
    tabs_cmp = _rope_tables(np.arange(n_piece) * CMP_STRIDE + CMP_LEN - 1, lane < NSA_DH)

    mod = _ada(c, w_ada[0], b_ada[0]).reshape(b, 6, d)
    rq, rk, rv, rg_act, nq, kc, vc, ksx, vst, kwx, vwt, gate_t = _in_proj(
        x, ln_mix_g[0], mod, w_in, tabs, _block_onehot_table(s))
    w1, w2, pe = _compress_weights(cmp_w1_k[0], cmp_w2_k[0], cmp_w1_v[0], cmp_w2_v[0],
                                   cmp_pe_k[0], cmp_pe_v[0])
    y_ret, kvcmp, vct = _retention_and_compress(rq, rk, rv, rg_act, _retention_tables(),
                                                kc, vc, w1, w2, pe, tabs_cmp)
    y_nsa, (wo_b, wg_b, wu_b, wd_b) = _nsa_attention(
        nq, kvcmp, vct, ksx, vst, kwx, vwt, gate_t, _overlap_t(s), (w_out, w_ff_gate, w_ff_up, w_ff_down))
    out = _out_ffn(x.reshape(b * s, d), y_ret.reshape(b * s, -1), y_nsa.reshape(b * s, -1), mod,
                   wo_b, ln_ffn_g[0], ln_final_g, wg_b, wu_b, wd_b, s)
    return out.reshape(b, s, d)
```
